```python
import math
import jax, jax.numpy as jnp
from jax import lax
import numpy as np

D_MODEL = 2048
BATCH = 8
SEQ = 8192
DEPTH = 1

GDN_HEADS = 8
GDN_HEAD_DIM = 128
GDN_CONV = 4
GDN_CHUNK = 64
GDN_QK = GDN_HEADS * GDN_HEAD_DIM
GDN_V = GDN_HEADS * GDN_HEAD_DIM
GDN_CONV_CH = 2 * GDN_QK + GDN_V
MLA_HEADS = 8
QK_NOPE = 128
QK_ROPE = 64
V_HEAD = 128
Q_LORA = 512
KV_LORA = 512
ROPE_THETA = 10000.0
Q_BLOCK = 128
MAX_POS_OFFSET = 1024
MIX_WIDTH = GDN_V + MLA_HEADS * V_HEAD
D_FF = ((8 * D_MODEL + 3 * 256 - 1) // (3 * 256)) * 256
EPS = 1e-6
IN_WIDTH = GDN_CONV_CH + GDN_V + 2 * GDN_HEADS + Q_LORA + KV_LORA + QK_ROPE
IN_SPLITS = (
    GDN_CONV_CH,
    GDN_CONV_CH + GDN_V,
    GDN_CONV_CH + GDN_V + GDN_HEADS,
    GDN_CONV_CH + GDN_V + 2 * GDN_HEADS,
    GDN_CONV_CH + GDN_V + 2 * GDN_HEADS + Q_LORA,
    GDN_CONV_CH + GDN_V + 2 * GDN_HEADS + Q_LORA + KV_LORA,
)

kernel_name = "hymba_gdn_mla_swiglu_layer"


def rms_norm(x, w):
    xf = x.astype(jnp.float32)
    y = xf * lax.rsqrt(jnp.mean(xf * xf, axis=-1, keepdims=True) + EPS)
    return (y * w.astype(jnp.float32)).astype(x.dtype)


def l2_normalize(x):
    return x * lax.rsqrt(jnp.sum(x * x, axis=-1, keepdims=True) + EPS)


def rotary(x, positions):
    half = x.shape[-1] // 2
    inv_freq = ROPE_THETA ** (-jnp.arange(half, dtype=jnp.float32) / half)
    ang = positions.astype(jnp.float32)[:, :, None, None] * inv_freq
    cos, sin = jnp.cos(ang), jnp.sin(ang)
    xf = x.astype(jnp.float32)
    x1, x2 = xf[..., :half], xf[..., half:]
    out = jnp.concatenate([x1 * cos - x2 * sin, x2 * cos + x1 * sin], axis=-1)
    return out.astype(x.dtype)


def causal_short_conv(u, w):
    k, c = w.shape
    out = lax.conv_general_dilated(
        u, w[:, None, :].astype(u.dtype), window_strides=(1,), padding=[(k - 1, 0)],
        dimension_numbers=("NWC", "WIO", "NWC"), feature_group_count=c)
    return jax.nn.silu(out)


def gated_delta_rule_chunked(q, k, v, g, beta):
    b, s, h, dk = q.shape
    dv = v.shape[-1]
    c = GDN_CHUNK
    n = s // c
    f32 = jnp.float32
    q = l2_normalize(q.astype(f32)) * (dk ** -0.5)
    k = l2_normalize(k.astype(f32))
    v = v.astype(f32)

    def chunks(t):
        return t.reshape(b, n, c, h, t.shape[-1]).transpose(0, 3, 1, 2, 4)

    q, k, v = chunks(q), chunks(k), chunks(v)
    g = g.astype(f32).reshape(b, n, c, h).transpose(0, 3, 1, 2)
    beta = beta.astype(f32).reshape(b, n, c, h).transpose(0, 3, 1, 2)
    gc = jnp.cumsum(g, axis=-1)

    idx = jnp.arange(c)
    tril = idx[:, None] >= idx[None, :]
    strict = idx[:, None] > idx[None, :]
    decay = jnp.exp(jnp.where(tril, gc[..., :, None] - gc[..., None, :], -jnp.inf))

    kb = k * beta[..., None]
    vb = v * beta[..., None]
    a_mat = jnp.where(strict, jnp.einsum("bhncd,bhnjd->bhncj", kb, k) * decay, 0.0)
    lhs = a_mat + jnp.eye(c, dtype=f32)
    rhs = jnp.concatenate([vb, kb * jnp.exp(gc)[..., None]], axis=-1)
    sol = lax.linalg.triangular_solve(lhs, rhs, left_side=True, lower=True, unit_diagonal=True)
    u, w = sol[..., :dv], sol[..., dv:]
    intra = jnp.einsum("bhncd,bhnjd->bhncj", q, k) * decay

    def step(state, inp):
        q_c, k_c, u_c, w_c, gc_c, intra_c = inp
        v_new = u_c - jnp.einsum("bhcd,bhdv->bhcv", w_c, state)
        o = (jnp.einsum("bhcd,bhdv->bhcv", q_c * jnp.exp(gc_c)[..., None], state)
             + jnp.einsum("bhcj,bhjv->bhcv", intra_c, v_new))
        g_last = gc_c[..., -1]
        k_dec = k_c * jnp.exp(g_last[..., None] - gc_c)[..., None]
        state = state * jnp.exp(g_last)[..., None, None] + jnp.einsum("bhcd,bhcv->bhdv", k_dec, v_new)
        return state, o

    xs = tuple(jnp.moveaxis(t, 2, 0) for t in (q, k, u, w, gc, intra))
    state0 = jnp.zeros((b, h, dk, dv), f32)
    _, o = lax.scan(step, state0, xs)
    return o.transpose(1, 0, 3, 2, 4).reshape(b, s, h, dv)


def blocked_causal_mla_attention(q_nope, q_rope, k_nope, k_rope, v):
    b, s, h, dn = q_nope.shape
    nb = s // Q_BLOCK
    scale = (QK_NOPE + QK_ROPE) ** -0.5
    qn = q_nope.reshape(b, nb, Q_BLOCK, h, dn).swapaxes(0, 1)
    qr = q_rope.reshape(b, nb, Q_BLOCK, h, QK_ROPE).swapaxes(0, 1)
    key_pos = jnp.arange(s)

    def block(args):
        qn_b, qr_b, start = args
        sc = (jnp.einsum("bqhd,bkhd->bhqk", qn_b, k_nope)
              + jnp.einsum("bqhr,bkr->bhqk", qr_b, k_rope)).astype(jnp.float32) * scale
        q_pos = start + jnp.arange(Q_BLOCK)
        sc = jnp.where(key_pos[None, :] <= q_pos[:, None], sc, -jnp.inf)
        p = jax.nn.softmax(sc, axis=-1).astype(v.dtype)
        return jnp.einsum("bhqk,bkhd->bqhd", p, v)

    out = lax.map(block, (qn, qr, jnp.arange(nb) * Q_BLOCK))
    return out.swapaxes(0, 1).reshape(b, s, h, v.shape[-1])


def _fwd_setup_inputs(seed: int = 0) -> dict:
    key = jax.random.key(seed)
    ks = jax.random.split(key, 20)
    L = DEPTH
    f32 = jnp.float32

    def nrm(k, shape, fan_in):
        return jax.random.normal(k, shape, f32) * fan_in ** -0.5

    def gain(k, shape):
        return 1.0 + 0.02 * jax.random.normal(k, shape, f32)

    x = jax.random.normal(ks[0], (BATCH, SEQ, D_MODEL), f32)
    offsets = jax.random.randint(ks[1], (BATCH, 1), 0, MAX_POS_OFFSET, dtype=jnp.int32)
    positions = offsets + jnp.arange(SEQ, dtype=jnp.int32)[None, :]
    attn_norm_w = gain(ks[2], (L, D_MODEL))
    w_in = nrm(ks[3], (L, D_MODEL, IN_WIDTH), D_MODEL)
    conv_w = nrm(ks[4], (L, GDN_CONV, GDN_CONV_CH), GDN_CONV)
    a_log = jnp.log(jax.random.uniform(ks[5], (L, GDN_HEADS), f32, 1.0, 16.0))
    dt = jnp.exp(jax.random.uniform(ks[6], (L, GDN_HEADS), f32, math.log(1e-3), math.log(1e-1)))
    dt_bias = dt + jnp.log(-jnp.expm1(-dt))
    gdn_norm_w = gain(ks[7], (L, GDN_HEAD_DIM))
    q_norm_w = gain(ks[8], (L, Q_LORA))
    w_uq = nrm(ks[9], (L, Q_LORA, MLA_HEADS * (QK_NOPE + QK_ROPE)), Q_LORA)
    kv_norm_w = gain(ks[10], (L, KV_LORA))
    w_ukv = nrm(ks[11], (L, KV_LORA, MLA_HEADS * (QK_NOPE + V_HEAD)), KV_LORA)
    mla_out_norm_w = gain(ks[12], (L, V_HEAD))
    w_out = nrm(ks[13], (L, MIX_WIDTH, D_MODEL), MIX_WIDTH)
    ffn_norm_w = gain(ks[14], (L, D_MODEL))
    w_gate = nrm(ks[15], (L, D_MODEL, D_FF), D_MODEL)
    w_up = nrm(ks[16], (L, D_MODEL, D_FF), D_MODEL)
    w_down = nrm(ks[17], (L, D_FF, D_MODEL), D_FF)
    final_norm_w = gain(ks[18], (D_MODEL,))
    return {"x": x, "positions": positions, "attn_norm_w": attn_norm_w, "w_in": w_in,
            "conv_w": conv_w, "a_log": a_log, "dt_bias": dt_bias, "gdn_norm_w": gdn_norm_w,
            "q_norm_w": q_norm_w, "w_uq": w_uq, "kv_norm_w": kv_norm_w, "w_ukv": w_ukv,
            "mla_out_norm_w": mla_out_norm_w, "w_out": w_out, "ffn_norm_w": ffn_norm_w,
            "w_gate": w_gate, "w_up": w_up, "w_down": w_down, "final_norm_w": final_norm_w}


def _fwd_reference(x, positions, attn_norm_w, w_in, conv_w, a_log, dt_bias, gdn_norm_w,
              q_norm_w, w_uq, kv_norm_w, w_ukv, mla_out_norm_w, w_out, ffn_norm_w,
              w_gate, w_up, w_down, final_norm_w):
    b, s, _ = x.shape
    for l in range(DEPTH):
        h = rms_norm(x, attn_norm_w[l])
        proj = h @ w_in[l]
        qkv_pre, z, b_raw, a_raw, cq, ckv, kr = jnp.split(proj, IN_SPLITS, axis=-1)

        qkv = causal_short_conv(qkv_pre, conv_w[l])
        gq, gk, gv = jnp.split(qkv, (GDN_QK, 2 * GDN_QK), axis=-1)
        gq = gq.reshape(b, s, GDN_HEADS, GDN_HEAD_DIM)
        gk = gk.reshape(b, s, GDN_HEADS, GDN_HEAD_DIM)
        gv = gv.reshape(b, s, GDN_HEADS, GDN_HEAD_DIM)
        beta = jax.nn.sigmoid(b_raw.astype(jnp.float32))
        g = -jnp.exp(a_log[l].astype(jnp.float32)) * jax.nn.softplus(
            a_raw.astype(jnp.float32) + dt_bias[l].astype(jnp.float32))
        o_gdn = gated_delta_rule_chunked(gq, gk, gv, g, beta).astype(x.dtype)
        o_gdn = rms_norm(o_gdn, gdn_norm_w[l]) * jax.nn.silu(z.reshape(b, s, GDN_HEADS, GDN_HEAD_DIM))

        q = (rms_norm(cq, q_norm_w[l]) @ w_uq[l]).reshape(b, s, MLA_HEADS, QK_NOPE + QK_ROPE)
        q_nope, q_rope = q[..., :QK_NOPE], rotary(q[..., QK_NOPE:], positions)
        kv = (rms_norm(ckv, kv_norm_w[l]) @ w_ukv[l]).reshape(b, s, MLA_HEADS, QK_NOPE + V_HEAD)
        k_nope, v = kv[..., :QK_NOPE], kv[..., QK_NOPE:]
        k_rope = rotary(kr[:, :, None, :], positions)[:, :, 0, :]
        o_mla = blocked_causal_mla_attention(q_nope, q_rope, k_nope, k_rope, v)
        o_mla = rms_norm(o_mla, mla_out_norm_w[l])

        mixed = jnp.concatenate([o_gdn.reshape(b, s, GDN_V), o_mla.reshape(b, s, MLA_HEADS * V_HEAD)], axis=-1)
        x = x + mixed @ w_out[l]

        h = rms_norm(x, ffn_norm_w[l])
        x = x + (jax.nn.silu(h @ w_gate[l]) * (h @ w_up[l])) @ w_down[l]
    return rms_norm(x, final_norm_w)


import jax as _jax
import jax.numpy as _jnp

TWIN_FORMAT = 'train_step'
FWD_PARAMS = ['x', 'positions', 'attn_norm_w', 'w_in', 'conv_w', 'a_log', 'dt_bias', 'gdn_norm_w', 'q_norm_w', 'w_uq', 'kv_norm_w', 'w_ukv', 'mla_out_norm_w', 'w_out', 'ffn_norm_w', 'w_gate', 'w_up', 'w_down', 'final_norm_w']
TWIN_WEIGHTS = ['attn_norm_w', 'w_in', 'conv_w', 'a_log', 'dt_bias', 'gdn_norm_w', 'q_norm_w', 'w_uq', 'kv_norm_w', 'w_ukv', 'mla_out_norm_w', 'w_out', 'ffn_norm_w', 'w_gate', 'w_up', 'w_down', 'final_norm_w']
TWIN_DIFF_INPUT = 'x'
TWIN_INPUTS = ['x', 'positions', 'attn_norm_w', 'w_in', 'conv_w', 'a_log', 'dt_bias', 'gdn_norm_w', 'q_norm_w', 'w_uq', 'kv_norm_w', 'w_ukv', 'mla_out_norm_w', 'w_out', 'ffn_norm_w', 'w_gate', 'w_up', 'w_down', 'final_norm_w', 'loss_target', 'm_attn_norm_w', 'm_w_in', 'm_conv_w', 'm_a_log', 'm_dt_bias', 'm_gdn_norm_w', 'm_q_norm_w', 'm_w_uq', 'm_kv_norm_w', 'm_w_ukv', 'm_mla_out_norm_w', 'm_w_out', 'm_ffn_norm_w', 'm_w_gate', 'm_w_up', 'm_w_down', 'm_final_norm_w', 'v_attn_norm_w', 'v_w_in', 'v_conv_w', 'v_a_log', 'v_dt_bias', 'v_gdn_norm_w', 'v_q_norm_w', 'v_w_uq', 'v_kv_norm_w', 'v_w_ukv', 'v_mla_out_norm_w', 'v_w_out', 'v_ffn_norm_w', 'v_w_gate', 'v_w_up', 'v_w_down', 'v_final_norm_w']
TWIN_OUTPUTS = ['loss', 'grad_x', 'grad_attn_norm_w', 'grad_w_in', 'grad_conv_w', 'grad_a_log', 'grad_dt_bias', 'grad_gdn_norm_w', 'grad_q_norm_w', 'grad_w_uq', 'grad_kv_norm_w', 'grad_w_ukv', 'grad_mla_out_norm_w', 'grad_w_out', 'grad_ffn_norm_w', 'grad_w_gate', 'grad_w_up', 'grad_w_down', 'grad_final_norm_w', 'delta_attn_norm_w', 'delta_w_in', 'delta_conv_w', 'delta_a_log', 'delta_dt_bias', 'delta_gdn_norm_w', 'delta_q_norm_w', 'delta_w_uq', 'delta_kv_norm_w', 'delta_w_ukv', 'delta_mla_out_norm_w', 'delta_w_out', 'delta_ffn_norm_w', 'delta_w_gate', 'delta_w_up', 'delta_w_down', 'delta_final_norm_w', 'new_m_attn_norm_w', 'new_m_w_in', 'new_m_conv_w', 'new_m_a_log', 'new_m_dt_bias', 'new_m_gdn_norm_w', 'new_m_q_norm_w', 'new_m_w_uq', 'new_m_kv_norm_w', 'new_m_w_ukv', 'new_m_mla_out_norm_w', 'new_m_w_out', 'new_m_ffn_norm_w', 'new_m_w_gate', 'new_m_w_up', 'new_m_w_down', 'new_m_final_norm_w', 'new_v_attn_norm_w', 'new_v_w_in', 'new_v_conv_w', 'new_v_a_log', 'new_v_dt_bias', 'new_v_gdn_norm_w', 'new_v_q_norm_w', 'new_v_w_uq', 'new_v_kv_norm_w', 'new_v_w_ukv', 'new_v_mla_out_norm_w', 'new_v_w_out', 'new_v_ffn_norm_w', 'new_v_w_gate', 'new_v_w_up', 'new_v_w_down', 'new_v_final_norm_w']
TWIN_LEAF_KINDS = {'loss': 'loss', 'grad_x': 'grad_x', 'grad_attn_norm_w': 'grad_w', 'grad_w_in': 'grad_w', 'grad_conv_w': 'grad_w', 'grad_a_log': 'grad_w', 'grad_dt_bias': 'grad_w', 'grad_gdn_norm_w': 'grad_w', 'grad_q_norm_w': 'grad_w', 'grad_w_uq': 'grad_w', 'grad_kv_norm_w': 'grad_w', 'grad_w_ukv': 'grad_w', 'grad_mla_out_norm_w': 'grad_w', 'grad_w_out': 'grad_w', 'grad_ffn_norm_w': 'grad_w', 'grad_w_gate': 'grad_w', 'grad_w_up': 'grad_w', 'grad_w_down': 'grad_w', 'grad_final_norm_w': 'grad_w', 'delta_attn_norm_w': 'delta_w', 'delta_w_in': 'delta_w', 'delta_conv_w': 'delta_w', 'delta_a_log': 'delta_w', 'delta_dt_bias': 'delta_w', 'delta_gdn_norm_w': 'delta_w', 'delta_q_norm_w': 'delta_w', 'delta_w_uq': 'delta_w', 'delta_kv_norm_w': 'delta_w', 'delta_w_ukv': 'delta_w', 'delta_mla_out_norm_w': 'delta_w', 'delta_w_out': 'delta_w', 'delta_ffn_norm_w': 'delta_w', 'delta_w_gate': 'delta_w', 'delta_w_up': 'delta_w', 'delta_w_down': 'delta_w', 'delta_final_norm_w': 'delta_w', 'new_m_attn_norm_w': 'new_m', 'new_m_w_in': 'new_m', 'new_m_conv_w': 'new_m', 'new_m_a_log': 'new_m', 'new_m_dt_bias': 'new_m', 'new_m_gdn_norm_w': 'new_m', 'new_m_q_norm_w': 'new_m', 'new_m_w_uq': 'new_m', 'new_m_kv_norm_w': 'new_m', 'new_m_w_ukv': 'new_m', 'new_m_mla_out_norm_w': 'new_m', 'new_m_w_out': 'new_m', 'new_m_ffn_norm_w': 'new_m', 'new_m_w_gate': 'new_m', 'new_m_w_up': 'new_m', 'new_m_w_down': 'new_m', 'new_m_final_norm_w': 'new_m', 'new_v_attn_norm_w': 'new_v', 'new_v_w_in': 'new_v', 'new_v_conv_w': 'new_v', 'new_v_a_log': 'new_v', 'new_v_dt_bias': 'new_v', 'new_v_gdn_norm_w': 'new_v', 'new_v_q_norm_w': 'new_v', 'new_v_w_uq': 'new_v', 'new_v_kv_norm_w': 'new_v', 'new_v_w_ukv': 'new_v', 'new_v_mla_out_norm_w': 'new_v', 'new_v_w_out': 'new_v', 'new_v_ffn_norm_w': 'new_v', 'new_v_w_gate': 'new_v', 'new_v_w_up': 'new_v', 'new_v_w_down': 'new_v', 'new_v_final_norm_w': 'new_v'}


def _forward(args):
    return _fwd_reference(*[args[k] for k in FWD_PARAMS])


def _output_shape():
    def fwd():
        inp = _fwd_setup_inputs(0)
        return _fwd_reference(*[inp[k] for k in FWD_PARAMS])
    out = _jax.eval_shape(fwd)
    return out.shape, out.dtype

N_MICROBATCH = 1
ADAM_LR = 0.001
ADAM_B1 = 0.9
ADAM_B2 = 0.999
ADAM_EPS = 1e-08
ADAM_WD = 0.01
ADAM_STEP = 10
PER_EXAMPLE_BATCH_AXIS = {'x': 0, 'positions': 0, 'loss_target': 0}
SHARED_INPUTS = []
_WEIGHT_DTYPES = {'attn_norm_w': _jnp.float32, 'w_in': _jnp.float32, 'conv_w': _jnp.float32, 'a_log': _jnp.float32, 'dt_bias': _jnp.float32, 'gdn_norm_w': _jnp.float32, 'q_norm_w': _jnp.float32, 'w_uq': _jnp.float32, 'kv_norm_w': _jnp.float32, 'w_ukv': _jnp.float32, 'mla_out_norm_w': _jnp.float32, 'w_out': _jnp.float32, 'ffn_norm_w': _jnp.float32, 'w_gate': _jnp.float32, 'w_up': _jnp.float32, 'w_down': _jnp.float32, 'final_norm_w': _jnp.float32}
MOMENT_SCALE = {'attn_norm_w': 1.496904e-01, 'w_in': 8.883678e-02, 'conv_w': 4.726361e-02, 'a_log': 2.319908e-01, 'dt_bias': 2.255771e-01, 'gdn_norm_w': 1.695164e-01, 'q_norm_w': 1.661611e-01, 'w_uq': 9.036752e-02, 'kv_norm_w': 2.166679e-01, 'w_ukv': 1.040458e-01, 'mla_out_norm_w': 3.141653e-01, 'w_out': 8.748866e-02, 'ffn_norm_w': 8.066898e-02, 'w_gate': 3.285687e-02, 'w_up': 3.174287e-02, 'w_down': 5.267111e-02, 'final_norm_w': 3.197732e+01}


def _to_microbatches(a, axis):
    t = _jnp.moveaxis(a, axis, 0)
    t = t.reshape((N_MICROBATCH, t.shape[0] // N_MICROBATCH) + t.shape[1:])
    return _jnp.moveaxis(t, 1, axis + 1)


def setup_inputs(seed: int = 0) -> dict:
    inp = _fwd_setup_inputs(seed)
    key = _jax.random.fold_in(_jax.random.key(seed), 7919)
    shape, _ = _output_shape()
    out = dict(inp)
    out["loss_target"] = _jax.random.normal(_jax.random.fold_in(key, 0), shape, _jnp.float32)
    for i, name in enumerate(TWIN_WEIGHTS):
        w = inp[name].astype(_jnp.float32)
        if MOMENT_SCALE is None:
            s = _jnp.sqrt(_jnp.mean(_jnp.square(w)) + 1e-30)
        else:
            s = MOMENT_SCALE[name]
        km, kv = _jax.random.split(_jax.random.fold_in(key, i + 1))
        out[name] = w
        out["m_" + name] = s * _jax.random.normal(km, w.shape, _jnp.float32)
        out["v_" + name] = (s * s) * _jax.random.uniform(kv, w.shape, _jnp.float32, 0.5, 1.5)
    if N_MICROBATCH > 1:
        for name, axis in PER_EXAMPLE_BATCH_AXIS.items():
            out[name] = _to_microbatches(out[name], axis)
    return {'x': out['x'], 'positions': out['positions'], 'attn_norm_w': out['attn_norm_w'], 'w_in': out['w_in'], 'conv_w': out['conv_w'], 'a_log': out['a_log'], 'dt_bias': out['dt_bias'], 'gdn_norm_w': out['gdn_norm_w'], 'q_norm_w': out['q_norm_w'], 'w_uq': out['w_uq'], 'kv_norm_w': out['kv_norm_w'], 'w_ukv': out['w_ukv'], 'mla_out_norm_w': out['mla_out_norm_w'], 'w_out': out['w_out'], 'ffn_norm_w': out['ffn_norm_w'], 'w_gate': out['w_gate'], 'w_up': out['w_up'], 'w_down': out['w_down'], 'final_norm_w': out['final_norm_w'], 'loss_target': out['loss_target'], 'm_attn_norm_w': out['m_attn_norm_w'], 'm_w_in': out['m_w_in'], 'm_conv_w': out['m_conv_w'], 'm_a_log': out['m_a_log'], 'm_dt_bias': out['m_dt_bias'], 'm_gdn_norm_w': out['m_gdn_norm_w'], 'm_q_norm_w': out['m_q_norm_w'], 'm_w_uq': out['m_w_uq'], 'm_kv_norm_w': out['m_kv_norm_w'], 'm_w_ukv': out['m_w_ukv'], 'm_mla_out_norm_w': out['m_mla_out_norm_w'], 'm_w_out': out['m_w_out'], 'm_ffn_norm_w': out['m_ffn_norm_w'], 'm_w_gate': out['m_w_gate'], 'm_w_up': out['m_w_up'], 'm_w_down': out['m_w_down'], 'm_final_norm_w': out['m_final_norm_w'], 'v_attn_norm_w': out['v_attn_norm_w'], 'v_w_in': out['v_w_in'], 'v_conv_w': out['v_conv_w'], 'v_a_log': out['v_a_log'], 'v_dt_bias': out['v_dt_bias'], 'v_gdn_norm_w': out['v_gdn_norm_w'], 'v_q_norm_w': out['v_q_norm_w'], 'v_w_uq': out['v_w_uq'], 'v_kv_norm_w': out['v_kv_norm_w'], 'v_w_ukv': out['v_w_ukv'], 'v_mla_out_norm_w': out['v_mla_out_norm_w'], 'v_w_out': out['v_w_out'], 'v_ffn_norm_w': out['v_ffn_norm_w'], 'v_w_gate': out['v_w_gate'], 'v_w_up': out['v_w_up'], 'v_w_down': out['v_w_down'], 'v_final_norm_w': out['v_final_norm_w']}


def _loss(weights, diff, rest, loss_target):
    with _jax.named_scope("forward"):
        args = {**rest, TWIN_DIFF_INPUT: diff, **{k: w.astype(_WEIGHT_DTYPES[k]) for k, w in weights.items()}}
        y = _forward(args)
    with _jax.named_scope("loss_head"):
        err = _jnp.square(y.astype(_jnp.float32) - loss_target)
        return 0.5 * _jnp.sum(_jnp.mean(err, axis=-1)) if err.ndim else 0.5 * err


def _adamw(w, g, m, v):
    m = ADAM_B1 * m + (1.0 - ADAM_B1) * g
    v = ADAM_B2 * v + (1.0 - ADAM_B2) * _jnp.square(g)
    m_hat = m / (1.0 - ADAM_B1 ** ADAM_STEP)
    v_hat = v / (1.0 - ADAM_B2 ** ADAM_STEP)
    delta = -ADAM_LR * (m_hat / (_jnp.sqrt(v_hat) + ADAM_EPS) + ADAM_WD * w)
    return delta, m, v


def reference(x, positions, attn_norm_w, w_in, conv_w, a_log, dt_bias, gdn_norm_w, q_norm_w, w_uq, kv_norm_w, w_ukv, mla_out_norm_w, w_out, ffn_norm_w, w_gate, w_up, w_down, final_norm_w, loss_target, m_attn_norm_w, m_w_in, m_conv_w, m_a_log, m_dt_bias, m_gdn_norm_w, m_q_norm_w, m_w_uq, m_kv_norm_w, m_w_ukv, m_mla_out_norm_w, m_w_out, m_ffn_norm_w, m_w_gate, m_w_up, m_w_down, m_final_norm_w, v_attn_norm_w, v_w_in, v_conv_w, v_a_log, v_dt_bias, v_gdn_norm_w, v_q_norm_w, v_w_uq, v_kv_norm_w, v_w_ukv, v_mla_out_norm_w, v_w_out, v_ffn_norm_w, v_w_gate, v_w_up, v_w_down, v_final_norm_w):
    given = dict(x=x, positions=positions, attn_norm_w=attn_norm_w, w_in=w_in, conv_w=conv_w, a_log=a_log, dt_bias=dt_bias, gdn_norm_w=gdn_norm_w, q_norm_w=q_norm_w, w_uq=w_uq, kv_norm_w=kv_norm_w, w_ukv=w_ukv, mla_out_norm_w=mla_out_norm_w, w_out=w_out, ffn_norm_w=ffn_norm_w, w_gate=w_gate, w_up=w_up, w_down=w_down, final_norm_w=final_norm_w, loss_target=loss_target, m_attn_norm_w=m_attn_norm_w, m_w_in=m_w_in, m_conv_w=m_conv_w, m_a_log=m_a_log, m_dt_bias=m_dt_bias, m_gdn_norm_w=m_gdn_norm_w, m_q_norm_w=m_q_norm_w, m_w_uq=m_w_uq, m_kv_norm_w=m_kv_norm_w, m_w_ukv=m_w_ukv, m_mla_out_norm_w=m_mla_out_norm_w, m_w_out=m_w_out, m_ffn_norm_w=m_ffn_norm_w, m_w_gate=m_w_gate, m_w_up=m_w_up, m_w_down=m_w_down, m_final_norm_w=m_final_norm_w, v_attn_norm_w=v_attn_norm_w, v_w_in=v_w_in, v_conv_w=v_conv_w, v_a_log=v_a_log, v_dt_bias=v_dt_bias, v_gdn_norm_w=v_gdn_norm_w, v_q_norm_w=v_q_norm_w, v_w_uq=v_w_uq, v_kv_norm_w=v_kv_norm_w, v_w_ukv=v_w_ukv, v_mla_out_norm_w=v_mla_out_norm_w, v_w_out=v_w_out, v_ffn_norm_w=v_ffn_norm_w, v_w_gate=v_w_gate, v_w_up=v_w_up, v_w_down=v_w_down, v_final_norm_w=v_final_norm_w)
    weights = {n: given[n] for n in TWIN_WEIGHTS}
    shared = {n: given[n] for n in SHARED_INPUTS}
    per_example = {n: given[n] for n in ['x', 'positions']}
    grad_fn = _jax.value_and_grad(_loss, argnums=(0, 1))

    def one_microbatch(ex, loss_target):
        ex = dict(ex)
        diff = ex.pop(TWIN_DIFF_INPUT)
        return grad_fn(weights, diff, {**shared, **ex}, loss_target)

    if N_MICROBATCH == 1:
        loss, (grad_w, grad_x) = one_microbatch(per_example, given["loss_target"])
    else:
        def body(carry, xs):
            loss_sum, grad_sum = carry
            l_k, (gw_k, gx_k) = one_microbatch(xs[0], xs[1])
            with _jax.named_scope("update"):
                return (loss_sum + l_k, _jax.tree.map(_jnp.add, grad_sum, gw_k)), gx_k

        init = (_jnp.zeros((), _jnp.float32), _jax.tree.map(_jnp.zeros_like, weights))
        (loss, grad_w), grad_x = _jax.lax.scan(body, init, (per_example, given["loss_target"]))
    with _jax.named_scope("update"):
        delta_w, new_m, new_v = {}, {}, {}
        for n in TWIN_WEIGHTS:
            delta_w[n], new_m[n], new_v[n] = _adamw(weights[n], grad_w[n], given["m_" + n], given["v_" + n])
    return (loss, grad_x, *[grad_w[n] for n in TWIN_WEIGHTS], *[delta_w[n] for n in TWIN_WEIGHTS],
            *[new_m[n] for n in TWIN_WEIGHTS], *[new_v[n] for n in TWIN_WEIGHTS])
```

```python
import functools
import math

import jax
import jax.numpy as jnp
from jax import lax
from jax.experimental import pallas as pl
from jax.experimental.pallas import tpu as pltpu

F32 = jnp.float32
BF16 = jnp.bfloat16
HI = lax.Precision.HIGHEST

D_MODEL = 2048
GDN_HEADS = 8
HEAD = 128
GDN_CONV = 4
GDN_CHUNK = 64
GDN_QK = GDN_HEADS * HEAD
CONV_CH = 3 * GDN_QK
MLA_HEADS = 8
QK_ROPE = 64
Q_LORA = 512
KV_LORA = 512
ROPE_THETA = 10000.0
D_FF = 5632
EPS = 1e-6
IN_WIDTH = 5200
ADAM_LR, ADAM_B1, ADAM_B2, ADAM_EPS, ADAM_WD, ADAM_STEP = 0.001, 0.9, 0.999, 1e-08, 0.01, 10

PROJ_W = 5376
COL_Z = 3072
COL_CQ = 4096
COL_CKV = 4608
COL_MISC = 5120
LANE_B = 64
LANE_A = 72
QHEAD = 256
FF_BLK = 512
N_DEV = 8
MESH = pl.DeviceIdType.MESH
VMEM_LIMIT_MB = 48

NN = ((1,), (0,))
NT = ((1,), (1,))
TN = ((0,), (0,))


def _pcall(body, *, name, grid, in_specs, out_specs, out_shape, scratch=()):
    return pl.pallas_call(
        body, name=name, grid=grid, in_specs=in_specs, out_specs=out_specs, out_shape=out_shape,
        scratch_shapes=list(scratch),
        compiler_params=pltpu.CompilerParams(dimension_semantics=("arbitrary",) * len(grid),
                                             vmem_limit_bytes=VMEM_LIMIT_MB << 20))


def _pick(dim, pref):
    if dim <= pref:
        return dim
    c = pref
    while c >= 128:
        if dim % c == 0 and c % 128 == 0:
            return c
        c -= 128
    return dim


def _rows(t, width, target_bytes=2 << 20):
    r = max(8, min(t, target_bytes // (4 * width)))
    r = 1 << (r.bit_length() - 1)
    while t % r:
        r //= 2
    return r


def _mm(a, b, *, name, ta=False, tb=False, res=None, out_dtype=F32, bm=1024, bn=1024, bk=512):
    m, k = (a.shape[1], a.shape[0]) if ta else a.shape
    n = b.shape[0] if tb else b.shape[1]
    assert (b.shape[1] if tb else b.shape[0]) == k
    bm, bn, bk = _pick(m, bm), _pick(n, bn), _pick(k, bk)
    nk = k // bk
    dims = (((0,) if ta else (1,), (1,) if tb else (0,)), ((), ()))

    def body(*refs):
        if res is None:
            a_ref, b_ref, o_ref, acc_ref = refs
        else:
            a_ref, b_ref, r_ref, o_ref, acc_ref = refs
        kk = pl.program_id(2)

        @pl.when(kk == 0)
        def _():
            acc_ref[...] = jnp.zeros_like(acc_ref)

        acc_ref[...] += lax.dot_general(a_ref[...].astype(BF16), b_ref[...].astype(BF16), dims,
                                        preferred_element_type=F32)

        @pl.when(kk == nk - 1)
        def _():
            out = acc_ref[...]
            if res is not None:
                out = out + r_ref[...]
            o_ref[...] = out.astype(o_ref.dtype)

    a_spec = pl.BlockSpec((bk, bm), lambda i, j, kk: (kk, i)) if ta else pl.BlockSpec((bm, bk), lambda i, j, kk: (i, kk))
    b_spec = pl.BlockSpec((bn, bk), lambda i, j, kk: (j, kk)) if tb else pl.BlockSpec((bk, bn), lambda i, j, kk: (kk, j))
    o_spec = pl.BlockSpec((bm, bn), lambda i, j, kk: (i, j))
    ins, specs = [a, b], [a_spec, b_spec]
    if res is not None:
        ins.append(res)
        specs.append(o_spec)
    return _pcall(body, name=name, grid=(m // bm, n // bn, nk), in_specs=specs, out_specs=o_spec,
                  out_shape=jax.ShapeDtypeStruct((m, n), out_dtype),
                  scratch=[pltpu.VMEM((bm, bn), F32)])(*ins)


def _rms_fwd(x, w, *, name, width, heads=1, col0=0, out_dtype=BF16):
    t = x.shape[0]
    tm = _rows(t, width)
    cb = col0 // width

    def body(x_ref, w_ref, o_ref):
        xv = x_ref[...]
        r = lax.rsqrt(jnp.mean(xv * xv, axis=-1, keepdims=True) + EPS)
        o_ref[...] = (xv * r * w_ref[...]).astype(o_ref.dtype)

    return _pcall(body, name=name, grid=(t // tm, heads),
                  in_specs=[pl.BlockSpec((tm, width), lambda i, h: (i, cb + h)),
                            pl.BlockSpec((1, width), lambda i, h: (0, 0))],
                  out_specs=pl.BlockSpec((tm, width), lambda i, h: (i, h)),
                  out_shape=jax.ShapeDtypeStruct((t, heads * width), out_dtype))(x, w)


def _rms_bwd(x, w, dy, *, name, width, heads=1, col0=0, dcol0=0, res=None, out_dtype=F32, with_delta=False):
    t = x.shape[0]
    tm = _rows(t, width)
    cb, dcb = col0 // width, dcol0 // width

    def body(*refs):
        refs = list(refs)
        x_ref, w_ref, dy_ref = refs[:3]
        r_ref = refs[3] if res is not None else None
        outs = refs[4:] if res is not None else refs[3:]
        dx_ref, dw_ref = outs[:2]
        xv = x_ref[...]
        dyv = dy_ref[...].astype(F32)
        r = lax.rsqrt(jnp.mean(xv * xv, axis=-1, keepdims=True) + EPS)
        xh = xv * r
        dyw = dyv * w_ref[...]
        dx = r * (dyw - xh * jnp.mean(dyw * xh, axis=-1, keepdims=True))
        if with_delta:
            outs[2][...] = jnp.broadcast_to(jnp.sum(dx * xv, axis=-1, keepdims=True), dx.shape)
        if res is not None:
            dx = dx + r_ref[...]
        dx_ref[...] = dx.astype(dx_ref.dtype)

        @pl.when((pl.program_id(0) == 0) & (pl.program_id(1) == 0))
        def _():
            dw_ref[...] = jnp.zeros_like(dw_ref)

        dw_ref[...] += (dyv * xh).reshape(tm // 8, 8, width).sum(axis=0)

    blk = pl.BlockSpec((tm, width), lambda i, h: (i, h))
    ins = [x, w, dy]
    specs = [pl.BlockSpec((tm, width), lambda i, h: (i, cb + h)), pl.BlockSpec((1, width), lambda i, h: (0, 0)),
             pl.BlockSpec((tm, width), lambda i, h: (i, dcb + h))]
    if res is not None:
        ins.append(res)
        specs.append(blk)
    out_shape = [jax.ShapeDtypeStruct((t, heads * width), out_dtype), jax.ShapeDtypeStruct((8, width), F32)]
    out_specs = [blk, pl.BlockSpec((8, width), lambda i, h: (0, 0))]
    if with_delta:
        out_shape.append(jax.ShapeDtypeStruct((t, heads * width), F32))
        out_specs.append(blk)
    return _pcall(body, name=name, grid=(t // tm, heads), in_specs=specs, out_specs=out_specs, out_shape=out_shape)(*ins)


def _sig(x):
    return 1.0 / (1.0 + jnp.exp(-x))


@jax.custom_vjp
def _sigmoid(x):
    return _sig(x)


def _sigmoid_fwd(x):
    s = _sig(x)
    return s, s


def _sigmoid_bwd(s, g):
    return (g * s * (1.0 - s),)


_sigmoid.defvjp(_sigmoid_fwd, _sigmoid_bwd)


@jax.custom_vjp
def _softplus(x):
    return jnp.maximum(x, 0.0) + jnp.log(1.0 + jnp.exp(-jnp.abs(x)))


def _softplus_fwd(x):
    return _softplus(x), x


def _softplus_bwd(x, g):
    return (g * _sig(x),)


_softplus.defvjp(_softplus_fwd, _softplus_bwd)


def _silu(x):
    return x * _sig(x)


def _dsilu(x):
    s = _sig(x)
    return s * (1.0 + x * (1.0 - s))


def _bdot(a, b, dims):
    return lax.dot_general(a.astype(BF16), b.astype(BF16), (dims, ((), ())), preferred_element_type=F32)


def _hdot(a, b, dims):
    return lax.dot_general(a, b, (dims, ((), ())), precision=HI, preferred_element_type=F32)


@jax.custom_vjp
def _nn(a, b):
    return _bdot(a, b, NN)


_nn.defvjp(lambda a, b: (_bdot(a, b, NN), (a, b)), lambda r, g: (_bdot(g, r[1], NT), _bdot(r[0], g, TN)))


@jax.custom_vjp
def _nt(a, b):
    return _bdot(a, b, NT)


_nt.defvjp(lambda a, b: (_bdot(a, b, NT), (a, b)), lambda r, g: (_bdot(g, r[1], NN), _bdot(g, r[0], TN)))


@jax.custom_vjp
def _tn(a, b):
    return _bdot(a, b, TN)


_tn.defvjp(lambda a, b: (_bdot(a, b, TN), (a, b)), lambda r, g: (_bdot(r[1], g, NT), _bdot(r[0], g, NN)))


def _conv_pre(ext, w, rows):
    acc = w[0:1] * ext[5:5 + rows]
    for j in range(1, GDN_CONV):
        acc = acc + w[j:j + 1] * ext[5 + j:5 + j + rows]
    return acc


def _conv_fwd(proj, conv_w, *, name):
    t = proj.shape[0]
    tm, tc = _pick(t, 512), 512
    nb = tm // 8

    def body(u_ref, p_ref, w_ref, o_ref):
        i = pl.program_id(1)
        prev = jnp.where(i > 0, p_ref[...], 0.0)
        ext = jnp.concatenate([prev, u_ref[...]], axis=0)
        o_ref[...] = _silu(_conv_pre(ext, w_ref[...], tm))

    return _pcall(body, name=name, grid=(CONV_CH // tc, t // tm),
                  in_specs=[pl.BlockSpec((tm, tc), lambda j, i: (i, j)),
                            pl.BlockSpec((8, tc), lambda j, i: (jnp.maximum(i * nb - 1, 0), j)),
                            pl.BlockSpec((8, tc), lambda j, i: (0, j))],
                  out_specs=pl.BlockSpec((tm, tc), lambda j, i: (i, j)),
                  out_shape=jax.ShapeDtypeStruct((t, CONV_CH), F32))(proj, proj, conv_w)


def _conv_bwd(proj, conv_w, dy, *, name):
    t = proj.shape[0]
    tm, tc = _pick(t, 512), 512
    nb = tm // 8
    last = t // tm - 1

    def body(u_ref, p_ref, n_ref, dy_ref, dyn_ref, w_ref, du_ref, dw_ref):
        i = pl.program_id(1)
        w = w_ref[...]
        prev = jnp.where(i > 0, p_ref[...], 0.0)
        ext = jnp.concatenate([prev, u_ref[...], n_ref[...]], axis=0)
        c = _conv_pre(ext, w, tm + 8)
        dy_ext = jnp.concatenate([dy_ref[...], jnp.where(i < last, dyn_ref[...], 0.0)], axis=0)
        dc = dy_ext * _dsilu(c)
        du = w[3:4] * dc[0:tm]
        for j in range(GDN_CONV - 1):
            du = du + w[j:j + 1] * dc[3 - j:3 - j + tm]
        du_ref[...] = du.astype(du_ref.dtype)

        @pl.when(i == 0)
        def _():
            dw_ref[...] = jnp.zeros_like(dw_ref)

        for j in range(GDN_CONV):
            dw_ref[j] += (dc[0:tm] * ext[5 + j:5 + j + tm]).reshape(nb, 8, tc).sum(axis=0)

    cur = lambda j, i: (i, j)
    return _pcall(body, name=name, grid=(CONV_CH // tc, t // tm),
                  in_specs=[pl.BlockSpec((tm, tc), cur),
                            pl.BlockSpec((8, tc), lambda j, i: (jnp.maximum(i * nb - 1, 0), j)),
                            pl.BlockSpec((8, tc), lambda j, i: (jnp.minimum((i + 1) * nb, t // 8 - 1), j)),
                            pl.BlockSpec((tm, tc), cur),
                            pl.BlockSpec((8, tc), lambda j, i: (jnp.minimum((i + 1) * nb, t // 8 - 1), j)),
                            pl.BlockSpec((8, tc), lambda j, i: (0, j))],
                  out_specs=[pl.BlockSpec((tm, tc), cur), pl.BlockSpec((GDN_CONV, 8, tc), lambda j, i: (0, 0, j))],
                  out_shape=[jax.ShapeDtypeStruct((t, CONV_CH), BF16), jax.ShapeDtypeStruct((GDN_CONV, 8, CONV_CH), F32)],
                  )(proj, proj, proj, dy, dy, conv_w)


def _gdn_chunk(q_raw, k_raw, v, misc, params, h, state):
    c = q_raw.shape[0]
    lane = lax.broadcasted_iota(jnp.int32, misc.shape, 1)
    b_raw = jnp.sum(jnp.where(lane == LANE_B + h, misc, 0.0), axis=1, keepdims=True)
    a_raw = jnp.sum(jnp.where(lane == LANE_A + h, misc, 0.0), axis=1, keepdims=True)
    prow = lax.broadcasted_iota(jnp.int32, params.shape, 0)
    plane = lax.broadcasted_iota(jnp.int32, params.shape, 1)
    a_log = jnp.sum(jnp.where((prow == 0) & (plane == h), params, 0.0), keepdims=True)
    dt_bias = jnp.sum(jnp.where((prow == 1) & (plane == h), params, 0.0), keepdims=True)
    beta = _sigmoid(b_raw)
    g = -jnp.exp(a_log) * _softplus(a_raw + dt_bias)

    q = q_raw * lax.rsqrt(jnp.sum(q_raw * q_raw, axis=-1, keepdims=True) + EPS) * (HEAD ** -0.5)
    k = k_raw * lax.rsqrt(jnp.sum(k_raw * k_raw, axis=-1, keepdims=True) + EPS)

    ri = lax.broadcasted_iota(jnp.int32, (c, c), 0)
    ci = lax.broadcasted_iota(jnp.int32, (c, c), 1)
    tril, strict = ri >= ci, ri > ci
    g_b = jnp.broadcast_to(g, (c, c))
    gc_col = _hdot(tril.astype(F32), g_b, NN)
    gc_row = _hdot(g_b, (ri <= ci).astype(F32), TN)
    gc = jnp.sum(jnp.where(ci == 0, gc_col, 0.0), axis=1, keepdims=True)
    decay = jnp.exp(jnp.where(tril, gc_col - gc_row, -1e30))

    kb = k * beta
    vb = v * beta
    a_mat = jnp.where(strict, _nt(kb, k) * decay, 0.0)
    x = -a_mat
    inv = (ri == ci).astype(F32) + x
    for _ in range(5):
        x = _hdot(x, x, NN)
        inv = inv + _hdot(inv, x, NN)
    u = _hdot(inv, vb, NN)
    w = _hdot(inv, kb * jnp.exp(gc), NN)
    intra = _nt(q, k) * decay

    v_new = u - _nn(w, state)
    o = _nn(q * jnp.exp(gc), state) + _nn(intra, v_new)
    g_last = jnp.sum(g, axis=0, keepdims=True)
    k_dec = k * jnp.exp(g_last - gc)
    new_state = state * jnp.exp(g_last) + _tn(k_dec, v_new)
    return o, new_state


def _gdn_specs(nc, rev):
    cidx = (lambda n: nc - 1 - n) if rev else (lambda n: n)
    hb = lambda off: pl.BlockSpec((GDN_CHUNK, HEAD), lambda n, h: (cidx(n), off + h))
    misc = pl.BlockSpec((GDN_CHUNK, HEAD), lambda n, h: (cidx(n), COL_MISC // HEAD))
    params = pl.BlockSpec((8, HEAD), lambda n, h: (0, 0))
    hist = pl.BlockSpec((1, 1, HEAD, HEAD), lambda n, h: (cidx(n), h, 0, 0))
    return hb, misc, params, hist


def _gdn_fwd(qkv, proj, params, *, name):
    t = qkv.shape[0]
    nc = t // GDN_CHUNK
    hb, misc, pspec, hist = _gdn_specs(nc, False)

    def body(q_ref, k_ref, v_ref, m_ref, p_ref, o_ref, hist_ref, s_ref):
        n, h = pl.program_id(0), pl.program_id(1)

        @pl.when(n == 0)
        def _():
            s_ref[h] = jnp.zeros((HEAD, HEAD), F32)

        state = s_ref[h]
        hist_ref[0, 0] = state
        o, new_state = _gdn_chunk(q_ref[...], k_ref[...], v_ref[...], m_ref[...], p_ref[...], h, state)
        o_ref[...] = o
        s_ref[h] = new_state

    return _pcall(body, name=name, grid=(nc, GDN_HEADS),
                  in_specs=[hb(0), hb(GDN_HEADS), hb(2 * GDN_HEADS), misc, pspec],
                  out_specs=[hb(0), hist],
                  out_shape=[jax.ShapeDtypeStruct((t, GDN_QK), F32),
                             jax.ShapeDtypeStruct((nc, GDN_HEADS, HEAD, HEAD), F32)],
                  scratch=[pltpu.VMEM((GDN_HEADS, HEAD, HEAD), F32)])(qkv, qkv, qkv, proj, params)


def _gdn_bwd(qkv, proj, params, hist_arr, do, dmisc_in, *, name):
    t = qkv.shape[0]
    nc = t // GDN_CHUNK
    hb, misc, pspec, hist = _gdn_specs(nc, True)
    mrow = pl.BlockSpec((GDN_CHUNK, HEAD), lambda n, h: (nc - 1 - n, 0))

    def body(q_ref, k_ref, v_ref, m_ref, p_ref, hist_ref, do_ref, dmi_ref,
             dq_ref, dk_ref, dv_ref, dm_ref, dp_ref, ds_ref):
        n, h = pl.program_id(0), pl.program_id(1)

        @pl.when(n == 0)
        def _():
            ds_ref[h] = jnp.zeros((HEAD, HEAD), F32)

        fn = lambda q, k, v, m, p, s: _gdn_chunk(q, k, v, m, p, h, s)
        _, vjp = jax.vjp(fn, q_ref[...], k_ref[...], v_ref[...], m_ref[...], p_ref[...], hist_ref[0, 0])
        dq, dk, dv, dm, dp, ds = vjp((do_ref[...], ds_ref[h]))
        dq_ref[...] = dq
        dk_ref[...] = dk
        dv_ref[...] = dv
        ds_ref[h] = ds

        @pl.when(h == 0)
        def _():
            dm_ref[...] = dmi_ref[...]

        dm_ref[...] += dm

        @pl.when((n == 0) & (h == 0))
        def _():
            dp_ref[...] = jnp.zeros_like(dp_ref)

        dp_ref[...] += dp

    hd = jax.ShapeDtypeStruct((t, GDN_QK), F32)
    return _pcall(body, name=name, grid=(nc, GDN_HEADS),
                  in_specs=[hb(0), hb(GDN_HEADS), hb(2 * GDN_HEADS), misc, pspec, hist, hb(0), mrow],
                  out_specs=[hb(0), hb(0), hb(0), mrow, pspec],
                  out_shape=[hd, hd, hd, jax.ShapeDtypeStruct((t, HEAD), F32), jax.ShapeDtypeStruct((8, HEAD), F32)],
                  scratch=[pltpu.VMEM((GDN_HEADS, HEAD, HEAD), F32)],
                  )(qkv, qkv, qkv, proj, params, hist_arr, do, dmisc_in)


def _gate_fwd(o_raw, proj, w, *, name):
    t = o_raw.shape[0]
    tm = _rows(t, HEAD)
    zb = COL_Z // HEAD

    def body(o_ref, z_ref, w_ref, out_ref):
        ov = o_ref[...]
        r = lax.rsqrt(jnp.mean(ov * ov, axis=-1, keepdims=True) + EPS)
        out_ref[...] = (ov * r * w_ref[...] * _silu(z_ref[...])).astype(out_ref.dtype)

    blk = pl.BlockSpec((tm, HEAD), lambda i, h: (i, h))
    return _pcall(body, name=name, grid=(t // tm, GDN_HEADS),
                  in_specs=[blk, pl.BlockSpec((tm, HEAD), lambda i, h: (i, zb + h)), pl.BlockSpec((1, HEAD), lambda i, h: (0, 0))],
                  out_specs=blk, out_shape=jax.ShapeDtypeStruct((t, GDN_QK), BF16))(o_raw, proj, w)


def _gate_bwd(o_raw, proj, w, dmixed, *, name):
    t = o_raw.shape[0]
    tm = _rows(t, HEAD)
    zb = COL_Z // HEAD

    def body(o_ref, z_ref, w_ref, dy_ref, do_ref, dz_ref, dw_ref):
        ov, zv, dyv = o_ref[...], z_ref[...], dy_ref[...]
        r = lax.rsqrt(jnp.mean(ov * ov, axis=-1, keepdims=True) + EPS)
        xh = ov * r
        dn = dyv * _silu(zv)
        dz_ref[...] = (dyv * xh * w_ref[...] * _dsilu(zv)).astype(dz_ref.dtype)
        dnw = dn * w_ref[...]
        do_ref[...] = r * (dnw - xh * jnp.mean(dnw * xh, axis=-1, keepdims=True))

        @pl.when((pl.program_id(0) == 0) & (pl.program_id(1) == 0))
        def _():
            dw_ref[...] = jnp.zeros_like(dw_ref)

        dw_ref[...] += (dn * xh).reshape(tm // 8, 8, HEAD).sum(axis=0)

    blk = pl.BlockSpec((tm, HEAD), lambda i, h: (i, h))
    return _pcall(body, name=name, grid=(t // tm, GDN_HEADS),
                  in_specs=[blk, pl.BlockSpec((tm, HEAD), lambda i, h: (i, zb + h)), pl.BlockSpec((1, HEAD), lambda i, h: (0, 0)), blk],
                  out_specs=[blk, blk, pl.BlockSpec((8, HEAD), lambda i, h: (0, 0))],
                  out_shape=[jax.ShapeDtypeStruct((t, GDN_QK), F32), jax.ShapeDtypeStruct((t, GDN_QK), BF16),
                             jax.ShapeDtypeStruct((8, HEAD), F32)])(o_raw, proj, w, dmixed)


def _rope_tables():
    half = QK_ROPE // 2
    inv = ROPE_THETA ** (-jnp.arange(half, dtype=F32) / half)
    zeros = jnp.zeros((HEAD - QK_ROPE,), F32)
    inv_row = jnp.concatenate([inv, inv, zeros])
    sign_row = jnp.concatenate([-jnp.ones((half,), F32), jnp.ones((half,), F32), zeros])
    mask_row = jnp.concatenate([jnp.ones((QK_ROPE,), F32), zeros])
    return jnp.concatenate([inv_row[None], sign_row[None], mask_row[None], jnp.zeros((5, HEAD), F32)], axis=0)


def _rotate(x, pos, tab, sign):
    ang = pos * tab[0:1]
    cos = jnp.cos(ang) * tab[2:3]
    sin = jnp.sin(ang) * (tab[1:2] * sign)
    lane = lax.broadcasted_iota(jnp.int32, x.shape, 1)
    half = QK_ROPE // 2
    partner = jnp.where(lane < half, pltpu.roll(x, HEAD - half, axis=1), pltpu.roll(x, half, axis=1))
    return x * cos + partner * sin


def _q_rot(q, pos, tab, *, name, sign, out_dtype=BF16):
    t = q.shape[0]
    tm = _pick(t, 1024)
    scale = (HEAD + QK_ROPE) ** -0.5

    def body(q_ref, pos_ref, tab_ref, o_ref):
        qv = q_ref[...].astype(F32)
        rot = _rotate(qv[:, HEAD:], pos_ref[...], tab_ref[...], sign)
        o_ref[...] = (jnp.concatenate([qv[:, :HEAD], rot], axis=1) * scale).astype(o_ref.dtype)

    blk = pl.BlockSpec((tm, QHEAD), lambda i, h: (i, h))
    return _pcall(body, name=name, grid=(t // tm, MLA_HEADS),
                  in_specs=[blk, pl.BlockSpec((tm, 1), lambda i, h: (i, 0)), pl.BlockSpec((8, HEAD), lambda i, h: (0, 0))],
                  out_specs=blk, out_shape=jax.ShapeDtypeStruct((t, MLA_HEADS * QHEAD), out_dtype))(q, pos, tab)


def _kv_prep(kv, proj, pos, tab, *, name):
    t = kv.shape[0]
    tm = _pick(t, 1024)

    def body(kv_ref, m_ref, pos_ref, tab_ref, k_ref, v_ref):
        kvv = kv_ref[...]
        tab = tab_ref[...]
        rot = _rotate(m_ref[...] * tab[2:3], pos_ref[...], tab, 1.0)
        k_ref[...] = jnp.concatenate([kvv[:, :HEAD], rot], axis=1).astype(k_ref.dtype)
        v_ref[...] = kvv[:, HEAD:].astype(v_ref.dtype)

    return _pcall(body, name=name, grid=(t // tm, MLA_HEADS),
                  in_specs=[pl.BlockSpec((tm, QHEAD), lambda i, h: (i, h)),
                            pl.BlockSpec((tm, HEAD), lambda i, h: (i, COL_MISC // HEAD)),
                            pl.BlockSpec((tm, 1), lambda i, h: (i, 0)), pl.BlockSpec((8, HEAD), lambda i, h: (0, 0))],
                  out_specs=[pl.BlockSpec((tm, QHEAD), lambda i, h: (i, h)), pl.BlockSpec((tm, HEAD), lambda i, h: (i, h))],
                  out_shape=[jax.ShapeDtypeStruct((t, MLA_HEADS * QHEAD), BF16), jax.ShapeDtypeStruct((t, MLA_HEADS * HEAD), BF16)],
                  )(kv, proj, pos, tab)


def _krope_bwd(dkr, pos, tab, *, name):
    t = dkr.shape[0]
    tm = _pick(t, 512)

    def body(d_ref, pos_ref, tab_ref, o_ref):
        d = d_ref[...]
        acc = d[:, :HEAD]
        for h in range(1, MLA_HEADS):
            acc = acc + d[:, h * HEAD:(h + 1) * HEAD]
        o_ref[...] = _rotate(acc, pos_ref[...], tab_ref[...], -1.0)

    return _pcall(body, name=name, grid=(t // tm,),
                  in_specs=[pl.BlockSpec((tm, MLA_HEADS * HEAD), lambda i: (i, 0)), pl.BlockSpec((tm, 1), lambda i: (i, 0)),
                            pl.BlockSpec((8, HEAD), lambda i: (0, 0))],
                  out_specs=pl.BlockSpec((tm, HEAD), lambda i: (i, 0)),
                  out_shape=jax.ShapeDtypeStruct((t, HEAD), F32))(dkr, pos, tab)


NEG = -1e30


def _attn_fwd(q, k, v, *, name, tq=1024, tk=1024):
    t = q.shape[0]
    tq, tk = _pick(t, tq), _pick(t, tk)
    nq, nk = t // tq, t // tk
    last_kv = lambda i: (i * tq + tq - 1) // tk

    def body(q_ref, k_ref, v_ref, o_ref, lse_ref, m_ref, l_ref, acc_ref):
        i, j = pl.program_id(1), pl.program_id(2)

        @pl.when(j == 0)
        def _():
            m_ref[...] = jnp.full_like(m_ref, NEG)
            l_ref[...] = jnp.zeros_like(l_ref)
            acc_ref[...] = jnp.zeros_like(acc_ref)

        @pl.when(j <= last_kv(i))
        def _():
            s = lax.dot_general(q_ref[...], k_ref[...], (NT, ((), ())), preferred_element_type=F32)
            qpos = i * tq + lax.broadcasted_iota(jnp.int32, s.shape, 0)
            kpos = j * tk + lax.broadcasted_iota(jnp.int32, s.shape, 1)
            s = jnp.where(kpos <= qpos, s, NEG)
            m_prev = m_ref[...]
            m_new = jnp.maximum(m_prev, jnp.max(s, axis=1, keepdims=True))
            alpha = jnp.exp(m_prev - m_new)
            p = jnp.exp(s - m_new)
            l_ref[...] = alpha * l_ref[...] + jnp.sum(p, axis=1, keepdims=True)
            acc_ref[...] = alpha * acc_ref[...] + lax.dot_general(p.astype(BF16), v_ref[...], (NN, ((), ())),
                                                                  preferred_element_type=F32)
            m_ref[...] = m_new

        @pl.when(j == nk - 1)
        def _():
            o_ref[...] = acc_ref[...] / l_ref[...]
            lse_ref[...] = jnp.broadcast_to(m_ref[...] + jnp.log(l_ref[...]), lse_ref.shape)

    qblk = pl.BlockSpec((tq, QHEAD), lambda h, i, j: (i, h))
    oblk = pl.BlockSpec((tq, HEAD), lambda h, i, j: (i, h))
    return _pcall(body, name=name, grid=(MLA_HEADS, nq, nk),
                  in_specs=[qblk, pl.BlockSpec((tk, QHEAD), lambda h, i, j: (jnp.minimum(j, last_kv(i)), h)),
                            pl.BlockSpec((tk, HEAD), lambda h, i, j: (jnp.minimum(j, last_kv(i)), h))],
                  out_specs=[oblk, oblk],
                  out_shape=[jax.ShapeDtypeStruct((t, MLA_HEADS * HEAD), F32), jax.ShapeDtypeStruct((t, MLA_HEADS * HEAD), F32)],
                  scratch=[pltpu.VMEM((tq, 1), F32), pltpu.VMEM((tq, 1), F32), pltpu.VMEM((tq, HEAD), F32)])(q, k, v)


def _attn_bwd(q, k, v, do, lse, delta, *, name, tq=512, tk=512):
    t = q.shape[0]
    tq, tk = _pick(t, tq), _pick(t, tk)
    nq, nk = t // tq, t // tk
    first_q = lambda j: (j * tk) // tq

    def body(q_ref, k_ref, v_ref, do_ref, lse_ref, dl_ref, dq_ref, dkv_ref, dkr_ref, dk_acc, dv_acc):
        j, i = pl.program_id(1), pl.program_id(2)

        @pl.when(i == 0)
        def _():
            dk_acc[...] = jnp.zeros_like(dk_acc)
            dv_acc[...] = jnp.zeros_like(dv_acc)

        @pl.when(i >= first_q(j))
        def _():
            qv, kv_, dov = q_ref[...], k_ref[...], do_ref[...].astype(BF16)
            s = lax.dot_general(qv, kv_, (NT, ((), ())), preferred_element_type=F32)
            qpos = i * tq + lax.broadcasted_iota(jnp.int32, s.shape, 0)
            kpos = j * tk + lax.broadcasted_iota(jnp.int32, s.shape, 1)
            p = jnp.where(kpos <= qpos, jnp.exp(s - lse_ref[...][:, :1]), 0.0)
            pb = p.astype(BF16)
            dv_acc[...] += lax.dot_general(pb, dov, (TN, ((), ())), preferred_element_type=F32)
            dp = lax.dot_general(dov, v_ref[...], (NT, ((), ())), preferred_element_type=F32)
            ds = (p * (dp - dl_ref[...][:, :1])).astype(BF16)
            dk_acc[...] += lax.dot_general(ds, qv, (TN, ((), ())), preferred_element_type=F32)
            contrib = lax.dot_general(ds, kv_, (NN, ((), ())), preferred_element_type=F32)
            rows = pl.ds(pl.multiple_of(i * tq, tq), tq)

            @pl.when(j == 0)
            def _():
                dq_ref[rows, :] = contrib

            @pl.when(j > 0)
            def _():
                dq_ref[rows, :] += contrib

        @pl.when(i == nq - 1)
        def _():
            dk = dk_acc[...]
            dkv_ref[...] = jnp.concatenate([dk[:, :HEAD], dv_acc[...]], axis=1).astype(dkv_ref.dtype)
            dkr_ref[...] = dk[:, HEAD:]

    qi = lambda h, j, i: (jnp.maximum(i, first_q(j)), h)
    kj = lambda h, j, i: (j, h)
    return _pcall(body, name=name, grid=(MLA_HEADS, nk, nq),
                  in_specs=[pl.BlockSpec((tq, QHEAD), qi), pl.BlockSpec((tk, QHEAD), kj), pl.BlockSpec((tk, HEAD), kj),
                            pl.BlockSpec((tq, HEAD), qi), pl.BlockSpec((tq, HEAD), qi), pl.BlockSpec((tq, HEAD), qi)],
                  out_specs=[pl.BlockSpec((t, QHEAD), lambda h, j, i: (0, h)), pl.BlockSpec((tk, QHEAD), kj),
                             pl.BlockSpec((tk, HEAD), kj)],
                  out_shape=[jax.ShapeDtypeStruct((t, MLA_HEADS * QHEAD), F32), jax.ShapeDtypeStruct((t, MLA_HEADS * QHEAD), BF16),
                             jax.ShapeDtypeStruct((t, MLA_HEADS * HEAD), F32)],
                  scratch=[pltpu.VMEM((tk, QHEAD), F32), pltpu.VMEM((tk, HEAD), F32)])(q, k, v, do, lse, delta)


def _swiglu_fwd(gu, *, name):
    t = gu.shape[0]
    tm = _pick(t, 512)

    def body(gu_ref, o_ref):
        v = gu_ref[...]
        o_ref[...] = (_silu(v[:, :FF_BLK]) * v[:, FF_BLK:]).astype(o_ref.dtype)

    return _pcall(body, name=name, grid=(t // tm, D_FF // FF_BLK),
                  in_specs=[pl.BlockSpec((tm, 2 * FF_BLK), lambda i, j: (i, j))],
                  out_specs=pl.BlockSpec((tm, FF_BLK), lambda i, j: (i, j)),
                  out_shape=jax.ShapeDtypeStruct((t, D_FF), BF16))(gu)


def _swiglu_bwd(gu, dact, *, name):
    t = gu.shape[0]
    tm = _pick(t, 512)

    def body(gu_ref, d_ref, o_ref):
        v = gu_ref[...]
        d = d_ref[...].astype(F32)
        gate, up = v[:, :FF_BLK], v[:, FF_BLK:]
        o_ref[...] = jnp.concatenate([d * up * _dsilu(gate), d * _silu(gate)], axis=1).astype(o_ref.dtype)

    return _pcall(body, name=name, grid=(t // tm, D_FF // FF_BLK),
                  in_specs=[pl.BlockSpec((tm, 2 * FF_BLK), lambda i, j: (i, j)), pl.BlockSpec((tm, FF_BLK), lambda i, j: (i, j))],
                  out_specs=pl.BlockSpec((tm, 2 * FF_BLK), lambda i, j: (i, j)),
                  out_shape=jax.ShapeDtypeStruct((t, 2 * D_FF), BF16))(gu, dact)


def _loss_bwd(x2, w, target, *, name):
    t = x2.shape[0]
    tm = _rows(t, D_MODEL)

    def body(x_ref, w_ref, t_ref, dx_ref, dw_ref, l_ref):
        xv, wv = x_ref[...], w_ref[...]
        r = lax.rsqrt(jnp.mean(xv * xv, axis=-1, keepdims=True) + EPS)
        xh = xv * r
        err = xh * wv - t_ref[...]
        dy = err * (1.0 / D_MODEL)
        dyw = dy * wv
        dx_ref[...] = r * (dyw - xh * jnp.mean(dyw * xh, axis=-1, keepdims=True))

        @pl.when(pl.program_id(0) == 0)
        def _():
            dw_ref[...] = jnp.zeros_like(dw_ref)
            l_ref[...] = jnp.zeros_like(l_ref)

        dw_ref[...] += (dy * xh).reshape(tm // 8, 8, D_MODEL).sum(axis=0)
        sq = (err * err).reshape(tm // 8, 8, D_MODEL).sum(axis=0)
        part = sq[:, :HEAD]
        for c in range(1, D_MODEL // HEAD):
            part = part + sq[:, c * HEAD:(c + 1) * HEAD]
        l_ref[...] += part * (0.5 / D_MODEL)

    row = pl.BlockSpec((tm, D_MODEL), lambda i: (i, 0))
    return _pcall(body, name=name, grid=(t // tm,),
                  in_specs=[row, pl.BlockSpec((1, D_MODEL), lambda i: (0, 0)), row],
                  out_specs=[row, pl.BlockSpec((8, D_MODEL), lambda i: (0, 0)), pl.BlockSpec((8, HEAD), lambda i: (0, 0))],
                  out_shape=[jax.ShapeDtypeStruct((t, D_MODEL), F32), jax.ShapeDtypeStruct((8, D_MODEL), F32),
                             jax.ShapeDtypeStruct((8, HEAD), F32)])(x2, w, target)


def _unshard_cols(g):
    return jnp.transpose(g, (1, 0, 2)).reshape(g.shape[1], N_DEV * g.shape[2])


def _shard_cols(w):
    return jnp.transpose(w.reshape(w.shape[0], N_DEV, w.shape[1] // N_DEV), (1, 0, 2))


def _win_to_padded(w):
    pad = jnp.zeros((w.shape[0], PROJ_W - IN_WIDTH), w.dtype)
    return jnp.concatenate([w[:, :4096], w[:, 4112:5136], w[:, 5136:5200], w[:, 4096:4112], pad], axis=1)


def _win_from_padded(d):
    return jnp.concatenate([d[:, :4096], d[:, 5184:5200], d[:, 4096:5120], d[:, 5120:5184]], axis=1)


def _wuq_to_padded(w):
    w3 = w.reshape(w.shape[0], MLA_HEADS, HEAD + QK_ROPE)
    return jnp.pad(w3, ((0, 0), (0, 0), (0, QHEAD - HEAD - QK_ROPE))).reshape(w.shape[0], MLA_HEADS * QHEAD)


def _wuq_from_padded(d):
    return d.reshape(d.shape[0], MLA_HEADS, QHEAD)[:, :, :HEAD + QK_ROPE].reshape(d.shape[0], MLA_HEADS * (HEAD + QK_ROPE))


def _gu_interleave(wg, wu):
    r = wg.shape[0]
    return jnp.stack([wg.reshape(r, D_FF // FF_BLK, FF_BLK), wu.reshape(r, D_FF // FF_BLK, FF_BLK)], axis=2).reshape(r, 2 * D_FF)


def _gu_split(d):
    d4 = d.reshape(d.shape[0], D_FF // FF_BLK, 2, FF_BLK)
    return d4[:, :, 0].reshape(d.shape[0], D_FF), d4[:, :, 1].reshape(d.shape[0], D_FF)


def _local_step(x, pos, target, win_p, wuq_p, wukv, wout, wgu, wdown, conv_w, small):
    tab = _rope_tables()
    h1 = _rms_fwd(x, small["attn_norm_w"], name="rms1_fwd", width=D_MODEL)
    proj = _mm(h1, win_p, name="mm_in", bn=768)
    qkv = _conv_fwd(proj, conv_w, name="conv_fwd")
    o_gdn_raw, hist = _gdn_fwd(qkv, proj, small["gdn_params"], name="gdn_fwd")
    o_gdn = _gate_fwd(o_gdn_raw, proj, small["gdn_norm_w"], name="gate_fwd")
    cqn = _rms_fwd(proj, small["q_norm_w"], name="rmsq_fwd", width=Q_LORA, col0=COL_CQ)
    ckvn = _rms_fwd(proj, small["kv_norm_w"], name="rmskv_fwd", width=KV_LORA, col0=COL_CKV)
    q_pre = _mm(cqn, wuq_p, name="mm_uq")
    kv = _mm(ckvn, wukv, name="mm_ukv")
    q_full = _q_rot(q_pre, pos, tab, name="q_rot", sign=1.0)
    k_full, v_b = _kv_prep(kv, proj, pos, tab, name="kv_prep")
    o_mla_raw, lse = _attn_fwd(q_full, k_full, v_b, name="attn_fwd")
    o_mla = _rms_fwd(o_mla_raw, small["mla_out_norm_w"], name="rmso_fwd", width=HEAD, heads=MLA_HEADS)
    mixed = jnp.concatenate([o_gdn, o_mla], axis=1)
    x1 = _mm(mixed, wout, name="mm_out", res=x)
    h2 = _rms_fwd(x1, small["ffn_norm_w"], name="rms2_fwd", width=D_MODEL)
    gu = _mm(h2, wgu, name="mm_gu")
    act = _swiglu_fwd(gu, name="swiglu_fwd")
    x2 = _mm(act, wdown, name="mm_down", res=x1)
    dx2, dw_final, loss_part = _loss_bwd(x2, small["final_norm_w"], target, name="loss_bwd")
    dact = _mm(dx2, wdown, name="mm_down_dx", tb=True, out_dtype=BF16)
    d_wdown = _mm(act, dx2, name="mm_down_dw", ta=True, out_dtype=BF16)
    dgu = _swiglu_bwd(gu, dact, name="swiglu_bwd")
    dh2 = _mm(dgu, wgu, name="mm_gu_dx", tb=True)
    d_wgu = _mm(h2, dgu, name="mm_gu_dw", ta=True, out_dtype=BF16)
    dx1, dw_ffn = _rms_bwd(x1, small["ffn_norm_w"], dh2, name="rms2_bwd", width=D_MODEL, res=dx2)
    dmixed = _mm(dx1, wout, name="mm_out_dx", tb=True)
    d_wout = _mm(mixed, dx1, name="mm_out_dw", ta=True, out_dtype=BF16)
    do_mla, dw_mla_out, delta = _rms_bwd(o_mla_raw, small["mla_out_norm_w"], dmixed, name="rmso_bwd", width=HEAD,
                                         heads=MLA_HEADS, dcol0=GDN_QK, with_delta=True)
    dq_full, dkv, dkr_h = _attn_bwd(q_full, k_full, v_b, do_mla, lse, delta, name="attn_bwd")
    dq_pre = _q_rot(dq_full, pos, tab, name="q_rot_bwd", sign=-1.0)
    dmisc_kr = _krope_bwd(dkr_h, pos, tab, name="krope_bwd")
    dcqn = _mm(dq_pre, wuq_p, name="mm_uq_dx", tb=True)
    d_wuq = _mm(cqn, dq_pre, name="mm_uq_dw", ta=True, out_dtype=BF16)
    dckvn = _mm(dkv, wukv, name="mm_ukv_dx", tb=True)
    d_wukv = _mm(ckvn, dkv, name="mm_ukv_dw", ta=True, out_dtype=BF16)
    dcq, dw_qn = _rms_bwd(proj, small["q_norm_w"], dcqn, name="rmsq_bwd", width=Q_LORA, col0=COL_CQ, out_dtype=BF16)
    dckv, dw_kvn = _rms_bwd(proj, small["kv_norm_w"], dckvn, name="rmskv_bwd", width=KV_LORA, col0=COL_CKV, out_dtype=BF16)
    do_gdn, dz, dw_gdn = _gate_bwd(o_gdn_raw, proj, small["gdn_norm_w"], dmixed, name="gate_bwd")
    dgq, dgk, dgv, dmisc, d_params = _gdn_bwd(qkv, proj, small["gdn_params"], hist, do_gdn, dmisc_kr, name="gdn_bwd")
    dqkv_pre, dconv = _conv_bwd(proj, conv_w, jnp.concatenate([dgq, dgk, dgv], axis=1), name="conv_bwd")
    dproj = jnp.concatenate([dqkv_pre, dz, dcq, dckv, dmisc.astype(BF16), jnp.zeros((x.shape[0], PROJ_W - COL_MISC - HEAD), BF16)], axis=1)
    dh1 = _mm(dproj, win_p, name="mm_in_dx", tb=True, bk=768)
    d_win = _mm(h1, dproj, name="mm_in_dw", ta=True, out_dtype=BF16, bn=768)
    dx, dw_attn = _rms_bwd(x, small["attn_norm_w"], dh1, name="rms1_bwd", width=D_MODEL, res=dx1)

    big = {"w_in": d_win, "w_uq": d_wuq, "w_ukv": d_wukv, "w_out": d_wout, "w_gu": d_wgu, "w_down": d_wdown}
    sm = {"attn_norm_w": dw_attn, "ffn_norm_w": dw_ffn, "final_norm_w": dw_final, "q_norm_w": dw_qn, "kv_norm_w": dw_kvn,
          "gdn_norm_w": dw_gdn, "mla_out_norm_w": dw_mla_out, "gdn_params": d_params, "conv_w": dconv, "loss": loss_part}
    return dx, big, sm


def _my_place():
    x, y, c = lax.axis_index("x"), lax.axis_index("y"), lax.axis_index("c")
    return x, y, c, 4 * x + 2 * y + c


def _peer(x, y, c, p):
    px, py, pc = x ^ ((p >> 2) & 1), y ^ ((p >> 1) & 1), c ^ (p & 1)
    return (px, py, pc), 4 * px + 2 * py + pc


def _exchange(arrays, gather, *, name):
    na = len(arrays)

    def body(*refs):
        ins, outs = refs[:na], refs[na:2 * na]
        send_sems, recv_sems, local_sems = refs[2 * na:]
        x, y, c, me = _my_place()
        local = []
        for k in range(na):
            src = ins[k] if gather[k] else ins[k].at[me]
            cp = pltpu.make_async_copy(src, outs[k].at[me], local_sems.at[k])
            cp.start()
            local.append(cp)
        sent = []
        for p in range(1, N_DEV):
            place, num = _peer(x, y, c, p)
            for k in range(na):
                src = ins[k] if gather[k] else ins[k].at[num]
                cp = pltpu.make_async_remote_copy(src_ref=src, dst_ref=outs[k].at[me],
                                                  send_sem=send_sems.at[k * (N_DEV - 1) + p - 1],
                                                  recv_sem=recv_sems.at[k * (N_DEV - 1) + p - 1],
                                                  device_id=place, device_id_type=MESH)
                cp.start()
                sent.append(cp)
        for p in range(1, N_DEV):
            place, num = _peer(x, y, c, p)
            for k in range(na):
                src = ins[k] if gather[k] else ins[k].at[num]
                pltpu.make_async_remote_copy(src_ref=src, dst_ref=outs[k].at[num],
                                             send_sem=send_sems.at[k * (N_DEV - 1) + p - 1],
                                             recv_sem=recv_sems.at[k * (N_DEV - 1) + p - 1],
                                             device_id=place, device_id_type=MESH).wait_recv()
        for cp in sent:
            cp.wait_send()
        for cp in local:
            cp.wait()

    any_spec = pl.BlockSpec(memory_space=pl.ANY)
    out_shape = [jax.ShapeDtypeStruct(((N_DEV,) + a.shape) if g else a.shape, a.dtype) for a, g in zip(arrays, gather)]
    return pl.pallas_call(
        body, name=name, in_specs=[any_spec] * na, out_specs=[any_spec] * na, out_shape=out_shape,
        scratch_shapes=[pltpu.SemaphoreType.DMA((na * (N_DEV - 1),)), pltpu.SemaphoreType.DMA((na * (N_DEV - 1),)),
                        pltpu.SemaphoreType.DMA((na,))])(*arrays)


def _adamw_math(g, w, m, v):
    m = ADAM_B1 * m + (1.0 - ADAM_B1) * g
    v = ADAM_B2 * v + (1.0 - ADAM_B2) * (g * g)
    m_hat = m / (1.0 - ADAM_B1 ** ADAM_STEP)
    v_hat = v / (1.0 - ADAM_B2 ** ADAM_STEP)
    delta = -ADAM_LR * (m_hat / (jnp.sqrt(v_hat) + ADAM_EPS) + ADAM_WD * w)
    return delta, m, v


def _adamw(parts, w, m, v, *, name):
    npart, r, c = parts.shape
    tr = r if r * c * 4 <= (1 << 20) else _rows(r, c, 1 << 20)

    def body(p_ref, w_ref, m_ref, v_ref, g_ref, d_ref, nm_ref, nv_ref):
        g = p_ref[0].astype(F32)
        for s in range(1, npart):
            g = g + p_ref[s].astype(F32)
        g_ref[...] = g
        d_ref[...], nm_ref[...], nv_ref[...] = _adamw_math(g, w_ref[...], m_ref[...], v_ref[...])

    blk = pl.BlockSpec((tr, c), lambda i: (i, 0))
    sds = jax.ShapeDtypeStruct((r, c), F32)
    return _pcall(body, name=name, grid=(r // tr,),
                  in_specs=[pl.BlockSpec((npart, tr, c), lambda i: (0, i, 0)), blk, blk, blk],
                  out_specs=[blk] * 4, out_shape=[sds] * 4)(parts, w, m, v)


def _sum_parts(parts, *, name):
    npart, r, c = parts.shape

    def body(p_ref, o_ref):
        g = p_ref[0]
        for s in range(1, npart):
            g = g + p_ref[s]
        o_ref[...] = g

    return _pcall(body, name=name, grid=(1,), in_specs=[pl.BlockSpec((npart, r, c), lambda i: (0, 0, 0))],
                  out_specs=pl.BlockSpec((r, c), lambda i: (0, 0)), out_shape=jax.ShapeDtypeStruct((r, c), F32))(parts)


_SMALL = (("attn_norm_w", D_MODEL), ("ffn_norm_w", D_MODEL), ("final_norm_w", D_MODEL), ("q_norm_w", Q_LORA),
          ("kv_norm_w", KV_LORA), ("gdn_norm_w", HEAD), ("mla_out_norm_w", HEAD), ("a_log", HEAD), ("dt_bias", HEAD))
_SMALL_ROWS = sum(n for _, n in _SMALL) // HEAD
_CONV_ROWS = GDN_CONV * CONV_CH // HEAD
_PACK_ROWS = 160


def _pad_lanes(v, n):
    v = v.reshape(-1)
    return jnp.concatenate([v, jnp.zeros((n - v.shape[0],), v.dtype)])


def kernel(x, positions, attn_norm_w, w_in, conv_w, a_log, dt_bias, gdn_norm_w, q_norm_w, w_uq, kv_norm_w, w_ukv, mla_out_norm_w, w_out, ffn_norm_w, w_gate, w_up, w_down, final_norm_w, loss_target, m_attn_norm_w, m_w_in, m_conv_w, m_a_log, m_dt_bias, m_gdn_norm_w, m_q_norm_w, m_w_uq, m_kv_norm_w, m_w_ukv, m_mla_out_norm_w, m_w_out, m_ffn_norm_w, m_w_gate, m_w_up, m_w_down, m_final_norm_w, v_attn_norm_w, v_w_in, v_conv_w, v_a_log, v_dt_bias, v_gdn_norm_w, v_q_norm_w, v_w_uq, v_kv_norm_w, v_w_ukv, v_mla_out_norm_w, v_w_out, v_ffn_norm_w, v_w_gate, v_w_up, v_w_down, v_final_norm_w):
    t = x.shape[1]
    me = 4 * lax.axis_index("x") + 2 * lax.axis_index("y") + lax.axis_index("c")
    weights = dict(attn_norm_w=attn_norm_w, w_in=w_in, conv_w=conv_w, a_log=a_log, dt_bias=dt_bias, gdn_norm_w=gdn_norm_w,
                   q_norm_w=q_norm_w, w_uq=w_uq, kv_norm_w=kv_norm_w, w_ukv=w_ukv, mla_out_norm_w=mla_out_norm_w, w_out=w_out,
                   ffn_norm_w=ffn_norm_w, w_gate=w_gate, w_up=w_up, w_down=w_down, final_norm_w=final_norm_w)
    mom_m = dict(attn_norm_w=m_attn_norm_w, w_in=m_w_in, conv_w=m_conv_w, a_log=m_a_log, dt_bias=m_dt_bias, gdn_norm_w=m_gdn_norm_w,
                 q_norm_w=m_q_norm_w, w_uq=m_w_uq, kv_norm_w=m_kv_norm_w, w_ukv=m_w_ukv, mla_out_norm_w=m_mla_out_norm_w,
                 w_out=m_w_out, ffn_norm_w=m_ffn_norm_w, w_gate=m_w_gate, w_up=m_w_up, w_down=m_w_down, final_norm_w=m_final_norm_w)
    mom_v = dict(attn_norm_w=v_attn_norm_w, w_in=v_w_in, conv_w=v_conv_w, a_log=v_a_log, dt_bias=v_dt_bias, gdn_norm_w=v_gdn_norm_w,
                 q_norm_w=v_q_norm_w, w_uq=v_w_uq, kv_norm_w=v_kv_norm_w, w_ukv=v_w_ukv, mla_out_norm_w=v_mla_out_norm_w,
                 w_out=v_w_out, ffn_norm_w=v_ffn_norm_w, w_gate=v_w_gate, w_up=v_w_up, w_down=v_w_down, final_norm_w=v_final_norm_w)
    big_names = ("w_in", "w_uq", "w_ukv", "w_out", "w_gate", "w_up", "w_down")

    shards = [weights[n][0].astype(BF16) for n in big_names] + [weights["conv_w"][0]]
    g_in, g_uq, g_ukv, g_out, g_gate, g_up, g_down, g_conv = _exchange(shards, [True] * 8, name="gather_weights")
    win_p = _win_to_padded(_unshard_cols(g_in))
    wuq_p = _wuq_to_padded(_unshard_cols(g_uq))
    wukv = _unshard_cols(g_ukv)
    wout = g_out.reshape(D_MODEL, D_MODEL)
    wgu = _gu_interleave(_unshard_cols(g_gate), _unshard_cols(g_up))
    wdown = g_down.reshape(D_FF, D_MODEL)
    conv_full = jnp.concatenate([_unshard_cols(g_conv), jnp.zeros((8 - GDN_CONV, CONV_CH), F32)], axis=0)

    gdn_params = jnp.concatenate([_pad_lanes(a_log, HEAD)[None], _pad_lanes(dt_bias, HEAD)[None], jnp.zeros((6, HEAD), F32)], axis=0)
    small = {n: weights[n].reshape(1, -1) for n in ("attn_norm_w", "ffn_norm_w", "final_norm_w", "q_norm_w", "kv_norm_w",
                                                    "gdn_norm_w", "mla_out_norm_w")}
    small["gdn_params"] = gdn_params

    dx, big, sm = _local_step(x[0], positions.reshape(t, 1).astype(F32), loss_target[0], win_p, wuq_p, wukv, wout, wgu, wdown,
                              conv_full, small)

    d_gate, d_up = _gu_split(big["w_gu"])
    send = [_shard_cols(_win_from_padded(big["w_in"])), _shard_cols(_wuq_from_padded(big["w_uq"])), _shard_cols(big["w_ukv"]),
            big["w_out"].reshape(N_DEV, D_MODEL // N_DEV, D_MODEL), _shard_cols(d_gate), _shard_cols(d_up),
            big["w_down"].reshape(N_DEV, D_FF // N_DEV, D_MODEL)]
    rows8 = lambda name: jnp.sum(sm[name], axis=0)
    pieces = [rows8(n) for n, _ in _SMALL[:7]]
    pieces += [_pad_lanes(jnp.sum(sm["gdn_params"][0:1], axis=0), HEAD), _pad_lanes(jnp.sum(sm["gdn_params"][1:2], axis=0), HEAD)]
    pieces.append(jnp.sum(sm["conv_w"], axis=1).reshape(-1))
    pieces.append(_pad_lanes(jnp.sum(sm["loss"]).reshape(1), HEAD))
    packed = _pad_lanes(jnp.concatenate(pieces), _PACK_ROWS * HEAD).reshape(_PACK_ROWS, HEAD)
    recv = _exchange(send + [packed], [False] * 7 + [True], name="exchange_grads")

    outs_g, outs_d, outs_m, outs_v = {}, {}, {}, {}
    for name, parts in zip(big_names, recv[:7]):
        w3 = weights[name]
        g, d, nm, nv = _adamw(parts, w3[0], mom_m[name][0], mom_v[name][0], name="adamw_" + name)
        outs_g[name], outs_d[name], outs_m[name], outs_v[name] = g[None], d[None], nm[None], nv[None]

    total = _sum_parts(recv[7], name="sum_small")
    flat = total.reshape(-1)
    loss = flat[(_SMALL_ROWS + _CONV_ROWS) * HEAD]
    g_small, off = {}, 0
    for n, size in _SMALL:
        g_small[n] = flat[off:off + size]
        off += size
    g_conv_full = flat[off:off + GDN_CONV * CONV_CH].reshape(GDN_CONV, CONV_CH)
    g_small["conv_w"] = lax.dynamic_slice(g_conv_full, (0, me * (CONV_CH // N_DEV)), (GDN_CONV, CONV_CH // N_DEV)).reshape(-1)
    order = [n for n, _ in _SMALL] + ["conv_w"]
    sizes = dict(_SMALL)
    sizes["conv_w"] = GDN_CONV * CONV_CH // N_DEV
    true_size = {n: weights[n].size for n in order}

    def pack(d):
        return jnp.concatenate([_pad_lanes(d[n], sizes[n]) for n in order]).reshape(1, -1, HEAD)

    g2, d2, m2, v2 = _adamw(pack(g_small), pack(weights)[0], pack(mom_m)[0], pack(mom_v)[0], name="adamw_small")
    off = 0
    for n in order:
        for src, dst in ((g2, outs_g), (d2, outs_d), (m2, outs_m), (v2, outs_v)):
            dst[n] = src.reshape(-1)[off:off + true_size[n]].reshape(weights[n].shape)
        off += sizes[n]

    names = ("attn_norm_w", "w_in", "conv_w", "a_log", "dt_bias", "gdn_norm_w", "q_norm_w", "w_uq", "kv_norm_w", "w_ukv",
             "mla_out_norm_w", "w_out", "ffn_norm_w", "w_gate", "w_up", "w_down", "final_norm_w")
    return (loss, dx[None], *[outs_g[n] for n in names], *[outs_d[n] for n in names], *[outs_m[n] for n in names],
            *[outs_v[n] for n in names])
```

```python
import functools
import math

import jax
import jax.numpy as jnp
from jax import lax
from jax.experimental import pallas as pl
from jax.experimental.pallas import tpu as pltpu

F32 = jnp.float32
BF16 = jnp.bfloat16
HI = lax.Precision.HIGHEST

D_MODEL = 2048
GDN_HEADS = 8
HEAD = 128
GDN_CONV = 4
GDN_CHUNK = 64
GDN_QK = GDN_HEADS * HEAD
CONV_CH = 3 * GDN_QK
MLA_HEADS = 8
QK_ROPE = 64
Q_LORA = 512
KV_LORA = 512
ROPE_THETA = 10000.0
D_FF = 5632
EPS = 1e-6
IN_WIDTH = 5200
ADAM_LR, ADAM_B1, ADAM_B2, ADAM_EPS, ADAM_WD, ADAM_STEP = 0.001, 0.9, 0.999, 1e-08, 0.01, 10

PROJ_W = 5376
COL_Z = 3072
COL_CQ = 4096
COL_CKV = 4608
COL_MISC = 5120
LANE_B = 64
LANE_A = 72
QHEAD = 256
FF_BLK = 512
N_DEV = 8
MESH = pl.DeviceIdType.MESH
VMEM_LIMIT_MB = 48

NN = ((1,), (0,))
NT = ((1,), (1,))
TN = ((0,), (0,))


def _my_place():
    x, y, c = lax.axis_index("x"), lax.axis_index("y"), lax.axis_index("c")
    return x, y, c, 4 * x + 2 * y + c


def _peer(x, y, c, p):
    px, py, pc = x ^ ((p >> 2) & 1), y ^ ((p >> 1) & 1), c ^ (p & 1)
    return (px, py, pc), 4 * px + 2 * py + pc


class _Exchange:
    def __init__(self, arrays, gather):
        self.arrays, self.gather, self.n = list(arrays), list(gather), len(arrays)

    def out_shape(self):
        return [jax.ShapeDtypeStruct(((N_DEV,) + a.shape) if g else a.shape, a.dtype)
                for a, g in zip(self.arrays, self.gather)]

    def sems(self):
        return [pltpu.SemaphoreType.DMA((self.n * (N_DEV - 1),)), pltpu.SemaphoreType.DMA((self.n * (N_DEV - 1),)),
                pltpu.SemaphoreType.DMA((self.n,))]

    def _copies(self, ins, outs, sems):
        send_sems, recv_sems, local_sems = sems
        x, y, c, me = _my_place()
        local = [pltpu.make_async_copy(ins[k] if self.gather[k] else ins[k].at[me], outs[k].at[me], local_sems.at[k])
                 for k in range(self.n)]
        sent, received = [], []
        for p in range(1, N_DEV):
            place, num = _peer(x, y, c, p)
            for k in range(self.n):
                src = ins[k] if self.gather[k] else ins[k].at[num]
                idx = k * (N_DEV - 1) + p - 1
                mk = lambda dst: pltpu.make_async_remote_copy(src_ref=src, dst_ref=dst, send_sem=send_sems.at[idx],
                                                              recv_sem=recv_sems.at[idx], device_id=place, device_id_type=MESH)
                sent.append(mk(outs[k].at[me]))
                received.append(mk(outs[k].at[num]))
        return local, sent, received

    def start(self, ins, outs, sems):
        local, sent, _ = self._copies(ins, outs, sems)
        for cp in local + sent:
            cp.start()

    def wait(self, ins, outs, sems):
        local, sent, received = self._copies(ins, outs, sems)
        for cp in received:
            cp.wait_recv()
        for cp in sent:
            cp.wait_send()
        for cp in local:
            cp.wait()


def _pcall(body, *, name, grid, in_specs, out_specs, out_shape, scratch=(), carry=None):
    params = pltpu.CompilerParams(dimension_semantics=("arbitrary",) * len(grid), vmem_limit_bytes=VMEM_LIMIT_MB << 20)
    if carry is None:
        return pl.pallas_call(body, name=name, grid=grid, in_specs=in_specs, out_specs=out_specs, out_shape=out_shape,
                              scratch_shapes=list(scratch), compiler_params=params)
    single = not isinstance(out_specs, (list, tuple))
    out_specs = [out_specs] if single else list(out_specs)
    out_shape = [out_shape] if single else list(out_shape)
    n_in, n_out, n_scr, na = len(in_specs), len(out_specs), len(scratch), carry.n

    def wrapped(*refs):
        ins, cin = refs[:n_in], refs[n_in:n_in + na]
        outs, cout = refs[n_in + na:n_in + na + n_out], refs[n_in + na + n_out:n_in + 2 * na + n_out]
        scr, sems = refs[n_in + 2 * na + n_out:n_in + 2 * na + n_out + n_scr], refs[n_in + 2 * na + n_out + n_scr:]
        first = functools.reduce(lambda a, b: a & b, [pl.program_id(d) == 0 for d in range(len(grid))])
        last = functools.reduce(lambda a, b: a & b, [pl.program_id(d) == grid[d] - 1 for d in range(len(grid))])

        @pl.when(first)
        def _():
            carry.start(cin, cout, sems)

        body(*ins, *outs, *scr)

        @pl.when(last)
        def _():
            carry.wait(cin, cout, sems)

    any_spec = pl.BlockSpec(memory_space=pl.ANY)
    call = pl.pallas_call(wrapped, name=name, grid=grid, in_specs=list(in_specs) + [any_spec] * na,
                          out_specs=out_specs + [any_spec] * na, out_shape=out_shape + carry.out_shape(),
                          scratch_shapes=list(scratch) + carry.sems(), compiler_params=params)

    def run(*args):
        res = call(*args, *carry.arrays)
        main = res[0] if single else list(res[:n_out])
        return main, list(res[n_out:])

    return run


def _pick(dim, pref):
    if dim <= pref:
        return dim
    c = pref
    while c >= 128:
        if dim % c == 0 and c % 128 == 0:
            return c
        c -= 128
    return dim


def _rows(t, width, target_bytes=2 << 20):
    r = max(8, min(t, target_bytes // (4 * width)))
    r = 1 << (r.bit_length() - 1)
    while t % r:
        r //= 2
    return r


def _mm(a, b, *, name, ta=False, tb=False, res=None, out_dtype=F32, bm=1024, bn=1024, bk=512):
    m, k = (a.shape[1], a.shape[0]) if ta else a.shape
    n = b.shape[0] if tb else b.shape[1]
    assert (b.shape[1] if tb else b.shape[0]) == k
    bm, bn, bk = _pick(m, bm), _pick(n, bn), _pick(k, bk)
    nk = k // bk
    dims = (((0,) if ta else (1,), (1,) if tb else (0,)), ((), ()))

    def body(*refs):
        if res is None:
            a_ref, b_ref, o_ref, acc_ref = refs
        else:
            a_ref, b_ref, r_ref, o_ref, acc_ref = refs
        kk = pl.program_id(2)

        @pl.when(kk == 0)
        def _():
            acc_ref[...] = jnp.zeros_like(acc_ref)

        acc_ref[...] += lax.dot_general(a_ref[...].astype(BF16), b_ref[...].astype(BF16), dims,
                                        preferred_element_type=F32)

        @pl.when(kk == nk - 1)
        def _():
            out = acc_ref[...]
            if res is not None:
                out = out + r_ref[...]
            o_ref[...] = out.astype(o_ref.dtype)

    a_spec = pl.BlockSpec((bk, bm), lambda i, j, kk: (kk, i)) if ta else pl.BlockSpec((bm, bk), lambda i, j, kk: (i, kk))
    b_spec = pl.BlockSpec((bn, bk), lambda i, j, kk: (j, kk)) if tb else pl.BlockSpec((bk, bn), lambda i, j, kk: (kk, j))
    o_spec = pl.BlockSpec((bm, bn), lambda i, j, kk: (i, j))
    ins, specs = [a, b], [a_spec, b_spec]
    if res is not None:
        ins.append(res)
        specs.append(o_spec)
    return _pcall(body, name=name, grid=(m // bm, n // bn, nk), in_specs=specs, out_specs=o_spec,
                  out_shape=jax.ShapeDtypeStruct((m, n), out_dtype),
                  scratch=[pltpu.VMEM((bm, bn), F32)])(*ins)


def _rms_fwd(x, w, *, name, width, heads=1, col0=0, out_dtype=BF16):
    t = x.shape[0]
    tm = _rows(t, width)
    cb = col0 // width

    def body(x_ref, w_ref, o_ref):
        xv = x_ref[...]
        r = lax.rsqrt(jnp.mean(xv * xv, axis=-1, keepdims=True) + EPS)
        o_ref[...] = (xv * r * w_ref[...]).astype(o_ref.dtype)

    return _pcall(body, name=name, grid=(t // tm, heads),
                  in_specs=[pl.BlockSpec((tm, width), lambda i, h: (i, cb + h)),
                            pl.BlockSpec((1, width), lambda i, h: (0, 0))],
                  out_specs=pl.BlockSpec((tm, width), lambda i, h: (i, h)),
                  out_shape=jax.ShapeDtypeStruct((t, heads * width), out_dtype))(x, w)


def _rms_bwd(x, w, dy, *, name, width, heads=1, col0=0, dcol0=0, res=None, out_dtype=F32, with_delta=False):
    t = x.shape[0]
    tm = _rows(t, width)
    cb, dcb = col0 // width, dcol0 // width

    def body(*refs):
        refs = list(refs)
        x_ref, w_ref, dy_ref = refs[:3]
        r_ref = refs[3] if res is not None else None
        outs = refs[4:] if res is not None else refs[3:]
        dx_ref, dw_ref = outs[:2]
        xv = x_ref[...]
        dyv = dy_ref[...].astype(F32)
        r = lax.rsqrt(jnp.mean(xv * xv, axis=-1, keepdims=True) + EPS)
        xh = xv * r
        dyw = dyv * w_ref[...]
        dx = r * (dyw - xh * jnp.mean(dyw * xh, axis=-1, keepdims=True))
        if with_delta:
            outs[2][...] = jnp.broadcast_to(jnp.sum(dx * xv, axis=-1, keepdims=True), dx.shape)
        if res is not None:
            dx = dx + r_ref[...]
        dx_ref[...] = dx.astype(dx_ref.dtype)

        @pl.when((pl.program_id(0) == 0) & (pl.program_id(1) == 0))
        def _():
            dw_ref[...] = jnp.zeros_like(dw_ref)

        dw_ref[...] += (dyv * xh).reshape(tm // 8, 8, width).sum(axis=0)

    blk = pl.BlockSpec((tm, width), lambda i, h: (i, h))
    ins = [x, w, dy]
    specs = [pl.BlockSpec((tm, width), lambda i, h: (i, cb + h)), pl.BlockSpec((1, width), lambda i, h: (0, 0)),
             pl.BlockSpec((tm, width), lambda i, h: (i, dcb + h))]
    if res is not None:
        ins.append(res)
        specs.append(blk)
    out_shape = [jax.ShapeDtypeStruct((t, heads * width), out_dtype), jax.ShapeDtypeStruct((8, width), F32)]
    out_specs = [blk, pl.BlockSpec((8, width), lambda i, h: (0, 0))]
    if with_delta:
        out_shape.append(jax.ShapeDtypeStruct((t, heads * width), F32))
        out_specs.append(blk)
    return _pcall(body, name=name, grid=(t // tm, heads), in_specs=specs, out_specs=out_specs, out_shape=out_shape)(*ins)


def _sig(x):
    return 1.0 / (1.0 + jnp.exp(-x))


@jax.custom_vjp
def _sigmoid(x):
    return _sig(x)


def _sigmoid_fwd(x):
    s = _sig(x)
    return s, s


def _sigmoid_bwd(s, g):
    return (g * s * (1.0 - s),)


_sigmoid.defvjp(_sigmoid_fwd, _sigmoid_bwd)


@jax.custom_vjp
def _softplus(x):
    return jnp.maximum(x, 0.0) + jnp.log(1.0 + jnp.exp(-jnp.abs(x)))


def _softplus_fwd(x):
    return _softplus(x), x


def _softplus_bwd(x, g):
    return (g * _sig(x),)


_softplus.defvjp(_softplus_fwd, _softplus_bwd)


def _silu(x):
    return x * _sig(x)


def _dsilu(x):
    s = _sig(x)
    return s * (1.0 + x * (1.0 - s))


NN3 = (((2,), (1,)), ((0,), (0,)))
NT3 = (((2,), (2,)), ((0,), (0,)))
TN3 = (((1,), (1,)), ((0,), (0,)))


def _bdot(a, b, dims):
    return lax.dot_general(a.astype(BF16), b.astype(BF16), dims, preferred_element_type=F32)


def _hdot(a, b, dims):
    return lax.dot_general(a, b, dims, precision=HI, preferred_element_type=F32)


@jax.custom_vjp
def _nn(a, b):
    return _bdot(a, b, NN3)


_nn.defvjp(lambda a, b: (_bdot(a, b, NN3), (a, b)), lambda r, g: (_bdot(g, r[1], NT3), _bdot(r[0], g, TN3)))


@jax.custom_vjp
def _nt(a, b):
    return _bdot(a, b, NT3)


_nt.defvjp(lambda a, b: (_bdot(a, b, NT3), (a, b)), lambda r, g: (_bdot(g, r[1], NN3), _bdot(g, r[0], TN3)))


@jax.custom_vjp
def _tn(a, b):
    return _bdot(a, b, TN3)


_tn.defvjp(lambda a, b: (_bdot(a, b, TN3), (a, b)), lambda r, g: (_bdot(r[1], g, NT3), _bdot(r[0], g, NN3)))


def _conv_pre(ext, w, rows):
    acc = w[0:1] * ext[5:5 + rows]
    for j in range(1, GDN_CONV):
        acc = acc + w[j:j + 1] * ext[5 + j:5 + j + rows]
    return acc


def _conv_fwd(proj, conv_w, *, name):
    t = proj.shape[0]
    tm, tc = _pick(t, 512), 512
    nb = tm // 8

    def body(u_ref, p_ref, w_ref, o_ref):
        i = pl.program_id(1)
        prev = jnp.where(i > 0, p_ref[...], 0.0)
        ext = jnp.concatenate([prev, u_ref[...]], axis=0)
        o_ref[...] = _silu(_conv_pre(ext, w_ref[...], tm))

    return _pcall(body, name=name, grid=(CONV_CH // tc, t // tm),
                  in_specs=[pl.BlockSpec((tm, tc), lambda j, i: (i, j)),
                            pl.BlockSpec((8, tc), lambda j, i: (jnp.maximum(i * nb - 1, 0), j)),
                            pl.BlockSpec((8, tc), lambda j, i: (0, j))],
                  out_specs=pl.BlockSpec((tm, tc), lambda j, i: (i, j)),
                  out_shape=jax.ShapeDtypeStruct((t, CONV_CH), F32))(proj, proj, conv_w)


def _conv_bwd(proj, conv_w, dy, *, name):
    t = proj.shape[0]
    tm, tc = _pick(t, 512), 512
    nb = tm // 8
    last = t // tm - 1

    def body(u_ref, p_ref, n_ref, dy_ref, dyn_ref, w_ref, du_ref, dw_ref):
        i = pl.program_id(1)
        w = w_ref[...]
        prev = jnp.where(i > 0, p_ref[...], 0.0)
        ext = jnp.concatenate([prev, u_ref[...], n_ref[...]], axis=0)
        c = _conv_pre(ext, w, tm + 8)
        dy_ext = jnp.concatenate([dy_ref[...], jnp.where(i < last, dyn_ref[...], 0.0)], axis=0)
        dc = dy_ext * _dsilu(c)
        du = w[3:4] * dc[0:tm]
        for j in range(GDN_CONV - 1):
            du = du + w[j:j + 1] * dc[3 - j:3 - j + tm]
        du_ref[...] = du.astype(du_ref.dtype)

        @pl.when(i == 0)
        def _():
            dw_ref[...] = jnp.zeros_like(dw_ref)

        for j in range(GDN_CONV):
            dw_ref[j] += (dc[0:tm] * ext[5 + j:5 + j + tm]).reshape(nb, 8, tc).sum(axis=0)

    cur = lambda j, i: (i, j)
    return _pcall(body, name=name, grid=(CONV_CH // tc, t // tm),
                  in_specs=[pl.BlockSpec((tm, tc), cur),
                            pl.BlockSpec((8, tc), lambda j, i: (jnp.maximum(i * nb - 1, 0), j)),
                            pl.BlockSpec((8, tc), lambda j, i: (jnp.minimum((i + 1) * nb, t // 8 - 1), j)),
                            pl.BlockSpec((tm, tc), cur),
                            pl.BlockSpec((8, tc), lambda j, i: (jnp.minimum((i + 1) * nb, t // 8 - 1), j)),
                            pl.BlockSpec((8, tc), lambda j, i: (0, j))],
                  out_specs=[pl.BlockSpec((tm, tc), cur), pl.BlockSpec((GDN_CONV, 8, tc), lambda j, i: (0, 0, j))],
                  out_shape=[jax.ShapeDtypeStruct((t, CONV_CH), BF16), jax.ShapeDtypeStruct((GDN_CONV, 8, CONV_CH), F32)],
                  )(proj, proj, proj, dy, dy, conv_w)


def _gdn_chunk(q_raw, k_raw, v, misc, params, state):
    nh, c = q_raw.shape[0], q_raw.shape[1]
    lane = lax.broadcasted_iota(jnp.int32, misc.shape, 1)
    prow = lax.broadcasted_iota(jnp.int32, params.shape, 0)
    plane = lax.broadcasted_iota(jnp.int32, params.shape, 1)
    heads = lambda pieces: jnp.concatenate([p[None] for p in pieces], axis=0)
    col = lambda at: heads([jnp.sum(jnp.where(lane == at + h, misc, 0.0), axis=1, keepdims=True) for h in range(nh)])
    par = lambda row: heads([jnp.sum(jnp.where((prow == row) & (plane == h), params, 0.0), keepdims=True)
                             for h in range(nh)])
    b_raw, a_raw = col(LANE_B), col(LANE_A)
    a_log, dt_bias = par(0), par(1)
    beta = _sigmoid(b_raw)
    g = -jnp.exp(a_log) * _softplus(a_raw + dt_bias)

    q = q_raw * lax.rsqrt(jnp.sum(q_raw * q_raw, axis=-1, keepdims=True) + EPS) * (HEAD ** -0.5)
    k = k_raw * lax.rsqrt(jnp.sum(k_raw * k_raw, axis=-1, keepdims=True) + EPS)

    ri = lax.broadcasted_iota(jnp.int32, (c, c), 0)
    ci = lax.broadcasted_iota(jnp.int32, (c, c), 1)
    tril, strict = ri >= ci, ri > ci
    batch = lambda m: jnp.broadcast_to(m.astype(F32), (nh, c, c))
    g_b = jnp.broadcast_to(g, (nh, c, c))
    gc_col = _hdot(batch(tril), g_b, NN3)
    gc_row = _hdot(g_b, batch(ri <= ci), TN3)
    gc = jnp.sum(jnp.where(ci == 0, gc_col, 0.0), axis=2, keepdims=True)
    decay = jnp.exp(jnp.where(tril, gc_col - gc_row, -1e30))

    kb = k * beta
    vb = v * beta
    a_mat = jnp.where(strict, _nt(kb, k) * decay, 0.0)
    x = -a_mat
    inv = (ri == ci).astype(F32) + x
    for _ in range(5):
        x = _hdot(x, x, NN3)
        inv = inv + _hdot(inv, x, NN3)
    u = _hdot(inv, vb, NN3)
    w = _hdot(inv, kb * jnp.exp(gc), NN3)
    intra = _nt(q, k) * decay

    v_new = u - _nn(w, state)
    o = _nn(q * jnp.exp(gc), state) + _nn(intra, v_new)
    g_last = jnp.sum(g, axis=1, keepdims=True)
    k_dec = k * jnp.exp(g_last - gc)
    new_state = state * jnp.exp(g_last) + _tn(k_dec, v_new)
    return o, new_state


def _gdn_specs(nc, rev):
    cidx = (lambda n: nc - 1 - n) if rev else (lambda n: n)
    hb = lambda part: pl.BlockSpec((GDN_CHUNK, GDN_QK), lambda n: (cidx(n), part))
    misc = pl.BlockSpec((GDN_CHUNK, HEAD), lambda n: (cidx(n), COL_MISC // HEAD))
    params = pl.BlockSpec((8, HEAD), lambda n: (0, 0))
    hist = pl.BlockSpec((1, GDN_HEADS, HEAD, HEAD), lambda n: (cidx(n), 0, 0, 0))
    return hb, misc, params, hist


def _split_heads(v):
    return jnp.stack([v[:, h * HEAD:(h + 1) * HEAD] for h in range(v.shape[1] // HEAD)])


def _merge_heads(v):
    return jnp.concatenate([v[h] for h in range(v.shape[0])], axis=1)


def _gdn_fwd(qkv, proj, params, *, name):
    t = qkv.shape[0]
    nc = t // GDN_CHUNK
    hb, misc, pspec, hist = _gdn_specs(nc, False)

    def body(q_ref, k_ref, v_ref, m_ref, p_ref, o_ref, hist_ref, s_ref):
        @pl.when(pl.program_id(0) == 0)
        def _():
            s_ref[...] = jnp.zeros_like(s_ref)

        state = s_ref[...]
        hist_ref[0] = state
        o, new_state = _gdn_chunk(_split_heads(q_ref[...]), _split_heads(k_ref[...]), _split_heads(v_ref[...]),
                                  m_ref[...], p_ref[...], state)
        o_ref[...] = _merge_heads(o)
        s_ref[...] = new_state

    return _pcall(body, name=name, grid=(nc,),
                  in_specs=[hb(0), hb(1), hb(2), misc, pspec],
                  out_specs=[hb(0), hist],
                  out_shape=[jax.ShapeDtypeStruct((t, GDN_QK), F32),
                             jax.ShapeDtypeStruct((nc, GDN_HEADS, HEAD, HEAD), F32)],
                  scratch=[pltpu.VMEM((GDN_HEADS, HEAD, HEAD), F32)])(qkv, qkv, qkv, proj, params)


def _gdn_bwd(qkv, proj, params, hist_arr, do, dmisc_in, *, name, carry=None):
    t = qkv.shape[0]
    nc = t // GDN_CHUNK
    hb, misc, pspec, hist = _gdn_specs(nc, True)
    mrow = pl.BlockSpec((GDN_CHUNK, HEAD), lambda n: (nc - 1 - n, 0))

    def body(q_ref, k_ref, v_ref, m_ref, p_ref, hist_ref, do_ref, dmi_ref,
             dq_ref, dk_ref, dv_ref, dm_ref, dp_ref, ds_ref):
        @pl.when(pl.program_id(0) == 0)
        def _():
            ds_ref[...] = jnp.zeros_like(ds_ref)
            dp_ref[...] = jnp.zeros_like(dp_ref)

        _, vjp = jax.vjp(_gdn_chunk, _split_heads(q_ref[...]), _split_heads(k_ref[...]), _split_heads(v_ref[...]),
                         m_ref[...], p_ref[...], hist_ref[0])
        dq, dk, dv, dm, dp, ds = vjp((_split_heads(do_ref[...]), ds_ref[...]))
        dq_ref[...] = _merge_heads(dq)
        dk_ref[...] = _merge_heads(dk)
        dv_ref[...] = _merge_heads(dv)
        ds_ref[...] = ds
        dm_ref[...] = dmi_ref[...] + dm
        dp_ref[...] += dp

    hd = jax.ShapeDtypeStruct((t, GDN_QK), F32)
    return _pcall(body, name=name, grid=(nc,),
                  in_specs=[hb(0), hb(1), hb(2), misc, pspec, hist, hb(0), mrow],
                  out_specs=[hb(0), hb(0), hb(0), mrow, pspec],
                  out_shape=[hd, hd, hd, jax.ShapeDtypeStruct((t, HEAD), F32), jax.ShapeDtypeStruct((8, HEAD), F32)],
                  scratch=[pltpu.VMEM((GDN_HEADS, HEAD, HEAD), F32)], carry=carry,
                  )(qkv, qkv, qkv, proj, params, hist_arr, do, dmisc_in)


def _gate_fwd(o_raw, proj, w, *, name):
    t = o_raw.shape[0]
    tm = _rows(t, HEAD)
    zb = COL_Z // HEAD

    def body(o_ref, z_ref, w_ref, out_ref):
        ov = o_ref[...]
        r = lax.rsqrt(jnp.mean(ov * ov, axis=-1, keepdims=True) + EPS)
        out_ref[...] = (ov * r * w_ref[...] * _silu(z_ref[...])).astype(out_ref.dtype)

    blk = pl.BlockSpec((tm, HEAD), lambda i, h: (i, h))
    return _pcall(body, name=name, grid=(t // tm, GDN_HEADS),
                  in_specs=[blk, pl.BlockSpec((tm, HEAD), lambda i, h: (i, zb + h)), pl.BlockSpec((1, HEAD), lambda i, h: (0, 0))],
                  out_specs=blk, out_shape=jax.ShapeDtypeStruct((t, GDN_QK), BF16))(o_raw, proj, w)


def _gate_bwd(o_raw, proj, w, dmixed, *, name):
    t = o_raw.shape[0]
    tm = _rows(t, HEAD)
    zb = COL_Z // HEAD

    def body(o_ref, z_ref, w_ref, dy_ref, do_ref, dz_ref, dw_ref):
        ov, zv, dyv = o_ref[...], z_ref[...], dy_ref[...]
        r = lax.rsqrt(jnp.mean(ov * ov, axis=-1, keepdims=True) + EPS)
        xh = ov * r
        dn = dyv * _silu(zv)
        dz_ref[...] = (dyv * xh * w_ref[...] * _dsilu(zv)).astype(dz_ref.dtype)
        dnw = dn * w_ref[...]
        do_ref[...] = r * (dnw - xh * jnp.mean(dnw * xh, axis=-1, keepdims=True))

        @pl.when((pl.program_id(0) == 0) & (pl.program_id(1) == 0))
        def _():
            dw_ref[...] = jnp.zeros_like(dw_ref)

        dw_ref[...] += (dn * xh).reshape(tm // 8, 8, HEAD).sum(axis=0)

    blk = pl.BlockSpec((tm, HEAD), lambda i, h: (i, h))
    return _pcall(body, name=name, grid=(t // tm, GDN_HEADS),
                  in_specs=[blk, pl.BlockSpec((tm, HEAD), lambda i, h: (i, zb + h)), pl.BlockSpec((1, HEAD), lambda i, h: (0, 0)), blk],
                  out_specs=[blk, blk, pl.BlockSpec((8, HEAD), lambda i, h: (0, 0))],
                  out_shape=[jax.ShapeDtypeStruct((t, GDN_QK), F32), jax.ShapeDtypeStruct((t, GDN_QK), BF16),
                             jax.ShapeDtypeStruct((8, HEAD), F32)])(o_raw, proj, w, dmixed)


def _rope_tables():
    half = QK_ROPE // 2
    inv = ROPE_THETA ** (-jnp.arange(half, dtype=F32) / half)
    zeros = jnp.zeros((HEAD - QK_ROPE,), F32)
    inv_row = jnp.concatenate([inv, inv, zeros])
    sign_row = jnp.concatenate([-jnp.ones((half,), F32), jnp.ones((half,), F32), zeros])
    mask_row = jnp.concatenate([jnp.ones((QK_ROPE,), F32), zeros])
    return jnp.concatenate([inv_row[None], sign_row[None], mask_row[None], jnp.zeros((5, HEAD), F32)], axis=0)


def _rotate(x, pos, tab, sign):
    ang = pos * tab[0:1]
    cos = jnp.cos(ang) * tab[2:3]
    sin = jnp.sin(ang) * (tab[1:2] * sign)
    lane = lax.broadcasted_iota(jnp.int32, x.shape, 1)
    half = QK_ROPE // 2
    partner = jnp.where(lane < half, pltpu.roll(x, HEAD - half, axis=1), pltpu.roll(x, half, axis=1))
    return x * cos + partner * sin


def _q_rot(q, pos, tab, *, name, sign, out_dtype=BF16):
    t = q.shape[0]
    tm = _pick(t, 1024)
    scale = (HEAD + QK_ROPE) ** -0.5

    def body(q_ref, pos_ref, tab_ref, o_ref):
        qv = q_ref[...].astype(F32)
        rot = _rotate(qv[:, HEAD:], pos_ref[...], tab_ref[...], sign)
        o_ref[...] = (jnp.concatenate([qv[:, :HEAD], rot], axis=1) * scale).astype(o_ref.dtype)

    blk = pl.BlockSpec((tm, QHEAD), lambda i, h: (i, h))
    return _pcall(body, name=name, grid=(t // tm, MLA_HEADS),
                  in_specs=[blk, pl.BlockSpec((tm, 1), lambda i, h: (i, 0)), pl.BlockSpec((8, HEAD), lambda i, h: (0, 0))],
                  out_specs=blk, out_shape=jax.ShapeDtypeStruct((t, MLA_HEADS * QHEAD), out_dtype))(q, pos, tab)


def _kv_prep(kv, proj, pos, tab, *, name):
    t = kv.shape[0]
    tm = _pick(t, 1024)

    def body(kv_ref, m_ref, pos_ref, tab_ref, k_ref, v_ref):
        kvv = kv_ref[...]
        tab = tab_ref[...]
        rot = _rotate(m_ref[...] * tab[2:3], pos_ref[...], tab, 1.0)
        k_ref[...] = jnp.concatenate([kvv[:, :HEAD], rot], axis=1).astype(k_ref.dtype)
        v_ref[...] = kvv[:, HEAD:].astype(v_ref.dtype)

    return _pcall(body, name=name, grid=(t // tm, MLA_HEADS),
                  in_specs=[pl.BlockSpec((tm, QHEAD), lambda i, h: (i, h)),
                            pl.BlockSpec((tm, HEAD), lambda i, h: (i, COL_MISC // HEAD)),
                            pl.BlockSpec((tm, 1), lambda i, h: (i, 0)), pl.BlockSpec((8, HEAD), lambda i, h: (0, 0))],
                  out_specs=[pl.BlockSpec((tm, QHEAD), lambda i, h: (i, h)), pl.BlockSpec((tm, HEAD), lambda i, h: (i, h))],
                  out_shape=[jax.ShapeDtypeStruct((t, MLA_HEADS * QHEAD), BF16), jax.ShapeDtypeStruct((t, MLA_HEADS * HEAD), BF16)],
                  )(kv, proj, pos, tab)


def _krope_bwd(dkr, pos, tab, *, name):
    t = dkr.shape[0]
    tm = _pick(t, 512)

    def body(d_ref, pos_ref, tab_ref, o_ref):
        d = d_ref[...]
        acc = d[:, :HEAD]
        for h in range(1, MLA_HEADS):
            acc = acc + d[:, h * HEAD:(h + 1) * HEAD]
        o_ref[...] = _rotate(acc, pos_ref[...], tab_ref[...], -1.0)

    return _pcall(body, name=name, grid=(t // tm,),
                  in_specs=[pl.BlockSpec((tm, MLA_HEADS * HEAD), lambda i: (i, 0)), pl.BlockSpec((tm, 1), lambda i: (i, 0)),
                            pl.BlockSpec((8, HEAD), lambda i: (0, 0))],
                  out_specs=pl.BlockSpec((tm, HEAD), lambda i: (i, 0)),
                  out_shape=jax.ShapeDtypeStruct((t, HEAD), F32))(dkr, pos, tab)


NEG = -1e30


def _attn_fwd(q, k, v, *, name, tq=1024, tk=1024, carry=None):
    t = q.shape[0]
    tq, tk = _pick(t, tq), _pick(t, tk)
    nq, nk = t // tq, t // tk
    last_kv = lambda i: (i * tq + tq - 1) // tk

    def body(q_ref, k_ref, v_ref, o_ref, lse_ref, m_ref, l_ref, acc_ref):
        i, j = pl.program_id(1), pl.program_id(2)

        @pl.when(j == 0)
        def _():
            m_ref[...] = jnp.full_like(m_ref, NEG)
            l_ref[...] = jnp.zeros_like(l_ref)
            acc_ref[...] = jnp.zeros_like(acc_ref)

        @pl.when(j <= last_kv(i))
        def _():
            s = lax.dot_general(q_ref[...], k_ref[...], (NT, ((), ())), preferred_element_type=F32)
            qpos = i * tq + lax.broadcasted_iota(jnp.int32, s.shape, 0)
            kpos = j * tk + lax.broadcasted_iota(jnp.int32, s.shape, 1)
            s = jnp.where(kpos <= qpos, s, NEG)
            m_prev = m_ref[...]
            m_new = jnp.maximum(m_prev, jnp.max(s, axis=1, keepdims=True))
            alpha = jnp.exp(m_prev - m_new)
            p = jnp.exp(s - m_new)
            l_ref[...] = alpha * l_ref[...] + jnp.sum(p, axis=1, keepdims=True)
            acc_ref[...] = alpha * acc_ref[...] + lax.dot_general(p.astype(BF16), v_ref[...], (NN, ((), ())),
                                                                  preferred_element_type=F32)
            m_ref[...] = m_new

        @pl.when(j == nk - 1)
        def _():
            o_ref[...] = acc_ref[...] / l_ref[...]
            lse_ref[...] = jnp.broadcast_to(m_ref[...] + jnp.log(l_ref[...]), lse_ref.shape)

    qblk = pl.BlockSpec((tq, QHEAD), lambda h, i, j: (i, h))
    oblk = pl.BlockSpec((tq, HEAD), lambda h, i, j: (i, h))
    return _pcall(body, name=name, grid=(MLA_HEADS, nq, nk),
                  in_specs=[qblk, pl.BlockSpec((tk, QHEAD), lambda h, i, j: (jnp.minimum(j, last_kv(i)), h)),
                            pl.BlockSpec((tk, HEAD), lambda h, i, j: (jnp.minimum(j, last_kv(i)), h))],
                  out_specs=[oblk, oblk],
                  out_shape=[jax.ShapeDtypeStruct((t, MLA_HEADS * HEAD), F32), jax.ShapeDtypeStruct((t, MLA_HEADS * HEAD), F32)],
                  scratch=[pltpu.VMEM((tq, 1), F32), pltpu.VMEM((tq, 1), F32), pltpu.VMEM((tq, HEAD), F32)],
                  carry=carry)(q, k, v)


def _attn_bwd(q, k, v, do, lse, delta, *, name, tq=512, tk=512, carry=None):
    t = q.shape[0]
    tq, tk = _pick(t, tq), _pick(t, tk)
    nq, nk = t // tq, t // tk
    first_q = lambda j: (j * tk) // tq

    def body(q_ref, k_ref, v_ref, do_ref, lse_ref, dl_ref, dq_ref, dkv_ref, dkr_ref, dk_acc, dv_acc):
        j, i = pl.program_id(1), pl.program_id(2)

        @pl.when(i == 0)
        def _():
            dk_acc[...] = jnp.zeros_like(dk_acc)
            dv_acc[...] = jnp.zeros_like(dv_acc)

        @pl.when(i >= first_q(j))
        def _():
            qv, kv_, dov = q_ref[...], k_ref[...], do_ref[...].astype(BF16)
            s = lax.dot_general(qv, kv_, (NT, ((), ())), preferred_element_type=F32)
            qpos = i * tq + lax.broadcasted_iota(jnp.int32, s.shape, 0)
            kpos = j * tk + lax.broadcasted_iota(jnp.int32, s.shape, 1)
            p = jnp.where(kpos <= qpos, jnp.exp(s - lse_ref[...][:, :1]), 0.0)
            pb = p.astype(BF16)
            dv_acc[...] += lax.dot_general(pb, dov, (TN, ((), ())), preferred_element_type=F32)
            dp = lax.dot_general(dov, v_ref[...], (NT, ((), ())), preferred_element_type=F32)
            ds = (p * (dp - dl_ref[...][:, :1])).astype(BF16)
            dk_acc[...] += lax.dot_general(ds, qv, (TN, ((), ())), preferred_element_type=F32)
            contrib = lax.dot_general(ds, kv_, (NN, ((), ())), preferred_element_type=F32)
            rows = pl.ds(pl.multiple_of(i * tq, tq), tq)

            @pl.when(j == 0)
            def _():
                dq_ref[rows, :] = contrib

            @pl.when(j > 0)
            def _():
                dq_ref[rows, :] += contrib

        @pl.when(i == nq - 1)
        def _():
            dk = dk_acc[...]
            dkv_ref[...] = jnp.concatenate([dk[:, :HEAD], dv_acc[...]], axis=1).astype(dkv_ref.dtype)
            dkr_ref[...] = dk[:, HEAD:]

    qi = lambda h, j, i: (jnp.maximum(i, first_q(j)), h)
    kj = lambda h, j, i: (j, h)
    return _pcall(body, name=name, grid=(MLA_HEADS, nk, nq),
                  in_specs=[pl.BlockSpec((tq, QHEAD), qi), pl.BlockSpec((tk, QHEAD), kj), pl.BlockSpec((tk, HEAD), kj),
                            pl.BlockSpec((tq, HEAD), qi), pl.BlockSpec((tq, HEAD), qi), pl.BlockSpec((tq, HEAD), qi)],
                  out_specs=[pl.BlockSpec((t, QHEAD), lambda h, j, i: (0, h)), pl.BlockSpec((tk, QHEAD), kj),
                             pl.BlockSpec((tk, HEAD), kj)],
                  out_shape=[jax.ShapeDtypeStruct((t, MLA_HEADS * QHEAD), F32), jax.ShapeDtypeStruct((t, MLA_HEADS * QHEAD), BF16),
                             jax.ShapeDtypeStruct((t, MLA_HEADS * HEAD), F32)],
                  scratch=[pltpu.VMEM((tk, QHEAD), F32), pltpu.VMEM((tk, HEAD), F32)], carry=carry)(q, k, v, do, lse, delta)


def _swiglu_fwd(gu, *, name):
    t = gu.shape[0]
    tm = _pick(t, 512)

    def body(gu_ref, o_ref):
        v = gu_ref[...]
        o_ref[...] = (_silu(v[:, :FF_BLK]) * v[:, FF_BLK:]).astype(o_ref.dtype)

    return _pcall(body, name=name, grid=(t // tm, D_FF // FF_BLK),
                  in_specs=[pl.BlockSpec((tm, 2 * FF_BLK), lambda i, j: (i, j))],
                  out_specs=pl.BlockSpec((tm, FF_BLK), lambda i, j: (i, j)),
                  out_shape=jax.ShapeDtypeStruct((t, D_FF), BF16))(gu)


def _swiglu_bwd(gu, dact, *, name):
    t = gu.shape[0]
    tm = _pick(t, 512)

    def body(gu_ref, d_ref, o_ref):
        v = gu_ref[...]
        d = d_ref[...].astype(F32)
        gate, up = v[:, :FF_BLK], v[:, FF_BLK:]
        o_ref[...] = jnp.concatenate([d * up * _dsilu(gate), d * _silu(gate)], axis=1).astype(o_ref.dtype)

    return _pcall(body, name=name, grid=(t // tm, D_FF // FF_BLK),
                  in_specs=[pl.BlockSpec((tm, 2 * FF_BLK), lambda i, j: (i, j)), pl.BlockSpec((tm, FF_BLK), lambda i, j: (i, j))],
                  out_specs=pl.BlockSpec((tm, 2 * FF_BLK), lambda i, j: (i, j)),
                  out_shape=jax.ShapeDtypeStruct((t, 2 * D_FF), BF16))(gu, dact)


def _loss_bwd(x2, w, target, *, name):
    t = x2.shape[0]
    tm = _rows(t, D_MODEL)

    def body(x_ref, w_ref, t_ref, dx_ref, dw_ref, l_ref):
        xv, wv = x_ref[...], w_ref[...]
        r = lax.rsqrt(jnp.mean(xv * xv, axis=-1, keepdims=True) + EPS)
        xh = xv * r
        err = xh * wv - t_ref[...]
        dy = err * (1.0 / D_MODEL)
        dyw = dy * wv
        dx_ref[...] = r * (dyw - xh * jnp.mean(dyw * xh, axis=-1, keepdims=True))

        @pl.when(pl.program_id(0) == 0)
        def _():
            dw_ref[...] = jnp.zeros_like(dw_ref)
            l_ref[...] = jnp.zeros_like(l_ref)

        dw_ref[...] += (dy * xh).reshape(tm // 8, 8, D_MODEL).sum(axis=0)
        sq = (err * err).reshape(tm // 8, 8, D_MODEL).sum(axis=0)
        part = sq[:, :HEAD]
        for c in range(1, D_MODEL // HEAD):
            part = part + sq[:, c * HEAD:(c + 1) * HEAD]
        l_ref[...] += part * (0.5 / D_MODEL)

    row = pl.BlockSpec((tm, D_MODEL), lambda i: (i, 0))
    return _pcall(body, name=name, grid=(t // tm,),
                  in_specs=[row, pl.BlockSpec((1, D_MODEL), lambda i: (0, 0)), row],
                  out_specs=[row, pl.BlockSpec((8, D_MODEL), lambda i: (0, 0)), pl.BlockSpec((8, HEAD), lambda i: (0, 0))],
                  out_shape=[jax.ShapeDtypeStruct((t, D_MODEL), F32), jax.ShapeDtypeStruct((8, D_MODEL), F32),
                             jax.ShapeDtypeStruct((8, HEAD), F32)])(x2, w, target)


def _unshard_cols(g):
    return jnp.transpose(g, (1, 0, 2)).reshape(g.shape[1], N_DEV * g.shape[2])


def _shard_cols(w):
    return jnp.transpose(w.reshape(w.shape[0], N_DEV, w.shape[1] // N_DEV), (1, 0, 2))


def _win_to_padded(w):
    pad = jnp.zeros((w.shape[0], PROJ_W - IN_WIDTH), w.dtype)
    return jnp.concatenate([w[:, :4096], w[:, 4112:5136], w[:, 5136:5200], w[:, 4096:4112], pad], axis=1)


def _win_from_padded(d):
    return jnp.concatenate([d[:, :4096], d[:, 5184:5200], d[:, 4096:5120], d[:, 5120:5184]], axis=1)


def _wuq_to_padded(w):
    w3 = w.reshape(w.shape[0], MLA_HEADS, HEAD + QK_ROPE)
    return jnp.pad(w3, ((0, 0), (0, 0), (0, QHEAD - HEAD - QK_ROPE))).reshape(w.shape[0], MLA_HEADS * QHEAD)


def _wuq_from_padded(d):
    return d.reshape(d.shape[0], MLA_HEADS, QHEAD)[:, :, :HEAD + QK_ROPE].reshape(d.shape[0], MLA_HEADS * (HEAD + QK_ROPE))


def _gu_interleave(wg, wu):
    r = wg.shape[0]
    return jnp.stack([wg.reshape(r, D_FF // FF_BLK, FF_BLK), wu.reshape(r, D_FF // FF_BLK, FF_BLK)], axis=2).reshape(r, 2 * D_FF)


def _gu_split(d):
    d4 = d.reshape(d.shape[0], D_FF // FF_BLK, 2, FF_BLK)
    return d4[:, :, 0].reshape(d.shape[0], D_FF), d4[:, :, 1].reshape(d.shape[0], D_FF)


def _late_weights(g_out, g_gate, g_up, g_down):
    return (g_out.reshape(D_MODEL, D_MODEL), _gu_interleave(_unshard_cols(g_gate), _unshard_cols(g_up)),
            g_down.reshape(D_FF, D_MODEL))


def _local_step(x, pos, target, win_p, wuq_p, wukv, late, conv_w, small, exchange):
    tab = _rope_tables()
    if not exchange:
        wout, wgu, wdown = late
    h1 = _rms_fwd(x, small["attn_norm_w"], name="rms1_fwd", width=D_MODEL)
    proj = _mm(h1, win_p, name="mm_in", bn=768)
    qkv = _conv_fwd(proj, conv_w, name="conv_fwd")
    o_gdn_raw, hist = _gdn_fwd(qkv, proj, small["gdn_params"], name="gdn_fwd")
    o_gdn = _gate_fwd(o_gdn_raw, proj, small["gdn_norm_w"], name="gate_fwd")
    cqn = _rms_fwd(proj, small["q_norm_w"], name="rmsq_fwd", width=Q_LORA, col0=COL_CQ)
    ckvn = _rms_fwd(proj, small["kv_norm_w"], name="rmskv_fwd", width=KV_LORA, col0=COL_CKV)
    q_pre = _mm(cqn, wuq_p, name="mm_uq")
    kv = _mm(ckvn, wukv, name="mm_ukv")
    q_full = _q_rot(q_pre, pos, tab, name="q_rot", sign=1.0)
    k_full, v_b = _kv_prep(kv, proj, pos, tab, name="kv_prep")
    if exchange:
        (o_mla_raw, lse), gathered = _attn_fwd(q_full, k_full, v_b, name="attn_fwd", carry=_Exchange(late, [True] * 4))
        wout, wgu, wdown = _late_weights(*gathered)
    else:
        o_mla_raw, lse = _attn_fwd(q_full, k_full, v_b, name="attn_fwd")
    o_mla = _rms_fwd(o_mla_raw, small["mla_out_norm_w"], name="rmso_fwd", width=HEAD, heads=MLA_HEADS)
    mixed = jnp.concatenate([o_gdn, o_mla], axis=1)
    x1 = _mm(mixed, wout, name="mm_out", res=x)
    h2 = _rms_fwd(x1, small["ffn_norm_w"], name="rms2_fwd", width=D_MODEL)
    gu = _mm(h2, wgu, name="mm_gu")
    act = _swiglu_fwd(gu, name="swiglu_fwd")
    x2 = _mm(act, wdown, name="mm_down", res=x1)
    dx2, dw_final, loss_part = _loss_bwd(x2, small["final_norm_w"], target, name="loss_bwd")
    dact = _mm(dx2, wdown, name="mm_down_dx", tb=True, out_dtype=BF16)
    d_wdown = _mm(act, dx2, name="mm_down_dw", ta=True, out_dtype=BF16)
    dgu = _swiglu_bwd(gu, dact, name="swiglu_bwd")
    dh2 = _mm(dgu, wgu, name="mm_gu_dx", tb=True)
    d_wgu = _mm(h2, dgu, name="mm_gu_dw", ta=True, out_dtype=BF16)
    dx1, dw_ffn = _rms_bwd(x1, small["ffn_norm_w"], dh2, name="rms2_bwd", width=D_MODEL, res=dx2)
    dmixed = _mm(dx1, wout, name="mm_out_dx", tb=True)
    d_wout = _mm(mixed, dx1, name="mm_out_dw", ta=True, out_dtype=BF16)
    do_mla, dw_mla_out, delta = _rms_bwd(o_mla_raw, small["mla_out_norm_w"], dmixed, name="rmso_bwd", width=HEAD,
                                         heads=MLA_HEADS, dcol0=GDN_QK, with_delta=True)
    if exchange:
        d_gate, d_up = _gu_split(d_wgu)
        send = [d_wdown.reshape(N_DEV, D_FF // N_DEV, D_MODEL), _shard_cols(d_gate), _shard_cols(d_up)]
        (dq_full, dkv, dkr_h), (r_down, r_gate, r_up) = _attn_bwd(q_full, k_full, v_b, do_mla, lse, delta, name="attn_bwd",
                                                                  carry=_Exchange(send, [False] * 3))
    else:
        dq_full, dkv, dkr_h = _attn_bwd(q_full, k_full, v_b, do_mla, lse, delta, name="attn_bwd")
    dq_pre = _q_rot(dq_full, pos, tab, name="q_rot_bwd", sign=-1.0)
    dmisc_kr = _krope_bwd(dkr_h, pos, tab, name="krope_bwd")
    dcqn = _mm(dq_pre, wuq_p, name="mm_uq_dx", tb=True)
    d_wuq = _mm(cqn, dq_pre, name="mm_uq_dw", ta=True, out_dtype=BF16)
    dckvn = _mm(dkv, wukv, name="mm_ukv_dx", tb=True)
    d_wukv = _mm(ckvn, dkv, name="mm_ukv_dw", ta=True, out_dtype=BF16)
    dcq, dw_qn = _rms_bwd(proj, small["q_norm_w"], dcqn, name="rmsq_bwd", width=Q_LORA, col0=COL_CQ, out_dtype=BF16)
    dckv, dw_kvn = _rms_bwd(proj, small["kv_norm_w"], dckvn, name="rmskv_bwd", width=KV_LORA, col0=COL_CKV, out_dtype=BF16)
    do_gdn, dz, dw_gdn = _gate_bwd(o_gdn_raw, proj, small["gdn_norm_w"], dmixed, name="gate_bwd")
    if exchange:
        send = [d_wout.reshape(N_DEV, D_MODEL // N_DEV, D_MODEL), _shard_cols(_wuq_from_padded(d_wuq)), _shard_cols(d_wukv)]
        (dgq, dgk, dgv, dmisc, d_params), (r_out, r_uq, r_ukv) = _gdn_bwd(
            qkv, proj, small["gdn_params"], hist, do_gdn, dmisc_kr, name="gdn_bwd", carry=_Exchange(send, [False] * 3))
    else:
        dgq, dgk, dgv, dmisc, d_params = _gdn_bwd(qkv, proj, small["gdn_params"], hist, do_gdn, dmisc_kr, name="gdn_bwd")
    dqkv_pre, dconv = _conv_bwd(proj, conv_w, jnp.concatenate([dgq, dgk, dgv], axis=1), name="conv_bwd")
    dproj = jnp.concatenate([dqkv_pre, dz, dcq, dckv, dmisc.astype(BF16), jnp.zeros((x.shape[0], PROJ_W - COL_MISC - HEAD), BF16)], axis=1)
    dh1 = _mm(dproj, win_p, name="mm_in_dx", tb=True, bk=768)
    d_win = _mm(h1, dproj, name="mm_in_dw", ta=True, out_dtype=BF16, bn=768)
    dx, dw_attn = _rms_bwd(x, small["attn_norm_w"], dh1, name="rms1_bwd", width=D_MODEL, res=dx1)

    if exchange:
        big = {"w_in": d_win, "w_uq": r_uq, "w_ukv": r_ukv, "w_out": r_out, "w_gate": r_gate, "w_up": r_up, "w_down": r_down}
    else:
        big = {"w_in": d_win, "w_uq": d_wuq, "w_ukv": d_wukv, "w_out": d_wout, "w_gu": d_wgu, "w_down": d_wdown}
    sm = {"attn_norm_w": dw_attn, "ffn_norm_w": dw_ffn, "final_norm_w": dw_final, "q_norm_w": dw_qn, "kv_norm_w": dw_kvn,
          "gdn_norm_w": dw_gdn, "mla_out_norm_w": dw_mla_out, "gdn_params": d_params, "conv_w": dconv, "loss": loss_part}
    return dx, big, sm


def _exchange(arrays, gather, *, name):
    ex = _Exchange(arrays, gather)

    def body(*refs):
        ins, outs, sems = refs[:ex.n], refs[ex.n:2 * ex.n], refs[2 * ex.n:]
        ex.start(ins, outs, sems)
        ex.wait(ins, outs, sems)

    any_spec = pl.BlockSpec(memory_space=pl.ANY)
    return pl.pallas_call(body, name=name, in_specs=[any_spec] * ex.n, out_specs=[any_spec] * ex.n,
                          out_shape=ex.out_shape(), scratch_shapes=ex.sems())(*arrays)


def _adamw_math(g, w, m, v):
    m = ADAM_B1 * m + (1.0 - ADAM_B1) * g
    v = ADAM_B2 * v + (1.0 - ADAM_B2) * (g * g)
    m_hat = m / (1.0 - ADAM_B1 ** ADAM_STEP)
    v_hat = v / (1.0 - ADAM_B2 ** ADAM_STEP)
    delta = -ADAM_LR * (m_hat / (jnp.sqrt(v_hat) + ADAM_EPS) + ADAM_WD * w)
    return delta, m, v


def _adamw(parts, w, m, v, *, name):
    npart, r, c = parts.shape
    tr = r if r * c * 4 <= (1 << 20) else _rows(r, c, 1 << 20)

    def body(p_ref, w_ref, m_ref, v_ref, g_ref, d_ref, nm_ref, nv_ref):
        g = p_ref[0].astype(F32)
        for s in range(1, npart):
            g = g + p_ref[s].astype(F32)
        g_ref[...] = g
        d_ref[...], nm_ref[...], nv_ref[...] = _adamw_math(g, w_ref[...], m_ref[...], v_ref[...])

    blk = pl.BlockSpec((tr, c), lambda i: (i, 0))
    sds = jax.ShapeDtypeStruct((r, c), F32)
    return _pcall(body, name=name, grid=(r // tr,),
                  in_specs=[pl.BlockSpec((npart, tr, c), lambda i: (0, i, 0)), blk, blk, blk],
                  out_specs=[blk] * 4, out_shape=[sds] * 4)(parts, w, m, v)


def _sum_parts(parts, *, name):
    npart, r, c = parts.shape

    def body(p_ref, o_ref):
        g = p_ref[0]
        for s in range(1, npart):
            g = g + p_ref[s]
        o_ref[...] = g

    return _pcall(body, name=name, grid=(1,), in_specs=[pl.BlockSpec((npart, r, c), lambda i: (0, 0, 0))],
                  out_specs=pl.BlockSpec((r, c), lambda i: (0, 0)), out_shape=jax.ShapeDtypeStruct((r, c), F32))(parts)


_SMALL = (("attn_norm_w", D_MODEL), ("ffn_norm_w", D_MODEL), ("final_norm_w", D_MODEL), ("q_norm_w", Q_LORA),
          ("kv_norm_w", KV_LORA), ("gdn_norm_w", HEAD), ("mla_out_norm_w", HEAD), ("a_log", HEAD), ("dt_bias", HEAD))
_SMALL_ROWS = sum(n for _, n in _SMALL) // HEAD
_CONV_ROWS = GDN_CONV * CONV_CH // HEAD
_PACK_ROWS = 160


def _pad_lanes(v, n):
    v = v.reshape(-1)
    return jnp.concatenate([v, jnp.zeros((n - v.shape[0],), v.dtype)])


def kernel(x, positions, attn_norm_w, w_in, conv_w, a_log, dt_bias, gdn_norm_w, q_norm_w, w_uq, kv_norm_w, w_ukv, mla_out_norm_w, w_out, ffn_norm_w, w_gate, w_up, w_down, final_norm_w, loss_target, m_attn_norm_w, m_w_in, m_conv_w, m_a_log, m_dt_bias, m_gdn_norm_w, m_q_norm_w, m_w_uq, m_kv_norm_w, m_w_ukv, m_mla_out_norm_w, m_w_out, m_ffn_norm_w, m_w_gate, m_w_up, m_w_down, m_final_norm_w, v_attn_norm_w, v_w_in, v_conv_w, v_a_log, v_dt_bias, v_gdn_norm_w, v_q_norm_w, v_w_uq, v_kv_norm_w, v_w_ukv, v_mla_out_norm_w, v_w_out, v_ffn_norm_w, v_w_gate, v_w_up, v_w_down, v_final_norm_w):
    t = x.shape[1]
    me = 4 * lax.axis_index("x") + 2 * lax.axis_index("y") + lax.axis_index("c")
    weights = dict(attn_norm_w=attn_norm_w, w_in=w_in, conv_w=conv_w, a_log=a_log, dt_bias=dt_bias, gdn_norm_w=gdn_norm_w,
                   q_norm_w=q_norm_w, w_uq=w_uq, kv_norm_w=kv_norm_w, w_ukv=w_ukv, mla_out_norm_w=mla_out_norm_w, w_out=w_out,
                   ffn_norm_w=ffn_norm_w, w_gate=w_gate, w_up=w_up, w_down=w_down, final_norm_w=final_norm_w)
    mom_m = dict(attn_norm_w=m_attn_norm_w, w_in=m_w_in, conv_w=m_conv_w, a_log=m_a_log, dt_bias=m_dt_bias, gdn_norm_w=m_gdn_norm_w,
                 q_norm_w=m_q_norm_w, w_uq=m_w_uq, kv_norm_w=m_kv_norm_w, w_ukv=m_w_ukv, mla_out_norm_w=m_mla_out_norm_w,
                 w_out=m_w_out, ffn_norm_w=m_ffn_norm_w, w_gate=m_w_gate, w_up=m_w_up, w_down=m_w_down, final_norm_w=m_final_norm_w)
    mom_v = dict(attn_norm_w=v_attn_norm_w, w_in=v_w_in, conv_w=v_conv_w, a_log=v_a_log, dt_bias=v_dt_bias, gdn_norm_w=v_gdn_norm_w,
                 q_norm_w=v_q_norm_w, w_uq=v_w_uq, kv_norm_w=v_kv_norm_w, w_ukv=v_w_ukv, mla_out_norm_w=v_mla_out_norm_w,
                 w_out=v_w_out, ffn_norm_w=v_ffn_norm_w, w_gate=v_w_gate, w_up=v_w_up, w_down=v_w_down, final_norm_w=v_final_norm_w)
    big_names = ("w_in", "w_uq", "w_ukv", "w_out", "w_gate", "w_up", "w_down")

    shard = {n: weights[n][0].astype(BF16) for n in big_names}
    g_in, g_uq, g_ukv, g_conv = _exchange([shard["w_in"], shard["w_uq"], shard["w_ukv"], weights["conv_w"][0]], [True] * 4,
                                          name="gather_weights")
    win_p = _win_to_padded(_unshard_cols(g_in))
    wuq_p = _wuq_to_padded(_unshard_cols(g_uq))
    wukv = _unshard_cols(g_ukv)
    late = [shard["w_out"], shard["w_gate"], shard["w_up"], shard["w_down"]]
    conv_full = jnp.concatenate([_unshard_cols(g_conv), jnp.zeros((8 - GDN_CONV, CONV_CH), F32)], axis=0)

    gdn_params = jnp.concatenate([_pad_lanes(a_log, HEAD)[None], _pad_lanes(dt_bias, HEAD)[None], jnp.zeros((6, HEAD), F32)], axis=0)
    small = {n: weights[n].reshape(1, -1) for n in ("attn_norm_w", "ffn_norm_w", "final_norm_w", "q_norm_w", "kv_norm_w",
                                                    "gdn_norm_w", "mla_out_norm_w")}
    small["gdn_params"] = gdn_params

    dx, big, sm = _local_step(x[0], positions.reshape(t, 1).astype(F32), loss_target[0], win_p, wuq_p, wukv, late,
                              conv_full, small, True)

    rows8 = lambda name: jnp.sum(sm[name], axis=0)
    pieces = [rows8(n) for n, _ in _SMALL[:7]]
    pieces += [_pad_lanes(jnp.sum(sm["gdn_params"][0:1], axis=0), HEAD), _pad_lanes(jnp.sum(sm["gdn_params"][1:2], axis=0), HEAD)]
    pieces.append(jnp.sum(sm["conv_w"], axis=1).reshape(-1))
    pieces.append(_pad_lanes(jnp.sum(sm["loss"]).reshape(1), HEAD))
    packed = _pad_lanes(jnp.concatenate(pieces), _PACK_ROWS * HEAD).reshape(_PACK_ROWS, HEAD)
    r_in, r_small = _exchange([_shard_cols(_win_from_padded(big["w_in"])), packed], [False, True], name="exchange_grads")
    big["w_in"] = r_in

    outs_g, outs_d, outs_m, outs_v = {}, {}, {}, {}
    for name in big_names:
        g, d, nm, nv = _adamw(big[name], weights[name][0], mom_m[name][0], mom_v[name][0], name="adamw_" + name)
        outs_g[name], outs_d[name], outs_m[name], outs_v[name] = g[None], d[None], nm[None], nv[None]

    total = _sum_parts(r_small, name="sum_small")
    flat = total.reshape(-1)
    loss = flat[(_SMALL_ROWS + _CONV_ROWS) * HEAD]
    g_small, off = {}, 0
    for n, size in _SMALL:
        g_small[n] = flat[off:off + size]
        off += size
    g_conv_full = flat[off:off + GDN_CONV * CONV_CH].reshape(GDN_CONV, CONV_CH)
    g_small["conv_w"] = lax.dynamic_slice(g_conv_full, (0, me * (CONV_CH // N_DEV)), (GDN_CONV, CONV_CH // N_DEV)).reshape(-1)
    order = [n for n, _ in _SMALL] + ["conv_w"]
    sizes = dict(_SMALL)
    sizes["conv_w"] = GDN_CONV * CONV_CH // N_DEV
    true_size = {n: weights[n].size for n in order}

    def pack(d):
        return jnp.concatenate([_pad_lanes(d[n], sizes[n]) for n in order]).reshape(1, -1, HEAD)

    g2, d2, m2, v2 = _adamw(pack(g_small), pack(weights)[0], pack(mom_m)[0], pack(mom_v)[0], name="adamw_small")
    off = 0
    for n in order:
        for src, dst in ((g2, outs_g), (d2, outs_d), (m2, outs_m), (v2, outs_v)):
            dst[n] = src.reshape(-1)[off:off + true_size[n]].reshape(weights[n].shape)
        off += sizes[n]

    names = ("attn_norm_w", "w_in", "conv_w", "a_log", "dt_bias", "gdn_norm_w", "q_norm_w", "w_uq", "kv_norm_w", "w_ukv",
             "mla_out_norm_w", "w_out", "ffn_norm_w", "w_gate", "w_up", "w_down", "final_norm_w")
    return (loss, dx[None], *[outs_g[n] for n in names], *[outs_d[n] for n in names], *[outs_m[n] for n in names],
            *[outs_v[n] for n in names])
```

```python
import functools
import math

import jax
import jax.numpy as jnp
from jax import lax
from jax.experimental import pallas as pl
from jax.experimental.pallas import tpu as pltpu

F32 = jnp.float32
BF16 = jnp.bfloat16

D_MODEL = 2048
GDN_HEADS = 8
HEAD = 128
GDN_CONV = 4
GDN_CHUNK = 64
GDN_QK = GDN_HEADS * HEAD
CONV_CH = 3 * GDN_QK
MLA_HEADS = 8
QK_ROPE = 64
Q_LORA = 512
KV_LORA = 512
ROPE_THETA = 10000.0
D_FF = 5632
EPS = 1e-6
IN_WIDTH = 5200
ADAM_LR, ADAM_B1, ADAM_B2, ADAM_EPS, ADAM_WD, ADAM_STEP = 0.001, 0.9, 0.999, 1e-08, 0.01, 10

PROJ_W = 5376
COL_Z = 3072
COL_CQ = 4096
COL_CKV = 4608
COL_MISC = 5120
LANE_B = 64
LANE_A = 72
QHEAD = 256
FF_BLK = 512
FF_WIDE = D_FF // 4
N_DEV = 8
MESH = pl.DeviceIdType.MESH
VMEM_LIMIT_MB = 48

NN = ((1,), (0,))
NT = ((1,), (1,))
TN = ((0,), (0,))


def _my_place():
    x, y, c = lax.axis_index("x"), lax.axis_index("y"), lax.axis_index("c")
    return x, y, c, 4 * x + 2 * y + c


def _peer(x, y, c, p):
    px, py, pc = x ^ ((p >> 2) & 1), y ^ ((p >> 1) & 1), c ^ (p & 1)
    return (px, py, pc), 4 * px + 2 * py + pc


class _Exchange:
    def __init__(self, arrays, gather):
        self.arrays, self.gather, self.n = list(arrays), list(gather), len(arrays)

    def out_shape(self):
        return [jax.ShapeDtypeStruct(((N_DEV,) + a.shape) if g else a.shape, a.dtype)
                for a, g in zip(self.arrays, self.gather)]

    def sems(self):
        return [pltpu.SemaphoreType.DMA((self.n * (N_DEV - 1),)), pltpu.SemaphoreType.DMA((self.n * (N_DEV - 1),)),
                pltpu.SemaphoreType.DMA((self.n,))]

    def _copies(self, ins, outs, sems):
        send_sems, recv_sems, local_sems = sems
        x, y, c, me = _my_place()
        local = [pltpu.make_async_copy(ins[k] if self.gather[k] else ins[k].at[me], outs[k].at[me], local_sems.at[k])
                 for k in range(self.n)]
        sent, received = [], []
        for p in range(1, N_DEV):
            place, num = _peer(x, y, c, p)
            for k in range(self.n):
                src = ins[k] if self.gather[k] else ins[k].at[num]
                idx = k * (N_DEV - 1) + p - 1
                mk = lambda dst: pltpu.make_async_remote_copy(src_ref=src, dst_ref=dst, send_sem=send_sems.at[idx],
                                                              recv_sem=recv_sems.at[idx], device_id=place, device_id_type=MESH)
                sent.append(mk(outs[k].at[me]))
                received.append(mk(outs[k].at[num]))
        return local, sent, received

    def start(self, ins, outs, sems):
        local, sent, _ = self._copies(ins, outs, sems)
        for cp in local + sent:
            cp.start()

    def wait(self, ins, outs, sems):
        local, sent, received = self._copies(ins, outs, sems)
        for cp in received:
            cp.wait_recv()
        for cp in sent:
            cp.wait_send()
        for cp in local:
            cp.wait()


def _pcall(body, *, name, grid, in_specs, out_specs, out_shape, scratch=(), carry=None):
    params = pltpu.CompilerParams(dimension_semantics=("arbitrary",) * len(grid), vmem_limit_bytes=VMEM_LIMIT_MB << 20)
    if carry is None:
        return pl.pallas_call(body, name=name, grid=grid, in_specs=in_specs, out_specs=out_specs, out_shape=out_shape,
                              scratch_shapes=list(scratch), compiler_params=params)
    single = not isinstance(out_specs, (list, tuple))
    out_specs = [out_specs] if single else list(out_specs)
    out_shape = [out_shape] if single else list(out_shape)
    n_in, n_out, n_scr, na = len(in_specs), len(out_specs), len(scratch), carry.n

    def wrapped(*refs):
        ins, cin = refs[:n_in], refs[n_in:n_in + na]
        outs, cout = refs[n_in + na:n_in + na + n_out], refs[n_in + na + n_out:n_in + 2 * na + n_out]
        scr, sems = refs[n_in + 2 * na + n_out:n_in + 2 * na + n_out + n_scr], refs[n_in + 2 * na + n_out + n_scr:]
        first = functools.reduce(lambda a, b: a & b, [pl.program_id(d) == 0 for d in range(len(grid))])
        last = functools.reduce(lambda a, b: a & b, [pl.program_id(d) == grid[d] - 1 for d in range(len(grid))])

        @pl.when(first)
        def _():
            carry.start(cin, cout, sems)

        body(*ins, *outs, *scr)

        @pl.when(last)
        def _():
            carry.wait(cin, cout, sems)

    any_spec = pl.BlockSpec(memory_space=pl.ANY)
    call = pl.pallas_call(wrapped, name=name, grid=grid, in_specs=list(in_specs) + [any_spec] * na,
                          out_specs=out_specs + [any_spec] * na, out_shape=out_shape + carry.out_shape(),
                          scratch_shapes=list(scratch) + carry.sems(), compiler_params=params)

    def run(*args):
        res = call(*args, *carry.arrays)
        main = res[0] if single else list(res[:n_out])
        return main, list(res[n_out:])

    return run


def _pick(dim, pref):
    if dim <= pref:
        return dim
    c = pref
    while c >= 128:
        if dim % c == 0 and c % 128 == 0:
            return c
        c -= 128
    return dim


def _rows(t, width, target_bytes=2 << 20):
    r = max(8, min(t, target_bytes // (4 * width)))
    r = 1 << (r.bit_length() - 1)
    while t % r:
        r //= 2
    return r


MM_FULL_K = 2048


def _mm(a, b, *, name, ta=False, tb=False, res=None, out_dtype=F32, bm=1024, bn=1024, bk=1024):
    m, k = (a.shape[1], a.shape[0]) if ta else a.shape
    n = b.shape[0] if tb else b.shape[1]
    assert (b.shape[1] if tb else b.shape[0]) == k
    bm, bn, bk = _pick(m, bm), _pick(n, bn), (k if k <= MM_FULL_K else _pick(k, bk))
    nk = k // bk
    dims = (((0,) if ta else (1,), (1,) if tb else (0,)), ((), ()))

    def body(*refs):
        a_ref, b_ref = refs[:2]
        r_ref = refs[2] if res is not None else None
        o_ref = refs[3] if res is not None else refs[2]
        part = lax.dot_general(a_ref[...].astype(BF16), b_ref[...].astype(BF16), dims, preferred_element_type=F32)

        def finish(out):
            if res is not None:
                out = out + r_ref[...]
            o_ref[...] = out.astype(o_ref.dtype)

        if nk == 1:
            finish(part)
            return
        acc_ref = refs[-1]
        kk = pl.program_id(2)

        @pl.when(kk == 0)
        def _():
            acc_ref[...] = part

        @pl.when((kk > 0) & (kk < nk - 1))
        def _():
            acc_ref[...] += part

        @pl.when(kk == nk - 1)
        def _():
            finish(acc_ref[...] + part)

    a_spec = pl.BlockSpec((bk, bm), lambda i, j, kk: (kk, i)) if ta else pl.BlockSpec((bm, bk), lambda i, j, kk: (i, kk))
    b_spec = pl.BlockSpec((bn, bk), lambda i, j, kk: (j, kk)) if tb else pl.BlockSpec((bk, bn), lambda i, j, kk: (kk, j))
    o_spec = pl.BlockSpec((bm, bn), lambda i, j, kk: (i, j))
    ins, specs = [a, b], [a_spec, b_spec]
    if res is not None:
        ins.append(res)
        specs.append(o_spec)
    return _pcall(body, name=name, grid=(m // bm, n // bn, nk), in_specs=specs, out_specs=o_spec,
                  out_shape=jax.ShapeDtypeStruct((m, n), out_dtype),
                  scratch=[pltpu.VMEM((bm, bn), F32)] if nk > 1 else [])(*ins)


def _rms_fwd(x, w, *, name, width, heads=1, col0=0, out_dtype=BF16):
    t = x.shape[0]
    tm = _rows(t, width)
    cb = col0 // width

    def body(x_ref, w_ref, o_ref):
        xv = x_ref[...]
        r = lax.rsqrt(jnp.mean(xv * xv, axis=-1, keepdims=True) + EPS)
        o_ref[...] = (xv * r * w_ref[...]).astype(o_ref.dtype)

    return _pcall(body, name=name, grid=(t // tm, heads),
                  in_specs=[pl.BlockSpec((tm, width), lambda i, h: (i, cb + h)),
                            pl.BlockSpec((1, width), lambda i, h: (0, 0))],
                  out_specs=pl.BlockSpec((tm, width), lambda i, h: (i, h)),
                  out_shape=jax.ShapeDtypeStruct((t, heads * width), out_dtype))(x, w)


def _rms_bwd(x, w, dy, *, name, width, heads=1, col0=0, dcol0=0, res=None, out_dtype=F32, with_delta=False):
    t = x.shape[0]
    tm = _rows(t, width)
    cb, dcb = col0 // width, dcol0 // width

    def body(*refs):
        refs = list(refs)
        x_ref, w_ref, dy_ref = refs[:3]
        r_ref = refs[3] if res is not None else None
        outs = refs[4:] if res is not None else refs[3:]
        dx_ref, dw_ref = outs[:2]
        xv = x_ref[...]
        dyv = dy_ref[...].astype(F32)
        r = lax.rsqrt(jnp.mean(xv * xv, axis=-1, keepdims=True) + EPS)
        xh = xv * r
        dyw = dyv * w_ref[...]
        dx = r * (dyw - xh * jnp.mean(dyw * xh, axis=-1, keepdims=True))
        if with_delta:
            outs[2][...] = jnp.broadcast_to(jnp.sum(dx * xv, axis=-1, keepdims=True), dx.shape)
        if res is not None:
            dx = dx + r_ref[...]
        dx_ref[...] = dx.astype(dx_ref.dtype)

        @pl.when((pl.program_id(0) == 0) & (pl.program_id(1) == 0))
        def _():
            dw_ref[...] = jnp.zeros_like(dw_ref)

        dw_ref[...] += (dyv * xh).reshape(tm // 8, 8, width).sum(axis=0)

    blk = pl.BlockSpec((tm, width), lambda i, h: (i, h))
    ins = [x, w, dy]
    specs = [pl.BlockSpec((tm, width), lambda i, h: (i, cb + h)), pl.BlockSpec((1, width), lambda i, h: (0, 0)),
             pl.BlockSpec((tm, width), lambda i, h: (i, dcb + h))]
    if res is not None:
        ins.append(res)
        specs.append(blk)
    out_shape = [jax.ShapeDtypeStruct((t, heads * width), out_dtype), jax.ShapeDtypeStruct((8, width), F32)]
    out_specs = [blk, pl.BlockSpec((8, width), lambda i, h: (0, 0))]
    if with_delta:
        out_shape.append(jax.ShapeDtypeStruct((t, heads * width), F32))
        out_specs.append(blk)
    return _pcall(body, name=name, grid=(t // tm, heads), in_specs=specs, out_specs=out_specs, out_shape=out_shape)(*ins)


def _sig(x):
    return 1.0 / (1.0 + jnp.exp(-x))


@jax.custom_vjp
def _sigmoid(x):
    return _sig(x)


def _sigmoid_fwd(x):
    s = _sig(x)
    return s, s


def _sigmoid_bwd(s, g):
    return (g * s * (1.0 - s),)


_sigmoid.defvjp(_sigmoid_fwd, _sigmoid_bwd)


@jax.custom_vjp
def _softplus(x):
    return jnp.maximum(x, 0.0) + jnp.log(1.0 + jnp.exp(-jnp.abs(x)))


def _softplus_fwd(x):
    return _softplus(x), x


def _softplus_bwd(x, g):
    return (g * _sig(x),)


_softplus.defvjp(_softplus_fwd, _softplus_bwd)


def _silu(x):
    return x * _sig(x)


def _dsilu(x):
    s = _sig(x)
    return s * (1.0 + x * (1.0 - s))


NN3 = (((2,), (1,)), ((0,), (0,)))
NT3 = (((2,), (2,)), ((0,), (0,)))
TN3 = (((1,), (1,)), ((0,), (0,)))


def _bdot(a, b, dims):
    return lax.dot_general(a.astype(BF16), b.astype(BF16), dims, preferred_element_type=F32)


def _bf16_part(x):
    bits = lax.bitcast_convert_type(x, jnp.uint32) & jnp.uint32(0xFFFF0000)
    return lax.bitcast_convert_type(bits, F32)


def _mask_dot(m, x, dims, mask_first):
    mb = m.astype(BF16)
    hi = _bf16_part(x)
    r1 = x - hi
    mid = _bf16_part(r1)
    lo = r1 - mid
    dot = (lambda p: lax.dot_general(mb, p.astype(BF16), dims, preferred_element_type=F32)) if mask_first else \
          (lambda p: lax.dot_general(p.astype(BF16), mb, dims, preferred_element_type=F32))
    return dot(hi) + (dot(mid) + dot(lo))


def _dot3(a, b, dims):
    a_hi, b_hi = _bf16_part(a), _bf16_part(b)
    a_lo, b_lo = (a - a_hi).astype(BF16), (b - b_hi).astype(BF16)
    a_hi, b_hi = a_hi.astype(BF16), b_hi.astype(BF16)
    dot = lambda x, y: lax.dot_general(x, y, dims, preferred_element_type=F32)
    return dot(a_hi, b_hi) + (dot(a_hi, b_lo) + dot(a_lo, b_hi))


@jax.custom_vjp
def _mask_nn(m, x):
    return _mask_dot(m, x, NN3, True)


_mask_nn.defvjp(lambda m, x: (_mask_dot(m, x, NN3, True), m),
                lambda m, g: (jnp.zeros_like(m), _mask_dot(m, g, TN3, True)))


@jax.custom_vjp
def _mask_tn(x, m):
    return _mask_dot(m, x, TN3, False)


_mask_tn.defvjp(lambda x, m: (_mask_dot(m, x, TN3, False), m),
                lambda m, g: (_mask_dot(m, g, NT3, True), jnp.zeros_like(m)))


@jax.custom_vjp
def _nn_hi(a, b):
    return _dot3(a, b, NN3)


_nn_hi.defvjp(lambda a, b: (_dot3(a, b, NN3), (a, b)), lambda r, g: (_dot3(g, r[1], NT3), _dot3(r[0], g, TN3)))


@jax.custom_vjp
def _nn(a, b):
    return _bdot(a, b, NN3)


_nn.defvjp(lambda a, b: (_bdot(a, b, NN3), (a, b)), lambda r, g: (_bdot(g, r[1], NT3), _bdot(r[0], g, TN3)))


@jax.custom_vjp
def _nt(a, b):
    return _bdot(a, b, NT3)


_nt.defvjp(lambda a, b: (_bdot(a, b, NT3), (a, b)), lambda r, g: (_bdot(g, r[1], NN3), _bdot(g, r[0], TN3)))


@jax.custom_vjp
def _tn(a, b):
    return _bdot(a, b, TN3)


_tn.defvjp(lambda a, b: (_bdot(a, b, TN3), (a, b)), lambda r, g: (_bdot(r[1], g, NT3), _bdot(r[0], g, NN3)))


def _conv_pre(ext, w, rows):
    acc = w[0:1] * ext[5:5 + rows]
    for j in range(1, GDN_CONV):
        acc = acc + w[j:j + 1] * ext[5 + j:5 + j + rows]
    return acc


def _conv_fwd(proj, conv_w, *, name):
    t = proj.shape[0]
    tm, tc = _pick(t, 512), 512
    nb = tm // 8

    def body(u_ref, p_ref, w_ref, o_ref):
        i = pl.program_id(1)
        prev = jnp.where(i > 0, p_ref[...], 0.0)
        ext = jnp.concatenate([prev, u_ref[...]], axis=0)
        o_ref[...] = _silu(_conv_pre(ext, w_ref[...], tm))

    return _pcall(body, name=name, grid=(CONV_CH // tc, t // tm),
                  in_specs=[pl.BlockSpec((tm, tc), lambda j, i: (i, j)),
                            pl.BlockSpec((8, tc), lambda j, i: (jnp.maximum(i * nb - 1, 0), j)),
                            pl.BlockSpec((8, tc), lambda j, i: (0, j))],
                  out_specs=pl.BlockSpec((tm, tc), lambda j, i: (i, j)),
                  out_shape=jax.ShapeDtypeStruct((t, CONV_CH), F32))(proj, proj, conv_w)


def _conv_bwd(proj, conv_w, dy, *, name):
    t = proj.shape[0]
    tm, tc = _pick(t, 512), 512
    nb = tm // 8
    last = t // tm - 1

    def body(u_ref, p_ref, n_ref, dy_ref, dyn_ref, w_ref, du_ref, dw_ref):
        i = pl.program_id(1)
        w = w_ref[...]
        prev = jnp.where(i > 0, p_ref[...], 0.0)
        ext = jnp.concatenate([prev, u_ref[...], n_ref[...]], axis=0)
        c = _conv_pre(ext, w, tm + 8)
        dy_ext = jnp.concatenate([dy_ref[...], jnp.where(i < last, dyn_ref[...], 0.0)], axis=0)
        dc = dy_ext * _dsilu(c)
        du = w[3:4] * dc[0:tm]
        for j in range(GDN_CONV - 1):
            du = du + w[j:j + 1] * dc[3 - j:3 - j + tm]
        du_ref[...] = du.astype(du_ref.dtype)

        @pl.when(i == 0)
        def _():
            dw_ref[...] = jnp.zeros_like(dw_ref)

        for j in range(GDN_CONV):
            dw_ref[j] += (dc[0:tm] * ext[5 + j:5 + j + tm]).reshape(nb, 8, tc).sum(axis=0)

    cur = lambda j, i: (i, j)
    return _pcall(body, name=name, grid=(CONV_CH // tc, t // tm),
                  in_specs=[pl.BlockSpec((tm, tc), cur),
                            pl.BlockSpec((8, tc), lambda j, i: (jnp.maximum(i * nb - 1, 0), j)),
                            pl.BlockSpec((8, tc), lambda j, i: (jnp.minimum((i + 1) * nb, t // 8 - 1), j)),
                            pl.BlockSpec((tm, tc), cur),
                            pl.BlockSpec((8, tc), lambda j, i: (jnp.minimum((i + 1) * nb, t // 8 - 1), j)),
                            pl.BlockSpec((8, tc), lambda j, i: (0, j))],
                  out_specs=[pl.BlockSpec((tm, tc), cur), pl.BlockSpec((GDN_CONV, 8, tc), lambda j, i: (0, 0, j))],
                  out_shape=[jax.ShapeDtypeStruct((t, CONV_CH), BF16), jax.ShapeDtypeStruct((GDN_CONV, 8, CONV_CH), F32)],
                  )(proj, proj, proj, dy, dy, conv_w)


def _gdn_chunk(q_raw, k_raw, v, misc, params, state):
    nh, c = q_raw.shape[0], q_raw.shape[1]
    lane = lax.broadcasted_iota(jnp.int32, misc.shape, 1)
    prow = lax.broadcasted_iota(jnp.int32, params.shape, 0)
    plane = lax.broadcasted_iota(jnp.int32, params.shape, 1)
    heads = lambda pieces: jnp.concatenate([p[None] for p in pieces], axis=0)
    col = lambda at: heads([jnp.sum(jnp.where(lane == at + h, misc, 0.0), axis=1, keepdims=True) for h in range(nh)])
    par = lambda row: heads([jnp.sum(jnp.where((prow == row) & (plane == h), params, 0.0), keepdims=True)
                             for h in range(nh)])
    b_raw, a_raw = col(LANE_B), col(LANE_A)
    a_log, dt_bias = par(0), par(1)
    beta = _sigmoid(b_raw)
    g = -jnp.exp(a_log) * _softplus(a_raw + dt_bias)

    q = q_raw * lax.rsqrt(jnp.sum(q_raw * q_raw, axis=-1, keepdims=True) + EPS) * (HEAD ** -0.5)
    k = k_raw * lax.rsqrt(jnp.sum(k_raw * k_raw, axis=-1, keepdims=True) + EPS)

    ri = lax.broadcasted_iota(jnp.int32, (c, c), 0)
    ci = lax.broadcasted_iota(jnp.int32, (c, c), 1)
    tril, strict = ri >= ci, ri > ci
    batch = lambda m: jnp.broadcast_to(m.astype(F32), (nh, c, c))
    g_b = jnp.broadcast_to(g, (nh, c, c))
    gc_col = _mask_nn(batch(tril), g_b)
    gc_row = _mask_tn(g_b, batch(ri <= ci))
    gc = jnp.sum(jnp.where(ci == 0, gc_col, 0.0), axis=2, keepdims=True)
    decay = jnp.exp(jnp.where(tril, gc_col - gc_row, -1e30))

    kb = k * beta
    vb = v * beta
    a_mat = jnp.where(strict, _nt(kb, k) * decay, 0.0)
    x = -a_mat
    inv = (ri == ci).astype(F32) + x
    for _ in range(5):
        x = _nn_hi(x, x)
        inv = inv + _nn_hi(inv, x)
    u = _nn_hi(inv, vb)
    w = _nn_hi(inv, kb * jnp.exp(gc))
    intra = _nt(q, k) * decay

    v_new = u - _nn(w, state)
    o = _nn(q * jnp.exp(gc), state) + _nn(intra, v_new)
    g_last = jnp.sum(g, axis=1, keepdims=True)
    k_dec = k * jnp.exp(g_last - gc)
    new_state = state * jnp.exp(g_last) + _tn(k_dec, v_new)
    return o, new_state


def _gdn_specs(nc, rev):
    cidx = (lambda n: nc - 1 - n) if rev else (lambda n: n)
    hb = lambda part: pl.BlockSpec((GDN_CHUNK, GDN_QK), lambda n: (cidx(n), part))
    misc = pl.BlockSpec((GDN_CHUNK, HEAD), lambda n: (cidx(n), COL_MISC // HEAD))
    params = pl.BlockSpec((8, HEAD), lambda n: (0, 0))
    hist = pl.BlockSpec((1, GDN_HEADS, HEAD, HEAD), lambda n: (cidx(n), 0, 0, 0))
    return hb, misc, params, hist


def _split_heads(v):
    return jnp.stack([v[:, h * HEAD:(h + 1) * HEAD] for h in range(v.shape[1] // HEAD)])


def _merge_heads(v):
    return jnp.concatenate([v[h] for h in range(v.shape[0])], axis=1)


def _gdn_fwd(qkv, proj, params, *, name):
    t = qkv.shape[0]
    nc = t // GDN_CHUNK
    hb, misc, pspec, hist = _gdn_specs(nc, False)

    def body(q_ref, k_ref, v_ref, m_ref, p_ref, o_ref, hist_ref, s_ref):
        @pl.when(pl.program_id(0) == 0)
        def _():
            s_ref[...] = jnp.zeros_like(s_ref)

        state = s_ref[...]
        hist_ref[0] = state
        o, new_state = _gdn_chunk(_split_heads(q_ref[...]), _split_heads(k_ref[...]), _split_heads(v_ref[...]),
                                  m_ref[...], p_ref[...], state)
        o_ref[...] = _merge_heads(o)
        s_ref[...] = new_state

    return _pcall(body, name=name, grid=(nc,),
                  in_specs=[hb(0), hb(1), hb(2), misc, pspec],
                  out_specs=[hb(0), hist],
                  out_shape=[jax.ShapeDtypeStruct((t, GDN_QK), F32),
                             jax.ShapeDtypeStruct((nc, GDN_HEADS, HEAD, HEAD), F32)],
                  scratch=[pltpu.VMEM((GDN_HEADS, HEAD, HEAD), F32)])(qkv, qkv, qkv, proj, params)


def _gdn_bwd(qkv, proj, params, hist_arr, do, dmisc_in, *, name, carry=None):
    t = qkv.shape[0]
    nc = t // GDN_CHUNK
    hb, misc, pspec, hist = _gdn_specs(nc, True)
    mrow = pl.BlockSpec((GDN_CHUNK, HEAD), lambda n: (nc - 1 - n, 0))

    def body(q_ref, k_ref, v_ref, m_ref, p_ref, hist_ref, do_ref, dmi_ref,
             dq_ref, dk_ref, dv_ref, dm_ref, dp_ref, ds_ref):
        @pl.when(pl.program_id(0) == 0)
        def _():
            ds_ref[...] = jnp.zeros_like(ds_ref)
            dp_ref[...] = jnp.zeros_like(dp_ref)

        _, vjp = jax.vjp(_gdn_chunk, _split_heads(q_ref[...]), _split_heads(k_ref[...]), _split_heads(v_ref[...]),
                         m_ref[...], p_ref[...], hist_ref[0])
        dq, dk, dv, dm, dp, ds = vjp((_split_heads(do_ref[...]), ds_ref[...]))
        dq_ref[...] = _merge_heads(dq)
        dk_ref[...] = _merge_heads(dk)
        dv_ref[...] = _merge_heads(dv)
        ds_ref[...] = ds
        dm_ref[...] = dmi_ref[...] + dm
        dp_ref[...] += dp

    hd = jax.ShapeDtypeStruct((t, GDN_QK), F32)
    return _pcall(body, name=name, grid=(nc,),
                  in_specs=[hb(0), hb(1), hb(2), misc, pspec, hist, hb(0), mrow],
                  out_specs=[hb(0), hb(0), hb(0), mrow, pspec],
                  out_shape=[hd, hd, hd, jax.ShapeDtypeStruct((t, HEAD), F32), jax.ShapeDtypeStruct((8, HEAD), F32)],
                  scratch=[pltpu.VMEM((GDN_HEADS, HEAD, HEAD), F32)], carry=carry,
                  )(qkv, qkv, qkv, proj, params, hist_arr, do, dmisc_in)


def _gate_fwd(o_raw, proj, w, *, name):
    t = o_raw.shape[0]
    tm = _rows(t, HEAD)
    zb = COL_Z // HEAD

    def body(o_ref, z_ref, w_ref, out_ref):
        ov = o_ref[...]
        r = lax.rsqrt(jnp.mean(ov * ov, axis=-1, keepdims=True) + EPS)
        out_ref[...] = (ov * r * w_ref[...] * _silu(z_ref[...])).astype(out_ref.dtype)

    blk = pl.BlockSpec((tm, HEAD), lambda i, h: (i, h))
    return _pcall(body, name=name, grid=(t // tm, GDN_HEADS),
                  in_specs=[blk, pl.BlockSpec((tm, HEAD), lambda i, h: (i, zb + h)), pl.BlockSpec((1, HEAD), lambda i, h: (0, 0))],
                  out_specs=blk, out_shape=jax.ShapeDtypeStruct((t, GDN_QK), BF16))(o_raw, proj, w)


def _gate_bwd(o_raw, proj, w, dmixed, *, name):
    t = o_raw.shape[0]
    tm = _rows(t, HEAD)
    zb = COL_Z // HEAD

    def body(o_ref, z_ref, w_ref, dy_ref, do_ref, dz_ref, dw_ref):
        ov, zv, dyv = o_ref[...], z_ref[...], dy_ref[...]
        r = lax.rsqrt(jnp.mean(ov * ov, axis=-1, keepdims=True) + EPS)
        xh = ov * r
        dn = dyv * _silu(zv)
        dz_ref[...] = (dyv * xh * w_ref[...] * _dsilu(zv)).astype(dz_ref.dtype)
        dnw = dn * w_ref[...]
        do_ref[...] = r * (dnw - xh * jnp.mean(dnw * xh, axis=-1, keepdims=True))

        @pl.when((pl.program_id(0) == 0) & (pl.program_id(1) == 0))
        def _():
            dw_ref[...] = jnp.zeros_like(dw_ref)

        dw_ref[...] += (dn * xh).reshape(tm // 8, 8, HEAD).sum(axis=0)

    blk = pl.BlockSpec((tm, HEAD), lambda i, h: (i, h))
    return _pcall(body, name=name, grid=(t // tm, GDN_HEADS),
                  in_specs=[blk, pl.BlockSpec((tm, HEAD), lambda i, h: (i, zb + h)), pl.BlockSpec((1, HEAD), lambda i, h: (0, 0)), blk],
                  out_specs=[blk, blk, pl.BlockSpec((8, HEAD), lambda i, h: (0, 0))],
                  out_shape=[jax.ShapeDtypeStruct((t, GDN_QK), F32), jax.ShapeDtypeStruct((t, GDN_QK), BF16),
                             jax.ShapeDtypeStruct((8, HEAD), F32)])(o_raw, proj, w, dmixed)


def _rope_tables():
    half = QK_ROPE // 2
    inv = ROPE_THETA ** (-jnp.arange(half, dtype=F32) / half)
    zeros = jnp.zeros((HEAD - QK_ROPE,), F32)
    inv_row = jnp.concatenate([inv, inv, zeros])
    sign_row = jnp.concatenate([-jnp.ones((half,), F32), jnp.ones((half,), F32), zeros])
    mask_row = jnp.concatenate([jnp.ones((QK_ROPE,), F32), zeros])
    return jnp.concatenate([inv_row[None], sign_row[None], mask_row[None], jnp.zeros((5, HEAD), F32)], axis=0)


def _rotate(x, pos, tab, sign):
    ang = pos * tab[0:1]
    cos = jnp.cos(ang) * tab[2:3]
    sin = jnp.sin(ang) * (tab[1:2] * sign)
    lane = lax.broadcasted_iota(jnp.int32, x.shape, 1)
    half = QK_ROPE // 2
    partner = jnp.where(lane < half, pltpu.roll(x, HEAD - half, axis=1), pltpu.roll(x, half, axis=1))
    return x * cos + partner * sin


def _q_rot(q, pos, tab, *, name, sign, out_dtype=BF16):
    t = q.shape[0]
    tm = _pick(t, 1024)
    scale = (HEAD + QK_ROPE) ** -0.5

    def body(q_ref, pos_ref, tab_ref, o_ref):
        qv = q_ref[...].astype(F32)
        rot = _rotate(qv[:, HEAD:], pos_ref[...], tab_ref[...], sign)
        o_ref[...] = (jnp.concatenate([qv[:, :HEAD], rot], axis=1) * scale).astype(o_ref.dtype)

    blk = pl.BlockSpec((tm, QHEAD), lambda i, h: (i, h))
    return _pcall(body, name=name, grid=(t // tm, MLA_HEADS),
                  in_specs=[blk, pl.BlockSpec((tm, 1), lambda i, h: (i, 0)), pl.BlockSpec((8, HEAD), lambda i, h: (0, 0))],
                  out_specs=blk, out_shape=jax.ShapeDtypeStruct((t, MLA_HEADS * QHEAD), out_dtype))(q, pos, tab)


def _kv_prep(kv, proj, pos, tab, *, name):
    t = kv.shape[0]
    tm = _pick(t, 1024)

    def body(kv_ref, m_ref, pos_ref, tab_ref, k_ref, v_ref):
        kvv = kv_ref[...]
        tab = tab_ref[...]
        rot = _rotate(m_ref[...] * tab[2:3], pos_ref[...], tab, 1.0)
        k_ref[...] = jnp.concatenate([kvv[:, :HEAD], rot], axis=1).astype(k_ref.dtype)
        v_ref[...] = kvv[:, HEAD:].astype(v_ref.dtype)

    return _pcall(body, name=name, grid=(t // tm, MLA_HEADS),
                  in_specs=[pl.BlockSpec((tm, QHEAD), lambda i, h: (i, h)),
                            pl.BlockSpec((tm, HEAD), lambda i, h: (i, COL_MISC // HEAD)),
                            pl.BlockSpec((tm, 1), lambda i, h: (i, 0)), pl.BlockSpec((8, HEAD), lambda i, h: (0, 0))],
                  out_specs=[pl.BlockSpec((tm, QHEAD), lambda i, h: (i, h)), pl.BlockSpec((tm, HEAD), lambda i, h: (i, h))],
                  out_shape=[jax.ShapeDtypeStruct((t, MLA_HEADS * QHEAD), BF16), jax.ShapeDtypeStruct((t, MLA_HEADS * HEAD), BF16)],
                  )(kv, proj, pos, tab)


def _krope_bwd(dkr, pos, tab, *, name):
    t = dkr.shape[0]
    tm = _pick(t, 512)

    def body(d_ref, pos_ref, tab_ref, o_ref):
        d = d_ref[...]
        acc = d[:, :HEAD]
        for h in range(1, MLA_HEADS):
            acc = acc + d[:, h * HEAD:(h + 1) * HEAD]
        o_ref[...] = _rotate(acc, pos_ref[...], tab_ref[...], -1.0)

    return _pcall(body, name=name, grid=(t // tm,),
                  in_specs=[pl.BlockSpec((tm, MLA_HEADS * HEAD), lambda i: (i, 0)), pl.BlockSpec((tm, 1), lambda i: (i, 0)),
                            pl.BlockSpec((8, HEAD), lambda i: (0, 0))],
                  out_specs=pl.BlockSpec((tm, HEAD), lambda i: (i, 0)),
                  out_shape=jax.ShapeDtypeStruct((t, HEAD), F32))(dkr, pos, tab)


NEG = -1e30
ATTN_STRIP = 32


def _attn_fwd(q, k, v, *, name, tq=1024, tk=1024, carry=None):
    t = q.shape[0]
    tq, tk = _pick(t, tq), _pick(t, tk)
    nq, nk = t // tq, t // tk
    last_kv = lambda i: (i * tq + tq - 1) // tk

    def body(q_ref, k_ref, v_ref, o_ref, lse_ref, m_ref, l_ref, acc_ref):
        i, j = pl.program_id(1), pl.program_id(2)

        @pl.when(j == 0)
        def _():
            m_ref[...] = jnp.full_like(m_ref, NEG)
            l_ref[...] = jnp.zeros_like(l_ref)
            acc_ref[...] = jnp.zeros_like(acc_ref)

        def step(masked):
            s = lax.dot_general(q_ref[...], k_ref[...], (NT, ((), ())), preferred_element_type=F32)
            if masked:
                qpos = i * tq + lax.broadcasted_iota(jnp.int32, s.shape, 0)
                kpos = j * tk + lax.broadcasted_iota(jnp.int32, s.shape, 1)
                s = jnp.where(kpos <= qpos, s, NEG)
            m_prev = m_ref[...]
            m_new = jnp.maximum(m_prev, jnp.max(s, axis=1, keepdims=True))
            alpha = jnp.exp(m_prev - m_new)
            p = jnp.exp(s - m_new)
            l_ref[...] = alpha * l_ref[...] + jnp.sum(p, axis=1, keepdims=True)
            acc_ref[...] = alpha * acc_ref[...] + lax.dot_general(p.astype(BF16), v_ref[...], (NN, ((), ())),
                                                                  preferred_element_type=F32)
            m_ref[...] = m_new

        crosses = j * tk + tk - 1 > i * tq

        @pl.when((j <= last_kv(i)) & crosses)
        def _():
            step(True)

        @pl.when((j <= last_kv(i)) & jnp.logical_not(crosses))
        def _():
            step(False)

        @pl.when(j == nk - 1)
        def _():
            o_ref[...] = acc_ref[...] / l_ref[...]
            lse_ref[...] = jnp.broadcast_to(m_ref[...] + jnp.log(l_ref[...]), lse_ref.shape)

    qblk = pl.BlockSpec((tq, QHEAD), lambda h, i, j: (i, h))
    oblk = pl.BlockSpec((tq, HEAD), lambda h, i, j: (i, h))
    return _pcall(body, name=name, grid=(MLA_HEADS, nq, nk),
                  in_specs=[qblk, pl.BlockSpec((tk, QHEAD), lambda h, i, j: (jnp.minimum(j, last_kv(i)), h)),
                            pl.BlockSpec((tk, HEAD), lambda h, i, j: (jnp.minimum(j, last_kv(i)), h))],
                  out_specs=[oblk, oblk],
                  out_shape=[jax.ShapeDtypeStruct((t, MLA_HEADS * HEAD), F32), jax.ShapeDtypeStruct((t, MLA_HEADS * HEAD), F32)],
                  scratch=[pltpu.VMEM((tq, 1), F32), pltpu.VMEM((tq, 1), F32), pltpu.VMEM((tq, HEAD), F32)],
                  carry=carry)(q, k, v)


def _attn_bwd(q, k, v, do, lse, delta, *, name, tq=512, tk=512, carry=None):
    t = q.shape[0]
    tq, tk = _pick(t, tq), _pick(t, tk)
    nq, nk = t // tq, t // tk
    first_q = lambda j: (j * tk) // tq

    strip = min(ATTN_STRIP, tq)
    lanes = lambda col: jnp.tile(col, (1, tk // HEAD))

    def body(q_ref, k_ref, v_ref, do_ref, lse_ref, dl_ref, dq_ref, dkv_ref, dkr_ref, dk_acc, dv_acc,
             s_ref, dp_ref, p_ref, ds_ref):
        j, i = pl.program_id(1), pl.program_id(2)

        @pl.when(i == 0)
        def _():
            dk_acc[...] = jnp.zeros_like(dk_acc)
            dv_acc[...] = jnp.zeros_like(dv_acc)

        def step(masked):
            qv, kv_, dov = q_ref[...], k_ref[...], do_ref[...].astype(BF16)
            s_ref[...] = lax.dot_general(qv, kv_, (NT, ((), ())), preferred_element_type=F32)
            dp_ref[...] = lax.dot_general(dov, v_ref[...], (NT, ((), ())), preferred_element_type=F32)

            def one(r, _):
                rows = pl.ds(pl.multiple_of(r * strip, strip), strip)
                p = jnp.exp(s_ref[rows, :] - lanes(lse_ref[rows, :]))
                if masked:
                    qpos = i * tq + r * strip + lax.broadcasted_iota(jnp.int32, p.shape, 0)
                    kpos = j * tk + lax.broadcasted_iota(jnp.int32, p.shape, 1)
                    p = jnp.where(kpos <= qpos, p, 0.0)
                p_ref[rows, :] = p.astype(BF16)
                ds_ref[rows, :] = (p * (dp_ref[rows, :] - lanes(dl_ref[rows, :]))).astype(BF16)
                return 0

            lax.fori_loop(0, tq // strip, one, 0)
            ds = ds_ref[...]
            dv_acc[...] += lax.dot_general(p_ref[...], dov, (TN, ((), ())), preferred_element_type=F32)
            dk_acc[...] += lax.dot_general(ds, qv, (TN, ((), ())), preferred_element_type=F32)
            contrib = lax.dot_general(ds, kv_, (NN, ((), ())), preferred_element_type=F32)
            rows = pl.ds(pl.multiple_of(i * tq, tq), tq)

            @pl.when(j == 0)
            def _():
                dq_ref[rows, :] = contrib

            @pl.when(j > 0)
            def _():
                dq_ref[rows, :] += contrib

        crosses = j * tk + tk - 1 > i * tq

        @pl.when((i >= first_q(j)) & crosses)
        def _():
            step(True)

        @pl.when((i >= first_q(j)) & jnp.logical_not(crosses))
        def _():
            step(False)

        @pl.when(i == nq - 1)
        def _():
            dk = dk_acc[...]
            dkv_ref[...] = jnp.concatenate([dk[:, :HEAD], dv_acc[...]], axis=1).astype(dkv_ref.dtype)
            dkr_ref[...] = dk[:, HEAD:]

    qi = lambda h, j, i: (jnp.maximum(i, first_q(j)), h)
    kj = lambda h, j, i: (j, h)
    return _pcall(body, name=name, grid=(MLA_HEADS, nk, nq),
                  in_specs=[pl.BlockSpec((tq, QHEAD), qi), pl.BlockSpec((tk, QHEAD), kj), pl.BlockSpec((tk, HEAD), kj),
                            pl.BlockSpec((tq, HEAD), qi), pl.BlockSpec((tq, HEAD), qi), pl.BlockSpec((tq, HEAD), qi)],
                  out_specs=[pl.BlockSpec((t, QHEAD), lambda h, j, i: (0, h)), pl.BlockSpec((tk, QHEAD), kj),
                             pl.BlockSpec((tk, HEAD), kj)],
                  out_shape=[jax.ShapeDtypeStruct((t, MLA_HEADS * QHEAD), F32), jax.ShapeDtypeStruct((t, MLA_HEADS * QHEAD), BF16),
                             jax.ShapeDtypeStruct((t, MLA_HEADS * HEAD), F32)],
                  scratch=[pltpu.VMEM((tk, QHEAD), F32), pltpu.VMEM((tk, HEAD), F32), pltpu.VMEM((tq, tk), F32),
                           pltpu.VMEM((tq, tk), F32), pltpu.VMEM((tq, tk), BF16), pltpu.VMEM((tq, tk), BF16)],
                  carry=carry)(q, k, v, do, lse, delta)


def _swiglu_fwd(gate, up, *, name):
    t = gate.shape[0]
    tm = _pick(t, 512)

    def body(g_ref, u_ref, o_ref):
        o_ref[...] = (_silu(g_ref[...]) * u_ref[...]).astype(o_ref.dtype)

    blk = pl.BlockSpec((tm, FF_BLK), lambda i, j: (i, j))
    return _pcall(body, name=name, grid=(t // tm, D_FF // FF_BLK), in_specs=[blk, blk], out_specs=blk,
                  out_shape=jax.ShapeDtypeStruct((t, D_FF), BF16))(gate, up)


def _swiglu_bwd(gate, up, dact, *, name):
    t = gate.shape[0]
    tm = _pick(t, 512)

    def body(g_ref, u_ref, d_ref, dg_ref, du_ref):
        g = g_ref[...]
        d = d_ref[...].astype(F32)
        dg_ref[...] = (d * u_ref[...] * _dsilu(g)).astype(dg_ref.dtype)
        du_ref[...] = (d * _silu(g)).astype(du_ref.dtype)

    blk = pl.BlockSpec((tm, FF_BLK), lambda i, j: (i, j))
    sds = jax.ShapeDtypeStruct((t, D_FF), BF16)
    return _pcall(body, name=name, grid=(t // tm, D_FF // FF_BLK), in_specs=[blk, blk, blk], out_specs=[blk, blk],
                  out_shape=[sds, sds])(gate, up, dact)


def _loss_bwd(x2, w, target, *, name):
    t = x2.shape[0]
    tm = _rows(t, D_MODEL)

    def body(x_ref, w_ref, t_ref, dx_ref, dw_ref, l_ref):
        xv, wv = x_ref[...], w_ref[...]
        r = lax.rsqrt(jnp.mean(xv * xv, axis=-1, keepdims=True) + EPS)
        xh = xv * r
        err = xh * wv - t_ref[...]
        dy = err * (1.0 / D_MODEL)
        dyw = dy * wv
        dx_ref[...] = r * (dyw - xh * jnp.mean(dyw * xh, axis=-1, keepdims=True))

        @pl.when(pl.program_id(0) == 0)
        def _():
            dw_ref[...] = jnp.zeros_like(dw_ref)
            l_ref[...] = jnp.zeros_like(l_ref)

        dw_ref[...] += (dy * xh).reshape(tm // 8, 8, D_MODEL).sum(axis=0)
        sq = (err * err).reshape(tm // 8, 8, D_MODEL).sum(axis=0)
        part = sq[:, :HEAD]
        for c in range(1, D_MODEL // HEAD):
            part = part + sq[:, c * HEAD:(c + 1) * HEAD]
        l_ref[...] += part * (0.5 / D_MODEL)

    row = pl.BlockSpec((tm, D_MODEL), lambda i: (i, 0))
    return _pcall(body, name=name, grid=(t // tm,),
                  in_specs=[row, pl.BlockSpec((1, D_MODEL), lambda i: (0, 0)), row],
                  out_specs=[row, pl.BlockSpec((8, D_MODEL), lambda i: (0, 0)), pl.BlockSpec((8, HEAD), lambda i: (0, 0))],
                  out_shape=[jax.ShapeDtypeStruct((t, D_MODEL), F32), jax.ShapeDtypeStruct((8, D_MODEL), F32),
                             jax.ShapeDtypeStruct((8, HEAD), F32)])(x2, w, target)


def _unshard_cols(g):
    return jnp.transpose(g, (1, 0, 2)).reshape(g.shape[1], N_DEV * g.shape[2])


def _shard_cols(w):
    return jnp.transpose(w.reshape(w.shape[0], N_DEV, w.shape[1] // N_DEV), (1, 0, 2))


def _win_to_padded(w):
    pad = jnp.zeros((w.shape[0], PROJ_W - IN_WIDTH), w.dtype)
    return jnp.concatenate([w[:, :4096], w[:, 4112:5136], w[:, 5136:5200], w[:, 4096:4112], pad], axis=1)


def _win_from_padded(d):
    return jnp.concatenate([d[:, :4096], d[:, 5184:5200], d[:, 4096:5120], d[:, 5120:5184]], axis=1)


def _wuq_to_padded(w):
    w3 = w.reshape(w.shape[0], MLA_HEADS, HEAD + QK_ROPE)
    return jnp.pad(w3, ((0, 0), (0, 0), (0, QHEAD - HEAD - QK_ROPE))).reshape(w.shape[0], MLA_HEADS * QHEAD)


def _wuq_from_padded(d):
    return d.reshape(d.shape[0], MLA_HEADS, QHEAD)[:, :, :HEAD + QK_ROPE].reshape(d.shape[0], MLA_HEADS * (HEAD + QK_ROPE))


def _late_weights(g_out, g_gate, g_up, g_down):
    return g_out.reshape(D_MODEL, D_MODEL), _unshard_cols(g_gate), _unshard_cols(g_up), g_down.reshape(D_FF, D_MODEL)


def _local_step(x, pos, target, win_p, wuq_p, wukv, late, conv_w, small, exchange):
    tab = _rope_tables()
    if not exchange:
        wout, wgate, wup, wdown = late
    h1 = _rms_fwd(x, small["attn_norm_w"], name="rms1_fwd", width=D_MODEL)
    proj = _mm(h1, win_p, name="mm_in", bn=768)
    qkv = _conv_fwd(proj, conv_w, name="conv_fwd")
    o_gdn_raw, hist = _gdn_fwd(qkv, proj, small["gdn_params"], name="gdn_fwd")
    o_gdn = _gate_fwd(o_gdn_raw, proj, small["gdn_norm_w"], name="gate_fwd")
    cqn = _rms_fwd(proj, small["q_norm_w"], name="rmsq_fwd", width=Q_LORA, col0=COL_CQ)
    ckvn = _rms_fwd(proj, small["kv_norm_w"], name="rmskv_fwd", width=KV_LORA, col0=COL_CKV)
    q_pre = _mm(cqn, wuq_p, name="mm_uq")
    kv = _mm(ckvn, wukv, name="mm_ukv")
    q_full = _q_rot(q_pre, pos, tab, name="q_rot", sign=1.0)
    k_full, v_b = _kv_prep(kv, proj, pos, tab, name="kv_prep")
    if exchange:
        (o_mla_raw, lse), gathered = _attn_fwd(q_full, k_full, v_b, name="attn_fwd", carry=_Exchange(late, [True] * 4))
        wout, wgate, wup, wdown = _late_weights(*gathered)
    else:
        o_mla_raw, lse = _attn_fwd(q_full, k_full, v_b, name="attn_fwd")
    o_mla = _rms_fwd(o_mla_raw, small["mla_out_norm_w"], name="rmso_fwd", width=HEAD, heads=MLA_HEADS)
    mixed = jnp.concatenate([o_gdn, o_mla], axis=1)
    x1 = _mm(mixed, wout, name="mm_out", res=x)
    h2 = _rms_fwd(x1, small["ffn_norm_w"], name="rms2_fwd", width=D_MODEL)
    gate = _mm(h2, wgate, name="mm_gate", bn=FF_WIDE)
    up = _mm(h2, wup, name="mm_up", bn=FF_WIDE)
    act = _swiglu_fwd(gate, up, name="swiglu_fwd")
    x2 = _mm(act, wdown, name="mm_down", res=x1, bk=FF_WIDE)
    dx2, dw_final, loss_part = _loss_bwd(x2, small["final_norm_w"], target, name="loss_bwd")
    dact = _mm(dx2, wdown, name="mm_down_dx", tb=True, out_dtype=BF16, bn=FF_WIDE)
    d_wdown = _mm(act, dx2, name="mm_down_dw", ta=True, out_dtype=BF16, bm=FF_WIDE)
    dgate, dup = _swiglu_bwd(gate, up, dact, name="swiglu_bwd")
    dh2 = _mm(dgate, wgate, name="mm_gate_dx", tb=True, bk=FF_WIDE)
    dh2 = _mm(dup, wup, name="mm_up_dx", tb=True, res=dh2, bk=FF_WIDE)
    d_wgate = _mm(h2, dgate, name="mm_gate_dw", ta=True, out_dtype=BF16, bn=FF_WIDE)
    d_wup = _mm(h2, dup, name="mm_up_dw", ta=True, out_dtype=BF16, bn=FF_WIDE)
    dx1, dw_ffn = _rms_bwd(x1, small["ffn_norm_w"], dh2, name="rms2_bwd", width=D_MODEL, res=dx2)
    dmixed = _mm(dx1, wout, name="mm_out_dx", tb=True)
    d_wout = _mm(mixed, dx1, name="mm_out_dw", ta=True, out_dtype=BF16)
    do_mla, dw_mla_out, delta = _rms_bwd(o_mla_raw, small["mla_out_norm_w"], dmixed, name="rmso_bwd", width=HEAD,
                                         heads=MLA_HEADS, dcol0=GDN_QK, with_delta=True)
    if exchange:
        send = [d_wdown.reshape(N_DEV, D_FF // N_DEV, D_MODEL), _shard_cols(d_wgate), _shard_cols(d_wup)]
        (dq_full, dkv, dkr_h), (r_down, r_gate, r_up) = _attn_bwd(q_full, k_full, v_b, do_mla, lse, delta, name="attn_bwd",
                                                                  carry=_Exchange(send, [False] * 3))
    else:
        dq_full, dkv, dkr_h = _attn_bwd(q_full, k_full, v_b, do_mla, lse, delta, name="attn_bwd")
    dq_pre = _q_rot(dq_full, pos, tab, name="q_rot_bwd", sign=-1.0)
    dmisc_kr = _krope_bwd(dkr_h, pos, tab, name="krope_bwd")
    dcqn = _mm(dq_pre, wuq_p, name="mm_uq_dx", tb=True)
    d_wuq = _mm(cqn, dq_pre, name="mm_uq_dw", ta=True, out_dtype=BF16)
    dckvn = _mm(dkv, wukv, name="mm_ukv_dx", tb=True)
    d_wukv = _mm(ckvn, dkv, name="mm_ukv_dw", ta=True, out_dtype=BF16)
    dcq, dw_qn = _rms_bwd(proj, small["q_norm_w"], dcqn, name="rmsq_bwd", width=Q_LORA, col0=COL_CQ, out_dtype=BF16)
    dckv, dw_kvn = _rms_bwd(proj, small["kv_norm_w"], dckvn, name="rmskv_bwd", width=KV_LORA, col0=COL_CKV, out_dtype=BF16)
    do_gdn, dz, dw_gdn = _gate_bwd(o_gdn_raw, proj, small["gdn_norm_w"], dmixed, name="gate_bwd")
    if exchange:
        send = [d_wout.reshape(N_DEV, D_MODEL // N_DEV, D_MODEL), _shard_cols(_wuq_from_padded(d_wuq)), _shard_cols(d_wukv)]
        (dgq, dgk, dgv, dmisc, d_params), (r_out, r_uq, r_ukv) = _gdn_bwd(
            qkv, proj, small["gdn_params"], hist, do_gdn, dmisc_kr, name="gdn_bwd", carry=_Exchange(send, [False] * 3))
    else:
        dgq, dgk, dgv, dmisc, d_params = _gdn_bwd(qkv, proj, small["gdn_params"], hist, do_gdn, dmisc_kr, name="gdn_bwd")
    dqkv_pre, dconv = _conv_bwd(proj, conv_w, jnp.concatenate([dgq, dgk, dgv], axis=1), name="conv_bwd")
    dproj = jnp.concatenate([dqkv_pre, dz, dcq, dckv, dmisc.astype(BF16), jnp.zeros((x.shape[0], PROJ_W - COL_MISC - HEAD), BF16)], axis=1)
    dh1 = _mm(dproj, win_p, name="mm_in_dx", tb=True, bk=768)
    d_win = _mm(h1, dproj, name="mm_in_dw", ta=True, out_dtype=BF16, bn=768)
    dx, dw_attn = _rms_bwd(x, small["attn_norm_w"], dh1, name="rms1_bwd", width=D_MODEL, res=dx1)

    if exchange:
        big = {"w_in": d_win, "w_uq": r_uq, "w_ukv": r_ukv, "w_out": r_out, "w_gate": r_gate, "w_up": r_up, "w_down": r_down}
    else:
        big = {"w_in": d_win, "w_uq": d_wuq, "w_ukv": d_wukv, "w_out": d_wout, "w_gate": d_wgate, "w_up": d_wup,
               "w_down": d_wdown}
    sm = {"attn_norm_w": dw_attn, "ffn_norm_w": dw_ffn, "final_norm_w": dw_final, "q_norm_w": dw_qn, "kv_norm_w": dw_kvn,
          "gdn_norm_w": dw_gdn, "mla_out_norm_w": dw_mla_out, "gdn_params": d_params, "conv_w": dconv, "loss": loss_part}
    return dx, big, sm


def _exchange(arrays, gather, *, name):
    ex = _Exchange(arrays, gather)

    def body(*refs):
        ins, outs, sems = refs[:ex.n], refs[ex.n:2 * ex.n], refs[2 * ex.n:]
        ex.start(ins, outs, sems)
        ex.wait(ins, outs, sems)

    any_spec = pl.BlockSpec(memory_space=pl.ANY)
    return pl.pallas_call(body, name=name, in_specs=[any_spec] * ex.n, out_specs=[any_spec] * ex.n,
                          out_shape=ex.out_shape(), scratch_shapes=ex.sems())(*arrays)


def _adamw_math(g, w, m, v):
    m = ADAM_B1 * m + (1.0 - ADAM_B1) * g
    v = ADAM_B2 * v + (1.0 - ADAM_B2) * (g * g)
    m_hat = m / (1.0 - ADAM_B1 ** ADAM_STEP)
    v_hat = v / (1.0 - ADAM_B2 ** ADAM_STEP)
    delta = -ADAM_LR * (m_hat / (jnp.sqrt(v_hat) + ADAM_EPS) + ADAM_WD * w)
    return delta, m, v


def _adamw(parts, w, m, v, *, name):
    npart, r, c = parts.shape
    tr = r if r * c * 4 <= (1 << 20) else _rows(r, c, 1 << 20)

    def body(p_ref, w_ref, m_ref, v_ref, g_ref, d_ref, nm_ref, nv_ref):
        g = p_ref[0].astype(F32)
        for s in range(1, npart):
            g = g + p_ref[s].astype(F32)
        g_ref[...] = g
        d_ref[...], nm_ref[...], nv_ref[...] = _adamw_math(g, w_ref[...], m_ref[...], v_ref[...])

    blk = pl.BlockSpec((tr, c), lambda i: (i, 0))
    sds = jax.ShapeDtypeStruct((r, c), F32)
    return _pcall(body, name=name, grid=(r // tr,),
                  in_specs=[pl.BlockSpec((npart, tr, c), lambda i: (0, i, 0)), blk, blk, blk],
                  out_specs=[blk] * 4, out_shape=[sds] * 4)(parts, w, m, v)


def _sum_parts(parts, *, name):
    npart, r, c = parts.shape

    def body(p_ref, o_ref):
        g = p_ref[0]
        for s in range(1, npart):
            g = g + p_ref[s]
        o_ref[...] = g

    return _pcall(body, name=name, grid=(1,), in_specs=[pl.BlockSpec((npart, r, c), lambda i: (0, 0, 0))],
                  out_specs=pl.BlockSpec((r, c), lambda i: (0, 0)), out_shape=jax.ShapeDtypeStruct((r, c), F32))(parts)


_SMALL = (("attn_norm_w", D_MODEL), ("ffn_norm_w", D_MODEL), ("final_norm_w", D_MODEL), ("q_norm_w", Q_LORA),
          ("kv_norm_w", KV_LORA), ("gdn_norm_w", HEAD), ("mla_out_norm_w", HEAD), ("a_log", HEAD), ("dt_bias", HEAD))
_SMALL_ROWS = sum(n for _, n in _SMALL) // HEAD
_CONV_ROWS = GDN_CONV * CONV_CH // HEAD
_PACK_ROWS = 160


def _pad_lanes(v, n):
    v = v.reshape(-1)
    return jnp.concatenate([v, jnp.zeros((n - v.shape[0],), v.dtype)])


def kernel(x, positions, attn_norm_w, w_in, conv_w, a_log, dt_bias, gdn_norm_w, q_norm_w, w_uq, kv_norm_w, w_ukv, mla_out_norm_w, w_out, ffn_norm_w, w_gate, w_up, w_down, final_norm_w, loss_target, m_attn_norm_w, m_w_in, m_conv_w, m_a_log, m_dt_bias, m_gdn_norm_w, m_q_norm_w, m_w_uq, m_kv_norm_w, m_w_ukv, m_mla_out_norm_w, m_w_out, m_ffn_norm_w, m_w_gate, m_w_up, m_w_down, m_final_norm_w, v_attn_norm_w, v_w_in, v_conv_w, v_a_log, v_dt_bias, v_gdn_norm_w, v_q_norm_w, v_w_uq, v_kv_norm_w, v_w_ukv, v_mla_out_norm_w, v_w_out, v_ffn_norm_w, v_w_gate, v_w_up, v_w_down, v_final_norm_w):
    t = x.shape[1]
    me = 4 * lax.axis_index("x") + 2 * lax.axis_index("y") + lax.axis_index("c")
    weights = dict(attn_norm_w=attn_norm_w, w_in=w_in, conv_w=conv_w, a_log=a_log, dt_bias=dt_bias, gdn_norm_w=gdn_norm_w,
                   q_norm_w=q_norm_w, w_uq=w_uq, kv_norm_w=kv_norm_w, w_ukv=w_ukv, mla_out_norm_w=mla_out_norm_w, w_out=w_out,
                   ffn_norm_w=ffn_norm_w, w_gate=w_gate, w_up=w_up, w_down=w_down, final_norm_w=final_norm_w)
    mom_m = dict(attn_norm_w=m_attn_norm_w, w_in=m_w_in, conv_w=m_conv_w, a_log=m_a_log, dt_bias=m_dt_bias, gdn_norm_w=m_gdn_norm_w,
                 q_norm_w=m_q_norm_w, w_uq=m_w_uq, kv_norm_w=m_kv_norm_w, w_ukv=m_w_ukv, mla_out_norm_w=m_mla_out_norm_w,
                 w_out=m_w_out, ffn_norm_w=m_ffn_norm_w, w_gate=m_w_gate, w_up=m_w_up, w_down=m_w_down, final_norm_w=m_final_norm_w)
    mom_v = dict(attn_norm_w=v_attn_norm_w, w_in=v_w_in, conv_w=v_conv_w, a_log=v_a_log, dt_bias=v_dt_bias, gdn_norm_w=v_gdn_norm_w,
                 q_norm_w=v_q_norm_w, w_uq=v_w_uq, kv_norm_w=v_kv_norm_w, w_ukv=v_w_ukv, mla_out_norm_w=v_mla_out_norm_w,
                 w_out=v_w_out, ffn_norm_w=v_ffn_norm_w, w_gate=v_w_gate, w_up=v_w_up, w_down=v_w_down, final_norm_w=v_final_norm_w)
    big_names = ("w_in", "w_uq", "w_ukv", "w_out", "w_gate", "w_up", "w_down")

    shard = {n: weights[n][0].astype(BF16) for n in big_names}
    g_in, g_uq, g_ukv, g_conv = _exchange([shard["w_in"], shard["w_uq"], shard["w_ukv"], weights["conv_w"][0]], [True] * 4,
                                          name="gather_weights")
    win_p = _win_to_padded(_unshard_cols(g_in))
    wuq_p = _wuq_to_padded(_unshard_cols(g_uq))
    wukv = _unshard_cols(g_ukv)
    late = [shard["w_out"], shard["w_gate"], shard["w_up"], shard["w_down"]]
    conv_full = jnp.concatenate([_unshard_cols(g_conv), jnp.zeros((8 - GDN_CONV, CONV_CH), F32)], axis=0)

    gdn_params = jnp.concatenate([_pad_lanes(a_log, HEAD)[None], _pad_lanes(dt_bias, HEAD)[None], jnp.zeros((6, HEAD), F32)], axis=0)
    small = {n: weights[n].reshape(1, -1) for n in ("attn_norm_w", "ffn_norm_w", "final_norm_w", "q_norm_w", "kv_norm_w",
                                                    "gdn_norm_w", "mla_out_norm_w")}
    small["gdn_params"] = gdn_params

    dx, big, sm = _local_step(x[0], positions.reshape(t, 1).astype(F32), loss_target[0], win_p, wuq_p, wukv, late,
                              conv_full, small, True)

    rows8 = lambda name: jnp.sum(sm[name], axis=0)
    pieces = [rows8(n) for n, _ in _SMALL[:7]]
    pieces += [_pad_lanes(jnp.sum(sm["gdn_params"][0:1], axis=0), HEAD), _pad_lanes(jnp.sum(sm["gdn_params"][1:2], axis=0), HEAD)]
    pieces.append(jnp.sum(sm["conv_w"], axis=1).reshape(-1))
    pieces.append(_pad_lanes(jnp.sum(sm["loss"]).reshape(1), HEAD))
    packed = _pad_lanes(jnp.concatenate(pieces), _PACK_ROWS * HEAD).reshape(_PACK_ROWS, HEAD)
    r_in, r_small = _exchange([_shard_cols(_win_from_padded(big["w_in"])), packed], [False, True], name="exchange_grads")
    big["w_in"] = r_in

    outs_g, outs_d, outs_m, outs_v = {}, {}, {}, {}
    for name in big_names:
        g, d, nm, nv = _adamw(big[name], weights[name][0], mom_m[name][0], mom_v[name][0], name="adamw_" + name)
        outs_g[name], outs_d[name], outs_m[name], outs_v[name] = g[None], d[None], nm[None], nv[None]

    total = _sum_parts(r_small, name="sum_small")
    flat = total.reshape(-1)
    loss = flat[(_SMALL_ROWS + _CONV_ROWS) * HEAD]
    g_small, off = {}, 0
    for n, size in _SMALL:
        g_small[n] = flat[off:off + size]
        off += size
    g_conv_full = flat[off:off + GDN_CONV * CONV_CH].reshape(GDN_CONV, CONV_CH)
    g_small["conv_w"] = lax.dynamic_slice(g_conv_full, (0, me * (CONV_CH // N_DEV)), (GDN_CONV, CONV_CH // N_DEV)).reshape(-1)
    order = [n for n, _ in _SMALL] + ["conv_w"]
    sizes = dict(_SMALL)
    sizes["conv_w"] = GDN_CONV * CONV_CH // N_DEV
    true_size = {n: weights[n].size for n in order}

    def pack(d):
        return jnp.concatenate([_pad_lanes(d[n], sizes[n]) for n in order]).reshape(1, -1, HEAD)

    g2, d2, m2, v2 = _adamw(pack(g_small), pack(weights)[0], pack(mom_m)[0], pack(mom_v)[0], name="adamw_small")
    off = 0
    for n in order:
        for src, dst in ((g2, outs_g), (d2, outs_d), (m2, outs_m), (v2, outs_v)):
            dst[n] = src.reshape(-1)[off:off + true_size[n]].reshape(weights[n].shape)
        off += sizes[n]

    names = ("attn_norm_w", "w_in", "conv_w", "a_log", "dt_bias", "gdn_norm_w", "q_norm_w", "w_uq", "kv_norm_w", "w_ukv",
             "mla_out_norm_w", "w_out", "ffn_norm_w", "w_gate", "w_up", "w_down", "final_norm_w")
    return (loss, dx[None], *[outs_g[n] for n in names], *[outs_d[n] for n in names], *[outs_m[n] for n in names],
            *[outs_v[n] for n in names])
```

```python
import functools
import math

import jax
import jax.numpy as jnp
from jax import lax
from jax.experimental import pallas as pl
from jax.experimental.pallas import tpu as pltpu

F32 = jnp.float32
BF16 = jnp.bfloat16

D_MODEL = 2048
GDN_HEADS = 8
HEAD = 128
GDN_CONV = 4
GDN_CHUNK = 64
GDN_QK = GDN_HEADS * HEAD
CONV_CH = 3 * GDN_QK
MLA_HEADS = 8
QK_ROPE = 64
Q_LORA = 512
KV_LORA = 512
ROPE_THETA = 10000.0
D_FF = 5632
EPS = 1e-6
IN_WIDTH = 5200
ADAM_LR, ADAM_B1, ADAM_B2, ADAM_EPS, ADAM_WD, ADAM_STEP = 0.001, 0.9, 0.999, 1e-08, 0.01, 10

PROJ_W = 5376
COL_Z = 3072
COL_CQ = 4096
COL_CKV = 4608
COL_MISC = 5120
LANE_B = 64
LANE_A = 72
QHEAD = 256
FF_WIDE = D_FF // 4
N_DEV = 8
MESH = pl.DeviceIdType.MESH
VMEM_LIMIT_MB = 48

NN = ((1,), (0,))
NT = ((1,), (1,))
TN = ((0,), (0,))


def _my_place():
    x, y, c = lax.axis_index("x"), lax.axis_index("y"), lax.axis_index("c")
    return x, y, c, 4 * x + 2 * y + c


def _peer(x, y, c, p):
    px, py, pc = x ^ ((p >> 2) & 1), y ^ ((p >> 1) & 1), c ^ (p & 1)
    return (px, py, pc), 4 * px + 2 * py + pc


class _Exchange:
    def __init__(self, arrays, gather):
        self.arrays, self.gather, self.n = list(arrays), list(gather), len(arrays)

    def out_shape(self):
        return [jax.ShapeDtypeStruct(((N_DEV,) + a.shape) if g else a.shape, a.dtype)
                for a, g in zip(self.arrays, self.gather)]

    def sems(self):
        return [pltpu.SemaphoreType.DMA((self.n * (N_DEV - 1),)), pltpu.SemaphoreType.DMA((self.n * (N_DEV - 1),)),
                pltpu.SemaphoreType.DMA((self.n,))]

    def _copies(self, ins, outs, sems):
        send_sems, recv_sems, local_sems = sems
        x, y, c, me = _my_place()
        local = [pltpu.make_async_copy(ins[k] if self.gather[k] else ins[k].at[me], outs[k].at[me], local_sems.at[k])
                 for k in range(self.n)]
        sent, received = [], []
        for p in range(1, N_DEV):
            place, num = _peer(x, y, c, p)
            for k in range(self.n):
                src = ins[k] if self.gather[k] else ins[k].at[num]
                idx = k * (N_DEV - 1) + p - 1
                mk = lambda dst: pltpu.make_async_remote_copy(src_ref=src, dst_ref=dst, send_sem=send_sems.at[idx],
                                                              recv_sem=recv_sems.at[idx], device_id=place, device_id_type=MESH)
                sent.append(mk(outs[k].at[me]))
                received.append(mk(outs[k].at[num]))
        return local, sent, received

    def start(self, ins, outs, sems):
        local, sent, _ = self._copies(ins, outs, sems)
        for cp in local + sent:
            cp.start()

    def wait(self, ins, outs, sems):
        local, sent, received = self._copies(ins, outs, sems)
        for cp in received:
            cp.wait_recv()
        for cp in sent:
            cp.wait_send()
        for cp in local:
            cp.wait()


def _pcall(body, *, name, grid, in_specs, out_specs, out_shape, scratch=(), carry=None):
    params = pltpu.CompilerParams(dimension_semantics=("arbitrary",) * len(grid), vmem_limit_bytes=VMEM_LIMIT_MB << 20)
    if carry is None:
        return pl.pallas_call(body, name=name, grid=grid, in_specs=in_specs, out_specs=out_specs, out_shape=out_shape,
                              scratch_shapes=list(scratch), compiler_params=params)
    single = not isinstance(out_specs, (list, tuple))
    out_specs = [out_specs] if single else list(out_specs)
    out_shape = [out_shape] if single else list(out_shape)
    n_in, n_out, n_scr, na = len(in_specs), len(out_specs), len(scratch), carry.n

    def wrapped(*refs):
        ins, cin = refs[:n_in], refs[n_in:n_in + na]
        outs, cout = refs[n_in + na:n_in + na + n_out], refs[n_in + na + n_out:n_in + 2 * na + n_out]
        scr, sems = refs[n_in + 2 * na + n_out:n_in + 2 * na + n_out + n_scr], refs[n_in + 2 * na + n_out + n_scr:]
        first = functools.reduce(lambda a, b: a & b, [pl.program_id(d) == 0 for d in range(len(grid))])
        last = functools.reduce(lambda a, b: a & b, [pl.program_id(d) == grid[d] - 1 for d in range(len(grid))])

        @pl.when(first)
        def _():
            carry.start(cin, cout, sems)

        body(*ins, *outs, *scr)

        @pl.when(last)
        def _():
            carry.wait(cin, cout, sems)

    any_spec = pl.BlockSpec(memory_space=pl.ANY)
    call = pl.pallas_call(wrapped, name=name, grid=grid, in_specs=list(in_specs) + [any_spec] * na,
                          out_specs=out_specs + [any_spec] * na, out_shape=out_shape + carry.out_shape(),
                          scratch_shapes=list(scratch) + carry.sems(), compiler_params=params)

    def run(*args):
        res = call(*args, *carry.arrays)
        main = res[0] if single else list(res[:n_out])
        return main, list(res[n_out:])

    return run


def _pick(dim, pref):
    if dim <= pref:
        return dim
    c = pref
    while c >= 128:
        if dim % c == 0 and c % 128 == 0:
            return c
        c -= 128
    return dim


def _rows(t, width, target_bytes=2 << 20):
    r = max(8, min(t, target_bytes // (4 * width)))
    r = 1 << (r.bit_length() - 1)
    while t % r:
        r //= 2
    return r


MM_FULL_K = 2048


def _mm(a, b, *, name, ta=False, tb=False, res=None, out_dtype=F32, bm=1024, bn=1024, bk=1024, carry=None):
    m, k = (a.shape[1], a.shape[0]) if ta else a.shape
    n = b.shape[0] if tb else b.shape[1]
    assert (b.shape[1] if tb else b.shape[0]) == k
    bm, bn, bk = _pick(m, bm), _pick(n, bn), (k if k <= MM_FULL_K else _pick(k, bk))
    nk = k // bk
    dims = (((0,) if ta else (1,), (1,) if tb else (0,)), ((), ()))

    def body(*refs):
        a_ref, b_ref = refs[:2]
        r_ref = refs[2] if res is not None else None
        o_ref = refs[3] if res is not None else refs[2]
        part = lax.dot_general(a_ref[...].astype(BF16), b_ref[...].astype(BF16), dims, preferred_element_type=F32)

        def finish(out):
            if res is not None:
                out = out + r_ref[...]
            o_ref[...] = out.astype(o_ref.dtype)

        if nk == 1:
            finish(part)
            return
        acc_ref = refs[-1]
        kk = pl.program_id(2)

        @pl.when(kk == 0)
        def _():
            acc_ref[...] = part

        @pl.when((kk > 0) & (kk < nk - 1))
        def _():
            acc_ref[...] += part

        @pl.when(kk == nk - 1)
        def _():
            finish(acc_ref[...] + part)

    a_spec = pl.BlockSpec((bk, bm), lambda i, j, kk: (kk, i)) if ta else pl.BlockSpec((bm, bk), lambda i, j, kk: (i, kk))
    b_spec = pl.BlockSpec((bn, bk), lambda i, j, kk: (j, kk)) if tb else pl.BlockSpec((bk, bn), lambda i, j, kk: (kk, j))
    o_spec = pl.BlockSpec((bm, bn), lambda i, j, kk: (i, j))
    ins, specs = [a, b], [a_spec, b_spec]
    if res is not None:
        ins.append(res)
        specs.append(o_spec)
    return _pcall(body, name=name, grid=(m // bm, n // bn, nk), in_specs=specs, out_specs=o_spec,
                  out_shape=jax.ShapeDtypeStruct((m, n), out_dtype),
                  scratch=[pltpu.VMEM((bm, bn), F32)] if nk > 1 else [], carry=carry)(*ins)


def _rms_fwd(x, w, *, name, width, heads=1, col0=0, out_dtype=BF16):
    t = x.shape[0]
    tm = _rows(t, width)
    cb = col0 // width

    def body(x_ref, w_ref, o_ref):
        xv = x_ref[...]
        r = lax.rsqrt(jnp.mean(xv * xv, axis=-1, keepdims=True) + EPS)
        o_ref[...] = (xv * r * w_ref[...]).astype(o_ref.dtype)

    return _pcall(body, name=name, grid=(t // tm, heads),
                  in_specs=[pl.BlockSpec((tm, width), lambda i, h: (i, cb + h)),
                            pl.BlockSpec((1, width), lambda i, h: (0, 0))],
                  out_specs=pl.BlockSpec((tm, width), lambda i, h: (i, h)),
                  out_shape=jax.ShapeDtypeStruct((t, heads * width), out_dtype))(x, w)


def _rms_bwd(x, w, dy, *, name, width, heads=1, col0=0, dcol0=0, res=None, out_dtype=F32, with_delta=False):
    t = x.shape[0]
    tm = _rows(t, width)
    cb, dcb = col0 // width, dcol0 // width

    def body(*refs):
        refs = list(refs)
        x_ref, w_ref, dy_ref = refs[:3]
        r_ref = refs[3] if res is not None else None
        outs = refs[4:] if res is not None else refs[3:]
        dx_ref, dw_ref = outs[:2]
        xv = x_ref[...]
        dyv = dy_ref[...].astype(F32)
        r = lax.rsqrt(jnp.mean(xv * xv, axis=-1, keepdims=True) + EPS)
        xh = xv * r
        dyw = dyv * w_ref[...]
        dx = r * (dyw - xh * jnp.mean(dyw * xh, axis=-1, keepdims=True))
        if with_delta:
            outs[2][...] = jnp.broadcast_to(jnp.sum(dx * xv, axis=-1, keepdims=True), dx.shape)
        if res is not None:
            dx = dx + r_ref[...]
        dx_ref[...] = dx.astype(dx_ref.dtype)

        @pl.when((pl.program_id(0) == 0) & (pl.program_id(1) == 0))
        def _():
            dw_ref[...] = jnp.zeros_like(dw_ref)

        dw_ref[...] += (dyv * xh).reshape(tm // 8, 8, width).sum(axis=0)

    blk = pl.BlockSpec((tm, width), lambda i, h: (i, h))
    ins = [x, w, dy]
    specs = [pl.BlockSpec((tm, width), lambda i, h: (i, cb + h)), pl.BlockSpec((1, width), lambda i, h: (0, 0)),
             pl.BlockSpec((tm, width), lambda i, h: (i, dcb + h))]
    if res is not None:
        ins.append(res)
        specs.append(blk)
    out_shape = [jax.ShapeDtypeStruct((t, heads * width), out_dtype), jax.ShapeDtypeStruct((8, width), F32)]
    out_specs = [blk, pl.BlockSpec((8, width), lambda i, h: (0, 0))]
    if with_delta:
        out_shape.append(jax.ShapeDtypeStruct((t, heads * width), F32))
        out_specs.append(blk)
    return _pcall(body, name=name, grid=(t // tm, heads), in_specs=specs, out_specs=out_specs, out_shape=out_shape)(*ins)


def _sig(x):
    return 1.0 / (1.0 + jnp.exp(-x))


@jax.custom_vjp
def _sigmoid(x):
    return _sig(x)


def _sigmoid_fwd(x):
    s = _sig(x)
    return s, s


def _sigmoid_bwd(s, g):
    return (g * s * (1.0 - s),)


_sigmoid.defvjp(_sigmoid_fwd, _sigmoid_bwd)


@jax.custom_vjp
def _softplus(x):
    return jnp.maximum(x, 0.0) + jnp.log(1.0 + jnp.exp(-jnp.abs(x)))


def _softplus_fwd(x):
    return _softplus(x), x


def _softplus_bwd(x, g):
    return (g * _sig(x),)


_softplus.defvjp(_softplus_fwd, _softplus_bwd)


def _silu(x):
    return x * _sig(x)


def _dsilu(x):
    s = _sig(x)
    return s * (1.0 + x * (1.0 - s))


NN3 = (((2,), (1,)), ((0,), (0,)))
NT3 = (((2,), (2,)), ((0,), (0,)))
TN3 = (((1,), (1,)), ((0,), (0,)))


def _bdot(a, b, dims):
    return lax.dot_general(a.astype(BF16), b.astype(BF16), dims, preferred_element_type=F32)


def _bf16_part(x):
    bits = lax.bitcast_convert_type(x, jnp.uint32) & jnp.uint32(0xFFFF0000)
    return lax.bitcast_convert_type(bits, F32)


def _mask_dot(m, x, dims, mask_first):
    mb = m.astype(BF16)
    hi = _bf16_part(x)
    r1 = x - hi
    mid = _bf16_part(r1)
    lo = r1 - mid
    dot = (lambda p: lax.dot_general(mb, p.astype(BF16), dims, preferred_element_type=F32)) if mask_first else \
          (lambda p: lax.dot_general(p.astype(BF16), mb, dims, preferred_element_type=F32))
    return dot(hi) + (dot(mid) + dot(lo))


def _dot3(a, b, dims):
    a_hi, b_hi = _bf16_part(a), _bf16_part(b)
    a_lo, b_lo = (a - a_hi).astype(BF16), (b - b_hi).astype(BF16)
    a_hi, b_hi = a_hi.astype(BF16), b_hi.astype(BF16)
    dot = lambda x, y: lax.dot_general(x, y, dims, preferred_element_type=F32)
    return dot(a_hi, b_hi) + (dot(a_hi, b_lo) + dot(a_lo, b_hi))


@jax.custom_vjp
def _mask_nn(m, x):
    return _mask_dot(m, x, NN3, True)


_mask_nn.defvjp(lambda m, x: (_mask_dot(m, x, NN3, True), m),
                lambda m, g: (jnp.zeros_like(m), _mask_dot(m, g, TN3, True)))


@jax.custom_vjp
def _mask_tn(x, m):
    return _mask_dot(m, x, TN3, False)


_mask_tn.defvjp(lambda x, m: (_mask_dot(m, x, TN3, False), m),
                lambda m, g: (_mask_dot(m, g, NT3, True), jnp.zeros_like(m)))


@jax.custom_vjp
def _nn_hi(a, b):
    return _dot3(a, b, NN3)


_nn_hi.defvjp(lambda a, b: (_dot3(a, b, NN3), (a, b)), lambda r, g: (_dot3(g, r[1], NT3), _dot3(r[0], g, TN3)))


@jax.custom_vjp
def _nn(a, b):
    return _bdot(a, b, NN3)


_nn.defvjp(lambda a, b: (_bdot(a, b, NN3), (a, b)), lambda r, g: (_bdot(g, r[1], NT3), _bdot(r[0], g, TN3)))


@jax.custom_vjp
def _nt(a, b):
    return _bdot(a, b, NT3)


_nt.defvjp(lambda a, b: (_bdot(a, b, NT3), (a, b)), lambda r, g: (_bdot(g, r[1], NN3), _bdot(g, r[0], TN3)))


@jax.custom_vjp
def _tn(a, b):
    return _bdot(a, b, TN3)


_tn.defvjp(lambda a, b: (_bdot(a, b, TN3), (a, b)), lambda r, g: (_bdot(r[1], g, NT3), _bdot(r[0], g, NN3)))


def _conv_pre(ext, w, rows):
    acc = w[0:1] * ext[5:5 + rows]
    for j in range(1, GDN_CONV):
        acc = acc + w[j:j + 1] * ext[5 + j:5 + j + rows]
    return acc


def _conv_fwd(proj, conv_w, *, name):
    t = proj.shape[0]
    tm, tc = _pick(t, 512), 512
    nb = tm // 8

    def body(u_ref, p_ref, w_ref, o_ref):
        i = pl.program_id(1)
        prev = jnp.where(i > 0, p_ref[...], 0.0)
        ext = jnp.concatenate([prev, u_ref[...]], axis=0)
        o_ref[...] = _silu(_conv_pre(ext, w_ref[...], tm))

    return _pcall(body, name=name, grid=(CONV_CH // tc, t // tm),
                  in_specs=[pl.BlockSpec((tm, tc), lambda j, i: (i, j)),
                            pl.BlockSpec((8, tc), lambda j, i: (jnp.maximum(i * nb - 1, 0), j)),
                            pl.BlockSpec((8, tc), lambda j, i: (0, j))],
                  out_specs=pl.BlockSpec((tm, tc), lambda j, i: (i, j)),
                  out_shape=jax.ShapeDtypeStruct((t, CONV_CH), F32))(proj, proj, conv_w)


def _conv_bwd(proj, conv_w, dy, *, name):
    t = proj.shape[0]
    tm, tc = _pick(t, 512), 512
    nb = tm // 8
    last = t // tm - 1

    def body(u_ref, p_ref, n_ref, dy_ref, dyn_ref, w_ref, du_ref, dw_ref):
        i = pl.program_id(1)
        w = w_ref[...]
        prev = jnp.where(i > 0, p_ref[...], 0.0)
        ext = jnp.concatenate([prev, u_ref[...], n_ref[...]], axis=0)
        c = _conv_pre(ext, w, tm + 8)
        dy_ext = jnp.concatenate([dy_ref[...], jnp.where(i < last, dyn_ref[...], 0.0)], axis=0)
        dc = dy_ext * _dsilu(c)
        du = w[3:4] * dc[0:tm]
        for j in range(GDN_CONV - 1):
            du = du + w[j:j + 1] * dc[3 - j:3 - j + tm]
        du_ref[...] = du.astype(du_ref.dtype)

        @pl.when(i == 0)
        def _():
            dw_ref[...] = jnp.zeros_like(dw_ref)

        for j in range(GDN_CONV):
            dw_ref[j] += (dc[0:tm] * ext[5 + j:5 + j + tm]).reshape(nb, 8, tc).sum(axis=0)

    cur = lambda j, i: (i, j)
    return _pcall(body, name=name, grid=(CONV_CH // tc, t // tm),
                  in_specs=[pl.BlockSpec((tm, tc), cur),
                            pl.BlockSpec((8, tc), lambda j, i: (jnp.maximum(i * nb - 1, 0), j)),
                            pl.BlockSpec((8, tc), lambda j, i: (jnp.minimum((i + 1) * nb, t // 8 - 1), j)),
                            pl.BlockSpec((tm, tc), cur),
                            pl.BlockSpec((8, tc), lambda j, i: (jnp.minimum((i + 1) * nb, t // 8 - 1), j)),
                            pl.BlockSpec((8, tc), lambda j, i: (0, j))],
                  out_specs=[pl.BlockSpec((tm, tc), cur), pl.BlockSpec((GDN_CONV, 8, tc), lambda j, i: (0, 0, j))],
                  out_shape=[jax.ShapeDtypeStruct((t, CONV_CH), BF16), jax.ShapeDtypeStruct((GDN_CONV, 8, CONV_CH), F32)],
                  )(proj, proj, proj, dy, dy, conv_w)


def _gdn_chunk(q_raw, k_raw, v, misc, params, state):
    nh, c = q_raw.shape[0], q_raw.shape[1]
    lane = lax.broadcasted_iota(jnp.int32, misc.shape, 1)
    prow = lax.broadcasted_iota(jnp.int32, params.shape, 0)
    plane = lax.broadcasted_iota(jnp.int32, params.shape, 1)
    heads = lambda pieces: jnp.concatenate([p[None] for p in pieces], axis=0)
    col = lambda at: heads([jnp.sum(jnp.where(lane == at + h, misc, 0.0), axis=1, keepdims=True) for h in range(nh)])
    par = lambda row: heads([jnp.sum(jnp.where((prow == row) & (plane == h), params, 0.0), keepdims=True)
                             for h in range(nh)])
    b_raw, a_raw = col(LANE_B), col(LANE_A)
    a_log, dt_bias = par(0), par(1)
    beta = _sigmoid(b_raw)
    g = -jnp.exp(a_log) * _softplus(a_raw + dt_bias)

    q = q_raw * lax.rsqrt(jnp.sum(q_raw * q_raw, axis=-1, keepdims=True) + EPS) * (HEAD ** -0.5)
    k = k_raw * lax.rsqrt(jnp.sum(k_raw * k_raw, axis=-1, keepdims=True) + EPS)

    ri = lax.broadcasted_iota(jnp.int32, (c, c), 0)
    ci = lax.broadcasted_iota(jnp.int32, (c, c), 1)
    tril, strict = ri >= ci, ri > ci
    batch = lambda m: jnp.broadcast_to(m.astype(F32), (nh, c, c))
    g_b = jnp.broadcast_to(g, (nh, c, c))
    gc_col = _mask_nn(batch(tril), g_b)
    gc_row = _mask_tn(g_b, batch(ri <= ci))
    gc = jnp.sum(jnp.where(ci == 0, gc_col, 0.0), axis=2, keepdims=True)
    decay = jnp.exp(jnp.where(tril, gc_col - gc_row, -1e30))

    kb = k * beta
    vb = v * beta
    a_mat = jnp.where(strict, _nt(kb, k) * decay, 0.0)
    x = -a_mat
    inv = (ri == ci).astype(F32) + x
    for _ in range(5):
        x = _nn_hi(x, x)
        inv = inv + _nn_hi(inv, x)
    u = _nn_hi(inv, vb)
    w = _nn_hi(inv, kb * jnp.exp(gc))
    intra = _nt(q, k) * decay

    v_new = u - _nn(w, state)
    o = _nn(q * jnp.exp(gc), state) + _nn(intra, v_new)
    g_last = jnp.sum(g, axis=1, keepdims=True)
    k_dec = k * jnp.exp(g_last - gc)
    new_state = state * jnp.exp(g_last) + _tn(k_dec, v_new)
    return o, new_state


def _gdn_specs(nc, rev):
    cidx = (lambda n: nc - 1 - n) if rev else (lambda n: n)
    hb = lambda part: pl.BlockSpec((GDN_CHUNK, GDN_QK), lambda n: (cidx(n), part))
    misc = pl.BlockSpec((GDN_CHUNK, HEAD), lambda n: (cidx(n), COL_MISC // HEAD))
    params = pl.BlockSpec((8, HEAD), lambda n: (0, 0))
    hist = pl.BlockSpec((1, GDN_HEADS, HEAD, HEAD), lambda n: (cidx(n), 0, 0, 0))
    return hb, misc, params, hist


def _split_heads(v):
    return jnp.stack([v[:, h * HEAD:(h + 1) * HEAD] for h in range(v.shape[1] // HEAD)])


def _merge_heads(v):
    return jnp.concatenate([v[h] for h in range(v.shape[0])], axis=1)


def _gdn_fwd(qkv, proj, params, *, name):
    t = qkv.shape[0]
    nc = t // GDN_CHUNK
    hb, misc, pspec, hist = _gdn_specs(nc, False)

    def body(q_ref, k_ref, v_ref, m_ref, p_ref, o_ref, hist_ref, s_ref):
        @pl.when(pl.program_id(0) == 0)
        def _():
            s_ref[...] = jnp.zeros_like(s_ref)

        state = s_ref[...]
        hist_ref[0] = state
        o, new_state = _gdn_chunk(_split_heads(q_ref[...]), _split_heads(k_ref[...]), _split_heads(v_ref[...]),
                                  m_ref[...], p_ref[...], state)
        o_ref[...] = _merge_heads(o)
        s_ref[...] = new_state

    return _pcall(body, name=name, grid=(nc,),
                  in_specs=[hb(0), hb(1), hb(2), misc, pspec],
                  out_specs=[hb(0), hist],
                  out_shape=[jax.ShapeDtypeStruct((t, GDN_QK), F32),
                             jax.ShapeDtypeStruct((nc, GDN_HEADS, HEAD, HEAD), F32)],
                  scratch=[pltpu.VMEM((GDN_HEADS, HEAD, HEAD), F32)])(qkv, qkv, qkv, proj, params)


def _gdn_bwd(qkv, proj, params, hist_arr, do, dmisc_in, *, name, carry=None):
    t = qkv.shape[0]
    nc = t // GDN_CHUNK
    hb, misc, pspec, hist = _gdn_specs(nc, True)
    mrow = pl.BlockSpec((GDN_CHUNK, HEAD), lambda n: (nc - 1 - n, 0))

    def body(q_ref, k_ref, v_ref, m_ref, p_ref, hist_ref, do_ref, dmi_ref,
             dq_ref, dk_ref, dv_ref, dm_ref, dp_ref, ds_ref):
        @pl.when(pl.program_id(0) == 0)
        def _():
            ds_ref[...] = jnp.zeros_like(ds_ref)
            dp_ref[...] = jnp.zeros_like(dp_ref)

        _, vjp = jax.vjp(_gdn_chunk, _split_heads(q_ref[...]), _split_heads(k_ref[...]), _split_heads(v_ref[...]),
                         m_ref[...], p_ref[...], hist_ref[0])
        dq, dk, dv, dm, dp, ds = vjp((_split_heads(do_ref[...]), ds_ref[...]))
        dq_ref[...] = _merge_heads(dq)
        dk_ref[...] = _merge_heads(dk)
        dv_ref[...] = _merge_heads(dv)
        ds_ref[...] = ds
        dm_ref[...] = dmi_ref[...] + dm
        dp_ref[...] += dp

    hd = jax.ShapeDtypeStruct((t, GDN_QK), F32)
    return _pcall(body, name=name, grid=(nc,),
                  in_specs=[hb(0), hb(1), hb(2), misc, pspec, hist, hb(0), mrow],
                  out_specs=[hb(0), hb(0), hb(0), mrow, pspec],
                  out_shape=[hd, hd, hd, jax.ShapeDtypeStruct((t, HEAD), F32), jax.ShapeDtypeStruct((8, HEAD), F32)],
                  scratch=[pltpu.VMEM((GDN_HEADS, HEAD, HEAD), F32)], carry=carry,
                  )(qkv, qkv, qkv, proj, params, hist_arr, do, dmisc_in)


def _gate_fwd(o_raw, proj, w, *, name):
    t = o_raw.shape[0]
    tm = _rows(t, HEAD)
    zb = COL_Z // HEAD

    def body(o_ref, z_ref, w_ref, out_ref):
        ov = o_ref[...]
        r = lax.rsqrt(jnp.mean(ov * ov, axis=-1, keepdims=True) + EPS)
        out_ref[...] = (ov * r * w_ref[...] * _silu(z_ref[...])).astype(out_ref.dtype)

    blk = pl.BlockSpec((tm, HEAD), lambda i, h: (i, h))
    return _pcall(body, name=name, grid=(t // tm, GDN_HEADS),
                  in_specs=[blk, pl.BlockSpec((tm, HEAD), lambda i, h: (i, zb + h)), pl.BlockSpec((1, HEAD), lambda i, h: (0, 0))],
                  out_specs=blk, out_shape=jax.ShapeDtypeStruct((t, GDN_QK), BF16))(o_raw, proj, w)


def _gate_bwd(o_raw, proj, w, dmixed, *, name):
    t = o_raw.shape[0]
    tm = _rows(t, HEAD)
    zb = COL_Z // HEAD

    def body(o_ref, z_ref, w_ref, dy_ref, do_ref, dz_ref, dw_ref):
        ov, zv, dyv = o_ref[...], z_ref[...], dy_ref[...]
        r = lax.rsqrt(jnp.mean(ov * ov, axis=-1, keepdims=True) + EPS)
        xh = ov * r
        dn = dyv * _silu(zv)
        dz_ref[...] = (dyv * xh * w_ref[...] * _dsilu(zv)).astype(dz_ref.dtype)
        dnw = dn * w_ref[...]
        do_ref[...] = r * (dnw - xh * jnp.mean(dnw * xh, axis=-1, keepdims=True))

        @pl.when((pl.program_id(0) == 0) & (pl.program_id(1) == 0))
        def _():
            dw_ref[...] = jnp.zeros_like(dw_ref)

        dw_ref[...] += (dn * xh).reshape(tm // 8, 8, HEAD).sum(axis=0)

    blk = pl.BlockSpec((tm, HEAD), lambda i, h: (i, h))
    return _pcall(body, name=name, grid=(t // tm, GDN_HEADS),
                  in_specs=[blk, pl.BlockSpec((tm, HEAD), lambda i, h: (i, zb + h)), pl.BlockSpec((1, HEAD), lambda i, h: (0, 0)), blk],
                  out_specs=[blk, blk, pl.BlockSpec((8, HEAD), lambda i, h: (0, 0))],
                  out_shape=[jax.ShapeDtypeStruct((t, GDN_QK), F32), jax.ShapeDtypeStruct((t, GDN_QK), BF16),
                             jax.ShapeDtypeStruct((8, HEAD), F32)])(o_raw, proj, w, dmixed)


def _rope_tables():
    half = QK_ROPE // 2
    inv = ROPE_THETA ** (-jnp.arange(half, dtype=F32) / half)
    zeros = jnp.zeros((HEAD - QK_ROPE,), F32)
    inv_row = jnp.concatenate([inv, inv, zeros])
    sign_row = jnp.concatenate([-jnp.ones((half,), F32), jnp.ones((half,), F32), zeros])
    mask_row = jnp.concatenate([jnp.ones((QK_ROPE,), F32), zeros])
    return jnp.concatenate([inv_row[None], sign_row[None], mask_row[None], jnp.zeros((5, HEAD), F32)], axis=0)


def _rope_cs(pos, tab, *, name):
    t = pos.shape[0]
    tm = _pick(t, 1024)

    def body(pos_ref, tab_ref, o_ref):
        tab = tab_ref[...]
        ang = pos_ref[...] * tab[0:1]
        o_ref[...] = jnp.concatenate([jnp.cos(ang) * tab[2:3], jnp.sin(ang) * tab[1:2]], axis=1)

    return _pcall(body, name=name, grid=(t // tm,),
                  in_specs=[pl.BlockSpec((tm, 1), lambda i: (i, 0)), pl.BlockSpec((8, HEAD), lambda i: (0, 0))],
                  out_specs=pl.BlockSpec((tm, 2 * HEAD), lambda i: (i, 0)),
                  out_shape=jax.ShapeDtypeStruct((t, 2 * HEAD), F32))(pos, tab)


def _rotate(x, cs, sign):
    lane = lax.broadcasted_iota(jnp.int32, x.shape, 1)
    half = QK_ROPE // 2
    partner = jnp.where(lane < half, pltpu.roll(x, HEAD - half, axis=1), pltpu.roll(x, half, axis=1))
    return x * cs[:, :HEAD] + partner * (cs[:, HEAD:] * sign)


def _q_rot(q, cs, *, name, sign, out_dtype=BF16):
    t = q.shape[0]
    tm = _pick(t, 1024)
    scale = (HEAD + QK_ROPE) ** -0.5

    def body(q_ref, cs_ref, o_ref):
        qv = q_ref[...].astype(F32)
        rot = _rotate(qv[:, HEAD:], cs_ref[...], sign)
        o_ref[...] = (jnp.concatenate([qv[:, :HEAD], rot], axis=1) * scale).astype(o_ref.dtype)

    blk = pl.BlockSpec((tm, QHEAD), lambda i, h: (i, h))
    return _pcall(body, name=name, grid=(t // tm, MLA_HEADS),
                  in_specs=[blk, pl.BlockSpec((tm, 2 * HEAD), lambda i, h: (i, 0))],
                  out_specs=blk, out_shape=jax.ShapeDtypeStruct((t, MLA_HEADS * QHEAD), out_dtype))(q, cs)


def _kv_prep(kv, proj, cs, *, name):
    t = kv.shape[0]
    tm = _pick(t, 1024)

    def body(kv_ref, m_ref, cs_ref, k_ref, v_ref):
        kvv = kv_ref[...]
        misc = m_ref[...]
        lane = lax.broadcasted_iota(jnp.int32, misc.shape, 1)
        rot = _rotate(jnp.where(lane < QK_ROPE, misc, 0.0), cs_ref[...], 1.0)
        k_ref[...] = jnp.concatenate([kvv[:, :HEAD], rot], axis=1).astype(k_ref.dtype)
        v_ref[...] = kvv[:, HEAD:].astype(v_ref.dtype)

    return _pcall(body, name=name, grid=(t // tm, MLA_HEADS),
                  in_specs=[pl.BlockSpec((tm, QHEAD), lambda i, h: (i, h)),
                            pl.BlockSpec((tm, HEAD), lambda i, h: (i, COL_MISC // HEAD)),
                            pl.BlockSpec((tm, 2 * HEAD), lambda i, h: (i, 0))],
                  out_specs=[pl.BlockSpec((tm, QHEAD), lambda i, h: (i, h)), pl.BlockSpec((tm, HEAD), lambda i, h: (i, h))],
                  out_shape=[jax.ShapeDtypeStruct((t, MLA_HEADS * QHEAD), BF16), jax.ShapeDtypeStruct((t, MLA_HEADS * HEAD), BF16)],
                  )(kv, proj, cs)


def _krope_bwd(dkr, cs, *, name):
    t = dkr.shape[0]
    tm = _pick(t, 512)

    def body(d_ref, cs_ref, o_ref):
        d = d_ref[...]
        acc = d[:, :HEAD]
        for h in range(1, MLA_HEADS):
            acc = acc + d[:, h * HEAD:(h + 1) * HEAD]
        o_ref[...] = _rotate(acc, cs_ref[...], -1.0)

    return _pcall(body, name=name, grid=(t // tm,),
                  in_specs=[pl.BlockSpec((tm, MLA_HEADS * HEAD), lambda i: (i, 0)), pl.BlockSpec((tm, 2 * HEAD), lambda i: (i, 0))],
                  out_specs=pl.BlockSpec((tm, HEAD), lambda i: (i, 0)),
                  out_shape=jax.ShapeDtypeStruct((t, HEAD), F32))(dkr, cs)


NEG = -1e30


def _attn_fwd(q, k, v, *, name, tq=1024, tk=1024, carry=None):
    t = q.shape[0]
    tq, tk = _pick(t, tq), _pick(t, tk)
    nq, nk = t // tq, t // tk
    last_kv = lambda i: (i * tq + tq - 1) // tk

    def body(q_ref, k_ref, v_ref, o_ref, lse_ref, m_ref, l_ref, acc_ref):
        i, j = pl.program_id(1), pl.program_id(2)

        @pl.when(j == 0)
        def _():
            m_ref[...] = jnp.full_like(m_ref, NEG)
            l_ref[...] = jnp.zeros_like(l_ref)
            acc_ref[...] = jnp.zeros_like(acc_ref)

        def step(masked):
            s = lax.dot_general(q_ref[...], k_ref[...], (NT, ((), ())), preferred_element_type=F32)
            if masked:
                qpos = i * tq + lax.broadcasted_iota(jnp.int32, s.shape, 0)
                kpos = j * tk + lax.broadcasted_iota(jnp.int32, s.shape, 1)
                s = jnp.where(kpos <= qpos, s, NEG)
            m_prev = m_ref[...]
            m_new = jnp.maximum(m_prev, jnp.max(s, axis=1, keepdims=True))
            alpha = jnp.exp(m_prev - m_new)
            p = jnp.exp(s - m_new)
            l_ref[...] = alpha * l_ref[...] + jnp.sum(p, axis=1, keepdims=True)
            acc_ref[...] = alpha * acc_ref[...] + lax.dot_general(p.astype(BF16), v_ref[...], (NN, ((), ())),
                                                                  preferred_element_type=F32)
            m_ref[...] = m_new

        crosses = j * tk + tk - 1 > i * tq

        @pl.when((j <= last_kv(i)) & crosses)
        def _():
            step(True)

        @pl.when((j <= last_kv(i)) & jnp.logical_not(crosses))
        def _():
            step(False)

        @pl.when(j == nk - 1)
        def _():
            o_ref[...] = acc_ref[...] / l_ref[...]
            lse_ref[...] = jnp.broadcast_to(m_ref[...] + jnp.log(l_ref[...]), lse_ref.shape)

    qblk = pl.BlockSpec((tq, QHEAD), lambda h, i, j: (i, h))
    oblk = pl.BlockSpec((tq, HEAD), lambda h, i, j: (i, h))
    return _pcall(body, name=name, grid=(MLA_HEADS, nq, nk),
                  in_specs=[qblk, pl.BlockSpec((tk, QHEAD), lambda h, i, j: (jnp.minimum(j, last_kv(i)), h)),
                            pl.BlockSpec((tk, HEAD), lambda h, i, j: (jnp.minimum(j, last_kv(i)), h))],
                  out_specs=[oblk, oblk],
                  out_shape=[jax.ShapeDtypeStruct((t, MLA_HEADS * HEAD), F32), jax.ShapeDtypeStruct((t, MLA_HEADS * HEAD), F32)],
                  scratch=[pltpu.VMEM((tq, 1), F32), pltpu.VMEM((tq, 1), F32), pltpu.VMEM((tq, HEAD), F32)],
                  carry=carry)(q, k, v)


def _attn_bwd(q, k, v, do, lse, delta, *, name, tq=512, tk=512, carry=None):
    t = q.shape[0]
    tq, tk = _pick(t, tq), _pick(t, tk)
    nq, nk = t // tq, t // tk
    first_q = lambda j: (j * tk) // tq

    lanes = lambda col: jnp.tile(col, (1, tk // HEAD))

    def body(q_ref, k_ref, v_ref, do_ref, lse_ref, dl_ref, dq_ref, dkv_ref, dkr_ref, dk_acc, dv_acc):
        j, i = pl.program_id(1), pl.program_id(2)

        @pl.when(i == 0)
        def _():
            dk_acc[...] = jnp.zeros_like(dk_acc)
            dv_acc[...] = jnp.zeros_like(dv_acc)

        def step(masked):
            qv, kv_, dov = q_ref[...], k_ref[...], do_ref[...].astype(BF16)
            s = lax.dot_general(qv, kv_, (NT, ((), ())), preferred_element_type=F32)
            p = jnp.exp(s - lanes(lse_ref[...]))
            if masked:
                qpos = i * tq + lax.broadcasted_iota(jnp.int32, s.shape, 0)
                kpos = j * tk + lax.broadcasted_iota(jnp.int32, s.shape, 1)
                p = jnp.where(kpos <= qpos, p, 0.0)
            dv_acc[...] += lax.dot_general(p.astype(BF16), dov, (TN, ((), ())), preferred_element_type=F32)
            dp = lax.dot_general(dov, v_ref[...], (NT, ((), ())), preferred_element_type=F32)
            ds = (p * (dp - lanes(dl_ref[...]))).astype(BF16)
            dk_acc[...] += lax.dot_general(ds, qv, (TN, ((), ())), preferred_element_type=F32)
            contrib = lax.dot_general(ds, kv_, (NN, ((), ())), preferred_element_type=F32)
            rows = pl.ds(pl.multiple_of(i * tq, tq), tq)

            @pl.when(j == 0)
            def _():
                dq_ref[rows, :] = contrib

            @pl.when(j > 0)
            def _():
                dq_ref[rows, :] += contrib

        crosses = j * tk + tk - 1 > i * tq

        @pl.when((i >= first_q(j)) & crosses)
        def _():
            step(True)

        @pl.when((i >= first_q(j)) & jnp.logical_not(crosses))
        def _():
            step(False)

        @pl.when(i == nq - 1)
        def _():
            dk = dk_acc[...]
            dkv_ref[...] = jnp.concatenate([dk[:, :HEAD], dv_acc[...]], axis=1).astype(dkv_ref.dtype)
            dkr_ref[...] = dk[:, HEAD:]

    qi = lambda h, j, i: (jnp.maximum(i, first_q(j)), h)
    kj = lambda h, j, i: (j, h)
    return _pcall(body, name=name, grid=(MLA_HEADS, nk, nq),
                  in_specs=[pl.BlockSpec((tq, QHEAD), qi), pl.BlockSpec((tk, QHEAD), kj), pl.BlockSpec((tk, HEAD), kj),
                            pl.BlockSpec((tq, HEAD), qi), pl.BlockSpec((tq, HEAD), qi), pl.BlockSpec((tq, HEAD), qi)],
                  out_specs=[pl.BlockSpec((t, QHEAD), lambda h, j, i: (0, h)), pl.BlockSpec((tk, QHEAD), kj),
                             pl.BlockSpec((tk, HEAD), kj)],
                  out_shape=[jax.ShapeDtypeStruct((t, MLA_HEADS * QHEAD), F32), jax.ShapeDtypeStruct((t, MLA_HEADS * QHEAD), BF16),
                             jax.ShapeDtypeStruct((t, MLA_HEADS * HEAD), F32)],
                  scratch=[pltpu.VMEM((tk, QHEAD), F32), pltpu.VMEM((tk, HEAD), F32)], carry=carry)(q, k, v, do, lse, delta)


def _ffn_up(h, wgate, wup, *, name, bm=512, bn=FF_WIDE):
    t = h.shape[0]
    bm = _pick(t, bm)

    def body(h_ref, wg_ref, wu_ref, g_ref, u_ref, a_ref):
        hv = h_ref[...]
        g = lax.dot_general(hv, wg_ref[...], (NN, ((), ())), preferred_element_type=F32)
        u = lax.dot_general(hv, wu_ref[...], (NN, ((), ())), preferred_element_type=F32)
        g_ref[...] = g.astype(g_ref.dtype)
        u_ref[...] = u.astype(u_ref.dtype)
        a_ref[...] = (_silu(g) * u).astype(a_ref.dtype)

    w_spec = pl.BlockSpec((D_MODEL, bn), lambda j, i: (0, j))
    o_spec = pl.BlockSpec((bm, bn), lambda j, i: (i, j))
    sds = jax.ShapeDtypeStruct((t, D_FF), BF16)
    return _pcall(body, name=name, grid=(D_FF // bn, t // bm),
                  in_specs=[pl.BlockSpec((bm, D_MODEL), lambda j, i: (i, 0)), w_spec, w_spec],
                  out_specs=[o_spec] * 3, out_shape=[sds] * 3)(h, wgate, wup)


def _ffn_down_dx(dy, wdown, gate, up, *, name, bm=512, bn=FF_WIDE):
    t = dy.shape[0]
    bm = _pick(t, bm)

    def body(dy_ref, w_ref, g_ref, u_ref, dg_ref, du_ref):
        d = lax.dot_general(dy_ref[...].astype(BF16), w_ref[...], (NT, ((), ())), preferred_element_type=F32)
        g = g_ref[...].astype(F32)
        dg_ref[...] = (d * u_ref[...].astype(F32) * _dsilu(g)).astype(dg_ref.dtype)
        du_ref[...] = (d * _silu(g)).astype(du_ref.dtype)

    o_spec = pl.BlockSpec((bm, bn), lambda j, i: (i, j))
    sds = jax.ShapeDtypeStruct((t, D_FF), BF16)
    return _pcall(body, name=name, grid=(D_FF // bn, t // bm),
                  in_specs=[pl.BlockSpec((bm, D_MODEL), lambda j, i: (i, 0)), pl.BlockSpec((bn, D_MODEL), lambda j, i: (j, 0)),
                            o_spec, o_spec],
                  out_specs=[o_spec, o_spec], out_shape=[sds, sds])(dy, wdown, gate, up)


def _loss_bwd(x2, w, target, *, name):
    t = x2.shape[0]
    tm = _rows(t, D_MODEL)

    def body(x_ref, w_ref, t_ref, dx_ref, dw_ref, l_ref):
        xv, wv = x_ref[...], w_ref[...]
        r = lax.rsqrt(jnp.mean(xv * xv, axis=-1, keepdims=True) + EPS)
        xh = xv * r
        err = xh * wv - t_ref[...]
        dy = err * (1.0 / D_MODEL)
        dyw = dy * wv
        dx_ref[...] = r * (dyw - xh * jnp.mean(dyw * xh, axis=-1, keepdims=True))

        @pl.when(pl.program_id(0) == 0)
        def _():
            dw_ref[...] = jnp.zeros_like(dw_ref)
            l_ref[...] = jnp.zeros_like(l_ref)

        dw_ref[...] += (dy * xh).reshape(tm // 8, 8, D_MODEL).sum(axis=0)
        sq = (err * err).reshape(tm // 8, 8, D_MODEL).sum(axis=0)
        part = sq[:, :HEAD]
        for c in range(1, D_MODEL // HEAD):
            part = part + sq[:, c * HEAD:(c + 1) * HEAD]
        l_ref[...] += part * (0.5 / D_MODEL)

    row = pl.BlockSpec((tm, D_MODEL), lambda i: (i, 0))
    return _pcall(body, name=name, grid=(t // tm,),
                  in_specs=[row, pl.BlockSpec((1, D_MODEL), lambda i: (0, 0)), row],
                  out_specs=[row, pl.BlockSpec((8, D_MODEL), lambda i: (0, 0)), pl.BlockSpec((8, HEAD), lambda i: (0, 0))],
                  out_shape=[jax.ShapeDtypeStruct((t, D_MODEL), F32), jax.ShapeDtypeStruct((8, D_MODEL), F32),
                             jax.ShapeDtypeStruct((8, HEAD), F32)])(x2, w, target)


def _unshard_cols(g):
    return jnp.transpose(g, (1, 0, 2)).reshape(g.shape[1], N_DEV * g.shape[2])


def _shard_cols(w):
    return jnp.transpose(w.reshape(w.shape[0], N_DEV, w.shape[1] // N_DEV), (1, 0, 2))


def _win_to_padded(w):
    pad = jnp.zeros((w.shape[0], PROJ_W - IN_WIDTH), w.dtype)
    return jnp.concatenate([w[:, :4096], w[:, 4112:5136], w[:, 5136:5200], w[:, 4096:4112], pad], axis=1)


def _win_from_padded(d):
    return jnp.concatenate([d[:, :4096], d[:, 5184:5200], d[:, 4096:5120], d[:, 5120:5184]], axis=1)


def _wuq_to_padded(w):
    w3 = w.reshape(w.shape[0], MLA_HEADS, HEAD + QK_ROPE)
    return jnp.pad(w3, ((0, 0), (0, 0), (0, QHEAD - HEAD - QK_ROPE))).reshape(w.shape[0], MLA_HEADS * QHEAD)


def _wuq_from_padded(d):
    return d.reshape(d.shape[0], MLA_HEADS, QHEAD)[:, :, :HEAD + QK_ROPE].reshape(d.shape[0], MLA_HEADS * (HEAD + QK_ROPE))


def _late_weights(g_out, g_gate, g_up, g_down):
    return g_out.reshape(D_MODEL, D_MODEL), _unshard_cols(g_gate), _unshard_cols(g_up), g_down.reshape(D_FF, D_MODEL)


def _local_step(x, pos, target, win_p, wuq_p, wukv, late, conv_w, small, exchange):
    cs = _rope_cs(pos, _rope_tables(), name="rope_cs")
    if not exchange:
        wout, wgate, wup, wdown = late
    h1 = _rms_fwd(x, small["attn_norm_w"], name="rms1_fwd", width=D_MODEL)
    proj = _mm(h1, win_p, name="mm_in", bn=768)
    qkv = _conv_fwd(proj, conv_w, name="conv_fwd")
    o_gdn_raw, hist = _gdn_fwd(qkv, proj, small["gdn_params"], name="gdn_fwd")
    o_gdn = _gate_fwd(o_gdn_raw, proj, small["gdn_norm_w"], name="gate_fwd")
    cqn = _rms_fwd(proj, small["q_norm_w"], name="rmsq_fwd", width=Q_LORA, col0=COL_CQ)
    ckvn = _rms_fwd(proj, small["kv_norm_w"], name="rmskv_fwd", width=KV_LORA, col0=COL_CKV)
    q_pre = _mm(cqn, wuq_p, name="mm_uq")
    kv = _mm(ckvn, wukv, name="mm_ukv")
    q_full = _q_rot(q_pre, cs, name="q_rot", sign=1.0)
    k_full, v_b = _kv_prep(kv, proj, cs, name="kv_prep")
    if exchange:
        (o_mla_raw, lse), gathered = _attn_fwd(q_full, k_full, v_b, name="attn_fwd", carry=_Exchange(late, [True] * 4))
        wout, wgate, wup, wdown = _late_weights(*gathered)
    else:
        o_mla_raw, lse = _attn_fwd(q_full, k_full, v_b, name="attn_fwd")
    o_mla = _rms_fwd(o_mla_raw, small["mla_out_norm_w"], name="rmso_fwd", width=HEAD, heads=MLA_HEADS)
    mixed = jnp.concatenate([o_gdn, o_mla], axis=1)
    x1 = _mm(mixed, wout, name="mm_out", res=x)
    h2 = _rms_fwd(x1, small["ffn_norm_w"], name="rms2_fwd", width=D_MODEL)
    gate, up, act = _ffn_up(h2, wgate, wup, name="ffn_up")
    x2 = _mm(act, wdown, name="mm_down", res=x1, bk=FF_WIDE)
    dx2, dw_final, loss_part = _loss_bwd(x2, small["final_norm_w"], target, name="loss_bwd")
    dgate, dup = _ffn_down_dx(dx2, wdown, gate, up, name="ffn_down_dx")
    d_wdown = _mm(act, dx2, name="mm_down_dw", ta=True, out_dtype=BF16, bm=FF_WIDE)
    dh2 = _mm(dgate, wgate, name="mm_gate_dx", tb=True, bk=FF_WIDE)
    dh2 = _mm(dup, wup, name="mm_up_dx", tb=True, res=dh2, bk=FF_WIDE)
    d_wgate = _mm(h2, dgate, name="mm_gate_dw", ta=True, out_dtype=BF16, bn=FF_WIDE)
    d_wup = _mm(h2, dup, name="mm_up_dw", ta=True, out_dtype=BF16, bn=FF_WIDE)
    dx1, dw_ffn = _rms_bwd(x1, small["ffn_norm_w"], dh2, name="rms2_bwd", width=D_MODEL, res=dx2)
    dmixed = _mm(dx1, wout, name="mm_out_dx", tb=True)
    d_wout = _mm(mixed, dx1, name="mm_out_dw", ta=True, out_dtype=BF16)
    do_mla, dw_mla_out, delta = _rms_bwd(o_mla_raw, small["mla_out_norm_w"], dmixed, name="rmso_bwd", width=HEAD,
                                         heads=MLA_HEADS, dcol0=GDN_QK, with_delta=True)
    if exchange:
        send = [d_wdown.reshape(N_DEV, D_FF // N_DEV, D_MODEL), _shard_cols(d_wgate), _shard_cols(d_wup)]
        (dq_full, dkv, dkr_h), (r_down, r_gate, r_up) = _attn_bwd(q_full, k_full, v_b, do_mla, lse, delta, name="attn_bwd",
                                                                  carry=_Exchange(send, [False] * 3))
    else:
        dq_full, dkv, dkr_h = _attn_bwd(q_full, k_full, v_b, do_mla, lse, delta, name="attn_bwd")
    dq_pre = _q_rot(dq_full, cs, name="q_rot_bwd", sign=-1.0)
    dmisc_kr = _krope_bwd(dkr_h, cs, name="krope_bwd")
    dcqn = _mm(dq_pre, wuq_p, name="mm_uq_dx", tb=True)
    d_wuq = _mm(cqn, dq_pre, name="mm_uq_dw", ta=True, out_dtype=BF16)
    dckvn = _mm(dkv, wukv, name="mm_ukv_dx", tb=True)
    d_wukv = _mm(ckvn, dkv, name="mm_ukv_dw", ta=True, out_dtype=BF16)
    dcq, dw_qn = _rms_bwd(proj, small["q_norm_w"], dcqn, name="rmsq_bwd", width=Q_LORA, col0=COL_CQ, out_dtype=BF16)
    dckv, dw_kvn = _rms_bwd(proj, small["kv_norm_w"], dckvn, name="rmskv_bwd", width=KV_LORA, col0=COL_CKV, out_dtype=BF16)
    do_gdn, dz, dw_gdn = _gate_bwd(o_gdn_raw, proj, small["gdn_norm_w"], dmixed, name="gate_bwd")
    if exchange:
        send = [d_wout.reshape(N_DEV, D_MODEL // N_DEV, D_MODEL), _shard_cols(_wuq_from_padded(d_wuq)), _shard_cols(d_wukv)]
        (dgq, dgk, dgv, dmisc, d_params), (r_out, r_uq, r_ukv) = _gdn_bwd(
            qkv, proj, small["gdn_params"], hist, do_gdn, dmisc_kr, name="gdn_bwd", carry=_Exchange(send, [False] * 3))
    else:
        dgq, dgk, dgv, dmisc, d_params = _gdn_bwd(qkv, proj, small["gdn_params"], hist, do_gdn, dmisc_kr, name="gdn_bwd")
    dqkv_pre, dconv = _conv_bwd(proj, conv_w, jnp.concatenate([dgq, dgk, dgv], axis=1), name="conv_bwd")
    dproj = jnp.concatenate([dqkv_pre, dz, dcq, dckv, dmisc.astype(BF16), jnp.zeros((x.shape[0], PROJ_W - COL_MISC - HEAD), BF16)], axis=1)
    d_win = _mm(h1, dproj, name="mm_in_dw", ta=True, out_dtype=BF16, bn=768)
    if exchange:
        dh1, (r_in,) = _mm(dproj, win_p, name="mm_in_dx", tb=True, bk=768,
                           carry=_Exchange([_shard_cols(_win_from_padded(d_win))], [False]))
        d_win = r_in
    else:
        dh1 = _mm(dproj, win_p, name="mm_in_dx", tb=True, bk=768)
    dx, dw_attn = _rms_bwd(x, small["attn_norm_w"], dh1, name="rms1_bwd", width=D_MODEL, res=dx1)

    if exchange:
        big = {"w_in": d_win, "w_uq": r_uq, "w_ukv": r_ukv, "w_out": r_out, "w_gate": r_gate, "w_up": r_up, "w_down": r_down}
    else:
        big = {"w_in": d_win, "w_uq": d_wuq, "w_ukv": d_wukv, "w_out": d_wout, "w_gate": d_wgate, "w_up": d_wup,
               "w_down": d_wdown}
    sm = {"attn_norm_w": dw_attn, "ffn_norm_w": dw_ffn, "final_norm_w": dw_final, "q_norm_w": dw_qn, "kv_norm_w": dw_kvn,
          "gdn_norm_w": dw_gdn, "mla_out_norm_w": dw_mla_out, "gdn_params": d_params, "conv_w": dconv, "loss": loss_part}
    return dx, big, sm


def _exchange(arrays, gather, *, name):
    ex = _Exchange(arrays, gather)

    def body(*refs):
        ins, outs, sems = refs[:ex.n], refs[ex.n:2 * ex.n], refs[2 * ex.n:]
        ex.start(ins, outs, sems)
        ex.wait(ins, outs, sems)

    any_spec = pl.BlockSpec(memory_space=pl.ANY)
    return pl.pallas_call(body, name=name, in_specs=[any_spec] * ex.n, out_specs=[any_spec] * ex.n,
                          out_shape=ex.out_shape(), scratch_shapes=ex.sems())(*arrays)


def _adamw_math(g, w, m, v):
    m = ADAM_B1 * m + (1.0 - ADAM_B1) * g
    v = ADAM_B2 * v + (1.0 - ADAM_B2) * (g * g)
    m_hat = m / (1.0 - ADAM_B1 ** ADAM_STEP)
    v_hat = v / (1.0 - ADAM_B2 ** ADAM_STEP)
    delta = -ADAM_LR * (m_hat / (jnp.sqrt(v_hat) + ADAM_EPS) + ADAM_WD * w)
    return delta, m, v


def _adamw(parts, w, m, v, *, name):
    npart, r, c = parts.shape
    tr = r if r * c * 4 <= (1 << 20) else _rows(r, c, 1 << 20)

    def body(p_ref, w_ref, m_ref, v_ref, g_ref, d_ref, nm_ref, nv_ref):
        g = p_ref[0].astype(F32)
        for s in range(1, npart):
            g = g + p_ref[s].astype(F32)
        g_ref[...] = g
        d_ref[...], nm_ref[...], nv_ref[...] = _adamw_math(g, w_ref[...], m_ref[...], v_ref[...])

    blk = pl.BlockSpec((tr, c), lambda i: (i, 0))
    sds = jax.ShapeDtypeStruct((r, c), F32)
    return _pcall(body, name=name, grid=(r // tr,),
                  in_specs=[pl.BlockSpec((npart, tr, c), lambda i: (0, i, 0)), blk, blk, blk],
                  out_specs=[blk] * 4, out_shape=[sds] * 4)(parts, w, m, v)


def _sum_parts(parts, *, name):
    npart, r, c = parts.shape

    def body(p_ref, o_ref):
        g = p_ref[0]
        for s in range(1, npart):
            g = g + p_ref[s]
        o_ref[...] = g

    return _pcall(body, name=name, grid=(1,), in_specs=[pl.BlockSpec((npart, r, c), lambda i: (0, 0, 0))],
                  out_specs=pl.BlockSpec((r, c), lambda i: (0, 0)), out_shape=jax.ShapeDtypeStruct((r, c), F32))(parts)


_SMALL = (("attn_norm_w", D_MODEL), ("ffn_norm_w", D_MODEL), ("final_norm_w", D_MODEL), ("q_norm_w", Q_LORA),
          ("kv_norm_w", KV_LORA), ("gdn_norm_w", HEAD), ("mla_out_norm_w", HEAD), ("a_log", HEAD), ("dt_bias", HEAD))
_SMALL_ROWS = sum(n for _, n in _SMALL) // HEAD
_CONV_ROWS = GDN_CONV * CONV_CH // HEAD
_PACK_ROWS = 160


def _pad_lanes(v, n):
    v = v.reshape(-1)
    return jnp.concatenate([v, jnp.zeros((n - v.shape[0],), v.dtype)])


def kernel(x, positions, attn_norm_w, w_in, conv_w, a_log, dt_bias, gdn_norm_w, q_norm_w, w_uq, kv_norm_w, w_ukv, mla_out_norm_w, w_out, ffn_norm_w, w_gate, w_up, w_down, final_norm_w, loss_target, m_attn_norm_w, m_w_in, m_conv_w, m_a_log, m_dt_bias, m_gdn_norm_w, m_q_norm_w, m_w_uq, m_kv_norm_w, m_w_ukv, m_mla_out_norm_w, m_w_out, m_ffn_norm_w, m_w_gate, m_w_up, m_w_down, m_final_norm_w, v_attn_norm_w, v_w_in, v_conv_w, v_a_log, v_dt_bias, v_gdn_norm_w, v_q_norm_w, v_w_uq, v_kv_norm_w, v_w_ukv, v_mla_out_norm_w, v_w_out, v_ffn_norm_w, v_w_gate, v_w_up, v_w_down, v_final_norm_w):
    t = x.shape[1]
    me = 4 * lax.axis_index("x") + 2 * lax.axis_index("y") + lax.axis_index("c")
    weights = dict(attn_norm_w=attn_norm_w, w_in=w_in, conv_w=conv_w, a_log=a_log, dt_bias=dt_bias, gdn_norm_w=gdn_norm_w,
                   q_norm_w=q_norm_w, w_uq=w_uq, kv_norm_w=kv_norm_w, w_ukv=w_ukv, mla_out_norm_w=mla_out_norm_w, w_out=w_out,
                   ffn_norm_w=ffn_norm_w, w_gate=w_gate, w_up=w_up, w_down=w_down, final_norm_w=final_norm_w)
    mom_m = dict(attn_norm_w=m_attn_norm_w, w_in=m_w_in, conv_w=m_conv_w, a_log=m_a_log, dt_bias=m_dt_bias, gdn_norm_w=m_gdn_norm_w,
                 q_norm_w=m_q_norm_w, w_uq=m_w_uq, kv_norm_w=m_kv_norm_w, w_ukv=m_w_ukv, mla_out_norm_w=m_mla_out_norm_w,
                 w_out=m_w_out, ffn_norm_w=m_ffn_norm_w, w_gate=m_w_gate, w_up=m_w_up, w_down=m_w_down, final_norm_w=m_final_norm_w)
    mom_v = dict(attn_norm_w=v_attn_norm_w, w_in=v_w_in, conv_w=v_conv_w, a_log=v_a_log, dt_bias=v_dt_bias, gdn_norm_w=v_gdn_norm_w,
                 q_norm_w=v_q_norm_w, w_uq=v_w_uq, kv_norm_w=v_kv_norm_w, w_ukv=v_w_ukv, mla_out_norm_w=v_mla_out_norm_w,
                 w_out=v_w_out, ffn_norm_w=v_ffn_norm_w, w_gate=v_w_gate, w_up=v_w_up, w_down=v_w_down, final_norm_w=v_final_norm_w)
    big_names = ("w_in", "w_uq", "w_ukv", "w_out", "w_gate", "w_up", "w_down")

    shard = {n: weights[n][0].astype(BF16) for n in big_names}
    g_in, g_uq, g_ukv, g_conv = _exchange([shard["w_in"], shard["w_uq"], shard["w_ukv"], weights["conv_w"][0]], [True] * 4,
                                          name="gather_weights")
    win_p = _win_to_padded(_unshard_cols(g_in))
    wuq_p = _wuq_to_padded(_unshard_cols(g_uq))
    wukv = _unshard_cols(g_ukv)
    late = [shard["w_out"], shard["w_gate"], shard["w_up"], shard["w_down"]]
    conv_full = jnp.concatenate([_unshard_cols(g_conv), jnp.zeros((8 - GDN_CONV, CONV_CH), F32)], axis=0)

    gdn_params = jnp.concatenate([_pad_lanes(a_log, HEAD)[None], _pad_lanes(dt_bias, HEAD)[None], jnp.zeros((6, HEAD), F32)], axis=0)
    small = {n: weights[n].reshape(1, -1) for n in ("attn_norm_w", "ffn_norm_w", "final_norm_w", "q_norm_w", "kv_norm_w",
                                                    "gdn_norm_w", "mla_out_norm_w")}
    small["gdn_params"] = gdn_params

    dx, big, sm = _local_step(x[0], positions.reshape(t, 1).astype(F32), loss_target[0], win_p, wuq_p, wukv, late,
                              conv_full, small, True)

    rows8 = lambda name: jnp.sum(sm[name], axis=0)
    pieces = [rows8(n) for n, _ in _SMALL[:7]]
    pieces += [_pad_lanes(jnp.sum(sm["gdn_params"][0:1], axis=0), HEAD), _pad_lanes(jnp.sum(sm["gdn_params"][1:2], axis=0), HEAD)]
    pieces.append(jnp.sum(sm["conv_w"], axis=1).reshape(-1))
    pieces.append(_pad_lanes(jnp.sum(sm["loss"]).reshape(1), HEAD))
    packed = _pad_lanes(jnp.concatenate(pieces), _PACK_ROWS * HEAD).reshape(_PACK_ROWS, HEAD)
    (r_small,) = _exchange([packed], [True], name="exchange_small")

    outs_g, outs_d, outs_m, outs_v = {}, {}, {}, {}
    for name in big_names:
        g, d, nm, nv = _adamw(big[name], weights[name][0], mom_m[name][0], mom_v[name][0], name="adamw_" + name)
        outs_g[name], outs_d[name], outs_m[name], outs_v[name] = g[None], d[None], nm[None], nv[None]

    total = _sum_parts(r_small, name="sum_small")
    flat = total.reshape(-1)
    loss = flat[(_SMALL_ROWS + _CONV_ROWS) * HEAD]
    g_small, off = {}, 0
    for n, size in _SMALL:
        g_small[n] = flat[off:off + size]
        off += size
    g_conv_full = flat[off:off + GDN_CONV * CONV_CH].reshape(GDN_CONV, CONV_CH)
    g_small["conv_w"] = lax.dynamic_slice(g_conv_full, (0, me * (CONV_CH // N_DEV)), (GDN_CONV, CONV_CH // N_DEV)).reshape(-1)
    order = [n for n, _ in _SMALL] + ["conv_w"]
    sizes = dict(_SMALL)
    sizes["conv_w"] = GDN_CONV * CONV_CH // N_DEV
    true_size = {n: weights[n].size for n in order}

    def pack(d):
        return jnp.concatenate([_pad_lanes(d[n], sizes[n]) for n in order]).reshape(1, -1, HEAD)

    g2, d2, m2, v2 = _adamw(pack(g_small), pack(weights)[0], pack(mom_m)[0], pack(mom_v)[0], name="adamw_small")
    off = 0
    for n in order:
        for src, dst in ((g2, outs_g), (d2, outs_d), (m2, outs_m), (v2, outs_v)):
            dst[n] = src.reshape(-1)[off:off + true_size[n]].reshape(weights[n].shape)
        off += sizes[n]

    names = ("attn_norm_w", "w_in", "conv_w", "a_log", "dt_bias", "gdn_norm_w", "q_norm_w", "w_uq", "kv_norm_w", "w_ukv",
             "mla_out_norm_w", "w_out", "ffn_norm_w", "w_gate", "w_up", "w_down", "final_norm_w")
    return (loss, dx[None], *[outs_g[n] for n in names], *[outs_d[n] for n in names], *[outs_m[n] for n in names],
            *[outs_v[n] for n in names])
```

```python
import functools
import math

import jax
import jax.numpy as jnp
from jax import lax
from jax.experimental import pallas as pl
from jax.experimental.pallas import tpu as pltpu

F32 = jnp.float32
BF16 = jnp.bfloat16

D_MODEL = 2048
GDN_HEADS = 8
HEAD = 128
GDN_CONV = 4
GDN_CHUNK = 64
GDN_QK = GDN_HEADS * HEAD
CONV_CH = 3 * GDN_QK
MLA_HEADS = 8
QK_ROPE = 64
Q_LORA = 512
KV_LORA = 512
ROPE_THETA = 10000.0
D_FF = 5632
EPS = 1e-6
IN_WIDTH = 5200
ADAM_LR, ADAM_B1, ADAM_B2, ADAM_EPS, ADAM_WD, ADAM_STEP = 0.001, 0.9, 0.999, 1e-08, 0.01, 10

PROJ_W = 5376
COL_Z = 3072
COL_CQ = 4096
COL_CKV = 4608
COL_MISC = 5120
LANE_B = 64
LANE_A = 72
QHEAD = 256
FF_WIDE = D_FF // 4
N_DEV = 8
MESH = pl.DeviceIdType.MESH
VMEM_LIMIT_MB = 48

NN = ((1,), (0,))
NT = ((1,), (1,))
TN = ((0,), (0,))


def _my_place():
    x, y, c = lax.axis_index("x"), lax.axis_index("y"), lax.axis_index("c")
    return x, y, c, 4 * x + 2 * y + c


def _peer(x, y, c, p):
    px, py, pc = x ^ ((p >> 2) & 1), y ^ ((p >> 1) & 1), c ^ (p & 1)
    return (px, py, pc), 4 * px + 2 * py + pc


class _Exchange:
    def __init__(self, arrays, gather):
        self.arrays, self.gather, self.n = list(arrays), list(gather), len(arrays)

    def out_shape(self):
        return [jax.ShapeDtypeStruct(((N_DEV,) + a.shape) if g else a.shape, a.dtype)
                for a, g in zip(self.arrays, self.gather)]

    def sems(self):
        return [pltpu.SemaphoreType.DMA((self.n * (N_DEV - 1),)), pltpu.SemaphoreType.DMA((self.n * (N_DEV - 1),)),
                pltpu.SemaphoreType.DMA((self.n,))]

    def _copies(self, ins, outs, sems):
        send_sems, recv_sems, local_sems = sems
        x, y, c, me = _my_place()
        local = [pltpu.make_async_copy(ins[k] if self.gather[k] else ins[k].at[me], outs[k].at[me], local_sems.at[k])
                 for k in range(self.n)]
        sent, received = [], []
        for p in range(1, N_DEV):
            place, num = _peer(x, y, c, p)
            for k in range(self.n):
                src = ins[k] if self.gather[k] else ins[k].at[num]
                idx = k * (N_DEV - 1) + p - 1
                mk = lambda dst: pltpu.make_async_remote_copy(src_ref=src, dst_ref=dst, send_sem=send_sems.at[idx],
                                                              recv_sem=recv_sems.at[idx], device_id=place, device_id_type=MESH)
                sent.append(mk(outs[k].at[me]))
                received.append(mk(outs[k].at[num]))
        return local, sent, received

    def start(self, ins, outs, sems):
        local, sent, _ = self._copies(ins, outs, sems)
        for cp in local + sent:
            cp.start()

    def wait(self, ins, outs, sems):
        local, sent, received = self._copies(ins, outs, sems)
        for cp in received:
            cp.wait_recv()
        for cp in sent:
            cp.wait_send()
        for cp in local:
            cp.wait()


def _pcall(body, *, name, grid, in_specs, out_specs, out_shape, scratch=(), carry=None):
    params = pltpu.CompilerParams(dimension_semantics=("arbitrary",) * len(grid), vmem_limit_bytes=VMEM_LIMIT_MB << 20)
    if carry is None:
        return pl.pallas_call(body, name=name, grid=grid, in_specs=in_specs, out_specs=out_specs, out_shape=out_shape,
                              scratch_shapes=list(scratch), compiler_params=params)
    single = not isinstance(out_specs, (list, tuple))
    out_specs = [out_specs] if single else list(out_specs)
    out_shape = [out_shape] if single else list(out_shape)
    n_in, n_out, n_scr, na = len(in_specs), len(out_specs), len(scratch), carry.n

    def wrapped(*refs):
        ins, cin = refs[:n_in], refs[n_in:n_in + na]
        outs, cout = refs[n_in + na:n_in + na + n_out], refs[n_in + na + n_out:n_in + 2 * na + n_out]
        scr, sems = refs[n_in + 2 * na + n_out:n_in + 2 * na + n_out + n_scr], refs[n_in + 2 * na + n_out + n_scr:]
        first = functools.reduce(lambda a, b: a & b, [pl.program_id(d) == 0 for d in range(len(grid))])
        last = functools.reduce(lambda a, b: a & b, [pl.program_id(d) == grid[d] - 1 for d in range(len(grid))])

        @pl.when(first)
        def _():
            carry.start(cin, cout, sems)

        body(*ins, *outs, *scr)

        @pl.when(last)
        def _():
            carry.wait(cin, cout, sems)

    any_spec = pl.BlockSpec(memory_space=pl.ANY)
    call = pl.pallas_call(wrapped, name=name, grid=grid, in_specs=list(in_specs) + [any_spec] * na,
                          out_specs=out_specs + [any_spec] * na, out_shape=out_shape + carry.out_shape(),
                          scratch_shapes=list(scratch) + carry.sems(), compiler_params=params)

    def run(*args):
        res = call(*args, *carry.arrays)
        main = res[0] if single else list(res[:n_out])
        return main, list(res[n_out:])

    return run


def _pick(dim, pref):
    if dim <= pref:
        return dim
    c = pref
    while c >= 128:
        if dim % c == 0 and c % 128 == 0:
            return c
        c -= 128
    return dim


def _rows(t, width, target_bytes=2 << 20):
    r = max(8, min(t, target_bytes // (4 * width)))
    r = 1 << (r.bit_length() - 1)
    while t % r:
        r //= 2
    return r


MM_FULL_K = 2048


def _mm(a, b, *, name, ta=False, tb=False, res=None, out_dtype=F32, bm=1024, bn=1024, bk=1024, carry=None):
    m, k = (a.shape[1], a.shape[0]) if ta else a.shape
    n = b.shape[0] if tb else b.shape[1]
    assert (b.shape[1] if tb else b.shape[0]) == k
    bm, bn, bk = _pick(m, bm), _pick(n, bn), (k if k <= MM_FULL_K else _pick(k, bk))
    nk = k // bk
    dims = (((0,) if ta else (1,), (1,) if tb else (0,)), ((), ()))

    def body(*refs):
        a_ref, b_ref = refs[:2]
        r_ref = refs[2] if res is not None else None
        o_ref = refs[3] if res is not None else refs[2]
        part = lax.dot_general(a_ref[...].astype(BF16), b_ref[...].astype(BF16), dims, preferred_element_type=F32)

        def finish(out):
            if res is not None:
                out = out + r_ref[...]
            o_ref[...] = out.astype(o_ref.dtype)

        if nk == 1:
            finish(part)
            return
        acc_ref = refs[-1]
        kk = pl.program_id(2)

        @pl.when(kk == 0)
        def _():
            acc_ref[...] = part

        @pl.when((kk > 0) & (kk < nk - 1))
        def _():
            acc_ref[...] += part

        @pl.when(kk == nk - 1)
        def _():
            finish(acc_ref[...] + part)

    a_spec = pl.BlockSpec((bk, bm), lambda i, j, kk: (kk, i)) if ta else pl.BlockSpec((bm, bk), lambda i, j, kk: (i, kk))
    b_spec = pl.BlockSpec((bn, bk), lambda i, j, kk: (j, kk)) if tb else pl.BlockSpec((bk, bn), lambda i, j, kk: (kk, j))
    o_spec = pl.BlockSpec((bm, bn), lambda i, j, kk: (i, j))
    ins, specs = [a, b], [a_spec, b_spec]
    if res is not None:
        ins.append(res)
        specs.append(o_spec)
    return _pcall(body, name=name, grid=(m // bm, n // bn, nk), in_specs=specs, out_specs=o_spec,
                  out_shape=jax.ShapeDtypeStruct((m, n), out_dtype),
                  scratch=[pltpu.VMEM((bm, bn), F32)] if nk > 1 else [], carry=carry)(*ins)


def _rms_fwd(x, w, *, name, width, heads=1, col0=0, out_dtype=BF16):
    t = x.shape[0]
    tm = _rows(t, width)
    cb = col0 // width

    def body(x_ref, w_ref, o_ref):
        xv = x_ref[...]
        r = lax.rsqrt(jnp.mean(xv * xv, axis=-1, keepdims=True) + EPS)
        o_ref[...] = (xv * r * w_ref[...]).astype(o_ref.dtype)

    return _pcall(body, name=name, grid=(t // tm, heads),
                  in_specs=[pl.BlockSpec((tm, width), lambda i, h: (i, cb + h)),
                            pl.BlockSpec((1, width), lambda i, h: (0, 0))],
                  out_specs=pl.BlockSpec((tm, width), lambda i, h: (i, h)),
                  out_shape=jax.ShapeDtypeStruct((t, heads * width), out_dtype))(x, w)


def _rms_bwd(x, w, dy, *, name, width, heads=1, col0=0, dcol0=0, res=None, out_dtype=F32, with_delta=False):
    t = x.shape[0]
    tm = _rows(t, width)
    cb, dcb = col0 // width, dcol0 // width

    def body(*refs):
        refs = list(refs)
        x_ref, w_ref, dy_ref = refs[:3]
        r_ref = refs[3] if res is not None else None
        outs = refs[4:] if res is not None else refs[3:]
        dx_ref, dw_ref = outs[:2]
        xv = x_ref[...]
        dyv = dy_ref[...].astype(F32)
        r = lax.rsqrt(jnp.mean(xv * xv, axis=-1, keepdims=True) + EPS)
        xh = xv * r
        dyw = dyv * w_ref[...]
        dx = r * (dyw - xh * jnp.mean(dyw * xh, axis=-1, keepdims=True))
        if with_delta:
            outs[2][...] = jnp.broadcast_to(jnp.sum(dx * xv, axis=-1, keepdims=True), dx.shape)
        if res is not None:
            dx = dx + r_ref[...]
        dx_ref[...] = dx.astype(dx_ref.dtype)

        @pl.when((pl.program_id(0) == 0) & (pl.program_id(1) == 0))
        def _():
            dw_ref[...] = jnp.zeros_like(dw_ref)

        dw_ref[...] += (dyv * xh).reshape(tm // 8, 8, width).sum(axis=0)

    blk = pl.BlockSpec((tm, width), lambda i, h: (i, h))
    ins = [x, w, dy]
    specs = [pl.BlockSpec((tm, width), lambda i, h: (i, cb + h)), pl.BlockSpec((1, width), lambda i, h: (0, 0)),
             pl.BlockSpec((tm, width), lambda i, h: (i, dcb + h))]
    if res is not None:
        ins.append(res)
        specs.append(blk)
    out_shape = [jax.ShapeDtypeStruct((t, heads * width), out_dtype), jax.ShapeDtypeStruct((8, width), F32)]
    out_specs = [blk, pl.BlockSpec((8, width), lambda i, h: (0, 0))]
    if with_delta:
        out_shape.append(jax.ShapeDtypeStruct((t, heads * width), F32))
        out_specs.append(blk)
    return _pcall(body, name=name, grid=(t // tm, heads), in_specs=specs, out_specs=out_specs, out_shape=out_shape)(*ins)


def _sig(x):
    return 1.0 / (1.0 + jnp.exp(-x))


@jax.custom_vjp
def _sigmoid(x):
    return _sig(x)


def _sigmoid_fwd(x):
    s = _sig(x)
    return s, s


def _sigmoid_bwd(s, g):
    return (g * s * (1.0 - s),)


_sigmoid.defvjp(_sigmoid_fwd, _sigmoid_bwd)


@jax.custom_vjp
def _softplus(x):
    return jnp.maximum(x, 0.0) + jnp.log(1.0 + jnp.exp(-jnp.abs(x)))


def _softplus_fwd(x):
    return _softplus(x), x


def _softplus_bwd(x, g):
    return (g * _sig(x),)


_softplus.defvjp(_softplus_fwd, _softplus_bwd)


def _silu(x):
    return x * _sig(x)


def _dsilu(x):
    s = _sig(x)
    return s * (1.0 + x * (1.0 - s))


NN3 = (((2,), (1,)), ((0,), (0,)))
NT3 = (((2,), (2,)), ((0,), (0,)))
TN3 = (((1,), (1,)), ((0,), (0,)))


def _bdot(a, b, dims):
    return lax.dot_general(a.astype(BF16), b.astype(BF16), dims, preferred_element_type=F32)


def _bf16_part(x):
    bits = lax.bitcast_convert_type(x, jnp.uint32) & jnp.uint32(0xFFFF0000)
    return lax.bitcast_convert_type(bits, F32)


def _mask_dot(m, x, dims, mask_first):
    mb = m.astype(BF16)
    hi = _bf16_part(x)
    r1 = x - hi
    mid = _bf16_part(r1)
    lo = r1 - mid
    dot = (lambda p: lax.dot_general(mb, p.astype(BF16), dims, preferred_element_type=F32)) if mask_first else \
          (lambda p: lax.dot_general(p.astype(BF16), mb, dims, preferred_element_type=F32))
    return dot(hi) + (dot(mid) + dot(lo))


def _dot3(a, b, dims):
    a_hi, b_hi = _bf16_part(a), _bf16_part(b)
    a_lo, b_lo = (a - a_hi).astype(BF16), (b - b_hi).astype(BF16)
    a_hi, b_hi = a_hi.astype(BF16), b_hi.astype(BF16)
    dot = lambda x, y: lax.dot_general(x, y, dims, preferred_element_type=F32)
    return dot(a_hi, b_hi) + (dot(a_hi, b_lo) + dot(a_lo, b_hi))


@jax.custom_vjp
def _mask_nn(m, x):
    return _mask_dot(m, x, NN3, True)


_mask_nn.defvjp(lambda m, x: (_mask_dot(m, x, NN3, True), m),
                lambda m, g: (jnp.zeros_like(m), _mask_dot(m, g, TN3, True)))


@jax.custom_vjp
def _mask_tn(x, m):
    return _mask_dot(m, x, TN3, False)


_mask_tn.defvjp(lambda x, m: (_mask_dot(m, x, TN3, False), m),
                lambda m, g: (_mask_dot(m, g, NT3, True), jnp.zeros_like(m)))


@jax.custom_vjp
def _nn_hi(a, b):
    return _dot3(a, b, NN3)


_nn_hi.defvjp(lambda a, b: (_dot3(a, b, NN3), (a, b)), lambda r, g: (_dot3(g, r[1], NT3), _dot3(r[0], g, TN3)))


@jax.custom_vjp
def _nn(a, b):
    return _bdot(a, b, NN3)


_nn.defvjp(lambda a, b: (_bdot(a, b, NN3), (a, b)), lambda r, g: (_bdot(g, r[1], NT3), _bdot(r[0], g, TN3)))


@jax.custom_vjp
def _nt(a, b):
    return _bdot(a, b, NT3)


_nt.defvjp(lambda a, b: (_bdot(a, b, NT3), (a, b)), lambda r, g: (_bdot(g, r[1], NN3), _bdot(g, r[0], TN3)))


@jax.custom_vjp
def _tn(a, b):
    return _bdot(a, b, TN3)


_tn.defvjp(lambda a, b: (_bdot(a, b, TN3), (a, b)), lambda r, g: (_bdot(r[1], g, NT3), _bdot(r[0], g, NN3)))


def _conv_pre(ext, w, rows):
    acc = w[0:1] * ext[5:5 + rows]
    for j in range(1, GDN_CONV):
        acc = acc + w[j:j + 1] * ext[5 + j:5 + j + rows]
    return acc


def _conv_fwd(proj, conv_w, *, name):
    t = proj.shape[0]
    tm, tc = _pick(t, 512), 512
    nb = tm // 8

    def body(u_ref, p_ref, w_ref, o_ref):
        i = pl.program_id(1)
        prev = jnp.where(i > 0, p_ref[...], 0.0)
        ext = jnp.concatenate([prev, u_ref[...]], axis=0)
        o_ref[...] = _silu(_conv_pre(ext, w_ref[...], tm))

    return _pcall(body, name=name, grid=(CONV_CH // tc, t // tm),
                  in_specs=[pl.BlockSpec((tm, tc), lambda j, i: (i, j)),
                            pl.BlockSpec((8, tc), lambda j, i: (jnp.maximum(i * nb - 1, 0), j)),
                            pl.BlockSpec((8, tc), lambda j, i: (0, j))],
                  out_specs=pl.BlockSpec((tm, tc), lambda j, i: (i, j)),
                  out_shape=jax.ShapeDtypeStruct((t, CONV_CH), F32))(proj, proj, conv_w)


def _conv_bwd(proj, conv_w, dy, *, name):
    t = proj.shape[0]
    tm, tc = _pick(t, 512), 512
    nb = tm // 8
    last = t // tm - 1

    def body(u_ref, p_ref, n_ref, dy_ref, dyn_ref, w_ref, du_ref, dw_ref):
        i = pl.program_id(1)
        w = w_ref[...]
        prev = jnp.where(i > 0, p_ref[...], 0.0)
        ext = jnp.concatenate([prev, u_ref[...], n_ref[...]], axis=0)
        c = _conv_pre(ext, w, tm + 8)
        dy_ext = jnp.concatenate([dy_ref[...], jnp.where(i < last, dyn_ref[...], 0.0)], axis=0)
        dc = dy_ext * _dsilu(c)
        du = w[3:4] * dc[0:tm]
        for j in range(GDN_CONV - 1):
            du = du + w[j:j + 1] * dc[3 - j:3 - j + tm]
        du_ref[...] = du.astype(du_ref.dtype)

        @pl.when(i == 0)
        def _():
            dw_ref[...] = jnp.zeros_like(dw_ref)

        for j in range(GDN_CONV):
            dw_ref[j] += (dc[0:tm] * ext[5 + j:5 + j + tm]).reshape(nb, 8, tc).sum(axis=0)

    cur = lambda j, i: (i, j)
    return _pcall(body, name=name, grid=(CONV_CH // tc, t // tm),
                  in_specs=[pl.BlockSpec((tm, tc), cur),
                            pl.BlockSpec((8, tc), lambda j, i: (jnp.maximum(i * nb - 1, 0), j)),
                            pl.BlockSpec((8, tc), lambda j, i: (jnp.minimum((i + 1) * nb, t // 8 - 1), j)),
                            pl.BlockSpec((tm, tc), cur),
                            pl.BlockSpec((8, tc), lambda j, i: (jnp.minimum((i + 1) * nb, t // 8 - 1), j)),
                            pl.BlockSpec((8, tc), lambda j, i: (0, j))],
                  out_specs=[pl.BlockSpec((tm, tc), cur), pl.BlockSpec((GDN_CONV, 8, tc), lambda j, i: (0, 0, j))],
                  out_shape=[jax.ShapeDtypeStruct((t, CONV_CH), BF16), jax.ShapeDtypeStruct((GDN_CONV, 8, CONV_CH), F32)],
                  )(proj, proj, proj, dy, dy, conv_w)


def _gdn_chunk(q_raw, k_raw, v, misc, params, state):
    nh, c = q_raw.shape[0], q_raw.shape[1]
    lane = lax.broadcasted_iota(jnp.int32, misc.shape, 1)
    prow = lax.broadcasted_iota(jnp.int32, params.shape, 0)
    plane = lax.broadcasted_iota(jnp.int32, params.shape, 1)
    heads = lambda pieces: jnp.concatenate([p[None] for p in pieces], axis=0)
    col = lambda at: heads([jnp.sum(jnp.where(lane == at + h, misc, 0.0), axis=1, keepdims=True) for h in range(nh)])
    par = lambda row: heads([jnp.sum(jnp.where((prow == row) & (plane == h), params, 0.0), keepdims=True)
                             for h in range(nh)])
    b_raw, a_raw = col(LANE_B), col(LANE_A)
    a_log, dt_bias = par(0), par(1)
    beta = _sigmoid(b_raw)
    g = -jnp.exp(a_log) * _softplus(a_raw + dt_bias)

    q = q_raw * lax.rsqrt(jnp.sum(q_raw * q_raw, axis=-1, keepdims=True) + EPS) * (HEAD ** -0.5)
    k = k_raw * lax.rsqrt(jnp.sum(k_raw * k_raw, axis=-1, keepdims=True) + EPS)

    ri = lax.broadcasted_iota(jnp.int32, (c, c), 0)
    ci = lax.broadcasted_iota(jnp.int32, (c, c), 1)
    tril, strict = ri >= ci, ri > ci
    batch = lambda m: jnp.broadcast_to(m.astype(F32), (nh, c, c))
    g_b = jnp.broadcast_to(g, (nh, c, c))
    gc_col = _mask_nn(batch(tril), g_b)
    gc_row = _mask_tn(g_b, batch(ri <= ci))
    gc = jnp.sum(jnp.where(ci == 0, gc_col, 0.0), axis=2, keepdims=True)
    decay = jnp.exp(jnp.where(tril, gc_col - gc_row, -1e30))

    kb = k * beta
    vb = v * beta
    a_mat = jnp.where(strict, _nt(kb, k) * decay, 0.0)
    x = -a_mat
    inv = (ri == ci).astype(F32) + x
    for _ in range(5):
        x = _nn_hi(x, x)
        inv = inv + _nn_hi(inv, x)
    u = _nn_hi(inv, vb)
    w = _nn_hi(inv, kb * jnp.exp(gc))
    intra = _nt(q, k) * decay

    v_new = u - _nn(w, state)
    o = _nn(q * jnp.exp(gc), state) + _nn(intra, v_new)
    g_last = jnp.sum(g, axis=1, keepdims=True)
    k_dec = k * jnp.exp(g_last - gc)
    new_state = state * jnp.exp(g_last) + _tn(k_dec, v_new)
    return o, new_state


def _gdn_specs(nc, rev):
    cidx = (lambda n: nc - 1 - n) if rev else (lambda n: n)
    hb = lambda part: pl.BlockSpec((GDN_CHUNK, GDN_QK), lambda n: (cidx(n), part))
    misc = pl.BlockSpec((GDN_CHUNK, HEAD), lambda n: (cidx(n), COL_MISC // HEAD))
    params = pl.BlockSpec((8, HEAD), lambda n: (0, 0))
    hist = pl.BlockSpec((1, GDN_HEADS, HEAD, HEAD), lambda n: (cidx(n), 0, 0, 0))
    return hb, misc, params, hist


def _split_heads(v):
    return jnp.stack([v[:, h * HEAD:(h + 1) * HEAD] for h in range(v.shape[1] // HEAD)])


def _merge_heads(v):
    return jnp.concatenate([v[h] for h in range(v.shape[0])], axis=1)


def _gdn_fwd(qkv, proj, params, *, name):
    t = qkv.shape[0]
    nc = t // GDN_CHUNK
    hb, misc, pspec, hist = _gdn_specs(nc, False)

    def body(q_ref, k_ref, v_ref, m_ref, p_ref, o_ref, hist_ref, s_ref):
        @pl.when(pl.program_id(0) == 0)
        def _():
            s_ref[...] = jnp.zeros_like(s_ref)

        state = s_ref[...]
        hist_ref[0] = state
        o, new_state = _gdn_chunk(_split_heads(q_ref[...]), _split_heads(k_ref[...]), _split_heads(v_ref[...]),
                                  m_ref[...], p_ref[...], state)
        o_ref[...] = _merge_heads(o)
        s_ref[...] = new_state

    return _pcall(body, name=name, grid=(nc,),
                  in_specs=[hb(0), hb(1), hb(2), misc, pspec],
                  out_specs=[hb(0), hist],
                  out_shape=[jax.ShapeDtypeStruct((t, GDN_QK), F32),
                             jax.ShapeDtypeStruct((nc, GDN_HEADS, HEAD, HEAD), F32)],
                  scratch=[pltpu.VMEM((GDN_HEADS, HEAD, HEAD), F32)])(qkv, qkv, qkv, proj, params)


def _gdn_bwd(qkv, proj, params, hist_arr, do, dmisc_in, *, name, carry=None):
    t = qkv.shape[0]
    nc = t // GDN_CHUNK
    hb, misc, pspec, hist = _gdn_specs(nc, True)
    mrow = pl.BlockSpec((GDN_CHUNK, HEAD), lambda n: (nc - 1 - n, 0))

    def body(q_ref, k_ref, v_ref, m_ref, p_ref, hist_ref, do_ref, dmi_ref,
             dq_ref, dk_ref, dv_ref, dm_ref, dp_ref, ds_ref):
        @pl.when(pl.program_id(0) == 0)
        def _():
            ds_ref[...] = jnp.zeros_like(ds_ref)
            dp_ref[...] = jnp.zeros_like(dp_ref)

        _, vjp = jax.vjp(_gdn_chunk, _split_heads(q_ref[...]), _split_heads(k_ref[...]), _split_heads(v_ref[...]),
                         m_ref[...], p_ref[...], hist_ref[0])
        dq, dk, dv, dm, dp, ds = vjp((_split_heads(do_ref[...]), ds_ref[...]))
        dq_ref[...] = _merge_heads(dq)
        dk_ref[...] = _merge_heads(dk)
        dv_ref[...] = _merge_heads(dv)
        ds_ref[...] = ds
        dm_ref[...] = dmi_ref[...] + dm
        dp_ref[...] += dp

    hd = jax.ShapeDtypeStruct((t, GDN_QK), F32)
    return _pcall(body, name=name, grid=(nc,),
                  in_specs=[hb(0), hb(1), hb(2), misc, pspec, hist, hb(0), mrow],
                  out_specs=[hb(0), hb(0), hb(0), mrow, pspec],
                  out_shape=[hd, hd, hd, jax.ShapeDtypeStruct((t, HEAD), F32), jax.ShapeDtypeStruct((8, HEAD), F32)],
                  scratch=[pltpu.VMEM((GDN_HEADS, HEAD, HEAD), F32)], carry=carry,
                  )(qkv, qkv, qkv, proj, params, hist_arr, do, dmisc_in)


def _gate_fwd(o_raw, proj, w, *, name):
    t = o_raw.shape[0]
    tm = _rows(t, HEAD)
    zb = COL_Z // HEAD

    def body(o_ref, z_ref, w_ref, out_ref):
        ov = o_ref[...]
        r = lax.rsqrt(jnp.mean(ov * ov, axis=-1, keepdims=True) + EPS)
        out_ref[...] = (ov * r * w_ref[...] * _silu(z_ref[...])).astype(out_ref.dtype)

    blk = pl.BlockSpec((tm, HEAD), lambda i, h: (i, h))
    return _pcall(body, name=name, grid=(t // tm, GDN_HEADS),
                  in_specs=[blk, pl.BlockSpec((tm, HEAD), lambda i, h: (i, zb + h)), pl.BlockSpec((1, HEAD), lambda i, h: (0, 0))],
                  out_specs=blk, out_shape=jax.ShapeDtypeStruct((t, GDN_QK), BF16))(o_raw, proj, w)


def _gate_bwd(o_raw, proj, w, dmixed, *, name):
    t = o_raw.shape[0]
    tm = _rows(t, HEAD)
    zb = COL_Z // HEAD

    def body(o_ref, z_ref, w_ref, dy_ref, do_ref, dz_ref, dw_ref):
        ov, zv, dyv = o_ref[...], z_ref[...], dy_ref[...]
        r = lax.rsqrt(jnp.mean(ov * ov, axis=-1, keepdims=True) + EPS)
        xh = ov * r
        dn = dyv * _silu(zv)
        dz_ref[...] = (dyv * xh * w_ref[...] * _dsilu(zv)).astype(dz_ref.dtype)
        dnw = dn * w_ref[...]
        do_ref[...] = r * (dnw - xh * jnp.mean(dnw * xh, axis=-1, keepdims=True))

        @pl.when((pl.program_id(0) == 0) & (pl.program_id(1) == 0))
        def _():
            dw_ref[...] = jnp.zeros_like(dw_ref)

        dw_ref[...] += (dn * xh).reshape(tm // 8, 8, HEAD).sum(axis=0)

    blk = pl.BlockSpec((tm, HEAD), lambda i, h: (i, h))
    return _pcall(body, name=name, grid=(t // tm, GDN_HEADS),
                  in_specs=[blk, pl.BlockSpec((tm, HEAD), lambda i, h: (i, zb + h)), pl.BlockSpec((1, HEAD), lambda i, h: (0, 0)), blk],
                  out_specs=[blk, blk, pl.BlockSpec((8, HEAD), lambda i, h: (0, 0))],
                  out_shape=[jax.ShapeDtypeStruct((t, GDN_QK), F32), jax.ShapeDtypeStruct((t, GDN_QK), BF16),
                             jax.ShapeDtypeStruct((8, HEAD), F32)])(o_raw, proj, w, dmixed)


def _rope_tables():
    half = QK_ROPE // 2
    inv = ROPE_THETA ** (-jnp.arange(half, dtype=F32) / half)
    zeros = jnp.zeros((HEAD - QK_ROPE,), F32)
    inv_row = jnp.concatenate([inv, inv, zeros])
    sign_row = jnp.concatenate([-jnp.ones((half,), F32), jnp.ones((half,), F32), zeros])
    mask_row = jnp.concatenate([jnp.ones((QK_ROPE,), F32), zeros])
    return jnp.concatenate([inv_row[None], sign_row[None], mask_row[None], jnp.zeros((5, HEAD), F32)], axis=0)


def _rope_cs(pos, tab, *, name):
    t = pos.shape[0]
    tm = _pick(t, 1024)

    def body(pos_ref, tab_ref, o_ref):
        tab = tab_ref[...]
        ang = pos_ref[...] * tab[0:1]
        o_ref[...] = jnp.concatenate([jnp.cos(ang) * tab[2:3], jnp.sin(ang) * tab[1:2]], axis=1)

    return _pcall(body, name=name, grid=(t // tm,),
                  in_specs=[pl.BlockSpec((tm, 1), lambda i: (i, 0)), pl.BlockSpec((8, HEAD), lambda i: (0, 0))],
                  out_specs=pl.BlockSpec((tm, 2 * HEAD), lambda i: (i, 0)),
                  out_shape=jax.ShapeDtypeStruct((t, 2 * HEAD), F32))(pos, tab)


def _rotate(x, cs, sign):
    lane = lax.broadcasted_iota(jnp.int32, x.shape, 1)
    half = QK_ROPE // 2
    partner = jnp.where(lane < half, pltpu.roll(x, HEAD - half, axis=1), pltpu.roll(x, half, axis=1))
    return x * cs[:, :HEAD] + partner * (cs[:, HEAD:] * sign)


def _q_rot(q, cs, *, name, sign, out_dtype=BF16):
    t = q.shape[0]
    tm = _pick(t, 1024)
    scale = (HEAD + QK_ROPE) ** -0.5

    def body(q_ref, cs_ref, o_ref):
        qv = q_ref[...].astype(F32)
        rot = _rotate(qv[:, HEAD:], cs_ref[...], sign)
        o_ref[...] = (jnp.concatenate([qv[:, :HEAD], rot], axis=1) * scale).astype(o_ref.dtype)

    blk = pl.BlockSpec((tm, QHEAD), lambda i, h: (i, h))
    return _pcall(body, name=name, grid=(t // tm, MLA_HEADS),
                  in_specs=[blk, pl.BlockSpec((tm, 2 * HEAD), lambda i, h: (i, 0))],
                  out_specs=blk, out_shape=jax.ShapeDtypeStruct((t, MLA_HEADS * QHEAD), out_dtype))(q, cs)


def _kv_prep(kv, proj, cs, *, name):
    t = kv.shape[0]
    tm = _pick(t, 1024)

    def body(kv_ref, m_ref, cs_ref, k_ref, v_ref):
        kvv = kv_ref[...]
        misc = m_ref[...]
        lane = lax.broadcasted_iota(jnp.int32, misc.shape, 1)
        rot = _rotate(jnp.where(lane < QK_ROPE, misc, 0.0), cs_ref[...], 1.0)
        k_ref[...] = jnp.concatenate([kvv[:, :HEAD], rot], axis=1).astype(k_ref.dtype)
        v_ref[...] = kvv[:, HEAD:].astype(v_ref.dtype)

    blk = pl.BlockSpec((tm, QHEAD), lambda i, h: (i, h))
    return _pcall(body, name=name, grid=(t // tm, MLA_HEADS),
                  in_specs=[blk, pl.BlockSpec((tm, HEAD), lambda i, h: (i, COL_MISC // HEAD)),
                            pl.BlockSpec((tm, 2 * HEAD), lambda i, h: (i, 0))],
                  out_specs=[blk, pl.BlockSpec((tm, HEAD), lambda i, h: (i, h))],
                  out_shape=[jax.ShapeDtypeStruct((t, MLA_HEADS * QHEAD), BF16), jax.ShapeDtypeStruct((t, MLA_HEADS * HEAD), BF16)],
                  )(kv, proj, cs)


def _krope_bwd(dkr, cs, *, name):
    t = dkr.shape[0]
    tm = _pick(t, 512)

    def body(d_ref, cs_ref, o_ref):
        d = d_ref[...]
        acc = d[:, :HEAD]
        for h in range(1, MLA_HEADS):
            acc = acc + d[:, h * HEAD:(h + 1) * HEAD]
        o_ref[...] = _rotate(acc, cs_ref[...], -1.0)

    return _pcall(body, name=name, grid=(t // tm,),
                  in_specs=[pl.BlockSpec((tm, MLA_HEADS * HEAD), lambda i: (i, 0)), pl.BlockSpec((tm, 2 * HEAD), lambda i: (i, 0))],
                  out_specs=pl.BlockSpec((tm, HEAD), lambda i: (i, 0)),
                  out_shape=jax.ShapeDtypeStruct((t, HEAD), F32))(dkr, cs)


NEG = -1e30


def _tri(step, counts):
    starts = [sum(counts[:o]) for o in range(len(counts))]
    outer = sum([(step >= s).astype(jnp.int32) for s in starts[1:]], jnp.int32(0))
    start = sum([(step >= starts[o]).astype(jnp.int32) * (starts[o] - starts[o - 1]) for o in range(1, len(counts))], jnp.int32(0))
    return outer, step - start


def _attn_fwd(q, k, v, *, name, tq=1024, tk=1024, carry=None):
    t = q.shape[0]
    tq, tk = _pick(t, tq), _pick(t, tk)
    nq = t // tq
    last_kv = lambda i: (i * tq + tq - 1) // tk
    counts = [last_kv(i) + 1 for i in range(nq)]

    def body(q_ref, k_ref, v_ref, o_ref, lse_ref, m_ref, l_ref, acc_ref):
        i, j = _tri(pl.program_id(1), counts)

        @pl.when(j == 0)
        def _():
            m_ref[...] = jnp.full_like(m_ref, NEG)
            l_ref[...] = jnp.zeros_like(l_ref)
            acc_ref[...] = jnp.zeros_like(acc_ref)

        def step(masked):
            s = lax.dot_general(q_ref[...], k_ref[...], (NT, ((), ())), preferred_element_type=F32)
            if masked:
                qpos = i * tq + lax.broadcasted_iota(jnp.int32, s.shape, 0)
                kpos = j * tk + lax.broadcasted_iota(jnp.int32, s.shape, 1)
                s = jnp.where(kpos <= qpos, s, NEG)
            m_prev = m_ref[...]
            m_new = jnp.maximum(m_prev, jnp.max(s, axis=1, keepdims=True))
            alpha = jnp.exp(m_prev - m_new)
            p = jnp.exp(s - m_new)
            l_ref[...] = alpha * l_ref[...] + jnp.sum(p, axis=1, keepdims=True)
            acc_ref[...] = alpha * acc_ref[...] + lax.dot_general(p.astype(BF16), v_ref[...], (NN, ((), ())),
                                                                  preferred_element_type=F32)
            m_ref[...] = m_new

        crosses = j * tk + tk - 1 > i * tq

        @pl.when(crosses)
        def _():
            step(True)

        @pl.when(jnp.logical_not(crosses))
        def _():
            step(False)

        @pl.when(j == last_kv(i))
        def _():
            o_ref[...] = acc_ref[...] / l_ref[...]
            lse_ref[...] = jnp.broadcast_to(m_ref[...] + jnp.log(l_ref[...]), lse_ref.shape)

    qblk = pl.BlockSpec((tq, QHEAD), lambda h, s: (_tri(s, counts)[0], h))
    oblk = pl.BlockSpec((tq, HEAD), lambda h, s: (_tri(s, counts)[0], h))
    return _pcall(body, name=name, grid=(MLA_HEADS, sum(counts)),
                  in_specs=[qblk, pl.BlockSpec((tk, QHEAD), lambda h, s: (_tri(s, counts)[1], h)),
                            pl.BlockSpec((tk, HEAD), lambda h, s: (_tri(s, counts)[1], h))],
                  out_specs=[oblk, oblk],
                  out_shape=[jax.ShapeDtypeStruct((t, MLA_HEADS * HEAD), F32), jax.ShapeDtypeStruct((t, MLA_HEADS * HEAD), F32)],
                  scratch=[pltpu.VMEM((tq, 1), F32), pltpu.VMEM((tq, 1), F32), pltpu.VMEM((tq, HEAD), F32)],
                  carry=carry)(q, k, v)


def _attn_bwd(q, k, v, do, lse, delta, *, name, tq=512, tk=512, carry=None):
    t = q.shape[0]
    tq, tk = _pick(t, tq), _pick(t, tk)
    nq, nk = t // tq, t // tk
    first_q = lambda j: (j * tk) // tq
    counts = [nq - first_q(j) for j in range(nk)]

    def where(step):
        j, off = _tri(step, counts)
        return j, first_q(j) + off

    lanes = lambda col: jnp.tile(col, (1, tk // HEAD))

    def body(q_ref, k_ref, v_ref, do_ref, lse_ref, dl_ref, dq_ref, dkv_ref, dkr_ref, dk_acc, dv_acc):
        j, i = where(pl.program_id(1))

        @pl.when(i == first_q(j))
        def _():
            dk_acc[...] = jnp.zeros_like(dk_acc)
            dv_acc[...] = jnp.zeros_like(dv_acc)

        def step(masked):
            qv, kv_, dov = q_ref[...], k_ref[...], do_ref[...].astype(BF16)
            s = lax.dot_general(qv, kv_, (NT, ((), ())), preferred_element_type=F32)
            p = jnp.exp((s - lanes(lse_ref[...])).astype(BF16))
            if masked:
                qpos = i * tq + lax.broadcasted_iota(jnp.int32, s.shape, 0)
                kpos = j * tk + lax.broadcasted_iota(jnp.int32, s.shape, 1)
                p = jnp.where(kpos <= qpos, p, jnp.zeros_like(p))
            dv_acc[...] += lax.dot_general(p, dov, (TN, ((), ())), preferred_element_type=F32)
            dp = lax.dot_general(dov, v_ref[...], (NT, ((), ())), preferred_element_type=F32)
            ds = p * (dp - lanes(dl_ref[...])).astype(BF16)
            dk_acc[...] += lax.dot_general(ds, qv, (TN, ((), ())), preferred_element_type=F32)
            contrib = lax.dot_general(ds, kv_, (NN, ((), ())), preferred_element_type=F32)
            rows = pl.ds(pl.multiple_of(i * tq, tq), tq)

            @pl.when(j == 0)
            def _():
                dq_ref[rows, :] = contrib

            @pl.when(j > 0)
            def _():
                dq_ref[rows, :] += contrib

        crosses = j * tk + tk - 1 > i * tq

        @pl.when(crosses)
        def _():
            step(True)

        @pl.when(jnp.logical_not(crosses))
        def _():
            step(False)

        @pl.when(i == nq - 1)
        def _():
            dk = dk_acc[...]
            dkv_ref[...] = jnp.concatenate([dk[:, :HEAD], dv_acc[...]], axis=1).astype(dkv_ref.dtype)
            dkr_ref[...] = dk[:, HEAD:]

    qi = lambda h, s: (where(s)[1], h)
    kj = lambda h, s: (where(s)[0], h)
    return _pcall(body, name=name, grid=(MLA_HEADS, sum(counts)),
                  in_specs=[pl.BlockSpec((tq, QHEAD), qi), pl.BlockSpec((tk, QHEAD), kj), pl.BlockSpec((tk, HEAD), kj),
                            pl.BlockSpec((tq, HEAD), qi), pl.BlockSpec((tq, HEAD), qi), pl.BlockSpec((tq, HEAD), qi)],
                  out_specs=[pl.BlockSpec((t, QHEAD), lambda h, s: (0, h)), pl.BlockSpec((tk, QHEAD), kj),
                             pl.BlockSpec((tk, HEAD), kj)],
                  out_shape=[jax.ShapeDtypeStruct((t, MLA_HEADS * QHEAD), F32), jax.ShapeDtypeStruct((t, MLA_HEADS * QHEAD), BF16),
                             jax.ShapeDtypeStruct((t, MLA_HEADS * HEAD), F32)],
                  scratch=[pltpu.VMEM((tk, QHEAD), F32), pltpu.VMEM((tk, HEAD), F32)], carry=carry)(q, k, v, do, lse, delta)


def _ffn_up(h, wgate, wup, *, name, bm=512, bn=FF_WIDE):
    t = h.shape[0]
    bm = _pick(t, bm)

    def body(h_ref, wg_ref, wu_ref, g_ref, u_ref, a_ref):
        hv = h_ref[...]
        g = lax.dot_general(hv, wg_ref[...], (NN, ((), ())), preferred_element_type=F32)
        u = lax.dot_general(hv, wu_ref[...], (NN, ((), ())), preferred_element_type=F32)
        g_ref[...] = g.astype(g_ref.dtype)
        u_ref[...] = u.astype(u_ref.dtype)
        a_ref[...] = (_silu(g) * u).astype(a_ref.dtype)

    w_spec = pl.BlockSpec((D_MODEL, bn), lambda j, i: (0, j))
    o_spec = pl.BlockSpec((bm, bn), lambda j, i: (i, j))
    sds = jax.ShapeDtypeStruct((t, D_FF), BF16)
    return _pcall(body, name=name, grid=(D_FF // bn, t // bm),
                  in_specs=[pl.BlockSpec((bm, D_MODEL), lambda j, i: (i, 0)), w_spec, w_spec],
                  out_specs=[o_spec] * 3, out_shape=[sds] * 3)(h, wgate, wup)


def _ffn_down_dx(dy, wdown, gate, up, *, name, bm=512, bn=FF_WIDE):
    t = dy.shape[0]
    bm = _pick(t, bm)

    def body(dy_ref, w_ref, g_ref, u_ref, dg_ref, du_ref):
        d = lax.dot_general(dy_ref[...].astype(BF16), w_ref[...], (NT, ((), ())), preferred_element_type=F32)
        g = g_ref[...].astype(F32)
        dg_ref[...] = (d * u_ref[...].astype(F32) * _dsilu(g)).astype(dg_ref.dtype)
        du_ref[...] = (d * _silu(g)).astype(du_ref.dtype)

    o_spec = pl.BlockSpec((bm, bn), lambda j, i: (i, j))
    sds = jax.ShapeDtypeStruct((t, D_FF), BF16)
    return _pcall(body, name=name, grid=(D_FF // bn, t // bm),
                  in_specs=[pl.BlockSpec((bm, D_MODEL), lambda j, i: (i, 0)), pl.BlockSpec((bn, D_MODEL), lambda j, i: (j, 0)),
                            o_spec, o_spec],
                  out_specs=[o_spec, o_spec], out_shape=[sds, sds])(dy, wdown, gate, up)


def _loss_bwd(x2, w, target, *, name):
    t = x2.shape[0]
    tm = _rows(t, D_MODEL)

    def body(x_ref, w_ref, t_ref, dx_ref, dw_ref, l_ref):
        xv, wv = x_ref[...], w_ref[...]
        r = lax.rsqrt(jnp.mean(xv * xv, axis=-1, keepdims=True) + EPS)
        xh = xv * r
        err = xh * wv - t_ref[...]
        dy = err * (1.0 / D_MODEL)
        dyw = dy * wv
        dx_ref[...] = r * (dyw - xh * jnp.mean(dyw * xh, axis=-1, keepdims=True))

        @pl.when(pl.program_id(0) == 0)
        def _():
            dw_ref[...] = jnp.zeros_like(dw_ref)
            l_ref[...] = jnp.zeros_like(l_ref)

        dw_ref[...] += (dy * xh).reshape(tm // 8, 8, D_MODEL).sum(axis=0)
        sq = (err * err).reshape(tm // 8, 8, D_MODEL).sum(axis=0)
        part = sq[:, :HEAD]
        for c in range(1, D_MODEL // HEAD):
            part = part + sq[:, c * HEAD:(c + 1) * HEAD]
        l_ref[...] += part * (0.5 / D_MODEL)

    row = pl.BlockSpec((tm, D_MODEL), lambda i: (i, 0))
    return _pcall(body, name=name, grid=(t // tm,),
                  in_specs=[row, pl.BlockSpec((1, D_MODEL), lambda i: (0, 0)), row],
                  out_specs=[row, pl.BlockSpec((8, D_MODEL), lambda i: (0, 0)), pl.BlockSpec((8, HEAD), lambda i: (0, 0))],
                  out_shape=[jax.ShapeDtypeStruct((t, D_MODEL), F32), jax.ShapeDtypeStruct((8, D_MODEL), F32),
                             jax.ShapeDtypeStruct((8, HEAD), F32)])(x2, w, target)


def _unshard_cols(g):
    return jnp.transpose(g, (1, 0, 2)).reshape(g.shape[1], N_DEV * g.shape[2])


def _shard_cols(w):
    return jnp.transpose(w.reshape(w.shape[0], N_DEV, w.shape[1] // N_DEV), (1, 0, 2))


def _win_to_padded(w):
    pad = jnp.zeros((w.shape[0], PROJ_W - IN_WIDTH), w.dtype)
    return jnp.concatenate([w[:, :4096], w[:, 4112:5136], w[:, 5136:5200], w[:, 4096:4112], pad], axis=1)


def _win_from_padded(d):
    return jnp.concatenate([d[:, :4096], d[:, 5184:5200], d[:, 4096:5120], d[:, 5120:5184]], axis=1)


def _wuq_to_padded(w):
    w3 = w.reshape(w.shape[0], MLA_HEADS, HEAD + QK_ROPE)
    return jnp.pad(w3, ((0, 0), (0, 0), (0, QHEAD - HEAD - QK_ROPE))).reshape(w.shape[0], MLA_HEADS * QHEAD)


def _wuq_from_padded(d):
    return d.reshape(d.shape[0], MLA_HEADS, QHEAD)[:, :, :HEAD + QK_ROPE].reshape(d.shape[0], MLA_HEADS * (HEAD + QK_ROPE))


def _late_weights(g_out, g_gate, g_up, g_down):
    return g_out.reshape(D_MODEL, D_MODEL), _unshard_cols(g_gate), _unshard_cols(g_up), g_down.reshape(D_FF, D_MODEL)


def _local_step(x, pos, target, win_p, wuq_p, wukv, late, conv_w, small, exchange):
    cs = _rope_cs(pos, _rope_tables(), name="rope_cs")
    if not exchange:
        wout, wgate, wup, wdown = late
    h1 = _rms_fwd(x, small["attn_norm_w"], name="rms1_fwd", width=D_MODEL)
    proj = _mm(h1, win_p, name="mm_in", bn=768)
    qkv = _conv_fwd(proj, conv_w, name="conv_fwd")
    o_gdn_raw, hist = _gdn_fwd(qkv, proj, small["gdn_params"], name="gdn_fwd")
    o_gdn = _gate_fwd(o_gdn_raw, proj, small["gdn_norm_w"], name="gate_fwd")
    cqn = _rms_fwd(proj, small["q_norm_w"], name="rmsq_fwd", width=Q_LORA, col0=COL_CQ)
    ckvn = _rms_fwd(proj, small["kv_norm_w"], name="rmskv_fwd", width=KV_LORA, col0=COL_CKV)
    q_pre = _mm(cqn, wuq_p, name="mm_uq")
    kv = _mm(ckvn, wukv, name="mm_ukv")
    q_full = _q_rot(q_pre, cs, name="q_rot", sign=1.0)
    k_full, v_b = _kv_prep(kv, proj, cs, name="kv_prep")
    if exchange:
        (o_mla_raw, lse), gathered = _attn_fwd(q_full, k_full, v_b, name="attn_fwd", carry=_Exchange(late, [True] * 4))
        wout, wgate, wup, wdown = _late_weights(*gathered)
    else:
        o_mla_raw, lse = _attn_fwd(q_full, k_full, v_b, name="attn_fwd")
    o_mla = _rms_fwd(o_mla_raw, small["mla_out_norm_w"], name="rmso_fwd", width=HEAD, heads=MLA_HEADS)
    mixed = jnp.concatenate([o_gdn, o_mla], axis=1)
    x1 = _mm(mixed, wout, name="mm_out", res=x)
    h2 = _rms_fwd(x1, small["ffn_norm_w"], name="rms2_fwd", width=D_MODEL)
    gate, up, act = _ffn_up(h2, wgate, wup, name="ffn_up")
    x2 = _mm(act, wdown, name="mm_down", res=x1, bk=FF_WIDE)
    dx2, dw_final, loss_part = _loss_bwd(x2, small["final_norm_w"], target, name="loss_bwd")
    dgate, dup = _ffn_down_dx(dx2, wdown, gate, up, name="ffn_down_dx")
    d_wdown = _mm(act, dx2, name="mm_down_dw", ta=True, out_dtype=BF16, bm=FF_WIDE)
    dh2 = _mm(dgate, wgate, name="mm_gate_dx", tb=True, bk=FF_WIDE)
    dh2 = _mm(dup, wup, name="mm_up_dx", tb=True, res=dh2, bk=FF_WIDE)
    d_wgate = _mm(h2, dgate, name="mm_gate_dw", ta=True, out_dtype=BF16, bn=FF_WIDE)
    d_wup = _mm(h2, dup, name="mm_up_dw", ta=True, out_dtype=BF16, bn=FF_WIDE)
    dx1, dw_ffn = _rms_bwd(x1, small["ffn_norm_w"], dh2, name="rms2_bwd", width=D_MODEL, res=dx2)
    dmixed = _mm(dx1, wout, name="mm_out_dx", tb=True)
    d_wout = _mm(mixed, dx1, name="mm_out_dw", ta=True, out_dtype=BF16)
    do_mla, dw_mla_out, delta = _rms_bwd(o_mla_raw, small["mla_out_norm_w"], dmixed, name="rmso_bwd", width=HEAD,
                                         heads=MLA_HEADS, dcol0=GDN_QK, with_delta=True)
    if exchange:
        send = [d_wdown.reshape(N_DEV, D_FF // N_DEV, D_MODEL), _shard_cols(d_wgate), _shard_cols(d_wup)]
        (dq_full, dkv, dkr_h), (r_down, r_gate, r_up) = _attn_bwd(q_full, k_full, v_b, do_mla, lse, delta, name="attn_bwd",
                                                                  carry=_Exchange(send, [False] * 3))
    else:
        dq_full, dkv, dkr_h = _attn_bwd(q_full, k_full, v_b, do_mla, lse, delta, name="attn_bwd")
    dq_pre = _q_rot(dq_full, cs, name="q_rot_bwd", sign=-1.0)
    dmisc_kr = _krope_bwd(dkr_h, cs, name="krope_bwd")
    dcqn = _mm(dq_pre, wuq_p, name="mm_uq_dx", tb=True)
    d_wuq = _mm(cqn, dq_pre, name="mm_uq_dw", ta=True, out_dtype=BF16)
    dckvn = _mm(dkv, wukv, name="mm_ukv_dx", tb=True)
    d_wukv = _mm(ckvn, dkv, name="mm_ukv_dw", ta=True, out_dtype=BF16)
    dcq, dw_qn = _rms_bwd(proj, small["q_norm_w"], dcqn, name="rmsq_bwd", width=Q_LORA, col0=COL_CQ, out_dtype=BF16)
    dckv, dw_kvn = _rms_bwd(proj, small["kv_norm_w"], dckvn, name="rmskv_bwd", width=KV_LORA, col0=COL_CKV, out_dtype=BF16)
    do_gdn, dz, dw_gdn = _gate_bwd(o_gdn_raw, proj, small["gdn_norm_w"], dmixed, name="gate_bwd")
    if exchange:
        send = [d_wout.reshape(N_DEV, D_MODEL // N_DEV, D_MODEL), _shard_cols(_wuq_from_padded(d_wuq)), _shard_cols(d_wukv)]
        (dgq, dgk, dgv, dmisc, d_params), (r_out, r_uq, r_ukv) = _gdn_bwd(
            qkv, proj, small["gdn_params"], hist, do_gdn, dmisc_kr, name="gdn_bwd", carry=_Exchange(send, [False] * 3))
    else:
        dgq, dgk, dgv, dmisc, d_params = _gdn_bwd(qkv, proj, small["gdn_params"], hist, do_gdn, dmisc_kr, name="gdn_bwd")
    dqkv_pre, dconv = _conv_bwd(proj, conv_w, jnp.concatenate([dgq, dgk, dgv], axis=1), name="conv_bwd")
    dproj = jnp.concatenate([dqkv_pre, dz, dcq, dckv, dmisc.astype(BF16), jnp.zeros((x.shape[0], PROJ_W - COL_MISC - HEAD), BF16)], axis=1)
    d_win = _mm(h1, dproj, name="mm_in_dw", ta=True, out_dtype=BF16, bn=768)
    if exchange:
        dh1, (r_in,) = _mm(dproj, win_p, name="mm_in_dx", tb=True, bk=768,
                           carry=_Exchange([_shard_cols(_win_from_padded(d_win))], [False]))
        d_win = r_in
    else:
        dh1 = _mm(dproj, win_p, name="mm_in_dx", tb=True, bk=768)
    dx, dw_attn = _rms_bwd(x, small["attn_norm_w"], dh1, name="rms1_bwd", width=D_MODEL, res=dx1)

    if exchange:
        big = {"w_in": d_win, "w_uq": r_uq, "w_ukv": r_ukv, "w_out": r_out, "w_gate": r_gate, "w_up": r_up, "w_down": r_down}
    else:
        big = {"w_in": d_win, "w_uq": d_wuq, "w_ukv": d_wukv, "w_out": d_wout, "w_gate": d_wgate, "w_up": d_wup,
               "w_down": d_wdown}
    sm = {"attn_norm_w": dw_attn, "ffn_norm_w": dw_ffn, "final_norm_w": dw_final, "q_norm_w": dw_qn, "kv_norm_w": dw_kvn,
          "gdn_norm_w": dw_gdn, "mla_out_norm_w": dw_mla_out, "gdn_params": d_params, "conv_w": dconv, "loss": loss_part}
    return dx, big, sm


def _exchange(arrays, gather, *, name):
    ex = _Exchange(arrays, gather)

    def body(*refs):
        ins, outs, sems = refs[:ex.n], refs[ex.n:2 * ex.n], refs[2 * ex.n:]
        ex.start(ins, outs, sems)
        ex.wait(ins, outs, sems)

    any_spec = pl.BlockSpec(memory_space=pl.ANY)
    return pl.pallas_call(body, name=name, in_specs=[any_spec] * ex.n, out_specs=[any_spec] * ex.n,
                          out_shape=ex.out_shape(), scratch_shapes=ex.sems())(*arrays)


def _adamw_math(g, w, m, v):
    m = ADAM_B1 * m + (1.0 - ADAM_B1) * g
    v = ADAM_B2 * v + (1.0 - ADAM_B2) * (g * g)
    m_hat = m / (1.0 - ADAM_B1 ** ADAM_STEP)
    v_hat = v / (1.0 - ADAM_B2 ** ADAM_STEP)
    delta = -ADAM_LR * (m_hat / (jnp.sqrt(v_hat) + ADAM_EPS) + ADAM_WD * w)
    return delta, m, v


def _adamw(parts, w, m, v, *, name):
    npart, r, c = parts.shape
    tr = r if r * c * 4 <= (1 << 20) else _rows(r, c, 1 << 20)

    def body(p_ref, w_ref, m_ref, v_ref, g_ref, d_ref, nm_ref, nv_ref):
        g = p_ref[0].astype(F32)
        for s in range(1, npart):
            g = g + p_ref[s].astype(F32)
        g_ref[...] = g
        d_ref[...], nm_ref[...], nv_ref[...] = _adamw_math(g, w_ref[...], m_ref[...], v_ref[...])

    blk = pl.BlockSpec((tr, c), lambda i: (i, 0))
    sds = jax.ShapeDtypeStruct((r, c), F32)
    return _pcall(body, name=name, grid=(r // tr,),
                  in_specs=[pl.BlockSpec((npart, tr, c), lambda i: (0, i, 0)), blk, blk, blk],
                  out_specs=[blk] * 4, out_shape=[sds] * 4)(parts, w, m, v)


def _sum_parts(parts, *, name):
    npart, r, c = parts.shape

    def body(p_ref, o_ref):
        g = p_ref[0]
        for s in range(1, npart):
            g = g + p_ref[s]
        o_ref[...] = g

    return _pcall(body, name=name, grid=(1,), in_specs=[pl.BlockSpec((npart, r, c), lambda i: (0, 0, 0))],
                  out_specs=pl.BlockSpec((r, c), lambda i: (0, 0)), out_shape=jax.ShapeDtypeStruct((r, c), F32))(parts)


_SMALL = (("attn_norm_w", D_MODEL), ("ffn_norm_w", D_MODEL), ("final_norm_w", D_MODEL), ("q_norm_w", Q_LORA),
          ("kv_norm_w", KV_LORA), ("gdn_norm_w", HEAD), ("mla_out_norm_w", HEAD), ("a_log", HEAD), ("dt_bias", HEAD))
_SMALL_ROWS = sum(n for _, n in _SMALL) // HEAD
_CONV_ROWS = GDN_CONV * CONV_CH // HEAD
_PACK_ROWS = 160


def _pad_lanes(v, n):
    v = v.reshape(-1)
    return jnp.concatenate([v, jnp.zeros((n - v.shape[0],), v.dtype)])


def kernel(x, positions, attn_norm_w, w_in, conv_w, a_log, dt_bias, gdn_norm_w, q_norm_w, w_uq, kv_norm_w, w_ukv, mla_out_norm_w, w_out, ffn_norm_w, w_gate, w_up, w_down, final_norm_w, loss_target, m_attn_norm_w, m_w_in, m_conv_w, m_a_log, m_dt_bias, m_gdn_norm_w, m_q_norm_w, m_w_uq, m_kv_norm_w, m_w_ukv, m_mla_out_norm_w, m_w_out, m_ffn_norm_w, m_w_gate, m_w_up, m_w_down, m_final_norm_w, v_attn_norm_w, v_w_in, v_conv_w, v_a_log, v_dt_bias, v_gdn_norm_w, v_q_norm_w, v_w_uq, v_kv_norm_w, v_w_ukv, v_mla_out_norm_w, v_w_out, v_ffn_norm_w, v_w_gate, v_w_up, v_w_down, v_final_norm_w):
    t = x.shape[1]
    me = 4 * lax.axis_index("x") + 2 * lax.axis_index("y") + lax.axis_index("c")
    weights = dict(attn_norm_w=attn_norm_w, w_in=w_in, conv_w=conv_w, a_log=a_log, dt_bias=dt_bias, gdn_norm_w=gdn_norm_w,
                   q_norm_w=q_norm_w, w_uq=w_uq, kv_norm_w=kv_norm_w, w_ukv=w_ukv, mla_out_norm_w=mla_out_norm_w, w_out=w_out,
                   ffn_norm_w=ffn_norm_w, w_gate=w_gate, w_up=w_up, w_down=w_down, final_norm_w=final_norm_w)
    mom_m = dict(attn_norm_w=m_attn_norm_w, w_in=m_w_in, conv_w=m_conv_w, a_log=m_a_log, dt_bias=m_dt_bias, gdn_norm_w=m_gdn_norm_w,
                 q_norm_w=m_q_norm_w, w_uq=m_w_uq, kv_norm_w=m_kv_norm_w, w_ukv=m_w_ukv, mla_out_norm_w=m_mla_out_norm_w,
                 w_out=m_w_out, ffn_norm_w=m_ffn_norm_w, w_gate=m_w_gate, w_up=m_w_up, w_down=m_w_down, final_norm_w=m_final_norm_w)
    mom_v = dict(attn_norm_w=v_attn_norm_w, w_in=v_w_in, conv_w=v_conv_w, a_log=v_a_log, dt_bias=v_dt_bias, gdn_norm_w=v_gdn_norm_w,
                 q_norm_w=v_q_norm_w, w_uq=v_w_uq, kv_norm_w=v_kv_norm_w, w_ukv=v_w_ukv, mla_out_norm_w=v_mla_out_norm_w,
                 w_out=v_w_out, ffn_norm_w=v_ffn_norm_w, w_gate=v_w_gate, w_up=v_w_up, w_down=v_w_down, final_norm_w=v_final_norm_w)
    big_names = ("w_in", "w_uq", "w_ukv", "w_out", "w_gate", "w_up", "w_down")

    shard = {n: weights[n][0].astype(BF16) for n in big_names}
    g_in, g_uq, g_ukv, g_conv = _exchange([shard["w_in"], shard["w_uq"], shard["w_ukv"], weights["conv_w"][0]], [True] * 4,
                                          name="gather_weights")
    win_p = _win_to_padded(_unshard_cols(g_in))
    wuq_p = _wuq_to_padded(_unshard_cols(g_uq))
    wukv = _unshard_cols(g_ukv)
    late = [shard["w_out"], shard["w_gate"], shard["w_up"], shard["w_down"]]
    conv_full = jnp.concatenate([_unshard_cols(g_conv), jnp.zeros((8 - GDN_CONV, CONV_CH), F32)], axis=0)

    gdn_params = jnp.concatenate([_pad_lanes(a_log, HEAD)[None], _pad_lanes(dt_bias, HEAD)[None], jnp.zeros((6, HEAD), F32)], axis=0)
    small = {n: weights[n].reshape(1, -1) for n in ("attn_norm_w", "ffn_norm_w", "final_norm_w", "q_norm_w", "kv_norm_w",
                                                    "gdn_norm_w", "mla_out_norm_w")}
    small["gdn_params"] = gdn_params

    dx, big, sm = _local_step(x[0], positions.reshape(t, 1).astype(F32), loss_target[0], win_p, wuq_p, wukv, late,
                              conv_full, small, True)

    rows8 = lambda name: jnp.sum(sm[name], axis=0)
    pieces = [rows8(n) for n, _ in _SMALL[:7]]
    pieces += [_pad_lanes(jnp.sum(sm["gdn_params"][0:1], axis=0), HEAD), _pad_lanes(jnp.sum(sm["gdn_params"][1:2], axis=0), HEAD)]
    pieces.append(jnp.sum(sm["conv_w"], axis=1).reshape(-1))
    pieces.append(_pad_lanes(jnp.sum(sm["loss"]).reshape(1), HEAD))
    packed = _pad_lanes(jnp.concatenate(pieces), _PACK_ROWS * HEAD).reshape(_PACK_ROWS, HEAD)
    (r_small,) = _exchange([packed], [True], name="exchange_small")

    outs_g, outs_d, outs_m, outs_v = {}, {}, {}, {}
    for name in big_names:
        g, d, nm, nv = _adamw(big[name], weights[name][0], mom_m[name][0], mom_v[name][0], name="adamw_" + name)
        outs_g[name], outs_d[name], outs_m[name], outs_v[name] = g[None], d[None], nm[None], nv[None]

    total = _sum_parts(r_small, name="sum_small")
    flat = total.reshape(-1)
    loss = flat[(_SMALL_ROWS + _CONV_ROWS) * HEAD]
    g_small, off = {}, 0
    for n, size in _SMALL:
        g_small[n] = flat[off:off + size]
        off += size
    g_conv_full = flat[off:off + GDN_CONV * CONV_CH].reshape(GDN_CONV, CONV_CH)
    g_small["conv_w"] = lax.dynamic_slice(g_conv_full, (0, me * (CONV_CH // N_DEV)), (GDN_CONV, CONV_CH // N_DEV)).reshape(-1)
    order = [n for n, _ in _SMALL] + ["conv_w"]
    sizes = dict(_SMALL)
    sizes["conv_w"] = GDN_CONV * CONV_CH // N_DEV
    true_size = {n: weights[n].size for n in order}

    def pack(d):
        return jnp.concatenate([_pad_lanes(d[n], sizes[n]) for n in order]).reshape(1, -1, HEAD)

    g2, d2, m2, v2 = _adamw(pack(g_small), pack(weights)[0], pack(mom_m)[0], pack(mom_v)[0], name="adamw_small")
    off = 0
    for n in order:
        for src, dst in ((g2, outs_g), (d2, outs_d), (m2, outs_m), (v2, outs_v)):
            dst[n] = src.reshape(-1)[off:off + true_size[n]].reshape(weights[n].shape)
        off += sizes[n]

    names = ("attn_norm_w", "w_in", "conv_w", "a_log", "dt_bias", "gdn_norm_w", "q_norm_w", "w_uq", "kv_norm_w", "w_ukv",
             "mla_out_norm_w", "w_out", "ffn_norm_w", "w_gate", "w_up", "w_down", "final_norm_w")
    return (loss, dx[None], *[outs_g[n] for n in names], *[outs_d[n] for n in names], *[outs_m[n] for n in names],
            *[outs_v[n] for n in names])
```

```python
import functools
import math

import jax
import jax.numpy as jnp
from jax import lax
from jax.experimental import pallas as pl
from jax.experimental.pallas import tpu as pltpu

F32 = jnp.float32
BF16 = jnp.bfloat16

D_MODEL = 2048
GDN_HEADS = 8
HEAD = 128
GDN_CONV = 4
GDN_CHUNK = 64
GDN_QK = GDN_HEADS * HEAD
CONV_CH = 3 * GDN_QK
MLA_HEADS = 8
QK_ROPE = 64
Q_LORA = 512
KV_LORA = 512
ROPE_THETA = 10000.0
D_FF = 5632
EPS = 1e-6
IN_WIDTH = 5200
ADAM_LR, ADAM_B1, ADAM_B2, ADAM_EPS, ADAM_WD, ADAM_STEP = 0.001, 0.9, 0.999, 1e-08, 0.01, 10

PROJ_W = 5376
COL_Z = 3072
COL_CQ = 4096
COL_CKV = 4608
COL_MISC = 5120
LANE_B = 64
LANE_A = 72
QHEAD = 256
FF_WIDE = D_FF // 4
N_DEV = 8
MESH = pl.DeviceIdType.MESH
VMEM_LIMIT_MB = 48
CARRY_SPREAD = 2

NN = ((1,), (0,))
NT = ((1,), (1,))
TN = ((0,), (0,))


def _my_place():
    x, y, c = lax.axis_index("x"), lax.axis_index("y"), lax.axis_index("c")
    return x, y, c, 4 * x + 2 * y + c


def _peer(x, y, c, p):
    px, py, pc = x ^ ((p >> 2) & 1), y ^ ((p >> 1) & 1), c ^ (p & 1)
    return (px, py, pc), 4 * px + 2 * py + pc


class _Exchange:
    def __init__(self, arrays, gather):
        self.arrays, self.gather, self.n = list(arrays), list(gather), len(arrays)

    def out_shape(self):
        return [jax.ShapeDtypeStruct(((N_DEV,) + a.shape) if g else a.shape, a.dtype)
                for a, g in zip(self.arrays, self.gather)]

    def sems(self):
        return [pltpu.SemaphoreType.DMA((self.n * (N_DEV - 1),)), pltpu.SemaphoreType.DMA((self.n * (N_DEV - 1),)),
                pltpu.SemaphoreType.DMA((self.n,))]

    def _copies(self, ins, outs, sems):
        send_sems, recv_sems, local_sems = sems
        x, y, c, me = _my_place()
        local = [pltpu.make_async_copy(ins[k] if self.gather[k] else ins[k].at[me], outs[k].at[me], local_sems.at[k])
                 for k in range(self.n)]
        sent, received = [], []
        for p in range(1, N_DEV):
            place, num = _peer(x, y, c, p)
            for k in range(self.n):
                src = ins[k] if self.gather[k] else ins[k].at[num]
                idx = k * (N_DEV - 1) + p - 1
                mk = lambda dst: pltpu.make_async_remote_copy(src_ref=src, dst_ref=dst, send_sem=send_sems.at[idx],
                                                              recv_sem=recv_sems.at[idx], device_id=place, device_id_type=MESH)
                sent.append(mk(outs[k].at[me]))
                received.append(mk(outs[k].at[num]))
        return local, sent, received

    def start(self, ins, outs, sems, peer=None):
        local, sent, _ = self._copies(ins, outs, sems)
        if peer is None:
            chosen = local + sent
        else:
            chosen = (local if peer == 1 else []) + sent[(peer - 1) * self.n:peer * self.n]
        for cp in chosen:
            cp.start()

    def wait(self, ins, outs, sems):
        local, sent, received = self._copies(ins, outs, sems)
        for cp in received:
            cp.wait_recv()
        for cp in sent:
            cp.wait_send()
        for cp in local:
            cp.wait()


def _pcall(body, *, name, grid, in_specs, out_specs, out_shape, scratch=(), carry=None):
    params = pltpu.CompilerParams(dimension_semantics=("arbitrary",) * len(grid), vmem_limit_bytes=VMEM_LIMIT_MB << 20)
    if carry is None:
        return pl.pallas_call(body, name=name, grid=grid, in_specs=in_specs, out_specs=out_specs, out_shape=out_shape,
                              scratch_shapes=list(scratch), compiler_params=params)
    single = not isinstance(out_specs, (list, tuple))
    out_specs = [out_specs] if single else list(out_specs)
    out_shape = [out_shape] if single else list(out_shape)
    n_in, n_out, n_scr, na = len(in_specs), len(out_specs), len(scratch), carry.n
    total = math.prod(grid)

    def wrapped(*refs):
        ins, cin = refs[:n_in], refs[n_in:n_in + na]
        outs, cout = refs[n_in + na:n_in + na + n_out], refs[n_in + na + n_out:n_in + 2 * na + n_out]
        scr, sems = refs[n_in + 2 * na + n_out:n_in + 2 * na + n_out + n_scr], refs[n_in + 2 * na + n_out + n_scr:]
        last = functools.reduce(lambda a, b: a & b, [pl.program_id(d) == grid[d] - 1 for d in range(len(grid))])
        step = functools.reduce(lambda a, d: a * grid[d] + pl.program_id(d), range(len(grid)), 0)
        stride = max(1, total // CARRY_SPREAD // (N_DEV - 1))
        for p in range(1, N_DEV):
            @pl.when(step == min((p - 1) * stride, total - 1))
            def _(p=p):
                carry.start(cin, cout, sems, peer=p)

        body(*ins, *outs, *scr)

        @pl.when(last)
        def _():
            carry.wait(cin, cout, sems)

    any_spec = pl.BlockSpec(memory_space=pl.ANY)
    call = pl.pallas_call(wrapped, name=name, grid=grid, in_specs=list(in_specs) + [any_spec] * na,
                          out_specs=out_specs + [any_spec] * na, out_shape=out_shape + carry.out_shape(),
                          scratch_shapes=list(scratch) + carry.sems(), compiler_params=params)

    def run(*args):
        res = call(*args, *carry.arrays)
        main = res[0] if single else list(res[:n_out])
        return main, list(res[n_out:])

    return run


def _pick(dim, pref):
    if dim <= pref:
        return dim
    c = pref
    while c >= 128:
        if dim % c == 0 and c % 128 == 0:
            return c
        c -= 128
    return dim


def _rows(t, width, target_bytes=2 << 20):
    r = max(8, min(t, target_bytes // (4 * width)))
    r = 1 << (r.bit_length() - 1)
    while t % r:
        r //= 2
    return r


MM_FULL_K = 2048


def _mm(a, b, *, name, ta=False, tb=False, res=None, out_dtype=F32, bm=1024, bn=1024, bk=1024, carry=None):
    m, k = (a.shape[1], a.shape[0]) if ta else a.shape
    n = b.shape[0] if tb else b.shape[1]
    assert (b.shape[1] if tb else b.shape[0]) == k
    bm, bn, bk = _pick(m, bm), _pick(n, bn), (k if k <= MM_FULL_K else _pick(k, bk))
    nk = k // bk
    dims = (((0,) if ta else (1,), (1,) if tb else (0,)), ((), ()))

    def body(*refs):
        a_ref, b_ref = refs[:2]
        r_ref = refs[2] if res is not None else None
        o_ref = refs[3] if res is not None else refs[2]
        part = lax.dot_general(a_ref[...].astype(BF16), b_ref[...].astype(BF16), dims, preferred_element_type=F32)

        def finish(out):
            if res is not None:
                out = out + r_ref[...]
            o_ref[...] = out.astype(o_ref.dtype)

        if nk == 1:
            finish(part)
            return
        acc_ref = refs[-1]
        kk = pl.program_id(2)

        @pl.when(kk == 0)
        def _():
            acc_ref[...] = part

        @pl.when((kk > 0) & (kk < nk - 1))
        def _():
            acc_ref[...] += part

        @pl.when(kk == nk - 1)
        def _():
            finish(acc_ref[...] + part)

    a_spec = pl.BlockSpec((bk, bm), lambda i, j, kk: (kk, i)) if ta else pl.BlockSpec((bm, bk), lambda i, j, kk: (i, kk))
    b_spec = pl.BlockSpec((bn, bk), lambda i, j, kk: (j, kk)) if tb else pl.BlockSpec((bk, bn), lambda i, j, kk: (kk, j))
    o_spec = pl.BlockSpec((bm, bn), lambda i, j, kk: (i, j))
    ins, specs = [a, b], [a_spec, b_spec]
    if res is not None:
        ins.append(res)
        specs.append(o_spec)
    return _pcall(body, name=name, grid=(m // bm, n // bn, nk), in_specs=specs, out_specs=o_spec,
                  out_shape=jax.ShapeDtypeStruct((m, n), out_dtype),
                  scratch=[pltpu.VMEM((bm, bn), F32)] if nk > 1 else [], carry=carry)(*ins)


def _rms_fwd(x, w, *, name, width, heads=1, col0=0, out_dtype=BF16):
    t = x.shape[0]
    tm = _rows(t, width)
    cb = col0 // width

    def body(x_ref, w_ref, o_ref):
        xv = x_ref[...]
        r = lax.rsqrt(jnp.mean(xv * xv, axis=-1, keepdims=True) + EPS)
        o_ref[...] = (xv * r * w_ref[...]).astype(o_ref.dtype)

    return _pcall(body, name=name, grid=(t // tm, heads),
                  in_specs=[pl.BlockSpec((tm, width), lambda i, h: (i, cb + h)),
                            pl.BlockSpec((1, width), lambda i, h: (0, 0))],
                  out_specs=pl.BlockSpec((tm, width), lambda i, h: (i, h)),
                  out_shape=jax.ShapeDtypeStruct((t, heads * width), out_dtype))(x, w)


def _rms_bwd(x, w, dy, *, name, width, heads=1, col0=0, dcol0=0, res=None, out_dtype=F32, with_delta=False):
    t = x.shape[0]
    tm = _rows(t, width)
    cb, dcb = col0 // width, dcol0 // width

    def body(*refs):
        refs = list(refs)
        x_ref, w_ref, dy_ref = refs[:3]
        r_ref = refs[3] if res is not None else None
        outs = refs[4:] if res is not None else refs[3:]
        dx_ref, dw_ref = outs[:2]
        xv = x_ref[...]
        dyv = dy_ref[...].astype(F32)
        r = lax.rsqrt(jnp.mean(xv * xv, axis=-1, keepdims=True) + EPS)
        xh = xv * r
        dyw = dyv * w_ref[...]
        dx = r * (dyw - xh * jnp.mean(dyw * xh, axis=-1, keepdims=True))
        if with_delta:
            outs[2][...] = jnp.broadcast_to(jnp.sum(dx * xv, axis=-1, keepdims=True), dx.shape)
        if res is not None:
            dx = dx + r_ref[...]
        dx_ref[...] = dx.astype(dx_ref.dtype)

        @pl.when((pl.program_id(0) == 0) & (pl.program_id(1) == 0))
        def _():
            dw_ref[...] = jnp.zeros_like(dw_ref)

        dw_ref[...] += (dyv * xh).reshape(tm // 8, 8, width).sum(axis=0)

    blk = pl.BlockSpec((tm, width), lambda i, h: (i, h))
    ins = [x, w, dy]
    specs = [pl.BlockSpec((tm, width), lambda i, h: (i, cb + h)), pl.BlockSpec((1, width), lambda i, h: (0, 0)),
             pl.BlockSpec((tm, width), lambda i, h: (i, dcb + h))]
    if res is not None:
        ins.append(res)
        specs.append(blk)
    out_shape = [jax.ShapeDtypeStruct((t, heads * width), out_dtype), jax.ShapeDtypeStruct((8, width), F32)]
    out_specs = [blk, pl.BlockSpec((8, width), lambda i, h: (0, 0))]
    if with_delta:
        out_shape.append(jax.ShapeDtypeStruct((t, heads * width), F32))
        out_specs.append(blk)
    return _pcall(body, name=name, grid=(t // tm, heads), in_specs=specs, out_specs=out_specs, out_shape=out_shape)(*ins)


def _sig(x):
    return 1.0 / (1.0 + jnp.exp(-x))


@jax.custom_vjp
def _sigmoid(x):
    return _sig(x)


def _sigmoid_fwd(x):
    s = _sig(x)
    return s, s


def _sigmoid_bwd(s, g):
    return (g * s * (1.0 - s),)


_sigmoid.defvjp(_sigmoid_fwd, _sigmoid_bwd)


@jax.custom_vjp
def _softplus(x):
    return jnp.maximum(x, 0.0) + jnp.log(1.0 + jnp.exp(-jnp.abs(x)))


def _softplus_fwd(x):
    return _softplus(x), x


def _softplus_bwd(x, g):
    return (g * _sig(x),)


_softplus.defvjp(_softplus_fwd, _softplus_bwd)


def _silu(x):
    return x * _sig(x)


def _dsilu(x):
    s = _sig(x)
    return s * (1.0 + x * (1.0 - s))


NN3 = (((2,), (1,)), ((0,), (0,)))
NT3 = (((2,), (2,)), ((0,), (0,)))
TN3 = (((1,), (1,)), ((0,), (0,)))


def _bdot(a, b, dims):
    return lax.dot_general(a.astype(BF16), b.astype(BF16), dims, preferred_element_type=F32)


def _bf16_part(x):
    bits = lax.bitcast_convert_type(x, jnp.uint32) & jnp.uint32(0xFFFF0000)
    return lax.bitcast_convert_type(bits, F32)


def _scan_rows(x, reverse):
    c = x.shape[1]
    row = lax.broadcasted_iota(jnp.int32, x.shape, 1)
    step = 1
    while step < c:
        if reverse:
            x = x + jnp.where(row < c - step, pltpu.roll(x, c - step, axis=1), 0.0)
        else:
            x = x + jnp.where(row >= step, pltpu.roll(x, step, axis=1), 0.0)
        step *= 2
    return x


@jax.custom_vjp
def _prefix_rows(x):
    return _scan_rows(x, False)


_prefix_rows.defvjp(lambda x: (_scan_rows(x, False), None), lambda _, g: (_scan_rows(g, True),))


def _dot3(a, b, dims):
    a_hi, b_hi = _bf16_part(a), _bf16_part(b)
    a_lo, b_lo = (a - a_hi).astype(BF16), (b - b_hi).astype(BF16)
    a_hi, b_hi = a_hi.astype(BF16), b_hi.astype(BF16)
    dot = lambda x, y: lax.dot_general(x, y, dims, preferred_element_type=F32)
    return dot(a_hi, b_hi) + (dot(a_hi, b_lo) + dot(a_lo, b_hi))


@jax.custom_vjp
def _nn_hi(a, b):
    return _dot3(a, b, NN3)


_nn_hi.defvjp(lambda a, b: (_dot3(a, b, NN3), (a, b)), lambda r, g: (_dot3(g, r[1], NT3), _dot3(r[0], g, TN3)))


@jax.custom_vjp
def _nn(a, b):
    return _bdot(a, b, NN3)


_nn.defvjp(lambda a, b: (_bdot(a, b, NN3), (a, b)), lambda r, g: (_bdot(g, r[1], NT3), _bdot(r[0], g, TN3)))


@jax.custom_vjp
def _nt(a, b):
    return _bdot(a, b, NT3)


_nt.defvjp(lambda a, b: (_bdot(a, b, NT3), (a, b)), lambda r, g: (_bdot(g, r[1], NN3), _bdot(g, r[0], TN3)))


@jax.custom_vjp
def _tn(a, b):
    return _bdot(a, b, TN3)


_tn.defvjp(lambda a, b: (_bdot(a, b, TN3), (a, b)), lambda r, g: (_bdot(r[1], g, NT3), _bdot(r[0], g, NN3)))


def _conv_pre(ext, w, rows):
    acc = w[0:1] * ext[5:5 + rows]
    for j in range(1, GDN_CONV):
        acc = acc + w[j:j + 1] * ext[5 + j:5 + j + rows]
    return acc


def _conv_fwd(proj, conv_w, *, name):
    t = proj.shape[0]
    tm, tc = _pick(t, 512), 512
    nb = tm // 8

    def body(u_ref, p_ref, w_ref, o_ref):
        i = pl.program_id(1)
        prev = jnp.where(i > 0, p_ref[...], 0.0)
        ext = jnp.concatenate([prev, u_ref[...]], axis=0)
        o_ref[...] = _silu(_conv_pre(ext, w_ref[...], tm))

    return _pcall(body, name=name, grid=(CONV_CH // tc, t // tm),
                  in_specs=[pl.BlockSpec((tm, tc), lambda j, i: (i, j)),
                            pl.BlockSpec((8, tc), lambda j, i: (jnp.maximum(i * nb - 1, 0), j)),
                            pl.BlockSpec((8, tc), lambda j, i: (0, j))],
                  out_specs=pl.BlockSpec((tm, tc), lambda j, i: (i, j)),
                  out_shape=jax.ShapeDtypeStruct((t, CONV_CH), F32))(proj, proj, conv_w)


def _conv_bwd(proj, conv_w, dy, *, name):
    t = proj.shape[0]
    tm, tc = _pick(t, 512), 512
    nb = tm // 8
    last = t // tm - 1

    def body(u_ref, p_ref, n_ref, dy_ref, dyn_ref, w_ref, du_ref, dw_ref):
        i = pl.program_id(1)
        w = w_ref[...]
        prev = jnp.where(i > 0, p_ref[...], 0.0)
        ext = jnp.concatenate([prev, u_ref[...], n_ref[...]], axis=0)
        c = _conv_pre(ext, w, tm + 8)
        dy_ext = jnp.concatenate([dy_ref[...], jnp.where(i < last, dyn_ref[...], 0.0)], axis=0)
        dc = dy_ext * _dsilu(c)
        du = w[3:4] * dc[0:tm]
        for j in range(GDN_CONV - 1):
            du = du + w[j:j + 1] * dc[3 - j:3 - j + tm]
        du_ref[...] = du.astype(du_ref.dtype)

        @pl.when(i == 0)
        def _():
            dw_ref[...] = jnp.zeros_like(dw_ref)

        for j in range(GDN_CONV):
            dw_ref[j] += (dc[0:tm] * ext[5 + j:5 + j + tm]).reshape(nb, 8, tc).sum(axis=0)

    cur = lambda j, i: (i, j)
    return _pcall(body, name=name, grid=(CONV_CH // tc, t // tm),
                  in_specs=[pl.BlockSpec((tm, tc), cur),
                            pl.BlockSpec((8, tc), lambda j, i: (jnp.maximum(i * nb - 1, 0), j)),
                            pl.BlockSpec((8, tc), lambda j, i: (jnp.minimum((i + 1) * nb, t // 8 - 1), j)),
                            pl.BlockSpec((tm, tc), cur),
                            pl.BlockSpec((8, tc), lambda j, i: (jnp.minimum((i + 1) * nb, t // 8 - 1), j)),
                            pl.BlockSpec((8, tc), lambda j, i: (0, j))],
                  out_specs=[pl.BlockSpec((tm, tc), cur), pl.BlockSpec((GDN_CONV, 8, tc), lambda j, i: (0, 0, j))],
                  out_shape=[jax.ShapeDtypeStruct((t, CONV_CH), BF16), jax.ShapeDtypeStruct((GDN_CONV, 8, CONV_CH), F32)],
                  )(proj, proj, proj, dy, dy, conv_w)


def _gdn_chunk(q_raw, k_raw, v, misc, params, state):
    nh, c = q_raw.shape[0], q_raw.shape[1]
    lane = lax.broadcasted_iota(jnp.int32, misc.shape, 1)
    prow = lax.broadcasted_iota(jnp.int32, params.shape, 0)
    plane = lax.broadcasted_iota(jnp.int32, params.shape, 1)
    heads = lambda pieces: jnp.concatenate([p[None] for p in pieces], axis=0)
    col = lambda at: heads([jnp.sum(jnp.where(lane == at + h, misc, 0.0), axis=1, keepdims=True) for h in range(nh)])
    par = lambda row: heads([jnp.sum(jnp.where((prow == row) & (plane == h), params, 0.0), keepdims=True)
                             for h in range(nh)])
    b_raw, a_raw = col(LANE_B), col(LANE_A)
    a_log, dt_bias = par(0), par(1)
    beta = _sigmoid(b_raw)
    g = -jnp.exp(a_log) * _softplus(a_raw + dt_bias)

    q = q_raw * lax.rsqrt(jnp.sum(q_raw * q_raw, axis=-1, keepdims=True) + EPS) * (HEAD ** -0.5)
    k = k_raw * lax.rsqrt(jnp.sum(k_raw * k_raw, axis=-1, keepdims=True) + EPS)

    ri = lax.broadcasted_iota(jnp.int32, (c, c), 0)
    ci = lax.broadcasted_iota(jnp.int32, (c, c), 1)
    tril, strict = ri >= ci, ri > ci
    gc = _prefix_rows(g)
    gc_col = jnp.broadcast_to(gc, (nh, c, c))
    gc_row = jnp.swapaxes(gc_col, 1, 2)
    decay = jnp.exp(jnp.where(tril, gc_col - gc_row, -1e30))

    kb = k * beta
    vb = v * beta
    a_mat = jnp.where(strict, _nt(kb, k) * decay, 0.0)
    x = -a_mat
    inv = (ri == ci).astype(F32) + x
    for _ in range(5):
        x = _nn_hi(x, x)
        inv = inv + _nn_hi(inv, x)
    u = _nn_hi(inv, vb)
    w = _nn_hi(inv, kb * jnp.exp(gc))
    intra = _nt(q, k) * decay

    v_new = u - _nn(w, state)
    o = _nn(q * jnp.exp(gc), state) + _nn(intra, v_new)
    g_last = jnp.sum(g, axis=1, keepdims=True)
    k_dec = k * jnp.exp(g_last - gc)
    new_state = state * jnp.exp(g_last) + _tn(k_dec, v_new)
    return o, new_state


def _gdn_specs(nc, rev):
    cidx = (lambda n: nc - 1 - n) if rev else (lambda n: n)
    hb = lambda part: pl.BlockSpec((GDN_CHUNK, GDN_QK), lambda n: (cidx(n), part))
    misc = pl.BlockSpec((GDN_CHUNK, HEAD), lambda n: (cidx(n), COL_MISC // HEAD))
    params = pl.BlockSpec((8, HEAD), lambda n: (0, 0))
    hist = pl.BlockSpec((1, GDN_HEADS, HEAD, HEAD), lambda n: (cidx(n), 0, 0, 0))
    return hb, misc, params, hist


def _split_heads(v):
    return jnp.stack([v[:, h * HEAD:(h + 1) * HEAD] for h in range(v.shape[1] // HEAD)])


def _merge_heads(v):
    return jnp.concatenate([v[h] for h in range(v.shape[0])], axis=1)


def _gdn_fwd(qkv, proj, params, *, name):
    t = qkv.shape[0]
    nc = t // GDN_CHUNK
    hb, misc, pspec, hist = _gdn_specs(nc, False)

    def body(q_ref, k_ref, v_ref, m_ref, p_ref, o_ref, hist_ref, s_ref):
        @pl.when(pl.program_id(0) == 0)
        def _():
            s_ref[...] = jnp.zeros_like(s_ref)

        state = s_ref[...]
        hist_ref[0] = state
        o, new_state = _gdn_chunk(_split_heads(q_ref[...]), _split_heads(k_ref[...]), _split_heads(v_ref[...]),
                                  m_ref[...], p_ref[...], state)
        o_ref[...] = _merge_heads(o)
        s_ref[...] = new_state

    return _pcall(body, name=name, grid=(nc,),
                  in_specs=[hb(0), hb(1), hb(2), misc, pspec],
                  out_specs=[hb(0), hist],
                  out_shape=[jax.ShapeDtypeStruct((t, GDN_QK), F32),
                             jax.ShapeDtypeStruct((nc, GDN_HEADS, HEAD, HEAD), F32)],
                  scratch=[pltpu.VMEM((GDN_HEADS, HEAD, HEAD), F32)])(qkv, qkv, qkv, proj, params)


def _gdn_bwd(qkv, proj, params, hist_arr, do, dmisc_in, *, name, carry=None):
    t = qkv.shape[0]
    nc = t // GDN_CHUNK
    hb, misc, pspec, hist = _gdn_specs(nc, True)
    mrow = pl.BlockSpec((GDN_CHUNK, HEAD), lambda n: (nc - 1 - n, 0))

    def body(q_ref, k_ref, v_ref, m_ref, p_ref, hist_ref, do_ref, dmi_ref, dqkv_ref, dm_ref, dp_ref, ds_ref):
        @pl.when(pl.program_id(0) == 0)
        def _():
            ds_ref[...] = jnp.zeros_like(ds_ref)
            dp_ref[...] = jnp.zeros_like(dp_ref)

        _, vjp = jax.vjp(_gdn_chunk, _split_heads(q_ref[...]), _split_heads(k_ref[...]), _split_heads(v_ref[...]),
                         m_ref[...], p_ref[...], hist_ref[0])
        dq, dk, dv, dm, dp, ds = vjp((_split_heads(do_ref[...]), ds_ref[...]))
        dqkv_ref[:, 0:GDN_QK] = _merge_heads(dq)
        dqkv_ref[:, GDN_QK:2 * GDN_QK] = _merge_heads(dk)
        dqkv_ref[:, 2 * GDN_QK:] = _merge_heads(dv)
        ds_ref[...] = ds
        dm_ref[...] = dmi_ref[...] + dm
        dp_ref[...] += dp

    return _pcall(body, name=name, grid=(nc,),
                  in_specs=[hb(0), hb(1), hb(2), misc, pspec, hist, hb(0), mrow],
                  out_specs=[pl.BlockSpec((GDN_CHUNK, CONV_CH), lambda n: (nc - 1 - n, 0)), mrow, pspec],
                  out_shape=[jax.ShapeDtypeStruct((t, CONV_CH), F32), jax.ShapeDtypeStruct((t, HEAD), F32),
                             jax.ShapeDtypeStruct((8, HEAD), F32)],
                  scratch=[pltpu.VMEM((GDN_HEADS, HEAD, HEAD), F32)], carry=carry,
                  )(qkv, qkv, qkv, proj, params, hist_arr, do, dmisc_in)


def _gate_fwd(o_raw, proj, w, *, name):
    t = o_raw.shape[0]
    tm = _rows(t, HEAD)
    zb = COL_Z // HEAD

    def body(o_ref, z_ref, w_ref, out_ref):
        ov = o_ref[...]
        r = lax.rsqrt(jnp.mean(ov * ov, axis=-1, keepdims=True) + EPS)
        out_ref[...] = (ov * r * w_ref[...] * _silu(z_ref[...])).astype(out_ref.dtype)

    blk = pl.BlockSpec((tm, HEAD), lambda i, h: (i, h))
    return _pcall(body, name=name, grid=(t // tm, GDN_HEADS),
                  in_specs=[blk, pl.BlockSpec((tm, HEAD), lambda i, h: (i, zb + h)), pl.BlockSpec((1, HEAD), lambda i, h: (0, 0))],
                  out_specs=blk, out_shape=jax.ShapeDtypeStruct((t, GDN_QK), BF16))(o_raw, proj, w)


def _gate_bwd(o_raw, proj, w, dmixed, *, name):
    t = o_raw.shape[0]
    tm = _rows(t, HEAD)
    zb = COL_Z // HEAD

    def body(o_ref, z_ref, w_ref, dy_ref, do_ref, dz_ref, dw_ref):
        ov, zv, dyv = o_ref[...], z_ref[...], dy_ref[...]
        r = lax.rsqrt(jnp.mean(ov * ov, axis=-1, keepdims=True) + EPS)
        xh = ov * r
        dn = dyv * _silu(zv)
        dz_ref[...] = (dyv * xh * w_ref[...] * _dsilu(zv)).astype(dz_ref.dtype)
        dnw = dn * w_ref[...]
        do_ref[...] = r * (dnw - xh * jnp.mean(dnw * xh, axis=-1, keepdims=True))

        @pl.when((pl.program_id(0) == 0) & (pl.program_id(1) == 0))
        def _():
            dw_ref[...] = jnp.zeros_like(dw_ref)

        dw_ref[...] += (dn * xh).reshape(tm // 8, 8, HEAD).sum(axis=0)

    blk = pl.BlockSpec((tm, HEAD), lambda i, h: (i, h))
    return _pcall(body, name=name, grid=(t // tm, GDN_HEADS),
                  in_specs=[blk, pl.BlockSpec((tm, HEAD), lambda i, h: (i, zb + h)), pl.BlockSpec((1, HEAD), lambda i, h: (0, 0)), blk],
                  out_specs=[blk, blk, pl.BlockSpec((8, HEAD), lambda i, h: (0, 0))],
                  out_shape=[jax.ShapeDtypeStruct((t, GDN_QK), F32), jax.ShapeDtypeStruct((t, GDN_QK), BF16),
                             jax.ShapeDtypeStruct((8, HEAD), F32)])(o_raw, proj, w, dmixed)


def _rope_tables():
    half = QK_ROPE // 2
    inv = ROPE_THETA ** (-jnp.arange(half, dtype=F32) / half)
    zeros = jnp.zeros((HEAD - QK_ROPE,), F32)
    inv_row = jnp.concatenate([inv, inv, zeros])
    sign_row = jnp.concatenate([-jnp.ones((half,), F32), jnp.ones((half,), F32), zeros])
    mask_row = jnp.concatenate([jnp.ones((QK_ROPE,), F32), zeros])
    return jnp.concatenate([inv_row[None], sign_row[None], mask_row[None], jnp.zeros((5, HEAD), F32)], axis=0)


def _rope_cs(pos, tab, *, name):
    t = pos.shape[0]
    tm = _pick(t, 1024)

    def body(pos_ref, tab_ref, o_ref):
        tab = tab_ref[...]
        ang = pos_ref[...] * tab[0:1]
        o_ref[...] = jnp.concatenate([jnp.cos(ang) * tab[2:3], jnp.sin(ang) * tab[1:2]], axis=1)

    return _pcall(body, name=name, grid=(t // tm,),
                  in_specs=[pl.BlockSpec((tm, 1), lambda i: (i, 0)), pl.BlockSpec((8, HEAD), lambda i: (0, 0))],
                  out_specs=pl.BlockSpec((tm, 2 * HEAD), lambda i: (i, 0)),
                  out_shape=jax.ShapeDtypeStruct((t, 2 * HEAD), F32))(pos, tab)


def _rotate(x, cs, sign):
    lane = lax.broadcasted_iota(jnp.int32, x.shape, 1)
    half = QK_ROPE // 2
    partner = jnp.where(lane < half, pltpu.roll(x, HEAD - half, axis=1), pltpu.roll(x, half, axis=1))
    return x * cs[:, :HEAD] + partner * (cs[:, HEAD:] * sign)


def _q_rot(q, cs, *, name, sign, out_dtype=BF16):
    t = q.shape[0]
    tm = _pick(t, 1024)
    scale = (HEAD + QK_ROPE) ** -0.5

    def body(q_ref, cs_ref, o_ref):
        qv = q_ref[...].astype(F32)
        rot = _rotate(qv[:, HEAD:], cs_ref[...], sign)
        o_ref[...] = (jnp.concatenate([qv[:, :HEAD], rot], axis=1) * scale).astype(o_ref.dtype)

    blk = pl.BlockSpec((tm, QHEAD), lambda i, h: (i, h))
    return _pcall(body, name=name, grid=(t // tm, MLA_HEADS),
                  in_specs=[blk, pl.BlockSpec((tm, 2 * HEAD), lambda i, h: (i, 0))],
                  out_specs=blk, out_shape=jax.ShapeDtypeStruct((t, MLA_HEADS * QHEAD), out_dtype))(q, cs)


def _kv_prep(kv, proj, cs, *, name):
    t = kv.shape[0]
    tm = _pick(t, 1024)

    def body(kv_ref, m_ref, cs_ref, k_ref, v_ref):
        kvv = kv_ref[...]
        misc = m_ref[...]
        lane = lax.broadcasted_iota(jnp.int32, misc.shape, 1)
        rot = _rotate(jnp.where(lane < QK_ROPE, misc, 0.0), cs_ref[...], 1.0)
        k_ref[...] = jnp.concatenate([kvv[:, :HEAD], rot], axis=1).astype(k_ref.dtype)
        v_ref[...] = kvv[:, HEAD:].astype(v_ref.dtype)

    blk = pl.BlockSpec((tm, QHEAD), lambda i, h: (i, h))
    return _pcall(body, name=name, grid=(t // tm, MLA_HEADS),
                  in_specs=[blk, pl.BlockSpec((tm, HEAD), lambda i, h: (i, COL_MISC // HEAD)),
                            pl.BlockSpec((tm, 2 * HEAD), lambda i, h: (i, 0))],
                  out_specs=[blk, pl.BlockSpec((tm, HEAD), lambda i, h: (i, h))],
                  out_shape=[jax.ShapeDtypeStruct((t, MLA_HEADS * QHEAD), BF16), jax.ShapeDtypeStruct((t, MLA_HEADS * HEAD), BF16)],
                  )(kv, proj, cs)


def _krope_bwd(dkr, cs, *, name):
    t = dkr.shape[0]
    tm = _pick(t, 512)

    def body(d_ref, cs_ref, o_ref):
        d = d_ref[...]
        acc = d[:, :HEAD]
        for h in range(1, MLA_HEADS):
            acc = acc + d[:, h * HEAD:(h + 1) * HEAD]
        o_ref[...] = _rotate(acc, cs_ref[...], -1.0)

    return _pcall(body, name=name, grid=(t // tm,),
                  in_specs=[pl.BlockSpec((tm, MLA_HEADS * HEAD), lambda i: (i, 0)), pl.BlockSpec((tm, 2 * HEAD), lambda i: (i, 0))],
                  out_specs=pl.BlockSpec((tm, HEAD), lambda i: (i, 0)),
                  out_shape=jax.ShapeDtypeStruct((t, HEAD), F32))(dkr, cs)


NEG = -1e30


def _tri(step, counts):
    starts = [sum(counts[:o]) for o in range(len(counts))]
    outer = sum([(step >= s).astype(jnp.int32) for s in starts[1:]], jnp.int32(0))
    start = sum([(step >= starts[o]).astype(jnp.int32) * (starts[o] - starts[o - 1]) for o in range(1, len(counts))], jnp.int32(0))
    return outer, step - start


def _attn_fwd(q, k, v, *, name, tq=1024, tk=1024, carry=None):
    t = q.shape[0]
    tq, tk = _pick(t, tq), _pick(t, tk)
    nq = t // tq
    last_kv = lambda i: (i * tq + tq - 1) // tk
    counts = [last_kv(i) + 1 for i in range(nq)]

    def body(q_ref, k_ref, v_ref, o_ref, lse_ref, m_ref, l_ref, acc_ref):
        i, j = _tri(pl.program_id(1), counts)

        @pl.when(j == 0)
        def _():
            m_ref[...] = jnp.full_like(m_ref, NEG)
            l_ref[...] = jnp.zeros_like(l_ref)
            acc_ref[...] = jnp.zeros_like(acc_ref)

        def step(masked):
            s = lax.dot_general(q_ref[...], k_ref[...], (NT, ((), ())), preferred_element_type=F32)
            if masked:
                qpos = i * tq + lax.broadcasted_iota(jnp.int32, s.shape, 0)
                kpos = j * tk + lax.broadcasted_iota(jnp.int32, s.shape, 1)
                s = jnp.where(kpos <= qpos, s, NEG)
            m_prev = m_ref[...]
            m_new = jnp.maximum(m_prev, jnp.max(s, axis=1, keepdims=True))
            alpha = jnp.exp(m_prev - m_new)
            p = jnp.exp(s - m_new)
            l_ref[...] = alpha * l_ref[...] + jnp.sum(p, axis=1, keepdims=True)
            acc_ref[...] = alpha * acc_ref[...] + lax.dot_general(p.astype(BF16), v_ref[...], (NN, ((), ())),
                                                                  preferred_element_type=F32)
            m_ref[...] = m_new

        crosses = j * tk + tk - 1 > i * tq

        @pl.when(crosses)
        def _():
            step(True)

        @pl.when(jnp.logical_not(crosses))
        def _():
            step(False)

        @pl.when(j == last_kv(i))
        def _():
            o_ref[...] = acc_ref[...] / l_ref[...]
            lse_ref[...] = jnp.broadcast_to(m_ref[...] + jnp.log(l_ref[...]), lse_ref.shape)

    qblk = pl.BlockSpec((tq, QHEAD), lambda h, s: (_tri(s, counts)[0], h))
    oblk = pl.BlockSpec((tq, HEAD), lambda h, s: (_tri(s, counts)[0], h))
    return _pcall(body, name=name, grid=(MLA_HEADS, sum(counts)),
                  in_specs=[qblk, pl.BlockSpec((tk, QHEAD), lambda h, s: (_tri(s, counts)[1], h)),
                            pl.BlockSpec((tk, HEAD), lambda h, s: (_tri(s, counts)[1], h))],
                  out_specs=[oblk, oblk],
                  out_shape=[jax.ShapeDtypeStruct((t, MLA_HEADS * HEAD), F32), jax.ShapeDtypeStruct((t, MLA_HEADS * HEAD), F32)],
                  scratch=[pltpu.VMEM((tq, 1), F32), pltpu.VMEM((tq, 1), F32), pltpu.VMEM((tq, HEAD), F32)],
                  carry=carry)(q, k, v)


def _attn_bwd(q, k, v, do, lse, delta, *, name, tq=512, tk=512, carry=None):
    t = q.shape[0]
    tq, tk = _pick(t, tq), _pick(t, tk)
    nq, nk = t // tq, t // tk
    first_q = lambda j: (j * tk) // tq
    counts = [nq - first_q(j) for j in range(nk)]

    def where(step):
        j, off = _tri(step, counts)
        return j, first_q(j) + off

    lanes = lambda col: jnp.tile(col, (1, tk // HEAD))

    def body(q_ref, k_ref, v_ref, do_ref, lse_ref, dl_ref, dq_ref, dkv_ref, dkr_ref, dk_acc, dv_acc):
        j, i = where(pl.program_id(1))

        @pl.when(i == first_q(j))
        def _():
            dk_acc[...] = jnp.zeros_like(dk_acc)
            dv_acc[...] = jnp.zeros_like(dv_acc)

        def step(masked):
            qv, kv_, dov = q_ref[...], k_ref[...], do_ref[...].astype(BF16)
            s = lax.dot_general(qv, kv_, (NT, ((), ())), preferred_element_type=F32)
            p = jnp.exp((s - lanes(lse_ref[...])).astype(BF16))
            if masked:
                qpos = i * tq + lax.broadcasted_iota(jnp.int32, s.shape, 0)
                kpos = j * tk + lax.broadcasted_iota(jnp.int32, s.shape, 1)
                p = jnp.where(kpos <= qpos, p, jnp.zeros_like(p))
            dv_acc[...] += lax.dot_general(p, dov, (TN, ((), ())), preferred_element_type=F32)
            dp = lax.dot_general(dov, v_ref[...], (NT, ((), ())), preferred_element_type=F32)
            ds = p * (dp - lanes(dl_ref[...])).astype(BF16)
            dk_acc[...] += lax.dot_general(ds, qv, (TN, ((), ())), preferred_element_type=F32)
            contrib = lax.dot_general(ds, kv_, (NN, ((), ())), preferred_element_type=F32)
            rows = pl.ds(pl.multiple_of(i * tq, tq), tq)

            @pl.when(j == 0)
            def _():
                dq_ref[rows, :] = contrib

            @pl.when(j > 0)
            def _():
                dq_ref[rows, :] += contrib

        crosses = j * tk + tk - 1 > i * tq

        @pl.when(crosses)
        def _():
            step(True)

        @pl.when(jnp.logical_not(crosses))
        def _():
            step(False)

        @pl.when(i == nq - 1)
        def _():
            dk = dk_acc[...]
            dkv_ref[...] = jnp.concatenate([dk[:, :HEAD], dv_acc[...]], axis=1).astype(dkv_ref.dtype)
            dkr_ref[...] = dk[:, HEAD:]

    qi = lambda h, s: (where(s)[1], h)
    kj = lambda h, s: (where(s)[0], h)
    return _pcall(body, name=name, grid=(MLA_HEADS, sum(counts)),
                  in_specs=[pl.BlockSpec((tq, QHEAD), qi), pl.BlockSpec((tk, QHEAD), kj), pl.BlockSpec((tk, HEAD), kj),
                            pl.BlockSpec((tq, HEAD), qi), pl.BlockSpec((tq, HEAD), qi), pl.BlockSpec((tq, HEAD), qi)],
                  out_specs=[pl.BlockSpec((t, QHEAD), lambda h, s: (0, h)), pl.BlockSpec((tk, QHEAD), kj),
                             pl.BlockSpec((tk, HEAD), kj)],
                  out_shape=[jax.ShapeDtypeStruct((t, MLA_HEADS * QHEAD), F32), jax.ShapeDtypeStruct((t, MLA_HEADS * QHEAD), BF16),
                             jax.ShapeDtypeStruct((t, MLA_HEADS * HEAD), F32)],
                  scratch=[pltpu.VMEM((tk, QHEAD), F32), pltpu.VMEM((tk, HEAD), F32)], carry=carry)(q, k, v, do, lse, delta)


def _ffn_up(h, wgate, wup, *, name, bm=512, bn=FF_WIDE):
    t = h.shape[0]
    bm = _pick(t, bm)

    def body(h_ref, wg_ref, wu_ref, g_ref, u_ref, a_ref):
        hv = h_ref[...]
        g = lax.dot_general(hv, wg_ref[...], (NN, ((), ())), preferred_element_type=F32)
        u = lax.dot_general(hv, wu_ref[...], (NN, ((), ())), preferred_element_type=F32)
        g_ref[...] = g.astype(g_ref.dtype)
        u_ref[...] = u.astype(u_ref.dtype)
        a_ref[...] = (_silu(g) * u).astype(a_ref.dtype)

    w_spec = pl.BlockSpec((D_MODEL, bn), lambda j, i: (0, j))
    o_spec = pl.BlockSpec((bm, bn), lambda j, i: (i, j))
    sds = jax.ShapeDtypeStruct((t, D_FF), BF16)
    return _pcall(body, name=name, grid=(D_FF // bn, t // bm),
                  in_specs=[pl.BlockSpec((bm, D_MODEL), lambda j, i: (i, 0)), w_spec, w_spec],
                  out_specs=[o_spec] * 3, out_shape=[sds] * 3)(h, wgate, wup)


def _ffn_down_dx(dy, wdown, gate, up, *, name, bm=512, bn=FF_WIDE):
    t = dy.shape[0]
    bm = _pick(t, bm)

    def body(dy_ref, w_ref, g_ref, u_ref, dg_ref, du_ref):
        d = lax.dot_general(dy_ref[...].astype(BF16), w_ref[...], (NT, ((), ())), preferred_element_type=F32)
        g = g_ref[...].astype(F32)
        dg_ref[...] = (d * u_ref[...].astype(F32) * _dsilu(g)).astype(dg_ref.dtype)
        du_ref[...] = (d * _silu(g)).astype(du_ref.dtype)

    o_spec = pl.BlockSpec((bm, bn), lambda j, i: (i, j))
    sds = jax.ShapeDtypeStruct((t, D_FF), BF16)
    return _pcall(body, name=name, grid=(D_FF // bn, t // bm),
                  in_specs=[pl.BlockSpec((bm, D_MODEL), lambda j, i: (i, 0)), pl.BlockSpec((bn, D_MODEL), lambda j, i: (j, 0)),
                            o_spec, o_spec],
                  out_specs=[o_spec, o_spec], out_shape=[sds, sds])(dy, wdown, gate, up)


def _loss_bwd(x2, w, target, *, name):
    t = x2.shape[0]
    tm = _rows(t, D_MODEL)

    def body(x_ref, w_ref, t_ref, dx_ref, dw_ref, l_ref):
        xv, wv = x_ref[...], w_ref[...]
        r = lax.rsqrt(jnp.mean(xv * xv, axis=-1, keepdims=True) + EPS)
        xh = xv * r
        err = xh * wv - t_ref[...]
        dy = err * (1.0 / D_MODEL)
        dyw = dy * wv
        dx_ref[...] = r * (dyw - xh * jnp.mean(dyw * xh, axis=-1, keepdims=True))

        @pl.when(pl.program_id(0) == 0)
        def _():
            dw_ref[...] = jnp.zeros_like(dw_ref)
            l_ref[...] = jnp.zeros_like(l_ref)

        dw_ref[...] += (dy * xh).reshape(tm // 8, 8, D_MODEL).sum(axis=0)
        sq = (err * err).reshape(tm // 8, 8, D_MODEL).sum(axis=0)
        part = sq[:, :HEAD]
        for c in range(1, D_MODEL // HEAD):
            part = part + sq[:, c * HEAD:(c + 1) * HEAD]
        l_ref[...] += part * (0.5 / D_MODEL)

    row = pl.BlockSpec((tm, D_MODEL), lambda i: (i, 0))
    return _pcall(body, name=name, grid=(t // tm,),
                  in_specs=[row, pl.BlockSpec((1, D_MODEL), lambda i: (0, 0)), row],
                  out_specs=[row, pl.BlockSpec((8, D_MODEL), lambda i: (0, 0)), pl.BlockSpec((8, HEAD), lambda i: (0, 0))],
                  out_shape=[jax.ShapeDtypeStruct((t, D_MODEL), F32), jax.ShapeDtypeStruct((8, D_MODEL), F32),
                             jax.ShapeDtypeStruct((8, HEAD), F32)])(x2, w, target)


def _unshard_cols(g):
    return jnp.transpose(g, (1, 0, 2)).reshape(g.shape[1], N_DEV * g.shape[2])


def _shard_cols(w):
    return jnp.transpose(w.reshape(w.shape[0], N_DEV, w.shape[1] // N_DEV), (1, 0, 2))


def _win_to_padded(w):
    pad = jnp.zeros((w.shape[0], PROJ_W - IN_WIDTH), w.dtype)
    return jnp.concatenate([w[:, :4096], w[:, 4112:5136], w[:, 5136:5200], w[:, 4096:4112], pad], axis=1)


def _win_from_padded(d):
    return jnp.concatenate([d[:, :4096], d[:, 5184:5200], d[:, 4096:5120], d[:, 5120:5184]], axis=1)


def _wuq_to_padded(w):
    w3 = w.reshape(w.shape[0], MLA_HEADS, HEAD + QK_ROPE)
    return jnp.pad(w3, ((0, 0), (0, 0), (0, QHEAD - HEAD - QK_ROPE))).reshape(w.shape[0], MLA_HEADS * QHEAD)


def _wuq_from_padded(d):
    return d.reshape(d.shape[0], MLA_HEADS, QHEAD)[:, :, :HEAD + QK_ROPE].reshape(d.shape[0], MLA_HEADS * (HEAD + QK_ROPE))


def _late_weights(g_out, g_gate, g_up, g_down):
    return g_out.reshape(D_MODEL, D_MODEL), _unshard_cols(g_gate), _unshard_cols(g_up), g_down.reshape(D_FF, D_MODEL)


def _local_step(x, pos, target, win_p, wuq_p, wukv, late, conv_w, small, exchange):
    cs = _rope_cs(pos, _rope_tables(), name="rope_cs")
    if not exchange:
        wout, wgate, wup, wdown = late
    h1 = _rms_fwd(x, small["attn_norm_w"], name="rms1_fwd", width=D_MODEL)
    proj = _mm(h1, win_p, name="mm_in", bn=768)
    qkv = _conv_fwd(proj, conv_w, name="conv_fwd")
    o_gdn_raw, hist = _gdn_fwd(qkv, proj, small["gdn_params"], name="gdn_fwd")
    o_gdn = _gate_fwd(o_gdn_raw, proj, small["gdn_norm_w"], name="gate_fwd")
    cqn = _rms_fwd(proj, small["q_norm_w"], name="rmsq_fwd", width=Q_LORA, col0=COL_CQ)
    ckvn = _rms_fwd(proj, small["kv_norm_w"], name="rmskv_fwd", width=KV_LORA, col0=COL_CKV)
    q_pre = _mm(cqn, wuq_p, name="mm_uq")
    kv = _mm(ckvn, wukv, name="mm_ukv")
    q_full = _q_rot(q_pre, cs, name="q_rot", sign=1.0)
    k_full, v_b = _kv_prep(kv, proj, cs, name="kv_prep")
    if exchange:
        (o_mla_raw, lse), gathered = _attn_fwd(q_full, k_full, v_b, name="attn_fwd", carry=_Exchange(late, [True] * 4))
        wout, wgate, wup, wdown = _late_weights(*gathered)
    else:
        o_mla_raw, lse = _attn_fwd(q_full, k_full, v_b, name="attn_fwd")
    o_mla = _rms_fwd(o_mla_raw, small["mla_out_norm_w"], name="rmso_fwd", width=HEAD, heads=MLA_HEADS)
    mixed = jnp.concatenate([o_gdn, o_mla], axis=1)
    x1 = _mm(mixed, wout, name="mm_out", res=x)
    h2 = _rms_fwd(x1, small["ffn_norm_w"], name="rms2_fwd", width=D_MODEL)
    gate, up, act = _ffn_up(h2, wgate, wup, name="ffn_up")
    x2 = _mm(act, wdown, name="mm_down", res=x1, bk=FF_WIDE)
    dx2, dw_final, loss_part = _loss_bwd(x2, small["final_norm_w"], target, name="loss_bwd")
    dgate, dup = _ffn_down_dx(dx2, wdown, gate, up, name="ffn_down_dx")
    d_wdown = _mm(act, dx2, name="mm_down_dw", ta=True, out_dtype=BF16, bm=FF_WIDE)
    dh2 = _mm(dgate, wgate, name="mm_gate_dx", tb=True, bk=FF_WIDE)
    dh2 = _mm(dup, wup, name="mm_up_dx", tb=True, res=dh2, bk=FF_WIDE)
    d_wgate = _mm(h2, dgate, name="mm_gate_dw", ta=True, out_dtype=BF16, bn=FF_WIDE)
    d_wup = _mm(h2, dup, name="mm_up_dw", ta=True, out_dtype=BF16, bn=FF_WIDE)
    dx1, dw_ffn = _rms_bwd(x1, small["ffn_norm_w"], dh2, name="rms2_bwd", width=D_MODEL, res=dx2)
    dmixed = _mm(dx1, wout, name="mm_out_dx", tb=True)
    d_wout = _mm(mixed, dx1, name="mm_out_dw", ta=True, out_dtype=BF16)
    do_mla, dw_mla_out, delta = _rms_bwd(o_mla_raw, small["mla_out_norm_w"], dmixed, name="rmso_bwd", width=HEAD,
                                         heads=MLA_HEADS, dcol0=GDN_QK, with_delta=True, out_dtype=BF16)
    if exchange:
        send = [d_wdown.reshape(N_DEV, D_FF // N_DEV, D_MODEL), _shard_cols(d_wgate), _shard_cols(d_wup)]
        (dq_full, dkv, dkr_h), (r_down, r_gate, r_up) = _attn_bwd(q_full, k_full, v_b, do_mla, lse, delta, name="attn_bwd",
                                                                  carry=_Exchange(send, [False] * 3))
    else:
        dq_full, dkv, dkr_h = _attn_bwd(q_full, k_full, v_b, do_mla, lse, delta, name="attn_bwd")
    dq_pre = _q_rot(dq_full, cs, name="q_rot_bwd", sign=-1.0)
    dmisc_kr = _krope_bwd(dkr_h, cs, name="krope_bwd")
    dcqn = _mm(dq_pre, wuq_p, name="mm_uq_dx", tb=True)
    d_wuq = _mm(cqn, dq_pre, name="mm_uq_dw", ta=True, out_dtype=BF16)
    dckvn = _mm(dkv, wukv, name="mm_ukv_dx", tb=True)
    d_wukv = _mm(ckvn, dkv, name="mm_ukv_dw", ta=True, out_dtype=BF16)
    dcq, dw_qn = _rms_bwd(proj, small["q_norm_w"], dcqn, name="rmsq_bwd", width=Q_LORA, col0=COL_CQ, out_dtype=BF16)
    dckv, dw_kvn = _rms_bwd(proj, small["kv_norm_w"], dckvn, name="rmskv_bwd", width=KV_LORA, col0=COL_CKV, out_dtype=BF16)
    do_gdn, dz, dw_gdn = _gate_bwd(o_gdn_raw, proj, small["gdn_norm_w"], dmixed, name="gate_bwd")
    if exchange:
        send = [d_wout.reshape(N_DEV, D_MODEL // N_DEV, D_MODEL), _shard_cols(_wuq_from_padded(d_wuq)), _shard_cols(d_wukv)]
        (dqkv, dmisc, d_params), (r_out, r_uq, r_ukv) = _gdn_bwd(
            qkv, proj, small["gdn_params"], hist, do_gdn, dmisc_kr, name="gdn_bwd", carry=_Exchange(send, [False] * 3))
    else:
        dqkv, dmisc, d_params = _gdn_bwd(qkv, proj, small["gdn_params"], hist, do_gdn, dmisc_kr, name="gdn_bwd")
    dqkv_pre, dconv = _conv_bwd(proj, conv_w, dqkv, name="conv_bwd")
    dproj = jnp.concatenate([dqkv_pre, dz, dcq, dckv, dmisc.astype(BF16), jnp.zeros((x.shape[0], PROJ_W - COL_MISC - HEAD), BF16)], axis=1)
    d_win = _mm(h1, dproj, name="mm_in_dw", ta=True, out_dtype=BF16, bn=768)
    if exchange:
        dh1, (r_in,) = _mm(dproj, win_p, name="mm_in_dx", tb=True, bk=768,
                           carry=_Exchange([_shard_cols(_win_from_padded(d_win))], [False]))
        d_win = r_in
    else:
        dh1 = _mm(dproj, win_p, name="mm_in_dx", tb=True, bk=768)
    dx, dw_attn = _rms_bwd(x, small["attn_norm_w"], dh1, name="rms1_bwd", width=D_MODEL, res=dx1)

    if exchange:
        big = {"w_in": d_win, "w_uq": r_uq, "w_ukv": r_ukv, "w_out": r_out, "w_gate": r_gate, "w_up": r_up, "w_down": r_down}
    else:
        big = {"w_in": d_win, "w_uq": d_wuq, "w_ukv": d_wukv, "w_out": d_wout, "w_gate": d_wgate, "w_up": d_wup,
               "w_down": d_wdown}
    sm = {"attn_norm_w": dw_attn, "ffn_norm_w": dw_ffn, "final_norm_w": dw_final, "q_norm_w": dw_qn, "kv_norm_w": dw_kvn,
          "gdn_norm_w": dw_gdn, "mla_out_norm_w": dw_mla_out, "gdn_params": d_params, "conv_w": dconv, "loss": loss_part}
    return dx, big, sm


def _exchange(arrays, gather, *, name):
    ex = _Exchange(arrays, gather)

    def body(*refs):
        ins, outs, sems = refs[:ex.n], refs[ex.n:2 * ex.n], refs[2 * ex.n:]
        ex.start(ins, outs, sems)
        ex.wait(ins, outs, sems)

    any_spec = pl.BlockSpec(memory_space=pl.ANY)
    return pl.pallas_call(body, name=name, in_specs=[any_spec] * ex.n, out_specs=[any_spec] * ex.n,
                          out_shape=ex.out_shape(), scratch_shapes=ex.sems())(*arrays)


def _adamw_math(g, w, m, v):
    m = ADAM_B1 * m + (1.0 - ADAM_B1) * g
    v = ADAM_B2 * v + (1.0 - ADAM_B2) * (g * g)
    m_hat = m / (1.0 - ADAM_B1 ** ADAM_STEP)
    v_hat = v / (1.0 - ADAM_B2 ** ADAM_STEP)
    delta = -ADAM_LR * (m_hat / (jnp.sqrt(v_hat) + ADAM_EPS) + ADAM_WD * w)
    return delta, m, v


def _adamw(parts, w, m, v, *, name):
    npart, r, c = parts.shape
    tr = r if r * c * 4 <= (1 << 20) else _rows(r, c, 1 << 20)

    def body(p_ref, w_ref, m_ref, v_ref, g_ref, d_ref, nm_ref, nv_ref):
        g = p_ref[0].astype(F32)
        for s in range(1, npart):
            g = g + p_ref[s].astype(F32)
        g_ref[...] = g
        d_ref[...], nm_ref[...], nv_ref[...] = _adamw_math(g, w_ref[...], m_ref[...], v_ref[...])

    blk = pl.BlockSpec((tr, c), lambda i: (i, 0))
    sds = jax.ShapeDtypeStruct((r, c), F32)
    return _pcall(body, name=name, grid=(r // tr,),
                  in_specs=[pl.BlockSpec((npart, tr, c), lambda i: (0, i, 0)), blk, blk, blk],
                  out_specs=[blk] * 4, out_shape=[sds] * 4)(parts, w, m, v)


def _sum_parts(parts, *, name):
    npart, r, c = parts.shape

    def body(p_ref, o_ref):
        g = p_ref[0]
        for s in range(1, npart):
            g = g + p_ref[s]
        o_ref[...] = g

    return _pcall(body, name=name, grid=(1,), in_specs=[pl.BlockSpec((npart, r, c), lambda i: (0, 0, 0))],
                  out_specs=pl.BlockSpec((r, c), lambda i: (0, 0)), out_shape=jax.ShapeDtypeStruct((r, c), F32))(parts)


_SMALL = (("attn_norm_w", D_MODEL), ("ffn_norm_w", D_MODEL), ("final_norm_w", D_MODEL), ("q_norm_w", Q_LORA),
          ("kv_norm_w", KV_LORA), ("gdn_norm_w", HEAD), ("mla_out_norm_w", HEAD), ("a_log", HEAD), ("dt_bias", HEAD))
_SMALL_ROWS = sum(n for _, n in _SMALL) // HEAD
_CONV_ROWS = GDN_CONV * CONV_CH // HEAD
_PACK_ROWS = 160


def _pad_lanes(v, n):
    v = v.reshape(-1)
    return jnp.concatenate([v, jnp.zeros((n - v.shape[0],), v.dtype)])


def kernel(x, positions, attn_norm_w, w_in, conv_w, a_log, dt_bias, gdn_norm_w, q_norm_w, w_uq, kv_norm_w, w_ukv, mla_out_norm_w, w_out, ffn_norm_w, w_gate, w_up, w_down, final_norm_w, loss_target, m_attn_norm_w, m_w_in, m_conv_w, m_a_log, m_dt_bias, m_gdn_norm_w, m_q_norm_w, m_w_uq, m_kv_norm_w, m_w_ukv, m_mla_out_norm_w, m_w_out, m_ffn_norm_w, m_w_gate, m_w_up, m_w_down, m_final_norm_w, v_attn_norm_w, v_w_in, v_conv_w, v_a_log, v_dt_bias, v_gdn_norm_w, v_q_norm_w, v_w_uq, v_kv_norm_w, v_w_ukv, v_mla_out_norm_w, v_w_out, v_ffn_norm_w, v_w_gate, v_w_up, v_w_down, v_final_norm_w):
    t = x.shape[1]
    me = 4 * lax.axis_index("x") + 2 * lax.axis_index("y") + lax.axis_index("c")
    weights = dict(attn_norm_w=attn_norm_w, w_in=w_in, conv_w=conv_w, a_log=a_log, dt_bias=dt_bias, gdn_norm_w=gdn_norm_w,
                   q_norm_w=q_norm_w, w_uq=w_uq, kv_norm_w=kv_norm_w, w_ukv=w_ukv, mla_out_norm_w=mla_out_norm_w, w_out=w_out,
                   ffn_norm_w=ffn_norm_w, w_gate=w_gate, w_up=w_up, w_down=w_down, final_norm_w=final_norm_w)
    mom_m = dict(attn_norm_w=m_attn_norm_w, w_in=m_w_in, conv_w=m_conv_w, a_log=m_a_log, dt_bias=m_dt_bias, gdn_norm_w=m_gdn_norm_w,
                 q_norm_w=m_q_norm_w, w_uq=m_w_uq, kv_norm_w=m_kv_norm_w, w_ukv=m_w_ukv, mla_out_norm_w=m_mla_out_norm_w,
                 w_out=m_w_out, ffn_norm_w=m_ffn_norm_w, w_gate=m_w_gate, w_up=m_w_up, w_down=m_w_down, final_norm_w=m_final_norm_w)
    mom_v = dict(attn_norm_w=v_attn_norm_w, w_in=v_w_in, conv_w=v_conv_w, a_log=v_a_log, dt_bias=v_dt_bias, gdn_norm_w=v_gdn_norm_w,
                 q_norm_w=v_q_norm_w, w_uq=v_w_uq, kv_norm_w=v_kv_norm_w, w_ukv=v_w_ukv, mla_out_norm_w=v_mla_out_norm_w,
                 w_out=v_w_out, ffn_norm_w=v_ffn_norm_w, w_gate=v_w_gate, w_up=v_w_up, w_down=v_w_down, final_norm_w=v_final_norm_w)
    big_names = ("w_in", "w_uq", "w_ukv", "w_out", "w_gate", "w_up", "w_down")

    shard = {n: weights[n][0].astype(BF16) for n in big_names}
    g_in, g_uq, g_ukv, g_conv = _exchange([shard["w_in"], shard["w_uq"], shard["w_ukv"], weights["conv_w"][0]], [True] * 4,
                                          name="gather_weights")
    win_p = _win_to_padded(_unshard_cols(g_in))
    wuq_p = _wuq_to_padded(_unshard_cols(g_uq))
    wukv = _unshard_cols(g_ukv)
    late = [shard["w_out"], shard["w_gate"], shard["w_up"], shard["w_down"]]
    conv_full = jnp.concatenate([_unshard_cols(g_conv), jnp.zeros((8 - GDN_CONV, CONV_CH), F32)], axis=0)

    gdn_params = jnp.concatenate([_pad_lanes(a_log, HEAD)[None], _pad_lanes(dt_bias, HEAD)[None], jnp.zeros((6, HEAD), F32)], axis=0)
    small = {n: weights[n].reshape(1, -1) for n in ("attn_norm_w", "ffn_norm_w", "final_norm_w", "q_norm_w", "kv_norm_w",
                                                    "gdn_norm_w", "mla_out_norm_w")}
    small["gdn_params"] = gdn_params

    dx, big, sm = _local_step(x[0], positions.reshape(t, 1).astype(F32), loss_target[0], win_p, wuq_p, wukv, late,
                              conv_full, small, True)

    rows8 = lambda name: jnp.sum(sm[name], axis=0)
    pieces = [rows8(n) for n, _ in _SMALL[:7]]
    pieces += [_pad_lanes(jnp.sum(sm["gdn_params"][0:1], axis=0), HEAD), _pad_lanes(jnp.sum(sm["gdn_params"][1:2], axis=0), HEAD)]
    pieces.append(jnp.sum(sm["conv_w"], axis=1).reshape(-1))
    pieces.append(_pad_lanes(jnp.sum(sm["loss"]).reshape(1), HEAD))
    packed = _pad_lanes(jnp.concatenate(pieces), _PACK_ROWS * HEAD).reshape(_PACK_ROWS, HEAD)
    (r_small,) = _exchange([packed], [True], name="exchange_small")

    outs_g, outs_d, outs_m, outs_v = {}, {}, {}, {}
    for name in big_names:
        g, d, nm, nv = _adamw(big[name], weights[name][0], mom_m[name][0], mom_v[name][0], name="adamw_" + name)
        outs_g[name], outs_d[name], outs_m[name], outs_v[name] = g[None], d[None], nm[None], nv[None]

    total = _sum_parts(r_small, name="sum_small")
    flat = total.reshape(-1)
    loss = flat[(_SMALL_ROWS + _CONV_ROWS) * HEAD]
    g_small, off = {}, 0
    for n, size in _SMALL:
        g_small[n] = flat[off:off + size]
        off += size
    g_conv_full = flat[off:off + GDN_CONV * CONV_CH].reshape(GDN_CONV, CONV_CH)
    g_small["conv_w"] = lax.dynamic_slice(g_conv_full, (0, me * (CONV_CH // N_DEV)), (GDN_CONV, CONV_CH // N_DEV)).reshape(-1)
    order = [n for n, _ in _SMALL] + ["conv_w"]
    sizes = dict(_SMALL)
    sizes["conv_w"] = GDN_CONV * CONV_CH // N_DEV
    true_size = {n: weights[n].size for n in order}

    def pack(d):
        return jnp.concatenate([_pad_lanes(d[n], sizes[n]) for n in order]).reshape(1, -1, HEAD)

    g2, d2, m2, v2 = _adamw(pack(g_small), pack(weights)[0], pack(mom_m)[0], pack(mom_v)[0], name="adamw_small")
    off = 0
    for n in order:
        for src, dst in ((g2, outs_g), (d2, outs_d), (m2, outs_m), (v2, outs_v)):
            dst[n] = src.reshape(-1)[off:off + true_size[n]].reshape(weights[n].shape)
        off += sizes[n]

    names = ("attn_norm_w", "w_in", "conv_w", "a_log", "dt_bias", "gdn_norm_w", "q_norm_w", "w_uq", "kv_norm_w", "w_ukv",
             "mla_out_norm_w", "w_out", "ffn_norm_w", "w_gate", "w_up", "w_down", "final_norm_w")
    return (loss, dx[None], *[outs_g[n] for n in names], *[outs_d[n] for n in names], *[outs_m[n] for n in names],
            *[outs_v[n] for n in names])
```

```python
import functools
import math

import jax
import jax.numpy as jnp
from jax import lax
from jax.experimental import pallas as pl
from jax.experimental.pallas import tpu as pltpu

F32 = jnp.float32
BF16 = jnp.bfloat16

D_MODEL = 2048
GDN_HEADS = 8
HEAD = 128
GDN_CONV = 4
GDN_CHUNK = 64
GDN_QK = GDN_HEADS * HEAD
CONV_CH = 3 * GDN_QK
MLA_HEADS = 8
QK_ROPE = 64
Q_LORA = 512
KV_LORA = 512
ROPE_THETA = 10000.0
D_FF = 5632
EPS = 1e-6
IN_WIDTH = 5200
ADAM_LR, ADAM_B1, ADAM_B2, ADAM_EPS, ADAM_WD, ADAM_STEP = 0.001, 0.9, 0.999, 1e-08, 0.01, 10

PROJ_W = 5376
COL_Z = 3072
COL_CQ = 4096
COL_CKV = 4608
COL_MISC = 5120
LANE_B = 64
LANE_A = 72
QHEAD = 256
FF_WIDE = D_FF // 4
N_DEV = 8
MESH = pl.DeviceIdType.MESH
VMEM_LIMIT_MB = 48

NN = ((1,), (0,))
NT = ((1,), (1,))
TN = ((0,), (0,))


def _my_place():
    x, y, c = lax.axis_index("x"), lax.axis_index("y"), lax.axis_index("c")
    return x, y, c, 4 * x + 2 * y + c


def _peer(x, y, c, p):
    px, py, pc = x ^ ((p >> 2) & 1), y ^ ((p >> 1) & 1), c ^ (p & 1)
    return (px, py, pc), 4 * px + 2 * py + pc


class _Exchange:
    def __init__(self, arrays, gather):
        self.arrays, self.gather, self.n = list(arrays), list(gather), len(arrays)

    def out_shape(self):
        return [jax.ShapeDtypeStruct(((N_DEV,) + a.shape) if g else a.shape, a.dtype)
                for a, g in zip(self.arrays, self.gather)]

    def sems(self):
        return [pltpu.SemaphoreType.DMA((self.n * (N_DEV - 1),)), pltpu.SemaphoreType.DMA((self.n * (N_DEV - 1),)),
                pltpu.SemaphoreType.DMA((self.n,))]

    def _copies(self, ins, outs, sems):
        send_sems, recv_sems, local_sems = sems
        x, y, c, me = _my_place()
        local = [pltpu.make_async_copy(ins[k] if self.gather[k] else ins[k].at[me], outs[k].at[me], local_sems.at[k])
                 for k in range(self.n)]
        sent, received = [], []
        for p in range(1, N_DEV):
            place, num = _peer(x, y, c, p)
            for k in range(self.n):
                src = ins[k] if self.gather[k] else ins[k].at[num]
                idx = k * (N_DEV - 1) + p - 1
                mk = lambda dst: pltpu.make_async_remote_copy(src_ref=src, dst_ref=dst, send_sem=send_sems.at[idx],
                                                              recv_sem=recv_sems.at[idx], device_id=place, device_id_type=MESH)
                sent.append(mk(outs[k].at[me]))
                received.append(mk(outs[k].at[num]))
        return local, sent, received

    def start(self, ins, outs, sems):
        local, sent, _ = self._copies(ins, outs, sems)
        for cp in local + sent:
            cp.start()

    def forward(self, ins, outs, sems):
        pass

    def finish(self, ins, outs, sems):
        local, sent, received = self._copies(ins, outs, sems)
        for cp in received:
            cp.wait_recv()
        for cp in sent:
            cp.wait_send()
        for cp in local:
            cp.wait()


class _Gather:
    def __init__(self, arrays):
        self.arrays, self.n = list(arrays), len(arrays)

    def out_shape(self):
        return [jax.ShapeDtypeStruct((N_DEV,) + a.shape, a.dtype) for a in self.arrays]

    def sems(self):
        return [pltpu.SemaphoreType.DMA((self.n * (N_DEV - 1),)), pltpu.SemaphoreType.DMA((self.n * (N_DEV - 1),)),
                pltpu.SemaphoreType.DMA((self.n,))]

    def _plan(self, ins, outs, sems):
        send_sems, recv_sems, local_sems = sems
        x, y, c, me = _my_place()
        sibling = (x, y, 1 - c)
        chips = [(1 - x, y), (x, 1 - y), (1 - x, 1 - y)]
        num = lambda px, py, pc: 4 * px + 2 * py + pc

        def copy(k, i, block, to, src=None):
            slot = outs[k].at[num(*block)]
            return pltpu.make_async_remote_copy(src_ref=slot if src is None else src, dst_ref=slot,
                                                send_sem=send_sems.at[k * (N_DEV - 1) + i],
                                                recv_sem=recv_sems.at[k * (N_DEV - 1) + i],
                                                device_id=to, device_id_type=MESH)

        local = [pltpu.make_async_copy(ins[k], outs[k].at[me], local_sems.at[k]) for k in range(self.n)]
        return (x, y, c), sibling, chips, copy, local

    def start(self, ins, outs, sems):
        me, sibling, chips, copy, local = self._plan(ins, outs, sems)
        for cp in local:
            cp.start()
        for k in range(self.n):
            copy(k, 0, me, sibling, src=ins[k]).start()
            for j, chip in enumerate(chips):
                copy(k, 1 + j, me, (*chip, me[2]), src=ins[k]).start()

    def forward(self, ins, outs, sems):
        me, sibling, chips, copy, _ = self._plan(ins, outs, sems)
        for j, chip in enumerate(chips):
            for k in range(self.n):
                copy(k, 1 + j, (*chip, me[2]), me).wait_recv()
                copy(k, 4 + j, (*chip, me[2]), sibling).start()

    def finish(self, ins, outs, sems):
        me, sibling, chips, copy, local = self._plan(ins, outs, sems)
        for k in range(self.n):
            copy(k, 0, sibling, me).wait_recv()
            for j, chip in enumerate(chips):
                copy(k, 4 + j, (*chip, 1 - me[2]), me).wait_recv()
        for k in range(self.n):
            copy(k, 0, me, sibling, src=ins[k]).wait_send()
            for j, chip in enumerate(chips):
                copy(k, 1 + j, me, (*chip, me[2]), src=ins[k]).wait_send()
                copy(k, 4 + j, (*chip, me[2]), sibling).wait_send()
        for cp in local:
            cp.wait()


def _pcall(body, *, name, grid, in_specs, out_specs, out_shape, scratch=(), carry=None):
    params = pltpu.CompilerParams(dimension_semantics=("arbitrary",) * len(grid), vmem_limit_bytes=VMEM_LIMIT_MB << 20)
    if carry is None:
        return pl.pallas_call(body, name=name, grid=grid, in_specs=in_specs, out_specs=out_specs, out_shape=out_shape,
                              scratch_shapes=list(scratch), compiler_params=params)
    single = not isinstance(out_specs, (list, tuple))
    out_specs = [out_specs] if single else list(out_specs)
    out_shape = [out_shape] if single else list(out_shape)
    n_in, n_out, n_scr, na = len(in_specs), len(out_specs), len(scratch), carry.n

    def wrapped(*refs):
        ins, cin = refs[:n_in], refs[n_in:n_in + na]
        outs, cout = refs[n_in + na:n_in + na + n_out], refs[n_in + na + n_out:n_in + 2 * na + n_out]
        scr, sems = refs[n_in + 2 * na + n_out:n_in + 2 * na + n_out + n_scr], refs[n_in + 2 * na + n_out + n_scr:]
        total = math.prod(grid)
        step = functools.reduce(lambda a, d: a * grid[d] + pl.program_id(d), range(len(grid)), 0)

        @pl.when(step == 0)
        def _():
            carry.start(cin, cout, sems)

        body(*ins, *outs, *scr)

        @pl.when(step == min(total * 5 // 8, total - 1))
        def _():
            carry.forward(cin, cout, sems)

        @pl.when(step == total - 1)
        def _():
            carry.finish(cin, cout, sems)

    any_spec = pl.BlockSpec(memory_space=pl.ANY)
    call = pl.pallas_call(wrapped, name=name, grid=grid, in_specs=list(in_specs) + [any_spec] * na,
                          out_specs=out_specs + [any_spec] * na, out_shape=out_shape + carry.out_shape(),
                          scratch_shapes=list(scratch) + carry.sems(), compiler_params=params)

    def run(*args):
        res = call(*args, *carry.arrays)
        main = res[0] if single else list(res[:n_out])
        return main, list(res[n_out:])

    return run


def _pick(dim, pref):
    if dim <= pref:
        return dim
    c = pref
    while c >= 128:
        if dim % c == 0 and c % 128 == 0:
            return c
        c -= 128
    return dim


def _rows(t, width, target_bytes=2 << 20):
    r = max(8, min(t, target_bytes // (4 * width)))
    r = 1 << (r.bit_length() - 1)
    while t % r:
        r //= 2
    return r


MM_FULL_K = 2048


def _mm(a, b, *, name, ta=False, tb=False, res=None, out_dtype=F32, bm=1024, bn=1024, bk=1024, carry=None):
    m, k = (a.shape[1], a.shape[0]) if ta else a.shape
    n = b.shape[0] if tb else b.shape[1]
    assert (b.shape[1] if tb else b.shape[0]) == k
    bm, bn, bk = _pick(m, bm), _pick(n, bn), (k if k <= MM_FULL_K else _pick(k, bk))
    nk = k // bk
    dims = (((0,) if ta else (1,), (1,) if tb else (0,)), ((), ()))

    def body(*refs):
        a_ref, b_ref = refs[:2]
        r_ref = refs[2] if res is not None else None
        o_ref = refs[3] if res is not None else refs[2]
        part = lax.dot_general(a_ref[...].astype(BF16), b_ref[...].astype(BF16), dims, preferred_element_type=F32)

        def finish(out):
            if res is not None:
                out = out + r_ref[...]
            o_ref[...] = out.astype(o_ref.dtype)

        if nk == 1:
            finish(part)
            return
        acc_ref = refs[-1]
        kk = pl.program_id(2)

        @pl.when(kk == 0)
        def _():
            acc_ref[...] = part

        @pl.when((kk > 0) & (kk < nk - 1))
        def _():
            acc_ref[...] += part

        @pl.when(kk == nk - 1)
        def _():
            finish(acc_ref[...] + part)

    a_spec = pl.BlockSpec((bk, bm), lambda i, j, kk: (kk, i)) if ta else pl.BlockSpec((bm, bk), lambda i, j, kk: (i, kk))
    b_spec = pl.BlockSpec((bn, bk), lambda i, j, kk: (j, kk)) if tb else pl.BlockSpec((bk, bn), lambda i, j, kk: (kk, j))
    o_spec = pl.BlockSpec((bm, bn), lambda i, j, kk: (i, j))
    ins, specs = [a, b], [a_spec, b_spec]
    if res is not None:
        ins.append(res)
        specs.append(o_spec)
    return _pcall(body, name=name, grid=(m // bm, n // bn, nk), in_specs=specs, out_specs=o_spec,
                  out_shape=jax.ShapeDtypeStruct((m, n), out_dtype),
                  scratch=[pltpu.VMEM((bm, bn), F32)] if nk > 1 else [], carry=carry)(*ins)


def _rms_fwd(x, w, *, name, width, heads=1, col0=0, out_dtype=BF16):
    t = x.shape[0]
    tm = _rows(t, width)
    cb = col0 // width

    def body(x_ref, w_ref, o_ref):
        xv = x_ref[...]
        r = lax.rsqrt(jnp.mean(xv * xv, axis=-1, keepdims=True) + EPS)
        o_ref[...] = (xv * r * w_ref[...]).astype(o_ref.dtype)

    return _pcall(body, name=name, grid=(t // tm, heads),
                  in_specs=[pl.BlockSpec((tm, width), lambda i, h: (i, cb + h)),
                            pl.BlockSpec((1, width), lambda i, h: (0, 0))],
                  out_specs=pl.BlockSpec((tm, width), lambda i, h: (i, h)),
                  out_shape=jax.ShapeDtypeStruct((t, heads * width), out_dtype))(x, w)


def _rms_bwd(x, w, dy, *, name, width, heads=1, col0=0, dcol0=0, res=None, out_dtype=F32, with_delta=False):
    t = x.shape[0]
    tm = _rows(t, width)
    cb, dcb = col0 // width, dcol0 // width

    def body(*refs):
        refs = list(refs)
        x_ref, w_ref, dy_ref = refs[:3]
        r_ref = refs[3] if res is not None else None
        outs = refs[4:] if res is not None else refs[3:]
        dx_ref, dw_ref = outs[:2]
        xv = x_ref[...]
        dyv = dy_ref[...].astype(F32)
        r = lax.rsqrt(jnp.mean(xv * xv, axis=-1, keepdims=True) + EPS)
        xh = xv * r
        dyw = dyv * w_ref[...]
        dx = r * (dyw - xh * jnp.mean(dyw * xh, axis=-1, keepdims=True))
        if with_delta:
            outs[2][...] = jnp.broadcast_to(jnp.sum(dx * xv, axis=-1, keepdims=True), dx.shape)
        if res is not None:
            dx = dx + r_ref[...]
        dx_ref[...] = dx.astype(dx_ref.dtype)

        @pl.when((pl.program_id(0) == 0) & (pl.program_id(1) == 0))
        def _():
            dw_ref[...] = jnp.zeros_like(dw_ref)

        dw_ref[...] += (dyv * xh).reshape(tm // 8, 8, width).sum(axis=0)

    blk = pl.BlockSpec((tm, width), lambda i, h: (i, h))
    ins = [x, w, dy]
    specs = [pl.BlockSpec((tm, width), lambda i, h: (i, cb + h)), pl.BlockSpec((1, width), lambda i, h: (0, 0)),
             pl.BlockSpec((tm, width), lambda i, h: (i, dcb + h))]
    if res is not None:
        ins.append(res)
        specs.append(blk)
    out_shape = [jax.ShapeDtypeStruct((t, heads * width), out_dtype), jax.ShapeDtypeStruct((8, width), F32)]
    out_specs = [blk, pl.BlockSpec((8, width), lambda i, h: (0, 0))]
    if with_delta:
        out_shape.append(jax.ShapeDtypeStruct((t, heads * width), F32))
        out_specs.append(blk)
    return _pcall(body, name=name, grid=(t // tm, heads), in_specs=specs, out_specs=out_specs, out_shape=out_shape)(*ins)


def _sig(x):
    return 1.0 / (1.0 + jnp.exp(-x))


@jax.custom_vjp
def _sigmoid(x):
    return _sig(x)


def _sigmoid_fwd(x):
    s = _sig(x)
    return s, s


def _sigmoid_bwd(s, g):
    return (g * s * (1.0 - s),)


_sigmoid.defvjp(_sigmoid_fwd, _sigmoid_bwd)


@jax.custom_vjp
def _softplus(x):
    return jnp.maximum(x, 0.0) + jnp.log(1.0 + jnp.exp(-jnp.abs(x)))


def _softplus_fwd(x):
    return _softplus(x), x


def _softplus_bwd(x, g):
    return (g * _sig(x),)


_softplus.defvjp(_softplus_fwd, _softplus_bwd)


def _silu(x):
    return x * _sig(x)


def _dsilu(x):
    s = _sig(x)
    return s * (1.0 + x * (1.0 - s))


NN3 = (((2,), (1,)), ((0,), (0,)))
NT3 = (((2,), (2,)), ((0,), (0,)))
TN3 = (((1,), (1,)), ((0,), (0,)))


def _bdot(a, b, dims):
    return lax.dot_general(a.astype(BF16), b.astype(BF16), dims, preferred_element_type=F32)


def _bf16_part(x):
    bits = lax.bitcast_convert_type(x, jnp.uint32) & jnp.uint32(0xFFFF0000)
    return lax.bitcast_convert_type(bits, F32)


def _scan_rows(x, reverse):
    c = x.shape[1]
    row = lax.broadcasted_iota(jnp.int32, x.shape, 1)
    step = 1
    while step < c:
        if reverse:
            x = x + jnp.where(row < c - step, pltpu.roll(x, c - step, axis=1), 0.0)
        else:
            x = x + jnp.where(row >= step, pltpu.roll(x, step, axis=1), 0.0)
        step *= 2
    return x


@jax.custom_vjp
def _prefix_rows(x):
    return _scan_rows(x, False)


_prefix_rows.defvjp(lambda x: (_scan_rows(x, False), None), lambda _, g: (_scan_rows(g, True),))


def _dot3(a, b, dims):
    a_hi, b_hi = _bf16_part(a), _bf16_part(b)
    a_lo, b_lo = (a - a_hi).astype(BF16), (b - b_hi).astype(BF16)
    a_hi, b_hi = a_hi.astype(BF16), b_hi.astype(BF16)
    dot = lambda x, y: lax.dot_general(x, y, dims, preferred_element_type=F32)
    return dot(a_hi, b_hi) + (dot(a_hi, b_lo) + dot(a_lo, b_hi))


@jax.custom_vjp
def _nn_hi(a, b):
    return _dot3(a, b, NN3)


_nn_hi.defvjp(lambda a, b: (_dot3(a, b, NN3), (a, b)), lambda r, g: (_dot3(g, r[1], NT3), _dot3(r[0], g, TN3)))


@jax.custom_vjp
def _nn(a, b):
    return _bdot(a, b, NN3)


_nn.defvjp(lambda a, b: (_bdot(a, b, NN3), (a, b)), lambda r, g: (_bdot(g, r[1], NT3), _bdot(r[0], g, TN3)))


@jax.custom_vjp
def _nt(a, b):
    return _bdot(a, b, NT3)


_nt.defvjp(lambda a, b: (_bdot(a, b, NT3), (a, b)), lambda r, g: (_bdot(g, r[1], NN3), _bdot(g, r[0], TN3)))


@jax.custom_vjp
def _tn(a, b):
    return _bdot(a, b, TN3)


_tn.defvjp(lambda a, b: (_bdot(a, b, TN3), (a, b)), lambda r, g: (_bdot(r[1], g, NT3), _bdot(r[0], g, NN3)))


def _conv_pre(ext, w, rows):
    acc = w[0:1] * ext[5:5 + rows]
    for j in range(1, GDN_CONV):
        acc = acc + w[j:j + 1] * ext[5 + j:5 + j + rows]
    return acc


def _conv_fwd(proj, conv_w, *, name):
    t = proj.shape[0]
    tm, tc = _pick(t, 512), 512
    nb = tm // 8

    def body(u_ref, p_ref, w_ref, o_ref):
        i = pl.program_id(1)
        prev = jnp.where(i > 0, p_ref[...], 0.0)
        ext = jnp.concatenate([prev, u_ref[...]], axis=0)
        o_ref[...] = _silu(_conv_pre(ext, w_ref[...], tm))

    return _pcall(body, name=name, grid=(CONV_CH // tc, t // tm),
                  in_specs=[pl.BlockSpec((tm, tc), lambda j, i: (i, j)),
                            pl.BlockSpec((8, tc), lambda j, i: (jnp.maximum(i * nb - 1, 0), j)),
                            pl.BlockSpec((8, tc), lambda j, i: (0, j))],
                  out_specs=pl.BlockSpec((tm, tc), lambda j, i: (i, j)),
                  out_shape=jax.ShapeDtypeStruct((t, CONV_CH), F32))(proj, proj, conv_w)


def _conv_bwd(proj, conv_w, dy, *, name):
    t = proj.shape[0]
    tm, tc = _pick(t, 512), 512
    nb = tm // 8
    last = t // tm - 1

    def body(u_ref, p_ref, n_ref, dy_ref, dyn_ref, w_ref, du_ref, dw_ref):
        i = pl.program_id(1)
        w = w_ref[...]
        prev = jnp.where(i > 0, p_ref[...], 0.0)
        ext = jnp.concatenate([prev, u_ref[...], n_ref[...]], axis=0)
        c = _conv_pre(ext, w, tm + 8)
        dy_ext = jnp.concatenate([dy_ref[...], jnp.where(i < last, dyn_ref[...], 0.0)], axis=0)
        dc = dy_ext * _dsilu(c)
        du = w[3:4] * dc[0:tm]
        for j in range(GDN_CONV - 1):
            du = du + w[j:j + 1] * dc[3 - j:3 - j + tm]
        du_ref[...] = du.astype(du_ref.dtype)

        @pl.when(i == 0)
        def _():
            dw_ref[...] = jnp.zeros_like(dw_ref)

        for j in range(GDN_CONV):
            dw_ref[j] += (dc[0:tm] * ext[5 + j:5 + j + tm]).reshape(nb, 8, tc).sum(axis=0)

    cur = lambda j, i: (i, j)
    return _pcall(body, name=name, grid=(CONV_CH // tc, t // tm),
                  in_specs=[pl.BlockSpec((tm, tc), cur),
                            pl.BlockSpec((8, tc), lambda j, i: (jnp.maximum(i * nb - 1, 0), j)),
                            pl.BlockSpec((8, tc), lambda j, i: (jnp.minimum((i + 1) * nb, t // 8 - 1), j)),
                            pl.BlockSpec((tm, tc), cur),
                            pl.BlockSpec((8, tc), lambda j, i: (jnp.minimum((i + 1) * nb, t // 8 - 1), j)),
                            pl.BlockSpec((8, tc), lambda j, i: (0, j))],
                  out_specs=[pl.BlockSpec((tm, tc), cur), pl.BlockSpec((GDN_CONV, 8, tc), lambda j, i: (0, 0, j))],
                  out_shape=[jax.ShapeDtypeStruct((t, CONV_CH), BF16), jax.ShapeDtypeStruct((GDN_CONV, 8, CONV_CH), F32)],
                  )(proj, proj, proj, dy, dy, conv_w)


def _gdn_chunk(q_raw, k_raw, v, misc, params, state):
    nh, c = q_raw.shape[0], q_raw.shape[1]
    lane = lax.broadcasted_iota(jnp.int32, misc.shape, 1)
    prow = lax.broadcasted_iota(jnp.int32, params.shape, 0)
    plane = lax.broadcasted_iota(jnp.int32, params.shape, 1)
    heads = lambda pieces: jnp.concatenate([p[None] for p in pieces], axis=0)
    col = lambda at: heads([jnp.sum(jnp.where(lane == at + h, misc, 0.0), axis=1, keepdims=True) for h in range(nh)])
    par = lambda row: heads([jnp.sum(jnp.where((prow == row) & (plane == h), params, 0.0), keepdims=True)
                             for h in range(nh)])
    b_raw, a_raw = col(LANE_B), col(LANE_A)
    a_log, dt_bias = par(0), par(1)
    beta = _sigmoid(b_raw)
    g = -jnp.exp(a_log) * _softplus(a_raw + dt_bias)

    q = q_raw * lax.rsqrt(jnp.sum(q_raw * q_raw, axis=-1, keepdims=True) + EPS) * (HEAD ** -0.5)
    k = k_raw * lax.rsqrt(jnp.sum(k_raw * k_raw, axis=-1, keepdims=True) + EPS)

    ri = lax.broadcasted_iota(jnp.int32, (c, c), 0)
    ci = lax.broadcasted_iota(jnp.int32, (c, c), 1)
    tril, strict = ri >= ci, ri > ci
    gc = _prefix_rows(g)
    gc_col = jnp.broadcast_to(gc, (nh, c, c))
    gc_row = jnp.swapaxes(gc_col, 1, 2)
    decay = jnp.exp(jnp.where(tril, gc_col - gc_row, -1e30))

    kb = k * beta
    vb = v * beta
    a_mat = jnp.where(strict, _nt(kb, k) * decay, 0.0)
    x = -a_mat
    inv = (ri == ci).astype(F32) + x
    for _ in range(5):
        x = _nn_hi(x, x)
        inv = inv + _nn_hi(inv, x)
    u = _nn_hi(inv, vb)
    w = _nn_hi(inv, kb * jnp.exp(gc))
    intra = _nt(q, k) * decay

    v_new = u - _nn(w, state)
    o = _nn(q * jnp.exp(gc), state) + _nn(intra, v_new)
    g_last = jnp.sum(g, axis=1, keepdims=True)
    k_dec = k * jnp.exp(g_last - gc)
    new_state = state * jnp.exp(g_last) + _tn(k_dec, v_new)
    return o, new_state


def _gdn_specs(nc, rev):
    cidx = (lambda n: nc - 1 - n) if rev else (lambda n: n)
    hb = lambda part: pl.BlockSpec((GDN_CHUNK, GDN_QK), lambda n: (cidx(n), part))
    misc = pl.BlockSpec((GDN_CHUNK, HEAD), lambda n: (cidx(n), COL_MISC // HEAD))
    params = pl.BlockSpec((8, HEAD), lambda n: (0, 0))
    hist = pl.BlockSpec((1, GDN_HEADS, HEAD, HEAD), lambda n: (cidx(n), 0, 0, 0))
    return hb, misc, params, hist


def _split_heads(v):
    return jnp.stack([v[:, h * HEAD:(h + 1) * HEAD] for h in range(v.shape[1] // HEAD)])


def _merge_heads(v):
    return jnp.concatenate([v[h] for h in range(v.shape[0])], axis=1)


def _gdn_fwd(qkv, proj, params, *, name):
    t = qkv.shape[0]
    nc = t // GDN_CHUNK
    hb, misc, pspec, hist = _gdn_specs(nc, False)

    def body(q_ref, k_ref, v_ref, m_ref, p_ref, o_ref, hist_ref, s_ref):
        @pl.when(pl.program_id(0) == 0)
        def _():
            s_ref[...] = jnp.zeros_like(s_ref)

        state = s_ref[...]
        hist_ref[0] = state
        o, new_state = _gdn_chunk(_split_heads(q_ref[...]), _split_heads(k_ref[...]), _split_heads(v_ref[...]),
                                  m_ref[...], p_ref[...], state)
        o_ref[...] = _merge_heads(o)
        s_ref[...] = new_state

    return _pcall(body, name=name, grid=(nc,),
                  in_specs=[hb(0), hb(1), hb(2), misc, pspec],
                  out_specs=[hb(0), hist],
                  out_shape=[jax.ShapeDtypeStruct((t, GDN_QK), F32),
                             jax.ShapeDtypeStruct((nc, GDN_HEADS, HEAD, HEAD), F32)],
                  scratch=[pltpu.VMEM((GDN_HEADS, HEAD, HEAD), F32)])(qkv, qkv, qkv, proj, params)


def _gdn_bwd(qkv, proj, params, hist_arr, do, dmisc_in, *, name, carry=None):
    t = qkv.shape[0]
    nc = t // GDN_CHUNK
    hb, misc, pspec, hist = _gdn_specs(nc, True)
    mrow = pl.BlockSpec((GDN_CHUNK, HEAD), lambda n: (nc - 1 - n, 0))

    def body(q_ref, k_ref, v_ref, m_ref, p_ref, hist_ref, do_ref, dmi_ref, dqkv_ref, dm_ref, dp_ref, ds_ref):
        @pl.when(pl.program_id(0) == 0)
        def _():
            ds_ref[...] = jnp.zeros_like(ds_ref)
            dp_ref[...] = jnp.zeros_like(dp_ref)

        _, vjp = jax.vjp(_gdn_chunk, _split_heads(q_ref[...]), _split_heads(k_ref[...]), _split_heads(v_ref[...]),
                         m_ref[...], p_ref[...], hist_ref[0])
        dq, dk, dv, dm, dp, ds = vjp((_split_heads(do_ref[...]), ds_ref[...]))
        dqkv_ref[:, 0:GDN_QK] = _merge_heads(dq)
        dqkv_ref[:, GDN_QK:2 * GDN_QK] = _merge_heads(dk)
        dqkv_ref[:, 2 * GDN_QK:] = _merge_heads(dv)
        ds_ref[...] = ds
        dm_ref[...] = dmi_ref[...] + dm
        dp_ref[...] += dp

    return _pcall(body, name=name, grid=(nc,),
                  in_specs=[hb(0), hb(1), hb(2), misc, pspec, hist, hb(0), mrow],
                  out_specs=[pl.BlockSpec((GDN_CHUNK, CONV_CH), lambda n: (nc - 1 - n, 0)), mrow, pspec],
                  out_shape=[jax.ShapeDtypeStruct((t, CONV_CH), F32), jax.ShapeDtypeStruct((t, HEAD), F32),
                             jax.ShapeDtypeStruct((8, HEAD), F32)],
                  scratch=[pltpu.VMEM((GDN_HEADS, HEAD, HEAD), F32)], carry=carry,
                  )(qkv, qkv, qkv, proj, params, hist_arr, do, dmisc_in)


def _gate_fwd(o_raw, proj, w, *, name):
    t = o_raw.shape[0]
    tm = _rows(t, HEAD)
    zb = COL_Z // HEAD

    def body(o_ref, z_ref, w_ref, out_ref):
        ov = o_ref[...]
        r = lax.rsqrt(jnp.mean(ov * ov, axis=-1, keepdims=True) + EPS)
        out_ref[...] = (ov * r * w_ref[...] * _silu(z_ref[...])).astype(out_ref.dtype)

    blk = pl.BlockSpec((tm, HEAD), lambda i, h: (i, h))
    return _pcall(body, name=name, grid=(t // tm, GDN_HEADS),
                  in_specs=[blk, pl.BlockSpec((tm, HEAD), lambda i, h: (i, zb + h)), pl.BlockSpec((1, HEAD), lambda i, h: (0, 0))],
                  out_specs=blk, out_shape=jax.ShapeDtypeStruct((t, GDN_QK), BF16))(o_raw, proj, w)


def _gate_bwd(o_raw, proj, w, dmixed, *, name):
    t = o_raw.shape[0]
    tm = _rows(t, HEAD)
    zb = COL_Z // HEAD

    def body(o_ref, z_ref, w_ref, dy_ref, do_ref, dz_ref, dw_ref):
        ov, zv, dyv = o_ref[...], z_ref[...], dy_ref[...]
        r = lax.rsqrt(jnp.mean(ov * ov, axis=-1, keepdims=True) + EPS)
        xh = ov * r
        dn = dyv * _silu(zv)
        dz_ref[...] = (dyv * xh * w_ref[...] * _dsilu(zv)).astype(dz_ref.dtype)
        dnw = dn * w_ref[...]
        do_ref[...] = r * (dnw - xh * jnp.mean(dnw * xh, axis=-1, keepdims=True))

        @pl.when((pl.program_id(0) == 0) & (pl.program_id(1) == 0))
        def _():
            dw_ref[...] = jnp.zeros_like(dw_ref)

        dw_ref[...] += (dn * xh).reshape(tm // 8, 8, HEAD).sum(axis=0)

    blk = pl.BlockSpec((tm, HEAD), lambda i, h: (i, h))
    return _pcall(body, name=name, grid=(t // tm, GDN_HEADS),
                  in_specs=[blk, pl.BlockSpec((tm, HEAD), lambda i, h: (i, zb + h)), pl.BlockSpec((1, HEAD), lambda i, h: (0, 0)), blk],
                  out_specs=[blk, blk, pl.BlockSpec((8, HEAD), lambda i, h: (0, 0))],
                  out_shape=[jax.ShapeDtypeStruct((t, GDN_QK), F32), jax.ShapeDtypeStruct((t, GDN_QK), BF16),
                             jax.ShapeDtypeStruct((8, HEAD), F32)])(o_raw, proj, w, dmixed)


def _rope_tables():
    half = QK_ROPE // 2
    inv = ROPE_THETA ** (-jnp.arange(half, dtype=F32) / half)
    zeros = jnp.zeros((HEAD - QK_ROPE,), F32)
    inv_row = jnp.concatenate([inv, inv, zeros])
    sign_row = jnp.concatenate([-jnp.ones((half,), F32), jnp.ones((half,), F32), zeros])
    mask_row = jnp.concatenate([jnp.ones((QK_ROPE,), F32), zeros])
    return jnp.concatenate([inv_row[None], sign_row[None], mask_row[None], jnp.zeros((5, HEAD), F32)], axis=0)


def _rope_cs(pos, tab, *, name):
    t = pos.shape[0]
    tm = _pick(t, 1024)

    def body(pos_ref, tab_ref, o_ref):
        tab = tab_ref[...]
        ang = pos_ref[...] * tab[0:1]
        o_ref[...] = jnp.concatenate([jnp.cos(ang) * tab[2:3], jnp.sin(ang) * tab[1:2]], axis=1)

    return _pcall(body, name=name, grid=(t // tm,),
                  in_specs=[pl.BlockSpec((tm, 1), lambda i: (i, 0)), pl.BlockSpec((8, HEAD), lambda i: (0, 0))],
                  out_specs=pl.BlockSpec((tm, 2 * HEAD), lambda i: (i, 0)),
                  out_shape=jax.ShapeDtypeStruct((t, 2 * HEAD), F32))(pos, tab)


def _rotate(x, cs, sign):
    lane = lax.broadcasted_iota(jnp.int32, x.shape, 1)
    half = QK_ROPE // 2
    partner = jnp.where(lane < half, pltpu.roll(x, HEAD - half, axis=1), pltpu.roll(x, half, axis=1))
    return x * cs[:, :HEAD] + partner * (cs[:, HEAD:] * sign)


def _q_rot(q, cs, *, name, sign, out_dtype=BF16):
    t = q.shape[0]
    tm = _pick(t, 1024)
    scale = (HEAD + QK_ROPE) ** -0.5

    def body(q_ref, cs_ref, o_ref):
        qv = q_ref[...].astype(F32)
        rot = _rotate(qv[:, HEAD:], cs_ref[...], sign)
        o_ref[...] = (jnp.concatenate([qv[:, :HEAD], rot], axis=1) * scale).astype(o_ref.dtype)

    blk = pl.BlockSpec((tm, QHEAD), lambda i, h: (i, h))
    return _pcall(body, name=name, grid=(t // tm, MLA_HEADS),
                  in_specs=[blk, pl.BlockSpec((tm, 2 * HEAD), lambda i, h: (i, 0))],
                  out_specs=blk, out_shape=jax.ShapeDtypeStruct((t, MLA_HEADS * QHEAD), out_dtype))(q, cs)


def _kv_prep(kv, proj, cs, *, name):
    t = kv.shape[0]
    tm = _pick(t, 1024)

    def body(kv_ref, m_ref, cs_ref, k_ref, v_ref):
        kvv = kv_ref[...]
        misc = m_ref[...]
        lane = lax.broadcasted_iota(jnp.int32, misc.shape, 1)
        rot = _rotate(jnp.where(lane < QK_ROPE, misc, 0.0), cs_ref[...], 1.0)
        k_ref[...] = jnp.concatenate([kvv[:, :HEAD], rot], axis=1).astype(k_ref.dtype)
        v_ref[...] = kvv[:, HEAD:].astype(v_ref.dtype)

    blk = pl.BlockSpec((tm, QHEAD), lambda i, h: (i, h))
    return _pcall(body, name=name, grid=(t // tm, MLA_HEADS),
                  in_specs=[blk, pl.BlockSpec((tm, HEAD), lambda i, h: (i, COL_MISC // HEAD)),
                            pl.BlockSpec((tm, 2 * HEAD), lambda i, h: (i, 0))],
                  out_specs=[blk, pl.BlockSpec((tm, HEAD), lambda i, h: (i, h))],
                  out_shape=[jax.ShapeDtypeStruct((t, MLA_HEADS * QHEAD), BF16), jax.ShapeDtypeStruct((t, MLA_HEADS * HEAD), BF16)],
                  )(kv, proj, cs)


def _krope_bwd(dkr, cs, *, name):
    t = dkr.shape[0]
    tm = _pick(t, 512)

    def body(d_ref, cs_ref, o_ref):
        d = d_ref[...]
        acc = d[:, :HEAD]
        for h in range(1, MLA_HEADS):
            acc = acc + d[:, h * HEAD:(h + 1) * HEAD]
        o_ref[...] = _rotate(acc, cs_ref[...], -1.0)

    return _pcall(body, name=name, grid=(t // tm,),
                  in_specs=[pl.BlockSpec((tm, MLA_HEADS * HEAD), lambda i: (i, 0)), pl.BlockSpec((tm, 2 * HEAD), lambda i: (i, 0))],
                  out_specs=pl.BlockSpec((tm, HEAD), lambda i: (i, 0)),
                  out_shape=jax.ShapeDtypeStruct((t, HEAD), F32))(dkr, cs)


NEG = -1e30


def _tri(step, counts):
    starts = [sum(counts[:o]) for o in range(len(counts))]
    outer = sum([(step >= s).astype(jnp.int32) for s in starts[1:]], jnp.int32(0))
    start = sum([(step >= starts[o]).astype(jnp.int32) * (starts[o] - starts[o - 1]) for o in range(1, len(counts))], jnp.int32(0))
    return outer, step - start


def _attn_fwd(q, k, v, *, name, tq=1024, tk=1024, carry=None):
    t = q.shape[0]
    tq, tk = _pick(t, tq), _pick(t, tk)
    nq = t // tq
    last_kv = lambda i: (i * tq + tq - 1) // tk
    counts = [last_kv(i) + 1 for i in range(nq)]

    def body(q_ref, k_ref, v_ref, o_ref, lse_ref, m_ref, l_ref, acc_ref):
        i, j = _tri(pl.program_id(1), counts)

        @pl.when(j == 0)
        def _():
            m_ref[...] = jnp.full_like(m_ref, NEG)
            l_ref[...] = jnp.zeros_like(l_ref)
            acc_ref[...] = jnp.zeros_like(acc_ref)

        def step(masked):
            s = lax.dot_general(q_ref[...], k_ref[...], (NT, ((), ())), preferred_element_type=F32)
            if masked:
                qpos = i * tq + lax.broadcasted_iota(jnp.int32, s.shape, 0)
                kpos = j * tk + lax.broadcasted_iota(jnp.int32, s.shape, 1)
                s = jnp.where(kpos <= qpos, s, NEG)
            m_prev = m_ref[...]
            m_new = jnp.maximum(m_prev, jnp.max(s, axis=1, keepdims=True))
            alpha = jnp.exp(m_prev - m_new)
            p = jnp.exp(s - m_new)
            l_ref[...] = alpha * l_ref[...] + jnp.sum(p, axis=1, keepdims=True)
            acc_ref[...] = alpha * acc_ref[...] + lax.dot_general(p.astype(BF16), v_ref[...], (NN, ((), ())),
                                                                  preferred_element_type=F32)
            m_ref[...] = m_new

        crosses = j * tk + tk - 1 > i * tq

        @pl.when(crosses)
        def _():
            step(True)

        @pl.when(jnp.logical_not(crosses))
        def _():
            step(False)

        @pl.when(j == last_kv(i))
        def _():
            o_ref[...] = acc_ref[...] / l_ref[...]
            lse_ref[...] = jnp.broadcast_to(m_ref[...] + jnp.log(l_ref[...]), lse_ref.shape)

    qblk = pl.BlockSpec((tq, QHEAD), lambda h, s: (_tri(s, counts)[0], h))
    oblk = pl.BlockSpec((tq, HEAD), lambda h, s: (_tri(s, counts)[0], h))
    return _pcall(body, name=name, grid=(MLA_HEADS, sum(counts)),
                  in_specs=[qblk, pl.BlockSpec((tk, QHEAD), lambda h, s: (_tri(s, counts)[1], h)),
                            pl.BlockSpec((tk, HEAD), lambda h, s: (_tri(s, counts)[1], h))],
                  out_specs=[oblk, oblk],
                  out_shape=[jax.ShapeDtypeStruct((t, MLA_HEADS * HEAD), F32), jax.ShapeDtypeStruct((t, MLA_HEADS * HEAD), F32)],
                  scratch=[pltpu.VMEM((tq, 1), F32), pltpu.VMEM((tq, 1), F32), pltpu.VMEM((tq, HEAD), F32)],
                  carry=carry)(q, k, v)


def _attn_bwd(q, k, v, do, lse, delta, *, name, tq=1024, tk=1024, carry=None):
    t = q.shape[0]
    tq, tk = _pick(t, tq), _pick(t, tk)
    nq, nk = t // tq, t // tk
    first_q = lambda j: (j * tk) // tq
    counts = [nq - first_q(j) for j in range(nk)]

    def where(step):
        j, off = _tri(step, counts)
        return j, first_q(j) + off

    lanes = lambda col: jnp.tile(col, (1, tk // HEAD))

    def body(q_ref, k_ref, v_ref, do_ref, lse_ref, dl_ref, dq_ref, dkv_ref, dkr_ref, dk_acc, dv_acc):
        j, i = where(pl.program_id(1))

        @pl.when(i == first_q(j))
        def _():
            dk_acc[...] = jnp.zeros_like(dk_acc)
            dv_acc[...] = jnp.zeros_like(dv_acc)

        def step(masked):
            qv, kv_, dov = q_ref[...], k_ref[...], do_ref[...].astype(BF16)
            s = lax.dot_general(qv, kv_, (NT, ((), ())), preferred_element_type=F32)
            p = jnp.exp((s - lanes(lse_ref[...])).astype(BF16))
            if masked:
                qpos = i * tq + lax.broadcasted_iota(jnp.int32, s.shape, 0)
                kpos = j * tk + lax.broadcasted_iota(jnp.int32, s.shape, 1)
                p = jnp.where(kpos <= qpos, p, jnp.zeros_like(p))
            dv_acc[...] += lax.dot_general(p, dov, (TN, ((), ())), preferred_element_type=F32)
            dp = lax.dot_general(dov, v_ref[...], (NT, ((), ())), preferred_element_type=F32)
            ds = p * (dp - lanes(dl_ref[...])).astype(BF16)
            dk_acc[...] += lax.dot_general(ds, qv, (TN, ((), ())), preferred_element_type=F32)
            contrib = lax.dot_general(ds, kv_, (NN, ((), ())), preferred_element_type=F32)
            rows = pl.ds(pl.multiple_of(i * tq, tq), tq)

            @pl.when(j == 0)
            def _():
                dq_ref[rows, :] = contrib

            @pl.when(j > 0)
            def _():
                dq_ref[rows, :] += contrib

        crosses = j * tk + tk - 1 > i * tq

        @pl.when(crosses)
        def _():
            step(True)

        @pl.when(jnp.logical_not(crosses))
        def _():
            step(False)

        @pl.when(i == nq - 1)
        def _():
            dk = dk_acc[...]
            dkv_ref[...] = jnp.concatenate([dk[:, :HEAD], dv_acc[...]], axis=1).astype(dkv_ref.dtype)
            dkr_ref[...] = dk[:, HEAD:]

    qi = lambda h, s: (where(s)[1], h)
    kj = lambda h, s: (where(s)[0], h)
    return _pcall(body, name=name, grid=(MLA_HEADS, sum(counts)),
                  in_specs=[pl.BlockSpec((tq, QHEAD), qi), pl.BlockSpec((tk, QHEAD), kj), pl.BlockSpec((tk, HEAD), kj),
                            pl.BlockSpec((tq, HEAD), qi), pl.BlockSpec((tq, HEAD), qi), pl.BlockSpec((tq, HEAD), qi)],
                  out_specs=[pl.BlockSpec((t, QHEAD), lambda h, s: (0, h)), pl.BlockSpec((tk, QHEAD), kj),
                             pl.BlockSpec((tk, HEAD), kj)],
                  out_shape=[jax.ShapeDtypeStruct((t, MLA_HEADS * QHEAD), F32), jax.ShapeDtypeStruct((t, MLA_HEADS * QHEAD), BF16),
                             jax.ShapeDtypeStruct((t, MLA_HEADS * HEAD), F32)],
                  scratch=[pltpu.VMEM((tk, QHEAD), F32), pltpu.VMEM((tk, HEAD), F32)], carry=carry)(q, k, v, do, lse, delta)


def _ffn_up(h, wgate, wup, *, name, bm=512, bn=FF_WIDE):
    t = h.shape[0]
    bm = _pick(t, bm)

    def body(h_ref, wg_ref, wu_ref, g_ref, u_ref, a_ref):
        hv = h_ref[...]
        g = lax.dot_general(hv, wg_ref[...], (NN, ((), ())), preferred_element_type=F32)
        u = lax.dot_general(hv, wu_ref[...], (NN, ((), ())), preferred_element_type=F32)
        g_ref[...] = g.astype(g_ref.dtype)
        u_ref[...] = u.astype(u_ref.dtype)
        a_ref[...] = (_silu(g) * u).astype(a_ref.dtype)

    w_spec = pl.BlockSpec((D_MODEL, bn), lambda j, i: (0, j))
    o_spec = pl.BlockSpec((bm, bn), lambda j, i: (i, j))
    sds = jax.ShapeDtypeStruct((t, D_FF), BF16)
    return _pcall(body, name=name, grid=(D_FF // bn, t // bm),
                  in_specs=[pl.BlockSpec((bm, D_MODEL), lambda j, i: (i, 0)), w_spec, w_spec],
                  out_specs=[o_spec] * 3, out_shape=[sds] * 3)(h, wgate, wup)


def _ffn_down_dx(dy, wdown, gate, up, *, name, bm=512, bn=FF_WIDE):
    t = dy.shape[0]
    bm = _pick(t, bm)

    def body(dy_ref, w_ref, g_ref, u_ref, dg_ref, du_ref):
        d = lax.dot_general(dy_ref[...].astype(BF16), w_ref[...], (NT, ((), ())), preferred_element_type=F32)
        g = g_ref[...].astype(F32)
        dg_ref[...] = (d * u_ref[...].astype(F32) * _dsilu(g)).astype(dg_ref.dtype)
        du_ref[...] = (d * _silu(g)).astype(du_ref.dtype)

    o_spec = pl.BlockSpec((bm, bn), lambda j, i: (i, j))
    sds = jax.ShapeDtypeStruct((t, D_FF), BF16)
    return _pcall(body, name=name, grid=(D_FF // bn, t // bm),
                  in_specs=[pl.BlockSpec((bm, D_MODEL), lambda j, i: (i, 0)), pl.BlockSpec((bn, D_MODEL), lambda j, i: (j, 0)),
                            o_spec, o_spec],
                  out_specs=[o_spec, o_spec], out_shape=[sds, sds])(dy, wdown, gate, up)


def _loss_bwd(x2, w, target, *, name):
    t = x2.shape[0]
    tm = _rows(t, D_MODEL)

    def body(x_ref, w_ref, t_ref, dx_ref, dw_ref, l_ref):
        xv, wv = x_ref[...], w_ref[...]
        r = lax.rsqrt(jnp.mean(xv * xv, axis=-1, keepdims=True) + EPS)
        xh = xv * r
        err = xh * wv - t_ref[...]
        dy = err * (1.0 / D_MODEL)
        dyw = dy * wv
        dx_ref[...] = r * (dyw - xh * jnp.mean(dyw * xh, axis=-1, keepdims=True))

        @pl.when(pl.program_id(0) == 0)
        def _():
            dw_ref[...] = jnp.zeros_like(dw_ref)
            l_ref[...] = jnp.zeros_like(l_ref)

        dw_ref[...] += (dy * xh).reshape(tm // 8, 8, D_MODEL).sum(axis=0)
        sq = (err * err).reshape(tm // 8, 8, D_MODEL).sum(axis=0)
        part = sq[:, :HEAD]
        for c in range(1, D_MODEL // HEAD):
            part = part + sq[:, c * HEAD:(c + 1) * HEAD]
        l_ref[...] += part * (0.5 / D_MODEL)

    row = pl.BlockSpec((tm, D_MODEL), lambda i: (i, 0))
    return _pcall(body, name=name, grid=(t // tm,),
                  in_specs=[row, pl.BlockSpec((1, D_MODEL), lambda i: (0, 0)), row],
                  out_specs=[row, pl.BlockSpec((8, D_MODEL), lambda i: (0, 0)), pl.BlockSpec((8, HEAD), lambda i: (0, 0))],
                  out_shape=[jax.ShapeDtypeStruct((t, D_MODEL), F32), jax.ShapeDtypeStruct((8, D_MODEL), F32),
                             jax.ShapeDtypeStruct((8, HEAD), F32)])(x2, w, target)


def _unshard_cols(g):
    return jnp.transpose(g, (1, 0, 2)).reshape(g.shape[1], N_DEV * g.shape[2])


def _shard_cols(w):
    return jnp.transpose(w.reshape(w.shape[0], N_DEV, w.shape[1] // N_DEV), (1, 0, 2))


def _win_to_padded(w):
    pad = jnp.zeros((w.shape[0], PROJ_W - IN_WIDTH), w.dtype)
    return jnp.concatenate([w[:, :4096], w[:, 4112:5136], w[:, 5136:5200], w[:, 4096:4112], pad], axis=1)


def _win_from_padded(d):
    return jnp.concatenate([d[:, :4096], d[:, 5184:5200], d[:, 4096:5120], d[:, 5120:5184]], axis=1)


def _wuq_to_padded(w):
    w3 = w.reshape(w.shape[0], MLA_HEADS, HEAD + QK_ROPE)
    return jnp.pad(w3, ((0, 0), (0, 0), (0, QHEAD - HEAD - QK_ROPE))).reshape(w.shape[0], MLA_HEADS * QHEAD)


def _wuq_from_padded(d):
    return d.reshape(d.shape[0], MLA_HEADS, QHEAD)[:, :, :HEAD + QK_ROPE].reshape(d.shape[0], MLA_HEADS * (HEAD + QK_ROPE))


def _late_weights(g_out, g_gate, g_up, g_down):
    return g_out.reshape(D_MODEL, D_MODEL), _unshard_cols(g_gate), _unshard_cols(g_up), g_down.reshape(D_FF, D_MODEL)


def _local_step(x, pos, target, win_p, wuq_p, wukv, late, conv_w, small, exchange):
    cs = _rope_cs(pos, _rope_tables(), name="rope_cs")
    if not exchange:
        wout, wgate, wup, wdown = late
    h1 = _rms_fwd(x, small["attn_norm_w"], name="rms1_fwd", width=D_MODEL)
    proj = _mm(h1, win_p, name="mm_in", bn=768)
    qkv = _conv_fwd(proj, conv_w, name="conv_fwd")
    o_gdn_raw, hist = _gdn_fwd(qkv, proj, small["gdn_params"], name="gdn_fwd")
    o_gdn = _gate_fwd(o_gdn_raw, proj, small["gdn_norm_w"], name="gate_fwd")
    cqn = _rms_fwd(proj, small["q_norm_w"], name="rmsq_fwd", width=Q_LORA, col0=COL_CQ)
    ckvn = _rms_fwd(proj, small["kv_norm_w"], name="rmskv_fwd", width=KV_LORA, col0=COL_CKV)
    q_pre = _mm(cqn, wuq_p, name="mm_uq")
    kv = _mm(ckvn, wukv, name="mm_ukv")
    q_full = _q_rot(q_pre, cs, name="q_rot", sign=1.0)
    k_full, v_b = _kv_prep(kv, proj, cs, name="kv_prep")
    if exchange:
        (o_mla_raw, lse), gathered = _attn_fwd(q_full, k_full, v_b, name="attn_fwd", carry=_Gather(late))
        wout, wgate, wup, wdown = _late_weights(*gathered)
    else:
        o_mla_raw, lse = _attn_fwd(q_full, k_full, v_b, name="attn_fwd")
    o_mla = _rms_fwd(o_mla_raw, small["mla_out_norm_w"], name="rmso_fwd", width=HEAD, heads=MLA_HEADS)
    mixed = jnp.concatenate([o_gdn, o_mla], axis=1)
    x1 = _mm(mixed, wout, name="mm_out", res=x)
    h2 = _rms_fwd(x1, small["ffn_norm_w"], name="rms2_fwd", width=D_MODEL)
    gate, up, act = _ffn_up(h2, wgate, wup, name="ffn_up")
    x2 = _mm(act, wdown, name="mm_down", res=x1, bk=FF_WIDE)
    dx2, dw_final, loss_part = _loss_bwd(x2, small["final_norm_w"], target, name="loss_bwd")
    dgate, dup = _ffn_down_dx(dx2, wdown, gate, up, name="ffn_down_dx")
    d_wdown = _mm(act, dx2, name="mm_down_dw", ta=True, out_dtype=BF16, bm=FF_WIDE)
    dh2 = _mm(dgate, wgate, name="mm_gate_dx", tb=True, bk=FF_WIDE)
    dh2 = _mm(dup, wup, name="mm_up_dx", tb=True, res=dh2, bk=FF_WIDE)
    d_wgate = _mm(h2, dgate, name="mm_gate_dw", ta=True, out_dtype=BF16, bn=FF_WIDE)
    d_wup = _mm(h2, dup, name="mm_up_dw", ta=True, out_dtype=BF16, bn=FF_WIDE)
    dx1, dw_ffn = _rms_bwd(x1, small["ffn_norm_w"], dh2, name="rms2_bwd", width=D_MODEL, res=dx2)
    dmixed = _mm(dx1, wout, name="mm_out_dx", tb=True)
    d_wout = _mm(mixed, dx1, name="mm_out_dw", ta=True, out_dtype=BF16)
    do_mla, dw_mla_out, delta = _rms_bwd(o_mla_raw, small["mla_out_norm_w"], dmixed, name="rmso_bwd", width=HEAD,
                                         heads=MLA_HEADS, dcol0=GDN_QK, with_delta=True, out_dtype=BF16)
    if exchange:
        send = [d_wdown.reshape(N_DEV, D_FF // N_DEV, D_MODEL), _shard_cols(d_wgate), _shard_cols(d_wup)]
        (dq_full, dkv, dkr_h), (r_down, r_gate, r_up) = _attn_bwd(q_full, k_full, v_b, do_mla, lse, delta, name="attn_bwd",
                                                                  carry=_Exchange(send, [False] * 3))
    else:
        dq_full, dkv, dkr_h = _attn_bwd(q_full, k_full, v_b, do_mla, lse, delta, name="attn_bwd")
    dq_pre = _q_rot(dq_full, cs, name="q_rot_bwd", sign=-1.0)
    dmisc_kr = _krope_bwd(dkr_h, cs, name="krope_bwd")
    dcqn = _mm(dq_pre, wuq_p, name="mm_uq_dx", tb=True)
    d_wuq = _mm(cqn, dq_pre, name="mm_uq_dw", ta=True, out_dtype=BF16)
    dckvn = _mm(dkv, wukv, name="mm_ukv_dx", tb=True)
    d_wukv = _mm(ckvn, dkv, name="mm_ukv_dw", ta=True, out_dtype=BF16)
    dcq, dw_qn = _rms_bwd(proj, small["q_norm_w"], dcqn, name="rmsq_bwd", width=Q_LORA, col0=COL_CQ, out_dtype=BF16)
    dckv, dw_kvn = _rms_bwd(proj, small["kv_norm_w"], dckvn, name="rmskv_bwd", width=KV_LORA, col0=COL_CKV, out_dtype=BF16)
    do_gdn, dz, dw_gdn = _gate_bwd(o_gdn_raw, proj, small["gdn_norm_w"], dmixed, name="gate_bwd")
    if exchange:
        send = [d_wout.reshape(N_DEV, D_MODEL // N_DEV, D_MODEL), _shard_cols(_wuq_from_padded(d_wuq)), _shard_cols(d_wukv)]
        (dqkv, dmisc, d_params), (r_out, r_uq, r_ukv) = _gdn_bwd(
            qkv, proj, small["gdn_params"], hist, do_gdn, dmisc_kr, name="gdn_bwd", carry=_Exchange(send, [False] * 3))
    else:
        dqkv, dmisc, d_params = _gdn_bwd(qkv, proj, small["gdn_params"], hist, do_gdn, dmisc_kr, name="gdn_bwd")
    dqkv_pre, dconv = _conv_bwd(proj, conv_w, dqkv, name="conv_bwd")
    dproj = jnp.concatenate([dqkv_pre, dz, dcq, dckv, dmisc.astype(BF16), jnp.zeros((x.shape[0], PROJ_W - COL_MISC - HEAD), BF16)], axis=1)
    d_win = _mm(h1, dproj, name="mm_in_dw", ta=True, out_dtype=BF16, bn=768)
    if exchange:
        dh1, (r_in,) = _mm(dproj, win_p, name="mm_in_dx", tb=True, bk=768,
                           carry=_Exchange([_shard_cols(_win_from_padded(d_win))], [False]))
        d_win = r_in
    else:
        dh1 = _mm(dproj, win_p, name="mm_in_dx", tb=True, bk=768)
    dx, dw_attn = _rms_bwd(x, small["attn_norm_w"], dh1, name="rms1_bwd", width=D_MODEL, res=dx1)

    if exchange:
        big = {"w_in": d_win, "w_uq": r_uq, "w_ukv": r_ukv, "w_out": r_out, "w_gate": r_gate, "w_up": r_up, "w_down": r_down}
    else:
        big = {"w_in": d_win, "w_uq": d_wuq, "w_ukv": d_wukv, "w_out": d_wout, "w_gate": d_wgate, "w_up": d_wup,
               "w_down": d_wdown}
    sm = {"attn_norm_w": dw_attn, "ffn_norm_w": dw_ffn, "final_norm_w": dw_final, "q_norm_w": dw_qn, "kv_norm_w": dw_kvn,
          "gdn_norm_w": dw_gdn, "mla_out_norm_w": dw_mla_out, "gdn_params": d_params, "conv_w": dconv, "loss": loss_part}
    return dx, big, sm


def _exchange(ex, *, name):
    def body(*refs):
        ins, outs, sems = refs[:ex.n], refs[ex.n:2 * ex.n], refs[2 * ex.n:]
        ex.start(ins, outs, sems)
        ex.forward(ins, outs, sems)
        ex.finish(ins, outs, sems)

    any_spec = pl.BlockSpec(memory_space=pl.ANY)
    return pl.pallas_call(body, name=name, in_specs=[any_spec] * ex.n, out_specs=[any_spec] * ex.n,
                          out_shape=ex.out_shape(), scratch_shapes=ex.sems())(*ex.arrays)


def _adamw_math(g, w, m, v):
    m = ADAM_B1 * m + (1.0 - ADAM_B1) * g
    v = ADAM_B2 * v + (1.0 - ADAM_B2) * (g * g)
    m_hat = m / (1.0 - ADAM_B1 ** ADAM_STEP)
    v_hat = v / (1.0 - ADAM_B2 ** ADAM_STEP)
    delta = -ADAM_LR * (m_hat / (jnp.sqrt(v_hat) + ADAM_EPS) + ADAM_WD * w)
    return delta, m, v


def _adamw(parts, w, m, v, *, name):
    npart, r, c = parts.shape
    tr = r if r * c * 4 <= (1 << 20) else _rows(r, c, 1 << 20)

    def body(p_ref, w_ref, m_ref, v_ref, g_ref, d_ref, nm_ref, nv_ref):
        g = p_ref[0].astype(F32)
        for s in range(1, npart):
            g = g + p_ref[s].astype(F32)
        g_ref[...] = g
        d_ref[...], nm_ref[...], nv_ref[...] = _adamw_math(g, w_ref[...], m_ref[...], v_ref[...])

    blk = pl.BlockSpec((tr, c), lambda i: (i, 0))
    sds = jax.ShapeDtypeStruct((r, c), F32)
    return _pcall(body, name=name, grid=(r // tr,),
                  in_specs=[pl.BlockSpec((npart, tr, c), lambda i: (0, i, 0)), blk, blk, blk],
                  out_specs=[blk] * 4, out_shape=[sds] * 4)(parts, w, m, v)


def _sum_parts(parts, *, name):
    npart, r, c = parts.shape

    def body(p_ref, o_ref):
        g = p_ref[0]
        for s in range(1, npart):
            g = g + p_ref[s]
        o_ref[...] = g

    return _pcall(body, name=name, grid=(1,), in_specs=[pl.BlockSpec((npart, r, c), lambda i: (0, 0, 0))],
                  out_specs=pl.BlockSpec((r, c), lambda i: (0, 0)), out_shape=jax.ShapeDtypeStruct((r, c), F32))(parts)


_SMALL = (("attn_norm_w", D_MODEL), ("ffn_norm_w", D_MODEL), ("final_norm_w", D_MODEL), ("q_norm_w", Q_LORA),
          ("kv_norm_w", KV_LORA), ("gdn_norm_w", HEAD), ("mla_out_norm_w", HEAD), ("a_log", HEAD), ("dt_bias", HEAD))
_SMALL_ROWS = sum(n for _, n in _SMALL) // HEAD
_CONV_ROWS = GDN_CONV * CONV_CH // HEAD
_PACK_ROWS = 160


def _pad_lanes(v, n):
    v = v.reshape(-1)
    return jnp.concatenate([v, jnp.zeros((n - v.shape[0],), v.dtype)])


def kernel(x, positions, attn_norm_w, w_in, conv_w, a_log, dt_bias, gdn_norm_w, q_norm_w, w_uq, kv_norm_w, w_ukv, mla_out_norm_w, w_out, ffn_norm_w, w_gate, w_up, w_down, final_norm_w, loss_target, m_attn_norm_w, m_w_in, m_conv_w, m_a_log, m_dt_bias, m_gdn_norm_w, m_q_norm_w, m_w_uq, m_kv_norm_w, m_w_ukv, m_mla_out_norm_w, m_w_out, m_ffn_norm_w, m_w_gate, m_w_up, m_w_down, m_final_norm_w, v_attn_norm_w, v_w_in, v_conv_w, v_a_log, v_dt_bias, v_gdn_norm_w, v_q_norm_w, v_w_uq, v_kv_norm_w, v_w_ukv, v_mla_out_norm_w, v_w_out, v_ffn_norm_w, v_w_gate, v_w_up, v_w_down, v_final_norm_w):
    t = x.shape[1]
    me = 4 * lax.axis_index("x") + 2 * lax.axis_index("y") + lax.axis_index("c")
    weights = dict(attn_norm_w=attn_norm_w, w_in=w_in, conv_w=conv_w, a_log=a_log, dt_bias=dt_bias, gdn_norm_w=gdn_norm_w,
                   q_norm_w=q_norm_w, w_uq=w_uq, kv_norm_w=kv_norm_w, w_ukv=w_ukv, mla_out_norm_w=mla_out_norm_w, w_out=w_out,
                   ffn_norm_w=ffn_norm_w, w_gate=w_gate, w_up=w_up, w_down=w_down, final_norm_w=final_norm_w)
    mom_m = dict(attn_norm_w=m_attn_norm_w, w_in=m_w_in, conv_w=m_conv_w, a_log=m_a_log, dt_bias=m_dt_bias, gdn_norm_w=m_gdn_norm_w,
                 q_norm_w=m_q_norm_w, w_uq=m_w_uq, kv_norm_w=m_kv_norm_w, w_ukv=m_w_ukv, mla_out_norm_w=m_mla_out_norm_w,
                 w_out=m_w_out, ffn_norm_w=m_ffn_norm_w, w_gate=m_w_gate, w_up=m_w_up, w_down=m_w_down, final_norm_w=m_final_norm_w)
    mom_v = dict(attn_norm_w=v_attn_norm_w, w_in=v_w_in, conv_w=v_conv_w, a_log=v_a_log, dt_bias=v_dt_bias, gdn_norm_w=v_gdn_norm_w,
                 q_norm_w=v_q_norm_w, w_uq=v_w_uq, kv_norm_w=v_kv_norm_w, w_ukv=v_w_ukv, mla_out_norm_w=v_mla_out_norm_w,
                 w_out=v_w_out, ffn_norm_w=v_ffn_norm_w, w_gate=v_w_gate, w_up=v_w_up, w_down=v_w_down, final_norm_w=v_final_norm_w)
    big_names = ("w_in", "w_uq", "w_ukv", "w_out", "w_gate", "w_up", "w_down")

    shard = {n: weights[n][0].astype(BF16) for n in big_names}
    g_in, g_uq, g_ukv, g_conv = _exchange(_Gather([shard["w_in"], shard["w_uq"], shard["w_ukv"], weights["conv_w"][0]]),
                                          name="gather_weights")
    win_p = _win_to_padded(_unshard_cols(g_in))
    wuq_p = _wuq_to_padded(_unshard_cols(g_uq))
    wukv = _unshard_cols(g_ukv)
    late = [shard["w_out"], shard["w_gate"], shard["w_up"], shard["w_down"]]
    conv_full = jnp.concatenate([_unshard_cols(g_conv), jnp.zeros((8 - GDN_CONV, CONV_CH), F32)], axis=0)

    gdn_params = jnp.concatenate([_pad_lanes(a_log, HEAD)[None], _pad_lanes(dt_bias, HEAD)[None], jnp.zeros((6, HEAD), F32)], axis=0)
    small = {n: weights[n].reshape(1, -1) for n in ("attn_norm_w", "ffn_norm_w", "final_norm_w", "q_norm_w", "kv_norm_w",
                                                    "gdn_norm_w", "mla_out_norm_w")}
    small["gdn_params"] = gdn_params

    dx, big, sm = _local_step(x[0], positions.reshape(t, 1).astype(F32), loss_target[0], win_p, wuq_p, wukv, late,
                              conv_full, small, True)

    rows8 = lambda name: jnp.sum(sm[name], axis=0)
    pieces = [rows8(n) for n, _ in _SMALL[:7]]
    pieces += [_pad_lanes(jnp.sum(sm["gdn_params"][0:1], axis=0), HEAD), _pad_lanes(jnp.sum(sm["gdn_params"][1:2], axis=0), HEAD)]
    pieces.append(jnp.sum(sm["conv_w"], axis=1).reshape(-1))
    pieces.append(_pad_lanes(jnp.sum(sm["loss"]).reshape(1), HEAD))
    packed = _pad_lanes(jnp.concatenate(pieces), _PACK_ROWS * HEAD).reshape(_PACK_ROWS, HEAD)
    (r_small,) = _exchange(_Exchange([packed], [True]), name="exchange_small")

    outs_g, outs_d, outs_m, outs_v = {}, {}, {}, {}
    for name in big_names:
        g, d, nm, nv = _adamw(big[name], weights[name][0], mom_m[name][0], mom_v[name][0], name="adamw_" + name)
        outs_g[name], outs_d[name], outs_m[name], outs_v[name] = g[None], d[None], nm[None], nv[None]

    total = _sum_parts(r_small, name="sum_small")
    flat = total.reshape(-1)
    loss = flat[(_SMALL_ROWS + _CONV_ROWS) * HEAD]
    g_small, off = {}, 0
    for n, size in _SMALL:
        g_small[n] = flat[off:off + size]
        off += size
    g_conv_full = flat[off:off + GDN_CONV * CONV_CH].reshape(GDN_CONV, CONV_CH)
    g_small["conv_w"] = lax.dynamic_slice(g_conv_full, (0, me * (CONV_CH // N_DEV)), (GDN_CONV, CONV_CH // N_DEV)).reshape(-1)
    order = [n for n, _ in _SMALL] + ["conv_w"]
    sizes = dict(_SMALL)
    sizes["conv_w"] = GDN_CONV * CONV_CH // N_DEV
    true_size = {n: weights[n].size for n in order}

    def pack(d):
        return jnp.concatenate([_pad_lanes(d[n], sizes[n]) for n in order]).reshape(1, -1, HEAD)

    g2, d2, m2, v2 = _adamw(pack(g_small), pack(weights)[0], pack(mom_m)[0], pack(mom_v)[0], name="adamw_small")
    off = 0
    for n in order:
        for src, dst in ((g2, outs_g), (d2, outs_d), (m2, outs_m), (v2, outs_v)):
            dst[n] = src.reshape(-1)[off:off + true_size[n]].reshape(weights[n].shape)
        off += sizes[n]

    names = ("attn_norm_w", "w_in", "conv_w", "a_log", "dt_bias", "gdn_norm_w", "q_norm_w", "w_uq", "kv_norm_w", "w_ukv",
             "mla_out_norm_w", "w_out", "ffn_norm_w", "w_gate", "w_up", "w_down", "final_norm_w")
    return (loss, dx[None], *[outs_g[n] for n in names], *[outs_d[n] for n in names], *[outs_m[n] for n in names],
            *[outs_v[n] for n in names])
```

```python
import functools
import math

import jax
import jax.numpy as jnp
from jax import lax
from jax.experimental import pallas as pl
from jax.experimental.pallas import tpu as pltpu

F32 = jnp.float32
BF16 = jnp.bfloat16

D_MODEL = 2048
GDN_HEADS = 8
HEAD = 128
GDN_CONV = 4
GDN_CHUNK = 64
GDN_QK = GDN_HEADS * HEAD
CONV_CH = 3 * GDN_QK
MLA_HEADS = 8
QK_ROPE = 64
Q_LORA = 512
KV_LORA = 512
ROPE_THETA = 10000.0
D_FF = 5632
EPS = 1e-6
IN_WIDTH = 5200
ADAM_LR, ADAM_B1, ADAM_B2, ADAM_EPS, ADAM_WD, ADAM_STEP = 0.001, 0.9, 0.999, 1e-08, 0.01, 10

PROJ_W = 5376
COL_Z = 3072
COL_CQ = 4096
COL_CKV = 4608
COL_MISC = 5120
LANE_B = 64
LANE_A = 72
QHEAD = 256
FF_WIDE = D_FF // 4
N_DEV = 8
MESH = pl.DeviceIdType.MESH
VMEM_LIMIT_MB = 48

NN = ((1,), (0,))
NT = ((1,), (1,))
TN = ((0,), (0,))


def _my_place():
    x, y, c = lax.axis_index("x"), lax.axis_index("y"), lax.axis_index("c")
    return x, y, c, 4 * x + 2 * y + c


def _peer(x, y, c, p):
    px, py, pc = x ^ ((p >> 2) & 1), y ^ ((p >> 1) & 1), c ^ (p & 1)
    return (px, py, pc), 4 * px + 2 * py + pc


class _Exchange:
    def __init__(self, arrays, gather):
        self.arrays, self.gather, self.n = list(arrays), list(gather), len(arrays)

    def out_shape(self):
        return [jax.ShapeDtypeStruct(((N_DEV,) + a.shape) if g else a.shape, a.dtype)
                for a, g in zip(self.arrays, self.gather)]

    def sems(self):
        return [pltpu.SemaphoreType.DMA((self.n * (N_DEV - 1),)), pltpu.SemaphoreType.DMA((self.n * (N_DEV - 1),)),
                pltpu.SemaphoreType.DMA((self.n,))]

    def _copies(self, ins, outs, sems):
        send_sems, recv_sems, local_sems = sems
        x, y, c, me = _my_place()
        local = [pltpu.make_async_copy(ins[k] if self.gather[k] else ins[k].at[me], outs[k].at[me], local_sems.at[k])
                 for k in range(self.n)]
        sent, received = [], []
        for p in range(1, N_DEV):
            place, num = _peer(x, y, c, p)
            for k in range(self.n):
                src = ins[k] if self.gather[k] else ins[k].at[num]
                idx = k * (N_DEV - 1) + p - 1
                mk = lambda dst: pltpu.make_async_remote_copy(src_ref=src, dst_ref=dst, send_sem=send_sems.at[idx],
                                                              recv_sem=recv_sems.at[idx], device_id=place, device_id_type=MESH)
                sent.append(mk(outs[k].at[me]))
                received.append(mk(outs[k].at[num]))
        return local, sent, received

    def start(self, ins, outs, sems):
        local, sent, _ = self._copies(ins, outs, sems)
        for cp in local + sent:
            cp.start()

    def forward(self, ins, outs, sems):
        pass

    def finish(self, ins, outs, sems):
        local, sent, received = self._copies(ins, outs, sems)
        for cp in received:
            cp.wait_recv()
        for cp in sent:
            cp.wait_send()
        for cp in local:
            cp.wait()


class _Gather:
    def __init__(self, arrays):
        self.arrays, self.n = list(arrays), len(arrays)

    def out_shape(self):
        return [jax.ShapeDtypeStruct((N_DEV,) + a.shape, a.dtype) for a in self.arrays]

    def sems(self):
        return [pltpu.SemaphoreType.DMA((self.n * (N_DEV - 1),)), pltpu.SemaphoreType.DMA((self.n * (N_DEV - 1),)),
                pltpu.SemaphoreType.DMA((self.n,))]

    def _plan(self, ins, outs, sems):
        send_sems, recv_sems, local_sems = sems
        x, y, c, me = _my_place()
        sibling = (x, y, 1 - c)
        chips = [(1 - x, y), (x, 1 - y), (1 - x, 1 - y)]
        num = lambda px, py, pc: 4 * px + 2 * py + pc

        def copy(k, i, block, to, src=None):
            slot = outs[k].at[num(*block)]
            return pltpu.make_async_remote_copy(src_ref=slot if src is None else src, dst_ref=slot,
                                                send_sem=send_sems.at[k * (N_DEV - 1) + i],
                                                recv_sem=recv_sems.at[k * (N_DEV - 1) + i],
                                                device_id=to, device_id_type=MESH)

        local = [pltpu.make_async_copy(ins[k], outs[k].at[me], local_sems.at[k]) for k in range(self.n)]
        return (x, y, c), sibling, chips, copy, local

    def start(self, ins, outs, sems):
        me, sibling, chips, copy, local = self._plan(ins, outs, sems)
        for cp in local:
            cp.start()
        for k in range(self.n):
            copy(k, 0, me, sibling, src=ins[k]).start()
            for j, chip in enumerate(chips):
                copy(k, 1 + j, me, (*chip, me[2]), src=ins[k]).start()

    def forward(self, ins, outs, sems):
        me, sibling, chips, copy, _ = self._plan(ins, outs, sems)
        for j, chip in enumerate(chips):
            for k in range(self.n):
                copy(k, 1 + j, (*chip, me[2]), me).wait_recv()
                copy(k, 4 + j, (*chip, me[2]), sibling).start()

    def finish(self, ins, outs, sems):
        me, sibling, chips, copy, local = self._plan(ins, outs, sems)
        for k in range(self.n):
            copy(k, 0, sibling, me).wait_recv()
            for j, chip in enumerate(chips):
                copy(k, 4 + j, (*chip, 1 - me[2]), me).wait_recv()
        for k in range(self.n):
            copy(k, 0, me, sibling, src=ins[k]).wait_send()
            for j, chip in enumerate(chips):
                copy(k, 1 + j, me, (*chip, me[2]), src=ins[k]).wait_send()
                copy(k, 4 + j, (*chip, me[2]), sibling).wait_send()
        for cp in local:
            cp.wait()


def _pcall(body, *, name, grid, in_specs, out_specs, out_shape, scratch=(), carry=None):
    params = pltpu.CompilerParams(dimension_semantics=("arbitrary",) * len(grid), vmem_limit_bytes=VMEM_LIMIT_MB << 20)
    if carry is None:
        return pl.pallas_call(body, name=name, grid=grid, in_specs=in_specs, out_specs=out_specs, out_shape=out_shape,
                              scratch_shapes=list(scratch), compiler_params=params)
    single = not isinstance(out_specs, (list, tuple))
    out_specs = [out_specs] if single else list(out_specs)
    out_shape = [out_shape] if single else list(out_shape)
    n_in, n_out, n_scr, na = len(in_specs), len(out_specs), len(scratch), carry.n

    def wrapped(*refs):
        ins, cin = refs[:n_in], refs[n_in:n_in + na]
        outs, cout = refs[n_in + na:n_in + na + n_out], refs[n_in + na + n_out:n_in + 2 * na + n_out]
        scr, sems = refs[n_in + 2 * na + n_out:n_in + 2 * na + n_out + n_scr], refs[n_in + 2 * na + n_out + n_scr:]
        total = math.prod(grid)
        step = functools.reduce(lambda a, d: a * grid[d] + pl.program_id(d), range(len(grid)), 0)

        @pl.when(step == 0)
        def _():
            carry.start(cin, cout, sems)

        body(*ins, *outs, *scr)

        @pl.when(step == min(total * 5 // 8, total - 1))
        def _():
            carry.forward(cin, cout, sems)

        @pl.when(step == total - 1)
        def _():
            carry.finish(cin, cout, sems)

    any_spec = pl.BlockSpec(memory_space=pl.ANY)
    call = pl.pallas_call(wrapped, name=name, grid=grid, in_specs=list(in_specs) + [any_spec] * na,
                          out_specs=out_specs + [any_spec] * na, out_shape=out_shape + carry.out_shape(),
                          scratch_shapes=list(scratch) + carry.sems(), compiler_params=params)

    def run(*args):
        res = call(*args, *carry.arrays)
        main = res[0] if single else list(res[:n_out])
        return main, list(res[n_out:])

    return run


def _pick(dim, pref):
    if dim <= pref:
        return dim
    c = pref
    while c >= 128:
        if dim % c == 0 and c % 128 == 0:
            return c
        c -= 128
    return dim


def _rows(t, width, target_bytes=2 << 20):
    r = max(8, min(t, target_bytes // (4 * width)))
    r = 1 << (r.bit_length() - 1)
    while t % r:
        r //= 2
    return r


MM_FULL_K = 2048


def _mm(a, b, *, name, ta=False, tb=False, res=None, out_dtype=F32, bm=1024, bn=1024, bk=1024, carry=None):
    m, k = (a.shape[1], a.shape[0]) if ta else a.shape
    n = b.shape[0] if tb else b.shape[1]
    assert (b.shape[1] if tb else b.shape[0]) == k
    bm, bn, bk = _pick(m, bm), _pick(n, bn), (k if k <= MM_FULL_K else _pick(k, bk))
    nk = k // bk
    dims = (((0,) if ta else (1,), (1,) if tb else (0,)), ((), ()))

    def body(*refs):
        a_ref, b_ref = refs[:2]
        r_ref = refs[2] if res is not None else None
        o_ref = refs[3] if res is not None else refs[2]
        part = lax.dot_general(a_ref[...].astype(BF16), b_ref[...].astype(BF16), dims, preferred_element_type=F32)

        def finish(out):
            if res is not None:
                out = out + r_ref[...]
            o_ref[...] = out.astype(o_ref.dtype)

        if nk == 1:
            finish(part)
            return
        acc_ref = refs[-1]
        kk = pl.program_id(2)

        @pl.when(kk == 0)
        def _():
            acc_ref[...] = part

        @pl.when((kk > 0) & (kk < nk - 1))
        def _():
            acc_ref[...] += part

        @pl.when(kk == nk - 1)
        def _():
            finish(acc_ref[...] + part)

    a_spec = pl.BlockSpec((bk, bm), lambda i, j, kk: (kk, i)) if ta else pl.BlockSpec((bm, bk), lambda i, j, kk: (i, kk))
    b_spec = pl.BlockSpec((bn, bk), lambda i, j, kk: (j, kk)) if tb else pl.BlockSpec((bk, bn), lambda i, j, kk: (kk, j))
    o_spec = pl.BlockSpec((bm, bn), lambda i, j, kk: (i, j))
    ins, specs = [a, b], [a_spec, b_spec]
    if res is not None:
        ins.append(res)
        specs.append(o_spec)
    return _pcall(body, name=name, grid=(m // bm, n // bn, nk), in_specs=specs, out_specs=o_spec,
                  out_shape=jax.ShapeDtypeStruct((m, n), out_dtype),
                  scratch=[pltpu.VMEM((bm, bn), F32)] if nk > 1 else [], carry=carry)(*ins)


def _rms_fwd(x, w, *, name, width, heads=1, col0=0, out_dtype=BF16):
    t = x.shape[0]
    tm = _rows(t, width)
    cb = col0 // width

    def body(x_ref, w_ref, o_ref):
        xv = x_ref[...]
        r = lax.rsqrt(jnp.mean(xv * xv, axis=-1, keepdims=True) + EPS)
        o_ref[...] = (xv * r * w_ref[...]).astype(o_ref.dtype)

    return _pcall(body, name=name, grid=(t // tm, heads),
                  in_specs=[pl.BlockSpec((tm, width), lambda i, h: (i, cb + h)),
                            pl.BlockSpec((1, width), lambda i, h: (0, 0))],
                  out_specs=pl.BlockSpec((tm, width), lambda i, h: (i, h)),
                  out_shape=jax.ShapeDtypeStruct((t, heads * width), out_dtype))(x, w)


def _rms_bwd(x, w, dy, *, name, width, heads=1, col0=0, dcol0=0, res=None, out_dtype=F32, with_delta=False):
    t = x.shape[0]
    tm = _rows(t, width)
    cb, dcb = col0 // width, dcol0 // width

    def body(*refs):
        refs = list(refs)
        x_ref, w_ref, dy_ref = refs[:3]
        r_ref = refs[3] if res is not None else None
        outs = refs[4:] if res is not None else refs[3:]
        dx_ref, dw_ref = outs[:2]
        xv = x_ref[...]
        dyv = dy_ref[...].astype(F32)
        r = lax.rsqrt(jnp.mean(xv * xv, axis=-1, keepdims=True) + EPS)
        xh = xv * r
        dyw = dyv * w_ref[...]
        dx = r * (dyw - xh * jnp.mean(dyw * xh, axis=-1, keepdims=True))
        if with_delta:
            outs[2][...] = jnp.broadcast_to(jnp.sum(dx * xv, axis=-1, keepdims=True), dx.shape)
        if res is not None:
            dx = dx + r_ref[...]
        dx_ref[...] = dx.astype(dx_ref.dtype)

        @pl.when((pl.program_id(0) == 0) & (pl.program_id(1) == 0))
        def _():
            dw_ref[...] = jnp.zeros_like(dw_ref)

        dw_ref[...] += (dyv * xh).reshape(tm // 8, 8, width).sum(axis=0)

    blk = pl.BlockSpec((tm, width), lambda i, h: (i, h))
    ins = [x, w, dy]
    specs = [pl.BlockSpec((tm, width), lambda i, h: (i, cb + h)), pl.BlockSpec((1, width), lambda i, h: (0, 0)),
             pl.BlockSpec((tm, width), lambda i, h: (i, dcb + h))]
    if res is not None:
        ins.append(res)
        specs.append(blk)
    out_shape = [jax.ShapeDtypeStruct((t, heads * width), out_dtype), jax.ShapeDtypeStruct((8, width), F32)]
    out_specs = [blk, pl.BlockSpec((8, width), lambda i, h: (0, 0))]
    if with_delta:
        out_shape.append(jax.ShapeDtypeStruct((t, heads * width), F32))
        out_specs.append(blk)
    return _pcall(body, name=name, grid=(t // tm, heads), in_specs=specs, out_specs=out_specs, out_shape=out_shape)(*ins)


def _sig(x):
    return 1.0 / (1.0 + jnp.exp(-x))


@jax.custom_vjp
def _sigmoid(x):
    return _sig(x)


def _sigmoid_fwd(x):
    s = _sig(x)
    return s, s


def _sigmoid_bwd(s, g):
    return (g * s * (1.0 - s),)


_sigmoid.defvjp(_sigmoid_fwd, _sigmoid_bwd)


@jax.custom_vjp
def _softplus(x):
    return jnp.maximum(x, 0.0) + jnp.log(1.0 + jnp.exp(-jnp.abs(x)))


def _softplus_fwd(x):
    return _softplus(x), x


def _softplus_bwd(x, g):
    return (g * _sig(x),)


_softplus.defvjp(_softplus_fwd, _softplus_bwd)


def _silu(x):
    return x * _sig(x)


def _dsilu(x):
    s = _sig(x)
    return s * (1.0 + x * (1.0 - s))


NN3 = (((2,), (1,)), ((0,), (0,)))
NT3 = (((2,), (2,)), ((0,), (0,)))
TN3 = (((1,), (1,)), ((0,), (0,)))


def _bdot(a, b, dims):
    return lax.dot_general(a.astype(BF16), b.astype(BF16), dims, preferred_element_type=F32)


def _bf16_part(x):
    bits = lax.bitcast_convert_type(x, jnp.uint32) & jnp.uint32(0xFFFF0000)
    return lax.bitcast_convert_type(bits, F32)


def _scan_rows(x, reverse):
    c = x.shape[1]
    row = lax.broadcasted_iota(jnp.int32, x.shape, 1)
    step = 1
    while step < c:
        if reverse:
            x = x + jnp.where(row < c - step, pltpu.roll(x, c - step, axis=1), 0.0)
        else:
            x = x + jnp.where(row >= step, pltpu.roll(x, step, axis=1), 0.0)
        step *= 2
    return x


@jax.custom_vjp
def _prefix_rows(x):
    return _scan_rows(x, False)


_prefix_rows.defvjp(lambda x: (_scan_rows(x, False), None), lambda _, g: (_scan_rows(g, True),))


def _dot3(a, b, dims):
    a_hi, b_hi = _bf16_part(a), _bf16_part(b)
    a_lo, b_lo = (a - a_hi).astype(BF16), (b - b_hi).astype(BF16)
    a_hi, b_hi = a_hi.astype(BF16), b_hi.astype(BF16)
    dot = lambda x, y: lax.dot_general(x, y, dims, preferred_element_type=F32)
    return dot(a_hi, b_hi) + (dot(a_hi, b_lo) + dot(a_lo, b_hi))


@jax.custom_vjp
def _nn_hi(a, b):
    return _dot3(a, b, NN3)


_nn_hi.defvjp(lambda a, b: (_dot3(a, b, NN3), (a, b)), lambda r, g: (_dot3(g, r[1], NT3), _dot3(r[0], g, TN3)))


@jax.custom_vjp
def _nn(a, b):
    return _bdot(a, b, NN3)


_nn.defvjp(lambda a, b: (_bdot(a, b, NN3), (a, b)), lambda r, g: (_bdot(g, r[1], NT3), _bdot(r[0], g, TN3)))


@jax.custom_vjp
def _nt(a, b):
    return _bdot(a, b, NT3)


_nt.defvjp(lambda a, b: (_bdot(a, b, NT3), (a, b)), lambda r, g: (_bdot(g, r[1], NN3), _bdot(g, r[0], TN3)))


@jax.custom_vjp
def _tn(a, b):
    return _bdot(a, b, TN3)


_tn.defvjp(lambda a, b: (_bdot(a, b, TN3), (a, b)), lambda r, g: (_bdot(r[1], g, NT3), _bdot(r[0], g, NN3)))


def _conv_pre(ext, w, rows):
    acc = w[0:1] * ext[5:5 + rows]
    for j in range(1, GDN_CONV):
        acc = acc + w[j:j + 1] * ext[5 + j:5 + j + rows]
    return acc


def _conv_fwd(proj, conv_w, *, name):
    t = proj.shape[0]
    tm, tc = _pick(t, 512), 512
    nb = tm // 8

    def body(u_ref, p_ref, w_ref, o_ref):
        i = pl.program_id(1)
        prev = jnp.where(i > 0, p_ref[...], 0.0)
        ext = jnp.concatenate([prev, u_ref[...]], axis=0)
        o_ref[...] = _silu(_conv_pre(ext, w_ref[...], tm))

    return _pcall(body, name=name, grid=(CONV_CH // tc, t // tm),
                  in_specs=[pl.BlockSpec((tm, tc), lambda j, i: (i, j)),
                            pl.BlockSpec((8, tc), lambda j, i: (jnp.maximum(i * nb - 1, 0), j)),
                            pl.BlockSpec((8, tc), lambda j, i: (0, j))],
                  out_specs=pl.BlockSpec((tm, tc), lambda j, i: (i, j)),
                  out_shape=jax.ShapeDtypeStruct((t, CONV_CH), F32))(proj, proj, conv_w)


def _conv_bwd(proj, conv_w, dy, *, name):
    t = proj.shape[0]
    tm, tc = _pick(t, 512), 512
    nb = tm // 8
    last = t // tm - 1

    def body(u_ref, p_ref, n_ref, dy_ref, dyn_ref, w_ref, du_ref, dw_ref):
        i = pl.program_id(1)
        w = w_ref[...]
        prev = jnp.where(i > 0, p_ref[...], 0.0)
        ext = jnp.concatenate([prev, u_ref[...], n_ref[...]], axis=0)
        c = _conv_pre(ext, w, tm + 8)
        dy_ext = jnp.concatenate([dy_ref[...], jnp.where(i < last, dyn_ref[...], 0.0)], axis=0)
        dc = dy_ext * _dsilu(c)
        du = w[3:4] * dc[0:tm]
        for j in range(GDN_CONV - 1):
            du = du + w[j:j + 1] * dc[3 - j:3 - j + tm]
        du_ref[...] = du.astype(du_ref.dtype)

        @pl.when(i == 0)
        def _():
            dw_ref[...] = jnp.zeros_like(dw_ref)

        for j in range(GDN_CONV):
            dw_ref[j] += (dc[0:tm] * ext[5 + j:5 + j + tm]).reshape(nb, 8, tc).sum(axis=0)

    cur = lambda j, i: (i, j)
    return _pcall(body, name=name, grid=(CONV_CH // tc, t // tm),
                  in_specs=[pl.BlockSpec((tm, tc), cur),
                            pl.BlockSpec((8, tc), lambda j, i: (jnp.maximum(i * nb - 1, 0), j)),
                            pl.BlockSpec((8, tc), lambda j, i: (jnp.minimum((i + 1) * nb, t // 8 - 1), j)),
                            pl.BlockSpec((tm, tc), cur),
                            pl.BlockSpec((8, tc), lambda j, i: (jnp.minimum((i + 1) * nb, t // 8 - 1), j)),
                            pl.BlockSpec((8, tc), lambda j, i: (0, j))],
                  out_specs=[pl.BlockSpec((tm, tc), cur), pl.BlockSpec((GDN_CONV, 8, tc), lambda j, i: (0, 0, j))],
                  out_shape=[jax.ShapeDtypeStruct((t, CONV_CH), BF16), jax.ShapeDtypeStruct((GDN_CONV, 8, CONV_CH), F32)],
                  )(proj, proj, proj, dy, dy, conv_w)


def _gdn_chunk(q_raw, k_raw, v, misc, params, state):
    nh, c = q_raw.shape[0], q_raw.shape[1]
    lane = lax.broadcasted_iota(jnp.int32, misc.shape, 1)
    prow = lax.broadcasted_iota(jnp.int32, params.shape, 0)
    plane = lax.broadcasted_iota(jnp.int32, params.shape, 1)
    heads = lambda pieces: jnp.concatenate([p[None] for p in pieces], axis=0)
    col = lambda at: heads([jnp.sum(jnp.where(lane == at + h, misc, 0.0), axis=1, keepdims=True) for h in range(nh)])
    par = lambda row: heads([jnp.sum(jnp.where((prow == row) & (plane == h), params, 0.0), keepdims=True)
                             for h in range(nh)])
    b_raw, a_raw = col(LANE_B), col(LANE_A)
    a_log, dt_bias = par(0), par(1)
    beta = _sigmoid(b_raw)
    g = -jnp.exp(a_log) * _softplus(a_raw + dt_bias)

    q = q_raw * lax.rsqrt(jnp.sum(q_raw * q_raw, axis=-1, keepdims=True) + EPS) * (HEAD ** -0.5)
    k = k_raw * lax.rsqrt(jnp.sum(k_raw * k_raw, axis=-1, keepdims=True) + EPS)

    ri = lax.broadcasted_iota(jnp.int32, (c, c), 0)
    ci = lax.broadcasted_iota(jnp.int32, (c, c), 1)
    tril, strict = ri >= ci, ri > ci
    gc = _prefix_rows(g)
    gc_col = jnp.broadcast_to(gc, (nh, c, c))
    gc_row = jnp.swapaxes(gc_col, 1, 2)
    decay = jnp.exp(jnp.where(tril, gc_col - gc_row, -1e30))

    kb = k * beta
    vb = v * beta
    a_mat = jnp.where(strict, _nt(kb, k) * decay, 0.0)
    x = -a_mat
    inv = (ri == ci).astype(F32) + x
    for _ in range(5):
        x = _nn_hi(x, x)
        inv = inv + _nn_hi(inv, x)
    u = _nn_hi(inv, vb)
    w = _nn_hi(inv, kb * jnp.exp(gc))
    intra = _nt(q, k) * decay

    v_new = u - _nn(w, state)
    o = _nn(q * jnp.exp(gc), state) + _nn(intra, v_new)
    g_last = jnp.sum(g, axis=1, keepdims=True)
    k_dec = k * jnp.exp(g_last - gc)
    new_state = state * jnp.exp(g_last) + _tn(k_dec, v_new)
    return o, new_state


def _gdn_specs(nc, rev):
    cidx = (lambda n: nc - 1 - n) if rev else (lambda n: n)
    hb = lambda part: pl.BlockSpec((GDN_CHUNK, GDN_QK), lambda n: (cidx(n), part))
    misc = pl.BlockSpec((GDN_CHUNK, HEAD), lambda n: (cidx(n), COL_MISC // HEAD))
    params = pl.BlockSpec((8, HEAD), lambda n: (0, 0))
    hist = pl.BlockSpec((1, GDN_HEADS, HEAD, HEAD), lambda n: (cidx(n), 0, 0, 0))
    return hb, misc, params, hist


def _split_heads(v):
    return jnp.stack([v[:, h * HEAD:(h + 1) * HEAD] for h in range(v.shape[1] // HEAD)])


def _merge_heads(v):
    return jnp.concatenate([v[h] for h in range(v.shape[0])], axis=1)


def _gdn_fwd(qkv, proj, params, *, name):
    t = qkv.shape[0]
    nc = t // GDN_CHUNK
    hb, misc, pspec, hist = _gdn_specs(nc, False)

    def body(q_ref, k_ref, v_ref, m_ref, p_ref, o_ref, hist_ref, s_ref):
        @pl.when(pl.program_id(0) == 0)
        def _():
            s_ref[...] = jnp.zeros_like(s_ref)

        state = s_ref[...]
        hist_ref[0] = state
        o, new_state = _gdn_chunk(_split_heads(q_ref[...]), _split_heads(k_ref[...]), _split_heads(v_ref[...]),
                                  m_ref[...], p_ref[...], state)
        o_ref[...] = _merge_heads(o)
        s_ref[...] = new_state

    return _pcall(body, name=name, grid=(nc,),
                  in_specs=[hb(0), hb(1), hb(2), misc, pspec],
                  out_specs=[hb(0), hist],
                  out_shape=[jax.ShapeDtypeStruct((t, GDN_QK), F32),
                             jax.ShapeDtypeStruct((nc, GDN_HEADS, HEAD, HEAD), F32)],
                  scratch=[pltpu.VMEM((GDN_HEADS, HEAD, HEAD), F32)])(qkv, qkv, qkv, proj, params)


def _gdn_bwd(qkv, proj, params, hist_arr, do, dmisc_in, *, name, carry=None):
    t = qkv.shape[0]
    nc = t // GDN_CHUNK
    hb, misc, pspec, hist = _gdn_specs(nc, True)
    mrow = pl.BlockSpec((GDN_CHUNK, HEAD), lambda n: (nc - 1 - n, 0))

    def body(q_ref, k_ref, v_ref, m_ref, p_ref, hist_ref, do_ref, dmi_ref, dqkv_ref, dm_ref, dp_ref, ds_ref):
        @pl.when(pl.program_id(0) == 0)
        def _():
            ds_ref[...] = jnp.zeros_like(ds_ref)
            dp_ref[...] = jnp.zeros_like(dp_ref)

        _, vjp = jax.vjp(_gdn_chunk, _split_heads(q_ref[...]), _split_heads(k_ref[...]), _split_heads(v_ref[...]),
                         m_ref[...], p_ref[...], hist_ref[0])
        dq, dk, dv, dm, dp, ds = vjp((_split_heads(do_ref[...]), ds_ref[...]))
        dqkv_ref[:, 0:GDN_QK] = _merge_heads(dq)
        dqkv_ref[:, GDN_QK:2 * GDN_QK] = _merge_heads(dk)
        dqkv_ref[:, 2 * GDN_QK:] = _merge_heads(dv)
        ds_ref[...] = ds
        dm_ref[...] = dmi_ref[...] + dm
        dp_ref[...] += dp

    return _pcall(body, name=name, grid=(nc,),
                  in_specs=[hb(0), hb(1), hb(2), misc, pspec, hist, hb(0), mrow],
                  out_specs=[pl.BlockSpec((GDN_CHUNK, CONV_CH), lambda n: (nc - 1 - n, 0)), mrow, pspec],
                  out_shape=[jax.ShapeDtypeStruct((t, CONV_CH), F32), jax.ShapeDtypeStruct((t, HEAD), F32),
                             jax.ShapeDtypeStruct((8, HEAD), F32)],
                  scratch=[pltpu.VMEM((GDN_HEADS, HEAD, HEAD), F32)], carry=carry,
                  )(qkv, qkv, qkv, proj, params, hist_arr, do, dmisc_in)


def _gate_fwd(o_raw, proj, w, *, name):
    t = o_raw.shape[0]
    tm = _rows(t, HEAD)
    zb = COL_Z // HEAD

    def body(o_ref, z_ref, w_ref, out_ref):
        ov = o_ref[...]
        r = lax.rsqrt(jnp.mean(ov * ov, axis=-1, keepdims=True) + EPS)
        out_ref[...] = (ov * r * w_ref[...] * _silu(z_ref[...])).astype(out_ref.dtype)

    blk = pl.BlockSpec((tm, HEAD), lambda i, h: (i, h))
    return _pcall(body, name=name, grid=(t // tm, GDN_HEADS),
                  in_specs=[blk, pl.BlockSpec((tm, HEAD), lambda i, h: (i, zb + h)), pl.BlockSpec((1, HEAD), lambda i, h: (0, 0))],
                  out_specs=blk, out_shape=jax.ShapeDtypeStruct((t, GDN_QK), BF16))(o_raw, proj, w)


def _gate_bwd(o_raw, proj, w, dmixed, *, name):
    t = o_raw.shape[0]
    tm = _rows(t, HEAD)
    zb = COL_Z // HEAD

    def body(o_ref, z_ref, w_ref, dy_ref, do_ref, dz_ref, dw_ref):
        ov, zv, dyv = o_ref[...], z_ref[...], dy_ref[...]
        r = lax.rsqrt(jnp.mean(ov * ov, axis=-1, keepdims=True) + EPS)
        xh = ov * r
        dn = dyv * _silu(zv)
        dz_ref[...] = (dyv * xh * w_ref[...] * _dsilu(zv)).astype(dz_ref.dtype)
        dnw = dn * w_ref[...]
        do_ref[...] = r * (dnw - xh * jnp.mean(dnw * xh, axis=-1, keepdims=True))

        @pl.when((pl.program_id(0) == 0) & (pl.program_id(1) == 0))
        def _():
            dw_ref[...] = jnp.zeros_like(dw_ref)

        dw_ref[...] += (dn * xh).reshape(tm // 8, 8, HEAD).sum(axis=0)

    blk = pl.BlockSpec((tm, HEAD), lambda i, h: (i, h))
    return _pcall(body, name=name, grid=(t // tm, GDN_HEADS),
                  in_specs=[blk, pl.BlockSpec((tm, HEAD), lambda i, h: (i, zb + h)), pl.BlockSpec((1, HEAD), lambda i, h: (0, 0)), blk],
                  out_specs=[blk, blk, pl.BlockSpec((8, HEAD), lambda i, h: (0, 0))],
                  out_shape=[jax.ShapeDtypeStruct((t, GDN_QK), F32), jax.ShapeDtypeStruct((t, GDN_QK), BF16),
                             jax.ShapeDtypeStruct((8, HEAD), F32)])(o_raw, proj, w, dmixed)


def _rope_tables():
    half = QK_ROPE // 2
    inv = ROPE_THETA ** (-jnp.arange(half, dtype=F32) / half)
    zeros = jnp.zeros((HEAD - QK_ROPE,), F32)
    inv_row = jnp.concatenate([inv, inv, zeros])
    sign_row = jnp.concatenate([-jnp.ones((half,), F32), jnp.ones((half,), F32), zeros])
    mask_row = jnp.concatenate([jnp.ones((QK_ROPE,), F32), zeros])
    return jnp.concatenate([inv_row[None], sign_row[None], mask_row[None], jnp.zeros((5, HEAD), F32)], axis=0)


def _rope_cs(pos, tab, *, name):
    t = pos.shape[0]
    tm = _pick(t, 1024)

    def body(pos_ref, tab_ref, o_ref):
        tab = tab_ref[...]
        ang = pos_ref[...] * tab[0:1]
        o_ref[...] = jnp.concatenate([jnp.cos(ang) * tab[2:3], jnp.sin(ang) * tab[1:2]], axis=1)

    return _pcall(body, name=name, grid=(t // tm,),
                  in_specs=[pl.BlockSpec((tm, 1), lambda i: (i, 0)), pl.BlockSpec((8, HEAD), lambda i: (0, 0))],
                  out_specs=pl.BlockSpec((tm, 2 * HEAD), lambda i: (i, 0)),
                  out_shape=jax.ShapeDtypeStruct((t, 2 * HEAD), F32))(pos, tab)


def _rotate(x, cs, sign):
    lane = lax.broadcasted_iota(jnp.int32, x.shape, 1)
    half = QK_ROPE // 2
    partner = jnp.where(lane < half, pltpu.roll(x, HEAD - half, axis=1), pltpu.roll(x, half, axis=1))
    return x * cs[:, :HEAD] + partner * (cs[:, HEAD:] * sign)


def _q_rot(q, cs, *, name, sign, out_dtype=BF16):
    t = q.shape[0]
    tm = _pick(t, 1024)
    scale = (HEAD + QK_ROPE) ** -0.5

    def body(q_ref, cs_ref, o_ref):
        qv = q_ref[...].astype(F32)
        rot = _rotate(qv[:, HEAD:], cs_ref[...], sign)
        o_ref[...] = (jnp.concatenate([qv[:, :HEAD], rot], axis=1) * scale).astype(o_ref.dtype)

    blk = pl.BlockSpec((tm, QHEAD), lambda i, h: (i, h))
    return _pcall(body, name=name, grid=(t // tm, MLA_HEADS),
                  in_specs=[blk, pl.BlockSpec((tm, 2 * HEAD), lambda i, h: (i, 0))],
                  out_specs=blk, out_shape=jax.ShapeDtypeStruct((t, MLA_HEADS * QHEAD), out_dtype))(q, cs)


def _q_up(cqn, wuq_p, cs, *, name):
    t, lora = cqn.shape
    tm = _pick(t, 1024)
    scale = (HEAD + QK_ROPE) ** -0.5

    def body(a_ref, w_ref, cs_ref, o_ref):
        qv = lax.dot_general(a_ref[...], w_ref[...], (NN, ((), ())), preferred_element_type=F32)
        rot = _rotate(qv[:, HEAD:], cs_ref[...], 1.0)
        o_ref[...] = (jnp.concatenate([qv[:, :HEAD], rot], axis=1) * scale).astype(o_ref.dtype)

    return _pcall(body, name=name, grid=(t // tm, MLA_HEADS),
                  in_specs=[pl.BlockSpec((tm, lora), lambda i, h: (i, 0)), pl.BlockSpec((lora, QHEAD), lambda i, h: (0, h)),
                            pl.BlockSpec((tm, 2 * HEAD), lambda i, h: (i, 0))],
                  out_specs=pl.BlockSpec((tm, QHEAD), lambda i, h: (i, h)),
                  out_shape=jax.ShapeDtypeStruct((t, MLA_HEADS * QHEAD), BF16))(cqn, wuq_p, cs)


def _kv_up(ckvn, wukv, proj, cs, *, name):
    t, lora = ckvn.shape
    tm = _pick(t, 1024)

    def body(a_ref, w_ref, m_ref, cs_ref, k_ref, v_ref):
        kvv = lax.dot_general(a_ref[...], w_ref[...], (NN, ((), ())), preferred_element_type=F32)
        misc = m_ref[...]
        lane = lax.broadcasted_iota(jnp.int32, misc.shape, 1)
        rot = _rotate(jnp.where(lane < QK_ROPE, misc, 0.0), cs_ref[...], 1.0)
        k_ref[...] = jnp.concatenate([kvv[:, :HEAD], rot], axis=1).astype(k_ref.dtype)
        v_ref[...] = kvv[:, HEAD:].astype(v_ref.dtype)

    return _pcall(body, name=name, grid=(t // tm, MLA_HEADS),
                  in_specs=[pl.BlockSpec((tm, lora), lambda i, h: (i, 0)), pl.BlockSpec((lora, QHEAD), lambda i, h: (0, h)),
                            pl.BlockSpec((tm, HEAD), lambda i, h: (i, COL_MISC // HEAD)),
                            pl.BlockSpec((tm, 2 * HEAD), lambda i, h: (i, 0))],
                  out_specs=[pl.BlockSpec((tm, QHEAD), lambda i, h: (i, h)), pl.BlockSpec((tm, HEAD), lambda i, h: (i, h))],
                  out_shape=[jax.ShapeDtypeStruct((t, MLA_HEADS * QHEAD), BF16), jax.ShapeDtypeStruct((t, MLA_HEADS * HEAD), BF16)],
                  )(ckvn, wukv, proj, cs)


def _krope_bwd(dkr, cs, *, name):
    t = dkr.shape[0]
    tm = _pick(t, 512)

    def body(d_ref, cs_ref, o_ref):
        d = d_ref[...]
        acc = d[:, :HEAD]
        for h in range(1, MLA_HEADS):
            acc = acc + d[:, h * HEAD:(h + 1) * HEAD]
        o_ref[...] = _rotate(acc, cs_ref[...], -1.0)

    return _pcall(body, name=name, grid=(t // tm,),
                  in_specs=[pl.BlockSpec((tm, MLA_HEADS * HEAD), lambda i: (i, 0)), pl.BlockSpec((tm, 2 * HEAD), lambda i: (i, 0))],
                  out_specs=pl.BlockSpec((tm, HEAD), lambda i: (i, 0)),
                  out_shape=jax.ShapeDtypeStruct((t, HEAD), F32))(dkr, cs)


NEG = -1e30


def _tri(step, counts):
    starts = [sum(counts[:o]) for o in range(len(counts))]
    outer = sum([(step >= s).astype(jnp.int32) for s in starts[1:]], jnp.int32(0))
    start = sum([(step >= starts[o]).astype(jnp.int32) * (starts[o] - starts[o - 1]) for o in range(1, len(counts))], jnp.int32(0))
    return outer, step - start


def _attn_fwd(q, k, v, *, name, tq=1024, tk=1024, carry=None):
    t = q.shape[0]
    tq, tk = _pick(t, tq), _pick(t, tk)
    nq = t // tq
    last_kv = lambda i: (i * tq + tq - 1) // tk
    counts = [last_kv(i) + 1 for i in range(nq)]

    def body(q_ref, k_ref, v_ref, o_ref, lse_ref, m_ref, l_ref, acc_ref):
        i, j = _tri(pl.program_id(1), counts)

        @pl.when(j == 0)
        def _():
            m_ref[...] = jnp.full_like(m_ref, NEG)
            l_ref[...] = jnp.zeros_like(l_ref)
            acc_ref[...] = jnp.zeros_like(acc_ref)

        def step(masked):
            s = lax.dot_general(q_ref[...], k_ref[...], (NT, ((), ())), preferred_element_type=F32)
            if masked:
                qpos = i * tq + lax.broadcasted_iota(jnp.int32, s.shape, 0)
                kpos = j * tk + lax.broadcasted_iota(jnp.int32, s.shape, 1)
                s = jnp.where(kpos <= qpos, s, NEG)
            m_prev = m_ref[...]
            m_new = jnp.maximum(m_prev, jnp.max(s, axis=1, keepdims=True))
            alpha = jnp.exp(m_prev - m_new)
            p = jnp.exp(s - m_new)
            l_ref[...] = alpha * l_ref[...] + jnp.sum(p, axis=1, keepdims=True)
            acc_ref[...] = alpha * acc_ref[...] + lax.dot_general(p.astype(BF16), v_ref[...], (NN, ((), ())),
                                                                  preferred_element_type=F32)
            m_ref[...] = m_new

        crosses = j * tk + tk - 1 > i * tq

        @pl.when(crosses)
        def _():
            step(True)

        @pl.when(jnp.logical_not(crosses))
        def _():
            step(False)

        @pl.when(j == last_kv(i))
        def _():
            o_ref[...] = acc_ref[...] / l_ref[...]
            lse_ref[...] = jnp.broadcast_to(m_ref[...] + jnp.log(l_ref[...]), lse_ref.shape)

    qblk = pl.BlockSpec((tq, QHEAD), lambda h, s: (_tri(s, counts)[0], h))
    oblk = pl.BlockSpec((tq, HEAD), lambda h, s: (_tri(s, counts)[0], h))
    return _pcall(body, name=name, grid=(MLA_HEADS, sum(counts)),
                  in_specs=[qblk, pl.BlockSpec((tk, QHEAD), lambda h, s: (_tri(s, counts)[1], h)),
                            pl.BlockSpec((tk, HEAD), lambda h, s: (_tri(s, counts)[1], h))],
                  out_specs=[oblk, oblk],
                  out_shape=[jax.ShapeDtypeStruct((t, MLA_HEADS * HEAD), F32), jax.ShapeDtypeStruct((t, MLA_HEADS * HEAD), F32)],
                  scratch=[pltpu.VMEM((tq, 1), F32), pltpu.VMEM((tq, 1), F32), pltpu.VMEM((tq, HEAD), F32)],
                  carry=carry)(q, k, v)


def _attn_bwd(q, k, v, do, lse, delta, *, name, tq=1024, tk=1024, carry=None):
    t = q.shape[0]
    tq, tk = _pick(t, tq), _pick(t, tk)
    nq, nk = t // tq, t // tk
    first_q = lambda j: (j * tk) // tq
    counts = [nq - first_q(j) for j in range(nk)]

    def where(step):
        j, off = _tri(step, counts)
        return j, first_q(j) + off

    lanes = lambda col: jnp.tile(col, (1, tk // HEAD))

    def body(q_ref, k_ref, v_ref, do_ref, lse_ref, dl_ref, dq_ref, dkv_ref, dkr_ref, dk_acc, dv_acc):
        j, i = where(pl.program_id(1))

        @pl.when(i == first_q(j))
        def _():
            dk_acc[...] = jnp.zeros_like(dk_acc)
            dv_acc[...] = jnp.zeros_like(dv_acc)

        def step(masked):
            qv, kv_, dov = q_ref[...], k_ref[...], do_ref[...].astype(BF16)
            s = lax.dot_general(qv, kv_, (NT, ((), ())), preferred_element_type=F32)
            p = jnp.exp((s - lanes(lse_ref[...])).astype(BF16))
            if masked:
                qpos = i * tq + lax.broadcasted_iota(jnp.int32, s.shape, 0)
                kpos = j * tk + lax.broadcasted_iota(jnp.int32, s.shape, 1)
                p = jnp.where(kpos <= qpos, p, jnp.zeros_like(p))
            dv_acc[...] += lax.dot_general(p, dov, (TN, ((), ())), preferred_element_type=F32)
            dp = lax.dot_general(dov, v_ref[...], (NT, ((), ())), preferred_element_type=F32)
            ds = p * (dp - lanes(dl_ref[...])).astype(BF16)
            dk_acc[...] += lax.dot_general(ds, qv, (TN, ((), ())), preferred_element_type=F32)
            contrib = lax.dot_general(ds, kv_, (NN, ((), ())), preferred_element_type=F32)
            rows = pl.ds(pl.multiple_of(i * tq, tq), tq)

            @pl.when(j == 0)
            def _():
                dq_ref[rows, :] = contrib

            @pl.when(j > 0)
            def _():
                dq_ref[rows, :] += contrib

        crosses = j * tk + tk - 1 > i * tq

        @pl.when(crosses)
        def _():
            step(True)

        @pl.when(jnp.logical_not(crosses))
        def _():
            step(False)

        @pl.when(i == nq - 1)
        def _():
            dk = dk_acc[...]
            dkv_ref[...] = jnp.concatenate([dk[:, :HEAD], dv_acc[...]], axis=1).astype(dkv_ref.dtype)
            dkr_ref[...] = dk[:, HEAD:]

    qi = lambda h, s: (where(s)[1], h)
    kj = lambda h, s: (where(s)[0], h)
    return _pcall(body, name=name, grid=(MLA_HEADS, sum(counts)),
                  in_specs=[pl.BlockSpec((tq, QHEAD), qi), pl.BlockSpec((tk, QHEAD), kj), pl.BlockSpec((tk, HEAD), kj),
                            pl.BlockSpec((tq, HEAD), qi), pl.BlockSpec((tq, HEAD), qi), pl.BlockSpec((tq, HEAD), qi)],
                  out_specs=[pl.BlockSpec((t, QHEAD), lambda h, s: (0, h)), pl.BlockSpec((tk, QHEAD), kj),
                             pl.BlockSpec((tk, HEAD), kj)],
                  out_shape=[jax.ShapeDtypeStruct((t, MLA_HEADS * QHEAD), F32), jax.ShapeDtypeStruct((t, MLA_HEADS * QHEAD), BF16),
                             jax.ShapeDtypeStruct((t, MLA_HEADS * HEAD), F32)],
                  scratch=[pltpu.VMEM((tk, QHEAD), F32), pltpu.VMEM((tk, HEAD), F32)], carry=carry)(q, k, v, do, lse, delta)


def _ffn_up(h, wgate, wup, *, name, bm=512, bn=FF_WIDE):
    t = h.shape[0]
    bm = _pick(t, bm)

    def body(h_ref, wg_ref, wu_ref, g_ref, u_ref, a_ref):
        hv = h_ref[...]
        g = lax.dot_general(hv, wg_ref[...], (NN, ((), ())), preferred_element_type=F32)
        u = lax.dot_general(hv, wu_ref[...], (NN, ((), ())), preferred_element_type=F32)
        g_ref[...] = g.astype(g_ref.dtype)
        u_ref[...] = u.astype(u_ref.dtype)
        a_ref[...] = (_silu(g) * u).astype(a_ref.dtype)

    w_spec = pl.BlockSpec((D_MODEL, bn), lambda j, i: (0, j))
    o_spec = pl.BlockSpec((bm, bn), lambda j, i: (i, j))
    sds = jax.ShapeDtypeStruct((t, D_FF), BF16)
    return _pcall(body, name=name, grid=(D_FF // bn, t // bm),
                  in_specs=[pl.BlockSpec((bm, D_MODEL), lambda j, i: (i, 0)), w_spec, w_spec],
                  out_specs=[o_spec] * 3, out_shape=[sds] * 3)(h, wgate, wup)


def _ffn_down_dx(dy, wdown, gate, up, *, name, bm=512, bn=FF_WIDE):
    t = dy.shape[0]
    bm = _pick(t, bm)

    def body(dy_ref, w_ref, g_ref, u_ref, dg_ref, du_ref):
        d = lax.dot_general(dy_ref[...].astype(BF16), w_ref[...], (NT, ((), ())), preferred_element_type=F32)
        g = g_ref[...].astype(F32)
        dg_ref[...] = (d * u_ref[...].astype(F32) * _dsilu(g)).astype(dg_ref.dtype)
        du_ref[...] = (d * _silu(g)).astype(du_ref.dtype)

    o_spec = pl.BlockSpec((bm, bn), lambda j, i: (i, j))
    sds = jax.ShapeDtypeStruct((t, D_FF), BF16)
    return _pcall(body, name=name, grid=(D_FF // bn, t // bm),
                  in_specs=[pl.BlockSpec((bm, D_MODEL), lambda j, i: (i, 0)), pl.BlockSpec((bn, D_MODEL), lambda j, i: (j, 0)),
                            o_spec, o_spec],
                  out_specs=[o_spec, o_spec], out_shape=[sds, sds])(dy, wdown, gate, up)


def _loss_bwd(x2, w, target, *, name):
    t = x2.shape[0]
    tm = _rows(t, D_MODEL)

    def body(x_ref, w_ref, t_ref, dx_ref, dw_ref, l_ref):
        xv, wv = x_ref[...], w_ref[...]
        r = lax.rsqrt(jnp.mean(xv * xv, axis=-1, keepdims=True) + EPS)
        xh = xv * r
        err = xh * wv - t_ref[...]
        dy = err * (1.0 / D_MODEL)
        dyw = dy * wv
        dx_ref[...] = r * (dyw - xh * jnp.mean(dyw * xh, axis=-1, keepdims=True))

        @pl.when(pl.program_id(0) == 0)
        def _():
            dw_ref[...] = jnp.zeros_like(dw_ref)
            l_ref[...] = jnp.zeros_like(l_ref)

        dw_ref[...] += (dy * xh).reshape(tm // 8, 8, D_MODEL).sum(axis=0)
        sq = (err * err).reshape(tm // 8, 8, D_MODEL).sum(axis=0)
        part = sq[:, :HEAD]
        for c in range(1, D_MODEL // HEAD):
            part = part + sq[:, c * HEAD:(c + 1) * HEAD]
        l_ref[...] += part * (0.5 / D_MODEL)

    row = pl.BlockSpec((tm, D_MODEL), lambda i: (i, 0))
    return _pcall(body, name=name, grid=(t // tm,),
                  in_specs=[row, pl.BlockSpec((1, D_MODEL), lambda i: (0, 0)), row],
                  out_specs=[row, pl.BlockSpec((8, D_MODEL), lambda i: (0, 0)), pl.BlockSpec((8, HEAD), lambda i: (0, 0))],
                  out_shape=[jax.ShapeDtypeStruct((t, D_MODEL), F32), jax.ShapeDtypeStruct((8, D_MODEL), F32),
                             jax.ShapeDtypeStruct((8, HEAD), F32)])(x2, w, target)


def _unshard_cols(g):
    return jnp.transpose(g, (1, 0, 2)).reshape(g.shape[1], N_DEV * g.shape[2])


def _shard_cols(w):
    return jnp.transpose(w.reshape(w.shape[0], N_DEV, w.shape[1] // N_DEV), (1, 0, 2))


_WIN_ORDER = ((0, 4096), (4112, 5136), (5136, 5200), (4096, 4112))
_WIN_SHARD = IN_WIDTH // N_DEV


def _win_pieces():
    out, pos = [], 0
    for a, b in _WIN_ORDER:
        c = a
        while c < b:
            dev, off = divmod(c, _WIN_SHARD)
            width = min(b, (dev + 1) * _WIN_SHARD) - c
            out.append((dev, off, width, pos))
            c, pos = c + width, pos + width
    return out


def _win_gathered_to_padded(g):
    pieces = [g[dev][:, off:off + width] for dev, off, width, _ in _win_pieces()]
    return jnp.concatenate(pieces + [jnp.zeros((g.shape[1], PROJ_W - IN_WIDTH), g.dtype)], axis=1)


def _win_padded_to_shards(d):
    shards = []
    for dev in range(N_DEV):
        mine = sorted((off, width, pos) for dv, off, width, pos in _win_pieces() if dv == dev)
        shards.append(jnp.concatenate([d[:, pos:pos + width] for _, width, pos in mine], axis=1))
    return jnp.stack(shards)


def _wuq_to_padded(w):
    w3 = w.reshape(w.shape[0], MLA_HEADS, HEAD + QK_ROPE)
    return jnp.pad(w3, ((0, 0), (0, 0), (0, QHEAD - HEAD - QK_ROPE))).reshape(w.shape[0], MLA_HEADS * QHEAD)


def _wuq_from_padded(d):
    return d.reshape(d.shape[0], MLA_HEADS, QHEAD)[:, :, :HEAD + QK_ROPE].reshape(d.shape[0], MLA_HEADS * (HEAD + QK_ROPE))


def _late_weights(g_out, g_gate, g_up, g_down):
    return g_out.reshape(D_MODEL, D_MODEL), _unshard_cols(g_gate), _unshard_cols(g_up), g_down.reshape(D_FF, D_MODEL)


def _local_step(x, pos, target, win_p, wuq_p, wukv, late, conv_w, small, exchange):
    cs = _rope_cs(pos, _rope_tables(), name="rope_cs")
    if not exchange:
        wout, wgate, wup, wdown = late
    h1 = _rms_fwd(x, small["attn_norm_w"], name="rms1_fwd", width=D_MODEL)
    proj = _mm(h1, win_p, name="mm_in", bn=768)
    qkv = _conv_fwd(proj, conv_w, name="conv_fwd")
    o_gdn_raw, hist = _gdn_fwd(qkv, proj, small["gdn_params"], name="gdn_fwd")
    o_gdn = _gate_fwd(o_gdn_raw, proj, small["gdn_norm_w"], name="gate_fwd")
    cqn = _rms_fwd(proj, small["q_norm_w"], name="rmsq_fwd", width=Q_LORA, col0=COL_CQ)
    ckvn = _rms_fwd(proj, small["kv_norm_w"], name="rmskv_fwd", width=KV_LORA, col0=COL_CKV)
    q_full = _q_up(cqn, wuq_p, cs, name="q_up")
    k_full, v_b = _kv_up(ckvn, wukv, proj, cs, name="kv_up")
    if exchange:
        (o_mla_raw, lse), gathered = _attn_fwd(q_full, k_full, v_b, name="attn_fwd", carry=_Gather(late))
        wout, wgate, wup, wdown = _late_weights(*gathered)
    else:
        o_mla_raw, lse = _attn_fwd(q_full, k_full, v_b, name="attn_fwd")
    o_mla = _rms_fwd(o_mla_raw, small["mla_out_norm_w"], name="rmso_fwd", width=HEAD, heads=MLA_HEADS)
    mixed = jnp.concatenate([o_gdn, o_mla], axis=1)
    x1 = _mm(mixed, wout, name="mm_out", res=x)
    h2 = _rms_fwd(x1, small["ffn_norm_w"], name="rms2_fwd", width=D_MODEL)
    gate, up, act = _ffn_up(h2, wgate, wup, name="ffn_up")
    x2 = _mm(act, wdown, name="mm_down", res=x1, bk=FF_WIDE)
    dx2, dw_final, loss_part = _loss_bwd(x2, small["final_norm_w"], target, name="loss_bwd")
    dgate, dup = _ffn_down_dx(dx2, wdown, gate, up, name="ffn_down_dx")
    d_wdown = _mm(act, dx2, name="mm_down_dw", ta=True, out_dtype=BF16, bm=FF_WIDE)
    dh2 = _mm(dgate, wgate, name="mm_gate_dx", tb=True, bk=FF_WIDE)
    dh2 = _mm(dup, wup, name="mm_up_dx", tb=True, res=dh2, bk=FF_WIDE)
    d_wgate = _mm(h2, dgate, name="mm_gate_dw", ta=True, out_dtype=BF16, bn=FF_WIDE)
    d_wup = _mm(h2, dup, name="mm_up_dw", ta=True, out_dtype=BF16, bn=FF_WIDE)
    dx1, dw_ffn = _rms_bwd(x1, small["ffn_norm_w"], dh2, name="rms2_bwd", width=D_MODEL, res=dx2)
    dmixed = _mm(dx1, wout, name="mm_out_dx", tb=True)
    d_wout = _mm(mixed, dx1, name="mm_out_dw", ta=True, out_dtype=BF16)
    do_mla, dw_mla_out, delta = _rms_bwd(o_mla_raw, small["mla_out_norm_w"], dmixed, name="rmso_bwd", width=HEAD,
                                         heads=MLA_HEADS, dcol0=GDN_QK, with_delta=True, out_dtype=BF16)
    if exchange:
        send = [d_wdown.reshape(N_DEV, D_FF // N_DEV, D_MODEL), _shard_cols(d_wgate), _shard_cols(d_wup)]
        (dq_full, dkv, dkr_h), (r_down, r_gate, r_up) = _attn_bwd(q_full, k_full, v_b, do_mla, lse, delta, name="attn_bwd",
                                                                  carry=_Exchange(send, [False] * 3))
    else:
        dq_full, dkv, dkr_h = _attn_bwd(q_full, k_full, v_b, do_mla, lse, delta, name="attn_bwd")
    dq_pre = _q_rot(dq_full, cs, name="q_rot_bwd", sign=-1.0)
    dmisc_kr = _krope_bwd(dkr_h, cs, name="krope_bwd")
    dcqn = _mm(dq_pre, wuq_p, name="mm_uq_dx", tb=True)
    d_wuq = _mm(cqn, dq_pre, name="mm_uq_dw", ta=True, out_dtype=BF16)
    dckvn = _mm(dkv, wukv, name="mm_ukv_dx", tb=True)
    d_wukv = _mm(ckvn, dkv, name="mm_ukv_dw", ta=True, out_dtype=BF16)
    dcq, dw_qn = _rms_bwd(proj, small["q_norm_w"], dcqn, name="rmsq_bwd", width=Q_LORA, col0=COL_CQ, out_dtype=BF16)
    dckv, dw_kvn = _rms_bwd(proj, small["kv_norm_w"], dckvn, name="rmskv_bwd", width=KV_LORA, col0=COL_CKV, out_dtype=BF16)
    do_gdn, dz, dw_gdn = _gate_bwd(o_gdn_raw, proj, small["gdn_norm_w"], dmixed, name="gate_bwd")
    if exchange:
        send = [d_wout.reshape(N_DEV, D_MODEL // N_DEV, D_MODEL), _shard_cols(_wuq_from_padded(d_wuq)), _shard_cols(d_wukv)]
        (dqkv, dmisc, d_params), (r_out, r_uq, r_ukv) = _gdn_bwd(
            qkv, proj, small["gdn_params"], hist, do_gdn, dmisc_kr, name="gdn_bwd", carry=_Exchange(send, [False] * 3))
    else:
        dqkv, dmisc, d_params = _gdn_bwd(qkv, proj, small["gdn_params"], hist, do_gdn, dmisc_kr, name="gdn_bwd")
    dqkv_pre, dconv = _conv_bwd(proj, conv_w, dqkv, name="conv_bwd")
    dproj = jnp.concatenate([dqkv_pre, dz, dcq, dckv, dmisc.astype(BF16), jnp.zeros((x.shape[0], PROJ_W - COL_MISC - HEAD), BF16)], axis=1)
    d_win = _mm(h1, dproj, name="mm_in_dw", ta=True, out_dtype=BF16, bn=768)
    if exchange:
        dh1, (r_in,) = _mm(dproj, win_p, name="mm_in_dx", tb=True, bk=768,
                           carry=_Exchange([_win_padded_to_shards(d_win)], [False]))
        d_win = r_in
    else:
        dh1 = _mm(dproj, win_p, name="mm_in_dx", tb=True, bk=768)
    dx, dw_attn = _rms_bwd(x, small["attn_norm_w"], dh1, name="rms1_bwd", width=D_MODEL, res=dx1)

    if exchange:
        big = {"w_in": d_win, "w_uq": r_uq, "w_ukv": r_ukv, "w_out": r_out, "w_gate": r_gate, "w_up": r_up, "w_down": r_down}
    else:
        big = {"w_in": d_win, "w_uq": d_wuq, "w_ukv": d_wukv, "w_out": d_wout, "w_gate": d_wgate, "w_up": d_wup,
               "w_down": d_wdown}
    sm = {"attn_norm_w": dw_attn, "ffn_norm_w": dw_ffn, "final_norm_w": dw_final, "q_norm_w": dw_qn, "kv_norm_w": dw_kvn,
          "gdn_norm_w": dw_gdn, "mla_out_norm_w": dw_mla_out, "gdn_params": d_params, "conv_w": dconv, "loss": loss_part}
    return dx, big, sm


def _exchange(ex, *, name):
    def body(*refs):
        ins, outs, sems = refs[:ex.n], refs[ex.n:2 * ex.n], refs[2 * ex.n:]
        ex.start(ins, outs, sems)
        ex.forward(ins, outs, sems)
        ex.finish(ins, outs, sems)

    any_spec = pl.BlockSpec(memory_space=pl.ANY)
    return pl.pallas_call(body, name=name, in_specs=[any_spec] * ex.n, out_specs=[any_spec] * ex.n,
                          out_shape=ex.out_shape(), scratch_shapes=ex.sems())(*ex.arrays)


def _adamw_math(g, w, m, v):
    m = ADAM_B1 * m + (1.0 - ADAM_B1) * g
    v = ADAM_B2 * v + (1.0 - ADAM_B2) * (g * g)
    m_hat = m / (1.0 - ADAM_B1 ** ADAM_STEP)
    v_hat = v / (1.0 - ADAM_B2 ** ADAM_STEP)
    delta = -ADAM_LR * (m_hat / (jnp.sqrt(v_hat) + ADAM_EPS) + ADAM_WD * w)
    return delta, m, v


def _adamw(parts, w, m, v, *, name):
    npart, r, c = parts.shape
    tr = r if r * c * 4 <= (1 << 20) else _rows(r, c, 1 << 20)

    def body(p_ref, w_ref, m_ref, v_ref, g_ref, d_ref, nm_ref, nv_ref):
        g = p_ref[0].astype(F32)
        for s in range(1, npart):
            g = g + p_ref[s].astype(F32)
        g_ref[...] = g
        d_ref[...], nm_ref[...], nv_ref[...] = _adamw_math(g, w_ref[...], m_ref[...], v_ref[...])

    blk = pl.BlockSpec((tr, c), lambda i: (i, 0))
    sds = jax.ShapeDtypeStruct((r, c), F32)
    return _pcall(body, name=name, grid=(r // tr,),
                  in_specs=[pl.BlockSpec((npart, tr, c), lambda i: (0, i, 0)), blk, blk, blk],
                  out_specs=[blk] * 4, out_shape=[sds] * 4)(parts, w, m, v)


def _sum_parts(parts, *, name):
    npart, r, c = parts.shape

    def body(p_ref, o_ref):
        g = p_ref[0]
        for s in range(1, npart):
            g = g + p_ref[s]
        o_ref[...] = g

    return _pcall(body, name=name, grid=(1,), in_specs=[pl.BlockSpec((npart, r, c), lambda i: (0, 0, 0))],
                  out_specs=pl.BlockSpec((r, c), lambda i: (0, 0)), out_shape=jax.ShapeDtypeStruct((r, c), F32))(parts)


_SMALL = (("attn_norm_w", D_MODEL), ("ffn_norm_w", D_MODEL), ("final_norm_w", D_MODEL), ("q_norm_w", Q_LORA),
          ("kv_norm_w", KV_LORA), ("gdn_norm_w", HEAD), ("mla_out_norm_w", HEAD), ("a_log", HEAD), ("dt_bias", HEAD))
_SMALL_ROWS = sum(n for _, n in _SMALL) // HEAD
_CONV_ROWS = GDN_CONV * CONV_CH // HEAD
_PACK_ROWS = 160


def _pad_lanes(v, n):
    v = v.reshape(-1)
    return jnp.concatenate([v, jnp.zeros((n - v.shape[0],), v.dtype)])


def kernel(x, positions, attn_norm_w, w_in, conv_w, a_log, dt_bias, gdn_norm_w, q_norm_w, w_uq, kv_norm_w, w_ukv, mla_out_norm_w, w_out, ffn_norm_w, w_gate, w_up, w_down, final_norm_w, loss_target, m_attn_norm_w, m_w_in, m_conv_w, m_a_log, m_dt_bias, m_gdn_norm_w, m_q_norm_w, m_w_uq, m_kv_norm_w, m_w_ukv, m_mla_out_norm_w, m_w_out, m_ffn_norm_w, m_w_gate, m_w_up, m_w_down, m_final_norm_w, v_attn_norm_w, v_w_in, v_conv_w, v_a_log, v_dt_bias, v_gdn_norm_w, v_q_norm_w, v_w_uq, v_kv_norm_w, v_w_ukv, v_mla_out_norm_w, v_w_out, v_ffn_norm_w, v_w_gate, v_w_up, v_w_down, v_final_norm_w):
    t = x.shape[1]
    me = 4 * lax.axis_index("x") + 2 * lax.axis_index("y") + lax.axis_index("c")
    weights = dict(attn_norm_w=attn_norm_w, w_in=w_in, conv_w=conv_w, a_log=a_log, dt_bias=dt_bias, gdn_norm_w=gdn_norm_w,
                   q_norm_w=q_norm_w, w_uq=w_uq, kv_norm_w=kv_norm_w, w_ukv=w_ukv, mla_out_norm_w=mla_out_norm_w, w_out=w_out,
                   ffn_norm_w=ffn_norm_w, w_gate=w_gate, w_up=w_up, w_down=w_down, final_norm_w=final_norm_w)
    mom_m = dict(attn_norm_w=m_attn_norm_w, w_in=m_w_in, conv_w=m_conv_w, a_log=m_a_log, dt_bias=m_dt_bias, gdn_norm_w=m_gdn_norm_w,
                 q_norm_w=m_q_norm_w, w_uq=m_w_uq, kv_norm_w=m_kv_norm_w, w_ukv=m_w_ukv, mla_out_norm_w=m_mla_out_norm_w,
                 w_out=m_w_out, ffn_norm_w=m_ffn_norm_w, w_gate=m_w_gate, w_up=m_w_up, w_down=m_w_down, final_norm_w=m_final_norm_w)
    mom_v = dict(attn_norm_w=v_attn_norm_w, w_in=v_w_in, conv_w=v_conv_w, a_log=v_a_log, dt_bias=v_dt_bias, gdn_norm_w=v_gdn_norm_w,
                 q_norm_w=v_q_norm_w, w_uq=v_w_uq, kv_norm_w=v_kv_norm_w, w_ukv=v_w_ukv, mla_out_norm_w=v_mla_out_norm_w,
                 w_out=v_w_out, ffn_norm_w=v_ffn_norm_w, w_gate=v_w_gate, w_up=v_w_up, w_down=v_w_down, final_norm_w=v_final_norm_w)
    big_names = ("w_in", "w_uq", "w_ukv", "w_out", "w_gate", "w_up", "w_down")

    shard = {n: weights[n][0].astype(BF16) for n in big_names}
    g_in, g_uq, g_ukv, g_conv = _exchange(_Gather([shard["w_in"], shard["w_uq"], shard["w_ukv"], weights["conv_w"][0]]),
                                          name="gather_weights")
    win_p = _win_gathered_to_padded(g_in)
    wuq_p = _wuq_to_padded(_unshard_cols(g_uq))
    wukv = _unshard_cols(g_ukv)
    late = [shard["w_out"], shard["w_gate"], shard["w_up"], shard["w_down"]]
    conv_full = jnp.concatenate([_unshard_cols(g_conv), jnp.zeros((8 - GDN_CONV, CONV_CH), F32)], axis=0)

    gdn_params = jnp.concatenate([_pad_lanes(a_log, HEAD)[None], _pad_lanes(dt_bias, HEAD)[None], jnp.zeros((6, HEAD), F32)], axis=0)
    small = {n: weights[n].reshape(1, -1) for n in ("attn_norm_w", "ffn_norm_w", "final_norm_w", "q_norm_w", "kv_norm_w",
                                                    "gdn_norm_w", "mla_out_norm_w")}
    small["gdn_params"] = gdn_params

    dx, big, sm = _local_step(x[0], positions.reshape(t, 1).astype(F32), loss_target[0], win_p, wuq_p, wukv, late,
                              conv_full, small, True)

    rows8 = lambda name: jnp.sum(sm[name], axis=0)
    pieces = [rows8(n) for n, _ in _SMALL[:7]]
    pieces += [_pad_lanes(jnp.sum(sm["gdn_params"][0:1], axis=0), HEAD), _pad_lanes(jnp.sum(sm["gdn_params"][1:2], axis=0), HEAD)]
    pieces.append(jnp.sum(sm["conv_w"], axis=1).reshape(-1))
    pieces.append(_pad_lanes(jnp.sum(sm["loss"]).reshape(1), HEAD))
    packed = _pad_lanes(jnp.concatenate(pieces), _PACK_ROWS * HEAD).reshape(_PACK_ROWS, HEAD)
    (r_small,) = _exchange(_Exchange([packed], [True]), name="exchange_small")

    outs_g, outs_d, outs_m, outs_v = {}, {}, {}, {}
    for name in big_names:
        g, d, nm, nv = _adamw(big[name], weights[name][0], mom_m[name][0], mom_v[name][0], name="adamw_" + name)
        outs_g[name], outs_d[name], outs_m[name], outs_v[name] = g[None], d[None], nm[None], nv[None]

    total = _sum_parts(r_small, name="sum_small")
    flat = total.reshape(-1)
    loss = flat[(_SMALL_ROWS + _CONV_ROWS) * HEAD]
    g_small, off = {}, 0
    for n, size in _SMALL:
        g_small[n] = flat[off:off + size]
        off += size
    g_conv_full = flat[off:off + GDN_CONV * CONV_CH].reshape(GDN_CONV, CONV_CH)
    g_small["conv_w"] = lax.dynamic_slice(g_conv_full, (0, me * (CONV_CH // N_DEV)), (GDN_CONV, CONV_CH // N_DEV)).reshape(-1)
    order = [n for n, _ in _SMALL] + ["conv_w"]
    sizes = dict(_SMALL)
    sizes["conv_w"] = GDN_CONV * CONV_CH // N_DEV
    true_size = {n: weights[n].size for n in order}

    def pack(d):
        return jnp.concatenate([_pad_lanes(d[n], sizes[n]) for n in order]).reshape(1, -1, HEAD)

    g2, d2, m2, v2 = _adamw(pack(g_small), pack(weights)[0], pack(mom_m)[0], pack(mom_v)[0], name="adamw_small")
    off = 0
    for n in order:
        for src, dst in ((g2, outs_g), (d2, outs_d), (m2, outs_m), (v2, outs_v)):
            dst[n] = src.reshape(-1)[off:off + true_size[n]].reshape(weights[n].shape)
        off += sizes[n]

    names = ("attn_norm_w", "w_in", "conv_w", "a_log", "dt_bias", "gdn_norm_w", "q_norm_w", "w_uq", "kv_norm_w", "w_ukv",
             "mla_out_norm_w", "w_out", "ffn_norm_w", "w_gate", "w_up", "w_down", "final_norm_w")
    return (loss, dx[None], *[outs_g[n] for n in names], *[outs_d[n] for n in names], *[outs_m[n] for n in names],
            *[outs_v[n] for n in names])
```

```python
import functools
import math

import jax
import jax.numpy as jnp
from jax import lax
from jax.experimental import pallas as pl
from jax.experimental.pallas import tpu as pltpu

F32 = jnp.float32
BF16 = jnp.bfloat16

D_MODEL = 2048
GDN_HEADS = 8
HEAD = 128
GDN_CONV = 4
GDN_CHUNK = 64
GDN_QK = GDN_HEADS * HEAD
CONV_CH = 3 * GDN_QK
MLA_HEADS = 8
QK_ROPE = 64
Q_LORA = 512
KV_LORA = 512
ROPE_THETA = 10000.0
D_FF = 5632
EPS = 1e-6
IN_WIDTH = 5200
ADAM_LR, ADAM_B1, ADAM_B2, ADAM_EPS, ADAM_WD, ADAM_STEP = 0.001, 0.9, 0.999, 1e-08, 0.01, 10

PROJ_W = 5376
COL_Z = 3072
COL_CQ = 4096
COL_CKV = 4608
COL_MISC = 5120
LANE_B = 64
LANE_A = 72
QHEAD = 256
FF_WIDE = D_FF // 4
N_DEV = 8
MESH = pl.DeviceIdType.MESH
VMEM_LIMIT_MB = 48

NN = ((1,), (0,))
NT = ((1,), (1,))
TN = ((0,), (0,))


def _my_place():
    x, y, c = lax.axis_index("x"), lax.axis_index("y"), lax.axis_index("c")
    return x, y, c, 4 * x + 2 * y + c


def _peer(x, y, c, p):
    px, py, pc = x ^ ((p >> 2) & 1), y ^ ((p >> 1) & 1), c ^ (p & 1)
    return (px, py, pc), 4 * px + 2 * py + pc


class _Exchange:
    def __init__(self, arrays, gather):
        self.arrays, self.gather, self.n = list(arrays), list(gather), len(arrays)

    def out_shape(self):
        return [jax.ShapeDtypeStruct(((N_DEV,) + a.shape) if g else a.shape, a.dtype)
                for a, g in zip(self.arrays, self.gather)]

    def sems(self):
        return [pltpu.SemaphoreType.DMA((self.n * (N_DEV - 1),)), pltpu.SemaphoreType.DMA((self.n * (N_DEV - 1),)),
                pltpu.SemaphoreType.DMA((self.n,))]

    def _copies(self, ins, outs, sems):
        send_sems, recv_sems, local_sems = sems
        x, y, c, me = _my_place()
        local = [pltpu.make_async_copy(ins[k] if self.gather[k] else ins[k].at[me], outs[k].at[me], local_sems.at[k])
                 for k in range(self.n)]
        sent, received = [], []
        for p in range(1, N_DEV):
            place, num = _peer(x, y, c, p)
            for k in range(self.n):
                src = ins[k] if self.gather[k] else ins[k].at[num]
                idx = k * (N_DEV - 1) + p - 1
                mk = lambda dst: pltpu.make_async_remote_copy(src_ref=src, dst_ref=dst, send_sem=send_sems.at[idx],
                                                              recv_sem=recv_sems.at[idx], device_id=place, device_id_type=MESH)
                sent.append(mk(outs[k].at[me]))
                received.append(mk(outs[k].at[num]))
        return local, sent, received

    def start(self, ins, outs, sems):
        local, sent, _ = self._copies(ins, outs, sems)
        for cp in local + sent:
            cp.start()

    def forward(self, ins, outs, sems):
        pass

    def finish(self, ins, outs, sems):
        local, sent, received = self._copies(ins, outs, sems)
        for cp in received:
            cp.wait_recv()
        for cp in sent:
            cp.wait_send()
        for cp in local:
            cp.wait()


class _Gather:
    def __init__(self, arrays):
        self.arrays, self.n = list(arrays), len(arrays)

    def out_shape(self):
        return [jax.ShapeDtypeStruct((N_DEV,) + a.shape, a.dtype) for a in self.arrays]

    def sems(self):
        return [pltpu.SemaphoreType.DMA((self.n * (N_DEV - 1),)), pltpu.SemaphoreType.DMA((self.n * (N_DEV - 1),)),
                pltpu.SemaphoreType.DMA((self.n,))]

    def _plan(self, ins, outs, sems):
        send_sems, recv_sems, local_sems = sems
        x, y, c, me = _my_place()
        sibling = (x, y, 1 - c)
        chips = [(1 - x, y), (x, 1 - y), (1 - x, 1 - y)]
        num = lambda px, py, pc: 4 * px + 2 * py + pc

        def copy(k, i, block, to, src=None):
            slot = outs[k].at[num(*block)]
            return pltpu.make_async_remote_copy(src_ref=slot if src is None else src, dst_ref=slot,
                                                send_sem=send_sems.at[k * (N_DEV - 1) + i],
                                                recv_sem=recv_sems.at[k * (N_DEV - 1) + i],
                                                device_id=to, device_id_type=MESH)

        local = [pltpu.make_async_copy(ins[k], outs[k].at[me], local_sems.at[k]) for k in range(self.n)]
        return (x, y, c), sibling, chips, copy, local

    def start(self, ins, outs, sems):
        me, sibling, chips, copy, local = self._plan(ins, outs, sems)
        for cp in local:
            cp.start()
        for k in range(self.n):
            copy(k, 0, me, sibling, src=ins[k]).start()
            for j, chip in enumerate(chips):
                copy(k, 1 + j, me, (*chip, me[2]), src=ins[k]).start()

    def forward(self, ins, outs, sems):
        me, sibling, chips, copy, _ = self._plan(ins, outs, sems)
        for j, chip in enumerate(chips):
            for k in range(self.n):
                copy(k, 1 + j, (*chip, me[2]), me).wait_recv()
                copy(k, 4 + j, (*chip, me[2]), sibling).start()

    def finish(self, ins, outs, sems):
        me, sibling, chips, copy, local = self._plan(ins, outs, sems)
        for k in range(self.n):
            copy(k, 0, sibling, me).wait_recv()
            for j, chip in enumerate(chips):
                copy(k, 4 + j, (*chip, 1 - me[2]), me).wait_recv()
        for k in range(self.n):
            copy(k, 0, me, sibling, src=ins[k]).wait_send()
            for j, chip in enumerate(chips):
                copy(k, 1 + j, me, (*chip, me[2]), src=ins[k]).wait_send()
                copy(k, 4 + j, (*chip, me[2]), sibling).wait_send()
        for cp in local:
            cp.wait()


def _pcall(body, *, name, grid, in_specs, out_specs, out_shape, scratch=(), carry=None):
    params = pltpu.CompilerParams(dimension_semantics=("arbitrary",) * len(grid), vmem_limit_bytes=VMEM_LIMIT_MB << 20)
    if carry is None:
        return pl.pallas_call(body, name=name, grid=grid, in_specs=in_specs, out_specs=out_specs, out_shape=out_shape,
                              scratch_shapes=list(scratch), compiler_params=params)
    single = not isinstance(out_specs, (list, tuple))
    out_specs = [out_specs] if single else list(out_specs)
    out_shape = [out_shape] if single else list(out_shape)
    n_in, n_out, n_scr, na = len(in_specs), len(out_specs), len(scratch), carry.n

    def wrapped(*refs):
        ins, cin = refs[:n_in], refs[n_in:n_in + na]
        outs, cout = refs[n_in + na:n_in + na + n_out], refs[n_in + na + n_out:n_in + 2 * na + n_out]
        scr, sems = refs[n_in + 2 * na + n_out:n_in + 2 * na + n_out + n_scr], refs[n_in + 2 * na + n_out + n_scr:]
        total = math.prod(grid)
        step = functools.reduce(lambda a, d: a * grid[d] + pl.program_id(d), range(len(grid)), 0)

        @pl.when(step == 0)
        def _():
            carry.start(cin, cout, sems)

        body(*ins, *outs, *scr)

        @pl.when(step == min(total * 5 // 8, total - 1))
        def _():
            carry.forward(cin, cout, sems)

        @pl.when(step == total - 1)
        def _():
            carry.finish(cin, cout, sems)

    any_spec = pl.BlockSpec(memory_space=pl.ANY)
    call = pl.pallas_call(wrapped, name=name, grid=grid, in_specs=list(in_specs) + [any_spec] * na,
                          out_specs=out_specs + [any_spec] * na, out_shape=out_shape + carry.out_shape(),
                          scratch_shapes=list(scratch) + carry.sems(), compiler_params=params)

    def run(*args):
        res = call(*args, *carry.arrays)
        main = res[0] if single else list(res[:n_out])
        return main, list(res[n_out:])

    return run


def _pick(dim, pref):
    if dim <= pref:
        return dim
    c = pref
    while c >= 128:
        if dim % c == 0 and c % 128 == 0:
            return c
        c -= 128
    return dim


def _rows(t, width, target_bytes=2 << 20):
    r = max(8, min(t, target_bytes // (4 * width)))
    r = 1 << (r.bit_length() - 1)
    while t % r:
        r //= 2
    return r


MM_FULL_K = 2048


def _mm(a, b, *, name, ta=False, tb=False, res=None, out_dtype=F32, bm=1024, bn=1024, bk=1024, carry=None):
    m, k = (a.shape[1], a.shape[0]) if ta else a.shape
    n = b.shape[0] if tb else b.shape[1]
    assert (b.shape[1] if tb else b.shape[0]) == k
    bm, bn, bk = _pick(m, bm), _pick(n, bn), (k if k <= MM_FULL_K else _pick(k, bk))
    nk = k // bk
    dims = (((0,) if ta else (1,), (1,) if tb else (0,)), ((), ()))

    def body(*refs):
        a_ref, b_ref = refs[:2]
        r_ref = refs[2] if res is not None else None
        o_ref = refs[3] if res is not None else refs[2]
        part = lax.dot_general(a_ref[...].astype(BF16), b_ref[...].astype(BF16), dims, preferred_element_type=F32)

        def finish(out):
            if res is not None:
                out = out + r_ref[...]
            o_ref[...] = out.astype(o_ref.dtype)

        if nk == 1:
            finish(part)
            return
        acc_ref = refs[-1]
        kk = pl.program_id(2)

        @pl.when(kk == 0)
        def _():
            acc_ref[...] = part

        @pl.when((kk > 0) & (kk < nk - 1))
        def _():
            acc_ref[...] += part

        @pl.when(kk == nk - 1)
        def _():
            finish(acc_ref[...] + part)

    a_spec = pl.BlockSpec((bk, bm), lambda i, j, kk: (kk, i)) if ta else pl.BlockSpec((bm, bk), lambda i, j, kk: (i, kk))
    b_spec = pl.BlockSpec((bn, bk), lambda i, j, kk: (j, kk)) if tb else pl.BlockSpec((bk, bn), lambda i, j, kk: (kk, j))
    o_spec = pl.BlockSpec((bm, bn), lambda i, j, kk: (i, j))
    ins, specs = [a, b], [a_spec, b_spec]
    if res is not None:
        ins.append(res)
        specs.append(o_spec)
    return _pcall(body, name=name, grid=(m // bm, n // bn, nk), in_specs=specs, out_specs=o_spec,
                  out_shape=jax.ShapeDtypeStruct((m, n), out_dtype),
                  scratch=[pltpu.VMEM((bm, bn), F32)] if nk > 1 else [], carry=carry)(*ins)


def _rms_fwd(x, w, *, name, width, heads=1, col0=0, out_dtype=BF16):
    t = x.shape[0]
    tm = _rows(t, width)
    cb = col0 // width

    def body(x_ref, w_ref, o_ref):
        xv = x_ref[...]
        r = lax.rsqrt(jnp.mean(xv * xv, axis=-1, keepdims=True) + EPS)
        o_ref[...] = (xv * r * w_ref[...]).astype(o_ref.dtype)

    return _pcall(body, name=name, grid=(t // tm, heads),
                  in_specs=[pl.BlockSpec((tm, width), lambda i, h: (i, cb + h)),
                            pl.BlockSpec((1, width), lambda i, h: (0, 0))],
                  out_specs=pl.BlockSpec((tm, width), lambda i, h: (i, h)),
                  out_shape=jax.ShapeDtypeStruct((t, heads * width), out_dtype))(x, w)


def _rms_bwd(x, w, dy, *, name, width, heads=1, col0=0, dcol0=0, res=None, out_dtype=F32, with_delta=False):
    t = x.shape[0]
    tm = _rows(t, width)
    cb, dcb = col0 // width, dcol0 // width

    def body(*refs):
        refs = list(refs)
        x_ref, w_ref, dy_ref = refs[:3]
        r_ref = refs[3] if res is not None else None
        outs = refs[4:] if res is not None else refs[3:]
        dx_ref, dw_ref = outs[:2]
        xv = x_ref[...]
        dyv = dy_ref[...].astype(F32)
        r = lax.rsqrt(jnp.mean(xv * xv, axis=-1, keepdims=True) + EPS)
        xh = xv * r
        dyw = dyv * w_ref[...]
        dx = r * (dyw - xh * jnp.mean(dyw * xh, axis=-1, keepdims=True))
        if with_delta:
            outs[2][...] = jnp.broadcast_to(jnp.sum(dx * xv, axis=-1, keepdims=True), dx.shape)
        if res is not None:
            dx = dx + r_ref[...]
        dx_ref[...] = dx.astype(dx_ref.dtype)

        @pl.when((pl.program_id(0) == 0) & (pl.program_id(1) == 0))
        def _():
            dw_ref[...] = jnp.zeros_like(dw_ref)

        dw_ref[...] += (dyv * xh).reshape(tm // 8, 8, width).sum(axis=0)

    blk = pl.BlockSpec((tm, width), lambda i, h: (i, h))
    ins = [x, w, dy]
    specs = [pl.BlockSpec((tm, width), lambda i, h: (i, cb + h)), pl.BlockSpec((1, width), lambda i, h: (0, 0)),
             pl.BlockSpec((tm, width), lambda i, h: (i, dcb + h))]
    if res is not None:
        ins.append(res)
        specs.append(blk)
    out_shape = [jax.ShapeDtypeStruct((t, heads * width), out_dtype), jax.ShapeDtypeStruct((8, width), F32)]
    out_specs = [blk, pl.BlockSpec((8, width), lambda i, h: (0, 0))]
    if with_delta:
        out_shape.append(jax.ShapeDtypeStruct((t, heads * width), F32))
        out_specs.append(blk)
    return _pcall(body, name=name, grid=(t // tm, heads), in_specs=specs, out_specs=out_specs, out_shape=out_shape)(*ins)


def _sig(x):
    return 1.0 / (1.0 + jnp.exp(-x))


@jax.custom_vjp
def _sigmoid(x):
    return _sig(x)


def _sigmoid_fwd(x):
    s = _sig(x)
    return s, s


def _sigmoid_bwd(s, g):
    return (g * s * (1.0 - s),)


_sigmoid.defvjp(_sigmoid_fwd, _sigmoid_bwd)


@jax.custom_vjp
def _softplus(x):
    return jnp.maximum(x, 0.0) + jnp.log(1.0 + jnp.exp(-jnp.abs(x)))


def _softplus_fwd(x):
    return _softplus(x), x


def _softplus_bwd(x, g):
    return (g * _sig(x),)


_softplus.defvjp(_softplus_fwd, _softplus_bwd)


def _silu(x):
    return x * _sig(x)


def _dsilu(x):
    s = _sig(x)
    return s * (1.0 + x * (1.0 - s))


NN3 = (((2,), (1,)), ((0,), (0,)))
NT3 = (((2,), (2,)), ((0,), (0,)))
TN3 = (((1,), (1,)), ((0,), (0,)))


def _bdot(a, b, dims):
    return lax.dot_general(a.astype(BF16), b.astype(BF16), dims, preferred_element_type=F32)


def _bf16_part(x):
    bits = lax.bitcast_convert_type(x, jnp.uint32) & jnp.uint32(0xFFFF0000)
    return lax.bitcast_convert_type(bits, F32)


def _scan_rows(x, reverse):
    c = x.shape[1]
    row = lax.broadcasted_iota(jnp.int32, x.shape, 1)
    step = 1
    while step < c:
        if reverse:
            x = x + jnp.where(row < c - step, pltpu.roll(x, c - step, axis=1), 0.0)
        else:
            x = x + jnp.where(row >= step, pltpu.roll(x, step, axis=1), 0.0)
        step *= 2
    return x


@jax.custom_vjp
def _prefix_rows(x):
    return _scan_rows(x, False)


_prefix_rows.defvjp(lambda x: (_scan_rows(x, False), None), lambda _, g: (_scan_rows(g, True),))


def _dot3(a, b, dims):
    (ca,), (cb,) = dims[0]
    a_hi, b_hi = _bf16_part(a), _bf16_part(b)
    a_lo, b_lo = (a - a_hi).astype(BF16), (b - b_hi).astype(BF16)
    a_hi, b_hi = a_hi.astype(BF16), b_hi.astype(BF16)
    return lax.dot_general(jnp.concatenate([a_hi, a_hi, a_lo], axis=ca), jnp.concatenate([b_hi, b_lo, b_hi], axis=cb),
                           dims, preferred_element_type=F32)


@jax.custom_vjp
def _nn_hi(a, b):
    return _dot3(a, b, NN3)


_nn_hi.defvjp(lambda a, b: (_dot3(a, b, NN3), (a, b)), lambda r, g: (_dot3(g, r[1], NT3), _dot3(r[0], g, TN3)))


@jax.custom_vjp
def _nn(a, b):
    return _bdot(a, b, NN3)


_nn.defvjp(lambda a, b: (_bdot(a, b, NN3), (a, b)), lambda r, g: (_bdot(g, r[1], NT3), _bdot(r[0], g, TN3)))


@jax.custom_vjp
def _nt(a, b):
    return _bdot(a, b, NT3)


_nt.defvjp(lambda a, b: (_bdot(a, b, NT3), (a, b)), lambda r, g: (_bdot(g, r[1], NN3), _bdot(g, r[0], TN3)))


@jax.custom_vjp
def _tn(a, b):
    return _bdot(a, b, TN3)


_tn.defvjp(lambda a, b: (_bdot(a, b, TN3), (a, b)), lambda r, g: (_bdot(r[1], g, NT3), _bdot(r[0], g, NN3)))


def _conv_pre(ext, w, rows):
    acc = w[0:1] * ext[5:5 + rows]
    for j in range(1, GDN_CONV):
        acc = acc + w[j:j + 1] * ext[5 + j:5 + j + rows]
    return acc


def _conv_fwd(proj, conv_w, *, name):
    t = proj.shape[0]
    tm, tc = _pick(t, 512), 512
    nb = tm // 8

    def body(u_ref, p_ref, w_ref, o_ref):
        i = pl.program_id(1)
        prev = jnp.where(i > 0, p_ref[...], 0.0)
        ext = jnp.concatenate([prev, u_ref[...]], axis=0)
        o_ref[...] = _silu(_conv_pre(ext, w_ref[...], tm))

    return _pcall(body, name=name, grid=(CONV_CH // tc, t // tm),
                  in_specs=[pl.BlockSpec((tm, tc), lambda j, i: (i, j)),
                            pl.BlockSpec((8, tc), lambda j, i: (jnp.maximum(i * nb - 1, 0), j)),
                            pl.BlockSpec((8, tc), lambda j, i: (0, j))],
                  out_specs=pl.BlockSpec((tm, tc), lambda j, i: (i, j)),
                  out_shape=jax.ShapeDtypeStruct((t, CONV_CH), F32))(proj, proj, conv_w)


def _conv_bwd(proj, conv_w, dy, *, name):
    t = proj.shape[0]
    tm, tc = _pick(t, 512), 512
    nb = tm // 8
    last = t // tm - 1

    def body(u_ref, p_ref, n_ref, dy_ref, dyn_ref, w_ref, du_ref, dw_ref):
        i = pl.program_id(1)
        w = w_ref[...]
        prev = jnp.where(i > 0, p_ref[...], 0.0)
        ext = jnp.concatenate([prev, u_ref[...], n_ref[...]], axis=0)
        c = _conv_pre(ext, w, tm + 8)
        dy_ext = jnp.concatenate([dy_ref[...], jnp.where(i < last, dyn_ref[...], 0.0)], axis=0)
        dc = dy_ext * _dsilu(c)
        du = w[3:4] * dc[0:tm]
        for j in range(GDN_CONV - 1):
            du = du + w[j:j + 1] * dc[3 - j:3 - j + tm]
        du_ref[...] = du.astype(du_ref.dtype)

        @pl.when(i == 0)
        def _():
            dw_ref[...] = jnp.zeros_like(dw_ref)

        for j in range(GDN_CONV):
            dw_ref[j] += (dc[0:tm] * ext[5 + j:5 + j + tm]).reshape(nb, 8, tc).sum(axis=0)

    cur = lambda j, i: (i, j)
    return _pcall(body, name=name, grid=(CONV_CH // tc, t // tm),
                  in_specs=[pl.BlockSpec((tm, tc), cur),
                            pl.BlockSpec((8, tc), lambda j, i: (jnp.maximum(i * nb - 1, 0), j)),
                            pl.BlockSpec((8, tc), lambda j, i: (jnp.minimum((i + 1) * nb, t // 8 - 1), j)),
                            pl.BlockSpec((tm, tc), cur),
                            pl.BlockSpec((8, tc), lambda j, i: (jnp.minimum((i + 1) * nb, t // 8 - 1), j)),
                            pl.BlockSpec((8, tc), lambda j, i: (0, j))],
                  out_specs=[pl.BlockSpec((tm, tc), cur), pl.BlockSpec((GDN_CONV, 8, tc), lambda j, i: (0, 0, j))],
                  out_shape=[jax.ShapeDtypeStruct((t, CONV_CH), BF16), jax.ShapeDtypeStruct((GDN_CONV, 8, CONV_CH), F32)],
                  )(proj, proj, proj, dy, dy, conv_w)


def _gdn_chunk(q_raw, k_raw, v, misc, params, state):
    nh, c = q_raw.shape[0], q_raw.shape[1]
    lane = lax.broadcasted_iota(jnp.int32, misc.shape, 1)
    prow = lax.broadcasted_iota(jnp.int32, params.shape, 0)
    plane = lax.broadcasted_iota(jnp.int32, params.shape, 1)
    heads = lambda pieces: jnp.concatenate([p[None] for p in pieces], axis=0)
    col = lambda at: heads([jnp.sum(jnp.where(lane == at + h, misc, 0.0), axis=1, keepdims=True) for h in range(nh)])
    par = lambda row: heads([jnp.sum(jnp.where((prow == row) & (plane == h), params, 0.0), keepdims=True)
                             for h in range(nh)])
    b_raw, a_raw = col(LANE_B), col(LANE_A)
    a_log, dt_bias = par(0), par(1)
    beta = _sigmoid(b_raw)
    g = -jnp.exp(a_log) * _softplus(a_raw + dt_bias)

    q = q_raw * lax.rsqrt(jnp.sum(q_raw * q_raw, axis=-1, keepdims=True) + EPS) * (HEAD ** -0.5)
    k = k_raw * lax.rsqrt(jnp.sum(k_raw * k_raw, axis=-1, keepdims=True) + EPS)

    ri = lax.broadcasted_iota(jnp.int32, (c, c), 0)
    ci = lax.broadcasted_iota(jnp.int32, (c, c), 1)
    tril, strict = ri >= ci, ri > ci
    gc = _prefix_rows(g)
    gc_col = jnp.broadcast_to(gc, (nh, c, c))
    gc_row = jnp.swapaxes(gc_col, 1, 2)
    decay = jnp.exp(jnp.where(tril, gc_col - gc_row, -1e30))

    kb = k * beta
    vb = v * beta
    a_mat = jnp.where(strict, _nt(kb, k) * decay, 0.0)
    x = -a_mat
    inv = (ri == ci).astype(F32) + x
    for _ in range(5):
        x = _nn_hi(x, x)
        inv = inv + _nn_hi(inv, x)
    u = _nn_hi(inv, vb)
    w = _nn_hi(inv, kb * jnp.exp(gc))
    intra = _nt(q, k) * decay

    v_new = u - _nn(w, state)
    o = _nn(q * jnp.exp(gc), state) + _nn(intra, v_new)
    g_last = jnp.sum(g, axis=1, keepdims=True)
    k_dec = k * jnp.exp(g_last - gc)
    new_state = state * jnp.exp(g_last) + _tn(k_dec, v_new)
    return o, new_state


def _gdn_specs(nc, rev):
    cidx = (lambda n: nc - 1 - n) if rev else (lambda n: n)
    hb = lambda part: pl.BlockSpec((GDN_CHUNK, GDN_QK), lambda n: (cidx(n), part))
    misc = pl.BlockSpec((GDN_CHUNK, HEAD), lambda n: (cidx(n), COL_MISC // HEAD))
    params = pl.BlockSpec((8, HEAD), lambda n: (0, 0))
    hist = pl.BlockSpec((1, GDN_HEADS, HEAD, HEAD), lambda n: (cidx(n), 0, 0, 0))
    return hb, misc, params, hist


def _split_heads(v):
    return jnp.stack([v[:, h * HEAD:(h + 1) * HEAD] for h in range(v.shape[1] // HEAD)])


def _merge_heads(v):
    return jnp.concatenate([v[h] for h in range(v.shape[0])], axis=1)


def _gdn_fwd(qkv, proj, params, *, name):
    t = qkv.shape[0]
    nc = t // GDN_CHUNK
    hb, misc, pspec, hist = _gdn_specs(nc, False)

    def body(q_ref, k_ref, v_ref, m_ref, p_ref, o_ref, hist_ref, s_ref):
        @pl.when(pl.program_id(0) == 0)
        def _():
            s_ref[...] = jnp.zeros_like(s_ref)

        state = s_ref[...]
        hist_ref[0] = state
        o, new_state = _gdn_chunk(_split_heads(q_ref[...]), _split_heads(k_ref[...]), _split_heads(v_ref[...]),
                                  m_ref[...], p_ref[...], state)
        o_ref[...] = _merge_heads(o)
        s_ref[...] = new_state

    return _pcall(body, name=name, grid=(nc,),
                  in_specs=[hb(0), hb(1), hb(2), misc, pspec],
                  out_specs=[hb(0), hist],
                  out_shape=[jax.ShapeDtypeStruct((t, GDN_QK), F32),
                             jax.ShapeDtypeStruct((nc, GDN_HEADS, HEAD, HEAD), F32)],
                  scratch=[pltpu.VMEM((GDN_HEADS, HEAD, HEAD), F32)])(qkv, qkv, qkv, proj, params)


def _gdn_bwd(qkv, proj, params, hist_arr, do, dmisc_in, *, name, carry=None):
    t = qkv.shape[0]
    nc = t // GDN_CHUNK
    hb, misc, pspec, hist = _gdn_specs(nc, True)
    mrow = pl.BlockSpec((GDN_CHUNK, HEAD), lambda n: (nc - 1 - n, 0))

    def body(q_ref, k_ref, v_ref, m_ref, p_ref, hist_ref, do_ref, dmi_ref, dqkv_ref, dm_ref, dp_ref, ds_ref):
        @pl.when(pl.program_id(0) == 0)
        def _():
            ds_ref[...] = jnp.zeros_like(ds_ref)
            dp_ref[...] = jnp.zeros_like(dp_ref)

        _, vjp = jax.vjp(_gdn_chunk, _split_heads(q_ref[...]), _split_heads(k_ref[...]), _split_heads(v_ref[...]),
                         m_ref[...], p_ref[...], hist_ref[0])
        dq, dk, dv, dm, dp, ds = vjp((_split_heads(do_ref[...]), ds_ref[...]))
        dqkv_ref[:, 0:GDN_QK] = _merge_heads(dq)
        dqkv_ref[:, GDN_QK:2 * GDN_QK] = _merge_heads(dk)
        dqkv_ref[:, 2 * GDN_QK:] = _merge_heads(dv)
        ds_ref[...] = ds
        dm_ref[...] = dmi_ref[...] + dm
        dp_ref[...] += dp

    return _pcall(body, name=name, grid=(nc,),
                  in_specs=[hb(0), hb(1), hb(2), misc, pspec, hist, hb(0), mrow],
                  out_specs=[pl.BlockSpec((GDN_CHUNK, CONV_CH), lambda n: (nc - 1 - n, 0)), mrow, pspec],
                  out_shape=[jax.ShapeDtypeStruct((t, CONV_CH), F32), jax.ShapeDtypeStruct((t, HEAD), F32),
                             jax.ShapeDtypeStruct((8, HEAD), F32)],
                  scratch=[pltpu.VMEM((GDN_HEADS, HEAD, HEAD), F32)], carry=carry,
                  )(qkv, qkv, qkv, proj, params, hist_arr, do, dmisc_in)


def _gate_fwd(o_raw, proj, w, *, name):
    t = o_raw.shape[0]
    tm = _rows(t, HEAD)
    zb = COL_Z // HEAD

    def body(o_ref, z_ref, w_ref, out_ref):
        ov = o_ref[...]
        r = lax.rsqrt(jnp.mean(ov * ov, axis=-1, keepdims=True) + EPS)
        out_ref[...] = (ov * r * w_ref[...] * _silu(z_ref[...])).astype(out_ref.dtype)

    blk = pl.BlockSpec((tm, HEAD), lambda i, h: (i, h))
    return _pcall(body, name=name, grid=(t // tm, GDN_HEADS),
                  in_specs=[blk, pl.BlockSpec((tm, HEAD), lambda i, h: (i, zb + h)), pl.BlockSpec((1, HEAD), lambda i, h: (0, 0))],
                  out_specs=blk, out_shape=jax.ShapeDtypeStruct((t, GDN_QK), BF16))(o_raw, proj, w)


def _gate_bwd(o_raw, proj, w, dmixed, *, name):
    t = o_raw.shape[0]
    tm = _rows(t, HEAD)
    zb = COL_Z // HEAD

    def body(o_ref, z_ref, w_ref, dy_ref, do_ref, dz_ref, dw_ref):
        ov, zv, dyv = o_ref[...], z_ref[...], dy_ref[...]
        r = lax.rsqrt(jnp.mean(ov * ov, axis=-1, keepdims=True) + EPS)
        xh = ov * r
        dn = dyv * _silu(zv)
        dz_ref[...] = (dyv * xh * w_ref[...] * _dsilu(zv)).astype(dz_ref.dtype)
        dnw = dn * w_ref[...]
        do_ref[...] = r * (dnw - xh * jnp.mean(dnw * xh, axis=-1, keepdims=True))

        @pl.when((pl.program_id(0) == 0) & (pl.program_id(1) == 0))
        def _():
            dw_ref[...] = jnp.zeros_like(dw_ref)

        dw_ref[...] += (dn * xh).reshape(tm // 8, 8, HEAD).sum(axis=0)

    blk = pl.BlockSpec((tm, HEAD), lambda i, h: (i, h))
    return _pcall(body, name=name, grid=(t // tm, GDN_HEADS),
                  in_specs=[blk, pl.BlockSpec((tm, HEAD), lambda i, h: (i, zb + h)), pl.BlockSpec((1, HEAD), lambda i, h: (0, 0)), blk],
                  out_specs=[blk, blk, pl.BlockSpec((8, HEAD), lambda i, h: (0, 0))],
                  out_shape=[jax.ShapeDtypeStruct((t, GDN_QK), F32), jax.ShapeDtypeStruct((t, GDN_QK), BF16),
                             jax.ShapeDtypeStruct((8, HEAD), F32)])(o_raw, proj, w, dmixed)


def _rope_tables():
    half = QK_ROPE // 2
    inv = ROPE_THETA ** (-jnp.arange(half, dtype=F32) / half)
    zeros = jnp.zeros((HEAD - QK_ROPE,), F32)
    inv_row = jnp.concatenate([inv, inv, zeros])
    sign_row = jnp.concatenate([-jnp.ones((half,), F32), jnp.ones((half,), F32), zeros])
    mask_row = jnp.concatenate([jnp.ones((QK_ROPE,), F32), zeros])
    return jnp.concatenate([inv_row[None], sign_row[None], mask_row[None], jnp.zeros((5, HEAD), F32)], axis=0)


def _rope_cs(pos, tab, *, name):
    t = pos.shape[0]
    tm = _pick(t, 1024)

    def body(pos_ref, tab_ref, o_ref):
        tab = tab_ref[...]
        ang = pos_ref[...] * tab[0:1]
        o_ref[...] = jnp.concatenate([jnp.cos(ang) * tab[2:3], jnp.sin(ang) * tab[1:2]], axis=1)

    return _pcall(body, name=name, grid=(t // tm,),
                  in_specs=[pl.BlockSpec((tm, 1), lambda i: (i, 0)), pl.BlockSpec((8, HEAD), lambda i: (0, 0))],
                  out_specs=pl.BlockSpec((tm, 2 * HEAD), lambda i: (i, 0)),
                  out_shape=jax.ShapeDtypeStruct((t, 2 * HEAD), F32))(pos, tab)


def _rotate(x, cs, sign):
    lane = lax.broadcasted_iota(jnp.int32, x.shape, 1)
    half = QK_ROPE // 2
    partner = jnp.where(lane < half, pltpu.roll(x, HEAD - half, axis=1), pltpu.roll(x, half, axis=1))
    return x * cs[:, :HEAD] + partner * (cs[:, HEAD:] * sign)


def _q_rot(q, cs, *, name, sign, out_dtype=BF16):
    t = q.shape[0]
    tm = _pick(t, 1024)
    scale = (HEAD + QK_ROPE) ** -0.5

    def body(q_ref, cs_ref, o_ref):
        qv = q_ref[...].astype(F32)
        rot = _rotate(qv[:, HEAD:], cs_ref[...], sign)
        o_ref[...] = (jnp.concatenate([qv[:, :HEAD], rot], axis=1) * scale).astype(o_ref.dtype)

    blk = pl.BlockSpec((tm, QHEAD), lambda i, h: (i, h))
    return _pcall(body, name=name, grid=(t // tm, MLA_HEADS),
                  in_specs=[blk, pl.BlockSpec((tm, 2 * HEAD), lambda i, h: (i, 0))],
                  out_specs=blk, out_shape=jax.ShapeDtypeStruct((t, MLA_HEADS * QHEAD), out_dtype))(q, cs)


def _q_up(cqn, wuq_p, cs, *, name):
    t, lora = cqn.shape
    tm = _pick(t, 1024)
    scale = (HEAD + QK_ROPE) ** -0.5

    def body(a_ref, w_ref, cs_ref, o_ref):
        qv = lax.dot_general(a_ref[...], w_ref[...], (NN, ((), ())), preferred_element_type=F32)
        rot = _rotate(qv[:, HEAD:], cs_ref[...], 1.0)
        o_ref[...] = (jnp.concatenate([qv[:, :HEAD], rot], axis=1) * scale).astype(o_ref.dtype)

    return _pcall(body, name=name, grid=(t // tm, MLA_HEADS),
                  in_specs=[pl.BlockSpec((tm, lora), lambda i, h: (i, 0)), pl.BlockSpec((lora, QHEAD), lambda i, h: (0, h)),
                            pl.BlockSpec((tm, 2 * HEAD), lambda i, h: (i, 0))],
                  out_specs=pl.BlockSpec((tm, QHEAD), lambda i, h: (i, h)),
                  out_shape=jax.ShapeDtypeStruct((t, MLA_HEADS * QHEAD), BF16))(cqn, wuq_p, cs)


def _kv_up(ckvn, wukv, proj, cs, *, name):
    t, lora = ckvn.shape
    tm = _pick(t, 1024)

    def body(a_ref, w_ref, m_ref, cs_ref, k_ref, v_ref):
        kvv = lax.dot_general(a_ref[...], w_ref[...], (NN, ((), ())), preferred_element_type=F32)
        misc = m_ref[...]
        lane = lax.broadcasted_iota(jnp.int32, misc.shape, 1)
        rot = _rotate(jnp.where(lane < QK_ROPE, misc, 0.0), cs_ref[...], 1.0)
        k_ref[...] = jnp.concatenate([kvv[:, :HEAD], rot], axis=1).astype(k_ref.dtype)
        v_ref[...] = kvv[:, HEAD:].astype(v_ref.dtype)

    return _pcall(body, name=name, grid=(t // tm, MLA_HEADS),
                  in_specs=[pl.BlockSpec((tm, lora), lambda i, h: (i, 0)), pl.BlockSpec((lora, QHEAD), lambda i, h: (0, h)),
                            pl.BlockSpec((tm, HEAD), lambda i, h: (i, COL_MISC // HEAD)),
                            pl.BlockSpec((tm, 2 * HEAD), lambda i, h: (i, 0))],
                  out_specs=[pl.BlockSpec((tm, QHEAD), lambda i, h: (i, h)), pl.BlockSpec((tm, HEAD), lambda i, h: (i, h))],
                  out_shape=[jax.ShapeDtypeStruct((t, MLA_HEADS * QHEAD), BF16), jax.ShapeDtypeStruct((t, MLA_HEADS * HEAD), BF16)],
                  )(ckvn, wukv, proj, cs)


def _krope_bwd(dkr, cs, *, name):
    t = dkr.shape[0]
    tm = _pick(t, 512)

    def body(d_ref, cs_ref, o_ref):
        d = d_ref[...]
        acc = d[:, :HEAD]
        for h in range(1, MLA_HEADS):
            acc = acc + d[:, h * HEAD:(h + 1) * HEAD]
        o_ref[...] = _rotate(acc, cs_ref[...], -1.0)

    return _pcall(body, name=name, grid=(t // tm,),
                  in_specs=[pl.BlockSpec((tm, MLA_HEADS * HEAD), lambda i: (i, 0)), pl.BlockSpec((tm, 2 * HEAD), lambda i: (i, 0))],
                  out_specs=pl.BlockSpec((tm, HEAD), lambda i: (i, 0)),
                  out_shape=jax.ShapeDtypeStruct((t, HEAD), F32))(dkr, cs)


NEG = -1e30


def _tri(step, counts):
    starts = [sum(counts[:o]) for o in range(len(counts))]
    outer = sum([(step >= s).astype(jnp.int32) for s in starts[1:]], jnp.int32(0))
    start = sum([(step >= starts[o]).astype(jnp.int32) * (starts[o] - starts[o - 1]) for o in range(1, len(counts))], jnp.int32(0))
    return outer, step - start


def _attn_fwd(q, k, v, *, name, tq=1024, tk=1024, carry=None):
    t = q.shape[0]
    tq, tk = _pick(t, tq), _pick(t, tk)
    nq = t // tq
    last_kv = lambda i: (i * tq + tq - 1) // tk
    counts = [last_kv(i) + 1 for i in range(nq)]

    def body(q_ref, k_ref, v_ref, o_ref, lse_ref, m_ref, l_ref, acc_ref):
        i, j = _tri(pl.program_id(1), counts)

        @pl.when(j == 0)
        def _():
            m_ref[...] = jnp.full_like(m_ref, NEG)
            l_ref[...] = jnp.zeros_like(l_ref)
            acc_ref[...] = jnp.zeros_like(acc_ref)

        def step(masked):
            s = lax.dot_general(q_ref[...], k_ref[...], (NT, ((), ())), preferred_element_type=F32)
            if masked:
                qpos = i * tq + lax.broadcasted_iota(jnp.int32, s.shape, 0)
                kpos = j * tk + lax.broadcasted_iota(jnp.int32, s.shape, 1)
                s = jnp.where(kpos <= qpos, s, NEG)
            m_prev = m_ref[...]
            m_new = jnp.maximum(m_prev, jnp.max(s, axis=1, keepdims=True))
            alpha = jnp.exp(m_prev - m_new)
            p = jnp.exp(s - m_new)
            l_ref[...] = alpha * l_ref[...] + jnp.sum(p, axis=1, keepdims=True)
            acc_ref[...] = alpha * acc_ref[...] + lax.dot_general(p.astype(BF16), v_ref[...], (NN, ((), ())),
                                                                  preferred_element_type=F32)
            m_ref[...] = m_new

        crosses = j * tk + tk - 1 > i * tq

        @pl.when(crosses)
        def _():
            step(True)

        @pl.when(jnp.logical_not(crosses))
        def _():
            step(False)

        @pl.when(j == last_kv(i))
        def _():
            o_ref[...] = acc_ref[...] / l_ref[...]
            lse_ref[...] = jnp.broadcast_to(m_ref[...] + jnp.log(l_ref[...]), lse_ref.shape)

    qblk = pl.BlockSpec((tq, QHEAD), lambda h, s: (_tri(s, counts)[0], h))
    oblk = pl.BlockSpec((tq, HEAD), lambda h, s: (_tri(s, counts)[0], h))
    return _pcall(body, name=name, grid=(MLA_HEADS, sum(counts)),
                  in_specs=[qblk, pl.BlockSpec((tk, QHEAD), lambda h, s: (_tri(s, counts)[1], h)),
                            pl.BlockSpec((tk, HEAD), lambda h, s: (_tri(s, counts)[1], h))],
                  out_specs=[oblk, oblk],
                  out_shape=[jax.ShapeDtypeStruct((t, MLA_HEADS * HEAD), F32), jax.ShapeDtypeStruct((t, MLA_HEADS * HEAD), F32)],
                  scratch=[pltpu.VMEM((tq, 1), F32), pltpu.VMEM((tq, 1), F32), pltpu.VMEM((tq, HEAD), F32)],
                  carry=carry)(q, k, v)


def _attn_bwd(q, k, v, do, lse, delta, *, name, tq=1024, tk=1024, carry=None):
    t = q.shape[0]
    tq, tk = _pick(t, tq), _pick(t, tk)
    nq, nk = t // tq, t // tk
    first_q = lambda j: (j * tk) // tq
    counts = [nq - first_q(j) for j in range(nk)]

    def where(step):
        j, off = _tri(step, counts)
        return j, first_q(j) + off

    lanes = lambda col: jnp.tile(col, (1, tk // HEAD))

    def body(q_ref, k_ref, v_ref, do_ref, lse_ref, dl_ref, dq_ref, dkv_ref, dkr_ref, dk_acc, dv_acc):
        j, i = where(pl.program_id(1))

        @pl.when(i == first_q(j))
        def _():
            dk_acc[...] = jnp.zeros_like(dk_acc)
            dv_acc[...] = jnp.zeros_like(dv_acc)

        def step(masked):
            qv, kv_, dov = q_ref[...], k_ref[...], do_ref[...].astype(BF16)
            s = lax.dot_general(qv, kv_, (NT, ((), ())), preferred_element_type=F32)
            p = jnp.exp((s - lanes(lse_ref[...])).astype(BF16))
            if masked:
                qpos = i * tq + lax.broadcasted_iota(jnp.int32, s.shape, 0)
                kpos = j * tk + lax.broadcasted_iota(jnp.int32, s.shape, 1)
                p = jnp.where(kpos <= qpos, p, jnp.zeros_like(p))
            dv_acc[...] += lax.dot_general(p, dov, (TN, ((), ())), preferred_element_type=F32)
            dp = lax.dot_general(dov, v_ref[...], (NT, ((), ())), preferred_element_type=F32)
            ds = p * (dp - lanes(dl_ref[...])).astype(BF16)
            dk_acc[...] += lax.dot_general(ds, qv, (TN, ((), ())), preferred_element_type=F32)
            contrib = lax.dot_general(ds, kv_, (NN, ((), ())), preferred_element_type=F32)
            rows = pl.ds(pl.multiple_of(i * tq, tq), tq)

            @pl.when(j == 0)
            def _():
                dq_ref[rows, :] = contrib

            @pl.when(j > 0)
            def _():
                dq_ref[rows, :] += contrib

        crosses = j * tk + tk - 1 > i * tq

        @pl.when(crosses)
        def _():
            step(True)

        @pl.when(jnp.logical_not(crosses))
        def _():
            step(False)

        @pl.when(i == nq - 1)
        def _():
            dk = dk_acc[...]
            dkv_ref[...] = jnp.concatenate([dk[:, :HEAD], dv_acc[...]], axis=1).astype(dkv_ref.dtype)
            dkr_ref[...] = dk[:, HEAD:]

    qi = lambda h, s: (where(s)[1], h)
    kj = lambda h, s: (where(s)[0], h)
    return _pcall(body, name=name, grid=(MLA_HEADS, sum(counts)),
                  in_specs=[pl.BlockSpec((tq, QHEAD), qi), pl.BlockSpec((tk, QHEAD), kj), pl.BlockSpec((tk, HEAD), kj),
                            pl.BlockSpec((tq, HEAD), qi), pl.BlockSpec((tq, HEAD), qi), pl.BlockSpec((tq, HEAD), qi)],
                  out_specs=[pl.BlockSpec((t, QHEAD), lambda h, s: (0, h)), pl.BlockSpec((tk, QHEAD), kj),
                             pl.BlockSpec((tk, HEAD), kj)],
                  out_shape=[jax.ShapeDtypeStruct((t, MLA_HEADS * QHEAD), F32), jax.ShapeDtypeStruct((t, MLA_HEADS * QHEAD), BF16),
                             jax.ShapeDtypeStruct((t, MLA_HEADS * HEAD), F32)],
                  scratch=[pltpu.VMEM((tk, QHEAD), F32), pltpu.VMEM((tk, HEAD), F32)], carry=carry)(q, k, v, do, lse, delta)


def _ffn_up(h, wgate, wup, *, name, bm=512, bn=FF_WIDE):
    t = h.shape[0]
    bm = _pick(t, bm)

    def body(h_ref, wg_ref, wu_ref, g_ref, u_ref, a_ref):
        hv = h_ref[...]
        g = lax.dot_general(hv, wg_ref[...], (NN, ((), ())), preferred_element_type=F32)
        u = lax.dot_general(hv, wu_ref[...], (NN, ((), ())), preferred_element_type=F32)
        g_ref[...] = g.astype(g_ref.dtype)
        u_ref[...] = u.astype(u_ref.dtype)
        a_ref[...] = (_silu(g) * u).astype(a_ref.dtype)

    w_spec = pl.BlockSpec((D_MODEL, bn), lambda j, i: (0, j))
    o_spec = pl.BlockSpec((bm, bn), lambda j, i: (i, j))
    sds = jax.ShapeDtypeStruct((t, D_FF), BF16)
    return _pcall(body, name=name, grid=(D_FF // bn, t // bm),
                  in_specs=[pl.BlockSpec((bm, D_MODEL), lambda j, i: (i, 0)), w_spec, w_spec],
                  out_specs=[o_spec] * 3, out_shape=[sds] * 3)(h, wgate, wup)


def _ffn_down_dx(dy, wdown, gate, up, *, name, bm=512, bn=FF_WIDE):
    t = dy.shape[0]
    bm = _pick(t, bm)

    def body(dy_ref, w_ref, g_ref, u_ref, dg_ref, du_ref):
        d = lax.dot_general(dy_ref[...].astype(BF16), w_ref[...], (NT, ((), ())), preferred_element_type=F32)
        g = g_ref[...].astype(F32)
        dg_ref[...] = (d * u_ref[...].astype(F32) * _dsilu(g)).astype(dg_ref.dtype)
        du_ref[...] = (d * _silu(g)).astype(du_ref.dtype)

    o_spec = pl.BlockSpec((bm, bn), lambda j, i: (i, j))
    sds = jax.ShapeDtypeStruct((t, D_FF), BF16)
    return _pcall(body, name=name, grid=(D_FF // bn, t // bm),
                  in_specs=[pl.BlockSpec((bm, D_MODEL), lambda j, i: (i, 0)), pl.BlockSpec((bn, D_MODEL), lambda j, i: (j, 0)),
                            o_spec, o_spec],
                  out_specs=[o_spec, o_spec], out_shape=[sds, sds])(dy, wdown, gate, up)


def _loss_bwd(x2, w, target, *, name):
    t = x2.shape[0]
    tm = _rows(t, D_MODEL)

    def body(x_ref, w_ref, t_ref, dx_ref, dw_ref, l_ref):
        xv, wv = x_ref[...], w_ref[...]
        r = lax.rsqrt(jnp.mean(xv * xv, axis=-1, keepdims=True) + EPS)
        xh = xv * r
        err = xh * wv - t_ref[...]
        dy = err * (1.0 / D_MODEL)
        dyw = dy * wv
        dx_ref[...] = r * (dyw - xh * jnp.mean(dyw * xh, axis=-1, keepdims=True))

        @pl.when(pl.program_id(0) == 0)
        def _():
            dw_ref[...] = jnp.zeros_like(dw_ref)
            l_ref[...] = jnp.zeros_like(l_ref)

        dw_ref[...] += (dy * xh).reshape(tm // 8, 8, D_MODEL).sum(axis=0)
        sq = (err * err).reshape(tm // 8, 8, D_MODEL).sum(axis=0)
        part = sq[:, :HEAD]
        for c in range(1, D_MODEL // HEAD):
            part = part + sq[:, c * HEAD:(c + 1) * HEAD]
        l_ref[...] += part * (0.5 / D_MODEL)

    row = pl.BlockSpec((tm, D_MODEL), lambda i: (i, 0))
    return _pcall(body, name=name, grid=(t // tm,),
                  in_specs=[row, pl.BlockSpec((1, D_MODEL), lambda i: (0, 0)), row],
                  out_specs=[row, pl.BlockSpec((8, D_MODEL), lambda i: (0, 0)), pl.BlockSpec((8, HEAD), lambda i: (0, 0))],
                  out_shape=[jax.ShapeDtypeStruct((t, D_MODEL), F32), jax.ShapeDtypeStruct((8, D_MODEL), F32),
                             jax.ShapeDtypeStruct((8, HEAD), F32)])(x2, w, target)


def _unshard_cols(g):
    return jnp.transpose(g, (1, 0, 2)).reshape(g.shape[1], N_DEV * g.shape[2])


def _shard_cols(w):
    return jnp.transpose(w.reshape(w.shape[0], N_DEV, w.shape[1] // N_DEV), (1, 0, 2))


_WIN_ORDER = ((0, 4096), (4112, 5136), (5136, 5200), (4096, 4112))
_WIN_SHARD = IN_WIDTH // N_DEV


def _win_pieces():
    out, pos = [], 0
    for a, b in _WIN_ORDER:
        c = a
        while c < b:
            dev, off = divmod(c, _WIN_SHARD)
            width = min(b, (dev + 1) * _WIN_SHARD) - c
            out.append((dev, off, width, pos))
            c, pos = c + width, pos + width
    return out


def _win_gathered_to_padded(g):
    pieces = [g[dev][:, off:off + width] for dev, off, width, _ in _win_pieces()]
    return jnp.concatenate(pieces + [jnp.zeros((g.shape[1], PROJ_W - IN_WIDTH), g.dtype)], axis=1)


def _win_padded_to_shards(d):
    shards = []
    for dev in range(N_DEV):
        mine = sorted((off, width, pos) for dv, off, width, pos in _win_pieces() if dv == dev)
        shards.append(jnp.concatenate([d[:, pos:pos + width] for _, width, pos in mine], axis=1))
    return jnp.stack(shards)


def _wuq_to_padded(w):
    w3 = w.reshape(w.shape[0], MLA_HEADS, HEAD + QK_ROPE)
    return jnp.pad(w3, ((0, 0), (0, 0), (0, QHEAD - HEAD - QK_ROPE))).reshape(w.shape[0], MLA_HEADS * QHEAD)


def _wuq_from_padded(d):
    return d.reshape(d.shape[0], MLA_HEADS, QHEAD)[:, :, :HEAD + QK_ROPE].reshape(d.shape[0], MLA_HEADS * (HEAD + QK_ROPE))


def _late_weights(g_out, g_gate, g_up, g_down):
    return g_out.reshape(D_MODEL, D_MODEL), _unshard_cols(g_gate), _unshard_cols(g_up), g_down.reshape(D_FF, D_MODEL)


def _local_step(x, pos, target, win_p, wuq_p, wukv, late, conv_w, small, exchange):
    cs = _rope_cs(pos, _rope_tables(), name="rope_cs")
    if not exchange:
        wout, wgate, wup, wdown = late
    h1 = _rms_fwd(x, small["attn_norm_w"], name="rms1_fwd", width=D_MODEL)
    proj = _mm(h1, win_p, name="mm_in", bn=768)
    qkv = _conv_fwd(proj, conv_w, name="conv_fwd")
    o_gdn_raw, hist = _gdn_fwd(qkv, proj, small["gdn_params"], name="gdn_fwd")
    o_gdn = _gate_fwd(o_gdn_raw, proj, small["gdn_norm_w"], name="gate_fwd")
    cqn = _rms_fwd(proj, small["q_norm_w"], name="rmsq_fwd", width=Q_LORA, col0=COL_CQ)
    ckvn = _rms_fwd(proj, small["kv_norm_w"], name="rmskv_fwd", width=KV_LORA, col0=COL_CKV)
    q_full = _q_up(cqn, wuq_p, cs, name="q_up")
    k_full, v_b = _kv_up(ckvn, wukv, proj, cs, name="kv_up")
    if exchange:
        (o_mla_raw, lse), gathered = _attn_fwd(q_full, k_full, v_b, name="attn_fwd", carry=_Gather(late))
        wout, wgate, wup, wdown = _late_weights(*gathered)
    else:
        o_mla_raw, lse = _attn_fwd(q_full, k_full, v_b, name="attn_fwd")
    o_mla = _rms_fwd(o_mla_raw, small["mla_out_norm_w"], name="rmso_fwd", width=HEAD, heads=MLA_HEADS)
    mixed = jnp.concatenate([o_gdn, o_mla], axis=1)
    x1 = _mm(mixed, wout, name="mm_out", res=x)
    h2 = _rms_fwd(x1, small["ffn_norm_w"], name="rms2_fwd", width=D_MODEL)
    gate, up, act = _ffn_up(h2, wgate, wup, name="ffn_up")
    x2 = _mm(act, wdown, name="mm_down", res=x1, bk=FF_WIDE)
    dx2, dw_final, loss_part = _loss_bwd(x2, small["final_norm_w"], target, name="loss_bwd")
    dgate, dup = _ffn_down_dx(dx2, wdown, gate, up, name="ffn_down_dx")
    d_wdown = _mm(act, dx2, name="mm_down_dw", ta=True, out_dtype=BF16, bm=FF_WIDE)
    dh2 = _mm(dgate, wgate, name="mm_gate_dx", tb=True, bk=FF_WIDE)
    dh2 = _mm(dup, wup, name="mm_up_dx", tb=True, res=dh2, bk=FF_WIDE)
    d_wgate = _mm(h2, dgate, name="mm_gate_dw", ta=True, out_dtype=BF16, bn=FF_WIDE)
    d_wup = _mm(h2, dup, name="mm_up_dw", ta=True, out_dtype=BF16, bn=FF_WIDE)
    dx1, dw_ffn = _rms_bwd(x1, small["ffn_norm_w"], dh2, name="rms2_bwd", width=D_MODEL, res=dx2)
    dmixed = _mm(dx1, wout, name="mm_out_dx", tb=True)
    d_wout = _mm(mixed, dx1, name="mm_out_dw", ta=True, out_dtype=BF16)
    do_mla, dw_mla_out, delta = _rms_bwd(o_mla_raw, small["mla_out_norm_w"], dmixed, name="rmso_bwd", width=HEAD,
                                         heads=MLA_HEADS, dcol0=GDN_QK, with_delta=True, out_dtype=BF16)
    if exchange:
        send = [d_wdown.reshape(N_DEV, D_FF // N_DEV, D_MODEL), _shard_cols(d_wgate), _shard_cols(d_wup)]
        (dq_full, dkv, dkr_h), (r_down, r_gate, r_up) = _attn_bwd(q_full, k_full, v_b, do_mla, lse, delta, name="attn_bwd",
                                                                  carry=_Exchange(send, [False] * 3))
    else:
        dq_full, dkv, dkr_h = _attn_bwd(q_full, k_full, v_b, do_mla, lse, delta, name="attn_bwd")
    dq_pre = _q_rot(dq_full, cs, name="q_rot_bwd", sign=-1.0)
    dmisc_kr = _krope_bwd(dkr_h, cs, name="krope_bwd")
    dcqn = _mm(dq_pre, wuq_p, name="mm_uq_dx", tb=True)
    d_wuq = _mm(cqn, dq_pre, name="mm_uq_dw", ta=True, out_dtype=BF16)
    dckvn = _mm(dkv, wukv, name="mm_ukv_dx", tb=True)
    d_wukv = _mm(ckvn, dkv, name="mm_ukv_dw", ta=True, out_dtype=BF16)
    dcq, dw_qn = _rms_bwd(proj, small["q_norm_w"], dcqn, name="rmsq_bwd", width=Q_LORA, col0=COL_CQ, out_dtype=BF16)
    dckv, dw_kvn = _rms_bwd(proj, small["kv_norm_w"], dckvn, name="rmskv_bwd", width=KV_LORA, col0=COL_CKV, out_dtype=BF16)
    do_gdn, dz, dw_gdn = _gate_bwd(o_gdn_raw, proj, small["gdn_norm_w"], dmixed, name="gate_bwd")
    if exchange:
        send = [d_wout.reshape(N_DEV, D_MODEL // N_DEV, D_MODEL), _shard_cols(_wuq_from_padded(d_wuq)), _shard_cols(d_wukv)]
        (dqkv, dmisc, d_params), (r_out, r_uq, r_ukv) = _gdn_bwd(
            qkv, proj, small["gdn_params"], hist, do_gdn, dmisc_kr, name="gdn_bwd", carry=_Exchange(send, [False] * 3))
    else:
        dqkv, dmisc, d_params = _gdn_bwd(qkv, proj, small["gdn_params"], hist, do_gdn, dmisc_kr, name="gdn_bwd")
    dqkv_pre, dconv = _conv_bwd(proj, conv_w, dqkv, name="conv_bwd")
    dproj = jnp.concatenate([dqkv_pre, dz, dcq, dckv, dmisc.astype(BF16), jnp.zeros((x.shape[0], PROJ_W - COL_MISC - HEAD), BF16)], axis=1)
    d_win = _mm(h1, dproj, name="mm_in_dw", ta=True, out_dtype=BF16, bn=768)
    if exchange:
        dh1, (r_in,) = _mm(dproj, win_p, name="mm_in_dx", tb=True, bk=768,
                           carry=_Exchange([_win_padded_to_shards(d_win)], [False]))
        d_win = r_in
    else:
        dh1 = _mm(dproj, win_p, name="mm_in_dx", tb=True, bk=768)
    dx, dw_attn = _rms_bwd(x, small["attn_norm_w"], dh1, name="rms1_bwd", width=D_MODEL, res=dx1)

    if exchange:
        big = {"w_in": d_win, "w_uq": r_uq, "w_ukv": r_ukv, "w_out": r_out, "w_gate": r_gate, "w_up": r_up, "w_down": r_down}
    else:
        big = {"w_in": d_win, "w_uq": d_wuq, "w_ukv": d_wukv, "w_out": d_wout, "w_gate": d_wgate, "w_up": d_wup,
               "w_down": d_wdown}
    sm = {"attn_norm_w": dw_attn, "ffn_norm_w": dw_ffn, "final_norm_w": dw_final, "q_norm_w": dw_qn, "kv_norm_w": dw_kvn,
          "gdn_norm_w": dw_gdn, "mla_out_norm_w": dw_mla_out, "gdn_params": d_params, "conv_w": dconv, "loss": loss_part}
    return dx, big, sm


def _exchange(ex, *, name):
    def body(*refs):
        ins, outs, sems = refs[:ex.n], refs[ex.n:2 * ex.n], refs[2 * ex.n:]
        ex.start(ins, outs, sems)
        ex.forward(ins, outs, sems)
        ex.finish(ins, outs, sems)

    any_spec = pl.BlockSpec(memory_space=pl.ANY)
    return pl.pallas_call(body, name=name, in_specs=[any_spec] * ex.n, out_specs=[any_spec] * ex.n,
                          out_shape=ex.out_shape(), scratch_shapes=ex.sems())(*ex.arrays)


def _adamw_math(g, w, m, v):
    m = ADAM_B1 * m + (1.0 - ADAM_B1) * g
    v = ADAM_B2 * v + (1.0 - ADAM_B2) * (g * g)
    m_hat = m / (1.0 - ADAM_B1 ** ADAM_STEP)
    v_hat = v / (1.0 - ADAM_B2 ** ADAM_STEP)
    delta = -ADAM_LR * (m_hat / (jnp.sqrt(v_hat) + ADAM_EPS) + ADAM_WD * w)
    return delta, m, v


def _adamw(parts, w, m, v, *, name):
    npart, r, c = parts.shape
    tr = r if r * c * 4 <= (1 << 20) else _rows(r, c, 1 << 20)

    def body(p_ref, w_ref, m_ref, v_ref, g_ref, d_ref, nm_ref, nv_ref):
        g = p_ref[0].astype(F32)
        for s in range(1, npart):
            g = g + p_ref[s].astype(F32)
        g_ref[...] = g
        d_ref[...], nm_ref[...], nv_ref[...] = _adamw_math(g, w_ref[...], m_ref[...], v_ref[...])

    blk = pl.BlockSpec((tr, c), lambda i: (i, 0))
    sds = jax.ShapeDtypeStruct((r, c), F32)
    return _pcall(body, name=name, grid=(r // tr,),
                  in_specs=[pl.BlockSpec((npart, tr, c), lambda i: (0, i, 0)), blk, blk, blk],
                  out_specs=[blk] * 4, out_shape=[sds] * 4)(parts, w, m, v)


def _sum_parts(parts, *, name):
    npart, r, c = parts.shape

    def body(p_ref, o_ref):
        g = p_ref[0]
        for s in range(1, npart):
            g = g + p_ref[s]
        o_ref[...] = g

    return _pcall(body, name=name, grid=(1,), in_specs=[pl.BlockSpec((npart, r, c), lambda i: (0, 0, 0))],
                  out_specs=pl.BlockSpec((r, c), lambda i: (0, 0)), out_shape=jax.ShapeDtypeStruct((r, c), F32))(parts)


_SMALL = (("attn_norm_w", D_MODEL), ("ffn_norm_w", D_MODEL), ("final_norm_w", D_MODEL), ("q_norm_w", Q_LORA),
          ("kv_norm_w", KV_LORA), ("gdn_norm_w", HEAD), ("mla_out_norm_w", HEAD), ("a_log", HEAD), ("dt_bias", HEAD))
_SMALL_ROWS = sum(n for _, n in _SMALL) // HEAD
_CONV_ROWS = GDN_CONV * CONV_CH // HEAD
_PACK_ROWS = 160


def _pad_lanes(v, n):
    v = v.reshape(-1)
    return jnp.concatenate([v, jnp.zeros((n - v.shape[0],), v.dtype)])


def kernel(x, positions, attn_norm_w, w_in, conv_w, a_log, dt_bias, gdn_norm_w, q_norm_w, w_uq, kv_norm_w, w_ukv, mla_out_norm_w, w_out, ffn_norm_w, w_gate, w_up, w_down, final_norm_w, loss_target, m_attn_norm_w, m_w_in, m_conv_w, m_a_log, m_dt_bias, m_gdn_norm_w, m_q_norm_w, m_w_uq, m_kv_norm_w, m_w_ukv, m_mla_out_norm_w, m_w_out, m_ffn_norm_w, m_w_gate, m_w_up, m_w_down, m_final_norm_w, v_attn_norm_w, v_w_in, v_conv_w, v_a_log, v_dt_bias, v_gdn_norm_w, v_q_norm_w, v_w_uq, v_kv_norm_w, v_w_ukv, v_mla_out_norm_w, v_w_out, v_ffn_norm_w, v_w_gate, v_w_up, v_w_down, v_final_norm_w):
    t = x.shape[1]
    me = 4 * lax.axis_index("x") + 2 * lax.axis_index("y") + lax.axis_index("c")
    weights = dict(attn_norm_w=attn_norm_w, w_in=w_in, conv_w=conv_w, a_log=a_log, dt_bias=dt_bias, gdn_norm_w=gdn_norm_w,
                   q_norm_w=q_norm_w, w_uq=w_uq, kv_norm_w=kv_norm_w, w_ukv=w_ukv, mla_out_norm_w=mla_out_norm_w, w_out=w_out,
                   ffn_norm_w=ffn_norm_w, w_gate=w_gate, w_up=w_up, w_down=w_down, final_norm_w=final_norm_w)
    mom_m = dict(attn_norm_w=m_attn_norm_w, w_in=m_w_in, conv_w=m_conv_w, a_log=m_a_log, dt_bias=m_dt_bias, gdn_norm_w=m_gdn_norm_w,
                 q_norm_w=m_q_norm_w, w_uq=m_w_uq, kv_norm_w=m_kv_norm_w, w_ukv=m_w_ukv, mla_out_norm_w=m_mla_out_norm_w,
                 w_out=m_w_out, ffn_norm_w=m_ffn_norm_w, w_gate=m_w_gate, w_up=m_w_up, w_down=m_w_down, final_norm_w=m_final_norm_w)
    mom_v = dict(attn_norm_w=v_attn_norm_w, w_in=v_w_in, conv_w=v_conv_w, a_log=v_a_log, dt_bias=v_dt_bias, gdn_norm_w=v_gdn_norm_w,
                 q_norm_w=v_q_norm_w, w_uq=v_w_uq, kv_norm_w=v_kv_norm_w, w_ukv=v_w_ukv, mla_out_norm_w=v_mla_out_norm_w,
                 w_out=v_w_out, ffn_norm_w=v_ffn_norm_w, w_gate=v_w_gate, w_up=v_w_up, w_down=v_w_down, final_norm_w=v_final_norm_w)
    big_names = ("w_in", "w_uq", "w_ukv", "w_out", "w_gate", "w_up", "w_down")

    shard = {n: weights[n][0].astype(BF16) for n in big_names}
    g_in, g_uq, g_ukv, g_conv = _exchange(_Gather([shard["w_in"], shard["w_uq"], shard["w_ukv"], weights["conv_w"][0]]),
                                          name="gather_weights")
    win_p = _win_gathered_to_padded(g_in)
    wuq_p = _wuq_to_padded(_unshard_cols(g_uq))
    wukv = _unshard_cols(g_ukv)
    late = [shard["w_out"], shard["w_gate"], shard["w_up"], shard["w_down"]]
    conv_full = jnp.concatenate([_unshard_cols(g_conv), jnp.zeros((8 - GDN_CONV, CONV_CH), F32)], axis=0)

    gdn_params = jnp.concatenate([_pad_lanes(a_log, HEAD)[None], _pad_lanes(dt_bias, HEAD)[None], jnp.zeros((6, HEAD), F32)], axis=0)
    small = {n: weights[n].reshape(1, -1) for n in ("attn_norm_w", "ffn_norm_w", "final_norm_w", "q_norm_w", "kv_norm_w",
                                                    "gdn_norm_w", "mla_out_norm_w")}
    small["gdn_params"] = gdn_params

    dx, big, sm = _local_step(x[0], positions.reshape(t, 1).astype(F32), loss_target[0], win_p, wuq_p, wukv, late,
                              conv_full, small, True)

    rows8 = lambda name: jnp.sum(sm[name], axis=0)
    pieces = [rows8(n) for n, _ in _SMALL[:7]]
    pieces += [_pad_lanes(jnp.sum(sm["gdn_params"][0:1], axis=0), HEAD), _pad_lanes(jnp.sum(sm["gdn_params"][1:2], axis=0), HEAD)]
    pieces.append(jnp.sum(sm["conv_w"], axis=1).reshape(-1))
    pieces.append(_pad_lanes(jnp.sum(sm["loss"]).reshape(1), HEAD))
    packed = _pad_lanes(jnp.concatenate(pieces), _PACK_ROWS * HEAD).reshape(_PACK_ROWS, HEAD)
    (r_small,) = _exchange(_Exchange([packed], [True]), name="exchange_small")

    outs_g, outs_d, outs_m, outs_v = {}, {}, {}, {}
    for name in big_names:
        g, d, nm, nv = _adamw(big[name], weights[name][0], mom_m[name][0], mom_v[name][0], name="adamw_" + name)
        outs_g[name], outs_d[name], outs_m[name], outs_v[name] = g[None], d[None], nm[None], nv[None]

    total = _sum_parts(r_small, name="sum_small")
    flat = total.reshape(-1)
    loss = flat[(_SMALL_ROWS + _CONV_ROWS) * HEAD]
    g_small, off = {}, 0
    for n, size in _SMALL:
        g_small[n] = flat[off:off + size]
        off += size
    g_conv_full = flat[off:off + GDN_CONV * CONV_CH].reshape(GDN_CONV, CONV_CH)
    g_small["conv_w"] = lax.dynamic_slice(g_conv_full, (0, me * (CONV_CH // N_DEV)), (GDN_CONV, CONV_CH // N_DEV)).reshape(-1)
    order = [n for n, _ in _SMALL] + ["conv_w"]
    sizes = dict(_SMALL)
    sizes["conv_w"] = GDN_CONV * CONV_CH // N_DEV
    true_size = {n: weights[n].size for n in order}

    def pack(d):
        return jnp.concatenate([_pad_lanes(d[n], sizes[n]) for n in order]).reshape(1, -1, HEAD)

    g2, d2, m2, v2 = _adamw(pack(g_small), pack(weights)[0], pack(mom_m)[0], pack(mom_v)[0], name="adamw_small")
    off = 0
    for n in order:
        for src, dst in ((g2, outs_g), (d2, outs_d), (m2, outs_m), (v2, outs_v)):
            dst[n] = src.reshape(-1)[off:off + true_size[n]].reshape(weights[n].shape)
        off += sizes[n]

    names = ("attn_norm_w", "w_in", "conv_w", "a_log", "dt_bias", "gdn_norm_w", "q_norm_w", "w_uq", "kv_norm_w", "w_ukv",
             "mla_out_norm_w", "w_out", "ffn_norm_w", "w_gate", "w_up", "w_down", "final_norm_w")
    return (loss, dx[None], *[outs_g[n] for n in names], *[outs_d[n] for n in names], *[outs_m[n] for n in names],
            *[outs_v[n] for n in names])
```

```python
import functools
import math

import jax
import jax.numpy as jnp
from jax import lax
from jax.experimental import pallas as pl
from jax.experimental.pallas import tpu as pltpu

F32 = jnp.float32
BF16 = jnp.bfloat16

D_MODEL = 2048
GDN_HEADS = 8
HEAD = 128
GDN_CONV = 4
GDN_CHUNK = 64
GDN_QK = GDN_HEADS * HEAD
CONV_CH = 3 * GDN_QK
MLA_HEADS = 8
QK_ROPE = 64
Q_LORA = 512
KV_LORA = 512
ROPE_THETA = 10000.0
D_FF = 5632
EPS = 1e-6
IN_WIDTH = 5200
ADAM_LR, ADAM_B1, ADAM_B2, ADAM_EPS, ADAM_WD, ADAM_STEP = 0.001, 0.9, 0.999, 1e-08, 0.01, 10

PROJ_W = 5376
COL_Z = 3072
COL_CQ = 4096
COL_CKV = 4608
COL_MISC = 5120
LANE_B = 64
LANE_A = 72
QHEAD = 256
FF_WIDE = D_FF // 4
N_DEV = 8
MESH = pl.DeviceIdType.MESH
VMEM_LIMIT_MB = 48

NN = ((1,), (0,))
NT = ((1,), (1,))
TN = ((0,), (0,))


def _my_place():
    x, y, c = lax.axis_index("x"), lax.axis_index("y"), lax.axis_index("c")
    return x, y, c, 4 * x + 2 * y + c


def _peer(x, y, c, p):
    px, py, pc = x ^ ((p >> 2) & 1), y ^ ((p >> 1) & 1), c ^ (p & 1)
    return (px, py, pc), 4 * px + 2 * py + pc


class _Exchange:
    def __init__(self, arrays, gather):
        self.arrays, self.gather, self.n = list(arrays), list(gather), len(arrays)

    def out_shape(self):
        return [jax.ShapeDtypeStruct(((N_DEV,) + a.shape) if g else a.shape, a.dtype)
                for a, g in zip(self.arrays, self.gather)]

    def sems(self):
        return [pltpu.SemaphoreType.DMA((self.n * (N_DEV - 1),)), pltpu.SemaphoreType.DMA((self.n * (N_DEV - 1),)),
                pltpu.SemaphoreType.DMA((self.n,))]

    def _copies(self, ins, outs, sems):
        send_sems, recv_sems, local_sems = sems
        x, y, c, me = _my_place()
        local = [pltpu.make_async_copy(ins[k] if self.gather[k] else ins[k].at[me], outs[k].at[me], local_sems.at[k])
                 for k in range(self.n)]
        sent, received = [], []
        for p in range(1, N_DEV):
            place, num = _peer(x, y, c, p)
            for k in range(self.n):
                src = ins[k] if self.gather[k] else ins[k].at[num]
                idx = k * (N_DEV - 1) + p - 1
                mk = lambda dst: pltpu.make_async_remote_copy(src_ref=src, dst_ref=dst, send_sem=send_sems.at[idx],
                                                              recv_sem=recv_sems.at[idx], device_id=place, device_id_type=MESH)
                sent.append(mk(outs[k].at[me]))
                received.append(mk(outs[k].at[num]))
        return local, sent, received

    def start(self, ins, outs, sems):
        local, sent, _ = self._copies(ins, outs, sems)
        for cp in local + sent:
            cp.start()

    def forward(self, ins, outs, sems):
        pass

    def finish(self, ins, outs, sems):
        local, sent, received = self._copies(ins, outs, sems)
        for cp in received:
            cp.wait_recv()
        for cp in sent:
            cp.wait_send()
        for cp in local:
            cp.wait()


class _Gather:
    def __init__(self, arrays):
        self.arrays, self.n = list(arrays), len(arrays)

    def out_shape(self):
        return [jax.ShapeDtypeStruct((N_DEV,) + a.shape, a.dtype) for a in self.arrays]

    def sems(self):
        return [pltpu.SemaphoreType.DMA((self.n * (N_DEV - 1),)), pltpu.SemaphoreType.DMA((self.n * (N_DEV - 1),)),
                pltpu.SemaphoreType.DMA((self.n,))]

    def _plan(self, ins, outs, sems):
        send_sems, recv_sems, local_sems = sems
        x, y, c, me = _my_place()
        sibling = (x, y, 1 - c)
        chips = [(1 - x, y), (x, 1 - y), (1 - x, 1 - y)]
        num = lambda px, py, pc: 4 * px + 2 * py + pc

        def copy(k, i, block, to, src=None):
            slot = outs[k].at[num(*block)]
            return pltpu.make_async_remote_copy(src_ref=slot if src is None else src, dst_ref=slot,
                                                send_sem=send_sems.at[k * (N_DEV - 1) + i],
                                                recv_sem=recv_sems.at[k * (N_DEV - 1) + i],
                                                device_id=to, device_id_type=MESH)

        local = [pltpu.make_async_copy(ins[k], outs[k].at[me], local_sems.at[k]) for k in range(self.n)]
        return (x, y, c), sibling, chips, copy, local

    def start(self, ins, outs, sems):
        me, sibling, chips, copy, local = self._plan(ins, outs, sems)
        for cp in local:
            cp.start()
        for k in range(self.n):
            copy(k, 0, me, sibling, src=ins[k]).start()
            for j, chip in enumerate(chips):
                copy(k, 1 + j, me, (*chip, me[2]), src=ins[k]).start()

    def forward(self, ins, outs, sems):
        me, sibling, chips, copy, _ = self._plan(ins, outs, sems)
        for j, chip in enumerate(chips):
            for k in range(self.n):
                copy(k, 1 + j, (*chip, me[2]), me).wait_recv()
                copy(k, 4 + j, (*chip, me[2]), sibling).start()

    def finish(self, ins, outs, sems):
        me, sibling, chips, copy, local = self._plan(ins, outs, sems)
        for k in range(self.n):
            copy(k, 0, sibling, me).wait_recv()
            for j, chip in enumerate(chips):
                copy(k, 4 + j, (*chip, 1 - me[2]), me).wait_recv()
        for k in range(self.n):
            copy(k, 0, me, sibling, src=ins[k]).wait_send()
            for j, chip in enumerate(chips):
                copy(k, 1 + j, me, (*chip, me[2]), src=ins[k]).wait_send()
                copy(k, 4 + j, (*chip, me[2]), sibling).wait_send()
        for cp in local:
            cp.wait()


def _pcall(body, *, name, grid, in_specs, out_specs, out_shape, scratch=(), carry=None):
    params = pltpu.CompilerParams(dimension_semantics=("arbitrary",) * len(grid), vmem_limit_bytes=VMEM_LIMIT_MB << 20)
    if carry is None:
        return pl.pallas_call(body, name=name, grid=grid, in_specs=in_specs, out_specs=out_specs, out_shape=out_shape,
                              scratch_shapes=list(scratch), compiler_params=params)
    single = not isinstance(out_specs, (list, tuple))
    out_specs = [out_specs] if single else list(out_specs)
    out_shape = [out_shape] if single else list(out_shape)
    n_in, n_out, n_scr, na = len(in_specs), len(out_specs), len(scratch), carry.n

    def wrapped(*refs):
        ins, cin = refs[:n_in], refs[n_in:n_in + na]
        outs, cout = refs[n_in + na:n_in + na + n_out], refs[n_in + na + n_out:n_in + 2 * na + n_out]
        scr, sems = refs[n_in + 2 * na + n_out:n_in + 2 * na + n_out + n_scr], refs[n_in + 2 * na + n_out + n_scr:]
        total = math.prod(grid)
        step = functools.reduce(lambda a, d: a * grid[d] + pl.program_id(d), range(len(grid)), 0)

        @pl.when(step == 0)
        def _():
            carry.start(cin, cout, sems)

        body(*ins, *outs, *scr)

        @pl.when(step == min(total * 7 // 8, total - 1))
        def _():
            carry.forward(cin, cout, sems)

        @pl.when(step == total - 1)
        def _():
            carry.finish(cin, cout, sems)

    any_spec = pl.BlockSpec(memory_space=pl.ANY)
    call = pl.pallas_call(wrapped, name=name, grid=grid, in_specs=list(in_specs) + [any_spec] * na,
                          out_specs=out_specs + [any_spec] * na, out_shape=out_shape + carry.out_shape(),
                          scratch_shapes=list(scratch) + carry.sems(), compiler_params=params)

    def run(*args):
        res = call(*args, *carry.arrays)
        main = res[0] if single else list(res[:n_out])
        return main, list(res[n_out:])

    return run


def _pick(dim, pref):
    if dim <= pref:
        return dim
    c = pref
    while c >= 128:
        if dim % c == 0 and c % 128 == 0:
            return c
        c -= 128
    return dim


def _rows(t, width, target_bytes=2 << 20):
    r = max(8, min(t, target_bytes // (4 * width)))
    r = 1 << (r.bit_length() - 1)
    while t % r:
        r //= 2
    return r


MM_FULL_K = 2048


def _mm(a, b, *, name, ta=False, tb=False, res=None, out_dtype=F32, bm=1024, bn=1024, bk=1024, carry=None, pair=None):
    m, k = (a.shape[1], a.shape[0]) if ta else a.shape
    n = b.shape[0] if tb else b.shape[1]
    assert (b.shape[1] if tb else b.shape[0]) == k
    bm, bn, bk = _pick(m, bm), _pick(n, bn), (k if k <= MM_FULL_K else _pick(k, bk))
    nk = k // bk
    dims = (((0,) if ta else (1,), (1,) if tb else (0,)), ((), ()))
    n_ab = 2 if pair is None else 4

    def body(*refs):
        a_ref, b_ref = refs[:2]
        r_ref = refs[n_ab] if res is not None else None
        o_ref = refs[n_ab + 1] if res is not None else refs[n_ab]
        part = lax.dot_general(a_ref[...].astype(BF16), b_ref[...].astype(BF16), dims, preferred_element_type=F32)
        if pair is not None:
            part = part + lax.dot_general(refs[2][...].astype(BF16), refs[3][...].astype(BF16), dims,
                                          preferred_element_type=F32)

        def finish(out):
            if res is not None:
                out = out + r_ref[...]
            o_ref[...] = out.astype(o_ref.dtype)

        if nk == 1:
            finish(part)
            return
        acc_ref = refs[-1]
        kk = pl.program_id(2)

        @pl.when(kk == 0)
        def _():
            acc_ref[...] = part

        @pl.when((kk > 0) & (kk < nk - 1))
        def _():
            acc_ref[...] += part

        @pl.when(kk == nk - 1)
        def _():
            finish(acc_ref[...] + part)

    a_spec = pl.BlockSpec((bk, bm), lambda i, j, kk: (kk, i)) if ta else pl.BlockSpec((bm, bk), lambda i, j, kk: (i, kk))
    b_spec = pl.BlockSpec((bn, bk), lambda i, j, kk: (j, kk)) if tb else pl.BlockSpec((bk, bn), lambda i, j, kk: (kk, j))
    o_spec = pl.BlockSpec((bm, bn), lambda i, j, kk: (i, j))
    ins, specs = [a, b], [a_spec, b_spec]
    if pair is not None:
        assert pair[0].shape == a.shape and pair[1].shape == b.shape
        ins += list(pair)
        specs += [a_spec, b_spec]
    if res is not None:
        ins.append(res)
        specs.append(o_spec)
    return _pcall(body, name=name, grid=(m // bm, n // bn, nk), in_specs=specs, out_specs=o_spec,
                  out_shape=jax.ShapeDtypeStruct((m, n), out_dtype),
                  scratch=[pltpu.VMEM((bm, bn), F32)] if nk > 1 else [], carry=carry)(*ins)


def _rms_fwd(x, w, *, name, width, heads=1, col0=0, out_dtype=BF16):
    t = x.shape[0]
    tm = _rows(t, width)
    cb = col0 // width

    def body(x_ref, w_ref, o_ref):
        xv = x_ref[...]
        r = lax.rsqrt(jnp.mean(xv * xv, axis=-1, keepdims=True) + EPS)
        o_ref[...] = (xv * r * w_ref[...]).astype(o_ref.dtype)

    return _pcall(body, name=name, grid=(t // tm, heads),
                  in_specs=[pl.BlockSpec((tm, width), lambda i, h: (i, cb + h)),
                            pl.BlockSpec((1, width), lambda i, h: (0, 0))],
                  out_specs=pl.BlockSpec((tm, width), lambda i, h: (i, h)),
                  out_shape=jax.ShapeDtypeStruct((t, heads * width), out_dtype))(x, w)


def _rms_bwd(x, w, dy, *, name, width, heads=1, col0=0, dcol0=0, res=None, out_dtype=F32, with_delta=False):
    t = x.shape[0]
    tm = _rows(t, width)
    cb, dcb = col0 // width, dcol0 // width

    def body(*refs):
        refs = list(refs)
        x_ref, w_ref, dy_ref = refs[:3]
        r_ref = refs[3] if res is not None else None
        outs = refs[4:] if res is not None else refs[3:]
        dx_ref, dw_ref = outs[:2]
        xv = x_ref[...]
        dyv = dy_ref[...].astype(F32)
        r = lax.rsqrt(jnp.mean(xv * xv, axis=-1, keepdims=True) + EPS)
        xh = xv * r
        dyw = dyv * w_ref[...]
        dx = r * (dyw - xh * jnp.mean(dyw * xh, axis=-1, keepdims=True))
        if with_delta:
            outs[2][...] = jnp.broadcast_to(jnp.sum(dx * xv, axis=-1, keepdims=True), dx.shape)
        if res is not None:
            dx = dx + r_ref[...]
        dx_ref[...] = dx.astype(dx_ref.dtype)

        @pl.when((pl.program_id(0) == 0) & (pl.program_id(1) == 0))
        def _():
            dw_ref[...] = jnp.zeros_like(dw_ref)

        dw_ref[...] += (dyv * xh).reshape(tm // 8, 8, width).sum(axis=0)

    blk = pl.BlockSpec((tm, width), lambda i, h: (i, h))
    ins = [x, w, dy]
    specs = [pl.BlockSpec((tm, width), lambda i, h: (i, cb + h)), pl.BlockSpec((1, width), lambda i, h: (0, 0)),
             pl.BlockSpec((tm, width), lambda i, h: (i, dcb + h))]
    if res is not None:
        ins.append(res)
        specs.append(blk)
    out_shape = [jax.ShapeDtypeStruct((t, heads * width), out_dtype), jax.ShapeDtypeStruct((8, width), F32)]
    out_specs = [blk, pl.BlockSpec((8, width), lambda i, h: (0, 0))]
    if with_delta:
        out_shape.append(jax.ShapeDtypeStruct((t, heads * width), F32))
        out_specs.append(blk)
    return _pcall(body, name=name, grid=(t // tm, heads), in_specs=specs, out_specs=out_specs, out_shape=out_shape)(*ins)


def _sig(x):
    return 1.0 / (1.0 + jnp.exp(-x))


@jax.custom_vjp
def _sigmoid(x):
    return _sig(x)


def _sigmoid_fwd(x):
    s = _sig(x)
    return s, s


def _sigmoid_bwd(s, g):
    return (g * s * (1.0 - s),)


_sigmoid.defvjp(_sigmoid_fwd, _sigmoid_bwd)


@jax.custom_vjp
def _softplus(x):
    return jnp.maximum(x, 0.0) + jnp.log(1.0 + jnp.exp(-jnp.abs(x)))


def _softplus_fwd(x):
    return _softplus(x), x


def _softplus_bwd(x, g):
    return (g * _sig(x),)


_softplus.defvjp(_softplus_fwd, _softplus_bwd)


def _silu(x):
    return x * _sig(x)


def _dsilu(x):
    s = _sig(x)
    return s * (1.0 + x * (1.0 - s))


NN3 = (((2,), (1,)), ((0,), (0,)))
NT3 = (((2,), (2,)), ((0,), (0,)))
TN3 = (((1,), (1,)), ((0,), (0,)))


def _bdot(a, b, dims):
    return lax.dot_general(a.astype(BF16), b.astype(BF16), dims, preferred_element_type=F32)


def _bf16_part(x):
    bits = lax.bitcast_convert_type(x, jnp.uint32) & jnp.uint32(0xFFFF0000)
    return lax.bitcast_convert_type(bits, F32)


def _scan_rows(x, reverse):
    c = x.shape[1]
    row = lax.broadcasted_iota(jnp.int32, x.shape, 1)
    step = 1
    while step < c:
        if reverse:
            x = x + jnp.where(row < c - step, pltpu.roll(x, c - step, axis=1), 0.0)
        else:
            x = x + jnp.where(row >= step, pltpu.roll(x, step, axis=1), 0.0)
        step *= 2
    return x


@jax.custom_vjp
def _prefix_rows(x):
    return _scan_rows(x, False)


_prefix_rows.defvjp(lambda x: (_scan_rows(x, False), None), lambda _, g: (_scan_rows(g, True),))


def _dot3(a, b, dims):
    (ca,), (cb,) = dims[0]
    a_hi, b_hi = _bf16_part(a), _bf16_part(b)
    a_lo, b_lo = (a - a_hi).astype(BF16), (b - b_hi).astype(BF16)
    a_hi, b_hi = a_hi.astype(BF16), b_hi.astype(BF16)
    return lax.dot_general(jnp.concatenate([a_hi, a_hi, a_lo], axis=ca), jnp.concatenate([b_hi, b_lo, b_hi], axis=cb),
                           dims, preferred_element_type=F32)


@jax.custom_vjp
def _nn_hi(a, b):
    return _dot3(a, b, NN3)


_nn_hi.defvjp(lambda a, b: (_dot3(a, b, NN3), (a, b)), lambda r, g: (_dot3(g, r[1], NT3), _dot3(r[0], g, TN3)))


@jax.custom_vjp
def _nn(a, b):
    return _bdot(a, b, NN3)


_nn.defvjp(lambda a, b: (_bdot(a, b, NN3), (a, b)), lambda r, g: (_bdot(g, r[1], NT3), _bdot(r[0], g, TN3)))


@jax.custom_vjp
def _nt(a, b):
    return _bdot(a, b, NT3)


_nt.defvjp(lambda a, b: (_bdot(a, b, NT3), (a, b)), lambda r, g: (_bdot(g, r[1], NN3), _bdot(g, r[0], TN3)))


@jax.custom_vjp
def _tn(a, b):
    return _bdot(a, b, TN3)


_tn.defvjp(lambda a, b: (_bdot(a, b, TN3), (a, b)), lambda r, g: (_bdot(r[1], g, NT3), _bdot(r[0], g, NN3)))


def _conv_pre(ext, w, rows):
    acc = w[0:1] * ext[5:5 + rows]
    for j in range(1, GDN_CONV):
        acc = acc + w[j:j + 1] * ext[5 + j:5 + j + rows]
    return acc


def _conv_fwd(proj, conv_w, *, name):
    t = proj.shape[0]
    tm, tc = _pick(t, 512), 512
    nb = tm // 8

    def body(u_ref, p_ref, w_ref, o_ref):
        i = pl.program_id(1)
        prev = jnp.where(i > 0, p_ref[...], 0.0)
        ext = jnp.concatenate([prev, u_ref[...]], axis=0)
        o_ref[...] = _silu(_conv_pre(ext, w_ref[...], tm))

    return _pcall(body, name=name, grid=(CONV_CH // tc, t // tm),
                  in_specs=[pl.BlockSpec((tm, tc), lambda j, i: (i, j)),
                            pl.BlockSpec((8, tc), lambda j, i: (jnp.maximum(i * nb - 1, 0), j)),
                            pl.BlockSpec((8, tc), lambda j, i: (0, j))],
                  out_specs=pl.BlockSpec((tm, tc), lambda j, i: (i, j)),
                  out_shape=jax.ShapeDtypeStruct((t, CONV_CH), F32))(proj, proj, conv_w)


def _conv_bwd(proj, conv_w, dy, *, name):
    t = proj.shape[0]
    tm, tc = _pick(t, 512), 512
    nb = tm // 8
    last = t // tm - 1

    def body(u_ref, p_ref, n_ref, dy_ref, dyn_ref, w_ref, du_ref, dw_ref):
        i = pl.program_id(1)
        w = w_ref[...]
        prev = jnp.where(i > 0, p_ref[...], 0.0)
        ext = jnp.concatenate([prev, u_ref[...], n_ref[...]], axis=0)
        c = _conv_pre(ext, w, tm + 8)
        dy_ext = jnp.concatenate([dy_ref[...], jnp.where(i < last, dyn_ref[...], 0.0)], axis=0)
        dc = dy_ext * _dsilu(c)
        du = w[3:4] * dc[0:tm]
        for j in range(GDN_CONV - 1):
            du = du + w[j:j + 1] * dc[3 - j:3 - j + tm]
        du_ref[...] = du.astype(du_ref.dtype)

        @pl.when(i == 0)
        def _():
            dw_ref[...] = jnp.zeros_like(dw_ref)

        for j in range(GDN_CONV):
            dw_ref[j] += (dc[0:tm] * ext[5 + j:5 + j + tm]).reshape(nb, 8, tc).sum(axis=0)

    cur = lambda j, i: (i, j)
    return _pcall(body, name=name, grid=(CONV_CH // tc, t // tm),
                  in_specs=[pl.BlockSpec((tm, tc), cur),
                            pl.BlockSpec((8, tc), lambda j, i: (jnp.maximum(i * nb - 1, 0), j)),
                            pl.BlockSpec((8, tc), lambda j, i: (jnp.minimum((i + 1) * nb, t // 8 - 1), j)),
                            pl.BlockSpec((tm, tc), cur),
                            pl.BlockSpec((8, tc), lambda j, i: (jnp.minimum((i + 1) * nb, t // 8 - 1), j)),
                            pl.BlockSpec((8, tc), lambda j, i: (0, j))],
                  out_specs=[pl.BlockSpec((tm, tc), cur), pl.BlockSpec((GDN_CONV, 8, tc), lambda j, i: (0, 0, j))],
                  out_shape=[jax.ShapeDtypeStruct((t, CONV_CH), BF16), jax.ShapeDtypeStruct((GDN_CONV, 8, CONV_CH), F32)],
                  )(proj, proj, proj, dy, dy, conv_w)


def _gdn_chunk(q_raw, k_raw, v, misc, params, state):
    nh, c = q_raw.shape[0], q_raw.shape[1]
    lane = lax.broadcasted_iota(jnp.int32, misc.shape, 1)
    prow = lax.broadcasted_iota(jnp.int32, params.shape, 0)
    plane = lax.broadcasted_iota(jnp.int32, params.shape, 1)
    heads = lambda pieces: jnp.concatenate([p[None] for p in pieces], axis=0)
    col = lambda at: heads([jnp.sum(jnp.where(lane == at + h, misc, 0.0), axis=1, keepdims=True) for h in range(nh)])
    par = lambda row: heads([jnp.sum(jnp.where((prow == row) & (plane == h), params, 0.0), keepdims=True)
                             for h in range(nh)])
    b_raw, a_raw = col(LANE_B), col(LANE_A)
    a_log, dt_bias = par(0), par(1)
    beta = _sigmoid(b_raw)
    g = -jnp.exp(a_log) * _softplus(a_raw + dt_bias)

    q = q_raw * lax.rsqrt(jnp.sum(q_raw * q_raw, axis=-1, keepdims=True) + EPS) * (HEAD ** -0.5)
    k = k_raw * lax.rsqrt(jnp.sum(k_raw * k_raw, axis=-1, keepdims=True) + EPS)

    ri = lax.broadcasted_iota(jnp.int32, (c, c), 0)
    ci = lax.broadcasted_iota(jnp.int32, (c, c), 1)
    tril, strict = ri >= ci, ri > ci
    gc = _prefix_rows(g)
    gc_col = jnp.broadcast_to(gc, (nh, c, c))
    gc_row = jnp.swapaxes(gc_col, 1, 2)
    decay = jnp.exp(jnp.where(tril, gc_col - gc_row, -1e30))

    kb = k * beta
    vb = v * beta
    a_mat = jnp.where(strict, _nt(kb, k) * decay, 0.0)
    x = -a_mat
    inv = (ri == ci).astype(F32) + x
    for _ in range(5):
        x = _nn_hi(x, x)
        inv = inv + _nn_hi(inv, x)
    u = _nn_hi(inv, vb)
    w = _nn_hi(inv, kb * jnp.exp(gc))
    intra = _nt(q, k) * decay

    v_new = u - _nn(w, state)
    o = _nn(q * jnp.exp(gc), state) + _nn(intra, v_new)
    g_last = jnp.sum(g, axis=1, keepdims=True)
    k_dec = k * jnp.exp(g_last - gc)
    new_state = state * jnp.exp(g_last) + _tn(k_dec, v_new)
    return o, new_state


def _gdn_specs(nc, rev):
    cidx = (lambda n: nc - 1 - n) if rev else (lambda n: n)
    hb = lambda part: pl.BlockSpec((GDN_CHUNK, GDN_QK), lambda n: (cidx(n), part))
    misc = pl.BlockSpec((GDN_CHUNK, HEAD), lambda n: (cidx(n), COL_MISC // HEAD))
    params = pl.BlockSpec((8, HEAD), lambda n: (0, 0))
    hist = pl.BlockSpec((1, GDN_HEADS, HEAD, HEAD), lambda n: (cidx(n), 0, 0, 0))
    return hb, misc, params, hist


def _split_heads(v):
    return jnp.stack([v[:, h * HEAD:(h + 1) * HEAD] for h in range(v.shape[1] // HEAD)])


def _merge_heads(v):
    return jnp.concatenate([v[h] for h in range(v.shape[0])], axis=1)


def _gdn_fwd(qkv, proj, params, *, name):
    t = qkv.shape[0]
    nc = t // GDN_CHUNK
    hb, misc, pspec, hist = _gdn_specs(nc, False)

    def body(q_ref, k_ref, v_ref, m_ref, p_ref, o_ref, hist_ref, s_ref):
        @pl.when(pl.program_id(0) == 0)
        def _():
            s_ref[...] = jnp.zeros_like(s_ref)

        state = s_ref[...]
        hist_ref[0] = state
        o, new_state = _gdn_chunk(_split_heads(q_ref[...]), _split_heads(k_ref[...]), _split_heads(v_ref[...]),
                                  m_ref[...], p_ref[...], state)
        o_ref[...] = _merge_heads(o)
        s_ref[...] = new_state

    return _pcall(body, name=name, grid=(nc,),
                  in_specs=[hb(0), hb(1), hb(2), misc, pspec],
                  out_specs=[hb(0), hist],
                  out_shape=[jax.ShapeDtypeStruct((t, GDN_QK), F32),
                             jax.ShapeDtypeStruct((nc, GDN_HEADS, HEAD, HEAD), F32)],
                  scratch=[pltpu.VMEM((GDN_HEADS, HEAD, HEAD), F32)])(qkv, qkv, qkv, proj, params)


def _gdn_bwd(qkv, proj, params, hist_arr, do, dmisc_in, *, name, carry=None):
    t = qkv.shape[0]
    nc = t // GDN_CHUNK
    hb, misc, pspec, hist = _gdn_specs(nc, True)
    mrow = pl.BlockSpec((GDN_CHUNK, HEAD), lambda n: (nc - 1 - n, 0))

    def body(q_ref, k_ref, v_ref, m_ref, p_ref, hist_ref, do_ref, dmi_ref, dqkv_ref, dm_ref, dp_ref, ds_ref):
        @pl.when(pl.program_id(0) == 0)
        def _():
            ds_ref[...] = jnp.zeros_like(ds_ref)
            dp_ref[...] = jnp.zeros_like(dp_ref)

        _, vjp = jax.vjp(_gdn_chunk, _split_heads(q_ref[...]), _split_heads(k_ref[...]), _split_heads(v_ref[...]),
                         m_ref[...], p_ref[...], hist_ref[0])
        dq, dk, dv, dm, dp, ds = vjp((_split_heads(do_ref[...]), ds_ref[...]))
        dqkv_ref[:, 0:GDN_QK] = _merge_heads(dq)
        dqkv_ref[:, GDN_QK:2 * GDN_QK] = _merge_heads(dk)
        dqkv_ref[:, 2 * GDN_QK:] = _merge_heads(dv)
        ds_ref[...] = ds
        dm_ref[...] = dmi_ref[...] + dm
        dp_ref[...] += dp

    return _pcall(body, name=name, grid=(nc,),
                  in_specs=[hb(0), hb(1), hb(2), misc, pspec, hist, hb(0), mrow],
                  out_specs=[pl.BlockSpec((GDN_CHUNK, CONV_CH), lambda n: (nc - 1 - n, 0)), mrow, pspec],
                  out_shape=[jax.ShapeDtypeStruct((t, CONV_CH), F32), jax.ShapeDtypeStruct((t, HEAD), F32),
                             jax.ShapeDtypeStruct((8, HEAD), F32)],
                  scratch=[pltpu.VMEM((GDN_HEADS, HEAD, HEAD), F32)], carry=carry,
                  )(qkv, qkv, qkv, proj, params, hist_arr, do, dmisc_in)


def _gate_fwd(o_raw, proj, w, *, name):
    t = o_raw.shape[0]
    tm = _rows(t, HEAD)
    zb = COL_Z // HEAD

    def body(o_ref, z_ref, w_ref, out_ref):
        ov = o_ref[...]
        r = lax.rsqrt(jnp.mean(ov * ov, axis=-1, keepdims=True) + EPS)
        out_ref[...] = (ov * r * w_ref[...] * _silu(z_ref[...])).astype(out_ref.dtype)

    blk = pl.BlockSpec((tm, HEAD), lambda i, h: (i, h))
    return _pcall(body, name=name, grid=(t // tm, GDN_HEADS),
                  in_specs=[blk, pl.BlockSpec((tm, HEAD), lambda i, h: (i, zb + h)), pl.BlockSpec((1, HEAD), lambda i, h: (0, 0))],
                  out_specs=blk, out_shape=jax.ShapeDtypeStruct((t, GDN_QK), BF16))(o_raw, proj, w)


def _gate_bwd(o_raw, proj, w, dmixed, *, name):
    t = o_raw.shape[0]
    tm = _rows(t, HEAD)
    zb = COL_Z // HEAD

    def body(o_ref, z_ref, w_ref, dy_ref, do_ref, dz_ref, dw_ref):
        ov, zv, dyv = o_ref[...], z_ref[...], dy_ref[...]
        r = lax.rsqrt(jnp.mean(ov * ov, axis=-1, keepdims=True) + EPS)
        xh = ov * r
        dn = dyv * _silu(zv)
        dz_ref[...] = (dyv * xh * w_ref[...] * _dsilu(zv)).astype(dz_ref.dtype)
        dnw = dn * w_ref[...]
        do_ref[...] = r * (dnw - xh * jnp.mean(dnw * xh, axis=-1, keepdims=True))

        @pl.when((pl.program_id(0) == 0) & (pl.program_id(1) == 0))
        def _():
            dw_ref[...] = jnp.zeros_like(dw_ref)

        dw_ref[...] += (dn * xh).reshape(tm // 8, 8, HEAD).sum(axis=0)

    blk = pl.BlockSpec((tm, HEAD), lambda i, h: (i, h))
    return _pcall(body, name=name, grid=(t // tm, GDN_HEADS),
                  in_specs=[blk, pl.BlockSpec((tm, HEAD), lambda i, h: (i, zb + h)), pl.BlockSpec((1, HEAD), lambda i, h: (0, 0)), blk],
                  out_specs=[blk, blk, pl.BlockSpec((8, HEAD), lambda i, h: (0, 0))],
                  out_shape=[jax.ShapeDtypeStruct((t, GDN_QK), F32), jax.ShapeDtypeStruct((t, GDN_QK), BF16),
                             jax.ShapeDtypeStruct((8, HEAD), F32)])(o_raw, proj, w, dmixed)


def _rope_tables():
    half = QK_ROPE // 2
    inv = ROPE_THETA ** (-jnp.arange(half, dtype=F32) / half)
    zeros = jnp.zeros((HEAD - QK_ROPE,), F32)
    inv_row = jnp.concatenate([inv, inv, zeros])
    sign_row = jnp.concatenate([-jnp.ones((half,), F32), jnp.ones((half,), F32), zeros])
    mask_row = jnp.concatenate([jnp.ones((QK_ROPE,), F32), zeros])
    return jnp.concatenate([inv_row[None], sign_row[None], mask_row[None], jnp.zeros((5, HEAD), F32)], axis=0)


def _rope_cs(pos, tab, *, name):
    t = pos.shape[0]
    tm = _pick(t, 1024)

    def body(pos_ref, tab_ref, o_ref):
        tab = tab_ref[...]
        ang = pos_ref[...] * tab[0:1]
        o_ref[...] = jnp.concatenate([jnp.cos(ang) * tab[2:3], jnp.sin(ang) * tab[1:2]], axis=1)

    return _pcall(body, name=name, grid=(t // tm,),
                  in_specs=[pl.BlockSpec((tm, 1), lambda i: (i, 0)), pl.BlockSpec((8, HEAD), lambda i: (0, 0))],
                  out_specs=pl.BlockSpec((tm, 2 * HEAD), lambda i: (i, 0)),
                  out_shape=jax.ShapeDtypeStruct((t, 2 * HEAD), F32))(pos, tab)


def _rotate(x, cs, sign):
    lane = lax.broadcasted_iota(jnp.int32, x.shape, 1)
    half = QK_ROPE // 2
    partner = jnp.where(lane < half, pltpu.roll(x, HEAD - half, axis=1), pltpu.roll(x, half, axis=1))
    return x * cs[:, :HEAD] + partner * (cs[:, HEAD:] * sign)


def _q_rot(q, cs, *, name, sign, out_dtype=BF16):
    t = q.shape[0]
    tm = _pick(t, 1024)
    scale = (HEAD + QK_ROPE) ** -0.5

    def body(q_ref, cs_ref, o_ref):
        qv = q_ref[...].astype(F32)
        rot = _rotate(qv[:, HEAD:], cs_ref[...], sign)
        o_ref[...] = (jnp.concatenate([qv[:, :HEAD], rot], axis=1) * scale).astype(o_ref.dtype)

    blk = pl.BlockSpec((tm, QHEAD), lambda i, h: (i, h))
    return _pcall(body, name=name, grid=(t // tm, MLA_HEADS),
                  in_specs=[blk, pl.BlockSpec((tm, 2 * HEAD), lambda i, h: (i, 0))],
                  out_specs=blk, out_shape=jax.ShapeDtypeStruct((t, MLA_HEADS * QHEAD), out_dtype))(q, cs)


def _q_up(cqn, wuq_p, cs, *, name):
    t, lora = cqn.shape
    tm = _pick(t, 1024)
    scale = (HEAD + QK_ROPE) ** -0.5

    def body(a_ref, w_ref, cs_ref, o_ref):
        qv = lax.dot_general(a_ref[...], w_ref[...], (NN, ((), ())), preferred_element_type=F32)
        rot = _rotate(qv[:, HEAD:], cs_ref[...], 1.0)
        o_ref[...] = (jnp.concatenate([qv[:, :HEAD], rot], axis=1) * scale).astype(o_ref.dtype)

    return _pcall(body, name=name, grid=(t // tm, MLA_HEADS),
                  in_specs=[pl.BlockSpec((tm, lora), lambda i, h: (i, 0)), pl.BlockSpec((lora, QHEAD), lambda i, h: (0, h)),
                            pl.BlockSpec((tm, 2 * HEAD), lambda i, h: (i, 0))],
                  out_specs=pl.BlockSpec((tm, QHEAD), lambda i, h: (i, h)),
                  out_shape=jax.ShapeDtypeStruct((t, MLA_HEADS * QHEAD), BF16))(cqn, wuq_p, cs)


def _kv_up(ckvn, wukv, proj, cs, *, name):
    t, lora = ckvn.shape
    tm = _pick(t, 1024)

    def body(a_ref, w_ref, m_ref, cs_ref, k_ref, v_ref):
        kvv = lax.dot_general(a_ref[...], w_ref[...], (NN, ((), ())), preferred_element_type=F32)
        misc = m_ref[...]
        lane = lax.broadcasted_iota(jnp.int32, misc.shape, 1)
        rot = _rotate(jnp.where(lane < QK_ROPE, misc, 0.0), cs_ref[...], 1.0)
        k_ref[...] = jnp.concatenate([kvv[:, :HEAD], rot], axis=1).astype(k_ref.dtype)
        v_ref[...] = kvv[:, HEAD:].astype(v_ref.dtype)

    return _pcall(body, name=name, grid=(t // tm, MLA_HEADS),
                  in_specs=[pl.BlockSpec((tm, lora), lambda i, h: (i, 0)), pl.BlockSpec((lora, QHEAD), lambda i, h: (0, h)),
                            pl.BlockSpec((tm, HEAD), lambda i, h: (i, COL_MISC // HEAD)),
                            pl.BlockSpec((tm, 2 * HEAD), lambda i, h: (i, 0))],
                  out_specs=[pl.BlockSpec((tm, QHEAD), lambda i, h: (i, h)), pl.BlockSpec((tm, HEAD), lambda i, h: (i, h))],
                  out_shape=[jax.ShapeDtypeStruct((t, MLA_HEADS * QHEAD), BF16), jax.ShapeDtypeStruct((t, MLA_HEADS * HEAD), BF16)],
                  )(ckvn, wukv, proj, cs)


def _krope_bwd(dkr, cs, *, name):
    t = dkr.shape[0]
    tm = _pick(t, 512)

    def body(d_ref, cs_ref, o_ref):
        d = d_ref[...]
        acc = d[:, :HEAD]
        for h in range(1, MLA_HEADS):
            acc = acc + d[:, h * HEAD:(h + 1) * HEAD]
        o_ref[...] = _rotate(acc, cs_ref[...], -1.0)

    return _pcall(body, name=name, grid=(t // tm,),
                  in_specs=[pl.BlockSpec((tm, MLA_HEADS * HEAD), lambda i: (i, 0)), pl.BlockSpec((tm, 2 * HEAD), lambda i: (i, 0))],
                  out_specs=pl.BlockSpec((tm, HEAD), lambda i: (i, 0)),
                  out_shape=jax.ShapeDtypeStruct((t, HEAD), F32))(dkr, cs)


NEG = -1e30


def _tri(step, counts):
    starts = [sum(counts[:o]) for o in range(len(counts))]
    outer = sum([(step >= s).astype(jnp.int32) for s in starts[1:]], jnp.int32(0))
    start = sum([(step >= starts[o]).astype(jnp.int32) * (starts[o] - starts[o - 1]) for o in range(1, len(counts))], jnp.int32(0))
    return outer, step - start


def _attn_fwd(q, k, v, *, name, tq=1024, tk=1024, carry=None):
    t = q.shape[0]
    tq, tk = _pick(t, tq), _pick(t, tk)
    nq = t // tq
    last_kv = lambda i: (i * tq + tq - 1) // tk
    counts = [last_kv(i) + 1 for i in range(nq)]

    def body(q_ref, k_ref, v_ref, o_ref, lse_ref, m_ref, l_ref, acc_ref):
        i, j = _tri(pl.program_id(1), counts)

        @pl.when(j == 0)
        def _():
            m_ref[...] = jnp.full_like(m_ref, NEG)
            l_ref[...] = jnp.zeros_like(l_ref)
            acc_ref[...] = jnp.zeros_like(acc_ref)

        def step(masked):
            s = lax.dot_general(q_ref[...], k_ref[...], (NT, ((), ())), preferred_element_type=F32)
            if masked:
                qpos = i * tq + lax.broadcasted_iota(jnp.int32, s.shape, 0)
                kpos = j * tk + lax.broadcasted_iota(jnp.int32, s.shape, 1)
                s = jnp.where(kpos <= qpos, s, NEG)
            m_prev = m_ref[...]
            m_new = jnp.maximum(m_prev, jnp.max(s, axis=1, keepdims=True))
            alpha = jnp.exp(m_prev - m_new)
            p = jnp.exp(s - m_new)
            l_ref[...] = alpha * l_ref[...] + jnp.sum(p, axis=1, keepdims=True)
            acc_ref[...] = alpha * acc_ref[...] + lax.dot_general(p.astype(BF16), v_ref[...], (NN, ((), ())),
                                                                  preferred_element_type=F32)
            m_ref[...] = m_new

        crosses = j * tk + tk - 1 > i * tq

        @pl.when(crosses)
        def _():
            step(True)

        @pl.when(jnp.logical_not(crosses))
        def _():
            step(False)

        @pl.when(j == last_kv(i))
        def _():
            o_ref[...] = acc_ref[...] / l_ref[...]
            lse_ref[...] = jnp.broadcast_to(m_ref[...] + jnp.log(l_ref[...]), lse_ref.shape)

    qblk = pl.BlockSpec((tq, QHEAD), lambda h, s: (_tri(s, counts)[0], h))
    oblk = pl.BlockSpec((tq, HEAD), lambda h, s: (_tri(s, counts)[0], h))
    return _pcall(body, name=name, grid=(MLA_HEADS, sum(counts)),
                  in_specs=[qblk, pl.BlockSpec((tk, QHEAD), lambda h, s: (_tri(s, counts)[1], h)),
                            pl.BlockSpec((tk, HEAD), lambda h, s: (_tri(s, counts)[1], h))],
                  out_specs=[oblk, oblk],
                  out_shape=[jax.ShapeDtypeStruct((t, MLA_HEADS * HEAD), F32), jax.ShapeDtypeStruct((t, MLA_HEADS * HEAD), F32)],
                  scratch=[pltpu.VMEM((tq, 1), F32), pltpu.VMEM((tq, 1), F32), pltpu.VMEM((tq, HEAD), F32)],
                  carry=carry)(q, k, v)


def _attn_bwd(q, k, v, do, lse, delta, *, name, tq=1024, tk=1024, carry=None):
    t = q.shape[0]
    tq, tk = _pick(t, tq), _pick(t, tk)
    nq, nk = t // tq, t // tk
    first_q = lambda j: (j * tk) // tq
    counts = [nq - first_q(j) for j in range(nk)]

    def where(step):
        j, off = _tri(step, counts)
        return j, first_q(j) + off

    lanes = lambda col: jnp.tile(col, (1, tk // HEAD))

    def body(q_ref, k_ref, v_ref, do_ref, lse_ref, dl_ref, dq_ref, dkv_ref, dkr_ref, dk_acc, dv_acc):
        j, i = where(pl.program_id(1))

        @pl.when(i == first_q(j))
        def _():
            dk_acc[...] = jnp.zeros_like(dk_acc)
            dv_acc[...] = jnp.zeros_like(dv_acc)

        def step(masked):
            qv, kv_, dov = q_ref[...], k_ref[...], do_ref[...].astype(BF16)
            s = lax.dot_general(qv, kv_, (NT, ((), ())), preferred_element_type=F32)
            p = jnp.exp((s - lanes(lse_ref[...])).astype(BF16))
            if masked:
                qpos = i * tq + lax.broadcasted_iota(jnp.int32, s.shape, 0)
                kpos = j * tk + lax.broadcasted_iota(jnp.int32, s.shape, 1)
                p = jnp.where(kpos <= qpos, p, jnp.zeros_like(p))
            dv_acc[...] += lax.dot_general(p, dov, (TN, ((), ())), preferred_element_type=F32)
            dp = lax.dot_general(dov, v_ref[...], (NT, ((), ())), preferred_element_type=F32)
            ds = p * (dp - lanes(dl_ref[...])).astype(BF16)
            dk_acc[...] += lax.dot_general(ds, qv, (TN, ((), ())), preferred_element_type=F32)
            contrib = lax.dot_general(ds, kv_, (NN, ((), ())), preferred_element_type=F32)
            rows = pl.ds(pl.multiple_of(i * tq, tq), tq)

            @pl.when(j == 0)
            def _():
                dq_ref[rows, :] = contrib

            @pl.when(j > 0)
            def _():
                dq_ref[rows, :] += contrib

        crosses = j * tk + tk - 1 > i * tq

        @pl.when(crosses)
        def _():
            step(True)

        @pl.when(jnp.logical_not(crosses))
        def _():
            step(False)

        @pl.when(i == nq - 1)
        def _():
            dk = dk_acc[...]
            dkv_ref[...] = jnp.concatenate([dk[:, :HEAD], dv_acc[...]], axis=1).astype(dkv_ref.dtype)
            dkr_ref[...] = dk[:, HEAD:]

    qi = lambda h, s: (where(s)[1], h)
    kj = lambda h, s: (where(s)[0], h)
    return _pcall(body, name=name, grid=(MLA_HEADS, sum(counts)),
                  in_specs=[pl.BlockSpec((tq, QHEAD), qi), pl.BlockSpec((tk, QHEAD), kj), pl.BlockSpec((tk, HEAD), kj),
                            pl.BlockSpec((tq, HEAD), qi), pl.BlockSpec((tq, HEAD), qi), pl.BlockSpec((tq, HEAD), qi)],
                  out_specs=[pl.BlockSpec((t, QHEAD), lambda h, s: (0, h)), pl.BlockSpec((tk, QHEAD), kj),
                             pl.BlockSpec((tk, HEAD), kj)],
                  out_shape=[jax.ShapeDtypeStruct((t, MLA_HEADS * QHEAD), F32), jax.ShapeDtypeStruct((t, MLA_HEADS * QHEAD), BF16),
                             jax.ShapeDtypeStruct((t, MLA_HEADS * HEAD), F32)],
                  scratch=[pltpu.VMEM((tk, QHEAD), F32), pltpu.VMEM((tk, HEAD), F32)], carry=carry)(q, k, v, do, lse, delta)


def _ffn_up(h, wgate, wup, *, name, bm=512, bn=FF_WIDE):
    t = h.shape[0]
    bm = _pick(t, bm)

    def body(h_ref, wg_ref, wu_ref, g_ref, u_ref, a_ref):
        hv = h_ref[...]
        g = lax.dot_general(hv, wg_ref[...], (NN, ((), ())), preferred_element_type=F32)
        u = lax.dot_general(hv, wu_ref[...], (NN, ((), ())), preferred_element_type=F32)
        g_ref[...] = g.astype(g_ref.dtype)
        u_ref[...] = u.astype(u_ref.dtype)
        a_ref[...] = (_silu(g) * u).astype(a_ref.dtype)

    w_spec = pl.BlockSpec((D_MODEL, bn), lambda j, i: (0, j))
    o_spec = pl.BlockSpec((bm, bn), lambda j, i: (i, j))
    sds = jax.ShapeDtypeStruct((t, D_FF), BF16)
    return _pcall(body, name=name, grid=(D_FF // bn, t // bm),
                  in_specs=[pl.BlockSpec((bm, D_MODEL), lambda j, i: (i, 0)), w_spec, w_spec],
                  out_specs=[o_spec] * 3, out_shape=[sds] * 3)(h, wgate, wup)


def _ffn_down_dx(dy, wdown, gate, up, *, name, bm=512, bn=FF_WIDE):
    t = dy.shape[0]
    bm = _pick(t, bm)

    def body(dy_ref, w_ref, g_ref, u_ref, dg_ref, du_ref):
        d = lax.dot_general(dy_ref[...].astype(BF16), w_ref[...], (NT, ((), ())), preferred_element_type=F32)
        g = g_ref[...].astype(F32)
        dg_ref[...] = (d * u_ref[...].astype(F32) * _dsilu(g)).astype(dg_ref.dtype)
        du_ref[...] = (d * _silu(g)).astype(du_ref.dtype)

    o_spec = pl.BlockSpec((bm, bn), lambda j, i: (i, j))
    sds = jax.ShapeDtypeStruct((t, D_FF), BF16)
    return _pcall(body, name=name, grid=(D_FF // bn, t // bm),
                  in_specs=[pl.BlockSpec((bm, D_MODEL), lambda j, i: (i, 0)), pl.BlockSpec((bn, D_MODEL), lambda j, i: (j, 0)),
                            o_spec, o_spec],
                  out_specs=[o_spec, o_spec], out_shape=[sds, sds])(dy, wdown, gate, up)


def _loss_bwd(x2, w, target, *, name):
    t = x2.shape[0]
    tm = _rows(t, D_MODEL)

    def body(x_ref, w_ref, t_ref, dx_ref, dw_ref, l_ref):
        xv, wv = x_ref[...], w_ref[...]
        r = lax.rsqrt(jnp.mean(xv * xv, axis=-1, keepdims=True) + EPS)
        xh = xv * r
        err = xh * wv - t_ref[...]
        dy = err * (1.0 / D_MODEL)
        dyw = dy * wv
        dx_ref[...] = r * (dyw - xh * jnp.mean(dyw * xh, axis=-1, keepdims=True))

        @pl.when(pl.program_id(0) == 0)
        def _():
            dw_ref[...] = jnp.zeros_like(dw_ref)
            l_ref[...] = jnp.zeros_like(l_ref)

        dw_ref[...] += (dy * xh).reshape(tm // 8, 8, D_MODEL).sum(axis=0)
        sq = (err * err).reshape(tm // 8, 8, D_MODEL).sum(axis=0)
        part = sq[:, :HEAD]
        for c in range(1, D_MODEL // HEAD):
            part = part + sq[:, c * HEAD:(c + 1) * HEAD]
        l_ref[...] += part * (0.5 / D_MODEL)

    row = pl.BlockSpec((tm, D_MODEL), lambda i: (i, 0))
    return _pcall(body, name=name, grid=(t // tm,),
                  in_specs=[row, pl.BlockSpec((1, D_MODEL), lambda i: (0, 0)), row],
                  out_specs=[row, pl.BlockSpec((8, D_MODEL), lambda i: (0, 0)), pl.BlockSpec((8, HEAD), lambda i: (0, 0))],
                  out_shape=[jax.ShapeDtypeStruct((t, D_MODEL), F32), jax.ShapeDtypeStruct((8, D_MODEL), F32),
                             jax.ShapeDtypeStruct((8, HEAD), F32)])(x2, w, target)


def _unshard_cols(g):
    return jnp.transpose(g, (1, 0, 2)).reshape(g.shape[1], N_DEV * g.shape[2])


def _shard_cols(w):
    return jnp.transpose(w.reshape(w.shape[0], N_DEV, w.shape[1] // N_DEV), (1, 0, 2))


_WIN_ORDER = ((0, 4096), (4112, 5136), (5136, 5200), (4096, 4112))
_WIN_SHARD = IN_WIDTH // N_DEV


def _win_pieces():
    out, pos = [], 0
    for a, b in _WIN_ORDER:
        c = a
        while c < b:
            dev, off = divmod(c, _WIN_SHARD)
            width = min(b, (dev + 1) * _WIN_SHARD) - c
            out.append((dev, off, width, pos))
            c, pos = c + width, pos + width
    return out


def _win_gathered_to_padded(g):
    pieces = [g[dev][:, off:off + width] for dev, off, width, _ in _win_pieces()]
    return jnp.concatenate(pieces + [jnp.zeros((g.shape[1], PROJ_W - IN_WIDTH), g.dtype)], axis=1)


def _win_padded_to_shards(d):
    shards = []
    for dev in range(N_DEV):
        mine = sorted((off, width, pos) for dv, off, width, pos in _win_pieces() if dv == dev)
        shards.append(jnp.concatenate([d[:, pos:pos + width] for _, width, pos in mine], axis=1))
    return jnp.stack(shards)


def _wuq_to_padded(w):
    w3 = w.reshape(w.shape[0], MLA_HEADS, HEAD + QK_ROPE)
    return jnp.pad(w3, ((0, 0), (0, 0), (0, QHEAD - HEAD - QK_ROPE))).reshape(w.shape[0], MLA_HEADS * QHEAD)


def _wuq_from_padded(d):
    return d.reshape(d.shape[0], MLA_HEADS, QHEAD)[:, :, :HEAD + QK_ROPE].reshape(d.shape[0], MLA_HEADS * (HEAD + QK_ROPE))


def _late_weights(g_out, g_gate, g_up, g_down):
    return g_out.reshape(D_MODEL, D_MODEL), _unshard_cols(g_gate), _unshard_cols(g_up), g_down.reshape(D_FF, D_MODEL)


def _local_step(x, pos, target, win_p, wuq_p, wukv, late, conv_w, small, exchange):
    cs = _rope_cs(pos, _rope_tables(), name="rope_cs")
    if not exchange:
        wout, wgate, wup, wdown = late
    h1 = _rms_fwd(x, small["attn_norm_w"], name="rms1_fwd", width=D_MODEL)
    proj = _mm(h1, win_p, name="mm_in", bn=768)
    qkv = _conv_fwd(proj, conv_w, name="conv_fwd")
    o_gdn_raw, hist = _gdn_fwd(qkv, proj, small["gdn_params"], name="gdn_fwd")
    o_gdn = _gate_fwd(o_gdn_raw, proj, small["gdn_norm_w"], name="gate_fwd")
    cqn = _rms_fwd(proj, small["q_norm_w"], name="rmsq_fwd", width=Q_LORA, col0=COL_CQ)
    ckvn = _rms_fwd(proj, small["kv_norm_w"], name="rmskv_fwd", width=KV_LORA, col0=COL_CKV)
    q_full = _q_up(cqn, wuq_p, cs, name="q_up")
    k_full, v_b = _kv_up(ckvn, wukv, proj, cs, name="kv_up")
    if exchange:
        (o_mla_raw, lse), gathered = _attn_fwd(q_full, k_full, v_b, name="attn_fwd", carry=_Gather(late))
        wout, wgate, wup, wdown = _late_weights(*gathered)
    else:
        o_mla_raw, lse = _attn_fwd(q_full, k_full, v_b, name="attn_fwd")
    o_mla = _rms_fwd(o_mla_raw, small["mla_out_norm_w"], name="rmso_fwd", width=HEAD, heads=MLA_HEADS)
    mixed = jnp.concatenate([o_gdn, o_mla], axis=1)
    x1 = _mm(mixed, wout, name="mm_out", res=x)
    h2 = _rms_fwd(x1, small["ffn_norm_w"], name="rms2_fwd", width=D_MODEL)
    gate, up, act = _ffn_up(h2, wgate, wup, name="ffn_up")
    x2 = _mm(act, wdown, name="mm_down", res=x1, bk=FF_WIDE)
    dx2, dw_final, loss_part = _loss_bwd(x2, small["final_norm_w"], target, name="loss_bwd")
    dgate, dup = _ffn_down_dx(dx2, wdown, gate, up, name="ffn_down_dx")
    d_wdown = _mm(act, dx2, name="mm_down_dw", ta=True, out_dtype=BF16, bm=FF_WIDE)
    dh2 = _mm(dgate, wgate, name="mm_gateup_dx", tb=True, bk=FF_WIDE, pair=(dup, wup))
    d_wgate = _mm(h2, dgate, name="mm_gate_dw", ta=True, out_dtype=BF16, bn=FF_WIDE)
    d_wup = _mm(h2, dup, name="mm_up_dw", ta=True, out_dtype=BF16, bn=FF_WIDE)
    dx1, dw_ffn = _rms_bwd(x1, small["ffn_norm_w"], dh2, name="rms2_bwd", width=D_MODEL, res=dx2)
    dmixed = _mm(dx1, wout, name="mm_out_dx", tb=True)
    d_wout = _mm(mixed, dx1, name="mm_out_dw", ta=True, out_dtype=BF16)
    do_mla, dw_mla_out, delta = _rms_bwd(o_mla_raw, small["mla_out_norm_w"], dmixed, name="rmso_bwd", width=HEAD,
                                         heads=MLA_HEADS, dcol0=GDN_QK, with_delta=True, out_dtype=BF16)
    if exchange:
        send = [d_wdown.reshape(N_DEV, D_FF // N_DEV, D_MODEL), _shard_cols(d_wgate), _shard_cols(d_wup)]
        (dq_full, dkv, dkr_h), (r_down, r_gate, r_up) = _attn_bwd(q_full, k_full, v_b, do_mla, lse, delta, name="attn_bwd",
                                                                  carry=_Exchange(send, [False] * 3))
    else:
        dq_full, dkv, dkr_h = _attn_bwd(q_full, k_full, v_b, do_mla, lse, delta, name="attn_bwd")
    dq_pre = _q_rot(dq_full, cs, name="q_rot_bwd", sign=-1.0)
    dmisc_kr = _krope_bwd(dkr_h, cs, name="krope_bwd")
    dcqn = _mm(dq_pre, wuq_p, name="mm_uq_dx", tb=True)
    d_wuq = _mm(cqn, dq_pre, name="mm_uq_dw", ta=True, out_dtype=BF16)
    dckvn = _mm(dkv, wukv, name="mm_ukv_dx", tb=True)
    d_wukv = _mm(ckvn, dkv, name="mm_ukv_dw", ta=True, out_dtype=BF16)
    dcq, dw_qn = _rms_bwd(proj, small["q_norm_w"], dcqn, name="rmsq_bwd", width=Q_LORA, col0=COL_CQ, out_dtype=BF16)
    dckv, dw_kvn = _rms_bwd(proj, small["kv_norm_w"], dckvn, name="rmskv_bwd", width=KV_LORA, col0=COL_CKV, out_dtype=BF16)
    do_gdn, dz, dw_gdn = _gate_bwd(o_gdn_raw, proj, small["gdn_norm_w"], dmixed, name="gate_bwd")
    if exchange:
        send = [d_wout.reshape(N_DEV, D_MODEL // N_DEV, D_MODEL), _shard_cols(_wuq_from_padded(d_wuq)), _shard_cols(d_wukv)]
        (dqkv, dmisc, d_params), (r_out, r_uq, r_ukv) = _gdn_bwd(
            qkv, proj, small["gdn_params"], hist, do_gdn, dmisc_kr, name="gdn_bwd", carry=_Exchange(send, [False] * 3))
    else:
        dqkv, dmisc, d_params = _gdn_bwd(qkv, proj, small["gdn_params"], hist, do_gdn, dmisc_kr, name="gdn_bwd")
    dqkv_pre, dconv = _conv_bwd(proj, conv_w, dqkv, name="conv_bwd")
    dproj = jnp.concatenate([dqkv_pre, dz, dcq, dckv, dmisc.astype(BF16), jnp.zeros((x.shape[0], PROJ_W - COL_MISC - HEAD), BF16)], axis=1)
    d_win = _mm(h1, dproj, name="mm_in_dw", ta=True, out_dtype=BF16, bn=768)
    if exchange:
        dh1, (r_in,) = _mm(dproj, win_p, name="mm_in_dx", tb=True, bk=768,
                           carry=_Exchange([_win_padded_to_shards(d_win)], [False]))
        d_win = r_in
    else:
        dh1 = _mm(dproj, win_p, name="mm_in_dx", tb=True, bk=768)
    dx, dw_attn = _rms_bwd(x, small["attn_norm_w"], dh1, name="rms1_bwd", width=D_MODEL, res=dx1)

    if exchange:
        big = {"w_in": d_win, "w_uq": r_uq, "w_ukv": r_ukv, "w_out": r_out, "w_gate": r_gate, "w_up": r_up, "w_down": r_down}
    else:
        big = {"w_in": d_win, "w_uq": d_wuq, "w_ukv": d_wukv, "w_out": d_wout, "w_gate": d_wgate, "w_up": d_wup,
               "w_down": d_wdown}
    sm = {"attn_norm_w": dw_attn, "ffn_norm_w": dw_ffn, "final_norm_w": dw_final, "q_norm_w": dw_qn, "kv_norm_w": dw_kvn,
          "gdn_norm_w": dw_gdn, "mla_out_norm_w": dw_mla_out, "gdn_params": d_params, "conv_w": dconv, "loss": loss_part}
    return dx, big, sm


def _exchange(ex, *, name):
    def body(*refs):
        ins, outs, sems = refs[:ex.n], refs[ex.n:2 * ex.n], refs[2 * ex.n:]
        ex.start(ins, outs, sems)
        ex.forward(ins, outs, sems)
        ex.finish(ins, outs, sems)

    any_spec = pl.BlockSpec(memory_space=pl.ANY)
    return pl.pallas_call(body, name=name, in_specs=[any_spec] * ex.n, out_specs=[any_spec] * ex.n,
                          out_shape=ex.out_shape(), scratch_shapes=ex.sems())(*ex.arrays)


def _adamw_math(g, w, m, v):
    m = ADAM_B1 * m + (1.0 - ADAM_B1) * g
    v = ADAM_B2 * v + (1.0 - ADAM_B2) * (g * g)
    m_hat = m / (1.0 - ADAM_B1 ** ADAM_STEP)
    v_hat = v / (1.0 - ADAM_B2 ** ADAM_STEP)
    delta = -ADAM_LR * (m_hat / (jnp.sqrt(v_hat) + ADAM_EPS) + ADAM_WD * w)
    return delta, m, v


def _adamw(parts, w, m, v, *, name):
    npart, r, c = parts.shape
    tr = r if r * c * 4 <= (1 << 20) else _rows(r, c, 1 << 20)

    def body(p_ref, w_ref, m_ref, v_ref, g_ref, d_ref, nm_ref, nv_ref):
        g = p_ref[0].astype(F32)
        for s in range(1, npart):
            g = g + p_ref[s].astype(F32)
        g_ref[...] = g
        d_ref[...], nm_ref[...], nv_ref[...] = _adamw_math(g, w_ref[...], m_ref[...], v_ref[...])

    blk = pl.BlockSpec((tr, c), lambda i: (i, 0))
    sds = jax.ShapeDtypeStruct((r, c), F32)
    return _pcall(body, name=name, grid=(r // tr,),
                  in_specs=[pl.BlockSpec((npart, tr, c), lambda i: (0, i, 0)), blk, blk, blk],
                  out_specs=[blk] * 4, out_shape=[sds] * 4)(parts, w, m, v)


def _sum_parts(parts, *, name):
    npart, r, c = parts.shape

    def body(p_ref, o_ref):
        g = p_ref[0]
        for s in range(1, npart):
            g = g + p_ref[s]
        o_ref[...] = g

    return _pcall(body, name=name, grid=(1,), in_specs=[pl.BlockSpec((npart, r, c), lambda i: (0, 0, 0))],
                  out_specs=pl.BlockSpec((r, c), lambda i: (0, 0)), out_shape=jax.ShapeDtypeStruct((r, c), F32))(parts)


_SMALL = (("attn_norm_w", D_MODEL), ("ffn_norm_w", D_MODEL), ("final_norm_w", D_MODEL), ("q_norm_w", Q_LORA),
          ("kv_norm_w", KV_LORA), ("gdn_norm_w", HEAD), ("mla_out_norm_w", HEAD), ("a_log", HEAD), ("dt_bias", HEAD))
_SMALL_ROWS = sum(n for _, n in _SMALL) // HEAD
_CONV_ROWS = GDN_CONV * CONV_CH // HEAD
_PACK_ROWS = 160


def _pad_lanes(v, n):
    v = v.reshape(-1)
    return jnp.concatenate([v, jnp.zeros((n - v.shape[0],), v.dtype)])


def kernel(x, positions, attn_norm_w, w_in, conv_w, a_log, dt_bias, gdn_norm_w, q_norm_w, w_uq, kv_norm_w, w_ukv, mla_out_norm_w, w_out, ffn_norm_w, w_gate, w_up, w_down, final_norm_w, loss_target, m_attn_norm_w, m_w_in, m_conv_w, m_a_log, m_dt_bias, m_gdn_norm_w, m_q_norm_w, m_w_uq, m_kv_norm_w, m_w_ukv, m_mla_out_norm_w, m_w_out, m_ffn_norm_w, m_w_gate, m_w_up, m_w_down, m_final_norm_w, v_attn_norm_w, v_w_in, v_conv_w, v_a_log, v_dt_bias, v_gdn_norm_w, v_q_norm_w, v_w_uq, v_kv_norm_w, v_w_ukv, v_mla_out_norm_w, v_w_out, v_ffn_norm_w, v_w_gate, v_w_up, v_w_down, v_final_norm_w):
    t = x.shape[1]
    me = 4 * lax.axis_index("x") + 2 * lax.axis_index("y") + lax.axis_index("c")
    weights = dict(attn_norm_w=attn_norm_w, w_in=w_in, conv_w=conv_w, a_log=a_log, dt_bias=dt_bias, gdn_norm_w=gdn_norm_w,
                   q_norm_w=q_norm_w, w_uq=w_uq, kv_norm_w=kv_norm_w, w_ukv=w_ukv, mla_out_norm_w=mla_out_norm_w, w_out=w_out,
                   ffn_norm_w=ffn_norm_w, w_gate=w_gate, w_up=w_up, w_down=w_down, final_norm_w=final_norm_w)
    mom_m = dict(attn_norm_w=m_attn_norm_w, w_in=m_w_in, conv_w=m_conv_w, a_log=m_a_log, dt_bias=m_dt_bias, gdn_norm_w=m_gdn_norm_w,
                 q_norm_w=m_q_norm_w, w_uq=m_w_uq, kv_norm_w=m_kv_norm_w, w_ukv=m_w_ukv, mla_out_norm_w=m_mla_out_norm_w,
                 w_out=m_w_out, ffn_norm_w=m_ffn_norm_w, w_gate=m_w_gate, w_up=m_w_up, w_down=m_w_down, final_norm_w=m_final_norm_w)
    mom_v = dict(attn_norm_w=v_attn_norm_w, w_in=v_w_in, conv_w=v_conv_w, a_log=v_a_log, dt_bias=v_dt_bias, gdn_norm_w=v_gdn_norm_w,
                 q_norm_w=v_q_norm_w, w_uq=v_w_uq, kv_norm_w=v_kv_norm_w, w_ukv=v_w_ukv, mla_out_norm_w=v_mla_out_norm_w,
                 w_out=v_w_out, ffn_norm_w=v_ffn_norm_w, w_gate=v_w_gate, w_up=v_w_up, w_down=v_w_down, final_norm_w=v_final_norm_w)
    big_names = ("w_in", "w_uq", "w_ukv", "w_out", "w_gate", "w_up", "w_down")

    shard = {n: weights[n][0].astype(BF16) for n in big_names}
    g_in, g_uq, g_ukv, g_conv = _exchange(_Gather([shard["w_in"], shard["w_uq"], shard["w_ukv"], weights["conv_w"][0]]),
                                          name="gather_weights")
    win_p = _win_gathered_to_padded(g_in)
    wuq_p = _wuq_to_padded(_unshard_cols(g_uq))
    wukv = _unshard_cols(g_ukv)
    late = [shard["w_out"], shard["w_gate"], shard["w_up"], shard["w_down"]]
    conv_full = jnp.concatenate([_unshard_cols(g_conv), jnp.zeros((8 - GDN_CONV, CONV_CH), F32)], axis=0)

    gdn_params = jnp.concatenate([_pad_lanes(a_log, HEAD)[None], _pad_lanes(dt_bias, HEAD)[None], jnp.zeros((6, HEAD), F32)], axis=0)
    small = {n: weights[n].reshape(1, -1) for n in ("attn_norm_w", "ffn_norm_w", "final_norm_w", "q_norm_w", "kv_norm_w",
                                                    "gdn_norm_w", "mla_out_norm_w")}
    small["gdn_params"] = gdn_params

    dx, big, sm = _local_step(x[0], positions.reshape(t, 1).astype(F32), loss_target[0], win_p, wuq_p, wukv, late,
                              conv_full, small, True)

    rows8 = lambda name: jnp.sum(sm[name], axis=0)
    pieces = [rows8(n) for n, _ in _SMALL[:7]]
    pieces += [_pad_lanes(jnp.sum(sm["gdn_params"][0:1], axis=0), HEAD), _pad_lanes(jnp.sum(sm["gdn_params"][1:2], axis=0), HEAD)]
    pieces.append(jnp.sum(sm["conv_w"], axis=1).reshape(-1))
    pieces.append(_pad_lanes(jnp.sum(sm["loss"]).reshape(1), HEAD))
    packed = _pad_lanes(jnp.concatenate(pieces), _PACK_ROWS * HEAD).reshape(_PACK_ROWS, HEAD)
    (r_small,) = _exchange(_Exchange([packed], [True]), name="exchange_small")

    outs_g, outs_d, outs_m, outs_v = {}, {}, {}, {}
    for name in big_names:
        g, d, nm, nv = _adamw(big[name], weights[name][0], mom_m[name][0], mom_v[name][0], name="adamw_" + name)
        outs_g[name], outs_d[name], outs_m[name], outs_v[name] = g[None], d[None], nm[None], nv[None]

    total = _sum_parts(r_small, name="sum_small")
    flat = total.reshape(-1)
    loss = flat[(_SMALL_ROWS + _CONV_ROWS) * HEAD]
    g_small, off = {}, 0
    for n, size in _SMALL:
        g_small[n] = flat[off:off + size]
        off += size
    g_conv_full = flat[off:off + GDN_CONV * CONV_CH].reshape(GDN_CONV, CONV_CH)
    g_small["conv_w"] = lax.dynamic_slice(g_conv_full, (0, me * (CONV_CH // N_DEV)), (GDN_CONV, CONV_CH // N_DEV)).reshape(-1)
    order = [n for n, _ in _SMALL] + ["conv_w"]
    sizes = dict(_SMALL)
    sizes["conv_w"] = GDN_CONV * CONV_CH // N_DEV
    true_size = {n: weights[n].size for n in order}

    def pack(d):
        return jnp.concatenate([_pad_lanes(d[n], sizes[n]) for n in order]).reshape(1, -1, HEAD)

    g2, d2, m2, v2 = _adamw(pack(g_small), pack(weights)[0], pack(mom_m)[0], pack(mom_v)[0], name="adamw_small")
    off = 0
    for n in order:
        for src, dst in ((g2, outs_g), (d2, outs_d), (m2, outs_m), (v2, outs_v)):
            dst[n] = src.reshape(-1)[off:off + true_size[n]].reshape(weights[n].shape)
        off += sizes[n]

    names = ("attn_norm_w", "w_in", "conv_w", "a_log", "dt_bias", "gdn_norm_w", "q_norm_w", "w_uq", "kv_norm_w", "w_ukv",
             "mla_out_norm_w", "w_out", "ffn_norm_w", "w_gate", "w_up", "w_down", "final_norm_w")
    return (loss, dx[None], *[outs_g[n] for n in names], *[outs_d[n] for n in names], *[outs_m[n] for n in names],
            *[outs_v[n] for n in names])
```

```python
import functools
import math

import jax
import jax.numpy as jnp
from jax import lax
from jax.experimental import pallas as pl
from jax.experimental.pallas import tpu as pltpu

F32 = jnp.float32
BF16 = jnp.bfloat16

D_MODEL = 2048
GDN_HEADS = 8
HEAD = 128
GDN_CONV = 4
GDN_CHUNK = 64
GDN_QK = GDN_HEADS * HEAD
CONV_CH = 3 * GDN_QK
MLA_HEADS = 8
QK_ROPE = 64
Q_LORA = 512
KV_LORA = 512
ROPE_THETA = 10000.0
D_FF = 5632
EPS = 1e-6
IN_WIDTH = 5200
ADAM_LR, ADAM_B1, ADAM_B2, ADAM_EPS, ADAM_WD, ADAM_STEP = 0.001, 0.9, 0.999, 1e-08, 0.01, 10

PROJ_W = 5376
COL_Z = 3072
COL_CQ = 4096
COL_CKV = 4608
COL_MISC = 5120
LANE_B = 64
LANE_A = 72
QHEAD = 256
FF_WIDE = D_FF // 4
N_DEV = 8
MESH = pl.DeviceIdType.MESH
VMEM_LIMIT_MB = 48

NN = ((1,), (0,))
NT = ((1,), (1,))
TN = ((0,), (0,))


def _my_place():
    x, y, c = lax.axis_index("x"), lax.axis_index("y"), lax.axis_index("c")
    return x, y, c, 4 * x + 2 * y + c


def _peer(x, y, c, p):
    px, py, pc = x ^ ((p >> 2) & 1), y ^ ((p >> 1) & 1), c ^ (p & 1)
    return (px, py, pc), 4 * px + 2 * py + pc


class _Exchange:
    def __init__(self, arrays, gather):
        self.arrays, self.gather, self.n = list(arrays), list(gather), len(arrays)

    def out_shape(self):
        return [jax.ShapeDtypeStruct(((N_DEV,) + a.shape) if g else a.shape, a.dtype)
                for a, g in zip(self.arrays, self.gather)]

    def sems(self):
        return [pltpu.SemaphoreType.DMA((self.n * (N_DEV - 1),)), pltpu.SemaphoreType.DMA((self.n * (N_DEV - 1),)),
                pltpu.SemaphoreType.DMA((self.n,))]

    def _copies(self, ins, outs, sems):
        send_sems, recv_sems, local_sems = sems
        x, y, c, me = _my_place()
        local = [pltpu.make_async_copy(ins[k] if self.gather[k] else ins[k].at[me], outs[k].at[me], local_sems.at[k])
                 for k in range(self.n)]
        sent, received = [], []
        for p in range(1, N_DEV):
            place, num = _peer(x, y, c, p)
            for k in range(self.n):
                src = ins[k] if self.gather[k] else ins[k].at[num]
                idx = k * (N_DEV - 1) + p - 1
                mk = lambda dst: pltpu.make_async_remote_copy(src_ref=src, dst_ref=dst, send_sem=send_sems.at[idx],
                                                              recv_sem=recv_sems.at[idx], device_id=place, device_id_type=MESH)
                sent.append(mk(outs[k].at[me]))
                received.append(mk(outs[k].at[num]))
        return local, sent, received

    def start(self, ins, outs, sems):
        local, sent, _ = self._copies(ins, outs, sems)
        for cp in local + sent:
            cp.start()

    def forward(self, ins, outs, sems):
        pass

    def finish(self, ins, outs, sems):
        local, sent, received = self._copies(ins, outs, sems)
        for cp in received:
            cp.wait_recv()
        for cp in sent:
            cp.wait_send()
        for cp in local:
            cp.wait()


class _Gather:
    def __init__(self, arrays):
        self.arrays, self.n = list(arrays), len(arrays)

    def out_shape(self):
        return [jax.ShapeDtypeStruct((N_DEV,) + a.shape, a.dtype) for a in self.arrays]

    def sems(self):
        return [pltpu.SemaphoreType.DMA((self.n * (N_DEV - 1),)), pltpu.SemaphoreType.DMA((self.n * (N_DEV - 1),)),
                pltpu.SemaphoreType.DMA((self.n,))]

    def _plan(self, ins, outs, sems):
        send_sems, recv_sems, local_sems = sems
        x, y, c, me = _my_place()
        sibling = (x, y, 1 - c)
        chips = [(1 - x, y), (x, 1 - y), (1 - x, 1 - y)]
        num = lambda px, py, pc: 4 * px + 2 * py + pc

        def copy(k, i, block, to, src=None):
            slot = outs[k].at[num(*block)]
            return pltpu.make_async_remote_copy(src_ref=slot if src is None else src, dst_ref=slot,
                                                send_sem=send_sems.at[k * (N_DEV - 1) + i],
                                                recv_sem=recv_sems.at[k * (N_DEV - 1) + i],
                                                device_id=to, device_id_type=MESH)

        local = [pltpu.make_async_copy(ins[k], outs[k].at[me], local_sems.at[k]) for k in range(self.n)]
        return (x, y, c), sibling, chips, copy, local

    def start(self, ins, outs, sems):
        me, sibling, chips, copy, local = self._plan(ins, outs, sems)
        for cp in local:
            cp.start()
        for k in range(self.n):
            copy(k, 0, me, sibling, src=ins[k]).start()
            for j, chip in enumerate(chips):
                copy(k, 1 + j, me, (*chip, me[2]), src=ins[k]).start()

    def forward(self, ins, outs, sems):
        me, sibling, chips, copy, _ = self._plan(ins, outs, sems)
        for j, chip in enumerate(chips):
            for k in range(self.n):
                copy(k, 1 + j, (*chip, me[2]), me).wait_recv()
                copy(k, 4 + j, (*chip, me[2]), sibling).start()

    def finish(self, ins, outs, sems):
        me, sibling, chips, copy, local = self._plan(ins, outs, sems)
        for k in range(self.n):
            copy(k, 0, sibling, me).wait_recv()
            for j, chip in enumerate(chips):
                copy(k, 4 + j, (*chip, 1 - me[2]), me).wait_recv()
        for k in range(self.n):
            copy(k, 0, me, sibling, src=ins[k]).wait_send()
            for j, chip in enumerate(chips):
                copy(k, 1 + j, me, (*chip, me[2]), src=ins[k]).wait_send()
                copy(k, 4 + j, (*chip, me[2]), sibling).wait_send()
        for cp in local:
            cp.wait()


def _pcall(body, *, name, grid, in_specs, out_specs, out_shape, scratch=(), carry=None):
    params = pltpu.CompilerParams(dimension_semantics=("arbitrary",) * len(grid), vmem_limit_bytes=VMEM_LIMIT_MB << 20)
    if carry is None:
        return pl.pallas_call(body, name=name, grid=grid, in_specs=in_specs, out_specs=out_specs, out_shape=out_shape,
                              scratch_shapes=list(scratch), compiler_params=params)
    single = not isinstance(out_specs, (list, tuple))
    out_specs = [out_specs] if single else list(out_specs)
    out_shape = [out_shape] if single else list(out_shape)
    n_in, n_out, n_scr, na = len(in_specs), len(out_specs), len(scratch), carry.n

    def wrapped(*refs):
        ins, cin = refs[:n_in], refs[n_in:n_in + na]
        outs, cout = refs[n_in + na:n_in + na + n_out], refs[n_in + na + n_out:n_in + 2 * na + n_out]
        scr, sems = refs[n_in + 2 * na + n_out:n_in + 2 * na + n_out + n_scr], refs[n_in + 2 * na + n_out + n_scr:]
        total = math.prod(grid)
        step = functools.reduce(lambda a, d: a * grid[d] + pl.program_id(d), range(len(grid)), 0)

        @pl.when(step == 0)
        def _():
            carry.start(cin, cout, sems)

        body(*ins, *outs, *scr)

        @pl.when(step == min(total * 7 // 8, total - 1))
        def _():
            carry.forward(cin, cout, sems)

        @pl.when(step == total - 1)
        def _():
            carry.finish(cin, cout, sems)

    any_spec = pl.BlockSpec(memory_space=pl.ANY)
    call = pl.pallas_call(wrapped, name=name, grid=grid, in_specs=list(in_specs) + [any_spec] * na,
                          out_specs=out_specs + [any_spec] * na, out_shape=out_shape + carry.out_shape(),
                          scratch_shapes=list(scratch) + carry.sems(), compiler_params=params)

    def run(*args):
        res = call(*args, *carry.arrays)
        main = res[0] if single else list(res[:n_out])
        return main, list(res[n_out:])

    return run


def _pick(dim, pref):
    if dim <= pref:
        return dim
    c = pref
    while c >= 128:
        if dim % c == 0 and c % 128 == 0:
            return c
        c -= 128
    return dim


def _rows(t, width, target_bytes=2 << 20):
    r = max(8, min(t, target_bytes // (4 * width)))
    r = 1 << (r.bit_length() - 1)
    while t % r:
        r //= 2
    return r


MM_FULL_K = 2048


def _mm(a, b, *, name, ta=False, tb=False, res=None, out_dtype=F32, bm=1024, bn=1024, bk=1024, carry=None, pair=None):
    m, k = (a.shape[1], a.shape[0]) if ta else a.shape
    n = b.shape[0] if tb else b.shape[1]
    assert (b.shape[1] if tb else b.shape[0]) == k
    bm, bn, bk = _pick(m, bm), _pick(n, bn), (k if k <= MM_FULL_K else _pick(k, bk))
    nk = k // bk
    dims = (((0,) if ta else (1,), (1,) if tb else (0,)), ((), ()))
    n_ab = 2 if pair is None else 4

    def body(*refs):
        a_ref, b_ref = refs[:2]
        r_ref = refs[n_ab] if res is not None else None
        o_ref = refs[n_ab + 1] if res is not None else refs[n_ab]
        part = lax.dot_general(a_ref[...].astype(BF16), b_ref[...].astype(BF16), dims, preferred_element_type=F32)
        if pair is not None:
            part = part + lax.dot_general(refs[2][...].astype(BF16), refs[3][...].astype(BF16), dims,
                                          preferred_element_type=F32)

        def finish(out):
            if res is not None:
                out = out + r_ref[...]
            o_ref[...] = out.astype(o_ref.dtype)

        if nk == 1:
            finish(part)
            return
        acc_ref = refs[-1]
        kk = pl.program_id(2)

        @pl.when(kk == 0)
        def _():
            acc_ref[...] = part

        @pl.when((kk > 0) & (kk < nk - 1))
        def _():
            acc_ref[...] += part

        @pl.when(kk == nk - 1)
        def _():
            finish(acc_ref[...] + part)

    a_spec = pl.BlockSpec((bk, bm), lambda i, j, kk: (kk, i)) if ta else pl.BlockSpec((bm, bk), lambda i, j, kk: (i, kk))
    b_spec = pl.BlockSpec((bn, bk), lambda i, j, kk: (j, kk)) if tb else pl.BlockSpec((bk, bn), lambda i, j, kk: (kk, j))
    o_spec = pl.BlockSpec((bm, bn), lambda i, j, kk: (i, j))
    ins, specs = [a, b], [a_spec, b_spec]
    if pair is not None:
        assert pair[0].shape == a.shape and pair[1].shape == b.shape
        ins += list(pair)
        specs += [a_spec, b_spec]
    if res is not None:
        ins.append(res)
        specs.append(o_spec)
    return _pcall(body, name=name, grid=(m // bm, n // bn, nk), in_specs=specs, out_specs=o_spec,
                  out_shape=jax.ShapeDtypeStruct((m, n), out_dtype),
                  scratch=[pltpu.VMEM((bm, bn), F32)] if nk > 1 else [], carry=carry)(*ins)


def _mm_dw2(a, b1, b2, *, name, bm=1024, bn=FF_WIDE, bk=1024):
    k, m = a.shape
    n = b1.shape[1]
    assert b1.shape == b2.shape == (k, n)
    bm, bn, bk = _pick(m, bm), _pick(n, bn), _pick(k, bk)
    nk = k // bk

    def body(a_ref, b1_ref, b2_ref, o1_ref, o2_ref, acc1_ref, acc2_ref):
        kk = pl.program_id(2)
        av = a_ref[...].astype(BF16)
        for b_ref, o_ref, acc_ref in ((b1_ref, o1_ref, acc1_ref), (b2_ref, o2_ref, acc2_ref)):
            part = lax.dot_general(av, b_ref[...].astype(BF16), (TN, ((), ())), preferred_element_type=F32)

            @pl.when(kk == 0)
            def _():
                acc_ref[...] = part

            @pl.when((kk > 0) & (kk < nk - 1))
            def _():
                acc_ref[...] += part

            @pl.when(kk == nk - 1)
            def _():
                o_ref[...] = (acc_ref[...] + part).astype(o_ref.dtype)

    b_spec = pl.BlockSpec((bk, bn), lambda i, j, kk: (kk, j))
    o_spec = pl.BlockSpec((bm, bn), lambda i, j, kk: (i, j))
    sds = jax.ShapeDtypeStruct((m, n), BF16)
    return _pcall(body, name=name, grid=(m // bm, n // bn, nk),
                  in_specs=[pl.BlockSpec((bk, bm), lambda i, j, kk: (kk, i)), b_spec, b_spec],
                  out_specs=[o_spec, o_spec], out_shape=[sds, sds],
                  scratch=[pltpu.VMEM((bm, bn), F32), pltpu.VMEM((bm, bn), F32)])(a, b1, b2)


def _rms_fwd(x, w, *, name, width, heads=1, col0=0, out_dtype=BF16):
    t = x.shape[0]
    tm = _rows(t, width)
    cb = col0 // width

    def body(x_ref, w_ref, o_ref):
        xv = x_ref[...]
        r = lax.rsqrt(jnp.mean(xv * xv, axis=-1, keepdims=True) + EPS)
        o_ref[...] = (xv * r * w_ref[...]).astype(o_ref.dtype)

    return _pcall(body, name=name, grid=(t // tm, heads),
                  in_specs=[pl.BlockSpec((tm, width), lambda i, h: (i, cb + h)),
                            pl.BlockSpec((1, width), lambda i, h: (0, 0))],
                  out_specs=pl.BlockSpec((tm, width), lambda i, h: (i, h)),
                  out_shape=jax.ShapeDtypeStruct((t, heads * width), out_dtype))(x, w)


def _rms_bwd(x, w, dy, *, name, width, heads=1, col0=0, dcol0=0, res=None, out_dtype=F32, with_delta=False):
    t = x.shape[0]
    tm = _rows(t, width)
    cb, dcb = col0 // width, dcol0 // width

    def body(*refs):
        refs = list(refs)
        x_ref, w_ref, dy_ref = refs[:3]
        r_ref = refs[3] if res is not None else None
        outs = refs[4:] if res is not None else refs[3:]
        dx_ref, dw_ref = outs[:2]
        xv = x_ref[...]
        dyv = dy_ref[...].astype(F32)
        r = lax.rsqrt(jnp.mean(xv * xv, axis=-1, keepdims=True) + EPS)
        xh = xv * r
        dyw = dyv * w_ref[...]
        dx = r * (dyw - xh * jnp.mean(dyw * xh, axis=-1, keepdims=True))
        if with_delta:
            outs[2][...] = jnp.broadcast_to(jnp.sum(dx * xv, axis=-1, keepdims=True), dx.shape)
        if res is not None:
            dx = dx + r_ref[...]
        dx_ref[...] = dx.astype(dx_ref.dtype)

        @pl.when((pl.program_id(0) == 0) & (pl.program_id(1) == 0))
        def _():
            dw_ref[...] = jnp.zeros_like(dw_ref)

        dw_ref[...] += (dyv * xh).reshape(tm // 8, 8, width).sum(axis=0)

    blk = pl.BlockSpec((tm, width), lambda i, h: (i, h))
    ins = [x, w, dy]
    specs = [pl.BlockSpec((tm, width), lambda i, h: (i, cb + h)), pl.BlockSpec((1, width), lambda i, h: (0, 0)),
             pl.BlockSpec((tm, width), lambda i, h: (i, dcb + h))]
    if res is not None:
        ins.append(res)
        specs.append(blk)
    out_shape = [jax.ShapeDtypeStruct((t, heads * width), out_dtype), jax.ShapeDtypeStruct((8, width), F32)]
    out_specs = [blk, pl.BlockSpec((8, width), lambda i, h: (0, 0))]
    if with_delta:
        out_shape.append(jax.ShapeDtypeStruct((t, heads * width), F32))
        out_specs.append(blk)
    return _pcall(body, name=name, grid=(t // tm, heads), in_specs=specs, out_specs=out_specs, out_shape=out_shape)(*ins)


def _sig(x):
    return 1.0 / (1.0 + jnp.exp(-x))


@jax.custom_vjp
def _sigmoid(x):
    return _sig(x)


def _sigmoid_fwd(x):
    s = _sig(x)
    return s, s


def _sigmoid_bwd(s, g):
    return (g * s * (1.0 - s),)


_sigmoid.defvjp(_sigmoid_fwd, _sigmoid_bwd)


@jax.custom_vjp
def _softplus(x):
    return jnp.maximum(x, 0.0) + jnp.log(1.0 + jnp.exp(-jnp.abs(x)))


def _softplus_fwd(x):
    return _softplus(x), x


def _softplus_bwd(x, g):
    return (g * _sig(x),)


_softplus.defvjp(_softplus_fwd, _softplus_bwd)


def _silu(x):
    return x * _sig(x)


def _dsilu(x):
    s = _sig(x)
    return s * (1.0 + x * (1.0 - s))


NN3 = (((2,), (1,)), ((0,), (0,)))
NT3 = (((2,), (2,)), ((0,), (0,)))
TN3 = (((1,), (1,)), ((0,), (0,)))


def _bdot(a, b, dims):
    return lax.dot_general(a.astype(BF16), b.astype(BF16), dims, preferred_element_type=F32)


def _bf16_part(x):
    bits = lax.bitcast_convert_type(x, jnp.uint32) & jnp.uint32(0xFFFF0000)
    return lax.bitcast_convert_type(bits, F32)


def _scan_rows(x, reverse):
    c = x.shape[1]
    row = lax.broadcasted_iota(jnp.int32, x.shape, 1)
    step = 1
    while step < c:
        if reverse:
            x = x + jnp.where(row < c - step, pltpu.roll(x, c - step, axis=1), 0.0)
        else:
            x = x + jnp.where(row >= step, pltpu.roll(x, step, axis=1), 0.0)
        step *= 2
    return x


@jax.custom_vjp
def _prefix_rows(x):
    return _scan_rows(x, False)


_prefix_rows.defvjp(lambda x: (_scan_rows(x, False), None), lambda _, g: (_scan_rows(g, True),))


def _dot3(a, b, dims):
    (ca,), (cb,) = dims[0]
    a_hi, b_hi = _bf16_part(a), _bf16_part(b)
    a_lo, b_lo = (a - a_hi).astype(BF16), (b - b_hi).astype(BF16)
    a_hi, b_hi = a_hi.astype(BF16), b_hi.astype(BF16)
    return lax.dot_general(jnp.concatenate([a_hi, a_hi, a_lo], axis=ca), jnp.concatenate([b_hi, b_lo, b_hi], axis=cb),
                           dims, preferred_element_type=F32)


@jax.custom_vjp
def _nn_hi(a, b):
    return _dot3(a, b, NN3)


_nn_hi.defvjp(lambda a, b: (_dot3(a, b, NN3), (a, b)), lambda r, g: (_dot3(g, r[1], NT3), _dot3(r[0], g, TN3)))


@jax.custom_vjp
def _nn(a, b):
    return _bdot(a, b, NN3)


_nn.defvjp(lambda a, b: (_bdot(a, b, NN3), (a, b)), lambda r, g: (_bdot(g, r[1], NT3), _bdot(r[0], g, TN3)))


@jax.custom_vjp
def _nt(a, b):
    return _bdot(a, b, NT3)


_nt.defvjp(lambda a, b: (_bdot(a, b, NT3), (a, b)), lambda r, g: (_bdot(g, r[1], NN3), _bdot(g, r[0], TN3)))


@jax.custom_vjp
def _tn(a, b):
    return _bdot(a, b, TN3)


_tn.defvjp(lambda a, b: (_bdot(a, b, TN3), (a, b)), lambda r, g: (_bdot(r[1], g, NT3), _bdot(r[0], g, NN3)))


def _rows_down(cur, prev8, s):
    r = pltpu.roll(cur, s, axis=0)
    rp = pltpu.roll(prev8, s, axis=0)
    row = lax.broadcasted_iota(jnp.int32, rp.shape, 0)
    head = jnp.where(row < s, rp, r[:8])
    return head if cur.shape[0] == 8 else jnp.concatenate([head, r[8:]], axis=0)


def _rows_up(cur, next8, s):
    n = cur.shape[0]
    r = pltpu.roll(cur, n - s, axis=0)
    rn = pltpu.roll(next8, 8 - s, axis=0)
    row = lax.broadcasted_iota(jnp.int32, rn.shape, 0)
    tail = jnp.where(row >= 8 - s, rn, r[n - 8:])
    return tail if n == 8 else jnp.concatenate([r[:n - 8], tail], axis=0)


def _conv_taps(cur, prev8):
    return [_rows_down(cur, prev8, GDN_CONV - 1 - j) for j in range(GDN_CONV - 1)] + [cur]


def _conv_pre(taps, w):
    acc = w[0:1] * taps[0]
    for j in range(1, GDN_CONV):
        acc = acc + w[j:j + 1] * taps[j]
    return acc


def _conv_fwd(proj, conv_w, *, name):
    t = proj.shape[0]
    tm, tc = _pick(t, 512), 512
    nb = tm // 8

    def body(u_ref, p_ref, w_ref, o_ref):
        i = pl.program_id(1)
        prev = jnp.where(i > 0, p_ref[...], 0.0)
        o_ref[...] = _silu(_conv_pre(_conv_taps(u_ref[...], prev), w_ref[...]))

    return _pcall(body, name=name, grid=(CONV_CH // tc, t // tm),
                  in_specs=[pl.BlockSpec((tm, tc), lambda j, i: (i, j)),
                            pl.BlockSpec((8, tc), lambda j, i: (jnp.maximum(i * nb - 1, 0), j)),
                            pl.BlockSpec((8, tc), lambda j, i: (0, j))],
                  out_specs=pl.BlockSpec((tm, tc), lambda j, i: (i, j)),
                  out_shape=jax.ShapeDtypeStruct((t, CONV_CH), F32))(proj, proj, conv_w)


def _conv_bwd(proj, conv_w, dy, *, name):
    t = proj.shape[0]
    tm, tc = _pick(t, 512), 512
    nb = tm // 8
    last = t // tm - 1

    def body(u_ref, p_ref, n_ref, dy_ref, dyn_ref, w_ref, du_ref, dw_ref):
        i = pl.program_id(1)
        w = w_ref[...]
        cur = u_ref[...]
        taps = _conv_taps(cur, jnp.where(i > 0, p_ref[...], 0.0))
        dc = dy_ref[...] * _dsilu(_conv_pre(taps, w))
        taps_next = _conv_taps(n_ref[...], cur[tm - 8:])
        dc_next = jnp.where(i < last, dyn_ref[...], 0.0) * _dsilu(_conv_pre(taps_next, w))
        du = w[3:4] * dc
        for j in range(GDN_CONV - 1):
            du = du + w[j:j + 1] * _rows_up(dc, dc_next, GDN_CONV - 1 - j)
        du_ref[...] = du.astype(du_ref.dtype)

        @pl.when(i == 0)
        def _():
            dw_ref[...] = jnp.zeros_like(dw_ref)

        for j in range(GDN_CONV):
            dw_ref[j] += (dc * taps[j]).reshape(nb, 8, tc).sum(axis=0)

    cur = lambda j, i: (i, j)
    return _pcall(body, name=name, grid=(CONV_CH // tc, t // tm),
                  in_specs=[pl.BlockSpec((tm, tc), cur),
                            pl.BlockSpec((8, tc), lambda j, i: (jnp.maximum(i * nb - 1, 0), j)),
                            pl.BlockSpec((8, tc), lambda j, i: (jnp.minimum((i + 1) * nb, t // 8 - 1), j)),
                            pl.BlockSpec((tm, tc), cur),
                            pl.BlockSpec((8, tc), lambda j, i: (jnp.minimum((i + 1) * nb, t // 8 - 1), j)),
                            pl.BlockSpec((8, tc), lambda j, i: (0, j))],
                  out_specs=[pl.BlockSpec((tm, tc), cur), pl.BlockSpec((GDN_CONV, 8, tc), lambda j, i: (0, 0, j))],
                  out_shape=[jax.ShapeDtypeStruct((t, CONV_CH), BF16), jax.ShapeDtypeStruct((GDN_CONV, 8, CONV_CH), F32)],
                  )(proj, proj, proj, dy, dy, conv_w)


def _gdn_chunk(q_raw, k_raw, v, misc, params, state):
    nh, c = q_raw.shape[0], q_raw.shape[1]
    lane = lax.broadcasted_iota(jnp.int32, misc.shape, 1)
    prow = lax.broadcasted_iota(jnp.int32, params.shape, 0)
    plane = lax.broadcasted_iota(jnp.int32, params.shape, 1)
    heads = lambda pieces: jnp.concatenate([p[None] for p in pieces], axis=0)
    col = lambda at: heads([jnp.sum(jnp.where(lane == at + h, misc, 0.0), axis=1, keepdims=True) for h in range(nh)])
    par = lambda row: heads([jnp.sum(jnp.where((prow == row) & (plane == h), params, 0.0), keepdims=True)
                             for h in range(nh)])
    b_raw, a_raw = col(LANE_B), col(LANE_A)
    a_log, dt_bias = par(0), par(1)
    beta = _sigmoid(b_raw)
    g = -jnp.exp(a_log) * _softplus(a_raw + dt_bias)

    q = q_raw * lax.rsqrt(jnp.sum(q_raw * q_raw, axis=-1, keepdims=True) + EPS) * (HEAD ** -0.5)
    k = k_raw * lax.rsqrt(jnp.sum(k_raw * k_raw, axis=-1, keepdims=True) + EPS)

    ri = lax.broadcasted_iota(jnp.int32, (c, c), 0)
    ci = lax.broadcasted_iota(jnp.int32, (c, c), 1)
    tril, strict = ri >= ci, ri > ci
    gc = _prefix_rows(g)
    gc_col = jnp.broadcast_to(gc, (nh, c, c))
    gc_row = jnp.swapaxes(gc_col, 1, 2)
    decay = jnp.exp(jnp.where(tril, gc_col - gc_row, -1e30))

    kb = k * beta
    vb = v * beta
    a_mat = jnp.where(strict, _nt(kb, k) * decay, 0.0)
    x = -a_mat
    inv = (ri == ci).astype(F32) + x
    for _ in range(5):
        x = _nn_hi(x, x)
        inv = inv + _nn_hi(inv, x)
    u = _nn_hi(inv, vb)
    w = _nn_hi(inv, kb * jnp.exp(gc))
    intra = _nt(q, k) * decay

    v_new = u - _nn(w, state)
    o = _nn(q * jnp.exp(gc), state) + _nn(intra, v_new)
    g_last = jnp.sum(g, axis=1, keepdims=True)
    k_dec = k * jnp.exp(g_last - gc)
    new_state = state * jnp.exp(g_last) + _tn(k_dec, v_new)
    return o, new_state


def _gdn_specs(nc, rev):
    cidx = (lambda n: nc - 1 - n) if rev else (lambda n: n)
    hb = lambda part: pl.BlockSpec((GDN_CHUNK, GDN_QK), lambda n: (cidx(n), part))
    misc = pl.BlockSpec((GDN_CHUNK, HEAD), lambda n: (cidx(n), COL_MISC // HEAD))
    params = pl.BlockSpec((8, HEAD), lambda n: (0, 0))
    hist = pl.BlockSpec((1, GDN_HEADS, HEAD, HEAD), lambda n: (cidx(n), 0, 0, 0))
    return hb, misc, params, hist


def _split_heads(v):
    return jnp.stack([v[:, h * HEAD:(h + 1) * HEAD] for h in range(v.shape[1] // HEAD)])


def _merge_heads(v):
    return jnp.concatenate([v[h] for h in range(v.shape[0])], axis=1)


def _gdn_fwd(qkv, proj, params, *, name):
    t = qkv.shape[0]
    nc = t // GDN_CHUNK
    hb, misc, pspec, hist = _gdn_specs(nc, False)

    def body(q_ref, k_ref, v_ref, m_ref, p_ref, o_ref, hist_ref, s_ref):
        @pl.when(pl.program_id(0) == 0)
        def _():
            s_ref[...] = jnp.zeros_like(s_ref)

        state = s_ref[...]
        hist_ref[0] = state
        o, new_state = _gdn_chunk(_split_heads(q_ref[...]), _split_heads(k_ref[...]), _split_heads(v_ref[...]),
                                  m_ref[...], p_ref[...], state)
        o_ref[...] = _merge_heads(o)
        s_ref[...] = new_state

    return _pcall(body, name=name, grid=(nc,),
                  in_specs=[hb(0), hb(1), hb(2), misc, pspec],
                  out_specs=[hb(0), hist],
                  out_shape=[jax.ShapeDtypeStruct((t, GDN_QK), F32),
                             jax.ShapeDtypeStruct((nc, GDN_HEADS, HEAD, HEAD), F32)],
                  scratch=[pltpu.VMEM((GDN_HEADS, HEAD, HEAD), F32)])(qkv, qkv, qkv, proj, params)


def _gdn_bwd(qkv, proj, params, hist_arr, do, dmisc_in, *, name, carry=None):
    t = qkv.shape[0]
    nc = t // GDN_CHUNK
    hb, misc, pspec, hist = _gdn_specs(nc, True)
    mrow = pl.BlockSpec((GDN_CHUNK, HEAD), lambda n: (nc - 1 - n, 0))

    def body(q_ref, k_ref, v_ref, m_ref, p_ref, hist_ref, do_ref, dmi_ref, dqkv_ref, dm_ref, dp_ref, ds_ref):
        @pl.when(pl.program_id(0) == 0)
        def _():
            ds_ref[...] = jnp.zeros_like(ds_ref)
            dp_ref[...] = jnp.zeros_like(dp_ref)

        _, vjp = jax.vjp(_gdn_chunk, _split_heads(q_ref[...]), _split_heads(k_ref[...]), _split_heads(v_ref[...]),
                         m_ref[...], p_ref[...], hist_ref[0])
        dq, dk, dv, dm, dp, ds = vjp((_split_heads(do_ref[...]), ds_ref[...]))
        dqkv_ref[:, 0:GDN_QK] = _merge_heads(dq)
        dqkv_ref[:, GDN_QK:2 * GDN_QK] = _merge_heads(dk)
        dqkv_ref[:, 2 * GDN_QK:] = _merge_heads(dv)
        ds_ref[...] = ds
        dm_ref[...] = dmi_ref[...] + dm
        dp_ref[...] += dp

    return _pcall(body, name=name, grid=(nc,),
                  in_specs=[hb(0), hb(1), hb(2), misc, pspec, hist, hb(0), mrow],
                  out_specs=[pl.BlockSpec((GDN_CHUNK, CONV_CH), lambda n: (nc - 1 - n, 0)), mrow, pspec],
                  out_shape=[jax.ShapeDtypeStruct((t, CONV_CH), F32), jax.ShapeDtypeStruct((t, HEAD), F32),
                             jax.ShapeDtypeStruct((8, HEAD), F32)],
                  scratch=[pltpu.VMEM((GDN_HEADS, HEAD, HEAD), F32)], carry=carry,
                  )(qkv, qkv, qkv, proj, params, hist_arr, do, dmisc_in)


def _gate_fwd(o_raw, proj, w, *, name):
    t = o_raw.shape[0]
    tm = _rows(t, HEAD)
    zb = COL_Z // HEAD

    def body(o_ref, z_ref, w_ref, out_ref):
        ov = o_ref[...]
        r = lax.rsqrt(jnp.mean(ov * ov, axis=-1, keepdims=True) + EPS)
        out_ref[...] = (ov * r * w_ref[...] * _silu(z_ref[...])).astype(out_ref.dtype)

    blk = pl.BlockSpec((tm, HEAD), lambda i, h: (i, h))
    return _pcall(body, name=name, grid=(t // tm, GDN_HEADS),
                  in_specs=[blk, pl.BlockSpec((tm, HEAD), lambda i, h: (i, zb + h)), pl.BlockSpec((1, HEAD), lambda i, h: (0, 0))],
                  out_specs=blk, out_shape=jax.ShapeDtypeStruct((t, GDN_QK), BF16))(o_raw, proj, w)


def _gate_bwd(o_raw, proj, w, dmixed, *, name):
    t = o_raw.shape[0]
    tm = _rows(t, HEAD)
    zb = COL_Z // HEAD

    def body(o_ref, z_ref, w_ref, dy_ref, do_ref, dz_ref, dw_ref):
        ov, zv, dyv = o_ref[...], z_ref[...], dy_ref[...]
        r = lax.rsqrt(jnp.mean(ov * ov, axis=-1, keepdims=True) + EPS)
        xh = ov * r
        dn = dyv * _silu(zv)
        dz_ref[...] = (dyv * xh * w_ref[...] * _dsilu(zv)).astype(dz_ref.dtype)
        dnw = dn * w_ref[...]
        do_ref[...] = r * (dnw - xh * jnp.mean(dnw * xh, axis=-1, keepdims=True))

        @pl.when((pl.program_id(0) == 0) & (pl.program_id(1) == 0))
        def _():
            dw_ref[...] = jnp.zeros_like(dw_ref)

        dw_ref[...] += (dn * xh).reshape(tm // 8, 8, HEAD).sum(axis=0)

    blk = pl.BlockSpec((tm, HEAD), lambda i, h: (i, h))
    return _pcall(body, name=name, grid=(t // tm, GDN_HEADS),
                  in_specs=[blk, pl.BlockSpec((tm, HEAD), lambda i, h: (i, zb + h)), pl.BlockSpec((1, HEAD), lambda i, h: (0, 0)), blk],
                  out_specs=[blk, blk, pl.BlockSpec((8, HEAD), lambda i, h: (0, 0))],
                  out_shape=[jax.ShapeDtypeStruct((t, GDN_QK), F32), jax.ShapeDtypeStruct((t, GDN_QK), BF16),
                             jax.ShapeDtypeStruct((8, HEAD), F32)])(o_raw, proj, w, dmixed)


def _rope_tables():
    half = QK_ROPE // 2
    inv = ROPE_THETA ** (-jnp.arange(half, dtype=F32) / half)
    zeros = jnp.zeros((HEAD - QK_ROPE,), F32)
    inv_row = jnp.concatenate([inv, inv, zeros])
    sign_row = jnp.concatenate([-jnp.ones((half,), F32), jnp.ones((half,), F32), zeros])
    mask_row = jnp.concatenate([jnp.ones((QK_ROPE,), F32), zeros])
    return jnp.concatenate([inv_row[None], sign_row[None], mask_row[None], jnp.zeros((5, HEAD), F32)], axis=0)


def _rope_cs(pos, tab, *, name):
    t = pos.shape[0]
    tm = _pick(t, 1024)

    def body(pos_ref, tab_ref, o_ref):
        tab = tab_ref[...]
        ang = pos_ref[...] * tab[0:1]
        o_ref[...] = jnp.concatenate([jnp.cos(ang) * tab[2:3], jnp.sin(ang) * tab[1:2]], axis=1)

    return _pcall(body, name=name, grid=(t // tm,),
                  in_specs=[pl.BlockSpec((tm, 1), lambda i: (i, 0)), pl.BlockSpec((8, HEAD), lambda i: (0, 0))],
                  out_specs=pl.BlockSpec((tm, 2 * HEAD), lambda i: (i, 0)),
                  out_shape=jax.ShapeDtypeStruct((t, 2 * HEAD), F32))(pos, tab)


def _rotate(x, cs, sign):
    lane = lax.broadcasted_iota(jnp.int32, x.shape, 1)
    half = QK_ROPE // 2
    partner = jnp.where(lane < half, pltpu.roll(x, HEAD - half, axis=1), pltpu.roll(x, half, axis=1))
    return x * cs[:, :HEAD] + partner * (cs[:, HEAD:] * sign)


def _q_rot(q, cs, *, name, sign, out_dtype=BF16):
    t = q.shape[0]
    tm = _pick(t, 1024)
    scale = (HEAD + QK_ROPE) ** -0.5

    def body(q_ref, cs_ref, o_ref):
        qv = q_ref[...].astype(F32)
        rot = _rotate(qv[:, HEAD:], cs_ref[...], sign)
        o_ref[...] = (jnp.concatenate([qv[:, :HEAD], rot], axis=1) * scale).astype(o_ref.dtype)

    blk = pl.BlockSpec((tm, QHEAD), lambda i, h: (i, h))
    return _pcall(body, name=name, grid=(t // tm, MLA_HEADS),
                  in_specs=[blk, pl.BlockSpec((tm, 2 * HEAD), lambda i, h: (i, 0))],
                  out_specs=blk, out_shape=jax.ShapeDtypeStruct((t, MLA_HEADS * QHEAD), out_dtype))(q, cs)


def _q_up(cqn, wuq_p, cs, *, name):
    t, lora = cqn.shape
    tm = _pick(t, 1024)
    scale = (HEAD + QK_ROPE) ** -0.5

    def body(a_ref, w_ref, cs_ref, o_ref):
        qv = lax.dot_general(a_ref[...], w_ref[...], (NN, ((), ())), preferred_element_type=F32)
        rot = _rotate(qv[:, HEAD:], cs_ref[...], 1.0)
        o_ref[...] = (jnp.concatenate([qv[:, :HEAD], rot], axis=1) * scale).astype(o_ref.dtype)

    return _pcall(body, name=name, grid=(t // tm, MLA_HEADS),
                  in_specs=[pl.BlockSpec((tm, lora), lambda i, h: (i, 0)), pl.BlockSpec((lora, QHEAD), lambda i, h: (0, h)),
                            pl.BlockSpec((tm, 2 * HEAD), lambda i, h: (i, 0))],
                  out_specs=pl.BlockSpec((tm, QHEAD), lambda i, h: (i, h)),
                  out_shape=jax.ShapeDtypeStruct((t, MLA_HEADS * QHEAD), BF16))(cqn, wuq_p, cs)


def _kv_up(ckvn, wukv, proj, cs, *, name):
    t, lora = ckvn.shape
    tm = _pick(t, 1024)

    def body(a_ref, w_ref, m_ref, cs_ref, k_ref, v_ref):
        kvv = lax.dot_general(a_ref[...], w_ref[...], (NN, ((), ())), preferred_element_type=F32)
        misc = m_ref[...]
        lane = lax.broadcasted_iota(jnp.int32, misc.shape, 1)
        rot = _rotate(jnp.where(lane < QK_ROPE, misc, 0.0), cs_ref[...], 1.0)
        k_ref[...] = jnp.concatenate([kvv[:, :HEAD], rot], axis=1).astype(k_ref.dtype)
        v_ref[...] = kvv[:, HEAD:].astype(v_ref.dtype)

    return _pcall(body, name=name, grid=(t // tm, MLA_HEADS),
                  in_specs=[pl.BlockSpec((tm, lora), lambda i, h: (i, 0)), pl.BlockSpec((lora, QHEAD), lambda i, h: (0, h)),
                            pl.BlockSpec((tm, HEAD), lambda i, h: (i, COL_MISC // HEAD)),
                            pl.BlockSpec((tm, 2 * HEAD), lambda i, h: (i, 0))],
                  out_specs=[pl.BlockSpec((tm, QHEAD), lambda i, h: (i, h)), pl.BlockSpec((tm, HEAD), lambda i, h: (i, h))],
                  out_shape=[jax.ShapeDtypeStruct((t, MLA_HEADS * QHEAD), BF16), jax.ShapeDtypeStruct((t, MLA_HEADS * HEAD), BF16)],
                  )(ckvn, wukv, proj, cs)


def _krope_bwd(dkr, cs, *, name):
    t = dkr.shape[0]
    tm = _pick(t, 512)

    def body(d_ref, cs_ref, o_ref):
        d = d_ref[...]
        acc = d[:, :HEAD]
        for h in range(1, MLA_HEADS):
            acc = acc + d[:, h * HEAD:(h + 1) * HEAD]
        o_ref[...] = _rotate(acc, cs_ref[...], -1.0)

    return _pcall(body, name=name, grid=(t // tm,),
                  in_specs=[pl.BlockSpec((tm, MLA_HEADS * HEAD), lambda i: (i, 0)), pl.BlockSpec((tm, 2 * HEAD), lambda i: (i, 0))],
                  out_specs=pl.BlockSpec((tm, HEAD), lambda i: (i, 0)),
                  out_shape=jax.ShapeDtypeStruct((t, HEAD), F32))(dkr, cs)


NEG = -1e30


def _tri(step, counts):
    starts = [sum(counts[:o]) for o in range(len(counts))]
    outer = sum([(step >= s).astype(jnp.int32) for s in starts[1:]], jnp.int32(0))
    start = sum([(step >= starts[o]).astype(jnp.int32) * (starts[o] - starts[o - 1]) for o in range(1, len(counts))], jnp.int32(0))
    return outer, step - start


def _attn_fwd(q, k, v, *, name, tq=1024, tk=1024, carry=None):
    t = q.shape[0]
    tq, tk = _pick(t, tq), _pick(t, tk)
    nq = t // tq
    last_kv = lambda i: (i * tq + tq - 1) // tk
    counts = [last_kv(i) + 1 for i in range(nq)]

    def body(q_ref, k_ref, v_ref, o_ref, lse_ref, m_ref, l_ref, acc_ref):
        i, j = _tri(pl.program_id(1), counts)

        @pl.when(j == 0)
        def _():
            m_ref[...] = jnp.full_like(m_ref, NEG)
            l_ref[...] = jnp.zeros_like(l_ref)
            acc_ref[...] = jnp.zeros_like(acc_ref)

        def step(masked):
            s = lax.dot_general(q_ref[...], k_ref[...], (NT, ((), ())), preferred_element_type=F32)
            if masked:
                qpos = i * tq + lax.broadcasted_iota(jnp.int32, s.shape, 0)
                kpos = j * tk + lax.broadcasted_iota(jnp.int32, s.shape, 1)
                s = jnp.where(kpos <= qpos, s, NEG)
            m_prev = m_ref[...]
            m_new = jnp.maximum(m_prev, jnp.max(s, axis=1, keepdims=True))
            alpha = jnp.exp(m_prev - m_new)
            p = jnp.exp(s - m_new)
            l_ref[...] = alpha * l_ref[...] + jnp.sum(p, axis=1, keepdims=True)
            acc_ref[...] = alpha * acc_ref[...] + lax.dot_general(p.astype(BF16), v_ref[...], (NN, ((), ())),
                                                                  preferred_element_type=F32)
            m_ref[...] = m_new

        crosses = j * tk + tk - 1 > i * tq

        @pl.when(crosses)
        def _():
            step(True)

        @pl.when(jnp.logical_not(crosses))
        def _():
            step(False)

        @pl.when(j == last_kv(i))
        def _():
            o_ref[...] = acc_ref[...] / l_ref[...]
            lse_ref[...] = jnp.broadcast_to(m_ref[...] + jnp.log(l_ref[...]), lse_ref.shape)

    qblk = pl.BlockSpec((tq, QHEAD), lambda h, s: (_tri(s, counts)[0], h))
    oblk = pl.BlockSpec((tq, HEAD), lambda h, s: (_tri(s, counts)[0], h))
    return _pcall(body, name=name, grid=(MLA_HEADS, sum(counts)),
                  in_specs=[qblk, pl.BlockSpec((tk, QHEAD), lambda h, s: (_tri(s, counts)[1], h)),
                            pl.BlockSpec((tk, HEAD), lambda h, s: (_tri(s, counts)[1], h))],
                  out_specs=[oblk, oblk],
                  out_shape=[jax.ShapeDtypeStruct((t, MLA_HEADS * HEAD), F32), jax.ShapeDtypeStruct((t, MLA_HEADS * HEAD), F32)],
                  scratch=[pltpu.VMEM((tq, 1), F32), pltpu.VMEM((tq, 1), F32), pltpu.VMEM((tq, HEAD), F32)],
                  carry=carry)(q, k, v)


def _attn_bwd(q, k, v, do, lse, delta, *, name, tq=1024, tk=1024, carry=None):
    t = q.shape[0]
    tq, tk = _pick(t, tq), _pick(t, tk)
    nq, nk = t // tq, t // tk
    first_q = lambda j: (j * tk) // tq
    counts = [nq - first_q(j) for j in range(nk)]

    def where(step):
        j, off = _tri(step, counts)
        return j, first_q(j) + off

    lanes = lambda col: jnp.tile(col, (1, tk // HEAD))

    def body(q_ref, k_ref, v_ref, do_ref, lse_ref, dl_ref, dq_ref, dkv_ref, dkr_ref, dk_acc, dv_acc):
        j, i = where(pl.program_id(1))

        @pl.when(i == first_q(j))
        def _():
            dk_acc[...] = jnp.zeros_like(dk_acc)
            dv_acc[...] = jnp.zeros_like(dv_acc)

        def step(masked):
            qv, kv_, dov = q_ref[...], k_ref[...], do_ref[...].astype(BF16)
            s = lax.dot_general(qv, kv_, (NT, ((), ())), preferred_element_type=F32)
            p = jnp.exp((s - lanes(lse_ref[...])).astype(BF16))
            if masked:
                qpos = i * tq + lax.broadcasted_iota(jnp.int32, s.shape, 0)
                kpos = j * tk + lax.broadcasted_iota(jnp.int32, s.shape, 1)
                p = jnp.where(kpos <= qpos, p, jnp.zeros_like(p))
            dv_acc[...] += lax.dot_general(p, dov, (TN, ((), ())), preferred_element_type=F32)
            dp = lax.dot_general(dov, v_ref[...], (NT, ((), ())), preferred_element_type=F32)
            ds = p * (dp - lanes(dl_ref[...])).astype(BF16)
            dk_acc[...] += lax.dot_general(ds, qv, (TN, ((), ())), preferred_element_type=F32)
            contrib = lax.dot_general(ds, kv_, (NN, ((), ())), preferred_element_type=F32)
            rows = pl.ds(pl.multiple_of(i * tq, tq), tq)

            @pl.when(j == 0)
            def _():
                dq_ref[rows, :] = contrib

            @pl.when(j > 0)
            def _():
                dq_ref[rows, :] += contrib

        crosses = j * tk + tk - 1 > i * tq

        @pl.when(crosses)
        def _():
            step(True)

        @pl.when(jnp.logical_not(crosses))
        def _():
            step(False)

        @pl.when(i == nq - 1)
        def _():
            dk = dk_acc[...]
            dkv_ref[...] = jnp.concatenate([dk[:, :HEAD], dv_acc[...]], axis=1).astype(dkv_ref.dtype)
            dkr_ref[...] = dk[:, HEAD:]

    qi = lambda h, s: (where(s)[1], h)
    kj = lambda h, s: (where(s)[0], h)
    return _pcall(body, name=name, grid=(MLA_HEADS, sum(counts)),
                  in_specs=[pl.BlockSpec((tq, QHEAD), qi), pl.BlockSpec((tk, QHEAD), kj), pl.BlockSpec((tk, HEAD), kj),
                            pl.BlockSpec((tq, HEAD), qi), pl.BlockSpec((tq, HEAD), qi), pl.BlockSpec((tq, HEAD), qi)],
                  out_specs=[pl.BlockSpec((t, QHEAD), lambda h, s: (0, h)), pl.BlockSpec((tk, QHEAD), kj),
                             pl.BlockSpec((tk, HEAD), kj)],
                  out_shape=[jax.ShapeDtypeStruct((t, MLA_HEADS * QHEAD), F32), jax.ShapeDtypeStruct((t, MLA_HEADS * QHEAD), BF16),
                             jax.ShapeDtypeStruct((t, MLA_HEADS * HEAD), F32)],
                  scratch=[pltpu.VMEM((tk, QHEAD), F32), pltpu.VMEM((tk, HEAD), F32)], carry=carry)(q, k, v, do, lse, delta)


def _ffn_up(h, wgate, wup, *, name, bm=512, bn=FF_WIDE):
    t = h.shape[0]
    bm = _pick(t, bm)

    def body(h_ref, wg_ref, wu_ref, g_ref, u_ref, a_ref):
        hv = h_ref[...]
        g = lax.dot_general(hv, wg_ref[...], (NN, ((), ())), preferred_element_type=F32)
        u = lax.dot_general(hv, wu_ref[...], (NN, ((), ())), preferred_element_type=F32)
        g_ref[...] = g.astype(g_ref.dtype)
        u_ref[...] = u.astype(u_ref.dtype)
        a_ref[...] = (_silu(g) * u).astype(a_ref.dtype)

    w_spec = pl.BlockSpec((D_MODEL, bn), lambda j, i: (0, j))
    o_spec = pl.BlockSpec((bm, bn), lambda j, i: (i, j))
    sds = jax.ShapeDtypeStruct((t, D_FF), BF16)
    return _pcall(body, name=name, grid=(D_FF // bn, t // bm),
                  in_specs=[pl.BlockSpec((bm, D_MODEL), lambda j, i: (i, 0)), w_spec, w_spec],
                  out_specs=[o_spec] * 3, out_shape=[sds] * 3)(h, wgate, wup)


def _ffn_down_dx(dy, wdown, gate, up, *, name, bm=512, bn=FF_WIDE):
    t = dy.shape[0]
    bm = _pick(t, bm)

    def body(dy_ref, w_ref, g_ref, u_ref, dg_ref, du_ref):
        d = lax.dot_general(dy_ref[...].astype(BF16), w_ref[...], (NT, ((), ())), preferred_element_type=F32)
        g = g_ref[...].astype(F32)
        dg_ref[...] = (d * u_ref[...].astype(F32) * _dsilu(g)).astype(dg_ref.dtype)
        du_ref[...] = (d * _silu(g)).astype(du_ref.dtype)

    o_spec = pl.BlockSpec((bm, bn), lambda j, i: (i, j))
    sds = jax.ShapeDtypeStruct((t, D_FF), BF16)
    return _pcall(body, name=name, grid=(D_FF // bn, t // bm),
                  in_specs=[pl.BlockSpec((bm, D_MODEL), lambda j, i: (i, 0)), pl.BlockSpec((bn, D_MODEL), lambda j, i: (j, 0)),
                            o_spec, o_spec],
                  out_specs=[o_spec, o_spec], out_shape=[sds, sds])(dy, wdown, gate, up)


def _loss_bwd(x2, w, target, *, name):
    t = x2.shape[0]
    tm = _rows(t, D_MODEL)

    def body(x_ref, w_ref, t_ref, dx_ref, dw_ref, l_ref):
        xv, wv = x_ref[...], w_ref[...]
        r = lax.rsqrt(jnp.mean(xv * xv, axis=-1, keepdims=True) + EPS)
        xh = xv * r
        err = xh * wv - t_ref[...]
        dy = err * (1.0 / D_MODEL)
        dyw = dy * wv
        dx_ref[...] = r * (dyw - xh * jnp.mean(dyw * xh, axis=-1, keepdims=True))

        @pl.when(pl.program_id(0) == 0)
        def _():
            dw_ref[...] = jnp.zeros_like(dw_ref)
            l_ref[...] = jnp.zeros_like(l_ref)

        dw_ref[...] += (dy * xh).reshape(tm // 8, 8, D_MODEL).sum(axis=0)
        sq = (err * err).reshape(tm // 8, 8, D_MODEL).sum(axis=0)
        part = sq[:, :HEAD]
        for c in range(1, D_MODEL // HEAD):
            part = part + sq[:, c * HEAD:(c + 1) * HEAD]
        l_ref[...] += part * (0.5 / D_MODEL)

    row = pl.BlockSpec((tm, D_MODEL), lambda i: (i, 0))
    return _pcall(body, name=name, grid=(t // tm,),
                  in_specs=[row, pl.BlockSpec((1, D_MODEL), lambda i: (0, 0)), row],
                  out_specs=[row, pl.BlockSpec((8, D_MODEL), lambda i: (0, 0)), pl.BlockSpec((8, HEAD), lambda i: (0, 0))],
                  out_shape=[jax.ShapeDtypeStruct((t, D_MODEL), F32), jax.ShapeDtypeStruct((8, D_MODEL), F32),
                             jax.ShapeDtypeStruct((8, HEAD), F32)])(x2, w, target)


def _unshard_cols(g):
    return jnp.transpose(g, (1, 0, 2)).reshape(g.shape[1], N_DEV * g.shape[2])


def _shard_cols(w):
    return jnp.transpose(w.reshape(w.shape[0], N_DEV, w.shape[1] // N_DEV), (1, 0, 2))


_WIN_ORDER = ((0, 4096), (4112, 5136), (5136, 5200), (4096, 4112))
_WIN_SHARD = IN_WIDTH // N_DEV


def _win_pieces():
    out, pos = [], 0
    for a, b in _WIN_ORDER:
        c = a
        while c < b:
            dev, off = divmod(c, _WIN_SHARD)
            width = min(b, (dev + 1) * _WIN_SHARD) - c
            out.append((dev, off, width, pos))
            c, pos = c + width, pos + width
    return out


def _win_gathered_to_padded(g):
    pieces = [g[dev][:, off:off + width] for dev, off, width, _ in _win_pieces()]
    return jnp.concatenate(pieces + [jnp.zeros((g.shape[1], PROJ_W - IN_WIDTH), g.dtype)], axis=1)


def _win_padded_to_shards(d):
    shards = []
    for dev in range(N_DEV):
        mine = sorted((off, width, pos) for dv, off, width, pos in _win_pieces() if dv == dev)
        shards.append(jnp.concatenate([d[:, pos:pos + width] for _, width, pos in mine], axis=1))
    return jnp.stack(shards)


def _wuq_to_padded(w):
    w3 = w.reshape(w.shape[0], MLA_HEADS, HEAD + QK_ROPE)
    return jnp.pad(w3, ((0, 0), (0, 0), (0, QHEAD - HEAD - QK_ROPE))).reshape(w.shape[0], MLA_HEADS * QHEAD)


def _wuq_from_padded(d):
    return d.reshape(d.shape[0], MLA_HEADS, QHEAD)[:, :, :HEAD + QK_ROPE].reshape(d.shape[0], MLA_HEADS * (HEAD + QK_ROPE))


def _late_weights(g_out, g_gate, g_up, g_down):
    return g_out.reshape(D_MODEL, D_MODEL), _unshard_cols(g_gate), _unshard_cols(g_up), g_down.reshape(D_FF, D_MODEL)


def _local_step(x, pos, target, win_p, wuq_p, wukv, late, conv_w, small, exchange):
    cs = _rope_cs(pos, _rope_tables(), name="rope_cs")
    if not exchange:
        wout, wgate, wup, wdown = late
    h1 = _rms_fwd(x, small["attn_norm_w"], name="rms1_fwd", width=D_MODEL)
    proj = _mm(h1, win_p, name="mm_in", bn=768)
    qkv = _conv_fwd(proj, conv_w, name="conv_fwd")
    o_gdn_raw, hist = _gdn_fwd(qkv, proj, small["gdn_params"], name="gdn_fwd")
    o_gdn = _gate_fwd(o_gdn_raw, proj, small["gdn_norm_w"], name="gate_fwd")
    cqn = _rms_fwd(proj, small["q_norm_w"], name="rmsq_fwd", width=Q_LORA, col0=COL_CQ)
    ckvn = _rms_fwd(proj, small["kv_norm_w"], name="rmskv_fwd", width=KV_LORA, col0=COL_CKV)
    q_full = _q_up(cqn, wuq_p, cs, name="q_up")
    k_full, v_b = _kv_up(ckvn, wukv, proj, cs, name="kv_up")
    if exchange:
        (o_mla_raw, lse), gathered = _attn_fwd(q_full, k_full, v_b, name="attn_fwd", carry=_Gather(late))
        wout, wgate, wup, wdown = _late_weights(*gathered)
    else:
        o_mla_raw, lse = _attn_fwd(q_full, k_full, v_b, name="attn_fwd")
    o_mla = _rms_fwd(o_mla_raw, small["mla_out_norm_w"], name="rmso_fwd", width=HEAD, heads=MLA_HEADS)
    mixed = jnp.concatenate([o_gdn, o_mla], axis=1)
    x1 = _mm(mixed, wout, name="mm_out", res=x)
    h2 = _rms_fwd(x1, small["ffn_norm_w"], name="rms2_fwd", width=D_MODEL)
    gate, up, act = _ffn_up(h2, wgate, wup, name="ffn_up")
    x2 = _mm(act, wdown, name="mm_down", res=x1, bk=FF_WIDE)
    dx2, dw_final, loss_part = _loss_bwd(x2, small["final_norm_w"], target, name="loss_bwd")
    dgate, dup = _ffn_down_dx(dx2, wdown, gate, up, name="ffn_down_dx")
    d_wdown = _mm(act, dx2, name="mm_down_dw", ta=True, out_dtype=BF16, bm=FF_WIDE)
    dh2 = _mm(dgate, wgate, name="mm_gateup_dx", tb=True, bk=FF_WIDE, pair=(dup, wup))
    d_wgate, d_wup = _mm_dw2(h2, dgate, dup, name="mm_gateup_dw")
    dx1, dw_ffn = _rms_bwd(x1, small["ffn_norm_w"], dh2, name="rms2_bwd", width=D_MODEL, res=dx2)
    dmixed = _mm(dx1, wout, name="mm_out_dx", tb=True)
    d_wout = _mm(mixed, dx1, name="mm_out_dw", ta=True, out_dtype=BF16)
    do_mla, dw_mla_out, delta = _rms_bwd(o_mla_raw, small["mla_out_norm_w"], dmixed, name="rmso_bwd", width=HEAD,
                                         heads=MLA_HEADS, dcol0=GDN_QK, with_delta=True, out_dtype=BF16)
    if exchange:
        send = [d_wdown.reshape(N_DEV, D_FF // N_DEV, D_MODEL), _shard_cols(d_wgate), _shard_cols(d_wup)]
        (dq_full, dkv, dkr_h), (r_down, r_gate, r_up) = _attn_bwd(q_full, k_full, v_b, do_mla, lse, delta, name="attn_bwd",
                                                                  carry=_Exchange(send, [False] * 3))
    else:
        dq_full, dkv, dkr_h = _attn_bwd(q_full, k_full, v_b, do_mla, lse, delta, name="attn_bwd")
    dq_pre = _q_rot(dq_full, cs, name="q_rot_bwd", sign=-1.0)
    dmisc_kr = _krope_bwd(dkr_h, cs, name="krope_bwd")
    dcqn = _mm(dq_pre, wuq_p, name="mm_uq_dx", tb=True)
    d_wuq = _mm(cqn, dq_pre, name="mm_uq_dw", ta=True, out_dtype=BF16)
    dckvn = _mm(dkv, wukv, name="mm_ukv_dx", tb=True)
    d_wukv = _mm(ckvn, dkv, name="mm_ukv_dw", ta=True, out_dtype=BF16)
    dcq, dw_qn = _rms_bwd(proj, small["q_norm_w"], dcqn, name="rmsq_bwd", width=Q_LORA, col0=COL_CQ, out_dtype=BF16)
    dckv, dw_kvn = _rms_bwd(proj, small["kv_norm_w"], dckvn, name="rmskv_bwd", width=KV_LORA, col0=COL_CKV, out_dtype=BF16)
    do_gdn, dz, dw_gdn = _gate_bwd(o_gdn_raw, proj, small["gdn_norm_w"], dmixed, name="gate_bwd")
    if exchange:
        send = [d_wout.reshape(N_DEV, D_MODEL // N_DEV, D_MODEL), _shard_cols(_wuq_from_padded(d_wuq)), _shard_cols(d_wukv)]
        (dqkv, dmisc, d_params), (r_out, r_uq, r_ukv) = _gdn_bwd(
            qkv, proj, small["gdn_params"], hist, do_gdn, dmisc_kr, name="gdn_bwd", carry=_Exchange(send, [False] * 3))
    else:
        dqkv, dmisc, d_params = _gdn_bwd(qkv, proj, small["gdn_params"], hist, do_gdn, dmisc_kr, name="gdn_bwd")
    dqkv_pre, dconv = _conv_bwd(proj, conv_w, dqkv, name="conv_bwd")
    dproj = jnp.concatenate([dqkv_pre, dz, dcq, dckv, dmisc.astype(BF16), jnp.zeros((x.shape[0], PROJ_W - COL_MISC - HEAD), BF16)], axis=1)
    d_win = _mm(h1, dproj, name="mm_in_dw", ta=True, out_dtype=BF16, bn=768)
    if exchange:
        dh1, (r_in,) = _mm(dproj, win_p, name="mm_in_dx", tb=True, bk=768,
                           carry=_Exchange([_win_padded_to_shards(d_win)], [False]))
        d_win = r_in
    else:
        dh1 = _mm(dproj, win_p, name="mm_in_dx", tb=True, bk=768)
    dx, dw_attn = _rms_bwd(x, small["attn_norm_w"], dh1, name="rms1_bwd", width=D_MODEL, res=dx1)

    if exchange:
        big = {"w_in": d_win, "w_uq": r_uq, "w_ukv": r_ukv, "w_out": r_out, "w_gate": r_gate, "w_up": r_up, "w_down": r_down}
    else:
        big = {"w_in": d_win, "w_uq": d_wuq, "w_ukv": d_wukv, "w_out": d_wout, "w_gate": d_wgate, "w_up": d_wup,
               "w_down": d_wdown}
    sm = {"attn_norm_w": dw_attn, "ffn_norm_w": dw_ffn, "final_norm_w": dw_final, "q_norm_w": dw_qn, "kv_norm_w": dw_kvn,
          "gdn_norm_w": dw_gdn, "mla_out_norm_w": dw_mla_out, "gdn_params": d_params, "conv_w": dconv, "loss": loss_part}
    return dx, big, sm


def _exchange(ex, *, name):
    def body(*refs):
        ins, outs, sems = refs[:ex.n], refs[ex.n:2 * ex.n], refs[2 * ex.n:]
        ex.start(ins, outs, sems)
        ex.forward(ins, outs, sems)
        ex.finish(ins, outs, sems)

    any_spec = pl.BlockSpec(memory_space=pl.ANY)
    return pl.pallas_call(body, name=name, in_specs=[any_spec] * ex.n, out_specs=[any_spec] * ex.n,
                          out_shape=ex.out_shape(), scratch_shapes=ex.sems())(*ex.arrays)


def _adamw_math(g, w, m, v):
    m = ADAM_B1 * m + (1.0 - ADAM_B1) * g
    v = ADAM_B2 * v + (1.0 - ADAM_B2) * (g * g)
    m_hat = m / (1.0 - ADAM_B1 ** ADAM_STEP)
    v_hat = v / (1.0 - ADAM_B2 ** ADAM_STEP)
    delta = -ADAM_LR * (m_hat / (jnp.sqrt(v_hat) + ADAM_EPS) + ADAM_WD * w)
    return delta, m, v


def _adamw(parts, w, m, v, *, name):
    npart, r, c = parts.shape
    tr = r if r * c * 4 <= (1 << 20) else _rows(r, c, 1 << 20)

    def body(p_ref, w_ref, m_ref, v_ref, g_ref, d_ref, nm_ref, nv_ref):
        g = p_ref[0].astype(F32)
        for s in range(1, npart):
            g = g + p_ref[s].astype(F32)
        g_ref[...] = g
        d_ref[...], nm_ref[...], nv_ref[...] = _adamw_math(g, w_ref[...], m_ref[...], v_ref[...])

    blk = pl.BlockSpec((tr, c), lambda i: (i, 0))
    sds = jax.ShapeDtypeStruct((r, c), F32)
    return _pcall(body, name=name, grid=(r // tr,),
                  in_specs=[pl.BlockSpec((npart, tr, c), lambda i: (0, i, 0)), blk, blk, blk],
                  out_specs=[blk] * 4, out_shape=[sds] * 4)(parts, w, m, v)


def _sum_parts(parts, *, name):
    npart, r, c = parts.shape

    def body(p_ref, o_ref):
        g = p_ref[0]
        for s in range(1, npart):
            g = g + p_ref[s]
        o_ref[...] = g

    return _pcall(body, name=name, grid=(1,), in_specs=[pl.BlockSpec((npart, r, c), lambda i: (0, 0, 0))],
                  out_specs=pl.BlockSpec((r, c), lambda i: (0, 0)), out_shape=jax.ShapeDtypeStruct((r, c), F32))(parts)


_SMALL = (("attn_norm_w", D_MODEL), ("ffn_norm_w", D_MODEL), ("final_norm_w", D_MODEL), ("q_norm_w", Q_LORA),
          ("kv_norm_w", KV_LORA), ("gdn_norm_w", HEAD), ("mla_out_norm_w", HEAD), ("a_log", HEAD), ("dt_bias", HEAD))
_SMALL_ROWS = sum(n for _, n in _SMALL) // HEAD
_CONV_ROWS = GDN_CONV * CONV_CH // HEAD
_PACK_ROWS = 160


def _pad_lanes(v, n):
    v = v.reshape(-1)
    return jnp.concatenate([v, jnp.zeros((n - v.shape[0],), v.dtype)])


def kernel(x, positions, attn_norm_w, w_in, conv_w, a_log, dt_bias, gdn_norm_w, q_norm_w, w_uq, kv_norm_w, w_ukv, mla_out_norm_w, w_out, ffn_norm_w, w_gate, w_up, w_down, final_norm_w, loss_target, m_attn_norm_w, m_w_in, m_conv_w, m_a_log, m_dt_bias, m_gdn_norm_w, m_q_norm_w, m_w_uq, m_kv_norm_w, m_w_ukv, m_mla_out_norm_w, m_w_out, m_ffn_norm_w, m_w_gate, m_w_up, m_w_down, m_final_norm_w, v_attn_norm_w, v_w_in, v_conv_w, v_a_log, v_dt_bias, v_gdn_norm_w, v_q_norm_w, v_w_uq, v_kv_norm_w, v_w_ukv, v_mla_out_norm_w, v_w_out, v_ffn_norm_w, v_w_gate, v_w_up, v_w_down, v_final_norm_w):
    t = x.shape[1]
    me = 4 * lax.axis_index("x") + 2 * lax.axis_index("y") + lax.axis_index("c")
    weights = dict(attn_norm_w=attn_norm_w, w_in=w_in, conv_w=conv_w, a_log=a_log, dt_bias=dt_bias, gdn_norm_w=gdn_norm_w,
                   q_norm_w=q_norm_w, w_uq=w_uq, kv_norm_w=kv_norm_w, w_ukv=w_ukv, mla_out_norm_w=mla_out_norm_w, w_out=w_out,
                   ffn_norm_w=ffn_norm_w, w_gate=w_gate, w_up=w_up, w_down=w_down, final_norm_w=final_norm_w)
    mom_m = dict(attn_norm_w=m_attn_norm_w, w_in=m_w_in, conv_w=m_conv_w, a_log=m_a_log, dt_bias=m_dt_bias, gdn_norm_w=m_gdn_norm_w,
                 q_norm_w=m_q_norm_w, w_uq=m_w_uq, kv_norm_w=m_kv_norm_w, w_ukv=m_w_ukv, mla_out_norm_w=m_mla_out_norm_w,
                 w_out=m_w_out, ffn_norm_w=m_ffn_norm_w, w_gate=m_w_gate, w_up=m_w_up, w_down=m_w_down, final_norm_w=m_final_norm_w)
    mom_v = dict(attn_norm_w=v_attn_norm_w, w_in=v_w_in, conv_w=v_conv_w, a_log=v_a_log, dt_bias=v_dt_bias, gdn_norm_w=v_gdn_norm_w,
                 q_norm_w=v_q_norm_w, w_uq=v_w_uq, kv_norm_w=v_kv_norm_w, w_ukv=v_w_ukv, mla_out_norm_w=v_mla_out_norm_w,
                 w_out=v_w_out, ffn_norm_w=v_ffn_norm_w, w_gate=v_w_gate, w_up=v_w_up, w_down=v_w_down, final_norm_w=v_final_norm_w)
    big_names = ("w_in", "w_uq", "w_ukv", "w_out", "w_gate", "w_up", "w_down")

    shard = {n: weights[n][0].astype(BF16) for n in big_names}
    g_in, g_uq, g_ukv, g_conv = _exchange(_Gather([shard["w_in"], shard["w_uq"], shard["w_ukv"], weights["conv_w"][0]]),
                                          name="gather_weights")
    win_p = _win_gathered_to_padded(g_in)
    wuq_p = _wuq_to_padded(_unshard_cols(g_uq))
    wukv = _unshard_cols(g_ukv)
    late = [shard["w_out"], shard["w_gate"], shard["w_up"], shard["w_down"]]
    conv_full = jnp.concatenate([_unshard_cols(g_conv), jnp.zeros((8 - GDN_CONV, CONV_CH), F32)], axis=0)

    gdn_params = jnp.concatenate([_pad_lanes(a_log, HEAD)[None], _pad_lanes(dt_bias, HEAD)[None], jnp.zeros((6, HEAD), F32)], axis=0)
    small = {n: weights[n].reshape(1, -1) for n in ("attn_norm_w", "ffn_norm_w", "final_norm_w", "q_norm_w", "kv_norm_w",
                                                    "gdn_norm_w", "mla_out_norm_w")}
    small["gdn_params"] = gdn_params

    dx, big, sm = _local_step(x[0], positions.reshape(t, 1).astype(F32), loss_target[0], win_p, wuq_p, wukv, late,
                              conv_full, small, True)

    rows8 = lambda name: jnp.sum(sm[name], axis=0)
    pieces = [rows8(n) for n, _ in _SMALL[:7]]
    pieces += [_pad_lanes(jnp.sum(sm["gdn_params"][0:1], axis=0), HEAD), _pad_lanes(jnp.sum(sm["gdn_params"][1:2], axis=0), HEAD)]
    pieces.append(jnp.sum(sm["conv_w"], axis=1).reshape(-1))
    pieces.append(_pad_lanes(jnp.sum(sm["loss"]).reshape(1), HEAD))
    packed = _pad_lanes(jnp.concatenate(pieces), _PACK_ROWS * HEAD).reshape(_PACK_ROWS, HEAD)
    (r_small,) = _exchange(_Exchange([packed], [True]), name="exchange_small")

    outs_g, outs_d, outs_m, outs_v = {}, {}, {}, {}
    for name in big_names:
        g, d, nm, nv = _adamw(big[name], weights[name][0], mom_m[name][0], mom_v[name][0], name="adamw_" + name)
        outs_g[name], outs_d[name], outs_m[name], outs_v[name] = g[None], d[None], nm[None], nv[None]

    total = _sum_parts(r_small, name="sum_small")
    flat = total.reshape(-1)
    loss = flat[(_SMALL_ROWS + _CONV_ROWS) * HEAD]
    g_small, off = {}, 0
    for n, size in _SMALL:
        g_small[n] = flat[off:off + size]
        off += size
    g_conv_full = flat[off:off + GDN_CONV * CONV_CH].reshape(GDN_CONV, CONV_CH)
    g_small["conv_w"] = lax.dynamic_slice(g_conv_full, (0, me * (CONV_CH // N_DEV)), (GDN_CONV, CONV_CH // N_DEV)).reshape(-1)
    order = [n for n, _ in _SMALL] + ["conv_w"]
    sizes = dict(_SMALL)
    sizes["conv_w"] = GDN_CONV * CONV_CH // N_DEV
    true_size = {n: weights[n].size for n in order}

    def pack(d):
        return jnp.concatenate([_pad_lanes(d[n], sizes[n]) for n in order]).reshape(1, -1, HEAD)

    g2, d2, m2, v2 = _adamw(pack(g_small), pack(weights)[0], pack(mom_m)[0], pack(mom_v)[0], name="adamw_small")
    off = 0
    for n in order:
        for src, dst in ((g2, outs_g), (d2, outs_d), (m2, outs_m), (v2, outs_v)):
            dst[n] = src.reshape(-1)[off:off + true_size[n]].reshape(weights[n].shape)
        off += sizes[n]

    names = ("attn_norm_w", "w_in", "conv_w", "a_log", "dt_bias", "gdn_norm_w", "q_norm_w", "w_uq", "kv_norm_w", "w_ukv",
             "mla_out_norm_w", "w_out", "ffn_norm_w", "w_gate", "w_up", "w_down", "final_norm_w")
    return (loss, dx[None], *[outs_g[n] for n in names], *[outs_d[n] for n in names], *[outs_m[n] for n in names],
            *[outs_v[n] for n in names])
```

```python
import functools
import math

import jax
import jax.numpy as jnp
from jax import lax
from jax.experimental import pallas as pl
from jax.experimental.pallas import tpu as pltpu

F32 = jnp.float32
BF16 = jnp.bfloat16

D_MODEL = 2048
GDN_HEADS = 8
HEAD = 128
GDN_CONV = 4
GDN_CHUNK = 64
GDN_QK = GDN_HEADS * HEAD
CONV_CH = 3 * GDN_QK
MLA_HEADS = 8
QK_ROPE = 64
Q_LORA = 512
KV_LORA = 512
ROPE_THETA = 10000.0
D_FF = 5632
EPS = 1e-6
IN_WIDTH = 5200
ADAM_LR, ADAM_B1, ADAM_B2, ADAM_EPS, ADAM_WD, ADAM_STEP = 0.001, 0.9, 0.999, 1e-08, 0.01, 10

PROJ_W = 5376
COL_Z = 3072
COL_CQ = 4096
COL_CKV = 4608
COL_MISC = 5120
LANE_B = 64
LANE_A = 72
QHEAD = 256
FF_WIDE = D_FF // 4
N_DEV = 8
MESH = pl.DeviceIdType.MESH
VMEM_LIMIT_MB = 48

NN = ((1,), (0,))
NT = ((1,), (1,))
TN = ((0,), (0,))


def _my_place():
    x, y, c = lax.axis_index("x"), lax.axis_index("y"), lax.axis_index("c")
    return x, y, c, 4 * x + 2 * y + c


def _peer(x, y, c, p):
    px, py, pc = x ^ ((p >> 2) & 1), y ^ ((p >> 1) & 1), c ^ (p & 1)
    return (px, py, pc), 4 * px + 2 * py + pc


class _Exchange:
    def __init__(self, arrays, gather):
        self.arrays, self.gather, self.n = list(arrays), list(gather), len(arrays)

    def out_shape(self):
        return [jax.ShapeDtypeStruct(((N_DEV,) + a.shape) if g else a.shape, a.dtype)
                for a, g in zip(self.arrays, self.gather)]

    def sems(self):
        return [pltpu.SemaphoreType.DMA((self.n * (N_DEV - 1),)), pltpu.SemaphoreType.DMA((self.n * (N_DEV - 1),)),
                pltpu.SemaphoreType.DMA((self.n,))]

    def _copies(self, ins, outs, sems):
        send_sems, recv_sems, local_sems = sems
        x, y, c, me = _my_place()
        local = [pltpu.make_async_copy(ins[k] if self.gather[k] else ins[k].at[me], outs[k].at[me], local_sems.at[k])
                 for k in range(self.n)]
        sent, received = [], []
        for p in range(1, N_DEV):
            place, num = _peer(x, y, c, p)
            for k in range(self.n):
                src = ins[k] if self.gather[k] else ins[k].at[num]
                idx = k * (N_DEV - 1) + p - 1
                mk = lambda dst: pltpu.make_async_remote_copy(src_ref=src, dst_ref=dst, send_sem=send_sems.at[idx],
                                                              recv_sem=recv_sems.at[idx], device_id=place, device_id_type=MESH)
                sent.append(mk(outs[k].at[me]))
                received.append(mk(outs[k].at[num]))
        return local, sent, received

    def start(self, ins, outs, sems):
        local, sent, _ = self._copies(ins, outs, sems)
        for cp in local + sent:
            cp.start()

    def forward(self, ins, outs, sems):
        pass

    def finish(self, ins, outs, sems):
        local, sent, received = self._copies(ins, outs, sems)
        for cp in received:
            cp.wait_recv()
        for cp in sent:
            cp.wait_send()
        for cp in local:
            cp.wait()


class _Gather:
    def __init__(self, arrays):
        self.arrays, self.n = list(arrays), len(arrays)

    def out_shape(self):
        return [jax.ShapeDtypeStruct((N_DEV,) + a.shape, a.dtype) for a in self.arrays]

    def sems(self):
        return [pltpu.SemaphoreType.DMA((self.n * (N_DEV - 1),)), pltpu.SemaphoreType.DMA((self.n * (N_DEV - 1),)),
                pltpu.SemaphoreType.DMA((self.n,))]

    def _plan(self, ins, outs, sems):
        send_sems, recv_sems, local_sems = sems
        x, y, c, me = _my_place()
        sibling = (x, y, 1 - c)
        chips = [(1 - x, y), (x, 1 - y), (1 - x, 1 - y)]
        num = lambda px, py, pc: 4 * px + 2 * py + pc

        def copy(k, i, block, to, src=None):
            slot = outs[k].at[num(*block)]
            return pltpu.make_async_remote_copy(src_ref=slot if src is None else src, dst_ref=slot,
                                                send_sem=send_sems.at[k * (N_DEV - 1) + i],
                                                recv_sem=recv_sems.at[k * (N_DEV - 1) + i],
                                                device_id=to, device_id_type=MESH)

        local = [pltpu.make_async_copy(ins[k], outs[k].at[me], local_sems.at[k]) for k in range(self.n)]
        return (x, y, c), sibling, chips, copy, local

    def start(self, ins, outs, sems):
        me, sibling, chips, copy, local = self._plan(ins, outs, sems)
        for cp in local:
            cp.start()
        for k in range(self.n):
            copy(k, 0, me, sibling, src=ins[k]).start()
            for j, chip in enumerate(chips):
                copy(k, 1 + j, me, (*chip, me[2]), src=ins[k]).start()

    def forward(self, ins, outs, sems):
        me, sibling, chips, copy, _ = self._plan(ins, outs, sems)
        for j, chip in enumerate(chips):
            for k in range(self.n):
                copy(k, 1 + j, (*chip, me[2]), me).wait_recv()
                copy(k, 4 + j, (*chip, me[2]), sibling).start()

    def finish(self, ins, outs, sems):
        me, sibling, chips, copy, local = self._plan(ins, outs, sems)
        for k in range(self.n):
            copy(k, 0, sibling, me).wait_recv()
            for j, chip in enumerate(chips):
                copy(k, 4 + j, (*chip, 1 - me[2]), me).wait_recv()
        for k in range(self.n):
            copy(k, 0, me, sibling, src=ins[k]).wait_send()
            for j, chip in enumerate(chips):
                copy(k, 1 + j, me, (*chip, me[2]), src=ins[k]).wait_send()
                copy(k, 4 + j, (*chip, me[2]), sibling).wait_send()
        for cp in local:
            cp.wait()


def _pcall(body, *, name, grid, in_specs, out_specs, out_shape, scratch=(), carry=None):
    params = pltpu.CompilerParams(dimension_semantics=("arbitrary",) * len(grid), vmem_limit_bytes=VMEM_LIMIT_MB << 20)
    if carry is None:
        return pl.pallas_call(body, name=name, grid=grid, in_specs=in_specs, out_specs=out_specs, out_shape=out_shape,
                              scratch_shapes=list(scratch), compiler_params=params)
    single = not isinstance(out_specs, (list, tuple))
    out_specs = [out_specs] if single else list(out_specs)
    out_shape = [out_shape] if single else list(out_shape)
    n_in, n_out, n_scr, na = len(in_specs), len(out_specs), len(scratch), carry.n

    def wrapped(*refs):
        ins, cin = refs[:n_in], refs[n_in:n_in + na]
        outs, cout = refs[n_in + na:n_in + na + n_out], refs[n_in + na + n_out:n_in + 2 * na + n_out]
        scr, sems = refs[n_in + 2 * na + n_out:n_in + 2 * na + n_out + n_scr], refs[n_in + 2 * na + n_out + n_scr:]
        total = math.prod(grid)
        step = functools.reduce(lambda a, d: a * grid[d] + pl.program_id(d), range(len(grid)), 0)

        @pl.when(step == 0)
        def _():
            carry.start(cin, cout, sems)

        body(*ins, *outs, *scr)

        @pl.when(step == min(total * 7 // 8, total - 1))
        def _():
            carry.forward(cin, cout, sems)

        @pl.when(step == total - 1)
        def _():
            carry.finish(cin, cout, sems)

    any_spec = pl.BlockSpec(memory_space=pl.ANY)
    call = pl.pallas_call(wrapped, name=name, grid=grid, in_specs=list(in_specs) + [any_spec] * na,
                          out_specs=out_specs + [any_spec] * na, out_shape=out_shape + carry.out_shape(),
                          scratch_shapes=list(scratch) + carry.sems(), compiler_params=params)

    def run(*args):
        res = call(*args, *carry.arrays)
        main = res[0] if single else list(res[:n_out])
        return main, list(res[n_out:])

    return run


def _pick(dim, pref):
    if dim <= pref:
        return dim
    c = pref
    while c >= 128:
        if dim % c == 0 and c % 128 == 0:
            return c
        c -= 128
    return dim


def _rows(t, width, target_bytes=2 << 20):
    r = max(8, min(t, target_bytes // (4 * width)))
    r = 1 << (r.bit_length() - 1)
    while t % r:
        r //= 2
    return r


MM_FULL_K = 2048


def _mm(a, b, *, name, ta=False, tb=False, res=None, out_dtype=F32, bm=1024, bn=1024, bk=1024, carry=None, pair=None):
    m, k = (a.shape[1], a.shape[0]) if ta else a.shape
    n = b.shape[0] if tb else b.shape[1]
    assert (b.shape[1] if tb else b.shape[0]) == k
    bm, bn, bk = _pick(m, bm), _pick(n, bn), (k if k <= MM_FULL_K else _pick(k, bk))
    nk = k // bk
    dims = (((0,) if ta else (1,), (1,) if tb else (0,)), ((), ()))
    n_ab = 2 if pair is None else 4

    def body(*refs):
        a_ref, b_ref = refs[:2]
        r_ref = refs[n_ab] if res is not None else None
        o_ref = refs[n_ab + 1] if res is not None else refs[n_ab]
        part = lax.dot_general(a_ref[...].astype(BF16), b_ref[...].astype(BF16), dims, preferred_element_type=F32)
        if pair is not None:
            part = part + lax.dot_general(refs[2][...].astype(BF16), refs[3][...].astype(BF16), dims,
                                          preferred_element_type=F32)

        def finish(out):
            if res is not None:
                out = out + r_ref[...]
            o_ref[...] = out.astype(o_ref.dtype)

        if nk == 1:
            finish(part)
            return
        acc_ref = refs[-1]
        kk = pl.program_id(2)

        @pl.when(kk == 0)
        def _():
            acc_ref[...] = part

        @pl.when((kk > 0) & (kk < nk - 1))
        def _():
            acc_ref[...] += part

        @pl.when(kk == nk - 1)
        def _():
            finish(acc_ref[...] + part)

    a_spec = pl.BlockSpec((bk, bm), lambda i, j, kk: (kk, i)) if ta else pl.BlockSpec((bm, bk), lambda i, j, kk: (i, kk))
    b_spec = pl.BlockSpec((bn, bk), lambda i, j, kk: (j, kk)) if tb else pl.BlockSpec((bk, bn), lambda i, j, kk: (kk, j))
    o_spec = pl.BlockSpec((bm, bn), lambda i, j, kk: (i, j))
    ins, specs = [a, b], [a_spec, b_spec]
    if pair is not None:
        assert pair[0].shape == a.shape and pair[1].shape == b.shape
        ins += list(pair)
        specs += [a_spec, b_spec]
    if res is not None:
        ins.append(res)
        specs.append(o_spec)
    return _pcall(body, name=name, grid=(m // bm, n // bn, nk), in_specs=specs, out_specs=o_spec,
                  out_shape=jax.ShapeDtypeStruct((m, n), out_dtype),
                  scratch=[pltpu.VMEM((bm, bn), F32)] if nk > 1 else [], carry=carry)(*ins)


def _mm_dw2(a, b1, b2, *, name, bm=1024, bn=FF_WIDE, bk=1024):
    k, m = a.shape
    n = b1.shape[1]
    assert b1.shape == b2.shape == (k, n)
    bm, bn, bk = _pick(m, bm), _pick(n, bn), _pick(k, bk)
    nk = k // bk

    def body(a_ref, b1_ref, b2_ref, o1_ref, o2_ref, acc1_ref, acc2_ref):
        kk = pl.program_id(2)
        av = a_ref[...].astype(BF16)
        for b_ref, o_ref, acc_ref in ((b1_ref, o1_ref, acc1_ref), (b2_ref, o2_ref, acc2_ref)):
            part = lax.dot_general(av, b_ref[...].astype(BF16), (TN, ((), ())), preferred_element_type=F32)

            @pl.when(kk == 0)
            def _():
                acc_ref[...] = part

            @pl.when((kk > 0) & (kk < nk - 1))
            def _():
                acc_ref[...] += part

            @pl.when(kk == nk - 1)
            def _():
                o_ref[...] = (acc_ref[...] + part).astype(o_ref.dtype)

    b_spec = pl.BlockSpec((bk, bn), lambda i, j, kk: (kk, j))
    o_spec = pl.BlockSpec((bm, bn), lambda i, j, kk: (i, j))
    sds = jax.ShapeDtypeStruct((m, n), BF16)
    return _pcall(body, name=name, grid=(m // bm, n // bn, nk),
                  in_specs=[pl.BlockSpec((bk, bm), lambda i, j, kk: (kk, i)), b_spec, b_spec],
                  out_specs=[o_spec, o_spec], out_shape=[sds, sds],
                  scratch=[pltpu.VMEM((bm, bn), F32), pltpu.VMEM((bm, bn), F32)])(a, b1, b2)


def _rms_fwd(x, w, *, name, width, heads=1, col0=0, out_dtype=BF16):
    t = x.shape[0]
    tm = _rows(t, width)
    cb = col0 // width

    def body(x_ref, w_ref, o_ref):
        xv = x_ref[...]
        r = lax.rsqrt(jnp.mean(xv * xv, axis=-1, keepdims=True) + EPS)
        o_ref[...] = (xv * r * w_ref[...]).astype(o_ref.dtype)

    return _pcall(body, name=name, grid=(t // tm, heads),
                  in_specs=[pl.BlockSpec((tm, width), lambda i, h: (i, cb + h)),
                            pl.BlockSpec((1, width), lambda i, h: (0, 0))],
                  out_specs=pl.BlockSpec((tm, width), lambda i, h: (i, h)),
                  out_shape=jax.ShapeDtypeStruct((t, heads * width), out_dtype))(x, w)


def _rms_bwd(x, w, dy, *, name, width, heads=1, col0=0, dcol0=0, res=None, out_dtype=F32, with_delta=False):
    t = x.shape[0]
    tm = _rows(t, width)
    cb, dcb = col0 // width, dcol0 // width

    def body(*refs):
        refs = list(refs)
        x_ref, w_ref, dy_ref = refs[:3]
        r_ref = refs[3] if res is not None else None
        outs = refs[4:] if res is not None else refs[3:]
        dx_ref, dw_ref = outs[:2]
        xv = x_ref[...]
        dyv = dy_ref[...].astype(F32)
        r = lax.rsqrt(jnp.mean(xv * xv, axis=-1, keepdims=True) + EPS)
        xh = xv * r
        dyw = dyv * w_ref[...]
        dx = r * (dyw - xh * jnp.mean(dyw * xh, axis=-1, keepdims=True))
        if with_delta:
            outs[2][...] = jnp.broadcast_to(jnp.sum(dx * xv, axis=-1, keepdims=True), dx.shape)
        if res is not None:
            dx = dx + r_ref[...]
        dx_ref[...] = dx.astype(dx_ref.dtype)

        @pl.when((pl.program_id(0) == 0) & (pl.program_id(1) == 0))
        def _():
            dw_ref[...] = jnp.zeros_like(dw_ref)

        dw_ref[...] += (dyv * xh).reshape(tm // 8, 8, width).sum(axis=0)

    blk = pl.BlockSpec((tm, width), lambda i, h: (i, h))
    ins = [x, w, dy]
    specs = [pl.BlockSpec((tm, width), lambda i, h: (i, cb + h)), pl.BlockSpec((1, width), lambda i, h: (0, 0)),
             pl.BlockSpec((tm, width), lambda i, h: (i, dcb + h))]
    if res is not None:
        ins.append(res)
        specs.append(blk)
    out_shape = [jax.ShapeDtypeStruct((t, heads * width), out_dtype), jax.ShapeDtypeStruct((8, width), F32)]
    out_specs = [blk, pl.BlockSpec((8, width), lambda i, h: (0, 0))]
    if with_delta:
        out_shape.append(jax.ShapeDtypeStruct((t, heads * width), F32))
        out_specs.append(blk)
    return _pcall(body, name=name, grid=(t // tm, heads), in_specs=specs, out_specs=out_specs, out_shape=out_shape)(*ins)


def _sig(x):
    return 1.0 / (1.0 + jnp.exp(-x))


@jax.custom_vjp
def _sigmoid(x):
    return _sig(x)


def _sigmoid_fwd(x):
    s = _sig(x)
    return s, s


def _sigmoid_bwd(s, g):
    return (g * s * (1.0 - s),)


_sigmoid.defvjp(_sigmoid_fwd, _sigmoid_bwd)


@jax.custom_vjp
def _softplus(x):
    return jnp.maximum(x, 0.0) + jnp.log(1.0 + jnp.exp(-jnp.abs(x)))


def _softplus_fwd(x):
    return _softplus(x), x


def _softplus_bwd(x, g):
    return (g * _sig(x),)


_softplus.defvjp(_softplus_fwd, _softplus_bwd)


def _silu(x):
    return x * _sig(x)


def _dsilu(x):
    s = _sig(x)
    return s * (1.0 + x * (1.0 - s))


NN3 = (((2,), (1,)), ((0,), (0,)))
NT3 = (((2,), (2,)), ((0,), (0,)))
TN3 = (((1,), (1,)), ((0,), (0,)))


def _bdot(a, b, dims):
    return lax.dot_general(a.astype(BF16), b.astype(BF16), dims, preferred_element_type=F32)


def _bf16_part(x):
    bits = lax.bitcast_convert_type(x, jnp.uint32) & jnp.uint32(0xFFFF0000)
    return lax.bitcast_convert_type(bits, F32)


def _scan_rows(x, reverse):
    c = x.shape[1]
    row = lax.broadcasted_iota(jnp.int32, x.shape, 1)
    step = 1
    while step < c:
        if reverse:
            x = x + jnp.where(row < c - step, pltpu.roll(x, c - step, axis=1), 0.0)
        else:
            x = x + jnp.where(row >= step, pltpu.roll(x, step, axis=1), 0.0)
        step *= 2
    return x


@jax.custom_vjp
def _prefix_rows(x):
    return _scan_rows(x, False)


_prefix_rows.defvjp(lambda x: (_scan_rows(x, False), None), lambda _, g: (_scan_rows(g, True),))


def _dot3(a, b, dims):
    (ca,), (cb,) = dims[0]
    a_hi, b_hi = _bf16_part(a), _bf16_part(b)
    a_lo, b_lo = (a - a_hi).astype(BF16), (b - b_hi).astype(BF16)
    a_hi, b_hi = a_hi.astype(BF16), b_hi.astype(BF16)
    return lax.dot_general(jnp.concatenate([a_hi, a_hi, a_lo], axis=ca), jnp.concatenate([b_hi, b_lo, b_hi], axis=cb),
                           dims, preferred_element_type=F32)


@jax.custom_vjp
def _nn_hi(a, b):
    return _dot3(a, b, NN3)


_nn_hi.defvjp(lambda a, b: (_dot3(a, b, NN3), (a, b)), lambda r, g: (_dot3(g, r[1], NT3), _dot3(r[0], g, TN3)))


@jax.custom_vjp
def _nn(a, b):
    return _bdot(a, b, NN3)


_nn.defvjp(lambda a, b: (_bdot(a, b, NN3), (a, b)), lambda r, g: (_bdot(g, r[1], NT3), _bdot(r[0], g, TN3)))


@jax.custom_vjp
def _nt(a, b):
    return _bdot(a, b, NT3)


_nt.defvjp(lambda a, b: (_bdot(a, b, NT3), (a, b)), lambda r, g: (_bdot(g, r[1], NN3), _bdot(g, r[0], TN3)))


@jax.custom_vjp
def _tn(a, b):
    return _bdot(a, b, TN3)


_tn.defvjp(lambda a, b: (_bdot(a, b, TN3), (a, b)), lambda r, g: (_bdot(r[1], g, NT3), _bdot(r[0], g, NN3)))


CONV_ROWS, CONV_COLS = 256, 1024


def _rows_down(cur, prev8, s):
    r = pltpu.roll(cur, s, axis=0)
    rp = pltpu.roll(prev8, s, axis=0)
    row = lax.broadcasted_iota(jnp.int32, rp.shape, 0)
    head = jnp.where(row < s, rp, r[:8])
    return head if cur.shape[0] == 8 else jnp.concatenate([head, r[8:]], axis=0)


def _rows_up(cur, next8, s):
    n = cur.shape[0]
    r = pltpu.roll(cur, n - s, axis=0)
    rn = pltpu.roll(next8, 8 - s, axis=0)
    row = lax.broadcasted_iota(jnp.int32, rn.shape, 0)
    tail = jnp.where(row >= 8 - s, rn, r[n - 8:])
    return tail if n == 8 else jnp.concatenate([r[:n - 8], tail], axis=0)


def _conv_taps(cur, prev8):
    return [_rows_down(cur, prev8, GDN_CONV - 1 - j) for j in range(GDN_CONV - 1)] + [cur]


def _conv_pre(taps, w):
    acc = w[0:1] * taps[0]
    for j in range(1, GDN_CONV):
        acc = acc + w[j:j + 1] * taps[j]
    return acc


def _conv_fwd(proj, conv_w, *, name):
    t = proj.shape[0]
    tm, tc = _pick(t, CONV_ROWS), CONV_COLS
    nb = tm // 8

    def body(u_ref, p_ref, w_ref, o_ref):
        i = pl.program_id(1)
        prev = jnp.where(i > 0, p_ref[...], 0.0)
        o_ref[...] = _silu(_conv_pre(_conv_taps(u_ref[...], prev), w_ref[...]))

    return _pcall(body, name=name, grid=(CONV_CH // tc, t // tm),
                  in_specs=[pl.BlockSpec((tm, tc), lambda j, i: (i, j)),
                            pl.BlockSpec((8, tc), lambda j, i: (jnp.maximum(i * nb - 1, 0), j)),
                            pl.BlockSpec((8, tc), lambda j, i: (0, j))],
                  out_specs=pl.BlockSpec((tm, tc), lambda j, i: (i, j)),
                  out_shape=jax.ShapeDtypeStruct((t, CONV_CH), F32))(proj, proj, conv_w)


def _conv_bwd(proj, conv_w, dy, *, name):
    t = proj.shape[0]
    tm, tc = _pick(t, CONV_ROWS), CONV_COLS
    nb = tm // 8
    last = t // tm - 1

    def body(u_ref, p_ref, n_ref, dy_ref, dyn_ref, w_ref, du_ref, dw_ref):
        i = pl.program_id(1)
        w = w_ref[...]
        cur = u_ref[...]
        taps = _conv_taps(cur, jnp.where(i > 0, p_ref[...], 0.0))
        dc = dy_ref[...] * _dsilu(_conv_pre(taps, w))
        taps_next = _conv_taps(n_ref[...], cur[tm - 8:])
        dc_next = jnp.where(i < last, dyn_ref[...], 0.0) * _dsilu(_conv_pre(taps_next, w))
        du = w[3:4] * dc
        for j in range(GDN_CONV - 1):
            du = du + w[j:j + 1] * _rows_up(dc, dc_next, GDN_CONV - 1 - j)
        du_ref[...] = du.astype(du_ref.dtype)

        @pl.when(i == 0)
        def _():
            dw_ref[...] = jnp.zeros_like(dw_ref)

        for j in range(GDN_CONV):
            dw_ref[j] += (dc * taps[j]).reshape(nb, 8, tc).sum(axis=0)

    cur = lambda j, i: (i, j)
    return _pcall(body, name=name, grid=(CONV_CH // tc, t // tm),
                  in_specs=[pl.BlockSpec((tm, tc), cur),
                            pl.BlockSpec((8, tc), lambda j, i: (jnp.maximum(i * nb - 1, 0), j)),
                            pl.BlockSpec((8, tc), lambda j, i: (jnp.minimum((i + 1) * nb, t // 8 - 1), j)),
                            pl.BlockSpec((tm, tc), cur),
                            pl.BlockSpec((8, tc), lambda j, i: (jnp.minimum((i + 1) * nb, t // 8 - 1), j)),
                            pl.BlockSpec((8, tc), lambda j, i: (0, j))],
                  out_specs=[pl.BlockSpec((tm, tc), cur), pl.BlockSpec((GDN_CONV, 8, tc), lambda j, i: (0, 0, j))],
                  out_shape=[jax.ShapeDtypeStruct((t, CONV_CH), BF16), jax.ShapeDtypeStruct((GDN_CONV, 8, CONV_CH), F32)],
                  )(proj, proj, proj, dy, dy, conv_w)


def _gdn_chunk(q_raw, k_raw, v, misc, params, state):
    nh, c = q_raw.shape[0], q_raw.shape[1]
    lane = lax.broadcasted_iota(jnp.int32, misc.shape, 1)
    prow = lax.broadcasted_iota(jnp.int32, params.shape, 0)
    plane = lax.broadcasted_iota(jnp.int32, params.shape, 1)
    heads = lambda pieces: jnp.concatenate([p[None] for p in pieces], axis=0)
    col = lambda at: heads([jnp.sum(jnp.where(lane == at + h, misc, 0.0), axis=1, keepdims=True) for h in range(nh)])
    par = lambda row: heads([jnp.sum(jnp.where((prow == row) & (plane == h), params, 0.0), keepdims=True)
                             for h in range(nh)])
    b_raw, a_raw = col(LANE_B), col(LANE_A)
    a_log, dt_bias = par(0), par(1)
    beta = _sigmoid(b_raw)
    g = -jnp.exp(a_log) * _softplus(a_raw + dt_bias)

    q = q_raw * lax.rsqrt(jnp.sum(q_raw * q_raw, axis=-1, keepdims=True) + EPS) * (HEAD ** -0.5)
    k = k_raw * lax.rsqrt(jnp.sum(k_raw * k_raw, axis=-1, keepdims=True) + EPS)

    ri = lax.broadcasted_iota(jnp.int32, (c, c), 0)
    ci = lax.broadcasted_iota(jnp.int32, (c, c), 1)
    tril, strict = ri >= ci, ri > ci
    gc = _prefix_rows(g)
    gc_col = jnp.broadcast_to(gc, (nh, c, c))
    gc_row = jnp.swapaxes(gc_col, 1, 2)
    decay = jnp.exp(jnp.where(tril, gc_col - gc_row, -1e30))

    kb = k * beta
    vb = v * beta
    a_mat = jnp.where(strict, _nt(kb, k) * decay, 0.0)
    x = -a_mat
    inv = (ri == ci).astype(F32) + x
    for _ in range(5):
        x = _nn_hi(x, x)
        inv = inv + _nn_hi(inv, x)
    u = _nn_hi(inv, vb)
    w = _nn_hi(inv, kb * jnp.exp(gc))
    intra = _nt(q, k) * decay

    v_new = u - _nn(w, state)
    o = _nn(q * jnp.exp(gc), state) + _nn(intra, v_new)
    g_last = jnp.sum(g, axis=1, keepdims=True)
    k_dec = k * jnp.exp(g_last - gc)
    new_state = state * jnp.exp(g_last) + _tn(k_dec, v_new)
    return o, new_state


def _gdn_specs(nc, rev):
    cidx = (lambda n: nc - 1 - n) if rev else (lambda n: n)
    hb = lambda part: pl.BlockSpec((GDN_CHUNK, GDN_QK), lambda n: (cidx(n), part))
    misc = pl.BlockSpec((GDN_CHUNK, HEAD), lambda n: (cidx(n), COL_MISC // HEAD))
    params = pl.BlockSpec((8, HEAD), lambda n: (0, 0))
    hist = pl.BlockSpec((1, GDN_HEADS, HEAD, HEAD), lambda n: (cidx(n), 0, 0, 0))
    return hb, misc, params, hist


def _split_heads(v):
    return jnp.stack([v[:, h * HEAD:(h + 1) * HEAD] for h in range(v.shape[1] // HEAD)])


def _merge_heads(v):
    return jnp.concatenate([v[h] for h in range(v.shape[0])], axis=1)


def _gdn_fwd(qkv, proj, params, *, name):
    t = qkv.shape[0]
    nc = t // GDN_CHUNK
    hb, misc, pspec, hist = _gdn_specs(nc, False)

    def body(q_ref, k_ref, v_ref, m_ref, p_ref, o_ref, hist_ref, s_ref):
        @pl.when(pl.program_id(0) == 0)
        def _():
            s_ref[...] = jnp.zeros_like(s_ref)

        state = s_ref[...]
        hist_ref[0] = state
        o, new_state = _gdn_chunk(_split_heads(q_ref[...]), _split_heads(k_ref[...]), _split_heads(v_ref[...]),
                                  m_ref[...], p_ref[...], state)
        o_ref[...] = _merge_heads(o)
        s_ref[...] = new_state

    return _pcall(body, name=name, grid=(nc,),
                  in_specs=[hb(0), hb(1), hb(2), misc, pspec],
                  out_specs=[hb(0), hist],
                  out_shape=[jax.ShapeDtypeStruct((t, GDN_QK), F32),
                             jax.ShapeDtypeStruct((nc, GDN_HEADS, HEAD, HEAD), F32)],
                  scratch=[pltpu.VMEM((GDN_HEADS, HEAD, HEAD), F32)])(qkv, qkv, qkv, proj, params)


def _gdn_bwd(qkv, proj, params, hist_arr, do, dmisc_in, *, name, carry=None):
    t = qkv.shape[0]
    nc = t // GDN_CHUNK
    hb, misc, pspec, hist = _gdn_specs(nc, True)
    mrow = pl.BlockSpec((GDN_CHUNK, HEAD), lambda n: (nc - 1 - n, 0))

    def body(q_ref, k_ref, v_ref, m_ref, p_ref, hist_ref, do_ref, dmi_ref, dqkv_ref, dm_ref, dp_ref, ds_ref):
        @pl.when(pl.program_id(0) == 0)
        def _():
            ds_ref[...] = jnp.zeros_like(ds_ref)
            dp_ref[...] = jnp.zeros_like(dp_ref)

        _, vjp = jax.vjp(_gdn_chunk, _split_heads(q_ref[...]), _split_heads(k_ref[...]), _split_heads(v_ref[...]),
                         m_ref[...], p_ref[...], hist_ref[0])
        dq, dk, dv, dm, dp, ds = vjp((_split_heads(do_ref[...]), ds_ref[...]))
        dqkv_ref[:, 0:GDN_QK] = _merge_heads(dq)
        dqkv_ref[:, GDN_QK:2 * GDN_QK] = _merge_heads(dk)
        dqkv_ref[:, 2 * GDN_QK:] = _merge_heads(dv)
        ds_ref[...] = ds
        dm_ref[...] = dmi_ref[...] + dm
        dp_ref[...] += dp

    return _pcall(body, name=name, grid=(nc,),
                  in_specs=[hb(0), hb(1), hb(2), misc, pspec, hist, hb(0), mrow],
                  out_specs=[pl.BlockSpec((GDN_CHUNK, CONV_CH), lambda n: (nc - 1 - n, 0)), mrow, pspec],
                  out_shape=[jax.ShapeDtypeStruct((t, CONV_CH), F32), jax.ShapeDtypeStruct((t, HEAD), F32),
                             jax.ShapeDtypeStruct((8, HEAD), F32)],
                  scratch=[pltpu.VMEM((GDN_HEADS, HEAD, HEAD), F32)], carry=carry,
                  )(qkv, qkv, qkv, proj, params, hist_arr, do, dmisc_in)


def _gate_fwd(o_raw, proj, w, *, name):
    t = o_raw.shape[0]
    tm = _rows(t, HEAD)
    zb = COL_Z // HEAD

    def body(o_ref, z_ref, w_ref, out_ref):
        ov = o_ref[...]
        r = lax.rsqrt(jnp.mean(ov * ov, axis=-1, keepdims=True) + EPS)
        out_ref[...] = (ov * r * w_ref[...] * _silu(z_ref[...])).astype(out_ref.dtype)

    blk = pl.BlockSpec((tm, HEAD), lambda i, h: (i, h))
    return _pcall(body, name=name, grid=(t // tm, GDN_HEADS),
                  in_specs=[blk, pl.BlockSpec((tm, HEAD), lambda i, h: (i, zb + h)), pl.BlockSpec((1, HEAD), lambda i, h: (0, 0))],
                  out_specs=blk, out_shape=jax.ShapeDtypeStruct((t, GDN_QK), BF16))(o_raw, proj, w)


def _gate_bwd(o_raw, proj, w, dmixed, *, name):
    t = o_raw.shape[0]
    tm = _rows(t, HEAD)
    zb = COL_Z // HEAD

    def body(o_ref, z_ref, w_ref, dy_ref, do_ref, dz_ref, dw_ref):
        ov, zv, dyv = o_ref[...], z_ref[...], dy_ref[...]
        r = lax.rsqrt(jnp.mean(ov * ov, axis=-1, keepdims=True) + EPS)
        xh = ov * r
        dn = dyv * _silu(zv)
        dz_ref[...] = (dyv * xh * w_ref[...] * _dsilu(zv)).astype(dz_ref.dtype)
        dnw = dn * w_ref[...]
        do_ref[...] = r * (dnw - xh * jnp.mean(dnw * xh, axis=-1, keepdims=True))

        @pl.when((pl.program_id(0) == 0) & (pl.program_id(1) == 0))
        def _():
            dw_ref[...] = jnp.zeros_like(dw_ref)

        dw_ref[...] += (dn * xh).reshape(tm // 8, 8, HEAD).sum(axis=0)

    blk = pl.BlockSpec((tm, HEAD), lambda i, h: (i, h))
    return _pcall(body, name=name, grid=(t // tm, GDN_HEADS),
                  in_specs=[blk, pl.BlockSpec((tm, HEAD), lambda i, h: (i, zb + h)), pl.BlockSpec((1, HEAD), lambda i, h: (0, 0)), blk],
                  out_specs=[blk, blk, pl.BlockSpec((8, HEAD), lambda i, h: (0, 0))],
                  out_shape=[jax.ShapeDtypeStruct((t, GDN_QK), F32), jax.ShapeDtypeStruct((t, GDN_QK), BF16),
                             jax.ShapeDtypeStruct((8, HEAD), F32)])(o_raw, proj, w, dmixed)


def _rope_tables():
    half = QK_ROPE // 2
    inv = ROPE_THETA ** (-jnp.arange(half, dtype=F32) / half)
    zeros = jnp.zeros((HEAD - QK_ROPE,), F32)
    inv_row = jnp.concatenate([inv, inv, zeros])
    sign_row = jnp.concatenate([-jnp.ones((half,), F32), jnp.ones((half,), F32), zeros])
    mask_row = jnp.concatenate([jnp.ones((QK_ROPE,), F32), zeros])
    return jnp.concatenate([inv_row[None], sign_row[None], mask_row[None], jnp.zeros((5, HEAD), F32)], axis=0)


def _rope_cs(pos, tab, *, name):
    t = pos.shape[0]
    tm = _pick(t, 1024)

    def body(pos_ref, tab_ref, o_ref):
        tab = tab_ref[...]
        ang = pos_ref[...] * tab[0:1]
        o_ref[...] = jnp.concatenate([jnp.cos(ang) * tab[2:3], jnp.sin(ang) * tab[1:2]], axis=1)

    return _pcall(body, name=name, grid=(t // tm,),
                  in_specs=[pl.BlockSpec((tm, 1), lambda i: (i, 0)), pl.BlockSpec((8, HEAD), lambda i: (0, 0))],
                  out_specs=pl.BlockSpec((tm, 2 * HEAD), lambda i: (i, 0)),
                  out_shape=jax.ShapeDtypeStruct((t, 2 * HEAD), F32))(pos, tab)


def _rotate(x, cs, sign):
    lane = lax.broadcasted_iota(jnp.int32, x.shape, 1)
    half = QK_ROPE // 2
    partner = jnp.where(lane < half, pltpu.roll(x, HEAD - half, axis=1), pltpu.roll(x, half, axis=1))
    return x * cs[:, :HEAD] + partner * (cs[:, HEAD:] * sign)


def _q_rot(q, cs, *, name, sign, out_dtype=BF16):
    t = q.shape[0]
    tm = _pick(t, 1024)
    scale = (HEAD + QK_ROPE) ** -0.5

    def body(q_ref, cs_ref, o_ref):
        qv = q_ref[...].astype(F32)
        rot = _rotate(qv[:, HEAD:], cs_ref[...], sign)
        o_ref[...] = (jnp.concatenate([qv[:, :HEAD], rot], axis=1) * scale).astype(o_ref.dtype)

    blk = pl.BlockSpec((tm, QHEAD), lambda i, h: (i, h))
    return _pcall(body, name=name, grid=(t // tm, MLA_HEADS),
                  in_specs=[blk, pl.BlockSpec((tm, 2 * HEAD), lambda i, h: (i, 0))],
                  out_specs=blk, out_shape=jax.ShapeDtypeStruct((t, MLA_HEADS * QHEAD), out_dtype))(q, cs)


def _q_up(cqn, wuq_p, cs, *, name):
    t, lora = cqn.shape
    tm = _pick(t, 1024)
    scale = (HEAD + QK_ROPE) ** -0.5

    def body(a_ref, w_ref, cs_ref, o_ref):
        qv = lax.dot_general(a_ref[...], w_ref[...], (NN, ((), ())), preferred_element_type=F32)
        rot = _rotate(qv[:, HEAD:], cs_ref[...], 1.0)
        o_ref[...] = (jnp.concatenate([qv[:, :HEAD], rot], axis=1) * scale).astype(o_ref.dtype)

    return _pcall(body, name=name, grid=(t // tm, MLA_HEADS),
                  in_specs=[pl.BlockSpec((tm, lora), lambda i, h: (i, 0)), pl.BlockSpec((lora, QHEAD), lambda i, h: (0, h)),
                            pl.BlockSpec((tm, 2 * HEAD), lambda i, h: (i, 0))],
                  out_specs=pl.BlockSpec((tm, QHEAD), lambda i, h: (i, h)),
                  out_shape=jax.ShapeDtypeStruct((t, MLA_HEADS * QHEAD), BF16))(cqn, wuq_p, cs)


def _kv_up(ckvn, wukv, proj, cs, *, name):
    t, lora = ckvn.shape
    tm = _pick(t, 1024)

    def body(a_ref, w_ref, m_ref, cs_ref, k_ref, v_ref):
        kvv = lax.dot_general(a_ref[...], w_ref[...], (NN, ((), ())), preferred_element_type=F32)
        misc = m_ref[...]
        lane = lax.broadcasted_iota(jnp.int32, misc.shape, 1)
        rot = _rotate(jnp.where(lane < QK_ROPE, misc, 0.0), cs_ref[...], 1.0)
        k_ref[...] = jnp.concatenate([kvv[:, :HEAD], rot], axis=1).astype(k_ref.dtype)
        v_ref[...] = kvv[:, HEAD:].astype(v_ref.dtype)

    return _pcall(body, name=name, grid=(t // tm, MLA_HEADS),
                  in_specs=[pl.BlockSpec((tm, lora), lambda i, h: (i, 0)), pl.BlockSpec((lora, QHEAD), lambda i, h: (0, h)),
                            pl.BlockSpec((tm, HEAD), lambda i, h: (i, COL_MISC // HEAD)),
                            pl.BlockSpec((tm, 2 * HEAD), lambda i, h: (i, 0))],
                  out_specs=[pl.BlockSpec((tm, QHEAD), lambda i, h: (i, h)), pl.BlockSpec((tm, HEAD), lambda i, h: (i, h))],
                  out_shape=[jax.ShapeDtypeStruct((t, MLA_HEADS * QHEAD), BF16), jax.ShapeDtypeStruct((t, MLA_HEADS * HEAD), BF16)],
                  )(ckvn, wukv, proj, cs)


def _krope_bwd(dkr, cs, *, name):
    t = dkr.shape[0]
    tm = _pick(t, 512)

    def body(d_ref, cs_ref, o_ref):
        d = d_ref[...]
        acc = d[:, :HEAD]
        for h in range(1, MLA_HEADS):
            acc = acc + d[:, h * HEAD:(h + 1) * HEAD]
        o_ref[...] = _rotate(acc, cs_ref[...], -1.0)

    return _pcall(body, name=name, grid=(t // tm,),
                  in_specs=[pl.BlockSpec((tm, MLA_HEADS * HEAD), lambda i: (i, 0)), pl.BlockSpec((tm, 2 * HEAD), lambda i: (i, 0))],
                  out_specs=pl.BlockSpec((tm, HEAD), lambda i: (i, 0)),
                  out_shape=jax.ShapeDtypeStruct((t, HEAD), F32))(dkr, cs)


NEG = -1e30


def _tri(step, counts):
    starts = [sum(counts[:o]) for o in range(len(counts))]
    outer = sum([(step >= s).astype(jnp.int32) for s in starts[1:]], jnp.int32(0))
    start = sum([(step >= starts[o]).astype(jnp.int32) * (starts[o] - starts[o - 1]) for o in range(1, len(counts))], jnp.int32(0))
    return outer, step - start


def _attn_fwd(q, k, v, *, name, tq=1024, tk=1024, carry=None):
    t = q.shape[0]
    tq, tk = _pick(t, tq), _pick(t, tk)
    nq = t // tq
    last_kv = lambda i: (i * tq + tq - 1) // tk
    counts = [last_kv(i) + 1 for i in range(nq)]

    def body(q_ref, k_ref, v_ref, o_ref, lse_ref, m_ref, l_ref, acc_ref):
        i, j = _tri(pl.program_id(1), counts)

        @pl.when(j == 0)
        def _():
            m_ref[...] = jnp.full_like(m_ref, NEG)
            l_ref[...] = jnp.zeros_like(l_ref)
            acc_ref[...] = jnp.zeros_like(acc_ref)

        def step(masked):
            s = lax.dot_general(q_ref[...], k_ref[...], (NT, ((), ())), preferred_element_type=F32)
            if masked:
                qpos = i * tq + lax.broadcasted_iota(jnp.int32, s.shape, 0)
                kpos = j * tk + lax.broadcasted_iota(jnp.int32, s.shape, 1)
                s = jnp.where(kpos <= qpos, s, NEG)
            m_prev = m_ref[...]
            m_new = jnp.maximum(m_prev, jnp.max(s, axis=1, keepdims=True))
            alpha = jnp.exp(m_prev - m_new)
            p = jnp.exp(s - m_new)
            l_ref[...] = alpha * l_ref[...] + jnp.sum(p, axis=1, keepdims=True)
            acc_ref[...] = alpha * acc_ref[...] + lax.dot_general(p.astype(BF16), v_ref[...], (NN, ((), ())),
                                                                  preferred_element_type=F32)
            m_ref[...] = m_new

        crosses = j * tk + tk - 1 > i * tq

        @pl.when(crosses)
        def _():
            step(True)

        @pl.when(jnp.logical_not(crosses))
        def _():
            step(False)

        @pl.when(j == last_kv(i))
        def _():
            o_ref[...] = acc_ref[...] / l_ref[...]
            lse_ref[...] = jnp.broadcast_to(m_ref[...] + jnp.log(l_ref[...]), lse_ref.shape)

    qblk = pl.BlockSpec((tq, QHEAD), lambda h, s: (_tri(s, counts)[0], h))
    oblk = pl.BlockSpec((tq, HEAD), lambda h, s: (_tri(s, counts)[0], h))
    return _pcall(body, name=name, grid=(MLA_HEADS, sum(counts)),
                  in_specs=[qblk, pl.BlockSpec((tk, QHEAD), lambda h, s: (_tri(s, counts)[1], h)),
                            pl.BlockSpec((tk, HEAD), lambda h, s: (_tri(s, counts)[1], h))],
                  out_specs=[oblk, oblk],
                  out_shape=[jax.ShapeDtypeStruct((t, MLA_HEADS * HEAD), F32), jax.ShapeDtypeStruct((t, MLA_HEADS * HEAD), F32)],
                  scratch=[pltpu.VMEM((tq, 1), F32), pltpu.VMEM((tq, 1), F32), pltpu.VMEM((tq, HEAD), F32)],
                  carry=carry)(q, k, v)


def _attn_bwd(q, k, v, do, lse, delta, *, name, tq=1024, tk=1024, carry=None):
    t = q.shape[0]
    tq, tk = _pick(t, tq), _pick(t, tk)
    nq, nk = t // tq, t // tk
    first_q = lambda j: (j * tk) // tq
    counts = [nq - first_q(j) for j in range(nk)]

    def where(step):
        j, off = _tri(step, counts)
        return j, first_q(j) + off

    lanes = lambda col: jnp.tile(col, (1, tk // HEAD))

    def body(q_ref, k_ref, v_ref, do_ref, lse_ref, dl_ref, dq_ref, dkv_ref, dkr_ref, dk_acc, dv_acc):
        j, i = where(pl.program_id(1))

        @pl.when(i == first_q(j))
        def _():
            dk_acc[...] = jnp.zeros_like(dk_acc)
            dv_acc[...] = jnp.zeros_like(dv_acc)

        def step(masked):
            qv, kv_, dov = q_ref[...], k_ref[...], do_ref[...].astype(BF16)
            s = lax.dot_general(qv, kv_, (NT, ((), ())), preferred_element_type=F32)
            p = jnp.exp((s - lanes(lse_ref[...])).astype(BF16))
            if masked:
                qpos = i * tq + lax.broadcasted_iota(jnp.int32, s.shape, 0)
                kpos = j * tk + lax.broadcasted_iota(jnp.int32, s.shape, 1)
                p = jnp.where(kpos <= qpos, p, jnp.zeros_like(p))
            dv_acc[...] += lax.dot_general(p, dov, (TN, ((), ())), preferred_element_type=F32)
            dp = lax.dot_general(dov, v_ref[...], (NT, ((), ())), preferred_element_type=F32)
            ds = p * (dp - lanes(dl_ref[...])).astype(BF16)
            dk_acc[...] += lax.dot_general(ds, qv, (TN, ((), ())), preferred_element_type=F32)
            contrib = lax.dot_general(ds, kv_, (NN, ((), ())), preferred_element_type=F32)
            rows = pl.ds(pl.multiple_of(i * tq, tq), tq)

            @pl.when(j == 0)
            def _():
                dq_ref[rows, :] = contrib

            @pl.when(j > 0)
            def _():
                dq_ref[rows, :] += contrib

        crosses = j * tk + tk - 1 > i * tq

        @pl.when(crosses)
        def _():
            step(True)

        @pl.when(jnp.logical_not(crosses))
        def _():
            step(False)

        @pl.when(i == nq - 1)
        def _():
            dk = dk_acc[...]
            dkv_ref[...] = jnp.concatenate([dk[:, :HEAD], dv_acc[...]], axis=1).astype(dkv_ref.dtype)
            dkr_ref[...] = dk[:, HEAD:]

    qi = lambda h, s: (where(s)[1], h)
    kj = lambda h, s: (where(s)[0], h)
    return _pcall(body, name=name, grid=(MLA_HEADS, sum(counts)),
                  in_specs=[pl.BlockSpec((tq, QHEAD), qi), pl.BlockSpec((tk, QHEAD), kj), pl.BlockSpec((tk, HEAD), kj),
                            pl.BlockSpec((tq, HEAD), qi), pl.BlockSpec((tq, HEAD), qi), pl.BlockSpec((tq, HEAD), qi)],
                  out_specs=[pl.BlockSpec((t, QHEAD), lambda h, s: (0, h)), pl.BlockSpec((tk, QHEAD), kj),
                             pl.BlockSpec((tk, HEAD), kj)],
                  out_shape=[jax.ShapeDtypeStruct((t, MLA_HEADS * QHEAD), F32), jax.ShapeDtypeStruct((t, MLA_HEADS * QHEAD), BF16),
                             jax.ShapeDtypeStruct((t, MLA_HEADS * HEAD), F32)],
                  scratch=[pltpu.VMEM((tk, QHEAD), F32), pltpu.VMEM((tk, HEAD), F32)], carry=carry)(q, k, v, do, lse, delta)


def _ffn_up(h, wgate, wup, *, name, bm=512, bn=FF_WIDE):
    t = h.shape[0]
    bm = _pick(t, bm)

    def body(h_ref, wg_ref, wu_ref, g_ref, u_ref, a_ref):
        hv = h_ref[...]
        g = lax.dot_general(hv, wg_ref[...], (NN, ((), ())), preferred_element_type=F32)
        u = lax.dot_general(hv, wu_ref[...], (NN, ((), ())), preferred_element_type=F32)
        g_ref[...] = g.astype(g_ref.dtype)
        u_ref[...] = u.astype(u_ref.dtype)
        a_ref[...] = (_silu(g) * u).astype(a_ref.dtype)

    w_spec = pl.BlockSpec((D_MODEL, bn), lambda j, i: (0, j))
    o_spec = pl.BlockSpec((bm, bn), lambda j, i: (i, j))
    sds = jax.ShapeDtypeStruct((t, D_FF), BF16)
    return _pcall(body, name=name, grid=(D_FF // bn, t // bm),
                  in_specs=[pl.BlockSpec((bm, D_MODEL), lambda j, i: (i, 0)), w_spec, w_spec],
                  out_specs=[o_spec] * 3, out_shape=[sds] * 3)(h, wgate, wup)


def _ffn_down_dx(dy, wdown, gate, up, *, name, bm=512, bn=FF_WIDE):
    t = dy.shape[0]
    bm = _pick(t, bm)

    def body(dy_ref, w_ref, g_ref, u_ref, dg_ref, du_ref):
        d = lax.dot_general(dy_ref[...].astype(BF16), w_ref[...], (NT, ((), ())), preferred_element_type=F32)
        g = g_ref[...].astype(F32)
        dg_ref[...] = (d * u_ref[...].astype(F32) * _dsilu(g)).astype(dg_ref.dtype)
        du_ref[...] = (d * _silu(g)).astype(du_ref.dtype)

    o_spec = pl.BlockSpec((bm, bn), lambda j, i: (i, j))
    sds = jax.ShapeDtypeStruct((t, D_FF), BF16)
    return _pcall(body, name=name, grid=(D_FF // bn, t // bm),
                  in_specs=[pl.BlockSpec((bm, D_MODEL), lambda j, i: (i, 0)), pl.BlockSpec((bn, D_MODEL), lambda j, i: (j, 0)),
                            o_spec, o_spec],
                  out_specs=[o_spec, o_spec], out_shape=[sds, sds])(dy, wdown, gate, up)


def _loss_bwd(x2, w, target, *, name):
    t = x2.shape[0]
    tm = _rows(t, D_MODEL)

    def body(x_ref, w_ref, t_ref, dx_ref, dw_ref, l_ref):
        xv, wv = x_ref[...], w_ref[...]
        r = lax.rsqrt(jnp.mean(xv * xv, axis=-1, keepdims=True) + EPS)
        xh = xv * r
        err = xh * wv - t_ref[...]
        dy = err * (1.0 / D_MODEL)
        dyw = dy * wv
        dx_ref[...] = r * (dyw - xh * jnp.mean(dyw * xh, axis=-1, keepdims=True))

        @pl.when(pl.program_id(0) == 0)
        def _():
            dw_ref[...] = jnp.zeros_like(dw_ref)
            l_ref[...] = jnp.zeros_like(l_ref)

        dw_ref[...] += (dy * xh).reshape(tm // 8, 8, D_MODEL).sum(axis=0)
        sq = (err * err).reshape(tm // 8, 8, D_MODEL).sum(axis=0)
        part = sq[:, :HEAD]
        for c in range(1, D_MODEL // HEAD):
            part = part + sq[:, c * HEAD:(c + 1) * HEAD]
        l_ref[...] += part * (0.5 / D_MODEL)

    row = pl.BlockSpec((tm, D_MODEL), lambda i: (i, 0))
    return _pcall(body, name=name, grid=(t // tm,),
                  in_specs=[row, pl.BlockSpec((1, D_MODEL), lambda i: (0, 0)), row],
                  out_specs=[row, pl.BlockSpec((8, D_MODEL), lambda i: (0, 0)), pl.BlockSpec((8, HEAD), lambda i: (0, 0))],
                  out_shape=[jax.ShapeDtypeStruct((t, D_MODEL), F32), jax.ShapeDtypeStruct((8, D_MODEL), F32),
                             jax.ShapeDtypeStruct((8, HEAD), F32)])(x2, w, target)


def _unshard_cols(g):
    return jnp.transpose(g, (1, 0, 2)).reshape(g.shape[1], N_DEV * g.shape[2])


def _shard_cols(w):
    return jnp.transpose(w.reshape(w.shape[0], N_DEV, w.shape[1] // N_DEV), (1, 0, 2))


_WIN_ORDER = ((0, 4096), (4112, 5136), (5136, 5200), (4096, 4112))
_WIN_SHARD = IN_WIDTH // N_DEV


def _win_pieces():
    out, pos = [], 0
    for a, b in _WIN_ORDER:
        c = a
        while c < b:
            dev, off = divmod(c, _WIN_SHARD)
            width = min(b, (dev + 1) * _WIN_SHARD) - c
            out.append((dev, off, width, pos))
            c, pos = c + width, pos + width
    return out


def _win_gathered_to_padded(g):
    pieces = [g[dev][:, off:off + width] for dev, off, width, _ in _win_pieces()]
    return jnp.concatenate(pieces + [jnp.zeros((g.shape[1], PROJ_W - IN_WIDTH), g.dtype)], axis=1)


def _win_padded_to_shards(d):
    shards = []
    for dev in range(N_DEV):
        mine = sorted((off, width, pos) for dv, off, width, pos in _win_pieces() if dv == dev)
        shards.append(jnp.concatenate([d[:, pos:pos + width] for _, width, pos in mine], axis=1))
    return jnp.stack(shards)


def _wuq_to_padded(w):
    w3 = w.reshape(w.shape[0], MLA_HEADS, HEAD + QK_ROPE)
    return jnp.pad(w3, ((0, 0), (0, 0), (0, QHEAD - HEAD - QK_ROPE))).reshape(w.shape[0], MLA_HEADS * QHEAD)


def _wuq_from_padded(d):
    return d.reshape(d.shape[0], MLA_HEADS, QHEAD)[:, :, :HEAD + QK_ROPE].reshape(d.shape[0], MLA_HEADS * (HEAD + QK_ROPE))


def _late_weights(g_out, g_gate, g_up, g_down):
    return g_out.reshape(D_MODEL, D_MODEL), _unshard_cols(g_gate), _unshard_cols(g_up), g_down.reshape(D_FF, D_MODEL)


def _local_step(x, pos, target, win_p, wuq_p, wukv, late, conv_w, small, exchange):
    cs = _rope_cs(pos, _rope_tables(), name="rope_cs")
    if not exchange:
        wout, wgate, wup, wdown = late
    h1 = _rms_fwd(x, small["attn_norm_w"], name="rms1_fwd", width=D_MODEL)
    proj = _mm(h1, win_p, name="mm_in", bn=768)
    qkv = _conv_fwd(proj, conv_w, name="conv_fwd")
    o_gdn_raw, hist = _gdn_fwd(qkv, proj, small["gdn_params"], name="gdn_fwd")
    o_gdn = _gate_fwd(o_gdn_raw, proj, small["gdn_norm_w"], name="gate_fwd")
    cqn = _rms_fwd(proj, small["q_norm_w"], name="rmsq_fwd", width=Q_LORA, col0=COL_CQ)
    ckvn = _rms_fwd(proj, small["kv_norm_w"], name="rmskv_fwd", width=KV_LORA, col0=COL_CKV)
    q_full = _q_up(cqn, wuq_p, cs, name="q_up")
    k_full, v_b = _kv_up(ckvn, wukv, proj, cs, name="kv_up")
    if exchange:
        (o_mla_raw, lse), gathered = _attn_fwd(q_full, k_full, v_b, name="attn_fwd", carry=_Gather(late))
        wout, wgate, wup, wdown = _late_weights(*gathered)
    else:
        o_mla_raw, lse = _attn_fwd(q_full, k_full, v_b, name="attn_fwd")
    o_mla = _rms_fwd(o_mla_raw, small["mla_out_norm_w"], name="rmso_fwd", width=HEAD, heads=MLA_HEADS)
    mixed = jnp.concatenate([o_gdn, o_mla], axis=1)
    x1 = _mm(mixed, wout, name="mm_out", res=x)
    h2 = _rms_fwd(x1, small["ffn_norm_w"], name="rms2_fwd", width=D_MODEL)
    gate, up, act = _ffn_up(h2, wgate, wup, name="ffn_up")
    x2 = _mm(act, wdown, name="mm_down", res=x1, bk=FF_WIDE)
    dx2, dw_final, loss_part = _loss_bwd(x2, small["final_norm_w"], target, name="loss_bwd")
    dgate, dup = _ffn_down_dx(dx2, wdown, gate, up, name="ffn_down_dx")
    d_wdown = _mm(act, dx2, name="mm_down_dw", ta=True, out_dtype=BF16, bm=FF_WIDE)
    dh2 = _mm(dgate, wgate, name="mm_gateup_dx", tb=True, bk=FF_WIDE, pair=(dup, wup))
    d_wgate, d_wup = _mm_dw2(h2, dgate, dup, name="mm_gateup_dw")
    dx1, dw_ffn = _rms_bwd(x1, small["ffn_norm_w"], dh2, name="rms2_bwd", width=D_MODEL, res=dx2)
    dmixed = _mm(dx1, wout, name="mm_out_dx", tb=True)
    d_wout = _mm(mixed, dx1, name="mm_out_dw", ta=True, out_dtype=BF16)
    do_mla, dw_mla_out, delta = _rms_bwd(o_mla_raw, small["mla_out_norm_w"], dmixed, name="rmso_bwd", width=HEAD,
                                         heads=MLA_HEADS, dcol0=GDN_QK, with_delta=True, out_dtype=BF16)
    if exchange:
        send = [d_wdown.reshape(N_DEV, D_FF // N_DEV, D_MODEL), _shard_cols(d_wgate), _shard_cols(d_wup)]
        (dq_full, dkv, dkr_h), (r_down, r_gate, r_up) = _attn_bwd(q_full, k_full, v_b, do_mla, lse, delta, name="attn_bwd",
                                                                  carry=_Exchange(send, [False] * 3))
    else:
        dq_full, dkv, dkr_h = _attn_bwd(q_full, k_full, v_b, do_mla, lse, delta, name="attn_bwd")
    dq_pre = _q_rot(dq_full, cs, name="q_rot_bwd", sign=-1.0)
    dmisc_kr = _krope_bwd(dkr_h, cs, name="krope_bwd")
    dcqn = _mm(dq_pre, wuq_p, name="mm_uq_dx", tb=True)
    d_wuq = _mm(cqn, dq_pre, name="mm_uq_dw", ta=True, out_dtype=BF16)
    dckvn = _mm(dkv, wukv, name="mm_ukv_dx", tb=True)
    d_wukv = _mm(ckvn, dkv, name="mm_ukv_dw", ta=True, out_dtype=BF16)
    dcq, dw_qn = _rms_bwd(proj, small["q_norm_w"], dcqn, name="rmsq_bwd", width=Q_LORA, col0=COL_CQ, out_dtype=BF16)
    dckv, dw_kvn = _rms_bwd(proj, small["kv_norm_w"], dckvn, name="rmskv_bwd", width=KV_LORA, col0=COL_CKV, out_dtype=BF16)
    do_gdn, dz, dw_gdn = _gate_bwd(o_gdn_raw, proj, small["gdn_norm_w"], dmixed, name="gate_bwd")
    if exchange:
        send = [d_wout.reshape(N_DEV, D_MODEL // N_DEV, D_MODEL), _shard_cols(_wuq_from_padded(d_wuq)), _shard_cols(d_wukv)]
        (dqkv, dmisc, d_params), (r_out, r_uq, r_ukv) = _gdn_bwd(
            qkv, proj, small["gdn_params"], hist, do_gdn, dmisc_kr, name="gdn_bwd", carry=_Exchange(send, [False] * 3))
    else:
        dqkv, dmisc, d_params = _gdn_bwd(qkv, proj, small["gdn_params"], hist, do_gdn, dmisc_kr, name="gdn_bwd")
    dqkv_pre, dconv = _conv_bwd(proj, conv_w, dqkv, name="conv_bwd")
    dproj = jnp.concatenate([dqkv_pre, dz, dcq, dckv, dmisc.astype(BF16), jnp.zeros((x.shape[0], PROJ_W - COL_MISC - HEAD), BF16)], axis=1)
    d_win = _mm(h1, dproj, name="mm_in_dw", ta=True, out_dtype=BF16, bn=768)
    if exchange:
        dh1, (r_in,) = _mm(dproj, win_p, name="mm_in_dx", tb=True, bk=768,
                           carry=_Exchange([_win_padded_to_shards(d_win)], [False]))
        d_win = r_in
    else:
        dh1 = _mm(dproj, win_p, name="mm_in_dx", tb=True, bk=768)
    dx, dw_attn = _rms_bwd(x, small["attn_norm_w"], dh1, name="rms1_bwd", width=D_MODEL, res=dx1)

    if exchange:
        big = {"w_in": d_win, "w_uq": r_uq, "w_ukv": r_ukv, "w_out": r_out, "w_gate": r_gate, "w_up": r_up, "w_down": r_down}
    else:
        big = {"w_in": d_win, "w_uq": d_wuq, "w_ukv": d_wukv, "w_out": d_wout, "w_gate": d_wgate, "w_up": d_wup,
               "w_down": d_wdown}
    sm = {"attn_norm_w": dw_attn, "ffn_norm_w": dw_ffn, "final_norm_w": dw_final, "q_norm_w": dw_qn, "kv_norm_w": dw_kvn,
          "gdn_norm_w": dw_gdn, "mla_out_norm_w": dw_mla_out, "gdn_params": d_params, "conv_w": dconv, "loss": loss_part}
    return dx, big, sm


def _exchange(ex, *, name):
    def body(*refs):
        ins, outs, sems = refs[:ex.n], refs[ex.n:2 * ex.n], refs[2 * ex.n:]
        ex.start(ins, outs, sems)
        ex.forward(ins, outs, sems)
        ex.finish(ins, outs, sems)

    any_spec = pl.BlockSpec(memory_space=pl.ANY)
    return pl.pallas_call(body, name=name, in_specs=[any_spec] * ex.n, out_specs=[any_spec] * ex.n,
                          out_shape=ex.out_shape(), scratch_shapes=ex.sems())(*ex.arrays)


def _adamw_math(g, w, m, v):
    m = ADAM_B1 * m + (1.0 - ADAM_B1) * g
    v = ADAM_B2 * v + (1.0 - ADAM_B2) * (g * g)
    m_hat = m / (1.0 - ADAM_B1 ** ADAM_STEP)
    v_hat = v / (1.0 - ADAM_B2 ** ADAM_STEP)
    delta = -ADAM_LR * (m_hat / (jnp.sqrt(v_hat) + ADAM_EPS) + ADAM_WD * w)
    return delta, m, v


def _adamw(parts, w, m, v, *, name):
    npart, r, c = parts.shape
    tr = r if r * c * 4 <= (1 << 20) else _rows(r, c, 1 << 20)

    def body(p_ref, w_ref, m_ref, v_ref, g_ref, d_ref, nm_ref, nv_ref):
        g = p_ref[0].astype(F32)
        for s in range(1, npart):
            g = g + p_ref[s].astype(F32)
        g_ref[...] = g
        d_ref[...], nm_ref[...], nv_ref[...] = _adamw_math(g, w_ref[...], m_ref[...], v_ref[...])

    blk = pl.BlockSpec((tr, c), lambda i: (i, 0))
    sds = jax.ShapeDtypeStruct((r, c), F32)
    return _pcall(body, name=name, grid=(r // tr,),
                  in_specs=[pl.BlockSpec((npart, tr, c), lambda i: (0, i, 0)), blk, blk, blk],
                  out_specs=[blk] * 4, out_shape=[sds] * 4)(parts, w, m, v)


def _sum_parts(parts, *, name):
    npart, r, c = parts.shape

    def body(p_ref, o_ref):
        g = p_ref[0]
        for s in range(1, npart):
            g = g + p_ref[s]
        o_ref[...] = g

    return _pcall(body, name=name, grid=(1,), in_specs=[pl.BlockSpec((npart, r, c), lambda i: (0, 0, 0))],
                  out_specs=pl.BlockSpec((r, c), lambda i: (0, 0)), out_shape=jax.ShapeDtypeStruct((r, c), F32))(parts)


_SMALL = (("attn_norm_w", D_MODEL), ("ffn_norm_w", D_MODEL), ("final_norm_w", D_MODEL), ("q_norm_w", Q_LORA),
          ("kv_norm_w", KV_LORA), ("gdn_norm_w", HEAD), ("mla_out_norm_w", HEAD), ("a_log", HEAD), ("dt_bias", HEAD))
_SMALL_ROWS = sum(n for _, n in _SMALL) // HEAD
_CONV_ROWS = GDN_CONV * CONV_CH // HEAD
_PACK_ROWS = 160


def _pad_lanes(v, n):
    v = v.reshape(-1)
    return jnp.concatenate([v, jnp.zeros((n - v.shape[0],), v.dtype)])


def kernel(x, positions, attn_norm_w, w_in, conv_w, a_log, dt_bias, gdn_norm_w, q_norm_w, w_uq, kv_norm_w, w_ukv, mla_out_norm_w, w_out, ffn_norm_w, w_gate, w_up, w_down, final_norm_w, loss_target, m_attn_norm_w, m_w_in, m_conv_w, m_a_log, m_dt_bias, m_gdn_norm_w, m_q_norm_w, m_w_uq, m_kv_norm_w, m_w_ukv, m_mla_out_norm_w, m_w_out, m_ffn_norm_w, m_w_gate, m_w_up, m_w_down, m_final_norm_w, v_attn_norm_w, v_w_in, v_conv_w, v_a_log, v_dt_bias, v_gdn_norm_w, v_q_norm_w, v_w_uq, v_kv_norm_w, v_w_ukv, v_mla_out_norm_w, v_w_out, v_ffn_norm_w, v_w_gate, v_w_up, v_w_down, v_final_norm_w):
    t = x.shape[1]
    me = 4 * lax.axis_index("x") + 2 * lax.axis_index("y") + lax.axis_index("c")
    weights = dict(attn_norm_w=attn_norm_w, w_in=w_in, conv_w=conv_w, a_log=a_log, dt_bias=dt_bias, gdn_norm_w=gdn_norm_w,
                   q_norm_w=q_norm_w, w_uq=w_uq, kv_norm_w=kv_norm_w, w_ukv=w_ukv, mla_out_norm_w=mla_out_norm_w, w_out=w_out,
                   ffn_norm_w=ffn_norm_w, w_gate=w_gate, w_up=w_up, w_down=w_down, final_norm_w=final_norm_w)
    mom_m = dict(attn_norm_w=m_attn_norm_w, w_in=m_w_in, conv_w=m_conv_w, a_log=m_a_log, dt_bias=m_dt_bias, gdn_norm_w=m_gdn_norm_w,
                 q_norm_w=m_q_norm_w, w_uq=m_w_uq, kv_norm_w=m_kv_norm_w, w_ukv=m_w_ukv, mla_out_norm_w=m_mla_out_norm_w,
                 w_out=m_w_out, ffn_norm_w=m_ffn_norm_w, w_gate=m_w_gate, w_up=m_w_up, w_down=m_w_down, final_norm_w=m_final_norm_w)
    mom_v = dict(attn_norm_w=v_attn_norm_w, w_in=v_w_in, conv_w=v_conv_w, a_log=v_a_log, dt_bias=v_dt_bias, gdn_norm_w=v_gdn_norm_w,
                 q_norm_w=v_q_norm_w, w_uq=v_w_uq, kv_norm_w=v_kv_norm_w, w_ukv=v_w_ukv, mla_out_norm_w=v_mla_out_norm_w,
                 w_out=v_w_out, ffn_norm_w=v_ffn_norm_w, w_gate=v_w_gate, w_up=v_w_up, w_down=v_w_down, final_norm_w=v_final_norm_w)
    big_names = ("w_in", "w_uq", "w_ukv", "w_out", "w_gate", "w_up", "w_down")

    shard = {n: weights[n][0].astype(BF16) for n in big_names}
    g_in, g_uq, g_ukv, g_conv = _exchange(_Gather([shard["w_in"], shard["w_uq"], shard["w_ukv"], weights["conv_w"][0]]),
                                          name="gather_weights")
    win_p = _win_gathered_to_padded(g_in)
    wuq_p = _wuq_to_padded(_unshard_cols(g_uq))
    wukv = _unshard_cols(g_ukv)
    late = [shard["w_out"], shard["w_gate"], shard["w_up"], shard["w_down"]]
    conv_full = jnp.concatenate([_unshard_cols(g_conv), jnp.zeros((8 - GDN_CONV, CONV_CH), F32)], axis=0)

    gdn_params = jnp.concatenate([_pad_lanes(a_log, HEAD)[None], _pad_lanes(dt_bias, HEAD)[None], jnp.zeros((6, HEAD), F32)], axis=0)
    small = {n: weights[n].reshape(1, -1) for n in ("attn_norm_w", "ffn_norm_w", "final_norm_w", "q_norm_w", "kv_norm_w",
                                                    "gdn_norm_w", "mla_out_norm_w")}
    small["gdn_params"] = gdn_params

    dx, big, sm = _local_step(x[0], positions.reshape(t, 1).astype(F32), loss_target[0], win_p, wuq_p, wukv, late,
                              conv_full, small, True)

    rows8 = lambda name: jnp.sum(sm[name], axis=0)
    pieces = [rows8(n) for n, _ in _SMALL[:7]]
    pieces += [_pad_lanes(jnp.sum(sm["gdn_params"][0:1], axis=0), HEAD), _pad_lanes(jnp.sum(sm["gdn_params"][1:2], axis=0), HEAD)]
    pieces.append(jnp.sum(sm["conv_w"], axis=1).reshape(-1))
    pieces.append(_pad_lanes(jnp.sum(sm["loss"]).reshape(1), HEAD))
    packed = _pad_lanes(jnp.concatenate(pieces), _PACK_ROWS * HEAD).reshape(_PACK_ROWS, HEAD)
    (r_small,) = _exchange(_Exchange([packed], [True]), name="exchange_small")

    outs_g, outs_d, outs_m, outs_v = {}, {}, {}, {}
    for name in big_names:
        g, d, nm, nv = _adamw(big[name], weights[name][0], mom_m[name][0], mom_v[name][0], name="adamw_" + name)
        outs_g[name], outs_d[name], outs_m[name], outs_v[name] = g[None], d[None], nm[None], nv[None]

    total = _sum_parts(r_small, name="sum_small")
    flat = total.reshape(-1)
    loss = flat[(_SMALL_ROWS + _CONV_ROWS) * HEAD]
    g_small, off = {}, 0
    for n, size in _SMALL:
        g_small[n] = flat[off:off + size]
        off += size
    g_conv_full = flat[off:off + GDN_CONV * CONV_CH].reshape(GDN_CONV, CONV_CH)
    g_small["conv_w"] = lax.dynamic_slice(g_conv_full, (0, me * (CONV_CH // N_DEV)), (GDN_CONV, CONV_CH // N_DEV)).reshape(-1)
    order = [n for n, _ in _SMALL] + ["conv_w"]
    sizes = dict(_SMALL)
    sizes["conv_w"] = GDN_CONV * CONV_CH // N_DEV
    true_size = {n: weights[n].size for n in order}

    def pack(d):
        return jnp.concatenate([_pad_lanes(d[n], sizes[n]) for n in order]).reshape(1, -1, HEAD)

    g2, d2, m2, v2 = _adamw(pack(g_small), pack(weights)[0], pack(mom_m)[0], pack(mom_v)[0], name="adamw_small")
    off = 0
    for n in order:
        for src, dst in ((g2, outs_g), (d2, outs_d), (m2, outs_m), (v2, outs_v)):
            dst[n] = src.reshape(-1)[off:off + true_size[n]].reshape(weights[n].shape)
        off += sizes[n]

    names = ("attn_norm_w", "w_in", "conv_w", "a_log", "dt_bias", "gdn_norm_w", "q_norm_w", "w_uq", "kv_norm_w", "w_ukv",
             "mla_out_norm_w", "w_out", "ffn_norm_w", "w_gate", "w_up", "w_down", "final_norm_w")
    return (loss, dx[None], *[outs_g[n] for n in names], *[outs_d[n] for n in names], *[outs_m[n] for n in names],
            *[outs_v[n] for n in names])
```

```python
import functools
import math

import jax
import jax.numpy as jnp
from jax import lax
from jax.experimental import pallas as pl
from jax.experimental.pallas import tpu as pltpu

F32 = jnp.float32
BF16 = jnp.bfloat16

D_MODEL = 2048
GDN_HEADS = 8
HEAD = 128
GDN_CONV = 4
GDN_CHUNK = 64
GDN_QK = GDN_HEADS * HEAD
CONV_CH = 3 * GDN_QK
MLA_HEADS = 8
QK_ROPE = 64
Q_LORA = 512
KV_LORA = 512
ROPE_THETA = 10000.0
D_FF = 5632
EPS = 1e-6
IN_WIDTH = 5200
ADAM_LR, ADAM_B1, ADAM_B2, ADAM_EPS, ADAM_WD, ADAM_STEP = 0.001, 0.9, 0.999, 1e-08, 0.01, 10

PROJ_W = 5376
PROJ_BLK = PROJ_W // 3
COL_Z = 3072
COL_CQ = 4096
COL_CKV = 4608
COL_MISC = 5120
LANE_B = 64
LANE_A = 72
QHEAD = 256
FF_WIDE = D_FF // 4
N_DEV = 8
MESH = pl.DeviceIdType.MESH
VMEM_LIMIT_MB = 48

NN = ((1,), (0,))
NT = ((1,), (1,))
TN = ((0,), (0,))


def _my_place():
    x, y, c = lax.axis_index("x"), lax.axis_index("y"), lax.axis_index("c")
    return x, y, c, 4 * x + 2 * y + c


def _peer(x, y, c, p):
    px, py, pc = x ^ ((p >> 2) & 1), y ^ ((p >> 1) & 1), c ^ (p & 1)
    return (px, py, pc), 4 * px + 2 * py + pc


class _Exchange:
    def __init__(self, arrays, gather):
        self.arrays, self.gather, self.n = list(arrays), list(gather), len(arrays)

    def out_shape(self):
        return [jax.ShapeDtypeStruct(((N_DEV,) + a.shape) if g else a.shape, a.dtype)
                for a, g in zip(self.arrays, self.gather)]

    def sems(self):
        return [pltpu.SemaphoreType.DMA((self.n * (N_DEV - 1),)), pltpu.SemaphoreType.DMA((self.n * (N_DEV - 1),)),
                pltpu.SemaphoreType.DMA((self.n,))]

    def _copies(self, ins, outs, sems):
        send_sems, recv_sems, local_sems = sems
        x, y, c, me = _my_place()
        local = [pltpu.make_async_copy(ins[k] if self.gather[k] else ins[k].at[me], outs[k].at[me], local_sems.at[k])
                 for k in range(self.n)]
        sent, received = [], []
        for p in range(1, N_DEV):
            place, num = _peer(x, y, c, p)
            for k in range(self.n):
                src = ins[k] if self.gather[k] else ins[k].at[num]
                idx = k * (N_DEV - 1) + p - 1
                mk = lambda dst: pltpu.make_async_remote_copy(src_ref=src, dst_ref=dst, send_sem=send_sems.at[idx],
                                                              recv_sem=recv_sems.at[idx], device_id=place, device_id_type=MESH)
                sent.append(mk(outs[k].at[me]))
                received.append(mk(outs[k].at[num]))
        return local, sent, received

    def start(self, ins, outs, sems):
        local, sent, _ = self._copies(ins, outs, sems)
        for cp in local + sent:
            cp.start()

    def forward(self, ins, outs, sems):
        pass

    def finish(self, ins, outs, sems):
        local, sent, received = self._copies(ins, outs, sems)
        for cp in received:
            cp.wait_recv()
        for cp in sent:
            cp.wait_send()
        for cp in local:
            cp.wait()


class _Gather:
    def __init__(self, arrays):
        self.arrays, self.n = list(arrays), len(arrays)

    def out_shape(self):
        return [jax.ShapeDtypeStruct((N_DEV,) + a.shape, a.dtype) for a in self.arrays]

    def sems(self):
        return [pltpu.SemaphoreType.DMA((self.n * (N_DEV - 1),)), pltpu.SemaphoreType.DMA((self.n * (N_DEV - 1),)),
                pltpu.SemaphoreType.DMA((self.n,))]

    def _plan(self, ins, outs, sems):
        send_sems, recv_sems, local_sems = sems
        x, y, c, me = _my_place()
        sibling = (x, y, 1 - c)
        chips = [(1 - x, y), (x, 1 - y), (1 - x, 1 - y)]
        num = lambda px, py, pc: 4 * px + 2 * py + pc

        def copy(k, i, block, to, src=None):
            slot = outs[k].at[num(*block)]
            return pltpu.make_async_remote_copy(src_ref=slot if src is None else src, dst_ref=slot,
                                                send_sem=send_sems.at[k * (N_DEV - 1) + i],
                                                recv_sem=recv_sems.at[k * (N_DEV - 1) + i],
                                                device_id=to, device_id_type=MESH)

        local = [pltpu.make_async_copy(ins[k], outs[k].at[me], local_sems.at[k]) for k in range(self.n)]
        return (x, y, c), sibling, chips, copy, local

    def start(self, ins, outs, sems):
        me, sibling, chips, copy, local = self._plan(ins, outs, sems)
        for cp in local:
            cp.start()
        for k in range(self.n):
            copy(k, 0, me, sibling, src=ins[k]).start()
            for j, chip in enumerate(chips):
                copy(k, 1 + j, me, (*chip, me[2]), src=ins[k]).start()

    def forward(self, ins, outs, sems):
        me, sibling, chips, copy, _ = self._plan(ins, outs, sems)
        for j, chip in enumerate(chips):
            for k in range(self.n):
                copy(k, 1 + j, (*chip, me[2]), me).wait_recv()
                copy(k, 4 + j, (*chip, me[2]), sibling).start()

    def finish(self, ins, outs, sems):
        me, sibling, chips, copy, local = self._plan(ins, outs, sems)
        for k in range(self.n):
            copy(k, 0, sibling, me).wait_recv()
            for j, chip in enumerate(chips):
                copy(k, 4 + j, (*chip, 1 - me[2]), me).wait_recv()
        for k in range(self.n):
            copy(k, 0, me, sibling, src=ins[k]).wait_send()
            for j, chip in enumerate(chips):
                copy(k, 1 + j, me, (*chip, me[2]), src=ins[k]).wait_send()
                copy(k, 4 + j, (*chip, me[2]), sibling).wait_send()
        for cp in local:
            cp.wait()


def _pcall(body, *, name, grid, in_specs, out_specs, out_shape, scratch=(), carry=None):
    params = pltpu.CompilerParams(dimension_semantics=("arbitrary",) * len(grid), vmem_limit_bytes=VMEM_LIMIT_MB << 20)
    if carry is None:
        return pl.pallas_call(body, name=name, grid=grid, in_specs=in_specs, out_specs=out_specs, out_shape=out_shape,
                              scratch_shapes=list(scratch), compiler_params=params)
    single = not isinstance(out_specs, (list, tuple))
    out_specs = [out_specs] if single else list(out_specs)
    out_shape = [out_shape] if single else list(out_shape)
    n_in, n_out, n_scr, na = len(in_specs), len(out_specs), len(scratch), carry.n

    def wrapped(*refs):
        ins, cin = refs[:n_in], refs[n_in:n_in + na]
        outs, cout = refs[n_in + na:n_in + na + n_out], refs[n_in + na + n_out:n_in + 2 * na + n_out]
        scr, sems = refs[n_in + 2 * na + n_out:n_in + 2 * na + n_out + n_scr], refs[n_in + 2 * na + n_out + n_scr:]
        total = math.prod(grid)
        step = functools.reduce(lambda a, d: a * grid[d] + pl.program_id(d), range(len(grid)), 0)

        @pl.when(step == 0)
        def _():
            carry.start(cin, cout, sems)

        body(*ins, *outs, *scr)

        @pl.when(step == min(total * 7 // 8, total - 1))
        def _():
            carry.forward(cin, cout, sems)

        @pl.when(step == total - 1)
        def _():
            carry.finish(cin, cout, sems)

    any_spec = pl.BlockSpec(memory_space=pl.ANY)
    call = pl.pallas_call(wrapped, name=name, grid=grid, in_specs=list(in_specs) + [any_spec] * na,
                          out_specs=out_specs + [any_spec] * na, out_shape=out_shape + carry.out_shape(),
                          scratch_shapes=list(scratch) + carry.sems(), compiler_params=params)

    def run(*args):
        res = call(*args, *carry.arrays)
        main = res[0] if single else list(res[:n_out])
        return main, list(res[n_out:])

    return run


def _pick(dim, pref):
    if dim <= pref:
        return dim
    c = pref
    while c >= 128:
        if dim % c == 0 and c % 128 == 0:
            return c
        c -= 128
    return dim


def _rows(t, width, target_bytes=2 << 20):
    r = max(8, min(t, target_bytes // (4 * width)))
    r = 1 << (r.bit_length() - 1)
    while t % r:
        r //= 2
    return r


MM_FULL_K = 2048


def _mm(a, b, *, name, ta=False, tb=False, res=None, out_dtype=F32, bm=1024, bn=1024, bk=1024, carry=None, pair=None):
    m, k = (a.shape[1], a.shape[0]) if ta else a.shape
    n = b.shape[0] if tb else b.shape[1]
    assert (b.shape[1] if tb else b.shape[0]) == k
    bm, bn, bk = _pick(m, bm), _pick(n, bn), (k if k <= MM_FULL_K else _pick(k, bk))
    nk = k // bk
    dims = (((0,) if ta else (1,), (1,) if tb else (0,)), ((), ()))
    n_ab = 2 if pair is None else 4

    def body(*refs):
        a_ref, b_ref = refs[:2]
        r_ref = refs[n_ab] if res is not None else None
        o_ref = refs[n_ab + 1] if res is not None else refs[n_ab]
        part = lax.dot_general(a_ref[...].astype(BF16), b_ref[...].astype(BF16), dims, preferred_element_type=F32)
        if pair is not None:
            part = part + lax.dot_general(refs[2][...].astype(BF16), refs[3][...].astype(BF16), dims,
                                          preferred_element_type=F32)

        def finish(out):
            if res is not None:
                out = out + r_ref[...]
            o_ref[...] = out.astype(o_ref.dtype)

        if nk == 1:
            finish(part)
            return
        acc_ref = refs[-1]
        kk = pl.program_id(2)

        @pl.when(kk == 0)
        def _():
            acc_ref[...] = part

        @pl.when((kk > 0) & (kk < nk - 1))
        def _():
            acc_ref[...] += part

        @pl.when(kk == nk - 1)
        def _():
            finish(acc_ref[...] + part)

    a_spec = pl.BlockSpec((bk, bm), lambda i, j, kk: (kk, i)) if ta else pl.BlockSpec((bm, bk), lambda i, j, kk: (i, kk))
    b_spec = pl.BlockSpec((bn, bk), lambda i, j, kk: (j, kk)) if tb else pl.BlockSpec((bk, bn), lambda i, j, kk: (kk, j))
    o_spec = pl.BlockSpec((bm, bn), lambda i, j, kk: (i, j))
    ins, specs = [a, b], [a_spec, b_spec]
    if pair is not None:
        assert pair[0].shape == a.shape and pair[1].shape == b.shape
        ins += list(pair)
        specs += [a_spec, b_spec]
    if res is not None:
        ins.append(res)
        specs.append(o_spec)
    return _pcall(body, name=name, grid=(m // bm, n // bn, nk), in_specs=specs, out_specs=o_spec,
                  out_shape=jax.ShapeDtypeStruct((m, n), out_dtype),
                  scratch=[pltpu.VMEM((bm, bn), F32)] if nk > 1 else [], carry=carry)(*ins)


def _mm_dw2(a, b1, b2, *, name, bm=1024, bn=FF_WIDE, bk=1024):
    k, m = a.shape
    n = b1.shape[1]
    assert b1.shape == b2.shape == (k, n)
    bm, bn, bk = _pick(m, bm), _pick(n, bn), _pick(k, bk)
    nk = k // bk

    def body(a_ref, b1_ref, b2_ref, o1_ref, o2_ref, acc1_ref, acc2_ref):
        kk = pl.program_id(2)
        av = a_ref[...].astype(BF16)
        for b_ref, o_ref, acc_ref in ((b1_ref, o1_ref, acc1_ref), (b2_ref, o2_ref, acc2_ref)):
            part = lax.dot_general(av, b_ref[...].astype(BF16), (TN, ((), ())), preferred_element_type=F32)

            @pl.when(kk == 0)
            def _():
                acc_ref[...] = part

            @pl.when((kk > 0) & (kk < nk - 1))
            def _():
                acc_ref[...] += part

            @pl.when(kk == nk - 1)
            def _():
                o_ref[...] = (acc_ref[...] + part).astype(o_ref.dtype)

    b_spec = pl.BlockSpec((bk, bn), lambda i, j, kk: (kk, j))
    o_spec = pl.BlockSpec((bm, bn), lambda i, j, kk: (i, j))
    sds = jax.ShapeDtypeStruct((m, n), BF16)
    return _pcall(body, name=name, grid=(m // bm, n // bn, nk),
                  in_specs=[pl.BlockSpec((bk, bm), lambda i, j, kk: (kk, i)), b_spec, b_spec],
                  out_specs=[o_spec, o_spec], out_shape=[sds, sds],
                  scratch=[pltpu.VMEM((bm, bn), F32), pltpu.VMEM((bm, bn), F32)])(a, b1, b2)


def _rms_fwd(x, w, *, name, width, heads=1, col0=0, out_dtype=BF16):
    t = x.shape[0]
    tm = _rows(t, width)
    cb = col0 // width

    def body(x_ref, w_ref, o_ref):
        xv = x_ref[...]
        r = lax.rsqrt(jnp.mean(xv * xv, axis=-1, keepdims=True) + EPS)
        o_ref[...] = (xv * r * w_ref[...]).astype(o_ref.dtype)

    return _pcall(body, name=name, grid=(t // tm, heads),
                  in_specs=[pl.BlockSpec((tm, width), lambda i, h: (i, cb + h)),
                            pl.BlockSpec((1, width), lambda i, h: (0, 0))],
                  out_specs=pl.BlockSpec((tm, width), lambda i, h: (i, h)),
                  out_shape=jax.ShapeDtypeStruct((t, heads * width), out_dtype))(x, w)


def _rms_bwd(x, w, dy, *, name, width, heads=1, col0=0, dcol0=0, res=None, out_dtype=F32, with_delta=False):
    t = x.shape[0]
    tm = _rows(t, width)
    cb, dcb = col0 // width, dcol0 // width

    def body(*refs):
        refs = list(refs)
        x_ref, w_ref, dy_ref = refs[:3]
        r_ref = refs[3] if res is not None else None
        outs = refs[4:] if res is not None else refs[3:]
        dx_ref, dw_ref = outs[:2]
        xv = x_ref[...]
        dyv = dy_ref[...].astype(F32)
        r = lax.rsqrt(jnp.mean(xv * xv, axis=-1, keepdims=True) + EPS)
        xh = xv * r
        dyw = dyv * w_ref[...]
        dx = r * (dyw - xh * jnp.mean(dyw * xh, axis=-1, keepdims=True))
        if with_delta:
            outs[2][...] = jnp.broadcast_to(jnp.sum(dx * xv, axis=-1, keepdims=True), dx.shape)
        if res is not None:
            dx = dx + r_ref[...]
        dx_ref[...] = dx.astype(dx_ref.dtype)

        @pl.when((pl.program_id(0) == 0) & (pl.program_id(1) == 0))
        def _():
            dw_ref[...] = jnp.zeros_like(dw_ref)

        dw_ref[...] += (dyv * xh).reshape(tm // 8, 8, width).sum(axis=0)

    blk = pl.BlockSpec((tm, width), lambda i, h: (i, h))
    ins = [x, w, dy]
    specs = [pl.BlockSpec((tm, width), lambda i, h: (i, cb + h)), pl.BlockSpec((1, width), lambda i, h: (0, 0)),
             pl.BlockSpec((tm, width), lambda i, h: (i, dcb + h))]
    if res is not None:
        ins.append(res)
        specs.append(blk)
    out_shape = [jax.ShapeDtypeStruct((t, heads * width), out_dtype), jax.ShapeDtypeStruct((8, width), F32)]
    out_specs = [blk, pl.BlockSpec((8, width), lambda i, h: (0, 0))]
    if with_delta:
        out_shape.append(jax.ShapeDtypeStruct((t, heads * width), F32))
        out_specs.append(blk)
    return _pcall(body, name=name, grid=(t // tm, heads), in_specs=specs, out_specs=out_specs, out_shape=out_shape)(*ins)


def _sig(x):
    return 1.0 / (1.0 + jnp.exp(-x))


@jax.custom_vjp
def _sigmoid(x):
    return _sig(x)


def _sigmoid_fwd(x):
    s = _sig(x)
    return s, s


def _sigmoid_bwd(s, g):
    return (g * s * (1.0 - s),)


_sigmoid.defvjp(_sigmoid_fwd, _sigmoid_bwd)


@jax.custom_vjp
def _softplus(x):
    return jnp.maximum(x, 0.0) + jnp.log(1.0 + jnp.exp(-jnp.abs(x)))


def _softplus_fwd(x):
    return _softplus(x), x


def _softplus_bwd(x, g):
    return (g * _sig(x),)


_softplus.defvjp(_softplus_fwd, _softplus_bwd)


def _silu(x):
    return x * _sig(x)


def _dsilu(x):
    s = _sig(x)
    return s * (1.0 + x * (1.0 - s))


NN3 = (((2,), (1,)), ((0,), (0,)))
NT3 = (((2,), (2,)), ((0,), (0,)))
TN3 = (((1,), (1,)), ((0,), (0,)))


def _bdot(a, b, dims):
    return lax.dot_general(a.astype(BF16), b.astype(BF16), dims, preferred_element_type=F32)


def _bf16_part(x):
    bits = lax.bitcast_convert_type(x, jnp.uint32) & jnp.uint32(0xFFFF0000)
    return lax.bitcast_convert_type(bits, F32)


def _scan_rows(x, reverse):
    c = x.shape[1]
    row = lax.broadcasted_iota(jnp.int32, x.shape, 1)
    step = 1
    while step < c:
        if reverse:
            x = x + jnp.where(row < c - step, pltpu.roll(x, c - step, axis=1), 0.0)
        else:
            x = x + jnp.where(row >= step, pltpu.roll(x, step, axis=1), 0.0)
        step *= 2
    return x


@jax.custom_vjp
def _prefix_rows(x):
    return _scan_rows(x, False)


_prefix_rows.defvjp(lambda x: (_scan_rows(x, False), None), lambda _, g: (_scan_rows(g, True),))


def _dot3(a, b, dims):
    (ca,), (cb,) = dims[0]
    a_hi, b_hi = _bf16_part(a), _bf16_part(b)
    a_lo, b_lo = (a - a_hi).astype(BF16), (b - b_hi).astype(BF16)
    a_hi, b_hi = a_hi.astype(BF16), b_hi.astype(BF16)
    return lax.dot_general(jnp.concatenate([a_hi, a_hi, a_lo], axis=ca), jnp.concatenate([b_hi, b_lo, b_hi], axis=cb),
                           dims, preferred_element_type=F32)


@jax.custom_vjp
def _nn_hi(a, b):
    return _dot3(a, b, NN3)


_nn_hi.defvjp(lambda a, b: (_dot3(a, b, NN3), (a, b)), lambda r, g: (_dot3(g, r[1], NT3), _dot3(r[0], g, TN3)))


@jax.custom_vjp
def _nn(a, b):
    return _bdot(a, b, NN3)


_nn.defvjp(lambda a, b: (_bdot(a, b, NN3), (a, b)), lambda r, g: (_bdot(g, r[1], NT3), _bdot(r[0], g, TN3)))


@jax.custom_vjp
def _nt(a, b):
    return _bdot(a, b, NT3)


_nt.defvjp(lambda a, b: (_bdot(a, b, NT3), (a, b)), lambda r, g: (_bdot(g, r[1], NN3), _bdot(g, r[0], TN3)))


@jax.custom_vjp
def _tn(a, b):
    return _bdot(a, b, TN3)


_tn.defvjp(lambda a, b: (_bdot(a, b, TN3), (a, b)), lambda r, g: (_bdot(r[1], g, NT3), _bdot(r[0], g, NN3)))


CONV_ROWS, CONV_COLS = 256, 1024


def _rows_down(cur, prev8, s):
    r = pltpu.roll(cur, s, axis=0)
    rp = pltpu.roll(prev8, s, axis=0)
    row = lax.broadcasted_iota(jnp.int32, rp.shape, 0)
    head = jnp.where(row < s, rp, r[:8])
    return head if cur.shape[0] == 8 else jnp.concatenate([head, r[8:]], axis=0)


def _rows_up(cur, next8, s):
    n = cur.shape[0]
    r = pltpu.roll(cur, n - s, axis=0)
    rn = pltpu.roll(next8, 8 - s, axis=0)
    row = lax.broadcasted_iota(jnp.int32, rn.shape, 0)
    tail = jnp.where(row >= 8 - s, rn, r[n - 8:])
    return tail if n == 8 else jnp.concatenate([r[:n - 8], tail], axis=0)


def _conv_taps(cur, prev8):
    return [_rows_down(cur, prev8, GDN_CONV - 1 - j) for j in range(GDN_CONV - 1)] + [cur]


def _conv_pre(taps, w):
    acc = w[0:1] * taps[0]
    for j in range(1, GDN_CONV):
        acc = acc + w[j:j + 1] * taps[j]
    return acc


def _conv_fwd(proj, conv_w, *, name):
    t = proj.shape[0]
    tm, tc = _pick(t, CONV_ROWS), CONV_COLS
    nb = tm // 8

    def body(u_ref, p_ref, w_ref, o_ref):
        i = pl.program_id(1)
        prev = jnp.where(i > 0, p_ref[...], 0.0)
        o_ref[...] = _silu(_conv_pre(_conv_taps(u_ref[...], prev), w_ref[...]))

    return _pcall(body, name=name, grid=(CONV_CH // tc, t // tm),
                  in_specs=[pl.BlockSpec((tm, tc), lambda j, i: (i, j)),
                            pl.BlockSpec((8, tc), lambda j, i: (jnp.maximum(i * nb - 1, 0), j)),
                            pl.BlockSpec((8, tc), lambda j, i: (0, j))],
                  out_specs=pl.BlockSpec((tm, tc), lambda j, i: (i, j)),
                  out_shape=jax.ShapeDtypeStruct((t, CONV_CH), F32))(proj, proj, conv_w)


def _conv_bwd(proj, conv_w, dy, *, name):
    t = proj.shape[0]
    tm, tc = _pick(t, CONV_ROWS), CONV_COLS
    nb = tm // 8
    last = t // tm - 1

    def body(u_ref, p_ref, n_ref, dy_ref, dyn_ref, w_ref, du_ref, dw_ref):
        i = pl.program_id(1)
        w = w_ref[...]
        cur = u_ref[...]
        taps = _conv_taps(cur, jnp.where(i > 0, p_ref[...], 0.0))
        dc = dy_ref[...] * _dsilu(_conv_pre(taps, w))
        taps_next = _conv_taps(n_ref[...], cur[tm - 8:])
        dc_next = jnp.where(i < last, dyn_ref[...], 0.0) * _dsilu(_conv_pre(taps_next, w))
        du = w[3:4] * dc
        for j in range(GDN_CONV - 1):
            du = du + w[j:j + 1] * _rows_up(dc, dc_next, GDN_CONV - 1 - j)
        du_ref[...] = du.astype(du_ref.dtype)

        @pl.when(i == 0)
        def _():
            dw_ref[...] = jnp.zeros_like(dw_ref)

        for j in range(GDN_CONV):
            dw_ref[j] += (dc * taps[j]).reshape(nb, 8, tc).sum(axis=0)

    cur = lambda j, i: (i, j)
    return _pcall(body, name=name, grid=(CONV_CH // tc, t // tm),
                  in_specs=[pl.BlockSpec((tm, tc), cur),
                            pl.BlockSpec((8, tc), lambda j, i: (jnp.maximum(i * nb - 1, 0), j)),
                            pl.BlockSpec((8, tc), lambda j, i: (jnp.minimum((i + 1) * nb, t // 8 - 1), j)),
                            pl.BlockSpec((tm, tc), cur),
                            pl.BlockSpec((8, tc), lambda j, i: (jnp.minimum((i + 1) * nb, t // 8 - 1), j)),
                            pl.BlockSpec((8, tc), lambda j, i: (0, j))],
                  out_specs=[pl.BlockSpec((tm, tc), cur), pl.BlockSpec((GDN_CONV, 8, tc), lambda j, i: (0, 0, j))],
                  out_shape=[jax.ShapeDtypeStruct((t, CONV_CH), BF16), jax.ShapeDtypeStruct((GDN_CONV, 8, CONV_CH), F32)],
                  )(proj, proj, proj, dy, dy, conv_w)


def _gdn_chunk(q_raw, k_raw, v, misc, params, state):
    nh, c = q_raw.shape[0], q_raw.shape[1]
    lane = lax.broadcasted_iota(jnp.int32, misc.shape, 1)
    prow = lax.broadcasted_iota(jnp.int32, params.shape, 0)
    plane = lax.broadcasted_iota(jnp.int32, params.shape, 1)
    heads = lambda pieces: jnp.concatenate([p[None] for p in pieces], axis=0)
    col = lambda at: heads([jnp.sum(jnp.where(lane == at + h, misc, 0.0), axis=1, keepdims=True) for h in range(nh)])
    par = lambda row: heads([jnp.sum(jnp.where((prow == row) & (plane == h), params, 0.0), keepdims=True)
                             for h in range(nh)])
    b_raw, a_raw = col(LANE_B), col(LANE_A)
    a_log, dt_bias = par(0), par(1)
    beta = _sigmoid(b_raw)
    g = -jnp.exp(a_log) * _softplus(a_raw + dt_bias)

    q = q_raw * lax.rsqrt(jnp.sum(q_raw * q_raw, axis=-1, keepdims=True) + EPS) * (HEAD ** -0.5)
    k = k_raw * lax.rsqrt(jnp.sum(k_raw * k_raw, axis=-1, keepdims=True) + EPS)

    ri = lax.broadcasted_iota(jnp.int32, (c, c), 0)
    ci = lax.broadcasted_iota(jnp.int32, (c, c), 1)
    tril, strict = ri >= ci, ri > ci
    gc = _prefix_rows(g)
    gc_col = jnp.broadcast_to(gc, (nh, c, c))
    gc_row = jnp.swapaxes(gc_col, 1, 2)
    decay = jnp.exp(jnp.where(tril, gc_col - gc_row, -1e30))

    kb = k * beta
    vb = v * beta
    a_mat = jnp.where(strict, _nt(kb, k) * decay, 0.0)
    x = -a_mat
    inv = (ri == ci).astype(F32) + x
    for _ in range(5):
        x = _nn_hi(x, x)
        inv = inv + _nn_hi(inv, x)
    u = _nn_hi(inv, vb)
    w = _nn_hi(inv, kb * jnp.exp(gc))
    intra = _nt(q, k) * decay

    v_new = u - _nn(w, state)
    o = _nn(q * jnp.exp(gc), state) + _nn(intra, v_new)
    g_last = jnp.sum(g, axis=1, keepdims=True)
    k_dec = k * jnp.exp(g_last - gc)
    new_state = state * jnp.exp(g_last) + _tn(k_dec, v_new)
    return o, new_state


def _gdn_specs(nc, rev):
    cidx = (lambda n: nc - 1 - n) if rev else (lambda n: n)
    hb = lambda part: pl.BlockSpec((GDN_CHUNK, GDN_QK), lambda n: (cidx(n), part))
    misc = pl.BlockSpec((GDN_CHUNK, HEAD), lambda n: (cidx(n), COL_MISC // HEAD))
    params = pl.BlockSpec((8, HEAD), lambda n: (0, 0))
    hist = pl.BlockSpec((1, GDN_HEADS, HEAD, HEAD), lambda n: (cidx(n), 0, 0, 0))
    return hb, misc, params, hist


def _split_heads(v):
    return jnp.stack([v[:, h * HEAD:(h + 1) * HEAD] for h in range(v.shape[1] // HEAD)])


def _merge_heads(v):
    return jnp.concatenate([v[h] for h in range(v.shape[0])], axis=1)


def _gdn_fwd(qkv, proj, params, *, name):
    t = qkv.shape[0]
    nc = t // GDN_CHUNK
    hb, misc, pspec, hist = _gdn_specs(nc, False)

    def body(q_ref, k_ref, v_ref, m_ref, p_ref, o_ref, hist_ref, s_ref):
        @pl.when(pl.program_id(0) == 0)
        def _():
            s_ref[...] = jnp.zeros_like(s_ref)

        state = s_ref[...]
        hist_ref[0] = state
        o, new_state = _gdn_chunk(_split_heads(q_ref[...]), _split_heads(k_ref[...]), _split_heads(v_ref[...]),
                                  m_ref[...], p_ref[...], state)
        o_ref[...] = _merge_heads(o)
        s_ref[...] = new_state

    return _pcall(body, name=name, grid=(nc,),
                  in_specs=[hb(0), hb(1), hb(2), misc, pspec],
                  out_specs=[hb(0), hist],
                  out_shape=[jax.ShapeDtypeStruct((t, GDN_QK), F32),
                             jax.ShapeDtypeStruct((nc, GDN_HEADS, HEAD, HEAD), F32)],
                  scratch=[pltpu.VMEM((GDN_HEADS, HEAD, HEAD), F32)])(qkv, qkv, qkv, proj, params)


def _gdn_bwd(qkv, proj, params, hist_arr, do, dmisc_in, *, name, carry=None):
    t = qkv.shape[0]
    nc = t // GDN_CHUNK
    hb, misc, pspec, hist = _gdn_specs(nc, True)
    mrow = pl.BlockSpec((GDN_CHUNK, HEAD), lambda n: (nc - 1 - n, 0))

    def body(q_ref, k_ref, v_ref, m_ref, p_ref, hist_ref, do_ref, dmi_ref, dqkv_ref, dm_ref, dp_ref, ds_ref):
        @pl.when(pl.program_id(0) == 0)
        def _():
            ds_ref[...] = jnp.zeros_like(ds_ref)
            dp_ref[...] = jnp.zeros_like(dp_ref)

        _, vjp = jax.vjp(_gdn_chunk, _split_heads(q_ref[...]), _split_heads(k_ref[...]), _split_heads(v_ref[...]),
                         m_ref[...], p_ref[...], hist_ref[0])
        dq, dk, dv, dm, dp, ds = vjp((_split_heads(do_ref[...]), ds_ref[...]))
        dqkv_ref[:, 0:GDN_QK] = _merge_heads(dq)
        dqkv_ref[:, GDN_QK:2 * GDN_QK] = _merge_heads(dk)
        dqkv_ref[:, 2 * GDN_QK:] = _merge_heads(dv)
        ds_ref[...] = ds
        dm_ref[...] = dmi_ref[...] + dm
        dp_ref[...] += dp

    return _pcall(body, name=name, grid=(nc,),
                  in_specs=[hb(0), hb(1), hb(2), misc, pspec, hist, hb(0), mrow],
                  out_specs=[pl.BlockSpec((GDN_CHUNK, CONV_CH), lambda n: (nc - 1 - n, 0)), mrow, pspec],
                  out_shape=[jax.ShapeDtypeStruct((t, CONV_CH), F32), jax.ShapeDtypeStruct((t, HEAD), F32),
                             jax.ShapeDtypeStruct((8, HEAD), F32)],
                  scratch=[pltpu.VMEM((GDN_HEADS, HEAD, HEAD), F32)], carry=carry,
                  )(qkv, qkv, qkv, proj, params, hist_arr, do, dmisc_in)


def _gate_fwd(o_raw, proj, w, *, name):
    t = o_raw.shape[0]
    tm = _rows(t, HEAD)
    zb = COL_Z // HEAD

    def body(o_ref, z_ref, w_ref, out_ref):
        ov = o_ref[...]
        r = lax.rsqrt(jnp.mean(ov * ov, axis=-1, keepdims=True) + EPS)
        out_ref[...] = (ov * r * w_ref[...] * _silu(z_ref[...])).astype(out_ref.dtype)

    blk = pl.BlockSpec((tm, HEAD), lambda i, h: (i, h))
    return _pcall(body, name=name, grid=(t // tm, GDN_HEADS),
                  in_specs=[blk, pl.BlockSpec((tm, HEAD), lambda i, h: (i, zb + h)), pl.BlockSpec((1, HEAD), lambda i, h: (0, 0))],
                  out_specs=blk, out_shape=jax.ShapeDtypeStruct((t, GDN_QK), BF16))(o_raw, proj, w)


def _gate_bwd(o_raw, proj, w, dmixed, *, name):
    t = o_raw.shape[0]
    tm = _rows(t, HEAD)
    zb = COL_Z // HEAD

    def body(o_ref, z_ref, w_ref, dy_ref, do_ref, dz_ref, dw_ref):
        ov, zv, dyv = o_ref[...], z_ref[...], dy_ref[...]
        r = lax.rsqrt(jnp.mean(ov * ov, axis=-1, keepdims=True) + EPS)
        xh = ov * r
        dn = dyv * _silu(zv)
        dz_ref[...] = (dyv * xh * w_ref[...] * _dsilu(zv)).astype(dz_ref.dtype)
        dnw = dn * w_ref[...]
        do_ref[...] = r * (dnw - xh * jnp.mean(dnw * xh, axis=-1, keepdims=True))

        @pl.when((pl.program_id(0) == 0) & (pl.program_id(1) == 0))
        def _():
            dw_ref[...] = jnp.zeros_like(dw_ref)

        dw_ref[...] += (dn * xh).reshape(tm // 8, 8, HEAD).sum(axis=0)

    blk = pl.BlockSpec((tm, HEAD), lambda i, h: (i, h))
    return _pcall(body, name=name, grid=(t // tm, GDN_HEADS),
                  in_specs=[blk, pl.BlockSpec((tm, HEAD), lambda i, h: (i, zb + h)), pl.BlockSpec((1, HEAD), lambda i, h: (0, 0)), blk],
                  out_specs=[blk, blk, pl.BlockSpec((8, HEAD), lambda i, h: (0, 0))],
                  out_shape=[jax.ShapeDtypeStruct((t, GDN_QK), F32), jax.ShapeDtypeStruct((t, GDN_QK), BF16),
                             jax.ShapeDtypeStruct((8, HEAD), F32)])(o_raw, proj, w, dmixed)


def _rope_tables():
    half = QK_ROPE // 2
    inv = ROPE_THETA ** (-jnp.arange(half, dtype=F32) / half)
    zeros = jnp.zeros((HEAD - QK_ROPE,), F32)
    inv_row = jnp.concatenate([inv, inv, zeros])
    sign_row = jnp.concatenate([-jnp.ones((half,), F32), jnp.ones((half,), F32), zeros])
    mask_row = jnp.concatenate([jnp.ones((QK_ROPE,), F32), zeros])
    return jnp.concatenate([inv_row[None], sign_row[None], mask_row[None], jnp.zeros((5, HEAD), F32)], axis=0)


def _rope_cs(pos, tab, *, name):
    t = pos.shape[0]
    tm = _pick(t, 1024)

    def body(pos_ref, tab_ref, o_ref):
        tab = tab_ref[...]
        ang = pos_ref[...] * tab[0:1]
        o_ref[...] = jnp.concatenate([jnp.cos(ang) * tab[2:3], jnp.sin(ang) * tab[1:2]], axis=1)

    return _pcall(body, name=name, grid=(t // tm,),
                  in_specs=[pl.BlockSpec((tm, 1), lambda i: (i, 0)), pl.BlockSpec((8, HEAD), lambda i: (0, 0))],
                  out_specs=pl.BlockSpec((tm, 2 * HEAD), lambda i: (i, 0)),
                  out_shape=jax.ShapeDtypeStruct((t, 2 * HEAD), F32))(pos, tab)


def _rotate(x, cs, sign):
    lane = lax.broadcasted_iota(jnp.int32, x.shape, 1)
    half = QK_ROPE // 2
    partner = jnp.where(lane < half, pltpu.roll(x, HEAD - half, axis=1), pltpu.roll(x, half, axis=1))
    return x * cs[:, :HEAD] + partner * (cs[:, HEAD:] * sign)


def _q_rot(q, cs, *, name, sign, out_dtype=BF16):
    t = q.shape[0]
    tm = _pick(t, 1024)
    scale = (HEAD + QK_ROPE) ** -0.5

    def body(q_ref, cs_ref, o_ref):
        qv = q_ref[...].astype(F32)
        rot = _rotate(qv[:, HEAD:], cs_ref[...], sign)
        o_ref[...] = (jnp.concatenate([qv[:, :HEAD], rot], axis=1) * scale).astype(o_ref.dtype)

    blk = pl.BlockSpec((tm, QHEAD), lambda i, h: (i, h))
    return _pcall(body, name=name, grid=(t // tm, MLA_HEADS),
                  in_specs=[blk, pl.BlockSpec((tm, 2 * HEAD), lambda i, h: (i, 0))],
                  out_specs=blk, out_shape=jax.ShapeDtypeStruct((t, MLA_HEADS * QHEAD), out_dtype))(q, cs)


def _q_up(cqn, wuq_p, cs, *, name):
    t, lora = cqn.shape
    tm = _pick(t, 1024)
    scale = (HEAD + QK_ROPE) ** -0.5

    def body(a_ref, w_ref, cs_ref, o_ref):
        qv = lax.dot_general(a_ref[...], w_ref[...], (NN, ((), ())), preferred_element_type=F32)
        rot = _rotate(qv[:, HEAD:], cs_ref[...], 1.0)
        o_ref[...] = (jnp.concatenate([qv[:, :HEAD], rot], axis=1) * scale).astype(o_ref.dtype)

    return _pcall(body, name=name, grid=(t // tm, MLA_HEADS),
                  in_specs=[pl.BlockSpec((tm, lora), lambda i, h: (i, 0)), pl.BlockSpec((lora, QHEAD), lambda i, h: (0, h)),
                            pl.BlockSpec((tm, 2 * HEAD), lambda i, h: (i, 0))],
                  out_specs=pl.BlockSpec((tm, QHEAD), lambda i, h: (i, h)),
                  out_shape=jax.ShapeDtypeStruct((t, MLA_HEADS * QHEAD), BF16))(cqn, wuq_p, cs)


def _kv_up(ckvn, wukv, proj, cs, *, name):
    t, lora = ckvn.shape
    tm = _pick(t, 1024)

    def body(a_ref, w_ref, m_ref, cs_ref, k_ref, v_ref):
        kvv = lax.dot_general(a_ref[...], w_ref[...], (NN, ((), ())), preferred_element_type=F32)
        misc = m_ref[...]
        lane = lax.broadcasted_iota(jnp.int32, misc.shape, 1)
        rot = _rotate(jnp.where(lane < QK_ROPE, misc, 0.0), cs_ref[...], 1.0)
        k_ref[...] = jnp.concatenate([kvv[:, :HEAD], rot], axis=1).astype(k_ref.dtype)
        v_ref[...] = kvv[:, HEAD:].astype(v_ref.dtype)

    return _pcall(body, name=name, grid=(t // tm, MLA_HEADS),
                  in_specs=[pl.BlockSpec((tm, lora), lambda i, h: (i, 0)), pl.BlockSpec((lora, QHEAD), lambda i, h: (0, h)),
                            pl.BlockSpec((tm, HEAD), lambda i, h: (i, COL_MISC // HEAD)),
                            pl.BlockSpec((tm, 2 * HEAD), lambda i, h: (i, 0))],
                  out_specs=[pl.BlockSpec((tm, QHEAD), lambda i, h: (i, h)), pl.BlockSpec((tm, HEAD), lambda i, h: (i, h))],
                  out_shape=[jax.ShapeDtypeStruct((t, MLA_HEADS * QHEAD), BF16), jax.ShapeDtypeStruct((t, MLA_HEADS * HEAD), BF16)],
                  )(ckvn, wukv, proj, cs)


def _krope_bwd(dkr, cs, *, name):
    t = dkr.shape[0]
    tm = _pick(t, 512)

    def body(d_ref, cs_ref, o_ref):
        d = d_ref[...]
        acc = d[:, :HEAD]
        for h in range(1, MLA_HEADS):
            acc = acc + d[:, h * HEAD:(h + 1) * HEAD]
        o_ref[...] = _rotate(acc, cs_ref[...], -1.0)

    return _pcall(body, name=name, grid=(t // tm,),
                  in_specs=[pl.BlockSpec((tm, MLA_HEADS * HEAD), lambda i: (i, 0)), pl.BlockSpec((tm, 2 * HEAD), lambda i: (i, 0))],
                  out_specs=pl.BlockSpec((tm, HEAD), lambda i: (i, 0)),
                  out_shape=jax.ShapeDtypeStruct((t, HEAD), F32))(dkr, cs)


NEG = -1e30


def _tri(step, counts):
    starts = [sum(counts[:o]) for o in range(len(counts))]
    outer = sum([(step >= s).astype(jnp.int32) for s in starts[1:]], jnp.int32(0))
    start = sum([(step >= starts[o]).astype(jnp.int32) * (starts[o] - starts[o - 1]) for o in range(1, len(counts))], jnp.int32(0))
    return outer, step - start


def _attn_fwd(q, k, v, *, name, tq=1024, tk=1024, carry=None):
    t = q.shape[0]
    tq, tk = _pick(t, tq), _pick(t, tk)
    nq = t // tq
    last_kv = lambda i: (i * tq + tq - 1) // tk
    counts = [last_kv(i) + 1 for i in range(nq)]

    def body(q_ref, k_ref, v_ref, o_ref, lse_ref, m_ref, l_ref, acc_ref):
        i, j = _tri(pl.program_id(1), counts)

        @pl.when(j == 0)
        def _():
            m_ref[...] = jnp.full_like(m_ref, NEG)
            l_ref[...] = jnp.zeros_like(l_ref)
            acc_ref[...] = jnp.zeros_like(acc_ref)

        def step(masked):
            s = lax.dot_general(q_ref[...], k_ref[...], (NT, ((), ())), preferred_element_type=F32)
            if masked:
                qpos = i * tq + lax.broadcasted_iota(jnp.int32, s.shape, 0)
                kpos = j * tk + lax.broadcasted_iota(jnp.int32, s.shape, 1)
                s = jnp.where(kpos <= qpos, s, NEG)
            m_prev = m_ref[...]
            m_new = jnp.maximum(m_prev, jnp.max(s, axis=1, keepdims=True))
            alpha = jnp.exp(m_prev - m_new)
            p = jnp.exp(s - m_new)
            l_ref[...] = alpha * l_ref[...] + jnp.sum(p, axis=1, keepdims=True)
            acc_ref[...] = alpha * acc_ref[...] + lax.dot_general(p.astype(BF16), v_ref[...], (NN, ((), ())),
                                                                  preferred_element_type=F32)
            m_ref[...] = m_new

        crosses = j * tk + tk - 1 > i * tq

        @pl.when(crosses)
        def _():
            step(True)

        @pl.when(jnp.logical_not(crosses))
        def _():
            step(False)

        @pl.when(j == last_kv(i))
        def _():
            o_ref[...] = acc_ref[...] / l_ref[...]
            lse_ref[...] = jnp.broadcast_to(m_ref[...] + jnp.log(l_ref[...]), lse_ref.shape)

    qblk = pl.BlockSpec((tq, QHEAD), lambda h, s: (_tri(s, counts)[0], h))
    oblk = pl.BlockSpec((tq, HEAD), lambda h, s: (_tri(s, counts)[0], h))
    return _pcall(body, name=name, grid=(MLA_HEADS, sum(counts)),
                  in_specs=[qblk, pl.BlockSpec((tk, QHEAD), lambda h, s: (_tri(s, counts)[1], h)),
                            pl.BlockSpec((tk, HEAD), lambda h, s: (_tri(s, counts)[1], h))],
                  out_specs=[oblk, oblk],
                  out_shape=[jax.ShapeDtypeStruct((t, MLA_HEADS * HEAD), F32), jax.ShapeDtypeStruct((t, MLA_HEADS * HEAD), F32)],
                  scratch=[pltpu.VMEM((tq, 1), F32), pltpu.VMEM((tq, 1), F32), pltpu.VMEM((tq, HEAD), F32)],
                  carry=carry)(q, k, v)


def _attn_bwd(q, k, v, do, lse, delta, *, name, tq=1024, tk=1024, carry=None):
    t = q.shape[0]
    tq, tk = _pick(t, tq), _pick(t, tk)
    nq, nk = t // tq, t // tk
    first_q = lambda j: (j * tk) // tq
    counts = [nq - first_q(j) for j in range(nk)]

    def where(step):
        j, off = _tri(step, counts)
        return j, first_q(j) + off

    lanes = lambda col: jnp.tile(col, (1, tk // HEAD))

    def body(q_ref, k_ref, v_ref, do_ref, lse_ref, dl_ref, dq_ref, dkv_ref, dkr_ref, dk_acc, dv_acc):
        j, i = where(pl.program_id(1))

        @pl.when(i == first_q(j))
        def _():
            dk_acc[...] = jnp.zeros_like(dk_acc)
            dv_acc[...] = jnp.zeros_like(dv_acc)

        def step(masked):
            qv, kv_, dov = q_ref[...], k_ref[...], do_ref[...].astype(BF16)
            s = lax.dot_general(qv, kv_, (NT, ((), ())), preferred_element_type=F32)
            p = jnp.exp((s - lanes(lse_ref[...])).astype(BF16))
            if masked:
                qpos = i * tq + lax.broadcasted_iota(jnp.int32, s.shape, 0)
                kpos = j * tk + lax.broadcasted_iota(jnp.int32, s.shape, 1)
                p = jnp.where(kpos <= qpos, p, jnp.zeros_like(p))
            dv_acc[...] += lax.dot_general(p, dov, (TN, ((), ())), preferred_element_type=F32)
            dp = lax.dot_general(dov, v_ref[...], (NT, ((), ())), preferred_element_type=F32)
            ds = p * (dp - lanes(dl_ref[...])).astype(BF16)
            dk_acc[...] += lax.dot_general(ds, qv, (TN, ((), ())), preferred_element_type=F32)
            contrib = lax.dot_general(ds, kv_, (NN, ((), ())), preferred_element_type=F32)
            rows = pl.ds(pl.multiple_of(i * tq, tq), tq)

            @pl.when(j == 0)
            def _():
                dq_ref[rows, :] = contrib

            @pl.when(j > 0)
            def _():
                dq_ref[rows, :] += contrib

        crosses = j * tk + tk - 1 > i * tq

        @pl.when(crosses)
        def _():
            step(True)

        @pl.when(jnp.logical_not(crosses))
        def _():
            step(False)

        @pl.when(i == nq - 1)
        def _():
            dk = dk_acc[...]
            dkv_ref[...] = jnp.concatenate([dk[:, :HEAD], dv_acc[...]], axis=1).astype(dkv_ref.dtype)
            dkr_ref[...] = dk[:, HEAD:]

    qi = lambda h, s: (where(s)[1], h)
    kj = lambda h, s: (where(s)[0], h)
    return _pcall(body, name=name, grid=(MLA_HEADS, sum(counts)),
                  in_specs=[pl.BlockSpec((tq, QHEAD), qi), pl.BlockSpec((tk, QHEAD), kj), pl.BlockSpec((tk, HEAD), kj),
                            pl.BlockSpec((tq, HEAD), qi), pl.BlockSpec((tq, HEAD), qi), pl.BlockSpec((tq, HEAD), qi)],
                  out_specs=[pl.BlockSpec((t, QHEAD), lambda h, s: (0, h)), pl.BlockSpec((tk, QHEAD), kj),
                             pl.BlockSpec((tk, HEAD), kj)],
                  out_shape=[jax.ShapeDtypeStruct((t, MLA_HEADS * QHEAD), F32), jax.ShapeDtypeStruct((t, MLA_HEADS * QHEAD), BF16),
                             jax.ShapeDtypeStruct((t, MLA_HEADS * HEAD), F32)],
                  scratch=[pltpu.VMEM((tk, QHEAD), F32), pltpu.VMEM((tk, HEAD), F32)], carry=carry)(q, k, v, do, lse, delta)


def _ffn_up(h, wgate, wup, *, name, bm=512, bn=FF_WIDE):
    t = h.shape[0]
    bm = _pick(t, bm)

    def body(h_ref, wg_ref, wu_ref, g_ref, u_ref, a_ref):
        hv = h_ref[...]
        g = lax.dot_general(hv, wg_ref[...], (NN, ((), ())), preferred_element_type=F32)
        u = lax.dot_general(hv, wu_ref[...], (NN, ((), ())), preferred_element_type=F32)
        g_ref[...] = g.astype(g_ref.dtype)
        u_ref[...] = u.astype(u_ref.dtype)
        a_ref[...] = (_silu(g) * u).astype(a_ref.dtype)

    w_spec = pl.BlockSpec((D_MODEL, bn), lambda j, i: (0, j))
    o_spec = pl.BlockSpec((bm, bn), lambda j, i: (i, j))
    sds = jax.ShapeDtypeStruct((t, D_FF), BF16)
    return _pcall(body, name=name, grid=(D_FF // bn, t // bm),
                  in_specs=[pl.BlockSpec((bm, D_MODEL), lambda j, i: (i, 0)), w_spec, w_spec],
                  out_specs=[o_spec] * 3, out_shape=[sds] * 3)(h, wgate, wup)


def _ffn_down_dx(dy, wdown, gate, up, *, name, bm=512, bn=FF_WIDE):
    t = dy.shape[0]
    bm = _pick(t, bm)

    def body(dy_ref, w_ref, g_ref, u_ref, dg_ref, du_ref):
        d = lax.dot_general(dy_ref[...].astype(BF16), w_ref[...], (NT, ((), ())), preferred_element_type=F32)
        g = g_ref[...].astype(F32)
        dg_ref[...] = (d * u_ref[...].astype(F32) * _dsilu(g)).astype(dg_ref.dtype)
        du_ref[...] = (d * _silu(g)).astype(du_ref.dtype)

    o_spec = pl.BlockSpec((bm, bn), lambda j, i: (i, j))
    sds = jax.ShapeDtypeStruct((t, D_FF), BF16)
    return _pcall(body, name=name, grid=(D_FF // bn, t // bm),
                  in_specs=[pl.BlockSpec((bm, D_MODEL), lambda j, i: (i, 0)), pl.BlockSpec((bn, D_MODEL), lambda j, i: (j, 0)),
                            o_spec, o_spec],
                  out_specs=[o_spec, o_spec], out_shape=[sds, sds])(dy, wdown, gate, up)


def _loss_bwd(x2, w, target, *, name):
    t = x2.shape[0]
    tm = _rows(t, D_MODEL)

    def body(x_ref, w_ref, t_ref, dx_ref, dw_ref, l_ref):
        xv, wv = x_ref[...], w_ref[...]
        r = lax.rsqrt(jnp.mean(xv * xv, axis=-1, keepdims=True) + EPS)
        xh = xv * r
        err = xh * wv - t_ref[...]
        dy = err * (1.0 / D_MODEL)
        dyw = dy * wv
        dx_ref[...] = r * (dyw - xh * jnp.mean(dyw * xh, axis=-1, keepdims=True))

        @pl.when(pl.program_id(0) == 0)
        def _():
            dw_ref[...] = jnp.zeros_like(dw_ref)
            l_ref[...] = jnp.zeros_like(l_ref)

        dw_ref[...] += (dy * xh).reshape(tm // 8, 8, D_MODEL).sum(axis=0)
        sq = (err * err).reshape(tm // 8, 8, D_MODEL).sum(axis=0)
        part = sq[:, :HEAD]
        for c in range(1, D_MODEL // HEAD):
            part = part + sq[:, c * HEAD:(c + 1) * HEAD]
        l_ref[...] += part * (0.5 / D_MODEL)

    row = pl.BlockSpec((tm, D_MODEL), lambda i: (i, 0))
    return _pcall(body, name=name, grid=(t // tm,),
                  in_specs=[row, pl.BlockSpec((1, D_MODEL), lambda i: (0, 0)), row],
                  out_specs=[row, pl.BlockSpec((8, D_MODEL), lambda i: (0, 0)), pl.BlockSpec((8, HEAD), lambda i: (0, 0))],
                  out_shape=[jax.ShapeDtypeStruct((t, D_MODEL), F32), jax.ShapeDtypeStruct((8, D_MODEL), F32),
                             jax.ShapeDtypeStruct((8, HEAD), F32)])(x2, w, target)


def _unshard_cols(g):
    return jnp.transpose(g, (1, 0, 2)).reshape(g.shape[1], N_DEV * g.shape[2])


def _shard_cols(w):
    return jnp.transpose(w.reshape(w.shape[0], N_DEV, w.shape[1] // N_DEV), (1, 0, 2))


_WIN_ORDER = ((0, 4096), (4112, 5136), (5136, 5200), (4096, 4112))
_WIN_SHARD = IN_WIDTH // N_DEV


def _win_pieces():
    out, pos = [], 0
    for a, b in _WIN_ORDER:
        c = a
        while c < b:
            dev, off = divmod(c, _WIN_SHARD)
            width = min(b, (dev + 1) * _WIN_SHARD) - c
            out.append((dev, off, width, pos))
            c, pos = c + width, pos + width
    return out


def _win_gathered_to_padded(g):
    pieces = [g[dev][:, off:off + width] for dev, off, width, _ in _win_pieces()]
    return jnp.concatenate(pieces + [jnp.zeros((g.shape[1], PROJ_W - IN_WIDTH), g.dtype)], axis=1)


def _win_padded_to_shards(d):
    shards = []
    for dev in range(N_DEV):
        mine = sorted((off, width, pos) for dv, off, width, pos in _win_pieces() if dv == dev)
        shards.append(jnp.concatenate([d[:, pos:pos + width] for _, width, pos in mine], axis=1))
    return jnp.stack(shards)


def _wuq_to_padded(w):
    w3 = w.reshape(w.shape[0], MLA_HEADS, HEAD + QK_ROPE)
    return jnp.pad(w3, ((0, 0), (0, 0), (0, QHEAD - HEAD - QK_ROPE))).reshape(w.shape[0], MLA_HEADS * QHEAD)


def _wuq_from_padded(d):
    return d.reshape(d.shape[0], MLA_HEADS, QHEAD)[:, :, :HEAD + QK_ROPE].reshape(d.shape[0], MLA_HEADS * (HEAD + QK_ROPE))


def _late_weights(g_out, g_gate, g_up, g_down):
    return g_out.reshape(D_MODEL, D_MODEL), _unshard_cols(g_gate), _unshard_cols(g_up), g_down.reshape(D_FF, D_MODEL)


def _local_step(x, pos, target, win_p, wuq_p, wukv, late, conv_w, small, exchange):
    cs = _rope_cs(pos, _rope_tables(), name="rope_cs")
    if not exchange:
        wout, wgate, wup, wdown = late
    h1 = _rms_fwd(x, small["attn_norm_w"], name="rms1_fwd", width=D_MODEL)
    proj = _mm(h1, win_p, name="mm_in", bn=PROJ_BLK)
    qkv = _conv_fwd(proj, conv_w, name="conv_fwd")
    o_gdn_raw, hist = _gdn_fwd(qkv, proj, small["gdn_params"], name="gdn_fwd")
    o_gdn = _gate_fwd(o_gdn_raw, proj, small["gdn_norm_w"], name="gate_fwd")
    cqn = _rms_fwd(proj, small["q_norm_w"], name="rmsq_fwd", width=Q_LORA, col0=COL_CQ)
    ckvn = _rms_fwd(proj, small["kv_norm_w"], name="rmskv_fwd", width=KV_LORA, col0=COL_CKV)
    q_full = _q_up(cqn, wuq_p, cs, name="q_up")
    k_full, v_b = _kv_up(ckvn, wukv, proj, cs, name="kv_up")
    if exchange:
        (o_mla_raw, lse), gathered = _attn_fwd(q_full, k_full, v_b, name="attn_fwd", carry=_Gather(late))
        wout, wgate, wup, wdown = _late_weights(*gathered)
    else:
        o_mla_raw, lse = _attn_fwd(q_full, k_full, v_b, name="attn_fwd")
    o_mla = _rms_fwd(o_mla_raw, small["mla_out_norm_w"], name="rmso_fwd", width=HEAD, heads=MLA_HEADS)
    mixed = jnp.concatenate([o_gdn, o_mla], axis=1)
    x1 = _mm(mixed, wout, name="mm_out", res=x)
    h2 = _rms_fwd(x1, small["ffn_norm_w"], name="rms2_fwd", width=D_MODEL)
    gate, up, act = _ffn_up(h2, wgate, wup, name="ffn_up")
    x2 = _mm(act, wdown, name="mm_down", res=x1, bk=FF_WIDE)
    dx2, dw_final, loss_part = _loss_bwd(x2, small["final_norm_w"], target, name="loss_bwd")
    dgate, dup = _ffn_down_dx(dx2, wdown, gate, up, name="ffn_down_dx")
    d_wdown = _mm(act, dx2, name="mm_down_dw", ta=True, out_dtype=BF16, bm=FF_WIDE)
    dh2 = _mm(dgate, wgate, name="mm_gateup_dx", tb=True, bk=FF_WIDE, pair=(dup, wup))
    d_wgate, d_wup = _mm_dw2(h2, dgate, dup, name="mm_gateup_dw")
    dx1, dw_ffn = _rms_bwd(x1, small["ffn_norm_w"], dh2, name="rms2_bwd", width=D_MODEL, res=dx2)
    dmixed = _mm(dx1, wout, name="mm_out_dx", tb=True)
    d_wout = _mm(mixed, dx1, name="mm_out_dw", ta=True, out_dtype=BF16)
    do_mla, dw_mla_out, delta = _rms_bwd(o_mla_raw, small["mla_out_norm_w"], dmixed, name="rmso_bwd", width=HEAD,
                                         heads=MLA_HEADS, dcol0=GDN_QK, with_delta=True, out_dtype=BF16)
    if exchange:
        send = [d_wdown.reshape(N_DEV, D_FF // N_DEV, D_MODEL), _shard_cols(d_wgate), _shard_cols(d_wup)]
        (dq_full, dkv, dkr_h), (r_down, r_gate, r_up) = _attn_bwd(q_full, k_full, v_b, do_mla, lse, delta, name="attn_bwd",
                                                                  carry=_Exchange(send, [False] * 3))
    else:
        dq_full, dkv, dkr_h = _attn_bwd(q_full, k_full, v_b, do_mla, lse, delta, name="attn_bwd")
    dq_pre = _q_rot(dq_full, cs, name="q_rot_bwd", sign=-1.0)
    dmisc_kr = _krope_bwd(dkr_h, cs, name="krope_bwd")
    dcqn = _mm(dq_pre, wuq_p, name="mm_uq_dx", tb=True)
    d_wuq = _mm(cqn, dq_pre, name="mm_uq_dw", ta=True, out_dtype=BF16)
    dckvn = _mm(dkv, wukv, name="mm_ukv_dx", tb=True)
    d_wukv = _mm(ckvn, dkv, name="mm_ukv_dw", ta=True, out_dtype=BF16)
    dcq, dw_qn = _rms_bwd(proj, small["q_norm_w"], dcqn, name="rmsq_bwd", width=Q_LORA, col0=COL_CQ, out_dtype=BF16)
    dckv, dw_kvn = _rms_bwd(proj, small["kv_norm_w"], dckvn, name="rmskv_bwd", width=KV_LORA, col0=COL_CKV, out_dtype=BF16)
    do_gdn, dz, dw_gdn = _gate_bwd(o_gdn_raw, proj, small["gdn_norm_w"], dmixed, name="gate_bwd")
    if exchange:
        send = [d_wout.reshape(N_DEV, D_MODEL // N_DEV, D_MODEL), _shard_cols(_wuq_from_padded(d_wuq)), _shard_cols(d_wukv)]
        (dqkv, dmisc, d_params), (r_out, r_uq, r_ukv) = _gdn_bwd(
            qkv, proj, small["gdn_params"], hist, do_gdn, dmisc_kr, name="gdn_bwd", carry=_Exchange(send, [False] * 3))
    else:
        dqkv, dmisc, d_params = _gdn_bwd(qkv, proj, small["gdn_params"], hist, do_gdn, dmisc_kr, name="gdn_bwd")
    dqkv_pre, dconv = _conv_bwd(proj, conv_w, dqkv, name="conv_bwd")
    dproj = jnp.concatenate([dqkv_pre, dz, dcq, dckv, dmisc.astype(BF16), jnp.zeros((x.shape[0], PROJ_W - COL_MISC - HEAD), BF16)], axis=1)
    d_win = _mm(h1, dproj, name="mm_in_dw", ta=True, out_dtype=BF16, bn=PROJ_BLK)
    if exchange:
        dh1, (r_in,) = _mm(dproj, win_p, name="mm_in_dx", tb=True, bk=PROJ_BLK,
                           carry=_Exchange([_win_padded_to_shards(d_win)], [False]))
        d_win = r_in
    else:
        dh1 = _mm(dproj, win_p, name="mm_in_dx", tb=True, bk=PROJ_BLK)
    dx, dw_attn = _rms_bwd(x, small["attn_norm_w"], dh1, name="rms1_bwd", width=D_MODEL, res=dx1)

    if exchange:
        big = {"w_in": d_win, "w_uq": r_uq, "w_ukv": r_ukv, "w_out": r_out, "w_gate": r_gate, "w_up": r_up, "w_down": r_down}
    else:
        big = {"w_in": d_win, "w_uq": d_wuq, "w_ukv": d_wukv, "w_out": d_wout, "w_gate": d_wgate, "w_up": d_wup,
               "w_down": d_wdown}
    sm = {"attn_norm_w": dw_attn, "ffn_norm_w": dw_ffn, "final_norm_w": dw_final, "q_norm_w": dw_qn, "kv_norm_w": dw_kvn,
          "gdn_norm_w": dw_gdn, "mla_out_norm_w": dw_mla_out, "gdn_params": d_params, "conv_w": dconv, "loss": loss_part}
    return dx, big, sm


def _exchange(ex, *, name):
    def body(*refs):
        ins, outs, sems = refs[:ex.n], refs[ex.n:2 * ex.n], refs[2 * ex.n:]
        ex.start(ins, outs, sems)
        ex.forward(ins, outs, sems)
        ex.finish(ins, outs, sems)

    any_spec = pl.BlockSpec(memory_space=pl.ANY)
    return pl.pallas_call(body, name=name, in_specs=[any_spec] * ex.n, out_specs=[any_spec] * ex.n,
                          out_shape=ex.out_shape(), scratch_shapes=ex.sems())(*ex.arrays)


def _adamw_math(g, w, m, v):
    m = ADAM_B1 * m + (1.0 - ADAM_B1) * g
    v = ADAM_B2 * v + (1.0 - ADAM_B2) * (g * g)
    m_hat = m / (1.0 - ADAM_B1 ** ADAM_STEP)
    v_hat = v / (1.0 - ADAM_B2 ** ADAM_STEP)
    delta = -ADAM_LR * (m_hat / (jnp.sqrt(v_hat) + ADAM_EPS) + ADAM_WD * w)
    return delta, m, v


def _adamw(parts, w, m, v, *, name):
    npart, r, c = parts.shape
    tr = r if r * c * 4 <= (1 << 20) else _rows(r, c, 1 << 20)

    def body(p_ref, w_ref, m_ref, v_ref, g_ref, d_ref, nm_ref, nv_ref):
        g = p_ref[0].astype(F32)
        for s in range(1, npart):
            g = g + p_ref[s].astype(F32)
        g_ref[...] = g
        d_ref[...], nm_ref[...], nv_ref[...] = _adamw_math(g, w_ref[...], m_ref[...], v_ref[...])

    blk = pl.BlockSpec((tr, c), lambda i: (i, 0))
    sds = jax.ShapeDtypeStruct((r, c), F32)
    return _pcall(body, name=name, grid=(r // tr,),
                  in_specs=[pl.BlockSpec((npart, tr, c), lambda i: (0, i, 0)), blk, blk, blk],
                  out_specs=[blk] * 4, out_shape=[sds] * 4)(parts, w, m, v)


def _sum_parts(parts, *, name):
    npart, r, c = parts.shape

    def body(p_ref, o_ref):
        g = p_ref[0]
        for s in range(1, npart):
            g = g + p_ref[s]
        o_ref[...] = g

    return _pcall(body, name=name, grid=(1,), in_specs=[pl.BlockSpec((npart, r, c), lambda i: (0, 0, 0))],
                  out_specs=pl.BlockSpec((r, c), lambda i: (0, 0)), out_shape=jax.ShapeDtypeStruct((r, c), F32))(parts)


_SMALL = (("attn_norm_w", D_MODEL), ("ffn_norm_w", D_MODEL), ("final_norm_w", D_MODEL), ("q_norm_w", Q_LORA),
          ("kv_norm_w", KV_LORA), ("gdn_norm_w", HEAD), ("mla_out_norm_w", HEAD), ("a_log", HEAD), ("dt_bias", HEAD))
_SMALL_ROWS = sum(n for _, n in _SMALL) // HEAD
_CONV_ROWS = GDN_CONV * CONV_CH // HEAD
_PACK_ROWS = 160


def _pad_lanes(v, n):
    v = v.reshape(-1)
    return jnp.concatenate([v, jnp.zeros((n - v.shape[0],), v.dtype)])


def kernel(x, positions, attn_norm_w, w_in, conv_w, a_log, dt_bias, gdn_norm_w, q_norm_w, w_uq, kv_norm_w, w_ukv, mla_out_norm_w, w_out, ffn_norm_w, w_gate, w_up, w_down, final_norm_w, loss_target, m_attn_norm_w, m_w_in, m_conv_w, m_a_log, m_dt_bias, m_gdn_norm_w, m_q_norm_w, m_w_uq, m_kv_norm_w, m_w_ukv, m_mla_out_norm_w, m_w_out, m_ffn_norm_w, m_w_gate, m_w_up, m_w_down, m_final_norm_w, v_attn_norm_w, v_w_in, v_conv_w, v_a_log, v_dt_bias, v_gdn_norm_w, v_q_norm_w, v_w_uq, v_kv_norm_w, v_w_ukv, v_mla_out_norm_w, v_w_out, v_ffn_norm_w, v_w_gate, v_w_up, v_w_down, v_final_norm_w):
    t = x.shape[1]
    me = 4 * lax.axis_index("x") + 2 * lax.axis_index("y") + lax.axis_index("c")
    weights = dict(attn_norm_w=attn_norm_w, w_in=w_in, conv_w=conv_w, a_log=a_log, dt_bias=dt_bias, gdn_norm_w=gdn_norm_w,
                   q_norm_w=q_norm_w, w_uq=w_uq, kv_norm_w=kv_norm_w, w_ukv=w_ukv, mla_out_norm_w=mla_out_norm_w, w_out=w_out,
                   ffn_norm_w=ffn_norm_w, w_gate=w_gate, w_up=w_up, w_down=w_down, final_norm_w=final_norm_w)
    mom_m = dict(attn_norm_w=m_attn_norm_w, w_in=m_w_in, conv_w=m_conv_w, a_log=m_a_log, dt_bias=m_dt_bias, gdn_norm_w=m_gdn_norm_w,
                 q_norm_w=m_q_norm_w, w_uq=m_w_uq, kv_norm_w=m_kv_norm_w, w_ukv=m_w_ukv, mla_out_norm_w=m_mla_out_norm_w,
                 w_out=m_w_out, ffn_norm_w=m_ffn_norm_w, w_gate=m_w_gate, w_up=m_w_up, w_down=m_w_down, final_norm_w=m_final_norm_w)
    mom_v = dict(attn_norm_w=v_attn_norm_w, w_in=v_w_in, conv_w=v_conv_w, a_log=v_a_log, dt_bias=v_dt_bias, gdn_norm_w=v_gdn_norm_w,
                 q_norm_w=v_q_norm_w, w_uq=v_w_uq, kv_norm_w=v_kv_norm_w, w_ukv=v_w_ukv, mla_out_norm_w=v_mla_out_norm_w,
                 w_out=v_w_out, ffn_norm_w=v_ffn_norm_w, w_gate=v_w_gate, w_up=v_w_up, w_down=v_w_down, final_norm_w=v_final_norm_w)
    big_names = ("w_in", "w_uq", "w_ukv", "w_out", "w_gate", "w_up", "w_down")

    shard = {n: weights[n][0].astype(BF16) for n in big_names}
    g_in, g_uq, g_ukv, g_conv = _exchange(_Gather([shard["w_in"], shard["w_uq"], shard["w_ukv"], weights["conv_w"][0]]),
                                          name="gather_weights")
    win_p = _win_gathered_to_padded(g_in)
    wuq_p = _wuq_to_padded(_unshard_cols(g_uq))
    wukv = _unshard_cols(g_ukv)
    late = [shard["w_out"], shard["w_gate"], shard["w_up"], shard["w_down"]]
    conv_full = jnp.concatenate([_unshard_cols(g_conv), jnp.zeros((8 - GDN_CONV, CONV_CH), F32)], axis=0)

    gdn_params = jnp.concatenate([_pad_lanes(a_log, HEAD)[None], _pad_lanes(dt_bias, HEAD)[None], jnp.zeros((6, HEAD), F32)], axis=0)
    small = {n: weights[n].reshape(1, -1) for n in ("attn_norm_w", "ffn_norm_w", "final_norm_w", "q_norm_w", "kv_norm_w",
                                                    "gdn_norm_w", "mla_out_norm_w")}
    small["gdn_params"] = gdn_params

    dx, big, sm = _local_step(x[0], positions.reshape(t, 1).astype(F32), loss_target[0], win_p, wuq_p, wukv, late,
                              conv_full, small, True)

    rows8 = lambda name: jnp.sum(sm[name], axis=0)
    pieces = [rows8(n) for n, _ in _SMALL[:7]]
    pieces += [_pad_lanes(jnp.sum(sm["gdn_params"][0:1], axis=0), HEAD), _pad_lanes(jnp.sum(sm["gdn_params"][1:2], axis=0), HEAD)]
    pieces.append(jnp.sum(sm["conv_w"], axis=1).reshape(-1))
    pieces.append(_pad_lanes(jnp.sum(sm["loss"]).reshape(1), HEAD))
    packed = _pad_lanes(jnp.concatenate(pieces), _PACK_ROWS * HEAD).reshape(_PACK_ROWS, HEAD)
    (r_small,) = _exchange(_Exchange([packed], [True]), name="exchange_small")

    outs_g, outs_d, outs_m, outs_v = {}, {}, {}, {}
    for name in big_names:
        g, d, nm, nv = _adamw(big[name], weights[name][0], mom_m[name][0], mom_v[name][0], name="adamw_" + name)
        outs_g[name], outs_d[name], outs_m[name], outs_v[name] = g[None], d[None], nm[None], nv[None]

    total = _sum_parts(r_small, name="sum_small")
    flat = total.reshape(-1)
    loss = flat[(_SMALL_ROWS + _CONV_ROWS) * HEAD]
    g_small, off = {}, 0
    for n, size in _SMALL:
        g_small[n] = flat[off:off + size]
        off += size
    g_conv_full = flat[off:off + GDN_CONV * CONV_CH].reshape(GDN_CONV, CONV_CH)
    g_small["conv_w"] = lax.dynamic_slice(g_conv_full, (0, me * (CONV_CH // N_DEV)), (GDN_CONV, CONV_CH // N_DEV)).reshape(-1)
    order = [n for n, _ in _SMALL] + ["conv_w"]
    sizes = dict(_SMALL)
    sizes["conv_w"] = GDN_CONV * CONV_CH // N_DEV
    true_size = {n: weights[n].size for n in order}

    def pack(d):
        return jnp.concatenate([_pad_lanes(d[n], sizes[n]) for n in order]).reshape(1, -1, HEAD)

    g2, d2, m2, v2 = _adamw(pack(g_small), pack(weights)[0], pack(mom_m)[0], pack(mom_v)[0], name="adamw_small")
    off = 0
    for n in order:
        for src, dst in ((g2, outs_g), (d2, outs_d), (m2, outs_m), (v2, outs_v)):
            dst[n] = src.reshape(-1)[off:off + true_size[n]].reshape(weights[n].shape)
        off += sizes[n]

    names = ("attn_norm_w", "w_in", "conv_w", "a_log", "dt_bias", "gdn_norm_w", "q_norm_w", "w_uq", "kv_norm_w", "w_ukv",
             "mla_out_norm_w", "w_out", "ffn_norm_w", "w_gate", "w_up", "w_down", "final_norm_w")
    return (loss, dx[None], *[outs_g[n] for n in names], *[outs_d[n] for n in names], *[outs_m[n] for n in names],
            *[outs_v[n] for n in names])
```

```python
import functools
import math

import jax
import jax.numpy as jnp
from jax import lax
from jax.experimental import pallas as pl
from jax.experimental.pallas import tpu as pltpu

F32 = jnp.float32
BF16 = jnp.bfloat16

D_MODEL = 2048
GDN_HEADS = 8
HEAD = 128
GDN_CONV = 4
GDN_CHUNK = 64
GDN_QK = GDN_HEADS * HEAD
CONV_CH = 3 * GDN_QK
MLA_HEADS = 8
QK_ROPE = 64
Q_LORA = 512
KV_LORA = 512
ROPE_THETA = 10000.0
D_FF = 5632
EPS = 1e-6
IN_WIDTH = 5200
ADAM_LR, ADAM_B1, ADAM_B2, ADAM_EPS, ADAM_WD, ADAM_STEP = 0.001, 0.9, 0.999, 1e-08, 0.01, 10

PROJ_W = 5376
PROJ_BLK = PROJ_W // 3
COL_Z = 3072
COL_CQ = 4096
COL_CKV = 4608
COL_MISC = 5120
LANE_B = 64
LANE_A = 72
QHEAD = 256
FF_WIDE = D_FF // 4
N_DEV = 8
MESH = pl.DeviceIdType.MESH
VMEM_LIMIT_MB = 48

NN = ((1,), (0,))
NT = ((1,), (1,))
TN = ((0,), (0,))


def _my_place():
    x, y, c = lax.axis_index("x"), lax.axis_index("y"), lax.axis_index("c")
    return x, y, c, 4 * x + 2 * y + c


def _peer(x, y, c, p):
    px, py, pc = x ^ ((p >> 2) & 1), y ^ ((p >> 1) & 1), c ^ (p & 1)
    return (px, py, pc), 4 * px + 2 * py + pc


class _Exchange:
    def __init__(self, arrays, gather):
        self.arrays, self.gather, self.n = list(arrays), list(gather), len(arrays)

    def out_shape(self):
        return [jax.ShapeDtypeStruct(((N_DEV,) + a.shape) if g else a.shape, a.dtype)
                for a, g in zip(self.arrays, self.gather)]

    def sems(self):
        return [pltpu.SemaphoreType.DMA((self.n * (N_DEV - 1),)), pltpu.SemaphoreType.DMA((self.n * (N_DEV - 1),)),
                pltpu.SemaphoreType.DMA((self.n,))]

    def _copies(self, ins, outs, sems):
        send_sems, recv_sems, local_sems = sems
        x, y, c, me = _my_place()
        local = [pltpu.make_async_copy(ins[k] if self.gather[k] else ins[k].at[me], outs[k].at[me], local_sems.at[k])
                 for k in range(self.n)]
        sent, received = [], []
        for p in range(1, N_DEV):
            place, num = _peer(x, y, c, p)
            for k in range(self.n):
                src = ins[k] if self.gather[k] else ins[k].at[num]
                idx = k * (N_DEV - 1) + p - 1
                mk = lambda dst: pltpu.make_async_remote_copy(src_ref=src, dst_ref=dst, send_sem=send_sems.at[idx],
                                                              recv_sem=recv_sems.at[idx], device_id=place, device_id_type=MESH)
                sent.append(mk(outs[k].at[me]))
                received.append(mk(outs[k].at[num]))
        return local, sent, received

    def start(self, ins, outs, sems):
        local, sent, _ = self._copies(ins, outs, sems)
        for cp in local + sent:
            cp.start()

    def forward(self, ins, outs, sems):
        pass

    def finish(self, ins, outs, sems):
        local, sent, received = self._copies(ins, outs, sems)
        for cp in received:
            cp.wait_recv()
        for cp in sent:
            cp.wait_send()
        for cp in local:
            cp.wait()


class _Gather:
    def __init__(self, arrays):
        self.arrays, self.n = list(arrays), len(arrays)

    def out_shape(self):
        return [jax.ShapeDtypeStruct((N_DEV,) + a.shape, a.dtype) for a in self.arrays]

    def sems(self):
        return [pltpu.SemaphoreType.DMA((self.n * (N_DEV - 1),)), pltpu.SemaphoreType.DMA((self.n * (N_DEV - 1),)),
                pltpu.SemaphoreType.DMA((self.n,))]

    def _plan(self, ins, outs, sems):
        send_sems, recv_sems, local_sems = sems
        x, y, c, me = _my_place()
        sibling = (x, y, 1 - c)
        chips = [(1 - x, y), (x, 1 - y), (1 - x, 1 - y)]
        num = lambda px, py, pc: 4 * px + 2 * py + pc

        def copy(k, i, block, to, src=None):
            slot = outs[k].at[num(*block)]
            return pltpu.make_async_remote_copy(src_ref=slot if src is None else src, dst_ref=slot,
                                                send_sem=send_sems.at[k * (N_DEV - 1) + i],
                                                recv_sem=recv_sems.at[k * (N_DEV - 1) + i],
                                                device_id=to, device_id_type=MESH)

        local = [pltpu.make_async_copy(ins[k], outs[k].at[me], local_sems.at[k]) for k in range(self.n)]
        return (x, y, c), sibling, chips, copy, local

    def start(self, ins, outs, sems):
        me, sibling, chips, copy, local = self._plan(ins, outs, sems)
        for cp in local:
            cp.start()
        for k in range(self.n):
            copy(k, 0, me, sibling, src=ins[k]).start()
            for j, chip in enumerate(chips):
                copy(k, 1 + j, me, (*chip, me[2]), src=ins[k]).start()

    def forward(self, ins, outs, sems):
        me, sibling, chips, copy, _ = self._plan(ins, outs, sems)
        for j, chip in enumerate(chips):
            for k in range(self.n):
                copy(k, 1 + j, (*chip, me[2]), me).wait_recv()
                copy(k, 4 + j, (*chip, me[2]), sibling).start()

    def finish(self, ins, outs, sems):
        me, sibling, chips, copy, local = self._plan(ins, outs, sems)
        for k in range(self.n):
            copy(k, 0, sibling, me).wait_recv()
            for j, chip in enumerate(chips):
                copy(k, 4 + j, (*chip, 1 - me[2]), me).wait_recv()
        for k in range(self.n):
            copy(k, 0, me, sibling, src=ins[k]).wait_send()
            for j, chip in enumerate(chips):
                copy(k, 1 + j, me, (*chip, me[2]), src=ins[k]).wait_send()
                copy(k, 4 + j, (*chip, me[2]), sibling).wait_send()
        for cp in local:
            cp.wait()


def _pcall(body, *, name, grid, in_specs, out_specs, out_shape, scratch=(), carry=None):
    params = pltpu.CompilerParams(dimension_semantics=("arbitrary",) * len(grid), vmem_limit_bytes=VMEM_LIMIT_MB << 20)
    if carry is None:
        return pl.pallas_call(body, name=name, grid=grid, in_specs=in_specs, out_specs=out_specs, out_shape=out_shape,
                              scratch_shapes=list(scratch), compiler_params=params)
    single = not isinstance(out_specs, (list, tuple))
    out_specs = [out_specs] if single else list(out_specs)
    out_shape = [out_shape] if single else list(out_shape)
    n_in, n_out, n_scr, na = len(in_specs), len(out_specs), len(scratch), carry.n

    def wrapped(*refs):
        ins, cin = refs[:n_in], refs[n_in:n_in + na]
        outs, cout = refs[n_in + na:n_in + na + n_out], refs[n_in + na + n_out:n_in + 2 * na + n_out]
        scr, sems = refs[n_in + 2 * na + n_out:n_in + 2 * na + n_out + n_scr], refs[n_in + 2 * na + n_out + n_scr:]
        total = math.prod(grid)
        step = functools.reduce(lambda a, d: a * grid[d] + pl.program_id(d), range(len(grid)), 0)

        @pl.when(step == 0)
        def _():
            carry.start(cin, cout, sems)

        body(*ins, *outs, *scr)

        @pl.when(step == min(total * 7 // 8, total - 1))
        def _():
            carry.forward(cin, cout, sems)

        @pl.when(step == total - 1)
        def _():
            carry.finish(cin, cout, sems)

    any_spec = pl.BlockSpec(memory_space=pl.ANY)
    call = pl.pallas_call(wrapped, name=name, grid=grid, in_specs=list(in_specs) + [any_spec] * na,
                          out_specs=out_specs + [any_spec] * na, out_shape=out_shape + carry.out_shape(),
                          scratch_shapes=list(scratch) + carry.sems(), compiler_params=params)

    def run(*args):
        res = call(*args, *carry.arrays)
        main = res[0] if single else list(res[:n_out])
        return main, list(res[n_out:])

    return run


def _pick(dim, pref):
    if dim <= pref:
        return dim
    c = pref
    while c >= 128:
        if dim % c == 0 and c % 128 == 0:
            return c
        c -= 128
    return dim


def _rows(t, width, target_bytes=2 << 20):
    r = max(8, min(t, target_bytes // (4 * width)))
    r = 1 << (r.bit_length() - 1)
    while t % r:
        r //= 2
    return r


MM_FULL_K = 2048


def _mm(a, b, *, name, ta=False, tb=False, res=None, out_dtype=F32, bm=1024, bn=1024, bk=1024, carry=None, pair=None):
    m, k = (a.shape[1], a.shape[0]) if ta else a.shape
    n = b.shape[0] if tb else b.shape[1]
    assert (b.shape[1] if tb else b.shape[0]) == k
    bm, bn, bk = _pick(m, bm), _pick(n, bn), (k if k <= MM_FULL_K else _pick(k, bk))
    nk = k // bk
    dims = (((0,) if ta else (1,), (1,) if tb else (0,)), ((), ()))
    n_ab = 2 if pair is None else 4

    def body(*refs):
        a_ref, b_ref = refs[:2]
        r_ref = refs[n_ab] if res is not None else None
        o_ref = refs[n_ab + 1] if res is not None else refs[n_ab]
        part = lax.dot_general(a_ref[...].astype(BF16), b_ref[...].astype(BF16), dims, preferred_element_type=F32)
        if pair is not None:
            part = part + lax.dot_general(refs[2][...].astype(BF16), refs[3][...].astype(BF16), dims,
                                          preferred_element_type=F32)

        def finish(out):
            if res is not None:
                out = out + r_ref[...]
            o_ref[...] = out.astype(o_ref.dtype)

        if nk == 1:
            finish(part)
            return
        acc_ref = refs[-1]
        kk = pl.program_id(2)

        @pl.when(kk == 0)
        def _():
            acc_ref[...] = part

        @pl.when((kk > 0) & (kk < nk - 1))
        def _():
            acc_ref[...] += part

        @pl.when(kk == nk - 1)
        def _():
            finish(acc_ref[...] + part)

    a_spec = pl.BlockSpec((bk, bm), lambda i, j, kk: (kk, i)) if ta else pl.BlockSpec((bm, bk), lambda i, j, kk: (i, kk))
    b_spec = pl.BlockSpec((bn, bk), lambda i, j, kk: (j, kk)) if tb else pl.BlockSpec((bk, bn), lambda i, j, kk: (kk, j))
    o_spec = pl.BlockSpec((bm, bn), lambda i, j, kk: (i, j))
    ins, specs = [a, b], [a_spec, b_spec]
    if pair is not None:
        assert pair[0].shape == a.shape and pair[1].shape == b.shape
        ins += list(pair)
        specs += [a_spec, b_spec]
    if res is not None:
        ins.append(res)
        specs.append(o_spec)
    return _pcall(body, name=name, grid=(m // bm, n // bn, nk), in_specs=specs, out_specs=o_spec,
                  out_shape=jax.ShapeDtypeStruct((m, n), out_dtype),
                  scratch=[pltpu.VMEM((bm, bn), F32)] if nk > 1 else [], carry=carry)(*ins)


def _mm_dw2(a, b1, b2, *, name, bm=1024, bn=FF_WIDE, bk=1024):
    k, m = a.shape
    n = b1.shape[1]
    assert b1.shape == b2.shape == (k, n)
    bm, bn, bk = _pick(m, bm), _pick(n, bn), _pick(k, bk)
    nk = k // bk

    def body(a_ref, b1_ref, b2_ref, o1_ref, o2_ref, acc1_ref, acc2_ref):
        kk = pl.program_id(2)
        av = a_ref[...].astype(BF16)
        for b_ref, o_ref, acc_ref in ((b1_ref, o1_ref, acc1_ref), (b2_ref, o2_ref, acc2_ref)):
            part = lax.dot_general(av, b_ref[...].astype(BF16), (TN, ((), ())), preferred_element_type=F32)

            @pl.when(kk == 0)
            def _():
                acc_ref[...] = part

            @pl.when((kk > 0) & (kk < nk - 1))
            def _():
                acc_ref[...] += part

            @pl.when(kk == nk - 1)
            def _():
                o_ref[...] = (acc_ref[...] + part).astype(o_ref.dtype)

    b_spec = pl.BlockSpec((bk, bn), lambda i, j, kk: (kk, j))
    o_spec = pl.BlockSpec((bm, bn), lambda i, j, kk: (i, j))
    sds = jax.ShapeDtypeStruct((m, n), BF16)
    return _pcall(body, name=name, grid=(m // bm, n // bn, nk),
                  in_specs=[pl.BlockSpec((bk, bm), lambda i, j, kk: (kk, i)), b_spec, b_spec],
                  out_specs=[o_spec, o_spec], out_shape=[sds, sds],
                  scratch=[pltpu.VMEM((bm, bn), F32), pltpu.VMEM((bm, bn), F32)])(a, b1, b2)


def _rms_fwd(x, w, *, name, width, heads=1, col0=0, out_dtype=BF16):
    t = x.shape[0]
    tm = _rows(t, width)
    cb = col0 // width

    def body(x_ref, w_ref, o_ref):
        xv = x_ref[...]
        r = lax.rsqrt(jnp.mean(xv * xv, axis=-1, keepdims=True) + EPS)
        o_ref[...] = (xv * r * w_ref[...]).astype(o_ref.dtype)

    return _pcall(body, name=name, grid=(t // tm, heads),
                  in_specs=[pl.BlockSpec((tm, width), lambda i, h: (i, cb + h)),
                            pl.BlockSpec((1, width), lambda i, h: (0, 0))],
                  out_specs=pl.BlockSpec((tm, width), lambda i, h: (i, h)),
                  out_shape=jax.ShapeDtypeStruct((t, heads * width), out_dtype))(x, w)


def _rms_bwd(x, w, dy, *, name, width, heads=1, col0=0, dcol0=0, res=None, out_dtype=F32, with_delta=False):
    t = x.shape[0]
    tm = _rows(t, width)
    cb, dcb = col0 // width, dcol0 // width

    def body(*refs):
        refs = list(refs)
        x_ref, w_ref, dy_ref = refs[:3]
        r_ref = refs[3] if res is not None else None
        outs = refs[4:] if res is not None else refs[3:]
        dx_ref, dw_ref = outs[:2]
        xv = x_ref[...]
        dyv = dy_ref[...].astype(F32)
        r = lax.rsqrt(jnp.mean(xv * xv, axis=-1, keepdims=True) + EPS)
        xh = xv * r
        dyw = dyv * w_ref[...]
        dx = r * (dyw - xh * jnp.mean(dyw * xh, axis=-1, keepdims=True))
        if with_delta:
            outs[2][...] = jnp.broadcast_to(jnp.sum(dx * xv, axis=-1, keepdims=True), dx.shape)
        if res is not None:
            dx = dx + r_ref[...]
        dx_ref[...] = dx.astype(dx_ref.dtype)

        @pl.when((pl.program_id(0) == 0) & (pl.program_id(1) == 0))
        def _():
            dw_ref[...] = jnp.zeros_like(dw_ref)

        dw_ref[...] += (dyv * xh).reshape(tm // 8, 8, width).sum(axis=0)

    blk = pl.BlockSpec((tm, width), lambda i, h: (i, h))
    ins = [x, w, dy]
    specs = [pl.BlockSpec((tm, width), lambda i, h: (i, cb + h)), pl.BlockSpec((1, width), lambda i, h: (0, 0)),
             pl.BlockSpec((tm, width), lambda i, h: (i, dcb + h))]
    if res is not None:
        ins.append(res)
        specs.append(blk)
    out_shape = [jax.ShapeDtypeStruct((t, heads * width), out_dtype), jax.ShapeDtypeStruct((8, width), F32)]
    out_specs = [blk, pl.BlockSpec((8, width), lambda i, h: (0, 0))]
    if with_delta:
        out_shape.append(jax.ShapeDtypeStruct((t, heads * width), F32))
        out_specs.append(blk)
    return _pcall(body, name=name, grid=(t // tm, heads), in_specs=specs, out_specs=out_specs, out_shape=out_shape)(*ins)


def _sig(x):
    return 1.0 / (1.0 + jnp.exp(-x))


@jax.custom_vjp
def _sigmoid(x):
    return _sig(x)


def _sigmoid_fwd(x):
    s = _sig(x)
    return s, s


def _sigmoid_bwd(s, g):
    return (g * s * (1.0 - s),)


_sigmoid.defvjp(_sigmoid_fwd, _sigmoid_bwd)


@jax.custom_vjp
def _softplus(x):
    return jnp.maximum(x, 0.0) + jnp.log(1.0 + jnp.exp(-jnp.abs(x)))


def _softplus_fwd(x):
    return _softplus(x), x


def _softplus_bwd(x, g):
    return (g * _sig(x),)


_softplus.defvjp(_softplus_fwd, _softplus_bwd)


def _silu(x):
    return x * _sig(x)


def _dsilu(x):
    s = _sig(x)
    return s * (1.0 + x * (1.0 - s))


NN3 = (((2,), (1,)), ((0,), (0,)))
NT3 = (((2,), (2,)), ((0,), (0,)))
TN3 = (((1,), (1,)), ((0,), (0,)))


def _bdot(a, b, dims):
    return lax.dot_general(a.astype(BF16), b.astype(BF16), dims, preferred_element_type=F32)


def _bf16_part(x):
    bits = lax.bitcast_convert_type(x, jnp.uint32) & jnp.uint32(0xFFFF0000)
    return lax.bitcast_convert_type(bits, F32)


def _scan_rows(x, reverse):
    c = x.shape[1]
    row = lax.broadcasted_iota(jnp.int32, x.shape, 1)
    step = 1
    while step < c:
        if reverse:
            x = x + jnp.where(row < c - step, pltpu.roll(x, c - step, axis=1), 0.0)
        else:
            x = x + jnp.where(row >= step, pltpu.roll(x, step, axis=1), 0.0)
        step *= 2
    return x


@jax.custom_vjp
def _prefix_rows(x):
    return _scan_rows(x, False)


_prefix_rows.defvjp(lambda x: (_scan_rows(x, False), None), lambda _, g: (_scan_rows(g, True),))


def _dot3(a, b, dims):
    (ca,), (cb,) = dims[0]
    a_hi, b_hi = _bf16_part(a), _bf16_part(b)
    a_lo, b_lo = (a - a_hi).astype(BF16), (b - b_hi).astype(BF16)
    a_hi, b_hi = a_hi.astype(BF16), b_hi.astype(BF16)
    return lax.dot_general(jnp.concatenate([a_hi, a_hi, a_lo], axis=ca), jnp.concatenate([b_hi, b_lo, b_hi], axis=cb),
                           dims, preferred_element_type=F32)


@jax.custom_vjp
def _nn_hi(a, b):
    return _dot3(a, b, NN3)


_nn_hi.defvjp(lambda a, b: (_dot3(a, b, NN3), (a, b)), lambda r, g: (_dot3(g, r[1], NT3), _dot3(r[0], g, TN3)))


@jax.custom_vjp
def _cat_lanes(a, b):
    return jnp.concatenate([a, b], axis=-1)


_cat_lanes.defvjp(lambda a, b: (jnp.concatenate([a, b], axis=-1), None), lambda _, g: (g[..., :HEAD], g[..., HEAD:]))


@jax.custom_vjp
def _split_lanes(x):
    return x[..., :HEAD], x[..., HEAD:]


_split_lanes.defvjp(lambda x: ((x[..., :HEAD], x[..., HEAD:]), None), lambda _, g: (jnp.concatenate(g, axis=-1),))


@jax.custom_vjp
def _nn(a, b):
    return _bdot(a, b, NN3)


_nn.defvjp(lambda a, b: (_bdot(a, b, NN3), (a, b)), lambda r, g: (_bdot(g, r[1], NT3), _bdot(r[0], g, TN3)))


@jax.custom_vjp
def _nt(a, b):
    return _bdot(a, b, NT3)


_nt.defvjp(lambda a, b: (_bdot(a, b, NT3), (a, b)), lambda r, g: (_bdot(g, r[1], NN3), _bdot(g, r[0], TN3)))


@jax.custom_vjp
def _tn(a, b):
    return _bdot(a, b, TN3)


_tn.defvjp(lambda a, b: (_bdot(a, b, TN3), (a, b)), lambda r, g: (_bdot(r[1], g, NT3), _bdot(r[0], g, NN3)))


CONV_ROWS, CONV_COLS = 256, 1024


def _rows_down(cur, prev8, s):
    r = pltpu.roll(cur, s, axis=0)
    rp = pltpu.roll(prev8, s, axis=0)
    row = lax.broadcasted_iota(jnp.int32, rp.shape, 0)
    head = jnp.where(row < s, rp, r[:8])
    return head if cur.shape[0] == 8 else jnp.concatenate([head, r[8:]], axis=0)


def _rows_up(cur, next8, s):
    n = cur.shape[0]
    r = pltpu.roll(cur, n - s, axis=0)
    rn = pltpu.roll(next8, 8 - s, axis=0)
    row = lax.broadcasted_iota(jnp.int32, rn.shape, 0)
    tail = jnp.where(row >= 8 - s, rn, r[n - 8:])
    return tail if n == 8 else jnp.concatenate([r[:n - 8], tail], axis=0)


def _conv_taps(cur, prev8):
    return [_rows_down(cur, prev8, GDN_CONV - 1 - j) for j in range(GDN_CONV - 1)] + [cur]


def _conv_pre(taps, w):
    acc = w[0:1] * taps[0]
    for j in range(1, GDN_CONV):
        acc = acc + w[j:j + 1] * taps[j]
    return acc


def _conv_fwd(proj, conv_w, *, name):
    t = proj.shape[0]
    tm, tc = _pick(t, CONV_ROWS), CONV_COLS
    nb = tm // 8

    def body(u_ref, p_ref, w_ref, o_ref):
        i = pl.program_id(1)
        prev = jnp.where(i > 0, p_ref[...], 0.0)
        o_ref[...] = _silu(_conv_pre(_conv_taps(u_ref[...], prev), w_ref[...]))

    return _pcall(body, name=name, grid=(CONV_CH // tc, t // tm),
                  in_specs=[pl.BlockSpec((tm, tc), lambda j, i: (i, j)),
                            pl.BlockSpec((8, tc), lambda j, i: (jnp.maximum(i * nb - 1, 0), j)),
                            pl.BlockSpec((8, tc), lambda j, i: (0, j))],
                  out_specs=pl.BlockSpec((tm, tc), lambda j, i: (i, j)),
                  out_shape=jax.ShapeDtypeStruct((t, CONV_CH), F32))(proj, proj, conv_w)


def _conv_bwd(proj, conv_w, dy, *, name):
    t = proj.shape[0]
    tm, tc = _pick(t, CONV_ROWS), CONV_COLS
    nb = tm // 8
    last = t // tm - 1

    def body(u_ref, p_ref, n_ref, dy_ref, dyn_ref, w_ref, du_ref, dw_ref):
        i = pl.program_id(1)
        w = w_ref[...]
        cur = u_ref[...]
        taps = _conv_taps(cur, jnp.where(i > 0, p_ref[...], 0.0))
        dc = dy_ref[...] * _dsilu(_conv_pre(taps, w))
        taps_next = _conv_taps(n_ref[...], cur[tm - 8:])
        dc_next = jnp.where(i < last, dyn_ref[...], 0.0) * _dsilu(_conv_pre(taps_next, w))
        du = w[3:4] * dc
        for j in range(GDN_CONV - 1):
            du = du + w[j:j + 1] * _rows_up(dc, dc_next, GDN_CONV - 1 - j)
        du_ref[...] = du.astype(du_ref.dtype)

        @pl.when(i == 0)
        def _():
            dw_ref[...] = jnp.zeros_like(dw_ref)

        for j in range(GDN_CONV):
            dw_ref[j] += (dc * taps[j]).reshape(nb, 8, tc).sum(axis=0)

    cur = lambda j, i: (i, j)
    return _pcall(body, name=name, grid=(CONV_CH // tc, t // tm),
                  in_specs=[pl.BlockSpec((tm, tc), cur),
                            pl.BlockSpec((8, tc), lambda j, i: (jnp.maximum(i * nb - 1, 0), j)),
                            pl.BlockSpec((8, tc), lambda j, i: (jnp.minimum((i + 1) * nb, t // 8 - 1), j)),
                            pl.BlockSpec((tm, tc), cur),
                            pl.BlockSpec((8, tc), lambda j, i: (jnp.minimum((i + 1) * nb, t // 8 - 1), j)),
                            pl.BlockSpec((8, tc), lambda j, i: (0, j))],
                  out_specs=[pl.BlockSpec((tm, tc), cur), pl.BlockSpec((GDN_CONV, 8, tc), lambda j, i: (0, 0, j))],
                  out_shape=[jax.ShapeDtypeStruct((t, CONV_CH), BF16), jax.ShapeDtypeStruct((GDN_CONV, 8, CONV_CH), F32)],
                  )(proj, proj, proj, dy, dy, conv_w)


def _gdn_chunk(q_raw, k_raw, v, misc, params, state):
    nh, c = q_raw.shape[0], q_raw.shape[1]
    lane = lax.broadcasted_iota(jnp.int32, misc.shape, 1)
    prow = lax.broadcasted_iota(jnp.int32, params.shape, 0)
    plane = lax.broadcasted_iota(jnp.int32, params.shape, 1)
    heads = lambda pieces: jnp.concatenate([p[None] for p in pieces], axis=0)
    col = lambda at: heads([jnp.sum(jnp.where(lane == at + h, misc, 0.0), axis=1, keepdims=True) for h in range(nh)])
    par = lambda row: heads([jnp.sum(jnp.where((prow == row) & (plane == h), params, 0.0), keepdims=True)
                             for h in range(nh)])
    b_raw, a_raw = col(LANE_B), col(LANE_A)
    a_log, dt_bias = par(0), par(1)
    beta = _sigmoid(b_raw)
    g = -jnp.exp(a_log) * _softplus(a_raw + dt_bias)

    q = q_raw * lax.rsqrt(jnp.sum(q_raw * q_raw, axis=-1, keepdims=True) + EPS) * (HEAD ** -0.5)
    k = k_raw * lax.rsqrt(jnp.sum(k_raw * k_raw, axis=-1, keepdims=True) + EPS)

    ri = lax.broadcasted_iota(jnp.int32, (c, c), 0)
    ci = lax.broadcasted_iota(jnp.int32, (c, c), 1)
    tril, strict = ri >= ci, ri > ci
    gc = _prefix_rows(g)
    gc_col = jnp.broadcast_to(gc, (nh, c, c))
    gc_row = jnp.swapaxes(gc_col, 1, 2)
    decay = jnp.exp(jnp.where(tril, gc_col - gc_row, -1e30))

    kb = k * beta
    vb = v * beta
    a_mat = jnp.where(strict, _nt(kb, k) * decay, 0.0)
    x = -a_mat
    inv = (ri == ci).astype(F32) + x
    for _ in range(5):
        x = _nn_hi(x, x)
        inv = inv + _nn_hi(inv, x)
    u, w = _split_lanes(_nn_hi(inv, _cat_lanes(vb, kb * jnp.exp(gc))))
    intra = _nt(q, k) * decay

    v_new = u - _nn(w, state)
    o = _nn(q * jnp.exp(gc), state) + _nn(intra, v_new)
    g_last = jnp.sum(g, axis=1, keepdims=True)
    k_dec = k * jnp.exp(g_last - gc)
    new_state = state * jnp.exp(g_last) + _tn(k_dec, v_new)
    return o, new_state


def _gdn_specs(nc, rev):
    cidx = (lambda n: nc - 1 - n) if rev else (lambda n: n)
    hb = lambda part: pl.BlockSpec((GDN_CHUNK, GDN_QK), lambda n: (cidx(n), part))
    misc = pl.BlockSpec((GDN_CHUNK, HEAD), lambda n: (cidx(n), COL_MISC // HEAD))
    params = pl.BlockSpec((8, HEAD), lambda n: (0, 0))
    hist = pl.BlockSpec((1, GDN_HEADS, HEAD, HEAD), lambda n: (cidx(n), 0, 0, 0))
    return hb, misc, params, hist


def _split_heads(v):
    return jnp.stack([v[:, h * HEAD:(h + 1) * HEAD] for h in range(v.shape[1] // HEAD)])


def _merge_heads(v):
    return jnp.concatenate([v[h] for h in range(v.shape[0])], axis=1)


def _gdn_fwd(qkv, proj, params, *, name):
    t = qkv.shape[0]
    nc = t // GDN_CHUNK
    hb, misc, pspec, hist = _gdn_specs(nc, False)

    def body(q_ref, k_ref, v_ref, m_ref, p_ref, o_ref, hist_ref, s_ref):
        @pl.when(pl.program_id(0) == 0)
        def _():
            s_ref[...] = jnp.zeros_like(s_ref)

        state = s_ref[...]
        hist_ref[0] = state
        o, new_state = _gdn_chunk(_split_heads(q_ref[...]), _split_heads(k_ref[...]), _split_heads(v_ref[...]),
                                  m_ref[...], p_ref[...], state)
        o_ref[...] = _merge_heads(o)
        s_ref[...] = new_state

    return _pcall(body, name=name, grid=(nc,),
                  in_specs=[hb(0), hb(1), hb(2), misc, pspec],
                  out_specs=[hb(0), hist],
                  out_shape=[jax.ShapeDtypeStruct((t, GDN_QK), F32),
                             jax.ShapeDtypeStruct((nc, GDN_HEADS, HEAD, HEAD), F32)],
                  scratch=[pltpu.VMEM((GDN_HEADS, HEAD, HEAD), F32)])(qkv, qkv, qkv, proj, params)


def _gdn_bwd(qkv, proj, params, hist_arr, do, dmisc_in, *, name, carry=None):
    t = qkv.shape[0]
    nc = t // GDN_CHUNK
    hb, misc, pspec, hist = _gdn_specs(nc, True)
    mrow = pl.BlockSpec((GDN_CHUNK, HEAD), lambda n: (nc - 1 - n, 0))

    def body(q_ref, k_ref, v_ref, m_ref, p_ref, hist_ref, do_ref, dmi_ref, dqkv_ref, dm_ref, dp_ref, ds_ref):
        @pl.when(pl.program_id(0) == 0)
        def _():
            ds_ref[...] = jnp.zeros_like(ds_ref)
            dp_ref[...] = jnp.zeros_like(dp_ref)

        _, vjp = jax.vjp(_gdn_chunk, _split_heads(q_ref[...]), _split_heads(k_ref[...]), _split_heads(v_ref[...]),
                         m_ref[...], p_ref[...], hist_ref[0])
        dq, dk, dv, dm, dp, ds = vjp((_split_heads(do_ref[...]), ds_ref[...]))
        dqkv_ref[:, 0:GDN_QK] = _merge_heads(dq)
        dqkv_ref[:, GDN_QK:2 * GDN_QK] = _merge_heads(dk)
        dqkv_ref[:, 2 * GDN_QK:] = _merge_heads(dv)
        ds_ref[...] = ds
        dm_ref[...] = dmi_ref[...] + dm
        dp_ref[...] += dp

    return _pcall(body, name=name, grid=(nc,),
                  in_specs=[hb(0), hb(1), hb(2), misc, pspec, hist, hb(0), mrow],
                  out_specs=[pl.BlockSpec((GDN_CHUNK, CONV_CH), lambda n: (nc - 1 - n, 0)), mrow, pspec],
                  out_shape=[jax.ShapeDtypeStruct((t, CONV_CH), F32), jax.ShapeDtypeStruct((t, HEAD), F32),
                             jax.ShapeDtypeStruct((8, HEAD), F32)],
                  scratch=[pltpu.VMEM((GDN_HEADS, HEAD, HEAD), F32)], carry=carry,
                  )(qkv, qkv, qkv, proj, params, hist_arr, do, dmisc_in)


def _gate_fwd(o_raw, proj, w, *, name):
    t = o_raw.shape[0]
    tm = _rows(t, HEAD)
    zb = COL_Z // HEAD

    def body(o_ref, z_ref, w_ref, out_ref):
        ov = o_ref[...]
        r = lax.rsqrt(jnp.mean(ov * ov, axis=-1, keepdims=True) + EPS)
        out_ref[...] = (ov * r * w_ref[...] * _silu(z_ref[...])).astype(out_ref.dtype)

    blk = pl.BlockSpec((tm, HEAD), lambda i, h: (i, h))
    return _pcall(body, name=name, grid=(t // tm, GDN_HEADS),
                  in_specs=[blk, pl.BlockSpec((tm, HEAD), lambda i, h: (i, zb + h)), pl.BlockSpec((1, HEAD), lambda i, h: (0, 0))],
                  out_specs=blk, out_shape=jax.ShapeDtypeStruct((t, GDN_QK), BF16))(o_raw, proj, w)


def _gate_bwd(o_raw, proj, w, dmixed, *, name):
    t = o_raw.shape[0]
    tm = _rows(t, HEAD)
    zb = COL_Z // HEAD

    def body(o_ref, z_ref, w_ref, dy_ref, do_ref, dz_ref, dw_ref):
        ov, zv, dyv = o_ref[...], z_ref[...], dy_ref[...]
        r = lax.rsqrt(jnp.mean(ov * ov, axis=-1, keepdims=True) + EPS)
        xh = ov * r
        dn = dyv * _silu(zv)
        dz_ref[...] = (dyv * xh * w_ref[...] * _dsilu(zv)).astype(dz_ref.dtype)
        dnw = dn * w_ref[...]
        do_ref[...] = r * (dnw - xh * jnp.mean(dnw * xh, axis=-1, keepdims=True))

        @pl.when((pl.program_id(0) == 0) & (pl.program_id(1) == 0))
        def _():
            dw_ref[...] = jnp.zeros_like(dw_ref)

        dw_ref[...] += (dn * xh).reshape(tm // 8, 8, HEAD).sum(axis=0)

    blk = pl.BlockSpec((tm, HEAD), lambda i, h: (i, h))
    return _pcall(body, name=name, grid=(t // tm, GDN_HEADS),
                  in_specs=[blk, pl.BlockSpec((tm, HEAD), lambda i, h: (i, zb + h)), pl.BlockSpec((1, HEAD), lambda i, h: (0, 0)), blk],
                  out_specs=[blk, blk, pl.BlockSpec((8, HEAD), lambda i, h: (0, 0))],
                  out_shape=[jax.ShapeDtypeStruct((t, GDN_QK), F32), jax.ShapeDtypeStruct((t, GDN_QK), BF16),
                             jax.ShapeDtypeStruct((8, HEAD), F32)])(o_raw, proj, w, dmixed)


def _rope_tables():
    half = QK_ROPE // 2
    inv = ROPE_THETA ** (-jnp.arange(half, dtype=F32) / half)
    zeros = jnp.zeros((HEAD - QK_ROPE,), F32)
    inv_row = jnp.concatenate([inv, inv, zeros])
    sign_row = jnp.concatenate([-jnp.ones((half,), F32), jnp.ones((half,), F32), zeros])
    mask_row = jnp.concatenate([jnp.ones((QK_ROPE,), F32), zeros])
    return jnp.concatenate([inv_row[None], sign_row[None], mask_row[None], jnp.zeros((5, HEAD), F32)], axis=0)


def _rope_cs(pos, tab, *, name):
    t = pos.shape[0]
    tm = _pick(t, 1024)

    def body(pos_ref, tab_ref, o_ref):
        tab = tab_ref[...]
        ang = pos_ref[...] * tab[0:1]
        o_ref[...] = jnp.concatenate([jnp.cos(ang) * tab[2:3], jnp.sin(ang) * tab[1:2]], axis=1)

    return _pcall(body, name=name, grid=(t // tm,),
                  in_specs=[pl.BlockSpec((tm, 1), lambda i: (i, 0)), pl.BlockSpec((8, HEAD), lambda i: (0, 0))],
                  out_specs=pl.BlockSpec((tm, 2 * HEAD), lambda i: (i, 0)),
                  out_shape=jax.ShapeDtypeStruct((t, 2 * HEAD), F32))(pos, tab)


def _rotate(x, cs, sign):
    lane = lax.broadcasted_iota(jnp.int32, x.shape, 1)
    half = QK_ROPE // 2
    partner = jnp.where(lane < half, pltpu.roll(x, HEAD - half, axis=1), pltpu.roll(x, half, axis=1))
    return x * cs[:, :HEAD] + partner * (cs[:, HEAD:] * sign)


def _q_rot(q, cs, *, name, sign, out_dtype=BF16):
    t = q.shape[0]
    tm = _pick(t, 1024)
    scale = (HEAD + QK_ROPE) ** -0.5

    def body(q_ref, cs_ref, o_ref):
        qv = q_ref[...].astype(F32)
        rot = _rotate(qv[:, HEAD:], cs_ref[...], sign)
        o_ref[...] = (jnp.concatenate([qv[:, :HEAD], rot], axis=1) * scale).astype(o_ref.dtype)

    blk = pl.BlockSpec((tm, QHEAD), lambda i, h: (i, h))
    return _pcall(body, name=name, grid=(t // tm, MLA_HEADS),
                  in_specs=[blk, pl.BlockSpec((tm, 2 * HEAD), lambda i, h: (i, 0))],
                  out_specs=blk, out_shape=jax.ShapeDtypeStruct((t, MLA_HEADS * QHEAD), out_dtype))(q, cs)


def _q_up(cqn, wuq_p, cs, *, name):
    t, lora = cqn.shape
    tm = _pick(t, 1024)
    scale = (HEAD + QK_ROPE) ** -0.5

    def body(a_ref, w_ref, cs_ref, o_ref):
        qv = lax.dot_general(a_ref[...], w_ref[...], (NN, ((), ())), preferred_element_type=F32)
        rot = _rotate(qv[:, HEAD:], cs_ref[...], 1.0)
        o_ref[...] = (jnp.concatenate([qv[:, :HEAD], rot], axis=1) * scale).astype(o_ref.dtype)

    return _pcall(body, name=name, grid=(t // tm, MLA_HEADS),
                  in_specs=[pl.BlockSpec((tm, lora), lambda i, h: (i, 0)), pl.BlockSpec((lora, QHEAD), lambda i, h: (0, h)),
                            pl.BlockSpec((tm, 2 * HEAD), lambda i, h: (i, 0))],
                  out_specs=pl.BlockSpec((tm, QHEAD), lambda i, h: (i, h)),
                  out_shape=jax.ShapeDtypeStruct((t, MLA_HEADS * QHEAD), BF16))(cqn, wuq_p, cs)


def _kv_up(ckvn, wukv, proj, cs, *, name):
    t, lora = ckvn.shape
    tm = _pick(t, 1024)

    def body(a_ref, w_ref, m_ref, cs_ref, k_ref, v_ref):
        kvv = lax.dot_general(a_ref[...], w_ref[...], (NN, ((), ())), preferred_element_type=F32)
        misc = m_ref[...]
        lane = lax.broadcasted_iota(jnp.int32, misc.shape, 1)
        rot = _rotate(jnp.where(lane < QK_ROPE, misc, 0.0), cs_ref[...], 1.0)
        k_ref[...] = jnp.concatenate([kvv[:, :HEAD], rot], axis=1).astype(k_ref.dtype)
        v_ref[...] = kvv[:, HEAD:].astype(v_ref.dtype)

    return _pcall(body, name=name, grid=(t // tm, MLA_HEADS),
                  in_specs=[pl.BlockSpec((tm, lora), lambda i, h: (i, 0)), pl.BlockSpec((lora, QHEAD), lambda i, h: (0, h)),
                            pl.BlockSpec((tm, HEAD), lambda i, h: (i, COL_MISC // HEAD)),
                            pl.BlockSpec((tm, 2 * HEAD), lambda i, h: (i, 0))],
                  out_specs=[pl.BlockSpec((tm, QHEAD), lambda i, h: (i, h)), pl.BlockSpec((tm, HEAD), lambda i, h: (i, h))],
                  out_shape=[jax.ShapeDtypeStruct((t, MLA_HEADS * QHEAD), BF16), jax.ShapeDtypeStruct((t, MLA_HEADS * HEAD), BF16)],
                  )(ckvn, wukv, proj, cs)


def _krope_bwd(dkr, cs, *, name):
    t = dkr.shape[0]
    tm = _pick(t, 512)

    def body(d_ref, cs_ref, o_ref):
        d = d_ref[...]
        acc = d[:, :HEAD]
        for h in range(1, MLA_HEADS):
            acc = acc + d[:, h * HEAD:(h + 1) * HEAD]
        o_ref[...] = _rotate(acc, cs_ref[...], -1.0)

    return _pcall(body, name=name, grid=(t // tm,),
                  in_specs=[pl.BlockSpec((tm, MLA_HEADS * HEAD), lambda i: (i, 0)), pl.BlockSpec((tm, 2 * HEAD), lambda i: (i, 0))],
                  out_specs=pl.BlockSpec((tm, HEAD), lambda i: (i, 0)),
                  out_shape=jax.ShapeDtypeStruct((t, HEAD), F32))(dkr, cs)


NEG = -1e30


def _tri(step, counts):
    starts = [sum(counts[:o]) for o in range(len(counts))]
    outer = sum([(step >= s).astype(jnp.int32) for s in starts[1:]], jnp.int32(0))
    start = sum([(step >= starts[o]).astype(jnp.int32) * (starts[o] - starts[o - 1]) for o in range(1, len(counts))], jnp.int32(0))
    return outer, step - start


def _attn_fwd(q, k, v, *, name, tq=1024, tk=1024, carry=None):
    t = q.shape[0]
    tq, tk = _pick(t, tq), _pick(t, tk)
    nq = t // tq
    last_kv = lambda i: (i * tq + tq - 1) // tk
    counts = [last_kv(i) + 1 for i in range(nq)]

    def body(q_ref, k_ref, v_ref, o_ref, lse_ref, m_ref, l_ref, acc_ref):
        i, j = _tri(pl.program_id(1), counts)

        @pl.when(j == 0)
        def _():
            m_ref[...] = jnp.full_like(m_ref, NEG)
            l_ref[...] = jnp.zeros_like(l_ref)
            acc_ref[...] = jnp.zeros_like(acc_ref)

        def step(masked):
            s = lax.dot_general(q_ref[...], k_ref[...], (NT, ((), ())), preferred_element_type=F32)
            if masked:
                qpos = i * tq + lax.broadcasted_iota(jnp.int32, s.shape, 0)
                kpos = j * tk + lax.broadcasted_iota(jnp.int32, s.shape, 1)
                s = jnp.where(kpos <= qpos, s, NEG)
            m_prev = m_ref[...]
            m_new = jnp.maximum(m_prev, jnp.max(s, axis=1, keepdims=True))
            alpha = jnp.exp(m_prev - m_new)
            p = jnp.exp(s - m_new)
            l_ref[...] = alpha * l_ref[...] + jnp.sum(p, axis=1, keepdims=True)
            acc_ref[...] = alpha * acc_ref[...] + lax.dot_general(p.astype(BF16), v_ref[...], (NN, ((), ())),
                                                                  preferred_element_type=F32)
            m_ref[...] = m_new

        crosses = j * tk + tk - 1 > i * tq

        @pl.when(crosses)
        def _():
            step(True)

        @pl.when(jnp.logical_not(crosses))
        def _():
            step(False)

        @pl.when(j == last_kv(i))
        def _():
            o_ref[...] = acc_ref[...] / l_ref[...]
            lse_ref[...] = jnp.broadcast_to(m_ref[...] + jnp.log(l_ref[...]), lse_ref.shape)

    qblk = pl.BlockSpec((tq, QHEAD), lambda h, s: (_tri(s, counts)[0], h))
    oblk = pl.BlockSpec((tq, HEAD), lambda h, s: (_tri(s, counts)[0], h))
    return _pcall(body, name=name, grid=(MLA_HEADS, sum(counts)),
                  in_specs=[qblk, pl.BlockSpec((tk, QHEAD), lambda h, s: (_tri(s, counts)[1], h)),
                            pl.BlockSpec((tk, HEAD), lambda h, s: (_tri(s, counts)[1], h))],
                  out_specs=[oblk, oblk],
                  out_shape=[jax.ShapeDtypeStruct((t, MLA_HEADS * HEAD), F32), jax.ShapeDtypeStruct((t, MLA_HEADS * HEAD), F32)],
                  scratch=[pltpu.VMEM((tq, 1), F32), pltpu.VMEM((tq, 1), F32), pltpu.VMEM((tq, HEAD), F32)],
                  carry=carry)(q, k, v)


def _attn_bwd(q, k, v, do, lse, delta, *, name, tq=1024, tk=1024, carry=None):
    t = q.shape[0]
    tq, tk = _pick(t, tq), _pick(t, tk)
    nq, nk = t // tq, t // tk
    first_q = lambda j: (j * tk) // tq
    counts = [nq - first_q(j) for j in range(nk)]

    def where(step):
        j, off = _tri(step, counts)
        return j, first_q(j) + off

    lanes = lambda col: jnp.tile(col, (1, tk // HEAD))

    def body(q_ref, k_ref, v_ref, do_ref, lse_ref, dl_ref, dq_ref, dkv_ref, dkr_ref, dk_acc, dv_acc):
        j, i = where(pl.program_id(1))

        @pl.when(i == first_q(j))
        def _():
            dk_acc[...] = jnp.zeros_like(dk_acc)
            dv_acc[...] = jnp.zeros_like(dv_acc)

        def step(masked):
            qv, kv_, dov = q_ref[...], k_ref[...], do_ref[...].astype(BF16)
            s = lax.dot_general(qv, kv_, (NT, ((), ())), preferred_element_type=F32)
            p = jnp.exp((s - lanes(lse_ref[...])).astype(BF16))
            if masked:
                qpos = i * tq + lax.broadcasted_iota(jnp.int32, s.shape, 0)
                kpos = j * tk + lax.broadcasted_iota(jnp.int32, s.shape, 1)
                p = jnp.where(kpos <= qpos, p, jnp.zeros_like(p))
            dv_acc[...] += lax.dot_general(p, dov, (TN, ((), ())), preferred_element_type=F32)
            dp = lax.dot_general(dov, v_ref[...], (NT, ((), ())), preferred_element_type=F32)
            ds = p * (dp - lanes(dl_ref[...])).astype(BF16)
            dk_acc[...] += lax.dot_general(ds, qv, (TN, ((), ())), preferred_element_type=F32)
            contrib = lax.dot_general(ds, kv_, (NN, ((), ())), preferred_element_type=F32)
            rows = pl.ds(pl.multiple_of(i * tq, tq), tq)

            @pl.when(j == 0)
            def _():
                dq_ref[rows, :] = contrib

            @pl.when(j > 0)
            def _():
                dq_ref[rows, :] += contrib

        crosses = j * tk + tk - 1 > i * tq

        @pl.when(crosses)
        def _():
            step(True)

        @pl.when(jnp.logical_not(crosses))
        def _():
            step(False)

        @pl.when(i == nq - 1)
        def _():
            dk = dk_acc[...]
            dkv_ref[...] = jnp.concatenate([dk[:, :HEAD], dv_acc[...]], axis=1).astype(dkv_ref.dtype)
            dkr_ref[...] = dk[:, HEAD:]

    qi = lambda h, s: (where(s)[1], h)
    kj = lambda h, s: (where(s)[0], h)
    return _pcall(body, name=name, grid=(MLA_HEADS, sum(counts)),
                  in_specs=[pl.BlockSpec((tq, QHEAD), qi), pl.BlockSpec((tk, QHEAD), kj), pl.BlockSpec((tk, HEAD), kj),
                            pl.BlockSpec((tq, HEAD), qi), pl.BlockSpec((tq, HEAD), qi), pl.BlockSpec((tq, HEAD), qi)],
                  out_specs=[pl.BlockSpec((t, QHEAD), lambda h, s: (0, h)), pl.BlockSpec((tk, QHEAD), kj),
                             pl.BlockSpec((tk, HEAD), kj)],
                  out_shape=[jax.ShapeDtypeStruct((t, MLA_HEADS * QHEAD), F32), jax.ShapeDtypeStruct((t, MLA_HEADS * QHEAD), BF16),
                             jax.ShapeDtypeStruct((t, MLA_HEADS * HEAD), F32)],
                  scratch=[pltpu.VMEM((tk, QHEAD), F32), pltpu.VMEM((tk, HEAD), F32)], carry=carry)(q, k, v, do, lse, delta)


def _ffn_up(h, wgate, wup, *, name, bm=512, bn=FF_WIDE):
    t = h.shape[0]
    bm = _pick(t, bm)

    def body(h_ref, wg_ref, wu_ref, g_ref, u_ref, a_ref):
        hv = h_ref[...]
        g = lax.dot_general(hv, wg_ref[...], (NN, ((), ())), preferred_element_type=F32)
        u = lax.dot_general(hv, wu_ref[...], (NN, ((), ())), preferred_element_type=F32)
        g_ref[...] = g.astype(g_ref.dtype)
        u_ref[...] = u.astype(u_ref.dtype)
        a_ref[...] = (_silu(g) * u).astype(a_ref.dtype)

    w_spec = pl.BlockSpec((D_MODEL, bn), lambda j, i: (0, j))
    o_spec = pl.BlockSpec((bm, bn), lambda j, i: (i, j))
    sds = jax.ShapeDtypeStruct((t, D_FF), BF16)
    return _pcall(body, name=name, grid=(D_FF // bn, t // bm),
                  in_specs=[pl.BlockSpec((bm, D_MODEL), lambda j, i: (i, 0)), w_spec, w_spec],
                  out_specs=[o_spec] * 3, out_shape=[sds] * 3)(h, wgate, wup)


def _ffn_down_dx(dy, wdown, gate, up, *, name, bm=512, bn=FF_WIDE):
    t = dy.shape[0]
    bm = _pick(t, bm)

    def body(dy_ref, w_ref, g_ref, u_ref, dg_ref, du_ref):
        d = lax.dot_general(dy_ref[...].astype(BF16), w_ref[...], (NT, ((), ())), preferred_element_type=F32)
        g = g_ref[...].astype(F32)
        dg_ref[...] = (d * u_ref[...].astype(F32) * _dsilu(g)).astype(dg_ref.dtype)
        du_ref[...] = (d * _silu(g)).astype(du_ref.dtype)

    o_spec = pl.BlockSpec((bm, bn), lambda j, i: (i, j))
    sds = jax.ShapeDtypeStruct((t, D_FF), BF16)
    return _pcall(body, name=name, grid=(D_FF // bn, t // bm),
                  in_specs=[pl.BlockSpec((bm, D_MODEL), lambda j, i: (i, 0)), pl.BlockSpec((bn, D_MODEL), lambda j, i: (j, 0)),
                            o_spec, o_spec],
                  out_specs=[o_spec, o_spec], out_shape=[sds, sds])(dy, wdown, gate, up)


def _loss_bwd(x2, w, target, *, name):
    t = x2.shape[0]
    tm = _rows(t, D_MODEL)

    def body(x_ref, w_ref, t_ref, dx_ref, dw_ref, l_ref):
        xv, wv = x_ref[...], w_ref[...]
        r = lax.rsqrt(jnp.mean(xv * xv, axis=-1, keepdims=True) + EPS)
        xh = xv * r
        err = xh * wv - t_ref[...]
        dy = err * (1.0 / D_MODEL)
        dyw = dy * wv
        dx_ref[...] = r * (dyw - xh * jnp.mean(dyw * xh, axis=-1, keepdims=True))

        @pl.when(pl.program_id(0) == 0)
        def _():
            dw_ref[...] = jnp.zeros_like(dw_ref)
            l_ref[...] = jnp.zeros_like(l_ref)

        dw_ref[...] += (dy * xh).reshape(tm // 8, 8, D_MODEL).sum(axis=0)
        sq = (err * err).reshape(tm // 8, 8, D_MODEL).sum(axis=0)
        part = sq[:, :HEAD]
        for c in range(1, D_MODEL // HEAD):
            part = part + sq[:, c * HEAD:(c + 1) * HEAD]
        l_ref[...] += part * (0.5 / D_MODEL)

    row = pl.BlockSpec((tm, D_MODEL), lambda i: (i, 0))
    return _pcall(body, name=name, grid=(t // tm,),
                  in_specs=[row, pl.BlockSpec((1, D_MODEL), lambda i: (0, 0)), row],
                  out_specs=[row, pl.BlockSpec((8, D_MODEL), lambda i: (0, 0)), pl.BlockSpec((8, HEAD), lambda i: (0, 0))],
                  out_shape=[jax.ShapeDtypeStruct((t, D_MODEL), F32), jax.ShapeDtypeStruct((8, D_MODEL), F32),
                             jax.ShapeDtypeStruct((8, HEAD), F32)])(x2, w, target)


def _unshard_cols(g):
    return jnp.transpose(g, (1, 0, 2)).reshape(g.shape[1], N_DEV * g.shape[2])


def _shard_cols(w):
    return jnp.transpose(w.reshape(w.shape[0], N_DEV, w.shape[1] // N_DEV), (1, 0, 2))


_WIN_ORDER = ((0, 4096), (4112, 5136), (5136, 5200), (4096, 4112))
_WIN_SHARD = IN_WIDTH // N_DEV


def _win_pieces():
    out, pos = [], 0
    for a, b in _WIN_ORDER:
        c = a
        while c < b:
            dev, off = divmod(c, _WIN_SHARD)
            width = min(b, (dev + 1) * _WIN_SHARD) - c
            out.append((dev, off, width, pos))
            c, pos = c + width, pos + width
    return out


def _win_gathered_to_padded(g):
    pieces = [g[dev][:, off:off + width] for dev, off, width, _ in _win_pieces()]
    return jnp.concatenate(pieces + [jnp.zeros((g.shape[1], PROJ_W - IN_WIDTH), g.dtype)], axis=1)


def _win_padded_to_shards(d):
    shards = []
    for dev in range(N_DEV):
        mine = sorted((off, width, pos) for dv, off, width, pos in _win_pieces() if dv == dev)
        shards.append(jnp.concatenate([d[:, pos:pos + width] for _, width, pos in mine], axis=1))
    return jnp.stack(shards)


def _wuq_to_padded(w):
    w3 = w.reshape(w.shape[0], MLA_HEADS, HEAD + QK_ROPE)
    return jnp.pad(w3, ((0, 0), (0, 0), (0, QHEAD - HEAD - QK_ROPE))).reshape(w.shape[0], MLA_HEADS * QHEAD)


def _wuq_from_padded(d):
    return d.reshape(d.shape[0], MLA_HEADS, QHEAD)[:, :, :HEAD + QK_ROPE].reshape(d.shape[0], MLA_HEADS * (HEAD + QK_ROPE))


def _late_weights(g_out, g_gate, g_up, g_down):
    return g_out.reshape(D_MODEL, D_MODEL), _unshard_cols(g_gate), _unshard_cols(g_up), g_down.reshape(D_FF, D_MODEL)


def _local_step(x, pos, target, win_p, wuq_p, wukv, late, conv_w, small, exchange):
    cs = _rope_cs(pos, _rope_tables(), name="rope_cs")
    if not exchange:
        wout, wgate, wup, wdown = late
    h1 = _rms_fwd(x, small["attn_norm_w"], name="rms1_fwd", width=D_MODEL)
    proj = _mm(h1, win_p, name="mm_in", bn=PROJ_BLK)
    qkv = _conv_fwd(proj, conv_w, name="conv_fwd")
    o_gdn_raw, hist = _gdn_fwd(qkv, proj, small["gdn_params"], name="gdn_fwd")
    o_gdn = _gate_fwd(o_gdn_raw, proj, small["gdn_norm_w"], name="gate_fwd")
    cqn = _rms_fwd(proj, small["q_norm_w"], name="rmsq_fwd", width=Q_LORA, col0=COL_CQ)
    ckvn = _rms_fwd(proj, small["kv_norm_w"], name="rmskv_fwd", width=KV_LORA, col0=COL_CKV)
    q_full = _q_up(cqn, wuq_p, cs, name="q_up")
    k_full, v_b = _kv_up(ckvn, wukv, proj, cs, name="kv_up")
    if exchange:
        (o_mla_raw, lse), gathered = _attn_fwd(q_full, k_full, v_b, name="attn_fwd", carry=_Gather(late))
        wout, wgate, wup, wdown = _late_weights(*gathered)
    else:
        o_mla_raw, lse = _attn_fwd(q_full, k_full, v_b, name="attn_fwd")
    o_mla = _rms_fwd(o_mla_raw, small["mla_out_norm_w"], name="rmso_fwd", width=HEAD, heads=MLA_HEADS)
    mixed = jnp.concatenate([o_gdn, o_mla], axis=1)
    x1 = _mm(mixed, wout, name="mm_out", res=x)
    h2 = _rms_fwd(x1, small["ffn_norm_w"], name="rms2_fwd", width=D_MODEL)
    gate, up, act = _ffn_up(h2, wgate, wup, name="ffn_up")
    x2 = _mm(act, wdown, name="mm_down", res=x1, bk=FF_WIDE)
    dx2, dw_final, loss_part = _loss_bwd(x2, small["final_norm_w"], target, name="loss_bwd")
    dgate, dup = _ffn_down_dx(dx2, wdown, gate, up, name="ffn_down_dx")
    d_wdown = _mm(act, dx2, name="mm_down_dw", ta=True, out_dtype=BF16, bm=FF_WIDE)
    dh2 = _mm(dgate, wgate, name="mm_gateup_dx", tb=True, bk=FF_WIDE, pair=(dup, wup))
    d_wgate, d_wup = _mm_dw2(h2, dgate, dup, name="mm_gateup_dw")
    dx1, dw_ffn = _rms_bwd(x1, small["ffn_norm_w"], dh2, name="rms2_bwd", width=D_MODEL, res=dx2)
    dmixed = _mm(dx1, wout, name="mm_out_dx", tb=True)
    d_wout = _mm(mixed, dx1, name="mm_out_dw", ta=True, out_dtype=BF16)
    do_mla, dw_mla_out, delta = _rms_bwd(o_mla_raw, small["mla_out_norm_w"], dmixed, name="rmso_bwd", width=HEAD,
                                         heads=MLA_HEADS, dcol0=GDN_QK, with_delta=True, out_dtype=BF16)
    if exchange:
        send = [d_wdown.reshape(N_DEV, D_FF // N_DEV, D_MODEL), _shard_cols(d_wgate), _shard_cols(d_wup)]
        (dq_full, dkv, dkr_h), (r_down, r_gate, r_up) = _attn_bwd(q_full, k_full, v_b, do_mla, lse, delta, name="attn_bwd",
                                                                  carry=_Exchange(send, [False] * 3))
    else:
        dq_full, dkv, dkr_h = _attn_bwd(q_full, k_full, v_b, do_mla, lse, delta, name="attn_bwd")
    dq_pre = _q_rot(dq_full, cs, name="q_rot_bwd", sign=-1.0)
    dmisc_kr = _krope_bwd(dkr_h, cs, name="krope_bwd")
    dcqn = _mm(dq_pre, wuq_p, name="mm_uq_dx", tb=True)
    d_wuq = _mm(cqn, dq_pre, name="mm_uq_dw", ta=True, out_dtype=BF16)
    dckvn = _mm(dkv, wukv, name="mm_ukv_dx", tb=True)
    d_wukv = _mm(ckvn, dkv, name="mm_ukv_dw", ta=True, out_dtype=BF16)
    dcq, dw_qn = _rms_bwd(proj, small["q_norm_w"], dcqn, name="rmsq_bwd", width=Q_LORA, col0=COL_CQ, out_dtype=BF16)
    dckv, dw_kvn = _rms_bwd(proj, small["kv_norm_w"], dckvn, name="rmskv_bwd", width=KV_LORA, col0=COL_CKV, out_dtype=BF16)
    do_gdn, dz, dw_gdn = _gate_bwd(o_gdn_raw, proj, small["gdn_norm_w"], dmixed, name="gate_bwd")
    if exchange:
        send = [d_wout.reshape(N_DEV, D_MODEL // N_DEV, D_MODEL), _shard_cols(_wuq_from_padded(d_wuq)), _shard_cols(d_wukv)]
        (dqkv, dmisc, d_params), (r_out, r_uq, r_ukv) = _gdn_bwd(
            qkv, proj, small["gdn_params"], hist, do_gdn, dmisc_kr, name="gdn_bwd", carry=_Exchange(send, [False] * 3))
    else:
        dqkv, dmisc, d_params = _gdn_bwd(qkv, proj, small["gdn_params"], hist, do_gdn, dmisc_kr, name="gdn_bwd")
    dqkv_pre, dconv = _conv_bwd(proj, conv_w, dqkv, name="conv_bwd")
    dproj = jnp.concatenate([dqkv_pre, dz, dcq, dckv, dmisc.astype(BF16), jnp.zeros((x.shape[0], PROJ_W - COL_MISC - HEAD), BF16)], axis=1)
    d_win = _mm(h1, dproj, name="mm_in_dw", ta=True, out_dtype=BF16, bn=PROJ_BLK)
    if exchange:
        dh1, (r_in,) = _mm(dproj, win_p, name="mm_in_dx", tb=True, bk=PROJ_BLK,
                           carry=_Exchange([_win_padded_to_shards(d_win)], [False]))
        d_win = r_in
    else:
        dh1 = _mm(dproj, win_p, name="mm_in_dx", tb=True, bk=PROJ_BLK)
    dx, dw_attn = _rms_bwd(x, small["attn_norm_w"], dh1, name="rms1_bwd", width=D_MODEL, res=dx1)

    if exchange:
        big = {"w_in": d_win, "w_uq": r_uq, "w_ukv": r_ukv, "w_out": r_out, "w_gate": r_gate, "w_up": r_up, "w_down": r_down}
    else:
        big = {"w_in": d_win, "w_uq": d_wuq, "w_ukv": d_wukv, "w_out": d_wout, "w_gate": d_wgate, "w_up": d_wup,
               "w_down": d_wdown}
    sm = {"attn_norm_w": dw_attn, "ffn_norm_w": dw_ffn, "final_norm_w": dw_final, "q_norm_w": dw_qn, "kv_norm_w": dw_kvn,
          "gdn_norm_w": dw_gdn, "mla_out_norm_w": dw_mla_out, "gdn_params": d_params, "conv_w": dconv, "loss": loss_part}
    return dx, big, sm


def _exchange(ex, *, name):
    def body(*refs):
        ins, outs, sems = refs[:ex.n], refs[ex.n:2 * ex.n], refs[2 * ex.n:]
        ex.start(ins, outs, sems)
        ex.forward(ins, outs, sems)
        ex.finish(ins, outs, sems)

    any_spec = pl.BlockSpec(memory_space=pl.ANY)
    return pl.pallas_call(body, name=name, in_specs=[any_spec] * ex.n, out_specs=[any_spec] * ex.n,
                          out_shape=ex.out_shape(), scratch_shapes=ex.sems())(*ex.arrays)


def _adamw_math(g, w, m, v):
    m = ADAM_B1 * m + (1.0 - ADAM_B1) * g
    v = ADAM_B2 * v + (1.0 - ADAM_B2) * (g * g)
    m_hat = m / (1.0 - ADAM_B1 ** ADAM_STEP)
    v_hat = v / (1.0 - ADAM_B2 ** ADAM_STEP)
    delta = -ADAM_LR * (m_hat / (jnp.sqrt(v_hat) + ADAM_EPS) + ADAM_WD * w)
    return delta, m, v


def _adamw(parts, w, m, v, *, name):
    npart, r, c = parts.shape
    tr = r if r * c * 4 <= (1 << 20) else _rows(r, c, 1 << 20)

    def body(p_ref, w_ref, m_ref, v_ref, g_ref, d_ref, nm_ref, nv_ref):
        g = p_ref[0].astype(F32)
        for s in range(1, npart):
            g = g + p_ref[s].astype(F32)
        g_ref[...] = g
        d_ref[...], nm_ref[...], nv_ref[...] = _adamw_math(g, w_ref[...], m_ref[...], v_ref[...])

    blk = pl.BlockSpec((tr, c), lambda i: (i, 0))
    sds = jax.ShapeDtypeStruct((r, c), F32)
    return _pcall(body, name=name, grid=(r // tr,),
                  in_specs=[pl.BlockSpec((npart, tr, c), lambda i: (0, i, 0)), blk, blk, blk],
                  out_specs=[blk] * 4, out_shape=[sds] * 4)(parts, w, m, v)


def _sum_parts(parts, *, name):
    npart, r, c = parts.shape

    def body(p_ref, o_ref):
        g = p_ref[0]
        for s in range(1, npart):
            g = g + p_ref[s]
        o_ref[...] = g

    return _pcall(body, name=name, grid=(1,), in_specs=[pl.BlockSpec((npart, r, c), lambda i: (0, 0, 0))],
                  out_specs=pl.BlockSpec((r, c), lambda i: (0, 0)), out_shape=jax.ShapeDtypeStruct((r, c), F32))(parts)


_SMALL = (("attn_norm_w", D_MODEL), ("ffn_norm_w", D_MODEL), ("final_norm_w", D_MODEL), ("q_norm_w", Q_LORA),
          ("kv_norm_w", KV_LORA), ("gdn_norm_w", HEAD), ("mla_out_norm_w", HEAD), ("a_log", HEAD), ("dt_bias", HEAD))
_SMALL_ROWS = sum(n for _, n in _SMALL) // HEAD
_CONV_ROWS = GDN_CONV * CONV_CH // HEAD
_PACK_ROWS = 160


def _pad_lanes(v, n):
    v = v.reshape(-1)
    return jnp.concatenate([v, jnp.zeros((n - v.shape[0],), v.dtype)])


def kernel(x, positions, attn_norm_w, w_in, conv_w, a_log, dt_bias, gdn_norm_w, q_norm_w, w_uq, kv_norm_w, w_ukv, mla_out_norm_w, w_out, ffn_norm_w, w_gate, w_up, w_down, final_norm_w, loss_target, m_attn_norm_w, m_w_in, m_conv_w, m_a_log, m_dt_bias, m_gdn_norm_w, m_q_norm_w, m_w_uq, m_kv_norm_w, m_w_ukv, m_mla_out_norm_w, m_w_out, m_ffn_norm_w, m_w_gate, m_w_up, m_w_down, m_final_norm_w, v_attn_norm_w, v_w_in, v_conv_w, v_a_log, v_dt_bias, v_gdn_norm_w, v_q_norm_w, v_w_uq, v_kv_norm_w, v_w_ukv, v_mla_out_norm_w, v_w_out, v_ffn_norm_w, v_w_gate, v_w_up, v_w_down, v_final_norm_w):
    t = x.shape[1]
    me = 4 * lax.axis_index("x") + 2 * lax.axis_index("y") + lax.axis_index("c")
    weights = dict(attn_norm_w=attn_norm_w, w_in=w_in, conv_w=conv_w, a_log=a_log, dt_bias=dt_bias, gdn_norm_w=gdn_norm_w,
                   q_norm_w=q_norm_w, w_uq=w_uq, kv_norm_w=kv_norm_w, w_ukv=w_ukv, mla_out_norm_w=mla_out_norm_w, w_out=w_out,
                   ffn_norm_w=ffn_norm_w, w_gate=w_gate, w_up=w_up, w_down=w_down, final_norm_w=final_norm_w)
    mom_m = dict(attn_norm_w=m_attn_norm_w, w_in=m_w_in, conv_w=m_conv_w, a_log=m_a_log, dt_bias=m_dt_bias, gdn_norm_w=m_gdn_norm_w,
                 q_norm_w=m_q_norm_w, w_uq=m_w_uq, kv_norm_w=m_kv_norm_w, w_ukv=m_w_ukv, mla_out_norm_w=m_mla_out_norm_w,
                 w_out=m_w_out, ffn_norm_w=m_ffn_norm_w, w_gate=m_w_gate, w_up=m_w_up, w_down=m_w_down, final_norm_w=m_final_norm_w)
    mom_v = dict(attn_norm_w=v_attn_norm_w, w_in=v_w_in, conv_w=v_conv_w, a_log=v_a_log, dt_bias=v_dt_bias, gdn_norm_w=v_gdn_norm_w,
                 q_norm_w=v_q_norm_w, w_uq=v_w_uq, kv_norm_w=v_kv_norm_w, w_ukv=v_w_ukv, mla_out_norm_w=v_mla_out_norm_w,
                 w_out=v_w_out, ffn_norm_w=v_ffn_norm_w, w_gate=v_w_gate, w_up=v_w_up, w_down=v_w_down, final_norm_w=v_final_norm_w)
    big_names = ("w_in", "w_uq", "w_ukv", "w_out", "w_gate", "w_up", "w_down")

    shard = {n: weights[n][0].astype(BF16) for n in big_names}
    g_in, g_uq, g_ukv, g_conv = _exchange(_Gather([shard["w_in"], shard["w_uq"], shard["w_ukv"], weights["conv_w"][0]]),
                                          name="gather_weights")
    win_p = _win_gathered_to_padded(g_in)
    wuq_p = _wuq_to_padded(_unshard_cols(g_uq))
    wukv = _unshard_cols(g_ukv)
    late = [shard["w_out"], shard["w_gate"], shard["w_up"], shard["w_down"]]
    conv_full = jnp.concatenate([_unshard_cols(g_conv), jnp.zeros((8 - GDN_CONV, CONV_CH), F32)], axis=0)

    gdn_params = jnp.concatenate([_pad_lanes(a_log, HEAD)[None], _pad_lanes(dt_bias, HEAD)[None], jnp.zeros((6, HEAD), F32)], axis=0)
    small = {n: weights[n].reshape(1, -1) for n in ("attn_norm_w", "ffn_norm_w", "final_norm_w", "q_norm_w", "kv_norm_w",
                                                    "gdn_norm_w", "mla_out_norm_w")}
    small["gdn_params"] = gdn_params

    dx, big, sm = _local_step(x[0], positions.reshape(t, 1).astype(F32), loss_target[0], win_p, wuq_p, wukv, late,
                              conv_full, small, True)

    rows8 = lambda name: jnp.sum(sm[name], axis=0)
    pieces = [rows8(n) for n, _ in _SMALL[:7]]
    pieces += [_pad_lanes(jnp.sum(sm["gdn_params"][0:1], axis=0), HEAD), _pad_lanes(jnp.sum(sm["gdn_params"][1:2], axis=0), HEAD)]
    pieces.append(jnp.sum(sm["conv_w"], axis=1).reshape(-1))
    pieces.append(_pad_lanes(jnp.sum(sm["loss"]).reshape(1), HEAD))
    packed = _pad_lanes(jnp.concatenate(pieces), _PACK_ROWS * HEAD).reshape(_PACK_ROWS, HEAD)
    (r_small,) = _exchange(_Exchange([packed], [True]), name="exchange_small")

    outs_g, outs_d, outs_m, outs_v = {}, {}, {}, {}
    for name in big_names:
        g, d, nm, nv = _adamw(big[name], weights[name][0], mom_m[name][0], mom_v[name][0], name="adamw_" + name)
        outs_g[name], outs_d[name], outs_m[name], outs_v[name] = g[None], d[None], nm[None], nv[None]

    total = _sum_parts(r_small, name="sum_small")
    flat = total.reshape(-1)
    loss = flat[(_SMALL_ROWS + _CONV_ROWS) * HEAD]
    g_small, off = {}, 0
    for n, size in _SMALL:
        g_small[n] = flat[off:off + size]
        off += size
    g_conv_full = flat[off:off + GDN_CONV * CONV_CH].reshape(GDN_CONV, CONV_CH)
    g_small["conv_w"] = lax.dynamic_slice(g_conv_full, (0, me * (CONV_CH // N_DEV)), (GDN_CONV, CONV_CH // N_DEV)).reshape(-1)
    order = [n for n, _ in _SMALL] + ["conv_w"]
    sizes = dict(_SMALL)
    sizes["conv_w"] = GDN_CONV * CONV_CH // N_DEV
    true_size = {n: weights[n].size for n in order}

    def pack(d):
        return jnp.concatenate([_pad_lanes(d[n], sizes[n]) for n in order]).reshape(1, -1, HEAD)

    g2, d2, m2, v2 = _adamw(pack(g_small), pack(weights)[0], pack(mom_m)[0], pack(mom_v)[0], name="adamw_small")
    off = 0
    for n in order:
        for src, dst in ((g2, outs_g), (d2, outs_d), (m2, outs_m), (v2, outs_v)):
            dst[n] = src.reshape(-1)[off:off + true_size[n]].reshape(weights[n].shape)
        off += sizes[n]

    names = ("attn_norm_w", "w_in", "conv_w", "a_log", "dt_bias", "gdn_norm_w", "q_norm_w", "w_uq", "kv_norm_w", "w_ukv",
             "mla_out_norm_w", "w_out", "ffn_norm_w", "w_gate", "w_up", "w_down", "final_norm_w")
    return (loss, dx[None], *[outs_g[n] for n in names], *[outs_d[n] for n in names], *[outs_m[n] for n in names],
            *[outs_v[n] for n in names])
```

```python
import functools
import math

import jax
import jax.numpy as jnp
from jax import lax
from jax.experimental import pallas as pl
from jax.experimental.pallas import tpu as pltpu

F32 = jnp.float32
BF16 = jnp.bfloat16

D_MODEL = 2048
GDN_HEADS = 8
HEAD = 128
GDN_CONV = 4
GDN_CHUNK = 64
GDN_QK = GDN_HEADS * HEAD
CONV_CH = 3 * GDN_QK
MLA_HEADS = 8
QK_ROPE = 64
Q_LORA = 512
KV_LORA = 512
ROPE_THETA = 10000.0
D_FF = 5632
EPS = 1e-6
IN_WIDTH = 5200
ADAM_LR, ADAM_B1, ADAM_B2, ADAM_EPS, ADAM_WD, ADAM_STEP = 0.001, 0.9, 0.999, 1e-08, 0.01, 10

PROJ_W = 5376
PROJ_BLK = PROJ_W // 3
COL_Z = 3072
COL_CQ = 4096
COL_CKV = 4608
COL_MISC = 5120
LANE_B = 64
LANE_A = 72
QHEAD = 256
FF_WIDE = D_FF // 4
N_DEV = 8
MESH = pl.DeviceIdType.MESH
VMEM_LIMIT_MB = 48

NN = ((1,), (0,))
NT = ((1,), (1,))
TN = ((0,), (0,))


def _my_place():
    x, y, c = lax.axis_index("x"), lax.axis_index("y"), lax.axis_index("c")
    return x, y, c, 4 * x + 2 * y + c


def _peer(x, y, c, p):
    px, py, pc = x ^ ((p >> 2) & 1), y ^ ((p >> 1) & 1), c ^ (p & 1)
    return (px, py, pc), 4 * px + 2 * py + pc


class _Exchange:
    def __init__(self, arrays, gather):
        self.arrays, self.gather, self.n = list(arrays), list(gather), len(arrays)

    def out_shape(self):
        return [jax.ShapeDtypeStruct(((N_DEV,) + a.shape) if g else a.shape, a.dtype)
                for a, g in zip(self.arrays, self.gather)]

    def sems(self):
        return [pltpu.SemaphoreType.DMA((self.n * (N_DEV - 1),)), pltpu.SemaphoreType.DMA((self.n * (N_DEV - 1),)),
                pltpu.SemaphoreType.DMA((self.n,))]

    def _copies(self, ins, outs, sems):
        send_sems, recv_sems, local_sems = sems
        x, y, c, me = _my_place()
        local = [pltpu.make_async_copy(ins[k] if self.gather[k] else ins[k].at[me], outs[k].at[me], local_sems.at[k])
                 for k in range(self.n)]
        sent, received = [], []
        for p in range(1, N_DEV):
            place, num = _peer(x, y, c, p)
            for k in range(self.n):
                src = ins[k] if self.gather[k] else ins[k].at[num]
                idx = k * (N_DEV - 1) + p - 1
                mk = lambda dst: pltpu.make_async_remote_copy(src_ref=src, dst_ref=dst, send_sem=send_sems.at[idx],
                                                              recv_sem=recv_sems.at[idx], device_id=place, device_id_type=MESH)
                sent.append(mk(outs[k].at[me]))
                received.append(mk(outs[k].at[num]))
        return local, sent, received

    def start(self, ins, outs, sems):
        local, sent, _ = self._copies(ins, outs, sems)
        for cp in local + sent:
            cp.start()

    def forward(self, ins, outs, sems):
        pass

    def finish(self, ins, outs, sems):
        local, sent, received = self._copies(ins, outs, sems)
        for cp in received:
            cp.wait_recv()
        for cp in sent:
            cp.wait_send()
        for cp in local:
            cp.wait()


class _Gather:
    def __init__(self, arrays):
        self.arrays, self.n = list(arrays), len(arrays)

    def out_shape(self):
        return [jax.ShapeDtypeStruct((N_DEV,) + a.shape, a.dtype) for a in self.arrays]

    def sems(self):
        return [pltpu.SemaphoreType.DMA((self.n * (N_DEV - 1),)), pltpu.SemaphoreType.DMA((self.n * (N_DEV - 1),)),
                pltpu.SemaphoreType.DMA((self.n,))]

    def _plan(self, ins, outs, sems):
        send_sems, recv_sems, local_sems = sems
        x, y, c, me = _my_place()
        sibling = (x, y, 1 - c)
        chips = [(1 - x, y), (x, 1 - y), (1 - x, 1 - y)]
        num = lambda px, py, pc: 4 * px + 2 * py + pc

        def copy(k, i, block, to, src=None):
            slot = outs[k].at[num(*block)]
            return pltpu.make_async_remote_copy(src_ref=slot if src is None else src, dst_ref=slot,
                                                send_sem=send_sems.at[k * (N_DEV - 1) + i],
                                                recv_sem=recv_sems.at[k * (N_DEV - 1) + i],
                                                device_id=to, device_id_type=MESH)

        local = [pltpu.make_async_copy(ins[k], outs[k].at[me], local_sems.at[k]) for k in range(self.n)]
        return (x, y, c), sibling, chips, copy, local

    def start(self, ins, outs, sems):
        me, sibling, chips, copy, local = self._plan(ins, outs, sems)
        for cp in local:
            cp.start()
        for k in range(self.n):
            copy(k, 0, me, sibling, src=ins[k]).start()
            for j, chip in enumerate(chips):
                copy(k, 1 + j, me, (*chip, me[2]), src=ins[k]).start()

    def forward(self, ins, outs, sems):
        me, sibling, chips, copy, _ = self._plan(ins, outs, sems)
        for j, chip in enumerate(chips):
            for k in range(self.n):
                copy(k, 1 + j, (*chip, me[2]), me).wait_recv()
                copy(k, 4 + j, (*chip, me[2]), sibling).start()

    def finish(self, ins, outs, sems):
        me, sibling, chips, copy, local = self._plan(ins, outs, sems)
        for k in range(self.n):
            copy(k, 0, sibling, me).wait_recv()
            for j, chip in enumerate(chips):
                copy(k, 4 + j, (*chip, 1 - me[2]), me).wait_recv()
        for k in range(self.n):
            copy(k, 0, me, sibling, src=ins[k]).wait_send()
            for j, chip in enumerate(chips):
                copy(k, 1 + j, me, (*chip, me[2]), src=ins[k]).wait_send()
                copy(k, 4 + j, (*chip, me[2]), sibling).wait_send()
        for cp in local:
            cp.wait()


def _pcall(body, *, name, grid, in_specs, out_specs, out_shape, scratch=(), carry=None):
    params = pltpu.CompilerParams(dimension_semantics=("arbitrary",) * len(grid), vmem_limit_bytes=VMEM_LIMIT_MB << 20)
    if carry is None:
        return pl.pallas_call(body, name=name, grid=grid, in_specs=in_specs, out_specs=out_specs, out_shape=out_shape,
                              scratch_shapes=list(scratch), compiler_params=params)
    single = not isinstance(out_specs, (list, tuple))
    out_specs = [out_specs] if single else list(out_specs)
    out_shape = [out_shape] if single else list(out_shape)
    n_in, n_out, n_scr, na = len(in_specs), len(out_specs), len(scratch), carry.n

    def wrapped(*refs):
        ins, cin = refs[:n_in], refs[n_in:n_in + na]
        outs, cout = refs[n_in + na:n_in + na + n_out], refs[n_in + na + n_out:n_in + 2 * na + n_out]
        scr, sems = refs[n_in + 2 * na + n_out:n_in + 2 * na + n_out + n_scr], refs[n_in + 2 * na + n_out + n_scr:]
        total = math.prod(grid)
        step = functools.reduce(lambda a, d: a * grid[d] + pl.program_id(d), range(len(grid)), 0)

        @pl.when(step == 0)
        def _():
            carry.start(cin, cout, sems)

        body(*ins, *outs, *scr)

        @pl.when(step == min(total * 7 // 8, total - 1))
        def _():
            carry.forward(cin, cout, sems)

        @pl.when(step == total - 1)
        def _():
            carry.finish(cin, cout, sems)

    any_spec = pl.BlockSpec(memory_space=pl.ANY)
    call = pl.pallas_call(wrapped, name=name, grid=grid, in_specs=list(in_specs) + [any_spec] * na,
                          out_specs=out_specs + [any_spec] * na, out_shape=out_shape + carry.out_shape(),
                          scratch_shapes=list(scratch) + carry.sems(), compiler_params=params)

    def run(*args):
        res = call(*args, *carry.arrays)
        main = res[0] if single else list(res[:n_out])
        return main, list(res[n_out:])

    return run


def _pick(dim, pref):
    if dim <= pref:
        return dim
    c = pref
    while c >= 128:
        if dim % c == 0 and c % 128 == 0:
            return c
        c -= 128
    return dim


def _rows(t, width, target_bytes=2 << 20):
    r = max(8, min(t, target_bytes // (4 * width)))
    r = 1 << (r.bit_length() - 1)
    while t % r:
        r //= 2
    return r


MM_FULL_K = 2048


def _mm(a, b, *, name, ta=False, tb=False, res=None, out_dtype=F32, bm=1024, bn=1024, bk=1024, carry=None, pair=None):
    m, k = (a.shape[1], a.shape[0]) if ta else a.shape
    n = b.shape[0] if tb else b.shape[1]
    assert (b.shape[1] if tb else b.shape[0]) == k
    bm, bn, bk = _pick(m, bm), _pick(n, bn), (k if k <= MM_FULL_K else _pick(k, bk))
    nk = k // bk
    dims = (((0,) if ta else (1,), (1,) if tb else (0,)), ((), ()))
    n_ab = 2 if pair is None else 4

    def body(*refs):
        a_ref, b_ref = refs[:2]
        r_ref = refs[n_ab] if res is not None else None
        o_ref = refs[n_ab + 1] if res is not None else refs[n_ab]
        part = lax.dot_general(a_ref[...].astype(BF16), b_ref[...].astype(BF16), dims, preferred_element_type=F32)
        if pair is not None:
            part = part + lax.dot_general(refs[2][...].astype(BF16), refs[3][...].astype(BF16), dims,
                                          preferred_element_type=F32)

        def finish(out):
            if res is not None:
                out = out + r_ref[...]
            o_ref[...] = out.astype(o_ref.dtype)

        if nk == 1:
            finish(part)
            return
        acc_ref = refs[-1]
        kk = pl.program_id(2)

        @pl.when(kk == 0)
        def _():
            acc_ref[...] = part

        @pl.when((kk > 0) & (kk < nk - 1))
        def _():
            acc_ref[...] += part

        @pl.when(kk == nk - 1)
        def _():
            finish(acc_ref[...] + part)

    a_spec = pl.BlockSpec((bk, bm), lambda i, j, kk: (kk, i)) if ta else pl.BlockSpec((bm, bk), lambda i, j, kk: (i, kk))
    b_spec = pl.BlockSpec((bn, bk), lambda i, j, kk: (j, kk)) if tb else pl.BlockSpec((bk, bn), lambda i, j, kk: (kk, j))
    o_spec = pl.BlockSpec((bm, bn), lambda i, j, kk: (i, j))
    ins, specs = [a, b], [a_spec, b_spec]
    if pair is not None:
        assert pair[0].shape == a.shape and pair[1].shape == b.shape
        ins += list(pair)
        specs += [a_spec, b_spec]
    if res is not None:
        ins.append(res)
        specs.append(o_spec)
    return _pcall(body, name=name, grid=(m // bm, n // bn, nk), in_specs=specs, out_specs=o_spec,
                  out_shape=jax.ShapeDtypeStruct((m, n), out_dtype),
                  scratch=[pltpu.VMEM((bm, bn), F32)] if nk > 1 else [], carry=carry)(*ins)


def _mm_dw2(a, b1, b2, *, name, bm=1024, bn=FF_WIDE, bk=1024):
    k, m = a.shape
    n = b1.shape[1]
    assert b1.shape == b2.shape == (k, n)
    bm, bn, bk = _pick(m, bm), _pick(n, bn), _pick(k, bk)
    nk = k // bk

    def body(a_ref, b1_ref, b2_ref, o1_ref, o2_ref, acc1_ref, acc2_ref):
        kk = pl.program_id(2)
        av = a_ref[...].astype(BF16)
        for b_ref, o_ref, acc_ref in ((b1_ref, o1_ref, acc1_ref), (b2_ref, o2_ref, acc2_ref)):
            part = lax.dot_general(av, b_ref[...].astype(BF16), (TN, ((), ())), preferred_element_type=F32)

            @pl.when(kk == 0)
            def _():
                acc_ref[...] = part

            @pl.when((kk > 0) & (kk < nk - 1))
            def _():
                acc_ref[...] += part

            @pl.when(kk == nk - 1)
            def _():
                o_ref[...] = (acc_ref[...] + part).astype(o_ref.dtype)

    b_spec = pl.BlockSpec((bk, bn), lambda i, j, kk: (kk, j))
    o_spec = pl.BlockSpec((bm, bn), lambda i, j, kk: (i, j))
    sds = jax.ShapeDtypeStruct((m, n), BF16)
    return _pcall(body, name=name, grid=(m // bm, n // bn, nk),
                  in_specs=[pl.BlockSpec((bk, bm), lambda i, j, kk: (kk, i)), b_spec, b_spec],
                  out_specs=[o_spec, o_spec], out_shape=[sds, sds],
                  scratch=[pltpu.VMEM((bm, bn), F32), pltpu.VMEM((bm, bn), F32)])(a, b1, b2)


def _rms_fwd(x, w, *, name, width, heads=1, col0=0, out_dtype=BF16, carry=None):
    t = x.shape[0]
    tm = _rows(t, width)
    cb = col0 // width

    def body(x_ref, w_ref, o_ref):
        xv = x_ref[...]
        r = lax.rsqrt(jnp.mean(xv * xv, axis=-1, keepdims=True) + EPS)
        o_ref[...] = (xv * r * w_ref[...]).astype(o_ref.dtype)

    return _pcall(body, name=name, grid=(t // tm, heads),
                  in_specs=[pl.BlockSpec((tm, width), lambda i, h: (i, cb + h)),
                            pl.BlockSpec((1, width), lambda i, h: (0, 0))],
                  out_specs=pl.BlockSpec((tm, width), lambda i, h: (i, h)),
                  out_shape=jax.ShapeDtypeStruct((t, heads * width), out_dtype), carry=carry)(x, w)


def _rms_bwd(x, w, dy, *, name, width, heads=1, col0=0, dcol0=0, res=None, out_dtype=F32, with_delta=False):
    t = x.shape[0]
    tm = _rows(t, width)
    cb, dcb = col0 // width, dcol0 // width

    def body(*refs):
        refs = list(refs)
        x_ref, w_ref, dy_ref = refs[:3]
        r_ref = refs[3] if res is not None else None
        outs = refs[4:] if res is not None else refs[3:]
        dx_ref, dw_ref = outs[:2]
        xv = x_ref[...]
        dyv = dy_ref[...].astype(F32)
        r = lax.rsqrt(jnp.mean(xv * xv, axis=-1, keepdims=True) + EPS)
        xh = xv * r
        dyw = dyv * w_ref[...]
        dx = r * (dyw - xh * jnp.mean(dyw * xh, axis=-1, keepdims=True))
        if with_delta:
            outs[2][...] = jnp.broadcast_to(jnp.sum(dx * xv, axis=-1, keepdims=True), dx.shape)
        if res is not None:
            dx = dx + r_ref[...]
        dx_ref[...] = dx.astype(dx_ref.dtype)

        @pl.when((pl.program_id(0) == 0) & (pl.program_id(1) == 0))
        def _():
            dw_ref[...] = jnp.zeros_like(dw_ref)

        dw_ref[...] += (dyv * xh).reshape(tm // 8, 8, width).sum(axis=0)

    blk = pl.BlockSpec((tm, width), lambda i, h: (i, h))
    ins = [x, w, dy]
    specs = [pl.BlockSpec((tm, width), lambda i, h: (i, cb + h)), pl.BlockSpec((1, width), lambda i, h: (0, 0)),
             pl.BlockSpec((tm, width), lambda i, h: (i, dcb + h))]
    if res is not None:
        ins.append(res)
        specs.append(blk)
    out_shape = [jax.ShapeDtypeStruct((t, heads * width), out_dtype), jax.ShapeDtypeStruct((8, width), F32)]
    out_specs = [blk, pl.BlockSpec((8, width), lambda i, h: (0, 0))]
    if with_delta:
        out_shape.append(jax.ShapeDtypeStruct((t, heads * width), F32))
        out_specs.append(blk)
    return _pcall(body, name=name, grid=(t // tm, heads), in_specs=specs, out_specs=out_specs, out_shape=out_shape)(*ins)


def _sig(x):
    return 1.0 / (1.0 + jnp.exp(-x))


@jax.custom_vjp
def _sigmoid(x):
    return _sig(x)


def _sigmoid_fwd(x):
    s = _sig(x)
    return s, s


def _sigmoid_bwd(s, g):
    return (g * s * (1.0 - s),)


_sigmoid.defvjp(_sigmoid_fwd, _sigmoid_bwd)


@jax.custom_vjp
def _softplus(x):
    return jnp.maximum(x, 0.0) + jnp.log(1.0 + jnp.exp(-jnp.abs(x)))


def _softplus_fwd(x):
    return _softplus(x), x


def _softplus_bwd(x, g):
    return (g * _sig(x),)


_softplus.defvjp(_softplus_fwd, _softplus_bwd)


def _silu(x):
    return x * _sig(x)


def _dsilu(x):
    s = _sig(x)
    return s * (1.0 + x * (1.0 - s))


NN3 = (((2,), (1,)), ((0,), (0,)))
NT3 = (((2,), (2,)), ((0,), (0,)))
TN3 = (((1,), (1,)), ((0,), (0,)))


def _bdot(a, b, dims):
    return lax.dot_general(a.astype(BF16), b.astype(BF16), dims, preferred_element_type=F32)


def _bf16_part(x):
    bits = lax.bitcast_convert_type(x, jnp.uint32) & jnp.uint32(0xFFFF0000)
    return lax.bitcast_convert_type(bits, F32)


def _scan_rows(x, reverse):
    c = x.shape[1]
    row = lax.broadcasted_iota(jnp.int32, x.shape, 1)
    step = 1
    while step < c:
        if reverse:
            x = x + jnp.where(row < c - step, pltpu.roll(x, c - step, axis=1), 0.0)
        else:
            x = x + jnp.where(row >= step, pltpu.roll(x, step, axis=1), 0.0)
        step *= 2
    return x


@jax.custom_vjp
def _prefix_rows(x):
    return _scan_rows(x, False)


_prefix_rows.defvjp(lambda x: (_scan_rows(x, False), None), lambda _, g: (_scan_rows(g, True),))


def _dot3(a, b, dims):
    (ca,), (cb,) = dims[0]
    a_hi, b_hi = _bf16_part(a), _bf16_part(b)
    a_lo, b_lo = (a - a_hi).astype(BF16), (b - b_hi).astype(BF16)
    a_hi, b_hi = a_hi.astype(BF16), b_hi.astype(BF16)
    return lax.dot_general(jnp.concatenate([a_hi, a_hi, a_lo], axis=ca), jnp.concatenate([b_hi, b_lo, b_hi], axis=cb),
                           dims, preferred_element_type=F32)


@jax.custom_vjp
def _nn_hi(a, b):
    return _dot3(a, b, NN3)


_nn_hi.defvjp(lambda a, b: (_dot3(a, b, NN3), (a, b)), lambda r, g: (_dot3(g, r[1], NT3), _dot3(r[0], g, TN3)))


@jax.custom_vjp
def _cat_lanes(a, b):
    return jnp.concatenate([a, b], axis=-1)


_cat_lanes.defvjp(lambda a, b: (jnp.concatenate([a, b], axis=-1), None), lambda _, g: (g[..., :HEAD], g[..., HEAD:]))


@jax.custom_vjp
def _split_lanes(x):
    return x[..., :HEAD], x[..., HEAD:]


_split_lanes.defvjp(lambda x: ((x[..., :HEAD], x[..., HEAD:]), None), lambda _, g: (jnp.concatenate(g, axis=-1),))


@jax.custom_vjp
def _nn(a, b):
    return _bdot(a, b, NN3)


_nn.defvjp(lambda a, b: (_bdot(a, b, NN3), (a, b)), lambda r, g: (_bdot(g, r[1], NT3), _bdot(r[0], g, TN3)))


@jax.custom_vjp
def _nt(a, b):
    return _bdot(a, b, NT3)


_nt.defvjp(lambda a, b: (_bdot(a, b, NT3), (a, b)), lambda r, g: (_bdot(g, r[1], NN3), _bdot(g, r[0], TN3)))


@jax.custom_vjp
def _tn(a, b):
    return _bdot(a, b, TN3)


_tn.defvjp(lambda a, b: (_bdot(a, b, TN3), (a, b)), lambda r, g: (_bdot(r[1], g, NT3), _bdot(r[0], g, NN3)))


CONV_ROWS, CONV_COLS = 256, 1024


def _rows_down(cur, prev8, s):
    r = pltpu.roll(cur, s, axis=0)
    rp = pltpu.roll(prev8, s, axis=0)
    row = lax.broadcasted_iota(jnp.int32, rp.shape, 0)
    head = jnp.where(row < s, rp, r[:8])
    return head if cur.shape[0] == 8 else jnp.concatenate([head, r[8:]], axis=0)


def _rows_up(cur, next8, s):
    n = cur.shape[0]
    r = pltpu.roll(cur, n - s, axis=0)
    rn = pltpu.roll(next8, 8 - s, axis=0)
    row = lax.broadcasted_iota(jnp.int32, rn.shape, 0)
    tail = jnp.where(row >= 8 - s, rn, r[n - 8:])
    return tail if n == 8 else jnp.concatenate([r[:n - 8], tail], axis=0)


def _conv_taps(cur, prev8):
    return [_rows_down(cur, prev8, GDN_CONV - 1 - j) for j in range(GDN_CONV - 1)] + [cur]


def _conv_pre(taps, w):
    acc = w[0:1] * taps[0]
    for j in range(1, GDN_CONV):
        acc = acc + w[j:j + 1] * taps[j]
    return acc


def _conv_fwd(proj, conv_w, *, name):
    t = proj.shape[0]
    tm, tc = _pick(t, CONV_ROWS), CONV_COLS
    nb = tm // 8

    def body(u_ref, p_ref, w_ref, o_ref):
        i = pl.program_id(1)
        prev = jnp.where(i > 0, p_ref[...], 0.0)
        o_ref[...] = _silu(_conv_pre(_conv_taps(u_ref[...], prev), w_ref[...]))

    return _pcall(body, name=name, grid=(CONV_CH // tc, t // tm),
                  in_specs=[pl.BlockSpec((tm, tc), lambda j, i: (i, j)),
                            pl.BlockSpec((8, tc), lambda j, i: (jnp.maximum(i * nb - 1, 0), j)),
                            pl.BlockSpec((8, tc), lambda j, i: (0, j))],
                  out_specs=pl.BlockSpec((tm, tc), lambda j, i: (i, j)),
                  out_shape=jax.ShapeDtypeStruct((t, CONV_CH), F32))(proj, proj, conv_w)


def _conv_bwd(proj, conv_w, dy, *, name):
    t = proj.shape[0]
    tm, tc = _pick(t, CONV_ROWS), CONV_COLS
    nb = tm // 8
    last = t // tm - 1

    def body(u_ref, p_ref, n_ref, dy_ref, dyn_ref, w_ref, du_ref, dw_ref):
        i = pl.program_id(1)
        w = w_ref[...]
        cur = u_ref[...]
        taps = _conv_taps(cur, jnp.where(i > 0, p_ref[...], 0.0))
        dc = dy_ref[...] * _dsilu(_conv_pre(taps, w))
        taps_next = _conv_taps(n_ref[...], cur[tm - 8:])
        dc_next = jnp.where(i < last, dyn_ref[...], 0.0) * _dsilu(_conv_pre(taps_next, w))
        du = w[3:4] * dc
        for j in range(GDN_CONV - 1):
            du = du + w[j:j + 1] * _rows_up(dc, dc_next, GDN_CONV - 1 - j)
        du_ref[...] = du.astype(du_ref.dtype)

        @pl.when(i == 0)
        def _():
            dw_ref[...] = jnp.zeros_like(dw_ref)

        for j in range(GDN_CONV):
            dw_ref[j] += (dc * taps[j]).reshape(nb, 8, tc).sum(axis=0)

    cur = lambda j, i: (i, j)
    return _pcall(body, name=name, grid=(CONV_CH // tc, t // tm),
                  in_specs=[pl.BlockSpec((tm, tc), cur),
                            pl.BlockSpec((8, tc), lambda j, i: (jnp.maximum(i * nb - 1, 0), j)),
                            pl.BlockSpec((8, tc), lambda j, i: (jnp.minimum((i + 1) * nb, t // 8 - 1), j)),
                            pl.BlockSpec((tm, tc), cur),
                            pl.BlockSpec((8, tc), lambda j, i: (jnp.minimum((i + 1) * nb, t // 8 - 1), j)),
                            pl.BlockSpec((8, tc), lambda j, i: (0, j))],
                  out_specs=[pl.BlockSpec((tm, tc), cur), pl.BlockSpec((GDN_CONV, 8, tc), lambda j, i: (0, 0, j))],
                  out_shape=[jax.ShapeDtypeStruct((t, CONV_CH), BF16), jax.ShapeDtypeStruct((GDN_CONV, 8, CONV_CH), F32)],
                  )(proj, proj, proj, dy, dy, conv_w)


def _gdn_chunk(q_raw, k_raw, v, misc, params, state):
    nh, c = q_raw.shape[0], q_raw.shape[1]
    lane = lax.broadcasted_iota(jnp.int32, misc.shape, 1)
    prow = lax.broadcasted_iota(jnp.int32, params.shape, 0)
    plane = lax.broadcasted_iota(jnp.int32, params.shape, 1)
    heads = lambda pieces: jnp.concatenate([p[None] for p in pieces], axis=0)
    col = lambda at: heads([jnp.sum(jnp.where(lane == at + h, misc, 0.0), axis=1, keepdims=True) for h in range(nh)])
    par = lambda row: heads([jnp.sum(jnp.where((prow == row) & (plane == h), params, 0.0), keepdims=True)
                             for h in range(nh)])
    b_raw, a_raw = col(LANE_B), col(LANE_A)
    a_log, dt_bias = par(0), par(1)
    beta = _sigmoid(b_raw)
    g = -jnp.exp(a_log) * _softplus(a_raw + dt_bias)

    q = q_raw * lax.rsqrt(jnp.sum(q_raw * q_raw, axis=-1, keepdims=True) + EPS) * (HEAD ** -0.5)
    k = k_raw * lax.rsqrt(jnp.sum(k_raw * k_raw, axis=-1, keepdims=True) + EPS)

    ri = lax.broadcasted_iota(jnp.int32, (c, c), 0)
    ci = lax.broadcasted_iota(jnp.int32, (c, c), 1)
    tril, strict = ri >= ci, ri > ci
    gc = _prefix_rows(g)
    gc_col = jnp.broadcast_to(gc, (nh, c, c))
    gc_row = jnp.swapaxes(gc_col, 1, 2)
    decay = jnp.exp(jnp.where(tril, gc_col - gc_row, -1e30))

    kb = k * beta
    vb = v * beta
    a_mat = jnp.where(strict, _nt(kb, k) * decay, 0.0)
    x = -a_mat
    inv = (ri == ci).astype(F32) + x
    for _ in range(5):
        x = _nn_hi(x, x)
        inv = inv + _nn_hi(inv, x)
    u, w = _split_lanes(_nn_hi(inv, _cat_lanes(vb, kb * jnp.exp(gc))))
    intra = _nt(q, k) * decay

    v_new = u - _nn(w, state)
    o = _nn(q * jnp.exp(gc), state) + _nn(intra, v_new)
    g_last = jnp.sum(g, axis=1, keepdims=True)
    k_dec = k * jnp.exp(g_last - gc)
    new_state = state * jnp.exp(g_last) + _tn(k_dec, v_new)
    return o, new_state


def _gdn_specs(nc, rev):
    cidx = (lambda n: nc - 1 - n) if rev else (lambda n: n)
    hb = lambda part: pl.BlockSpec((GDN_CHUNK, GDN_QK), lambda n: (cidx(n), part))
    misc = pl.BlockSpec((GDN_CHUNK, HEAD), lambda n: (cidx(n), COL_MISC // HEAD))
    params = pl.BlockSpec((8, HEAD), lambda n: (0, 0))
    hist = pl.BlockSpec((1, GDN_HEADS, HEAD, HEAD), lambda n: (cidx(n), 0, 0, 0))
    return hb, misc, params, hist


def _split_heads(v):
    return jnp.stack([v[:, h * HEAD:(h + 1) * HEAD] for h in range(v.shape[1] // HEAD)])


def _merge_heads(v):
    return jnp.concatenate([v[h] for h in range(v.shape[0])], axis=1)


def _gdn_fwd(qkv, proj, params, *, name):
    t = qkv.shape[0]
    nc = t // GDN_CHUNK
    hb, misc, pspec, hist = _gdn_specs(nc, False)

    def body(q_ref, k_ref, v_ref, m_ref, p_ref, o_ref, hist_ref, s_ref):
        @pl.when(pl.program_id(0) == 0)
        def _():
            s_ref[...] = jnp.zeros_like(s_ref)

        state = s_ref[...]
        hist_ref[0] = state
        o, new_state = _gdn_chunk(_split_heads(q_ref[...]), _split_heads(k_ref[...]), _split_heads(v_ref[...]),
                                  m_ref[...], p_ref[...], state)
        o_ref[...] = _merge_heads(o)
        s_ref[...] = new_state

    return _pcall(body, name=name, grid=(nc,),
                  in_specs=[hb(0), hb(1), hb(2), misc, pspec],
                  out_specs=[hb(0), hist],
                  out_shape=[jax.ShapeDtypeStruct((t, GDN_QK), F32),
                             jax.ShapeDtypeStruct((nc, GDN_HEADS, HEAD, HEAD), F32)],
                  scratch=[pltpu.VMEM((GDN_HEADS, HEAD, HEAD), F32)])(qkv, qkv, qkv, proj, params)


def _gdn_bwd(qkv, proj, params, hist_arr, do, dmisc_in, *, name, carry=None):
    t = qkv.shape[0]
    nc = t // GDN_CHUNK
    hb, misc, pspec, hist = _gdn_specs(nc, True)
    mrow = pl.BlockSpec((GDN_CHUNK, HEAD), lambda n: (nc - 1 - n, 0))

    def body(q_ref, k_ref, v_ref, m_ref, p_ref, hist_ref, do_ref, dmi_ref, dqkv_ref, dm_ref, dp_ref, ds_ref):
        @pl.when(pl.program_id(0) == 0)
        def _():
            ds_ref[...] = jnp.zeros_like(ds_ref)
            dp_ref[...] = jnp.zeros_like(dp_ref)

        _, vjp = jax.vjp(_gdn_chunk, _split_heads(q_ref[...]), _split_heads(k_ref[...]), _split_heads(v_ref[...]),
                         m_ref[...], p_ref[...], hist_ref[0])
        dq, dk, dv, dm, dp, ds = vjp((_split_heads(do_ref[...]), ds_ref[...]))
        dqkv_ref[:, 0:GDN_QK] = _merge_heads(dq)
        dqkv_ref[:, GDN_QK:2 * GDN_QK] = _merge_heads(dk)
        dqkv_ref[:, 2 * GDN_QK:] = _merge_heads(dv)
        ds_ref[...] = ds
        dm_ref[...] = dmi_ref[...] + dm
        dp_ref[...] += dp

    return _pcall(body, name=name, grid=(nc,),
                  in_specs=[hb(0), hb(1), hb(2), misc, pspec, hist, hb(0), mrow],
                  out_specs=[pl.BlockSpec((GDN_CHUNK, CONV_CH), lambda n: (nc - 1 - n, 0)), mrow, pspec],
                  out_shape=[jax.ShapeDtypeStruct((t, CONV_CH), F32), jax.ShapeDtypeStruct((t, HEAD), F32),
                             jax.ShapeDtypeStruct((8, HEAD), F32)],
                  scratch=[pltpu.VMEM((GDN_HEADS, HEAD, HEAD), F32)], carry=carry,
                  )(qkv, qkv, qkv, proj, params, hist_arr, do, dmisc_in)


def _gate_fwd(o_raw, proj, w, *, name):
    t = o_raw.shape[0]
    tm = _rows(t, HEAD)
    zb = COL_Z // HEAD

    def body(o_ref, z_ref, w_ref, out_ref):
        ov = o_ref[...]
        r = lax.rsqrt(jnp.mean(ov * ov, axis=-1, keepdims=True) + EPS)
        out_ref[...] = (ov * r * w_ref[...] * _silu(z_ref[...])).astype(out_ref.dtype)

    blk = pl.BlockSpec((tm, HEAD), lambda i, h: (i, h))
    return _pcall(body, name=name, grid=(t // tm, GDN_HEADS),
                  in_specs=[blk, pl.BlockSpec((tm, HEAD), lambda i, h: (i, zb + h)), pl.BlockSpec((1, HEAD), lambda i, h: (0, 0))],
                  out_specs=blk, out_shape=jax.ShapeDtypeStruct((t, GDN_QK), BF16))(o_raw, proj, w)


def _gate_bwd(o_raw, proj, w, dmixed, *, name):
    t = o_raw.shape[0]
    tm = _rows(t, HEAD)
    zb = COL_Z // HEAD

    def body(o_ref, z_ref, w_ref, dy_ref, do_ref, dz_ref, dw_ref):
        ov, zv, dyv = o_ref[...], z_ref[...], dy_ref[...]
        r = lax.rsqrt(jnp.mean(ov * ov, axis=-1, keepdims=True) + EPS)
        xh = ov * r
        dn = dyv * _silu(zv)
        dz_ref[...] = (dyv * xh * w_ref[...] * _dsilu(zv)).astype(dz_ref.dtype)
        dnw = dn * w_ref[...]
        do_ref[...] = r * (dnw - xh * jnp.mean(dnw * xh, axis=-1, keepdims=True))

        @pl.when((pl.program_id(0) == 0) & (pl.program_id(1) == 0))
        def _():
            dw_ref[...] = jnp.zeros_like(dw_ref)

        dw_ref[...] += (dn * xh).reshape(tm // 8, 8, HEAD).sum(axis=0)

    blk = pl.BlockSpec((tm, HEAD), lambda i, h: (i, h))
    return _pcall(body, name=name, grid=(t // tm, GDN_HEADS),
                  in_specs=[blk, pl.BlockSpec((tm, HEAD), lambda i, h: (i, zb + h)), pl.BlockSpec((1, HEAD), lambda i, h: (0, 0)), blk],
                  out_specs=[blk, blk, pl.BlockSpec((8, HEAD), lambda i, h: (0, 0))],
                  out_shape=[jax.ShapeDtypeStruct((t, GDN_QK), F32), jax.ShapeDtypeStruct((t, GDN_QK), BF16),
                             jax.ShapeDtypeStruct((8, HEAD), F32)])(o_raw, proj, w, dmixed)


def _rope_tables():
    half = QK_ROPE // 2
    inv = ROPE_THETA ** (-jnp.arange(half, dtype=F32) / half)
    zeros = jnp.zeros((HEAD - QK_ROPE,), F32)
    inv_row = jnp.concatenate([inv, inv, zeros])
    sign_row = jnp.concatenate([-jnp.ones((half,), F32), jnp.ones((half,), F32), zeros])
    mask_row = jnp.concatenate([jnp.ones((QK_ROPE,), F32), zeros])
    return jnp.concatenate([inv_row[None], sign_row[None], mask_row[None], jnp.zeros((5, HEAD), F32)], axis=0)


def _rope_cs(pos, tab, *, name):
    t = pos.shape[0]
    tm = _pick(t, 1024)

    def body(pos_ref, tab_ref, o_ref):
        tab = tab_ref[...]
        ang = pos_ref[...] * tab[0:1]
        o_ref[...] = jnp.concatenate([jnp.cos(ang) * tab[2:3], jnp.sin(ang) * tab[1:2]], axis=1)

    return _pcall(body, name=name, grid=(t // tm,),
                  in_specs=[pl.BlockSpec((tm, 1), lambda i: (i, 0)), pl.BlockSpec((8, HEAD), lambda i: (0, 0))],
                  out_specs=pl.BlockSpec((tm, 2 * HEAD), lambda i: (i, 0)),
                  out_shape=jax.ShapeDtypeStruct((t, 2 * HEAD), F32))(pos, tab)


def _rotate(x, cs, sign):
    lane = lax.broadcasted_iota(jnp.int32, x.shape, 1)
    half = QK_ROPE // 2
    partner = jnp.where(lane < half, pltpu.roll(x, HEAD - half, axis=1), pltpu.roll(x, half, axis=1))
    return x * cs[:, :HEAD] + partner * (cs[:, HEAD:] * sign)


def _q_rot(q, cs, *, name, sign, out_dtype=BF16):
    t = q.shape[0]
    tm = _pick(t, 1024)
    scale = (HEAD + QK_ROPE) ** -0.5

    def body(q_ref, cs_ref, o_ref):
        qv = q_ref[...].astype(F32)
        rot = _rotate(qv[:, HEAD:], cs_ref[...], sign)
        o_ref[...] = (jnp.concatenate([qv[:, :HEAD], rot], axis=1) * scale).astype(o_ref.dtype)

    blk = pl.BlockSpec((tm, QHEAD), lambda i, h: (i, h))
    return _pcall(body, name=name, grid=(t // tm, MLA_HEADS),
                  in_specs=[blk, pl.BlockSpec((tm, 2 * HEAD), lambda i, h: (i, 0))],
                  out_specs=blk, out_shape=jax.ShapeDtypeStruct((t, MLA_HEADS * QHEAD), out_dtype))(q, cs)


def _q_up(cqn, wuq_p, cs, *, name):
    t, lora = cqn.shape
    tm = _pick(t, 1024)
    scale = (HEAD + QK_ROPE) ** -0.5

    def body(a_ref, w_ref, cs_ref, o_ref):
        qv = lax.dot_general(a_ref[...], w_ref[...], (NN, ((), ())), preferred_element_type=F32)
        rot = _rotate(qv[:, HEAD:], cs_ref[...], 1.0)
        o_ref[...] = (jnp.concatenate([qv[:, :HEAD], rot], axis=1) * scale).astype(o_ref.dtype)

    return _pcall(body, name=name, grid=(t // tm, MLA_HEADS),
                  in_specs=[pl.BlockSpec((tm, lora), lambda i, h: (i, 0)), pl.BlockSpec((lora, QHEAD), lambda i, h: (0, h)),
                            pl.BlockSpec((tm, 2 * HEAD), lambda i, h: (i, 0))],
                  out_specs=pl.BlockSpec((tm, QHEAD), lambda i, h: (i, h)),
                  out_shape=jax.ShapeDtypeStruct((t, MLA_HEADS * QHEAD), BF16))(cqn, wuq_p, cs)


def _kv_up(ckvn, wukv, proj, cs, *, name):
    t, lora = ckvn.shape
    tm = _pick(t, 1024)

    def body(a_ref, w_ref, m_ref, cs_ref, k_ref, v_ref):
        kvv = lax.dot_general(a_ref[...], w_ref[...], (NN, ((), ())), preferred_element_type=F32)
        misc = m_ref[...]
        lane = lax.broadcasted_iota(jnp.int32, misc.shape, 1)
        rot = _rotate(jnp.where(lane < QK_ROPE, misc, 0.0), cs_ref[...], 1.0)
        k_ref[...] = jnp.concatenate([kvv[:, :HEAD], rot], axis=1).astype(k_ref.dtype)
        v_ref[...] = kvv[:, HEAD:].astype(v_ref.dtype)

    return _pcall(body, name=name, grid=(t // tm, MLA_HEADS),
                  in_specs=[pl.BlockSpec((tm, lora), lambda i, h: (i, 0)), pl.BlockSpec((lora, QHEAD), lambda i, h: (0, h)),
                            pl.BlockSpec((tm, HEAD), lambda i, h: (i, COL_MISC // HEAD)),
                            pl.BlockSpec((tm, 2 * HEAD), lambda i, h: (i, 0))],
                  out_specs=[pl.BlockSpec((tm, QHEAD), lambda i, h: (i, h)), pl.BlockSpec((tm, HEAD), lambda i, h: (i, h))],
                  out_shape=[jax.ShapeDtypeStruct((t, MLA_HEADS * QHEAD), BF16), jax.ShapeDtypeStruct((t, MLA_HEADS * HEAD), BF16)],
                  )(ckvn, wukv, proj, cs)


def _krope_bwd(dkr, cs, *, name):
    t = dkr.shape[0]
    tm = _pick(t, 512)

    def body(d_ref, cs_ref, o_ref):
        d = d_ref[...]
        acc = d[:, :HEAD]
        for h in range(1, MLA_HEADS):
            acc = acc + d[:, h * HEAD:(h + 1) * HEAD]
        o_ref[...] = _rotate(acc, cs_ref[...], -1.0)

    return _pcall(body, name=name, grid=(t // tm,),
                  in_specs=[pl.BlockSpec((tm, MLA_HEADS * HEAD), lambda i: (i, 0)), pl.BlockSpec((tm, 2 * HEAD), lambda i: (i, 0))],
                  out_specs=pl.BlockSpec((tm, HEAD), lambda i: (i, 0)),
                  out_shape=jax.ShapeDtypeStruct((t, HEAD), F32))(dkr, cs)


NEG = -1e30


def _tri(step, counts):
    starts = [sum(counts[:o]) for o in range(len(counts))]
    outer = sum([(step >= s).astype(jnp.int32) for s in starts[1:]], jnp.int32(0))
    start = sum([(step >= starts[o]).astype(jnp.int32) * (starts[o] - starts[o - 1]) for o in range(1, len(counts))], jnp.int32(0))
    return outer, step - start


def _attn_fwd(q, k, v, *, name, tq=1024, tk=1024, carry=None):
    t = q.shape[0]
    tq, tk = _pick(t, tq), _pick(t, tk)
    nq = t // tq
    last_kv = lambda i: (i * tq + tq - 1) // tk
    counts = [last_kv(i) + 1 for i in range(nq)]

    def body(q_ref, k_ref, v_ref, o_ref, lse_ref, m_ref, l_ref, acc_ref):
        i, j = _tri(pl.program_id(1), counts)

        @pl.when(j == 0)
        def _():
            m_ref[...] = jnp.full_like(m_ref, NEG)
            l_ref[...] = jnp.zeros_like(l_ref)
            acc_ref[...] = jnp.zeros_like(acc_ref)

        def step(masked):
            s = lax.dot_general(q_ref[...], k_ref[...], (NT, ((), ())), preferred_element_type=F32)
            if masked:
                qpos = i * tq + lax.broadcasted_iota(jnp.int32, s.shape, 0)
                kpos = j * tk + lax.broadcasted_iota(jnp.int32, s.shape, 1)
                s = jnp.where(kpos <= qpos, s, NEG)
            m_prev = m_ref[...]
            m_new = jnp.maximum(m_prev, jnp.max(s, axis=1, keepdims=True))
            alpha = jnp.exp(m_prev - m_new)
            p = jnp.exp(s - m_new)
            l_ref[...] = alpha * l_ref[...] + jnp.sum(p, axis=1, keepdims=True)
            acc_ref[...] = alpha * acc_ref[...] + lax.dot_general(p.astype(BF16), v_ref[...], (NN, ((), ())),
                                                                  preferred_element_type=F32)
            m_ref[...] = m_new

        crosses = j * tk + tk - 1 > i * tq

        @pl.when(crosses)
        def _():
            step(True)

        @pl.when(jnp.logical_not(crosses))
        def _():
            step(False)

        @pl.when(j == last_kv(i))
        def _():
            o_ref[...] = acc_ref[...] / l_ref[...]
            lse_ref[...] = jnp.broadcast_to(m_ref[...] + jnp.log(l_ref[...]), lse_ref.shape)

    qblk = pl.BlockSpec((tq, QHEAD), lambda h, s: (_tri(s, counts)[0], h))
    oblk = pl.BlockSpec((tq, HEAD), lambda h, s: (_tri(s, counts)[0], h))
    return _pcall(body, name=name, grid=(MLA_HEADS, sum(counts)),
                  in_specs=[qblk, pl.BlockSpec((tk, QHEAD), lambda h, s: (_tri(s, counts)[1], h)),
                            pl.BlockSpec((tk, HEAD), lambda h, s: (_tri(s, counts)[1], h))],
                  out_specs=[oblk, oblk],
                  out_shape=[jax.ShapeDtypeStruct((t, MLA_HEADS * HEAD), F32), jax.ShapeDtypeStruct((t, MLA_HEADS * HEAD), F32)],
                  scratch=[pltpu.VMEM((tq, 1), F32), pltpu.VMEM((tq, 1), F32), pltpu.VMEM((tq, HEAD), F32)],
                  carry=carry)(q, k, v)


def _attn_bwd(q, k, v, do, lse, delta, *, name, tq=1024, tk=1024, carry=None):
    t = q.shape[0]
    tq, tk = _pick(t, tq), _pick(t, tk)
    nq, nk = t // tq, t // tk
    first_q = lambda j: (j * tk) // tq
    counts = [nq - first_q(j) for j in range(nk)]

    def where(step):
        j, off = _tri(step, counts)
        return j, first_q(j) + off

    lanes = lambda col: jnp.tile(col, (1, tk // HEAD))

    def body(q_ref, k_ref, v_ref, do_ref, lse_ref, dl_ref, dq_ref, dkv_ref, dkr_ref, dk_acc, dv_acc):
        j, i = where(pl.program_id(1))

        @pl.when(i == first_q(j))
        def _():
            dk_acc[...] = jnp.zeros_like(dk_acc)
            dv_acc[...] = jnp.zeros_like(dv_acc)

        def step(masked):
            qv, kv_, dov = q_ref[...], k_ref[...], do_ref[...].astype(BF16)
            s = lax.dot_general(qv, kv_, (NT, ((), ())), preferred_element_type=F32)
            p = jnp.exp((s - lanes(lse_ref[...])).astype(BF16))
            if masked:
                qpos = i * tq + lax.broadcasted_iota(jnp.int32, s.shape, 0)
                kpos = j * tk + lax.broadcasted_iota(jnp.int32, s.shape, 1)
                p = jnp.where(kpos <= qpos, p, jnp.zeros_like(p))
            dv_acc[...] += lax.dot_general(p, dov, (TN, ((), ())), preferred_element_type=F32)
            dp = lax.dot_general(dov, v_ref[...], (NT, ((), ())), preferred_element_type=F32)
            ds = p * (dp - lanes(dl_ref[...])).astype(BF16)
            dk_acc[...] += lax.dot_general(ds, qv, (TN, ((), ())), preferred_element_type=F32)
            contrib = lax.dot_general(ds, kv_, (NN, ((), ())), preferred_element_type=F32)
            rows = pl.ds(pl.multiple_of(i * tq, tq), tq)

            @pl.when(j == 0)
            def _():
                dq_ref[rows, :] = contrib

            @pl.when(j > 0)
            def _():
                dq_ref[rows, :] += contrib

        crosses = j * tk + tk - 1 > i * tq

        @pl.when(crosses)
        def _():
            step(True)

        @pl.when(jnp.logical_not(crosses))
        def _():
            step(False)

        @pl.when(i == nq - 1)
        def _():
            dk = dk_acc[...]
            dkv_ref[...] = jnp.concatenate([dk[:, :HEAD], dv_acc[...]], axis=1).astype(dkv_ref.dtype)
            dkr_ref[...] = dk[:, HEAD:]

    qi = lambda h, s: (where(s)[1], h)
    kj = lambda h, s: (where(s)[0], h)
    return _pcall(body, name=name, grid=(MLA_HEADS, sum(counts)),
                  in_specs=[pl.BlockSpec((tq, QHEAD), qi), pl.BlockSpec((tk, QHEAD), kj), pl.BlockSpec((tk, HEAD), kj),
                            pl.BlockSpec((tq, HEAD), qi), pl.BlockSpec((tq, HEAD), qi), pl.BlockSpec((tq, HEAD), qi)],
                  out_specs=[pl.BlockSpec((t, QHEAD), lambda h, s: (0, h)), pl.BlockSpec((tk, QHEAD), kj),
                             pl.BlockSpec((tk, HEAD), kj)],
                  out_shape=[jax.ShapeDtypeStruct((t, MLA_HEADS * QHEAD), F32), jax.ShapeDtypeStruct((t, MLA_HEADS * QHEAD), BF16),
                             jax.ShapeDtypeStruct((t, MLA_HEADS * HEAD), F32)],
                  scratch=[pltpu.VMEM((tk, QHEAD), F32), pltpu.VMEM((tk, HEAD), F32)], carry=carry)(q, k, v, do, lse, delta)


def _ffn_up(h, wgate, wup, *, name, bm=512, bn=FF_WIDE):
    t = h.shape[0]
    bm = _pick(t, bm)

    def body(h_ref, wg_ref, wu_ref, g_ref, u_ref, a_ref):
        hv = h_ref[...]
        g = lax.dot_general(hv, wg_ref[...], (NN, ((), ())), preferred_element_type=F32)
        u = lax.dot_general(hv, wu_ref[...], (NN, ((), ())), preferred_element_type=F32)
        g_ref[...] = g.astype(g_ref.dtype)
        u_ref[...] = u.astype(u_ref.dtype)
        a_ref[...] = (_silu(g) * u).astype(a_ref.dtype)

    w_spec = pl.BlockSpec((D_MODEL, bn), lambda j, i: (0, j))
    o_spec = pl.BlockSpec((bm, bn), lambda j, i: (i, j))
    sds = jax.ShapeDtypeStruct((t, D_FF), BF16)
    return _pcall(body, name=name, grid=(D_FF // bn, t // bm),
                  in_specs=[pl.BlockSpec((bm, D_MODEL), lambda j, i: (i, 0)), w_spec, w_spec],
                  out_specs=[o_spec] * 3, out_shape=[sds] * 3)(h, wgate, wup)


def _ffn_down_dx(dy, wdown, gate, up, *, name, bm=512, bn=FF_WIDE):
    t = dy.shape[0]
    bm = _pick(t, bm)

    def body(dy_ref, w_ref, g_ref, u_ref, dg_ref, du_ref):
        d = lax.dot_general(dy_ref[...].astype(BF16), w_ref[...], (NT, ((), ())), preferred_element_type=F32)
        g = g_ref[...].astype(F32)
        dg_ref[...] = (d * u_ref[...].astype(F32) * _dsilu(g)).astype(dg_ref.dtype)
        du_ref[...] = (d * _silu(g)).astype(du_ref.dtype)

    o_spec = pl.BlockSpec((bm, bn), lambda j, i: (i, j))
    sds = jax.ShapeDtypeStruct((t, D_FF), BF16)
    return _pcall(body, name=name, grid=(D_FF // bn, t // bm),
                  in_specs=[pl.BlockSpec((bm, D_MODEL), lambda j, i: (i, 0)), pl.BlockSpec((bn, D_MODEL), lambda j, i: (j, 0)),
                            o_spec, o_spec],
                  out_specs=[o_spec, o_spec], out_shape=[sds, sds])(dy, wdown, gate, up)


def _loss_bwd(x2, w, target, *, name):
    t = x2.shape[0]
    tm = _rows(t, D_MODEL)

    def body(x_ref, w_ref, t_ref, dx_ref, dw_ref, l_ref):
        xv, wv = x_ref[...], w_ref[...]
        r = lax.rsqrt(jnp.mean(xv * xv, axis=-1, keepdims=True) + EPS)
        xh = xv * r
        err = xh * wv - t_ref[...]
        dy = err * (1.0 / D_MODEL)
        dyw = dy * wv
        dx_ref[...] = r * (dyw - xh * jnp.mean(dyw * xh, axis=-1, keepdims=True))

        @pl.when(pl.program_id(0) == 0)
        def _():
            dw_ref[...] = jnp.zeros_like(dw_ref)
            l_ref[...] = jnp.zeros_like(l_ref)

        dw_ref[...] += (dy * xh).reshape(tm // 8, 8, D_MODEL).sum(axis=0)
        sq = (err * err).reshape(tm // 8, 8, D_MODEL).sum(axis=0)
        part = sq[:, :HEAD]
        for c in range(1, D_MODEL // HEAD):
            part = part + sq[:, c * HEAD:(c + 1) * HEAD]
        l_ref[...] += part * (0.5 / D_MODEL)

    row = pl.BlockSpec((tm, D_MODEL), lambda i: (i, 0))
    return _pcall(body, name=name, grid=(t // tm,),
                  in_specs=[row, pl.BlockSpec((1, D_MODEL), lambda i: (0, 0)), row],
                  out_specs=[row, pl.BlockSpec((8, D_MODEL), lambda i: (0, 0)), pl.BlockSpec((8, HEAD), lambda i: (0, 0))],
                  out_shape=[jax.ShapeDtypeStruct((t, D_MODEL), F32), jax.ShapeDtypeStruct((8, D_MODEL), F32),
                             jax.ShapeDtypeStruct((8, HEAD), F32)])(x2, w, target)


def _unshard_cols(g):
    return jnp.transpose(g, (1, 0, 2)).reshape(g.shape[1], N_DEV * g.shape[2])


def _shard_cols(w):
    return jnp.transpose(w.reshape(w.shape[0], N_DEV, w.shape[1] // N_DEV), (1, 0, 2))


_WIN_ORDER = ((0, 4096), (4112, 5136), (5136, 5200), (4096, 4112))
_WIN_SHARD = IN_WIDTH // N_DEV


def _win_pieces():
    out, pos = [], 0
    for a, b in _WIN_ORDER:
        c = a
        while c < b:
            dev, off = divmod(c, _WIN_SHARD)
            width = min(b, (dev + 1) * _WIN_SHARD) - c
            out.append((dev, off, width, pos))
            c, pos = c + width, pos + width
    return out


def _win_gathered_to_padded(g):
    pieces = [g[dev][:, off:off + width] for dev, off, width, _ in _win_pieces()]
    return jnp.concatenate(pieces + [jnp.zeros((g.shape[1], PROJ_W - IN_WIDTH), g.dtype)], axis=1)


def _win_padded_to_shards(d):
    shards = []
    for dev in range(N_DEV):
        mine = sorted((off, width, pos) for dv, off, width, pos in _win_pieces() if dv == dev)
        shards.append(jnp.concatenate([d[:, pos:pos + width] for _, width, pos in mine], axis=1))
    return jnp.stack(shards)


def _wuq_to_padded(w):
    w3 = w.reshape(w.shape[0], MLA_HEADS, HEAD + QK_ROPE)
    return jnp.pad(w3, ((0, 0), (0, 0), (0, QHEAD - HEAD - QK_ROPE))).reshape(w.shape[0], MLA_HEADS * QHEAD)


def _wuq_from_padded(d):
    return d.reshape(d.shape[0], MLA_HEADS, QHEAD)[:, :, :HEAD + QK_ROPE].reshape(d.shape[0], MLA_HEADS * (HEAD + QK_ROPE))


def _late_weights(g_out, g_gate, g_up, g_down):
    return g_out.reshape(D_MODEL, D_MODEL), _unshard_cols(g_gate), _unshard_cols(g_up), g_down.reshape(D_FF, D_MODEL)


def _early_weights(g_in, g_uq, g_ukv, g_conv):
    conv = jnp.concatenate([_unshard_cols(g_conv), jnp.zeros((8 - GDN_CONV, CONV_CH), F32)], axis=0)
    return _win_gathered_to_padded(g_in), _wuq_to_padded(_unshard_cols(g_uq)), _unshard_cols(g_ukv), conv


def _local_step(x, pos, target, early, late, small, exchange):
    cs = _rope_cs(pos, _rope_tables(), name="rope_cs")
    if exchange:
        h1, gathered = _rms_fwd(x, small["attn_norm_w"], name="rms1_fwd", width=D_MODEL, carry=_Gather(early))
        win_p, wuq_p, wukv, conv_w = _early_weights(*gathered)
    else:
        h1 = _rms_fwd(x, small["attn_norm_w"], name="rms1_fwd", width=D_MODEL)
        win_p, wuq_p, wukv, conv_w = early
        wout, wgate, wup, wdown = late
    proj = _mm(h1, win_p, name="mm_in", bn=PROJ_BLK)
    qkv = _conv_fwd(proj, conv_w, name="conv_fwd")
    o_gdn_raw, hist = _gdn_fwd(qkv, proj, small["gdn_params"], name="gdn_fwd")
    o_gdn = _gate_fwd(o_gdn_raw, proj, small["gdn_norm_w"], name="gate_fwd")
    cqn = _rms_fwd(proj, small["q_norm_w"], name="rmsq_fwd", width=Q_LORA, col0=COL_CQ)
    ckvn = _rms_fwd(proj, small["kv_norm_w"], name="rmskv_fwd", width=KV_LORA, col0=COL_CKV)
    q_full = _q_up(cqn, wuq_p, cs, name="q_up")
    k_full, v_b = _kv_up(ckvn, wukv, proj, cs, name="kv_up")
    if exchange:
        (o_mla_raw, lse), gathered = _attn_fwd(q_full, k_full, v_b, name="attn_fwd", carry=_Gather(late))
        wout, wgate, wup, wdown = _late_weights(*gathered)
    else:
        o_mla_raw, lse = _attn_fwd(q_full, k_full, v_b, name="attn_fwd")
    o_mla = _rms_fwd(o_mla_raw, small["mla_out_norm_w"], name="rmso_fwd", width=HEAD, heads=MLA_HEADS)
    mixed = jnp.concatenate([o_gdn, o_mla], axis=1)
    x1 = _mm(mixed, wout, name="mm_out", res=x)
    h2 = _rms_fwd(x1, small["ffn_norm_w"], name="rms2_fwd", width=D_MODEL)
    gate, up, act = _ffn_up(h2, wgate, wup, name="ffn_up")
    x2 = _mm(act, wdown, name="mm_down", res=x1, bk=FF_WIDE)
    dx2, dw_final, loss_part = _loss_bwd(x2, small["final_norm_w"], target, name="loss_bwd")
    dgate, dup = _ffn_down_dx(dx2, wdown, gate, up, name="ffn_down_dx")
    d_wdown = _mm(act, dx2, name="mm_down_dw", ta=True, out_dtype=BF16, bm=FF_WIDE)
    dh2 = _mm(dgate, wgate, name="mm_gateup_dx", tb=True, bk=FF_WIDE, pair=(dup, wup))
    d_wgate, d_wup = _mm_dw2(h2, dgate, dup, name="mm_gateup_dw")
    dx1, dw_ffn = _rms_bwd(x1, small["ffn_norm_w"], dh2, name="rms2_bwd", width=D_MODEL, res=dx2)
    dmixed = _mm(dx1, wout, name="mm_out_dx", tb=True)
    d_wout = _mm(mixed, dx1, name="mm_out_dw", ta=True, out_dtype=BF16)
    do_mla, dw_mla_out, delta = _rms_bwd(o_mla_raw, small["mla_out_norm_w"], dmixed, name="rmso_bwd", width=HEAD,
                                         heads=MLA_HEADS, dcol0=GDN_QK, with_delta=True, out_dtype=BF16)
    if exchange:
        send = [d_wdown.reshape(N_DEV, D_FF // N_DEV, D_MODEL), _shard_cols(d_wgate), _shard_cols(d_wup)]
        (dq_full, dkv, dkr_h), (r_down, r_gate, r_up) = _attn_bwd(q_full, k_full, v_b, do_mla, lse, delta, name="attn_bwd",
                                                                  carry=_Exchange(send, [False] * 3))
    else:
        dq_full, dkv, dkr_h = _attn_bwd(q_full, k_full, v_b, do_mla, lse, delta, name="attn_bwd")
    dq_pre = _q_rot(dq_full, cs, name="q_rot_bwd", sign=-1.0)
    dmisc_kr = _krope_bwd(dkr_h, cs, name="krope_bwd")
    dcqn = _mm(dq_pre, wuq_p, name="mm_uq_dx", tb=True)
    d_wuq = _mm(cqn, dq_pre, name="mm_uq_dw", ta=True, out_dtype=BF16)
    dckvn = _mm(dkv, wukv, name="mm_ukv_dx", tb=True)
    d_wukv = _mm(ckvn, dkv, name="mm_ukv_dw", ta=True, out_dtype=BF16)
    dcq, dw_qn = _rms_bwd(proj, small["q_norm_w"], dcqn, name="rmsq_bwd", width=Q_LORA, col0=COL_CQ, out_dtype=BF16)
    dckv, dw_kvn = _rms_bwd(proj, small["kv_norm_w"], dckvn, name="rmskv_bwd", width=KV_LORA, col0=COL_CKV, out_dtype=BF16)
    do_gdn, dz, dw_gdn = _gate_bwd(o_gdn_raw, proj, small["gdn_norm_w"], dmixed, name="gate_bwd")
    if exchange:
        send = [d_wout.reshape(N_DEV, D_MODEL // N_DEV, D_MODEL), _shard_cols(_wuq_from_padded(d_wuq)), _shard_cols(d_wukv)]
        (dqkv, dmisc, d_params), (r_out, r_uq, r_ukv) = _gdn_bwd(
            qkv, proj, small["gdn_params"], hist, do_gdn, dmisc_kr, name="gdn_bwd", carry=_Exchange(send, [False] * 3))
    else:
        dqkv, dmisc, d_params = _gdn_bwd(qkv, proj, small["gdn_params"], hist, do_gdn, dmisc_kr, name="gdn_bwd")
    dqkv_pre, dconv = _conv_bwd(proj, conv_w, dqkv, name="conv_bwd")
    dproj = jnp.concatenate([dqkv_pre, dz, dcq, dckv, dmisc.astype(BF16), jnp.zeros((x.shape[0], PROJ_W - COL_MISC - HEAD), BF16)], axis=1)
    d_win = _mm(h1, dproj, name="mm_in_dw", ta=True, out_dtype=BF16, bn=PROJ_BLK)
    if exchange:
        dh1, (r_in,) = _mm(dproj, win_p, name="mm_in_dx", tb=True, bk=PROJ_BLK,
                           carry=_Exchange([_win_padded_to_shards(d_win)], [False]))
        d_win = r_in
    else:
        dh1 = _mm(dproj, win_p, name="mm_in_dx", tb=True, bk=PROJ_BLK)
    dx, dw_attn = _rms_bwd(x, small["attn_norm_w"], dh1, name="rms1_bwd", width=D_MODEL, res=dx1)

    if exchange:
        big = {"w_in": d_win, "w_uq": r_uq, "w_ukv": r_ukv, "w_out": r_out, "w_gate": r_gate, "w_up": r_up, "w_down": r_down}
    else:
        big = {"w_in": d_win, "w_uq": d_wuq, "w_ukv": d_wukv, "w_out": d_wout, "w_gate": d_wgate, "w_up": d_wup,
               "w_down": d_wdown}
    sm = {"attn_norm_w": dw_attn, "ffn_norm_w": dw_ffn, "final_norm_w": dw_final, "q_norm_w": dw_qn, "kv_norm_w": dw_kvn,
          "gdn_norm_w": dw_gdn, "mla_out_norm_w": dw_mla_out, "gdn_params": d_params, "conv_w": dconv, "loss": loss_part}
    return dx, big, sm


def _exchange(ex, *, name):
    def body(*refs):
        ins, outs, sems = refs[:ex.n], refs[ex.n:2 * ex.n], refs[2 * ex.n:]
        ex.start(ins, outs, sems)
        ex.forward(ins, outs, sems)
        ex.finish(ins, outs, sems)

    any_spec = pl.BlockSpec(memory_space=pl.ANY)
    return pl.pallas_call(body, name=name, in_specs=[any_spec] * ex.n, out_specs=[any_spec] * ex.n,
                          out_shape=ex.out_shape(), scratch_shapes=ex.sems())(*ex.arrays)


def _adamw_math(g, w, m, v):
    m = ADAM_B1 * m + (1.0 - ADAM_B1) * g
    v = ADAM_B2 * v + (1.0 - ADAM_B2) * (g * g)
    m_hat = m / (1.0 - ADAM_B1 ** ADAM_STEP)
    v_hat = v / (1.0 - ADAM_B2 ** ADAM_STEP)
    delta = -ADAM_LR * (m_hat / (jnp.sqrt(v_hat) + ADAM_EPS) + ADAM_WD * w)
    return delta, m, v


def _adamw(parts, w, m, v, *, name):
    npart, r, c = parts.shape
    tr = r if r * c * 4 <= (1 << 20) else _rows(r, c, 1 << 20)

    def body(p_ref, w_ref, m_ref, v_ref, g_ref, d_ref, nm_ref, nv_ref):
        g = p_ref[0].astype(F32)
        for s in range(1, npart):
            g = g + p_ref[s].astype(F32)
        g_ref[...] = g
        d_ref[...], nm_ref[...], nv_ref[...] = _adamw_math(g, w_ref[...], m_ref[...], v_ref[...])

    blk = pl.BlockSpec((tr, c), lambda i: (i, 0))
    sds = jax.ShapeDtypeStruct((r, c), F32)
    return _pcall(body, name=name, grid=(r // tr,),
                  in_specs=[pl.BlockSpec((npart, tr, c), lambda i: (0, i, 0)), blk, blk, blk],
                  out_specs=[blk] * 4, out_shape=[sds] * 4)(parts, w, m, v)


def _sum_parts(parts, *, name):
    npart, r, c = parts.shape

    def body(p_ref, o_ref):
        g = p_ref[0]
        for s in range(1, npart):
            g = g + p_ref[s]
        o_ref[...] = g

    return _pcall(body, name=name, grid=(1,), in_specs=[pl.BlockSpec((npart, r, c), lambda i: (0, 0, 0))],
                  out_specs=pl.BlockSpec((r, c), lambda i: (0, 0)), out_shape=jax.ShapeDtypeStruct((r, c), F32))(parts)


_SMALL = (("attn_norm_w", D_MODEL), ("ffn_norm_w", D_MODEL), ("final_norm_w", D_MODEL), ("q_norm_w", Q_LORA),
          ("kv_norm_w", KV_LORA), ("gdn_norm_w", HEAD), ("mla_out_norm_w", HEAD), ("a_log", HEAD), ("dt_bias", HEAD))
_SMALL_ROWS = sum(n for _, n in _SMALL) // HEAD
_CONV_ROWS = GDN_CONV * CONV_CH // HEAD
_PACK_ROWS = 160


def _pad_lanes(v, n):
    v = v.reshape(-1)
    return jnp.concatenate([v, jnp.zeros((n - v.shape[0],), v.dtype)])


def kernel(x, positions, attn_norm_w, w_in, conv_w, a_log, dt_bias, gdn_norm_w, q_norm_w, w_uq, kv_norm_w, w_ukv, mla_out_norm_w, w_out, ffn_norm_w, w_gate, w_up, w_down, final_norm_w, loss_target, m_attn_norm_w, m_w_in, m_conv_w, m_a_log, m_dt_bias, m_gdn_norm_w, m_q_norm_w, m_w_uq, m_kv_norm_w, m_w_ukv, m_mla_out_norm_w, m_w_out, m_ffn_norm_w, m_w_gate, m_w_up, m_w_down, m_final_norm_w, v_attn_norm_w, v_w_in, v_conv_w, v_a_log, v_dt_bias, v_gdn_norm_w, v_q_norm_w, v_w_uq, v_kv_norm_w, v_w_ukv, v_mla_out_norm_w, v_w_out, v_ffn_norm_w, v_w_gate, v_w_up, v_w_down, v_final_norm_w):
    t = x.shape[1]
    me = 4 * lax.axis_index("x") + 2 * lax.axis_index("y") + lax.axis_index("c")
    weights = dict(attn_norm_w=attn_norm_w, w_in=w_in, conv_w=conv_w, a_log=a_log, dt_bias=dt_bias, gdn_norm_w=gdn_norm_w,
                   q_norm_w=q_norm_w, w_uq=w_uq, kv_norm_w=kv_norm_w, w_ukv=w_ukv, mla_out_norm_w=mla_out_norm_w, w_out=w_out,
                   ffn_norm_w=ffn_norm_w, w_gate=w_gate, w_up=w_up, w_down=w_down, final_norm_w=final_norm_w)
    mom_m = dict(attn_norm_w=m_attn_norm_w, w_in=m_w_in, conv_w=m_conv_w, a_log=m_a_log, dt_bias=m_dt_bias, gdn_norm_w=m_gdn_norm_w,
                 q_norm_w=m_q_norm_w, w_uq=m_w_uq, kv_norm_w=m_kv_norm_w, w_ukv=m_w_ukv, mla_out_norm_w=m_mla_out_norm_w,
                 w_out=m_w_out, ffn_norm_w=m_ffn_norm_w, w_gate=m_w_gate, w_up=m_w_up, w_down=m_w_down, final_norm_w=m_final_norm_w)
    mom_v = dict(attn_norm_w=v_attn_norm_w, w_in=v_w_in, conv_w=v_conv_w, a_log=v_a_log, dt_bias=v_dt_bias, gdn_norm_w=v_gdn_norm_w,
                 q_norm_w=v_q_norm_w, w_uq=v_w_uq, kv_norm_w=v_kv_norm_w, w_ukv=v_w_ukv, mla_out_norm_w=v_mla_out_norm_w,
                 w_out=v_w_out, ffn_norm_w=v_ffn_norm_w, w_gate=v_w_gate, w_up=v_w_up, w_down=v_w_down, final_norm_w=v_final_norm_w)
    big_names = ("w_in", "w_uq", "w_ukv", "w_out", "w_gate", "w_up", "w_down")

    shard = {n: weights[n][0].astype(BF16) for n in big_names}
    early = [shard["w_in"], shard["w_uq"], shard["w_ukv"], weights["conv_w"][0]]
    late = [shard["w_out"], shard["w_gate"], shard["w_up"], shard["w_down"]]

    gdn_params = jnp.concatenate([_pad_lanes(a_log, HEAD)[None], _pad_lanes(dt_bias, HEAD)[None], jnp.zeros((6, HEAD), F32)], axis=0)
    small = {n: weights[n].reshape(1, -1) for n in ("attn_norm_w", "ffn_norm_w", "final_norm_w", "q_norm_w", "kv_norm_w",
                                                    "gdn_norm_w", "mla_out_norm_w")}
    small["gdn_params"] = gdn_params

    dx, big, sm = _local_step(x[0], positions.reshape(t, 1).astype(F32), loss_target[0], early, late, small, True)

    rows8 = lambda name: jnp.sum(sm[name], axis=0)
    pieces = [rows8(n) for n, _ in _SMALL[:7]]
    pieces += [_pad_lanes(jnp.sum(sm["gdn_params"][0:1], axis=0), HEAD), _pad_lanes(jnp.sum(sm["gdn_params"][1:2], axis=0), HEAD)]
    pieces.append(jnp.sum(sm["conv_w"], axis=1).reshape(-1))
    pieces.append(_pad_lanes(jnp.sum(sm["loss"]).reshape(1), HEAD))
    packed = _pad_lanes(jnp.concatenate(pieces), _PACK_ROWS * HEAD).reshape(_PACK_ROWS, HEAD)
    (r_small,) = _exchange(_Exchange([packed], [True]), name="exchange_small")

    outs_g, outs_d, outs_m, outs_v = {}, {}, {}, {}
    for name in big_names:
        g, d, nm, nv = _adamw(big[name], weights[name][0], mom_m[name][0], mom_v[name][0], name="adamw_" + name)
        outs_g[name], outs_d[name], outs_m[name], outs_v[name] = g[None], d[None], nm[None], nv[None]

    total = _sum_parts(r_small, name="sum_small")
    flat = total.reshape(-1)
    loss = flat[(_SMALL_ROWS + _CONV_ROWS) * HEAD]
    g_small, off = {}, 0
    for n, size in _SMALL:
        g_small[n] = flat[off:off + size]
        off += size
    g_conv_full = flat[off:off + GDN_CONV * CONV_CH].reshape(GDN_CONV, CONV_CH)
    g_small["conv_w"] = lax.dynamic_slice(g_conv_full, (0, me * (CONV_CH // N_DEV)), (GDN_CONV, CONV_CH // N_DEV)).reshape(-1)
    order = [n for n, _ in _SMALL] + ["conv_w"]
    sizes = dict(_SMALL)
    sizes["conv_w"] = GDN_CONV * CONV_CH // N_DEV
    true_size = {n: weights[n].size for n in order}

    def pack(d):
        return jnp.concatenate([_pad_lanes(d[n], sizes[n]) for n in order]).reshape(1, -1, HEAD)

    g2, d2, m2, v2 = _adamw(pack(g_small), pack(weights)[0], pack(mom_m)[0], pack(mom_v)[0], name="adamw_small")
    off = 0
    for n in order:
        for src, dst in ((g2, outs_g), (d2, outs_d), (m2, outs_m), (v2, outs_v)):
            dst[n] = src.reshape(-1)[off:off + true_size[n]].reshape(weights[n].shape)
        off += sizes[n]

    names = ("attn_norm_w", "w_in", "conv_w", "a_log", "dt_bias", "gdn_norm_w", "q_norm_w", "w_uq", "kv_norm_w", "w_ukv",
             "mla_out_norm_w", "w_out", "ffn_norm_w", "w_gate", "w_up", "w_down", "final_norm_w")
    return (loss, dx[None], *[outs_g[n] for n in names], *[outs_d[n] for n in names], *[outs_m[n] for n in names],
            *[outs_v[n] for n in names])
```

```python
import functools
import math

import jax
import jax.numpy as jnp
from jax import lax
from jax.experimental import pallas as pl
from jax.experimental.pallas import tpu as pltpu

F32 = jnp.float32
BF16 = jnp.bfloat16

D_MODEL = 2048
GDN_HEADS = 8
HEAD = 128
GDN_CONV = 4
GDN_CHUNK = 64
GDN_QK = GDN_HEADS * HEAD
CONV_CH = 3 * GDN_QK
MLA_HEADS = 8
QK_ROPE = 64
Q_LORA = 512
KV_LORA = 512
ROPE_THETA = 10000.0
D_FF = 5632
EPS = 1e-6
IN_WIDTH = 5200
ADAM_LR, ADAM_B1, ADAM_B2, ADAM_EPS, ADAM_WD, ADAM_STEP = 0.001, 0.9, 0.999, 1e-08, 0.01, 10

PROJ_W = 5376
PROJ_BLK = PROJ_W // 3
COL_Z = 3072
COL_CQ = 4096
COL_CKV = 4608
COL_MISC = 5120
LANE_B = 64
LANE_A = 72
QHEAD = 256
FF_WIDE = D_FF // 4
N_DEV = 8
MESH = pl.DeviceIdType.MESH
VMEM_LIMIT_MB = 48

NN = ((1,), (0,))
NT = ((1,), (1,))
TN = ((0,), (0,))


def _my_place():
    x, y, c = lax.axis_index("x"), lax.axis_index("y"), lax.axis_index("c")
    return x, y, c, 4 * x + 2 * y + c


def _peer(x, y, c, p):
    px, py, pc = x ^ ((p >> 2) & 1), y ^ ((p >> 1) & 1), c ^ (p & 1)
    return (px, py, pc), 4 * px + 2 * py + pc


class _Exchange:
    def __init__(self, arrays, gather):
        self.arrays, self.gather, self.n = list(arrays), list(gather), len(arrays)

    def out_shape(self):
        return [jax.ShapeDtypeStruct(((N_DEV,) + a.shape) if g else a.shape, a.dtype)
                for a, g in zip(self.arrays, self.gather)]

    def sems(self):
        return [pltpu.SemaphoreType.DMA((self.n * (N_DEV - 1),)), pltpu.SemaphoreType.DMA((self.n * (N_DEV - 1),)),
                pltpu.SemaphoreType.DMA((self.n,))]

    def _copies(self, ins, outs, sems):
        send_sems, recv_sems, local_sems = sems
        x, y, c, me = _my_place()
        local = [pltpu.make_async_copy(ins[k] if self.gather[k] else ins[k].at[me], outs[k].at[me], local_sems.at[k])
                 for k in range(self.n)]
        sent, received = [], []
        for p in range(1, N_DEV):
            place, num = _peer(x, y, c, p)
            for k in range(self.n):
                src = ins[k] if self.gather[k] else ins[k].at[num]
                idx = k * (N_DEV - 1) + p - 1
                mk = lambda dst: pltpu.make_async_remote_copy(src_ref=src, dst_ref=dst, send_sem=send_sems.at[idx],
                                                              recv_sem=recv_sems.at[idx], device_id=place, device_id_type=MESH)
                sent.append(mk(outs[k].at[me]))
                received.append(mk(outs[k].at[num]))
        return local, sent, received

    def start(self, ins, outs, sems):
        local, sent, _ = self._copies(ins, outs, sems)
        for cp in local + sent:
            cp.start()

    def forward(self, ins, outs, sems):
        pass

    def finish(self, ins, outs, sems):
        local, sent, received = self._copies(ins, outs, sems)
        for cp in received:
            cp.wait_recv()
        for cp in sent:
            cp.wait_send()
        for cp in local:
            cp.wait()


class _Gather:
    def __init__(self, arrays):
        self.arrays, self.n = list(arrays), len(arrays)

    def out_shape(self):
        return [jax.ShapeDtypeStruct((N_DEV,) + a.shape, a.dtype) for a in self.arrays]

    def sems(self):
        return [pltpu.SemaphoreType.DMA((self.n * (N_DEV - 1),)), pltpu.SemaphoreType.DMA((self.n * (N_DEV - 1),)),
                pltpu.SemaphoreType.DMA((self.n,))]

    def _plan(self, ins, outs, sems):
        send_sems, recv_sems, local_sems = sems
        x, y, c, me = _my_place()
        sibling = (x, y, 1 - c)
        chips = [(1 - x, y), (x, 1 - y), (1 - x, 1 - y)]
        num = lambda px, py, pc: 4 * px + 2 * py + pc

        def copy(k, i, block, to, src=None):
            slot = outs[k].at[num(*block)]
            return pltpu.make_async_remote_copy(src_ref=slot if src is None else src, dst_ref=slot,
                                                send_sem=send_sems.at[k * (N_DEV - 1) + i],
                                                recv_sem=recv_sems.at[k * (N_DEV - 1) + i],
                                                device_id=to, device_id_type=MESH)

        local = [pltpu.make_async_copy(ins[k], outs[k].at[me], local_sems.at[k]) for k in range(self.n)]
        return (x, y, c), sibling, chips, copy, local

    def start(self, ins, outs, sems):
        me, sibling, chips, copy, local = self._plan(ins, outs, sems)
        for cp in local:
            cp.start()
        for k in range(self.n):
            copy(k, 0, me, sibling, src=ins[k]).start()
            for j, chip in enumerate(chips):
                copy(k, 1 + j, me, (*chip, me[2]), src=ins[k]).start()

    def forward(self, ins, outs, sems):
        me, sibling, chips, copy, _ = self._plan(ins, outs, sems)
        for j, chip in enumerate(chips):
            for k in range(self.n):
                copy(k, 1 + j, (*chip, me[2]), me).wait_recv()
                copy(k, 4 + j, (*chip, me[2]), sibling).start()

    def finish(self, ins, outs, sems):
        me, sibling, chips, copy, local = self._plan(ins, outs, sems)
        for k in range(self.n):
            copy(k, 0, sibling, me).wait_recv()
            for j, chip in enumerate(chips):
                copy(k, 4 + j, (*chip, 1 - me[2]), me).wait_recv()
        for k in range(self.n):
            copy(k, 0, me, sibling, src=ins[k]).wait_send()
            for j, chip in enumerate(chips):
                copy(k, 1 + j, me, (*chip, me[2]), src=ins[k]).wait_send()
                copy(k, 4 + j, (*chip, me[2]), sibling).wait_send()
        for cp in local:
            cp.wait()


def _pcall(body, *, name, grid, in_specs, out_specs, out_shape, scratch=(), carry=None):
    params = pltpu.CompilerParams(dimension_semantics=("arbitrary",) * len(grid), vmem_limit_bytes=VMEM_LIMIT_MB << 20)
    if carry is None:
        return pl.pallas_call(body, name=name, grid=grid, in_specs=in_specs, out_specs=out_specs, out_shape=out_shape,
                              scratch_shapes=list(scratch), compiler_params=params)
    single = not isinstance(out_specs, (list, tuple))
    out_specs = [out_specs] if single else list(out_specs)
    out_shape = [out_shape] if single else list(out_shape)
    n_in, n_out, n_scr, na = len(in_specs), len(out_specs), len(scratch), carry.n

    def wrapped(*refs):
        ins, cin = refs[:n_in], refs[n_in:n_in + na]
        outs, cout = refs[n_in + na:n_in + na + n_out], refs[n_in + na + n_out:n_in + 2 * na + n_out]
        scr, sems = refs[n_in + 2 * na + n_out:n_in + 2 * na + n_out + n_scr], refs[n_in + 2 * na + n_out + n_scr:]
        total = math.prod(grid)
        step = functools.reduce(lambda a, d: a * grid[d] + pl.program_id(d), range(len(grid)), 0)

        @pl.when(step == 0)
        def _():
            carry.start(cin, cout, sems)

        body(*ins, *outs, *scr)

        @pl.when(step == min(total * 7 // 8, total - 1))
        def _():
            carry.forward(cin, cout, sems)

        @pl.when(step == total - 1)
        def _():
            carry.finish(cin, cout, sems)

    any_spec = pl.BlockSpec(memory_space=pl.ANY)
    call = pl.pallas_call(wrapped, name=name, grid=grid, in_specs=list(in_specs) + [any_spec] * na,
                          out_specs=out_specs + [any_spec] * na, out_shape=out_shape + carry.out_shape(),
                          scratch_shapes=list(scratch) + carry.sems(), compiler_params=params)

    def run(*args):
        res = call(*args, *carry.arrays)
        main = res[0] if single else list(res[:n_out])
        return main, list(res[n_out:])

    return run


def _pick(dim, pref):
    if dim <= pref:
        return dim
    c = pref
    while c >= 128:
        if dim % c == 0 and c % 128 == 0:
            return c
        c -= 128
    return dim


def _rows(t, width, target_bytes=2 << 20):
    r = max(8, min(t, target_bytes // (4 * width)))
    r = 1 << (r.bit_length() - 1)
    while t % r:
        r //= 2
    return r


MM_FULL_K = 2048


def _mm(a, b, *, name, ta=False, tb=False, res=None, out_dtype=F32, bm=1024, bn=1024, bk=1024, carry=None, pair=None):
    m, k = (a.shape[1], a.shape[0]) if ta else a.shape
    n = b.shape[0] if tb else b.shape[1]
    assert (b.shape[1] if tb else b.shape[0]) == k
    bm, bn, bk = _pick(m, bm), _pick(n, bn), (k if k <= MM_FULL_K else _pick(k, bk))
    nk = k // bk
    dims = (((0,) if ta else (1,), (1,) if tb else (0,)), ((), ()))
    n_ab = 2 if pair is None else 4

    def body(*refs):
        a_ref, b_ref = refs[:2]
        r_ref = refs[n_ab] if res is not None else None
        o_ref = refs[n_ab + 1] if res is not None else refs[n_ab]
        part = lax.dot_general(a_ref[...].astype(BF16), b_ref[...].astype(BF16), dims, preferred_element_type=F32)
        if pair is not None:
            part = part + lax.dot_general(refs[2][...].astype(BF16), refs[3][...].astype(BF16), dims,
                                          preferred_element_type=F32)

        def finish(out):
            if res is not None:
                out = out + r_ref[...]
            o_ref[...] = out.astype(o_ref.dtype)

        if nk == 1:
            finish(part)
            return
        acc_ref = refs[-1]
        kk = pl.program_id(2)

        @pl.when(kk == 0)
        def _():
            acc_ref[...] = part

        @pl.when((kk > 0) & (kk < nk - 1))
        def _():
            acc_ref[...] += part

        @pl.when(kk == nk - 1)
        def _():
            finish(acc_ref[...] + part)

    a_spec = pl.BlockSpec((bk, bm), lambda i, j, kk: (kk, i)) if ta else pl.BlockSpec((bm, bk), lambda i, j, kk: (i, kk))
    b_spec = pl.BlockSpec((bn, bk), lambda i, j, kk: (j, kk)) if tb else pl.BlockSpec((bk, bn), lambda i, j, kk: (kk, j))
    o_spec = pl.BlockSpec((bm, bn), lambda i, j, kk: (i, j))
    ins, specs = [a, b], [a_spec, b_spec]
    if pair is not None:
        assert pair[0].shape == a.shape and pair[1].shape == b.shape
        ins += list(pair)
        specs += [a_spec, b_spec]
    if res is not None:
        ins.append(res)
        specs.append(o_spec)
    return _pcall(body, name=name, grid=(m // bm, n // bn, nk), in_specs=specs, out_specs=o_spec,
                  out_shape=jax.ShapeDtypeStruct((m, n), out_dtype),
                  scratch=[pltpu.VMEM((bm, bn), F32)] if nk > 1 else [], carry=carry)(*ins)


def _mm_dw2(a, b1, b2, *, name, bm=1024, bn=FF_WIDE, bk=1024):
    k, m = a.shape
    n = b1.shape[1]
    assert b1.shape == b2.shape == (k, n)
    bm, bn, bk = _pick(m, bm), _pick(n, bn), _pick(k, bk)
    nk = k // bk

    def body(a_ref, b1_ref, b2_ref, o1_ref, o2_ref, acc1_ref, acc2_ref):
        kk = pl.program_id(2)
        av = a_ref[...].astype(BF16)
        for b_ref, o_ref, acc_ref in ((b1_ref, o1_ref, acc1_ref), (b2_ref, o2_ref, acc2_ref)):
            part = lax.dot_general(av, b_ref[...].astype(BF16), (TN, ((), ())), preferred_element_type=F32)
            if nk == 1:
                o_ref[...] = part.astype(o_ref.dtype)
                continue

            @pl.when(kk == 0)
            def _():
                acc_ref[...] = part

            @pl.when((kk > 0) & (kk < nk - 1))
            def _():
                acc_ref[...] += part

            @pl.when(kk == nk - 1)
            def _():
                o_ref[...] = (acc_ref[...] + part).astype(o_ref.dtype)

    b_spec = pl.BlockSpec((bk, bn), lambda i, j, kk: (kk, j))
    o_spec = pl.BlockSpec((bm, bn), lambda i, j, kk: (i, j))
    sds = jax.ShapeDtypeStruct((m, n), BF16)
    return _pcall(body, name=name, grid=(m // bm, n // bn, nk),
                  in_specs=[pl.BlockSpec((bk, bm), lambda i, j, kk: (kk, i)), b_spec, b_spec],
                  out_specs=[o_spec, o_spec], out_shape=[sds, sds],
                  scratch=[pltpu.VMEM((bm, bn), F32), pltpu.VMEM((bm, bn), F32)])(a, b1, b2)


def _rms_fwd(x, w, *, name, width, heads=1, col0=0, out_dtype=BF16, carry=None):
    t = x.shape[0]
    tm = _rows(t, width)
    cb = col0 // width

    def body(x_ref, w_ref, o_ref):
        xv = x_ref[...]
        r = lax.rsqrt(jnp.mean(xv * xv, axis=-1, keepdims=True) + EPS)
        o_ref[...] = (xv * r * w_ref[...]).astype(o_ref.dtype)

    return _pcall(body, name=name, grid=(t // tm, heads),
                  in_specs=[pl.BlockSpec((tm, width), lambda i, h: (i, cb + h)),
                            pl.BlockSpec((1, width), lambda i, h: (0, 0))],
                  out_specs=pl.BlockSpec((tm, width), lambda i, h: (i, h)),
                  out_shape=jax.ShapeDtypeStruct((t, heads * width), out_dtype), carry=carry)(x, w)


def _rms_bwd(x, w, dy, *, name, width, heads=1, col0=0, dcol0=0, res=None, out_dtype=F32, with_delta=False):
    t = x.shape[0]
    tm = _rows(t, width)
    cb, dcb = col0 // width, dcol0 // width

    def body(*refs):
        refs = list(refs)
        x_ref, w_ref, dy_ref = refs[:3]
        r_ref = refs[3] if res is not None else None
        outs = refs[4:] if res is not None else refs[3:]
        dx_ref, dw_ref = outs[:2]
        xv = x_ref[...]
        dyv = dy_ref[...].astype(F32)
        r = lax.rsqrt(jnp.mean(xv * xv, axis=-1, keepdims=True) + EPS)
        xh = xv * r
        dyw = dyv * w_ref[...]
        dx = r * (dyw - xh * jnp.mean(dyw * xh, axis=-1, keepdims=True))
        if with_delta:
            outs[2][...] = jnp.broadcast_to(jnp.sum(dx * xv, axis=-1, keepdims=True), dx.shape)
        if res is not None:
            dx = dx + r_ref[...]
        dx_ref[...] = dx.astype(dx_ref.dtype)

        @pl.when((pl.program_id(0) == 0) & (pl.program_id(1) == 0))
        def _():
            dw_ref[...] = jnp.zeros_like(dw_ref)

        dw_ref[...] += (dyv * xh).reshape(tm // 8, 8, width).sum(axis=0)

    blk = pl.BlockSpec((tm, width), lambda i, h: (i, h))
    ins = [x, w, dy]
    specs = [pl.BlockSpec((tm, width), lambda i, h: (i, cb + h)), pl.BlockSpec((1, width), lambda i, h: (0, 0)),
             pl.BlockSpec((tm, width), lambda i, h: (i, dcb + h))]
    if res is not None:
        ins.append(res)
        specs.append(blk)
    out_shape = [jax.ShapeDtypeStruct((t, heads * width), out_dtype), jax.ShapeDtypeStruct((8, width), F32)]
    out_specs = [blk, pl.BlockSpec((8, width), lambda i, h: (0, 0))]
    if with_delta:
        out_shape.append(jax.ShapeDtypeStruct((t, heads * width), F32))
        out_specs.append(blk)
    return _pcall(body, name=name, grid=(t // tm, heads), in_specs=specs, out_specs=out_specs, out_shape=out_shape)(*ins)


def _sig(x):
    return 1.0 / (1.0 + jnp.exp(-x))


@jax.custom_vjp
def _sigmoid(x):
    return _sig(x)


def _sigmoid_fwd(x):
    s = _sig(x)
    return s, s


def _sigmoid_bwd(s, g):
    return (g * s * (1.0 - s),)


_sigmoid.defvjp(_sigmoid_fwd, _sigmoid_bwd)


@jax.custom_vjp
def _softplus(x):
    return jnp.maximum(x, 0.0) + jnp.log(1.0 + jnp.exp(-jnp.abs(x)))


def _softplus_fwd(x):
    return _softplus(x), x


def _softplus_bwd(x, g):
    return (g * _sig(x),)


_softplus.defvjp(_softplus_fwd, _softplus_bwd)


def _silu(x):
    return x * _sig(x)


def _dsilu(x):
    s = _sig(x)
    return s * (1.0 + x * (1.0 - s))


NN3 = (((2,), (1,)), ((0,), (0,)))
NT3 = (((2,), (2,)), ((0,), (0,)))
TN3 = (((1,), (1,)), ((0,), (0,)))


def _bdot(a, b, dims):
    return lax.dot_general(a.astype(BF16), b.astype(BF16), dims, preferred_element_type=F32)


def _bf16_part(x):
    bits = lax.bitcast_convert_type(x, jnp.uint32) & jnp.uint32(0xFFFF0000)
    return lax.bitcast_convert_type(bits, F32)


def _scan_rows(x, reverse):
    c = x.shape[1]
    row = lax.broadcasted_iota(jnp.int32, x.shape, 1)
    step = 1
    while step < c:
        if reverse:
            x = x + jnp.where(row < c - step, pltpu.roll(x, c - step, axis=1), 0.0)
        else:
            x = x + jnp.where(row >= step, pltpu.roll(x, step, axis=1), 0.0)
        step *= 2
    return x


@jax.custom_vjp
def _prefix_rows(x):
    return _scan_rows(x, False)


_prefix_rows.defvjp(lambda x: (_scan_rows(x, False), None), lambda _, g: (_scan_rows(g, True),))


def _dot3(a, b, dims):
    (ca,), (cb,) = dims[0]
    a_hi, b_hi = _bf16_part(a), _bf16_part(b)
    a_lo, b_lo = (a - a_hi).astype(BF16), (b - b_hi).astype(BF16)
    a_hi, b_hi = a_hi.astype(BF16), b_hi.astype(BF16)
    return lax.dot_general(jnp.concatenate([a_hi, a_hi, a_lo], axis=ca), jnp.concatenate([b_hi, b_lo, b_hi], axis=cb),
                           dims, preferred_element_type=F32)


@jax.custom_vjp
def _nn_hi(a, b):
    return _dot3(a, b, NN3)


_nn_hi.defvjp(lambda a, b: (_dot3(a, b, NN3), (a, b)), lambda r, g: (_dot3(g, r[1], NT3), _dot3(r[0], g, TN3)))


@jax.custom_vjp
def _cat_lanes(a, b):
    return jnp.concatenate([a, b], axis=-1)


_cat_lanes.defvjp(lambda a, b: (jnp.concatenate([a, b], axis=-1), None), lambda _, g: (g[..., :HEAD], g[..., HEAD:]))


@jax.custom_vjp
def _split_lanes(x):
    return x[..., :HEAD], x[..., HEAD:]


_split_lanes.defvjp(lambda x: ((x[..., :HEAD], x[..., HEAD:]), None), lambda _, g: (jnp.concatenate(g, axis=-1),))


@jax.custom_vjp
def _nn(a, b):
    return _bdot(a, b, NN3)


_nn.defvjp(lambda a, b: (_bdot(a, b, NN3), (a, b)), lambda r, g: (_bdot(g, r[1], NT3), _bdot(r[0], g, TN3)))


@jax.custom_vjp
def _nt(a, b):
    return _bdot(a, b, NT3)


_nt.defvjp(lambda a, b: (_bdot(a, b, NT3), (a, b)), lambda r, g: (_bdot(g, r[1], NN3), _bdot(g, r[0], TN3)))


@jax.custom_vjp
def _tn(a, b):
    return _bdot(a, b, TN3)


_tn.defvjp(lambda a, b: (_bdot(a, b, TN3), (a, b)), lambda r, g: (_bdot(r[1], g, NT3), _bdot(r[0], g, NN3)))


CONV_ROWS, CONV_COLS = 256, 1024


def _rows_down(cur, prev8, s):
    r = pltpu.roll(cur, s, axis=0)
    rp = pltpu.roll(prev8, s, axis=0)
    row = lax.broadcasted_iota(jnp.int32, rp.shape, 0)
    head = jnp.where(row < s, rp, r[:8])
    return head if cur.shape[0] == 8 else jnp.concatenate([head, r[8:]], axis=0)


def _rows_up(cur, next8, s):
    n = cur.shape[0]
    r = pltpu.roll(cur, n - s, axis=0)
    rn = pltpu.roll(next8, 8 - s, axis=0)
    row = lax.broadcasted_iota(jnp.int32, rn.shape, 0)
    tail = jnp.where(row >= 8 - s, rn, r[n - 8:])
    return tail if n == 8 else jnp.concatenate([r[:n - 8], tail], axis=0)


def _conv_taps(cur, prev8):
    return [_rows_down(cur, prev8, GDN_CONV - 1 - j) for j in range(GDN_CONV - 1)] + [cur]


def _conv_pre(taps, w):
    acc = w[0:1] * taps[0]
    for j in range(1, GDN_CONV):
        acc = acc + w[j:j + 1] * taps[j]
    return acc


def _conv_fwd(proj, conv_w, *, name):
    t = proj.shape[0]
    tm, tc = _pick(t, CONV_ROWS), CONV_COLS
    nb = tm // 8

    def body(u_ref, p_ref, w_ref, o_ref):
        i = pl.program_id(1)
        prev = jnp.where(i > 0, p_ref[...], 0.0)
        o_ref[...] = _silu(_conv_pre(_conv_taps(u_ref[...], prev), w_ref[...]))

    return _pcall(body, name=name, grid=(CONV_CH // tc, t // tm),
                  in_specs=[pl.BlockSpec((tm, tc), lambda j, i: (i, j)),
                            pl.BlockSpec((8, tc), lambda j, i: (jnp.maximum(i * nb - 1, 0), j)),
                            pl.BlockSpec((8, tc), lambda j, i: (0, j))],
                  out_specs=pl.BlockSpec((tm, tc), lambda j, i: (i, j)),
                  out_shape=jax.ShapeDtypeStruct((t, CONV_CH), F32))(proj, proj, conv_w)


def _conv_bwd(proj, conv_w, dy, *, name):
    t = proj.shape[0]
    tm, tc = _pick(t, CONV_ROWS), CONV_COLS
    nb = tm // 8
    last = t // tm - 1

    def body(u_ref, p_ref, n_ref, dy_ref, dyn_ref, w_ref, du_ref, dw_ref):
        i = pl.program_id(1)
        w = w_ref[...]
        cur = u_ref[...]
        taps = _conv_taps(cur, jnp.where(i > 0, p_ref[...], 0.0))
        dc = dy_ref[...] * _dsilu(_conv_pre(taps, w))
        taps_next = _conv_taps(n_ref[...], cur[tm - 8:])
        dc_next = jnp.where(i < last, dyn_ref[...], 0.0) * _dsilu(_conv_pre(taps_next, w))
        du = w[3:4] * dc
        for j in range(GDN_CONV - 1):
            du = du + w[j:j + 1] * _rows_up(dc, dc_next, GDN_CONV - 1 - j)
        du_ref[...] = du.astype(du_ref.dtype)

        @pl.when(i == 0)
        def _():
            dw_ref[...] = jnp.zeros_like(dw_ref)

        for j in range(GDN_CONV):
            dw_ref[j] += (dc * taps[j]).reshape(nb, 8, tc).sum(axis=0)

    cur = lambda j, i: (i, j)
    return _pcall(body, name=name, grid=(CONV_CH // tc, t // tm),
                  in_specs=[pl.BlockSpec((tm, tc), cur),
                            pl.BlockSpec((8, tc), lambda j, i: (jnp.maximum(i * nb - 1, 0), j)),
                            pl.BlockSpec((8, tc), lambda j, i: (jnp.minimum((i + 1) * nb, t // 8 - 1), j)),
                            pl.BlockSpec((tm, tc), cur),
                            pl.BlockSpec((8, tc), lambda j, i: (jnp.minimum((i + 1) * nb, t // 8 - 1), j)),
                            pl.BlockSpec((8, tc), lambda j, i: (0, j))],
                  out_specs=[pl.BlockSpec((tm, tc), cur), pl.BlockSpec((GDN_CONV, 8, tc), lambda j, i: (0, 0, j))],
                  out_shape=[jax.ShapeDtypeStruct((t, CONV_CH), BF16), jax.ShapeDtypeStruct((GDN_CONV, 8, CONV_CH), F32)],
                  )(proj, proj, proj, dy, dy, conv_w)


def _gdn_chunk(q_raw, k_raw, v, misc, params, state):
    nh, c = q_raw.shape[0], q_raw.shape[1]
    lane = lax.broadcasted_iota(jnp.int32, misc.shape, 1)
    prow = lax.broadcasted_iota(jnp.int32, params.shape, 0)
    plane = lax.broadcasted_iota(jnp.int32, params.shape, 1)
    heads = lambda pieces: jnp.concatenate([p[None] for p in pieces], axis=0)
    col = lambda at: heads([jnp.sum(jnp.where(lane == at + h, misc, 0.0), axis=1, keepdims=True) for h in range(nh)])
    par = lambda row: heads([jnp.sum(jnp.where((prow == row) & (plane == h), params, 0.0), keepdims=True)
                             for h in range(nh)])
    b_raw, a_raw = col(LANE_B), col(LANE_A)
    a_log, dt_bias = par(0), par(1)
    beta = _sigmoid(b_raw)
    g = -jnp.exp(a_log) * _softplus(a_raw + dt_bias)

    q = q_raw * lax.rsqrt(jnp.sum(q_raw * q_raw, axis=-1, keepdims=True) + EPS) * (HEAD ** -0.5)
    k = k_raw * lax.rsqrt(jnp.sum(k_raw * k_raw, axis=-1, keepdims=True) + EPS)

    ri = lax.broadcasted_iota(jnp.int32, (c, c), 0)
    ci = lax.broadcasted_iota(jnp.int32, (c, c), 1)
    tril, strict = ri >= ci, ri > ci
    gc = _prefix_rows(g)
    gc_col = jnp.broadcast_to(gc, (nh, c, c))
    gc_row = jnp.swapaxes(gc_col, 1, 2)
    decay = jnp.exp(jnp.where(tril, gc_col - gc_row, -1e30))

    kb = k * beta
    vb = v * beta
    a_mat = jnp.where(strict, _nt(kb, k) * decay, 0.0)
    x = -a_mat
    inv = (ri == ci).astype(F32) + x
    for _ in range(5):
        x = _nn_hi(x, x)
        inv = inv + _nn_hi(inv, x)
    u, w = _split_lanes(_nn_hi(inv, _cat_lanes(vb, kb * jnp.exp(gc))))
    intra = _nt(q, k) * decay

    v_new = u - _nn(w, state)
    o = _nn(q * jnp.exp(gc), state) + _nn(intra, v_new)
    g_last = jnp.sum(g, axis=1, keepdims=True)
    k_dec = k * jnp.exp(g_last - gc)
    new_state = state * jnp.exp(g_last) + _tn(k_dec, v_new)
    return o, new_state


def _gdn_specs(nc, rev):
    cidx = (lambda n: nc - 1 - n) if rev else (lambda n: n)
    hb = lambda part: pl.BlockSpec((GDN_CHUNK, GDN_QK), lambda n: (cidx(n), part))
    misc = pl.BlockSpec((GDN_CHUNK, HEAD), lambda n: (cidx(n), COL_MISC // HEAD))
    params = pl.BlockSpec((8, HEAD), lambda n: (0, 0))
    hist = pl.BlockSpec((1, GDN_HEADS, HEAD, HEAD), lambda n: (cidx(n), 0, 0, 0))
    return hb, misc, params, hist


def _split_heads(v):
    return jnp.stack([v[:, h * HEAD:(h + 1) * HEAD] for h in range(v.shape[1] // HEAD)])


def _merge_heads(v):
    return jnp.concatenate([v[h] for h in range(v.shape[0])], axis=1)


def _gdn_fwd(qkv, proj, params, *, name):
    t = qkv.shape[0]
    nc = t // GDN_CHUNK
    hb, misc, pspec, hist = _gdn_specs(nc, False)

    def body(q_ref, k_ref, v_ref, m_ref, p_ref, o_ref, hist_ref, s_ref):
        @pl.when(pl.program_id(0) == 0)
        def _():
            s_ref[...] = jnp.zeros_like(s_ref)

        state = s_ref[...]
        hist_ref[0] = state
        o, new_state = _gdn_chunk(_split_heads(q_ref[...]), _split_heads(k_ref[...]), _split_heads(v_ref[...]),
                                  m_ref[...], p_ref[...], state)
        o_ref[...] = _merge_heads(o)
        s_ref[...] = new_state

    return _pcall(body, name=name, grid=(nc,),
                  in_specs=[hb(0), hb(1), hb(2), misc, pspec],
                  out_specs=[hb(0), hist],
                  out_shape=[jax.ShapeDtypeStruct((t, GDN_QK), F32),
                             jax.ShapeDtypeStruct((nc, GDN_HEADS, HEAD, HEAD), F32)],
                  scratch=[pltpu.VMEM((GDN_HEADS, HEAD, HEAD), F32)])(qkv, qkv, qkv, proj, params)


def _gdn_bwd(qkv, proj, params, hist_arr, do, dmisc_in, *, name, carry=None):
    t = qkv.shape[0]
    nc = t // GDN_CHUNK
    hb, misc, pspec, hist = _gdn_specs(nc, True)
    mrow = pl.BlockSpec((GDN_CHUNK, HEAD), lambda n: (nc - 1 - n, 0))

    def body(q_ref, k_ref, v_ref, m_ref, p_ref, hist_ref, do_ref, dmi_ref, dqkv_ref, dm_ref, dp_ref, ds_ref):
        @pl.when(pl.program_id(0) == 0)
        def _():
            ds_ref[...] = jnp.zeros_like(ds_ref)
            dp_ref[...] = jnp.zeros_like(dp_ref)

        _, vjp = jax.vjp(_gdn_chunk, _split_heads(q_ref[...]), _split_heads(k_ref[...]), _split_heads(v_ref[...]),
                         m_ref[...], p_ref[...], hist_ref[0])
        dq, dk, dv, dm, dp, ds = vjp((_split_heads(do_ref[...]), ds_ref[...]))
        dqkv_ref[:, 0:GDN_QK] = _merge_heads(dq)
        dqkv_ref[:, GDN_QK:2 * GDN_QK] = _merge_heads(dk)
        dqkv_ref[:, 2 * GDN_QK:] = _merge_heads(dv)
        ds_ref[...] = ds
        dm_ref[...] = dmi_ref[...] + dm
        dp_ref[...] += dp

    return _pcall(body, name=name, grid=(nc,),
                  in_specs=[hb(0), hb(1), hb(2), misc, pspec, hist, hb(0), mrow],
                  out_specs=[pl.BlockSpec((GDN_CHUNK, CONV_CH), lambda n: (nc - 1 - n, 0)), mrow, pspec],
                  out_shape=[jax.ShapeDtypeStruct((t, CONV_CH), F32), jax.ShapeDtypeStruct((t, HEAD), F32),
                             jax.ShapeDtypeStruct((8, HEAD), F32)],
                  scratch=[pltpu.VMEM((GDN_HEADS, HEAD, HEAD), F32)], carry=carry,
                  )(qkv, qkv, qkv, proj, params, hist_arr, do, dmisc_in)


def _gate_fwd(o_raw, proj, w, *, name):
    t = o_raw.shape[0]
    tm = _rows(t, HEAD)
    zb = COL_Z // HEAD

    def body(o_ref, z_ref, w_ref, out_ref):
        ov = o_ref[...]
        r = lax.rsqrt(jnp.mean(ov * ov, axis=-1, keepdims=True) + EPS)
        out_ref[...] = (ov * r * w_ref[...] * _silu(z_ref[...])).astype(out_ref.dtype)

    blk = pl.BlockSpec((tm, HEAD), lambda i, h: (i, h))
    return _pcall(body, name=name, grid=(t // tm, GDN_HEADS),
                  in_specs=[blk, pl.BlockSpec((tm, HEAD), lambda i, h: (i, zb + h)), pl.BlockSpec((1, HEAD), lambda i, h: (0, 0))],
                  out_specs=blk, out_shape=jax.ShapeDtypeStruct((t, GDN_QK), BF16))(o_raw, proj, w)


def _gate_bwd(o_raw, proj, w, dmixed, *, name):
    t = o_raw.shape[0]
    tm = _rows(t, HEAD)
    zb = COL_Z // HEAD

    def body(o_ref, z_ref, w_ref, dy_ref, do_ref, dz_ref, dw_ref):
        ov, zv, dyv = o_ref[...], z_ref[...], dy_ref[...]
        r = lax.rsqrt(jnp.mean(ov * ov, axis=-1, keepdims=True) + EPS)
        xh = ov * r
        dn = dyv * _silu(zv)
        dz_ref[...] = (dyv * xh * w_ref[...] * _dsilu(zv)).astype(dz_ref.dtype)
        dnw = dn * w_ref[...]
        do_ref[...] = r * (dnw - xh * jnp.mean(dnw * xh, axis=-1, keepdims=True))

        @pl.when((pl.program_id(0) == 0) & (pl.program_id(1) == 0))
        def _():
            dw_ref[...] = jnp.zeros_like(dw_ref)

        dw_ref[...] += (dn * xh).reshape(tm // 8, 8, HEAD).sum(axis=0)

    blk = pl.BlockSpec((tm, HEAD), lambda i, h: (i, h))
    return _pcall(body, name=name, grid=(t // tm, GDN_HEADS),
                  in_specs=[blk, pl.BlockSpec((tm, HEAD), lambda i, h: (i, zb + h)), pl.BlockSpec((1, HEAD), lambda i, h: (0, 0)), blk],
                  out_specs=[blk, blk, pl.BlockSpec((8, HEAD), lambda i, h: (0, 0))],
                  out_shape=[jax.ShapeDtypeStruct((t, GDN_QK), F32), jax.ShapeDtypeStruct((t, GDN_QK), BF16),
                             jax.ShapeDtypeStruct((8, HEAD), F32)])(o_raw, proj, w, dmixed)


def _rope_tables():
    half = QK_ROPE // 2
    inv = ROPE_THETA ** (-jnp.arange(half, dtype=F32) / half)
    zeros = jnp.zeros((HEAD - QK_ROPE,), F32)
    inv_row = jnp.concatenate([inv, inv, zeros])
    sign_row = jnp.concatenate([-jnp.ones((half,), F32), jnp.ones((half,), F32), zeros])
    mask_row = jnp.concatenate([jnp.ones((QK_ROPE,), F32), zeros])
    return jnp.concatenate([inv_row[None], sign_row[None], mask_row[None], jnp.zeros((5, HEAD), F32)], axis=0)


def _rope_cs(pos, tab, *, name):
    t = pos.shape[0]
    tm = _pick(t, 1024)

    def body(pos_ref, tab_ref, o_ref):
        tab = tab_ref[...]
        ang = pos_ref[...] * tab[0:1]
        o_ref[...] = jnp.concatenate([jnp.cos(ang) * tab[2:3], jnp.sin(ang) * tab[1:2]], axis=1)

    return _pcall(body, name=name, grid=(t // tm,),
                  in_specs=[pl.BlockSpec((tm, 1), lambda i: (i, 0)), pl.BlockSpec((8, HEAD), lambda i: (0, 0))],
                  out_specs=pl.BlockSpec((tm, 2 * HEAD), lambda i: (i, 0)),
                  out_shape=jax.ShapeDtypeStruct((t, 2 * HEAD), F32))(pos, tab)


def _rotate(x, cs, sign):
    lane = lax.broadcasted_iota(jnp.int32, x.shape, 1)
    half = QK_ROPE // 2
    partner = jnp.where(lane < half, pltpu.roll(x, HEAD - half, axis=1), pltpu.roll(x, half, axis=1))
    return x * cs[:, :HEAD] + partner * (cs[:, HEAD:] * sign)


def _q_rot(q, cs, *, name, sign, out_dtype=BF16):
    t = q.shape[0]
    tm = _pick(t, 1024)
    scale = (HEAD + QK_ROPE) ** -0.5

    def body(q_ref, cs_ref, o_ref):
        qv = q_ref[...].astype(F32)
        rot = _rotate(qv[:, HEAD:], cs_ref[...], sign)
        o_ref[...] = (jnp.concatenate([qv[:, :HEAD], rot], axis=1) * scale).astype(o_ref.dtype)

    blk = pl.BlockSpec((tm, QHEAD), lambda i, h: (i, h))
    return _pcall(body, name=name, grid=(t // tm, MLA_HEADS),
                  in_specs=[blk, pl.BlockSpec((tm, 2 * HEAD), lambda i, h: (i, 0))],
                  out_specs=blk, out_shape=jax.ShapeDtypeStruct((t, MLA_HEADS * QHEAD), out_dtype))(q, cs)


def _q_up(cqn, wuq_p, cs, *, name):
    t, lora = cqn.shape
    tm = _pick(t, 1024)
    scale = (HEAD + QK_ROPE) ** -0.5

    def body(a_ref, w_ref, cs_ref, o_ref):
        qv = lax.dot_general(a_ref[...], w_ref[...], (NN, ((), ())), preferred_element_type=F32)
        rot = _rotate(qv[:, HEAD:], cs_ref[...], 1.0)
        o_ref[...] = (jnp.concatenate([qv[:, :HEAD], rot], axis=1) * scale).astype(o_ref.dtype)

    return _pcall(body, name=name, grid=(t // tm, MLA_HEADS),
                  in_specs=[pl.BlockSpec((tm, lora), lambda i, h: (i, 0)), pl.BlockSpec((lora, QHEAD), lambda i, h: (0, h)),
                            pl.BlockSpec((tm, 2 * HEAD), lambda i, h: (i, 0))],
                  out_specs=pl.BlockSpec((tm, QHEAD), lambda i, h: (i, h)),
                  out_shape=jax.ShapeDtypeStruct((t, MLA_HEADS * QHEAD), BF16))(cqn, wuq_p, cs)


def _kv_up(ckvn, wukv, proj, cs, *, name):
    t, lora = ckvn.shape
    tm = _pick(t, 1024)

    def body(a_ref, w_ref, m_ref, cs_ref, k_ref, v_ref):
        kvv = lax.dot_general(a_ref[...], w_ref[...], (NN, ((), ())), preferred_element_type=F32)
        misc = m_ref[...]
        lane = lax.broadcasted_iota(jnp.int32, misc.shape, 1)
        rot = _rotate(jnp.where(lane < QK_ROPE, misc, 0.0), cs_ref[...], 1.0)
        k_ref[...] = jnp.concatenate([kvv[:, :HEAD], rot], axis=1).astype(k_ref.dtype)
        v_ref[...] = kvv[:, HEAD:].astype(v_ref.dtype)

    return _pcall(body, name=name, grid=(t // tm, MLA_HEADS),
                  in_specs=[pl.BlockSpec((tm, lora), lambda i, h: (i, 0)), pl.BlockSpec((lora, QHEAD), lambda i, h: (0, h)),
                            pl.BlockSpec((tm, HEAD), lambda i, h: (i, COL_MISC // HEAD)),
                            pl.BlockSpec((tm, 2 * HEAD), lambda i, h: (i, 0))],
                  out_specs=[pl.BlockSpec((tm, QHEAD), lambda i, h: (i, h)), pl.BlockSpec((tm, HEAD), lambda i, h: (i, h))],
                  out_shape=[jax.ShapeDtypeStruct((t, MLA_HEADS * QHEAD), BF16), jax.ShapeDtypeStruct((t, MLA_HEADS * HEAD), BF16)],
                  )(ckvn, wukv, proj, cs)


def _krope_bwd(dkr, cs, *, name):
    t = dkr.shape[0]
    tm = _pick(t, 512)

    def body(d_ref, cs_ref, o_ref):
        d = d_ref[...]
        acc = d[:, :HEAD]
        for h in range(1, MLA_HEADS):
            acc = acc + d[:, h * HEAD:(h + 1) * HEAD]
        o_ref[...] = _rotate(acc, cs_ref[...], -1.0)

    return _pcall(body, name=name, grid=(t // tm,),
                  in_specs=[pl.BlockSpec((tm, MLA_HEADS * HEAD), lambda i: (i, 0)), pl.BlockSpec((tm, 2 * HEAD), lambda i: (i, 0))],
                  out_specs=pl.BlockSpec((tm, HEAD), lambda i: (i, 0)),
                  out_shape=jax.ShapeDtypeStruct((t, HEAD), F32))(dkr, cs)


NEG = -1e30


def _tri(step, counts):
    starts = [sum(counts[:o]) for o in range(len(counts))]
    outer = sum([(step >= s).astype(jnp.int32) for s in starts[1:]], jnp.int32(0))
    start = sum([(step >= starts[o]).astype(jnp.int32) * (starts[o] - starts[o - 1]) for o in range(1, len(counts))], jnp.int32(0))
    return outer, step - start


def _attn_fwd(q, k, v, *, name, tq=1024, tk=1024, carry=None):
    t = q.shape[0]
    tq, tk = _pick(t, tq), _pick(t, tk)
    nq = t // tq
    last_kv = lambda i: (i * tq + tq - 1) // tk
    counts = [last_kv(i) + 1 for i in range(nq)]

    def body(q_ref, k_ref, v_ref, o_ref, lse_ref, m_ref, l_ref, acc_ref):
        i, j = _tri(pl.program_id(1), counts)

        @pl.when(j == 0)
        def _():
            m_ref[...] = jnp.full_like(m_ref, NEG)
            l_ref[...] = jnp.zeros_like(l_ref)
            acc_ref[...] = jnp.zeros_like(acc_ref)

        def step(masked):
            s = lax.dot_general(q_ref[...], k_ref[...], (NT, ((), ())), preferred_element_type=F32)
            if masked:
                qpos = i * tq + lax.broadcasted_iota(jnp.int32, s.shape, 0)
                kpos = j * tk + lax.broadcasted_iota(jnp.int32, s.shape, 1)
                s = jnp.where(kpos <= qpos, s, NEG)
            m_prev = m_ref[...]
            m_new = jnp.maximum(m_prev, jnp.max(s, axis=1, keepdims=True))
            alpha = jnp.exp(m_prev - m_new)
            p = jnp.exp(s - m_new)
            l_ref[...] = alpha * l_ref[...] + jnp.sum(p, axis=1, keepdims=True)
            acc_ref[...] = alpha * acc_ref[...] + lax.dot_general(p.astype(BF16), v_ref[...], (NN, ((), ())),
                                                                  preferred_element_type=F32)
            m_ref[...] = m_new

        crosses = j * tk + tk - 1 > i * tq

        @pl.when(crosses)
        def _():
            step(True)

        @pl.when(jnp.logical_not(crosses))
        def _():
            step(False)

        @pl.when(j == last_kv(i))
        def _():
            o_ref[...] = acc_ref[...] / l_ref[...]
            lse_ref[...] = jnp.broadcast_to(m_ref[...] + jnp.log(l_ref[...]), lse_ref.shape)

    qblk = pl.BlockSpec((tq, QHEAD), lambda h, s: (_tri(s, counts)[0], h))
    oblk = pl.BlockSpec((tq, HEAD), lambda h, s: (_tri(s, counts)[0], h))
    return _pcall(body, name=name, grid=(MLA_HEADS, sum(counts)),
                  in_specs=[qblk, pl.BlockSpec((tk, QHEAD), lambda h, s: (_tri(s, counts)[1], h)),
                            pl.BlockSpec((tk, HEAD), lambda h, s: (_tri(s, counts)[1], h))],
                  out_specs=[oblk, oblk],
                  out_shape=[jax.ShapeDtypeStruct((t, MLA_HEADS * HEAD), F32), jax.ShapeDtypeStruct((t, MLA_HEADS * HEAD), F32)],
                  scratch=[pltpu.VMEM((tq, 1), F32), pltpu.VMEM((tq, 1), F32), pltpu.VMEM((tq, HEAD), F32)],
                  carry=carry)(q, k, v)


def _attn_bwd(q, k, v, do, lse, delta, *, name, tq=1024, tk=1024, carry=None):
    t = q.shape[0]
    tq, tk = _pick(t, tq), _pick(t, tk)
    nq, nk = t // tq, t // tk
    first_q = lambda j: (j * tk) // tq
    counts = [nq - first_q(j) for j in range(nk)]

    def where(step):
        j, off = _tri(step, counts)
        return j, first_q(j) + off

    lanes = lambda col: jnp.tile(col, (1, tk // HEAD))

    def body(q_ref, k_ref, v_ref, do_ref, lse_ref, dl_ref, dq_ref, dkv_ref, dkr_ref, dk_acc, dv_acc):
        j, i = where(pl.program_id(1))

        @pl.when(i == first_q(j))
        def _():
            dk_acc[...] = jnp.zeros_like(dk_acc)
            dv_acc[...] = jnp.zeros_like(dv_acc)

        def step(masked):
            qv, kv_, dov = q_ref[...], k_ref[...], do_ref[...].astype(BF16)
            s = lax.dot_general(qv, kv_, (NT, ((), ())), preferred_element_type=F32)
            p = jnp.exp((s - lanes(lse_ref[...])).astype(BF16))
            if masked:
                qpos = i * tq + lax.broadcasted_iota(jnp.int32, s.shape, 0)
                kpos = j * tk + lax.broadcasted_iota(jnp.int32, s.shape, 1)
                p = jnp.where(kpos <= qpos, p, jnp.zeros_like(p))
            dv_acc[...] += lax.dot_general(p, dov, (TN, ((), ())), preferred_element_type=F32)
            dp = lax.dot_general(dov, v_ref[...], (NT, ((), ())), preferred_element_type=F32)
            ds = p * (dp - lanes(dl_ref[...])).astype(BF16)
            dk_acc[...] += lax.dot_general(ds, qv, (TN, ((), ())), preferred_element_type=F32)
            contrib = lax.dot_general(ds, kv_, (NN, ((), ())), preferred_element_type=F32)
            rows = pl.ds(pl.multiple_of(i * tq, tq), tq)

            @pl.when(j == 0)
            def _():
                dq_ref[rows, :] = contrib

            @pl.when(j > 0)
            def _():
                dq_ref[rows, :] += contrib

        crosses = j * tk + tk - 1 > i * tq

        @pl.when(crosses)
        def _():
            step(True)

        @pl.when(jnp.logical_not(crosses))
        def _():
            step(False)

        @pl.when(i == nq - 1)
        def _():
            dk = dk_acc[...]
            dkv_ref[...] = jnp.concatenate([dk[:, :HEAD], dv_acc[...]], axis=1).astype(dkv_ref.dtype)
            dkr_ref[...] = dk[:, HEAD:]

    qi = lambda h, s: (where(s)[1], h)
    kj = lambda h, s: (where(s)[0], h)
    return _pcall(body, name=name, grid=(MLA_HEADS, sum(counts)),
                  in_specs=[pl.BlockSpec((tq, QHEAD), qi), pl.BlockSpec((tk, QHEAD), kj), pl.BlockSpec((tk, HEAD), kj),
                            pl.BlockSpec((tq, HEAD), qi), pl.BlockSpec((tq, HEAD), qi), pl.BlockSpec((tq, HEAD), qi)],
                  out_specs=[pl.BlockSpec((t, QHEAD), lambda h, s: (0, h)), pl.BlockSpec((tk, QHEAD), kj),
                             pl.BlockSpec((tk, HEAD), kj)],
                  out_shape=[jax.ShapeDtypeStruct((t, MLA_HEADS * QHEAD), F32), jax.ShapeDtypeStruct((t, MLA_HEADS * QHEAD), BF16),
                             jax.ShapeDtypeStruct((t, MLA_HEADS * HEAD), F32)],
                  scratch=[pltpu.VMEM((tk, QHEAD), F32), pltpu.VMEM((tk, HEAD), F32)], carry=carry)(q, k, v, do, lse, delta)


def _ffn_up(h, wgate, wup, *, name, bm=512, bn=FF_WIDE):
    t = h.shape[0]
    bm = _pick(t, bm)

    def body(h_ref, wg_ref, wu_ref, g_ref, u_ref, a_ref):
        hv = h_ref[...]
        g = lax.dot_general(hv, wg_ref[...], (NN, ((), ())), preferred_element_type=F32)
        u = lax.dot_general(hv, wu_ref[...], (NN, ((), ())), preferred_element_type=F32)
        g_ref[...] = g.astype(g_ref.dtype)
        u_ref[...] = u.astype(u_ref.dtype)
        a_ref[...] = (_silu(g) * u).astype(a_ref.dtype)

    w_spec = pl.BlockSpec((D_MODEL, bn), lambda j, i: (0, j))
    o_spec = pl.BlockSpec((bm, bn), lambda j, i: (i, j))
    sds = jax.ShapeDtypeStruct((t, D_FF), BF16)
    return _pcall(body, name=name, grid=(D_FF // bn, t // bm),
                  in_specs=[pl.BlockSpec((bm, D_MODEL), lambda j, i: (i, 0)), w_spec, w_spec],
                  out_specs=[o_spec] * 3, out_shape=[sds] * 3)(h, wgate, wup)


def _ffn_down_dx(dy, wdown, gate, up, *, name, bm=512, bn=FF_WIDE):
    t = dy.shape[0]
    bm = _pick(t, bm)

    def body(dy_ref, w_ref, g_ref, u_ref, dg_ref, du_ref):
        d = lax.dot_general(dy_ref[...].astype(BF16), w_ref[...], (NT, ((), ())), preferred_element_type=F32)
        g = g_ref[...].astype(F32)
        dg_ref[...] = (d * u_ref[...].astype(F32) * _dsilu(g)).astype(dg_ref.dtype)
        du_ref[...] = (d * _silu(g)).astype(du_ref.dtype)

    o_spec = pl.BlockSpec((bm, bn), lambda j, i: (i, j))
    sds = jax.ShapeDtypeStruct((t, D_FF), BF16)
    return _pcall(body, name=name, grid=(D_FF // bn, t // bm),
                  in_specs=[pl.BlockSpec((bm, D_MODEL), lambda j, i: (i, 0)), pl.BlockSpec((bn, D_MODEL), lambda j, i: (j, 0)),
                            o_spec, o_spec],
                  out_specs=[o_spec, o_spec], out_shape=[sds, sds])(dy, wdown, gate, up)


def _loss_bwd(x2, w, target, *, name):
    t = x2.shape[0]
    tm = _rows(t, D_MODEL)

    def body(x_ref, w_ref, t_ref, dx_ref, dxb_ref, dw_ref, l_ref):
        xv, wv = x_ref[...], w_ref[...]
        r = lax.rsqrt(jnp.mean(xv * xv, axis=-1, keepdims=True) + EPS)
        xh = xv * r
        err = xh * wv - t_ref[...]
        dy = err * (1.0 / D_MODEL)
        dyw = dy * wv
        dx = r * (dyw - xh * jnp.mean(dyw * xh, axis=-1, keepdims=True))
        dx_ref[...] = dx
        dxb_ref[...] = dx.astype(BF16)

        @pl.when(pl.program_id(0) == 0)
        def _():
            dw_ref[...] = jnp.zeros_like(dw_ref)
            l_ref[...] = jnp.zeros_like(l_ref)

        dw_ref[...] += (dy * xh).reshape(tm // 8, 8, D_MODEL).sum(axis=0)
        sq = (err * err).reshape(tm // 8, 8, D_MODEL).sum(axis=0)
        part = sq[:, :HEAD]
        for c in range(1, D_MODEL // HEAD):
            part = part + sq[:, c * HEAD:(c + 1) * HEAD]
        l_ref[...] += part * (0.5 / D_MODEL)

    row = pl.BlockSpec((tm, D_MODEL), lambda i: (i, 0))
    return _pcall(body, name=name, grid=(t // tm,),
                  in_specs=[row, pl.BlockSpec((1, D_MODEL), lambda i: (0, 0)), row],
                  out_specs=[row, row, pl.BlockSpec((8, D_MODEL), lambda i: (0, 0)), pl.BlockSpec((8, HEAD), lambda i: (0, 0))],
                  out_shape=[jax.ShapeDtypeStruct((t, D_MODEL), F32), jax.ShapeDtypeStruct((t, D_MODEL), BF16),
                             jax.ShapeDtypeStruct((8, D_MODEL), F32), jax.ShapeDtypeStruct((8, HEAD), F32)])(x2, w, target)


def _unshard_cols(g):
    return jnp.transpose(g, (1, 0, 2)).reshape(g.shape[1], N_DEV * g.shape[2])


def _shard_cols(w):
    return jnp.transpose(w.reshape(w.shape[0], N_DEV, w.shape[1] // N_DEV), (1, 0, 2))


_WIN_ORDER = ((0, 4096), (4112, 5136), (5136, 5200), (4096, 4112))
_WIN_SHARD = IN_WIDTH // N_DEV


def _win_pieces():
    out, pos = [], 0
    for a, b in _WIN_ORDER:
        c = a
        while c < b:
            dev, off = divmod(c, _WIN_SHARD)
            width = min(b, (dev + 1) * _WIN_SHARD) - c
            out.append((dev, off, width, pos))
            c, pos = c + width, pos + width
    return out


def _win_gathered_to_padded(g):
    pieces = [g[dev][:, off:off + width] for dev, off, width, _ in _win_pieces()]
    return jnp.concatenate(pieces + [jnp.zeros((g.shape[1], PROJ_W - IN_WIDTH), g.dtype)], axis=1)


def _win_padded_to_shards(d):
    shards = []
    for dev in range(N_DEV):
        mine = sorted((off, width, pos) for dv, off, width, pos in _win_pieces() if dv == dev)
        shards.append(jnp.concatenate([d[:, pos:pos + width] for _, width, pos in mine], axis=1))
    return jnp.stack(shards)


def _wuq_to_padded(w):
    w3 = w.reshape(w.shape[0], MLA_HEADS, HEAD + QK_ROPE)
    return jnp.pad(w3, ((0, 0), (0, 0), (0, QHEAD - HEAD - QK_ROPE))).reshape(w.shape[0], MLA_HEADS * QHEAD)


def _wuq_from_padded(d):
    return d.reshape(d.shape[0], MLA_HEADS, QHEAD)[:, :, :HEAD + QK_ROPE].reshape(d.shape[0], MLA_HEADS * (HEAD + QK_ROPE))


def _late_weights(g_out, g_gate, g_up, g_down):
    return g_out.reshape(D_MODEL, D_MODEL), _unshard_cols(g_gate), _unshard_cols(g_up), g_down.reshape(D_FF, D_MODEL)


def _early_weights(g_in, g_uq, g_ukv, g_conv):
    conv = jnp.concatenate([_unshard_cols(g_conv), jnp.zeros((8 - GDN_CONV, CONV_CH), F32)], axis=0)
    return _win_gathered_to_padded(g_in), _wuq_to_padded(_unshard_cols(g_uq)), _unshard_cols(g_ukv), conv


def _local_step(x, pos, target, early, late, small, exchange):
    cs = _rope_cs(pos, _rope_tables(), name="rope_cs")
    if exchange:
        h1, gathered = _rms_fwd(x, small["attn_norm_w"], name="rms1_fwd", width=D_MODEL, carry=_Gather(early))
        win_p, wuq_p, wukv, conv_w = _early_weights(*gathered)
    else:
        h1 = _rms_fwd(x, small["attn_norm_w"], name="rms1_fwd", width=D_MODEL)
        win_p, wuq_p, wukv, conv_w = early
        wout, wgate, wup, wdown = late
    proj = _mm(h1, win_p, name="mm_in", bn=PROJ_BLK)
    qkv = _conv_fwd(proj, conv_w, name="conv_fwd")
    o_gdn_raw, hist = _gdn_fwd(qkv, proj, small["gdn_params"], name="gdn_fwd")
    o_gdn = _gate_fwd(o_gdn_raw, proj, small["gdn_norm_w"], name="gate_fwd")
    cqn = _rms_fwd(proj, small["q_norm_w"], name="rmsq_fwd", width=Q_LORA, col0=COL_CQ)
    ckvn = _rms_fwd(proj, small["kv_norm_w"], name="rmskv_fwd", width=KV_LORA, col0=COL_CKV)
    q_full = _q_up(cqn, wuq_p, cs, name="q_up")
    k_full, v_b = _kv_up(ckvn, wukv, proj, cs, name="kv_up")
    if exchange:
        (o_mla_raw, lse), gathered = _attn_fwd(q_full, k_full, v_b, name="attn_fwd", carry=_Gather(late))
        wout, wgate, wup, wdown = _late_weights(*gathered)
    else:
        o_mla_raw, lse = _attn_fwd(q_full, k_full, v_b, name="attn_fwd")
    o_mla = _rms_fwd(o_mla_raw, small["mla_out_norm_w"], name="rmso_fwd", width=HEAD, heads=MLA_HEADS)
    mixed = jnp.concatenate([o_gdn, o_mla], axis=1)
    x1 = _mm(mixed, wout, name="mm_out", res=x)
    h2 = _rms_fwd(x1, small["ffn_norm_w"], name="rms2_fwd", width=D_MODEL)
    gate, up, act = _ffn_up(h2, wgate, wup, name="ffn_up")
    x2 = _mm(act, wdown, name="mm_down", res=x1, bk=FF_WIDE)
    dx2, dx2_b, dw_final, loss_part = _loss_bwd(x2, small["final_norm_w"], target, name="loss_bwd")
    dgate, dup = _ffn_down_dx(dx2_b, wdown, gate, up, name="ffn_down_dx")
    d_wdown = _mm(act, dx2_b, name="mm_down_dw", ta=True, out_dtype=BF16, bm=FF_WIDE)
    dh2 = _mm(dgate, wgate, name="mm_gateup_dx", tb=True, bk=FF_WIDE, pair=(dup, wup))
    d_wgate, d_wup = _mm_dw2(h2, dgate, dup, name="mm_gateup_dw")
    dx1, dw_ffn = _rms_bwd(x1, small["ffn_norm_w"], dh2, name="rms2_bwd", width=D_MODEL, res=dx2)
    dmixed = _mm(dx1, wout, name="mm_out_dx", tb=True)
    d_wout = _mm(mixed, dx1, name="mm_out_dw", ta=True, out_dtype=BF16)
    do_mla, dw_mla_out, delta = _rms_bwd(o_mla_raw, small["mla_out_norm_w"], dmixed, name="rmso_bwd", width=HEAD,
                                         heads=MLA_HEADS, dcol0=GDN_QK, with_delta=True, out_dtype=BF16)
    if exchange:
        send = [d_wdown.reshape(N_DEV, D_FF // N_DEV, D_MODEL), _shard_cols(d_wgate), _shard_cols(d_wup)]
        (dq_full, dkv, dkr_h), (r_down, r_gate, r_up) = _attn_bwd(q_full, k_full, v_b, do_mla, lse, delta, name="attn_bwd",
                                                                  carry=_Exchange(send, [False] * 3))
    else:
        dq_full, dkv, dkr_h = _attn_bwd(q_full, k_full, v_b, do_mla, lse, delta, name="attn_bwd")
    dq_pre = _q_rot(dq_full, cs, name="q_rot_bwd", sign=-1.0)
    dmisc_kr = _krope_bwd(dkr_h, cs, name="krope_bwd")
    dcqn = _mm(dq_pre, wuq_p, name="mm_uq_dx", tb=True)
    d_wuq = _mm(cqn, dq_pre, name="mm_uq_dw", ta=True, out_dtype=BF16)
    dckvn = _mm(dkv, wukv, name="mm_ukv_dx", tb=True)
    d_wukv = _mm(ckvn, dkv, name="mm_ukv_dw", ta=True, out_dtype=BF16)
    dcq, dw_qn = _rms_bwd(proj, small["q_norm_w"], dcqn, name="rmsq_bwd", width=Q_LORA, col0=COL_CQ, out_dtype=BF16)
    dckv, dw_kvn = _rms_bwd(proj, small["kv_norm_w"], dckvn, name="rmskv_bwd", width=KV_LORA, col0=COL_CKV, out_dtype=BF16)
    do_gdn, dz, dw_gdn = _gate_bwd(o_gdn_raw, proj, small["gdn_norm_w"], dmixed, name="gate_bwd")
    if exchange:
        send = [d_wout.reshape(N_DEV, D_MODEL // N_DEV, D_MODEL), _shard_cols(_wuq_from_padded(d_wuq)), _shard_cols(d_wukv)]
        (dqkv, dmisc, d_params), (r_out, r_uq, r_ukv) = _gdn_bwd(
            qkv, proj, small["gdn_params"], hist, do_gdn, dmisc_kr, name="gdn_bwd", carry=_Exchange(send, [False] * 3))
    else:
        dqkv, dmisc, d_params = _gdn_bwd(qkv, proj, small["gdn_params"], hist, do_gdn, dmisc_kr, name="gdn_bwd")
    dqkv_pre, dconv = _conv_bwd(proj, conv_w, dqkv, name="conv_bwd")
    dproj = jnp.concatenate([dqkv_pre, dz, dcq, dckv, dmisc.astype(BF16), jnp.zeros((x.shape[0], PROJ_W - COL_MISC - HEAD), BF16)], axis=1)
    d_win = _mm(h1, dproj, name="mm_in_dw", ta=True, out_dtype=BF16, bn=PROJ_BLK)
    if exchange:
        dh1, (r_in,) = _mm(dproj, win_p, name="mm_in_dx", tb=True, bk=PROJ_BLK,
                           carry=_Exchange([_win_padded_to_shards(d_win)], [False]))
        d_win = r_in
    else:
        dh1 = _mm(dproj, win_p, name="mm_in_dx", tb=True, bk=PROJ_BLK)
    dx, dw_attn = _rms_bwd(x, small["attn_norm_w"], dh1, name="rms1_bwd", width=D_MODEL, res=dx1)

    if exchange:
        big = {"w_in": d_win, "w_uq": r_uq, "w_ukv": r_ukv, "w_out": r_out, "w_gate": r_gate, "w_up": r_up, "w_down": r_down}
    else:
        big = {"w_in": d_win, "w_uq": d_wuq, "w_ukv": d_wukv, "w_out": d_wout, "w_gate": d_wgate, "w_up": d_wup,
               "w_down": d_wdown}
    sm = {"attn_norm_w": dw_attn, "ffn_norm_w": dw_ffn, "final_norm_w": dw_final, "q_norm_w": dw_qn, "kv_norm_w": dw_kvn,
          "gdn_norm_w": dw_gdn, "mla_out_norm_w": dw_mla_out, "gdn_params": d_params, "conv_w": dconv, "loss": loss_part}
    return dx, big, sm


def _exchange(ex, *, name):
    def body(*refs):
        ins, outs, sems = refs[:ex.n], refs[ex.n:2 * ex.n], refs[2 * ex.n:]
        ex.start(ins, outs, sems)
        ex.forward(ins, outs, sems)
        ex.finish(ins, outs, sems)

    any_spec = pl.BlockSpec(memory_space=pl.ANY)
    return pl.pallas_call(body, name=name, in_specs=[any_spec] * ex.n, out_specs=[any_spec] * ex.n,
                          out_shape=ex.out_shape(), scratch_shapes=ex.sems())(*ex.arrays)


def _adamw_math(g, w, m, v):
    m = ADAM_B1 * m + (1.0 - ADAM_B1) * g
    v = ADAM_B2 * v + (1.0 - ADAM_B2) * (g * g)
    m_hat = m / (1.0 - ADAM_B1 ** ADAM_STEP)
    v_hat = v / (1.0 - ADAM_B2 ** ADAM_STEP)
    delta = -ADAM_LR * (m_hat / (jnp.sqrt(v_hat) + ADAM_EPS) + ADAM_WD * w)
    return delta, m, v


def _adamw(parts, w, m, v, *, name):
    npart, r, c = parts.shape
    tr = r if r * c * 4 <= (1 << 20) else _rows(r, c, 1 << 20)

    def body(p_ref, w_ref, m_ref, v_ref, g_ref, d_ref, nm_ref, nv_ref):
        g = p_ref[0].astype(F32)
        for s in range(1, npart):
            g = g + p_ref[s].astype(F32)
        g_ref[...] = g
        d_ref[...], nm_ref[...], nv_ref[...] = _adamw_math(g, w_ref[...], m_ref[...], v_ref[...])

    blk = pl.BlockSpec((tr, c), lambda i: (i, 0))
    sds = jax.ShapeDtypeStruct((r, c), F32)
    return _pcall(body, name=name, grid=(r // tr,),
                  in_specs=[pl.BlockSpec((npart, tr, c), lambda i: (0, i, 0)), blk, blk, blk],
                  out_specs=[blk] * 4, out_shape=[sds] * 4)(parts, w, m, v)


def _sum_parts(parts, *, name):
    npart, r, c = parts.shape

    def body(p_ref, o_ref):
        g = p_ref[0]
        for s in range(1, npart):
            g = g + p_ref[s]
        o_ref[...] = g

    return _pcall(body, name=name, grid=(1,), in_specs=[pl.BlockSpec((npart, r, c), lambda i: (0, 0, 0))],
                  out_specs=pl.BlockSpec((r, c), lambda i: (0, 0)), out_shape=jax.ShapeDtypeStruct((r, c), F32))(parts)


_SMALL = (("attn_norm_w", D_MODEL), ("ffn_norm_w", D_MODEL), ("final_norm_w", D_MODEL), ("q_norm_w", Q_LORA),
          ("kv_norm_w", KV_LORA), ("gdn_norm_w", HEAD), ("mla_out_norm_w", HEAD), ("a_log", HEAD), ("dt_bias", HEAD))
_SMALL_ROWS = sum(n for _, n in _SMALL) // HEAD
_CONV_ROWS = GDN_CONV * CONV_CH // HEAD
_PACK_ROWS = 160


def _pad_lanes(v, n):
    v = v.reshape(-1)
    return jnp.concatenate([v, jnp.zeros((n - v.shape[0],), v.dtype)])


def kernel(x, positions, attn_norm_w, w_in, conv_w, a_log, dt_bias, gdn_norm_w, q_norm_w, w_uq, kv_norm_w, w_ukv, mla_out_norm_w, w_out, ffn_norm_w, w_gate, w_up, w_down, final_norm_w, loss_target, m_attn_norm_w, m_w_in, m_conv_w, m_a_log, m_dt_bias, m_gdn_norm_w, m_q_norm_w, m_w_uq, m_kv_norm_w, m_w_ukv, m_mla_out_norm_w, m_w_out, m_ffn_norm_w, m_w_gate, m_w_up, m_w_down, m_final_norm_w, v_attn_norm_w, v_w_in, v_conv_w, v_a_log, v_dt_bias, v_gdn_norm_w, v_q_norm_w, v_w_uq, v_kv_norm_w, v_w_ukv, v_mla_out_norm_w, v_w_out, v_ffn_norm_w, v_w_gate, v_w_up, v_w_down, v_final_norm_w):
    t = x.shape[1]
    me = 4 * lax.axis_index("x") + 2 * lax.axis_index("y") + lax.axis_index("c")
    weights = dict(attn_norm_w=attn_norm_w, w_in=w_in, conv_w=conv_w, a_log=a_log, dt_bias=dt_bias, gdn_norm_w=gdn_norm_w,
                   q_norm_w=q_norm_w, w_uq=w_uq, kv_norm_w=kv_norm_w, w_ukv=w_ukv, mla_out_norm_w=mla_out_norm_w, w_out=w_out,
                   ffn_norm_w=ffn_norm_w, w_gate=w_gate, w_up=w_up, w_down=w_down, final_norm_w=final_norm_w)
    mom_m = dict(attn_norm_w=m_attn_norm_w, w_in=m_w_in, conv_w=m_conv_w, a_log=m_a_log, dt_bias=m_dt_bias, gdn_norm_w=m_gdn_norm_w,
                 q_norm_w=m_q_norm_w, w_uq=m_w_uq, kv_norm_w=m_kv_norm_w, w_ukv=m_w_ukv, mla_out_norm_w=m_mla_out_norm_w,
                 w_out=m_w_out, ffn_norm_w=m_ffn_norm_w, w_gate=m_w_gate, w_up=m_w_up, w_down=m_w_down, final_norm_w=m_final_norm_w)
    mom_v = dict(attn_norm_w=v_attn_norm_w, w_in=v_w_in, conv_w=v_conv_w, a_log=v_a_log, dt_bias=v_dt_bias, gdn_norm_w=v_gdn_norm_w,
                 q_norm_w=v_q_norm_w, w_uq=v_w_uq, kv_norm_w=v_kv_norm_w, w_ukv=v_w_ukv, mla_out_norm_w=v_mla_out_norm_w,
                 w_out=v_w_out, ffn_norm_w=v_ffn_norm_w, w_gate=v_w_gate, w_up=v_w_up, w_down=v_w_down, final_norm_w=v_final_norm_w)
    big_names = ("w_in", "w_uq", "w_ukv", "w_out", "w_gate", "w_up", "w_down")

    shard = {n: weights[n][0].astype(BF16) for n in big_names}
    early = [shard["w_in"], shard["w_uq"], shard["w_ukv"], weights["conv_w"][0]]
    late = [shard["w_out"], shard["w_gate"], shard["w_up"], shard["w_down"]]

    gdn_params = jnp.concatenate([_pad_lanes(a_log, HEAD)[None], _pad_lanes(dt_bias, HEAD)[None], jnp.zeros((6, HEAD), F32)], axis=0)
    small = {n: weights[n].reshape(1, -1) for n in ("attn_norm_w", "ffn_norm_w", "final_norm_w", "q_norm_w", "kv_norm_w",
                                                    "gdn_norm_w", "mla_out_norm_w")}
    small["gdn_params"] = gdn_params

    dx, big, sm = _local_step(x[0], positions.reshape(t, 1).astype(F32), loss_target[0], early, late, small, True)

    rows8 = lambda name: jnp.sum(sm[name], axis=0)
    pieces = [rows8(n) for n, _ in _SMALL[:7]]
    pieces += [_pad_lanes(jnp.sum(sm["gdn_params"][0:1], axis=0), HEAD), _pad_lanes(jnp.sum(sm["gdn_params"][1:2], axis=0), HEAD)]
    pieces.append(jnp.sum(sm["conv_w"], axis=1).reshape(-1))
    pieces.append(_pad_lanes(jnp.sum(sm["loss"]).reshape(1), HEAD))
    packed = _pad_lanes(jnp.concatenate(pieces), _PACK_ROWS * HEAD).reshape(_PACK_ROWS, HEAD)
    (r_small,) = _exchange(_Exchange([packed], [True]), name="exchange_small")

    outs_g, outs_d, outs_m, outs_v = {}, {}, {}, {}
    for name in big_names:
        g, d, nm, nv = _adamw(big[name], weights[name][0], mom_m[name][0], mom_v[name][0], name="adamw_" + name)
        outs_g[name], outs_d[name], outs_m[name], outs_v[name] = g[None], d[None], nm[None], nv[None]

    total = _sum_parts(r_small, name="sum_small")
    flat = total.reshape(-1)
    loss = flat[(_SMALL_ROWS + _CONV_ROWS) * HEAD]
    g_small, off = {}, 0
    for n, size in _SMALL:
        g_small[n] = flat[off:off + size]
        off += size
    g_conv_full = flat[off:off + GDN_CONV * CONV_CH].reshape(GDN_CONV, CONV_CH)
    g_small["conv_w"] = lax.dynamic_slice(g_conv_full, (0, me * (CONV_CH // N_DEV)), (GDN_CONV, CONV_CH // N_DEV)).reshape(-1)
    order = [n for n, _ in _SMALL] + ["conv_w"]
    sizes = dict(_SMALL)
    sizes["conv_w"] = GDN_CONV * CONV_CH // N_DEV
    true_size = {n: weights[n].size for n in order}

    def pack(d):
        return jnp.concatenate([_pad_lanes(d[n], sizes[n]) for n in order]).reshape(1, -1, HEAD)

    g2, d2, m2, v2 = _adamw(pack(g_small), pack(weights)[0], pack(mom_m)[0], pack(mom_v)[0], name="adamw_small")
    off = 0
    for n in order:
        for src, dst in ((g2, outs_g), (d2, outs_d), (m2, outs_m), (v2, outs_v)):
            dst[n] = src.reshape(-1)[off:off + true_size[n]].reshape(weights[n].shape)
        off += sizes[n]

    names = ("attn_norm_w", "w_in", "conv_w", "a_log", "dt_bias", "gdn_norm_w", "q_norm_w", "w_uq", "kv_norm_w", "w_ukv",
             "mla_out_norm_w", "w_out", "ffn_norm_w", "w_gate", "w_up", "w_down", "final_norm_w")
    return (loss, dx[None], *[outs_g[n] for n in names], *[outs_d[n] for n in names], *[outs_m[n] for n in names],
            *[outs_v[n] for n in names])
```
